```python
import jax, jax.numpy as jnp
from jax import lax
import numpy as np

D_MODEL = 1024
BATCH = 8
SEQ = 16384
DEPTH = 2

HG_HEADS = 8
HG_KEY_DIM = 128
HG_VAL_DIM = D_MODEL // HG_HEADS
HG_KEY = HG_HEADS * HG_KEY_DIM
HG_VAL = HG_HEADS * HG_VAL_DIM
HG_CHUNK = 64
SG_GROUPS = 8
SG_GROUP_DIM = 64
SG_WIDTH = SG_GROUPS * SG_GROUP_DIM
SG_CHUNK = 128
FFN_HIDDEN = ((8 * D_MODEL // 3 + 255) // 256) * 256
PLE_DIM = 256
EPS = 1e-6
IN_SPLITS = (HG_KEY, HG_KEY, HG_KEY, HG_VAL, HG_VAL, SG_WIDTH, SG_WIDTH, D_MODEL, D_MODEL)
N_IN = HG_KEY * 3 + HG_VAL * 2 + SG_WIDTH * 2 + D_MODEL * 2

kernel_name = "hgrn2_gmlp_gated_hybrid_encoder"


def rms_norm(x, g):
    xf = x.astype(jnp.float32)
    y = xf * lax.rsqrt(jnp.mean(xf * xf, axis=-1, keepdims=True) + EPS)
    return (y * g.astype(jnp.float32)).astype(x.dtype)


def layer_norm(x, g, b):
    xf = x.astype(jnp.float32)
    mu = jnp.mean(xf, axis=-1, keepdims=True)
    xc = xf - mu
    y = xc * lax.rsqrt(jnp.mean(xc * xc, axis=-1, keepdims=True) + EPS)
    return (y * g.astype(jnp.float32) + b.astype(jnp.float32)).astype(x.dtype)


def layer_lower_bounds(gamma):
    sm = jax.nn.softmax(gamma.astype(jnp.float32), axis=0)
    return jnp.cumsum(sm, axis=0) - sm[0:1]


def _to_chunks(t):
    b, s, h, d = t.shape
    return t.reshape(b, s // HG_CHUNK, HG_CHUNK, h, d).transpose(1, 0, 3, 2, 4)


def hgrn2_direction(q, k, v, logf):
    bsz, s, h, dk = q.shape
    dv = v.shape[-1]
    mask = jnp.tril(jnp.ones((HG_CHUNK, HG_CHUNK), dtype=jnp.float32))

    def step(state, inp):
        qc, kc, vc, gc = inp
        b = jnp.cumsum(gc, axis=2)
        o_inter = jnp.einsum('bhtk,bhkv->bhtv', qc * jnp.exp(b), state)
        diff = b[:, :, :, None, :] - b[:, :, None, :, :]
        decay = jnp.exp(jnp.minimum(diff, 0.0)) * mask[:, :, None]
        scores = jnp.einsum('bhtk,bhsk,bhtsk->bhts', qc, kc, decay)
        o_intra = jnp.einsum('bhts,bhsv->bhtv', scores, vc)
        b_last = b[:, :, -1:, :]
        new_state = (jnp.exp(b_last[:, :, 0, :])[..., None] * state
                     + jnp.einsum('bhsk,bhsv->bhkv', kc * jnp.exp(b_last - b), vc))
        return new_state, o_inter + o_intra

    init = jnp.zeros((bsz, h, dk, dv), jnp.float32)
    _, o = lax.scan(step, init, (_to_chunks(q), _to_chunks(k), _to_chunks(v), _to_chunks(logf)))
    return o.transpose(1, 0, 3, 2, 4).reshape(bsz, s, h, dv)


def hgrn2_mixer(zq, zf_fwd, zf_bwd, zi, zg, lb_f, lb_b, norm_g):
    bsz, s, _ = zq.shape
    f32 = jnp.float32
    tiny = jnp.finfo(f32).tiny
    q = jax.nn.silu(zq.astype(f32)).reshape(bsz, s, HG_HEADS, HG_KEY_DIM)
    v = zi.astype(f32).reshape(bsz, s, HG_HEADS, HG_VAL_DIM)

    def gates(zf, lb):
        zf = zf.astype(f32)
        f = lb + (1.0 - lb) * jax.nn.sigmoid(zf)
        logf = jnp.log(jnp.maximum(f, tiny))
        k = (1.0 - lb) * jax.nn.sigmoid(-zf)
        return (k.reshape(bsz, s, HG_HEADS, HG_KEY_DIM), logf.reshape(bsz, s, HG_HEADS, HG_KEY_DIM))

    k_f, logf_f = gates(zf_fwd, lb_f)
    k_b, logf_b = gates(zf_bwd, lb_b)
    o_fwd = hgrn2_direction(q, k_f, v, logf_f)
    o_bwd = hgrn2_direction(q[:, ::-1], k_b[:, ::-1], v[:, ::-1], logf_b[:, ::-1])[:, ::-1]
    o = o_fwd + o_bwd
    o = rms_norm(o, norm_g.reshape(HG_HEADS, HG_VAL_DIM)).reshape(bsz, s, HG_VAL)
    return (o * jax.nn.silu(zg.astype(f32))).astype(zq.dtype)


def spatial_gating(zu, zv, w_s, b_s, ln_g, ln_b):
    bsz, s, _ = zu.shape
    u = jax.nn.gelu(zu, approximate=False)
    v = layer_norm(jax.nn.gelu(zv, approximate=False), ln_g, ln_b)
    vr = v.reshape(bsz, s // SG_CHUNK, SG_CHUNK, SG_GROUPS, SG_GROUP_DIM)
    sg = jnp.einsum('gts,bcsge->bctge', w_s, vr) + b_s.T[None, None, :, :, None]
    return u * sg.reshape(bsz, s, SG_WIDTH)


def _fwd_setup_inputs(seed: int = 0) -> dict:
    key = jax.random.key(seed)
    ks = jax.random.split(key, 24)
    f32 = jnp.float32

    def nrm(k, shape, scale):
        return jax.random.normal(k, shape, f32) * scale

    def gain(k, shape):
        return 1.0 + 0.05 * jax.random.normal(k, shape, f32)

    return {
        "x": nrm(ks[0], (BATCH, SEQ, D_MODEL), 1.0),
        "p": nrm(ks[1], (DEPTH, BATCH, SEQ, PLE_DIM), 1.0),
        "norm_mix_pre": gain(ks[2], (DEPTH, D_MODEL)),
        "w_in": nrm(ks[3], (DEPTH, D_MODEL, N_IN), D_MODEL ** -0.5),
        "lb_gamma_fwd": nrm(ks[4], (DEPTH, HG_KEY), 0.1),
        "lb_gamma_bwd": nrm(ks[5], (DEPTH, HG_KEY), 0.1),
        "hg_norm": gain(ks[6], (DEPTH, HG_VAL)),
        "sg_w": nrm(ks[7], (DEPTH, SG_GROUPS, SG_CHUNK, SG_CHUNK), SG_CHUNK ** -0.5),
        "sg_b": gain(ks[8], (DEPTH, SG_GROUPS, SG_CHUNK)),
        "sg_ln_g": gain(ks[9], (DEPTH, SG_WIDTH)),
        "sg_ln_b": nrm(ks[10], (DEPTH, SG_WIDTH), 0.02),
        "w_a": nrm(ks[11], (DEPTH, HG_VAL, D_MODEL), HG_VAL ** -0.5),
        "w_b": nrm(ks[12], (DEPTH, SG_WIDTH, D_MODEL), SG_WIDTH ** -0.5),
        "w_out": nrm(ks[13], (DEPTH, D_MODEL, D_MODEL), D_MODEL ** -0.5),
        "norm_mix_post": gain(ks[14], (DEPTH, D_MODEL)),
        "norm_ffn_pre": gain(ks[15], (DEPTH, D_MODEL)),
        "w_gate": nrm(ks[16], (DEPTH, D_MODEL, FFN_HIDDEN), D_MODEL ** -0.5),
        "w_up": nrm(ks[17], (DEPTH, D_MODEL, FFN_HIDDEN), D_MODEL ** -0.5),
        "w_down": nrm(ks[18], (DEPTH, FFN_HIDDEN, D_MODEL), FFN_HIDDEN ** -0.5),
        "norm_ffn_post": gain(ks[19], (DEPTH, D_MODEL)),
        "w_ple": nrm(ks[20], (DEPTH, PLE_DIM, D_MODEL), PLE_DIM ** -0.5),
        "w_ple_gate": nrm(ks[21], (DEPTH, D_MODEL, D_MODEL), D_MODEL ** -0.5),
    }


def _fwd_reference(x, p, norm_mix_pre, w_in, lb_gamma_fwd, lb_gamma_bwd, hg_norm, sg_w, sg_b,
              sg_ln_g, sg_ln_b, w_a, w_b, w_out, norm_mix_post, norm_ffn_pre, w_gate, w_up,
              w_down, norm_ffn_post, w_ple, w_ple_gate):
    lb_fwd_all = layer_lower_bounds(lb_gamma_fwd)
    lb_bwd_all = layer_lower_bounds(lb_gamma_bwd)
    offsets = [int(o) for o in np.cumsum(IN_SPLITS)[:-1]]
    for l in range(DEPTH):
        h = rms_norm(x, norm_mix_pre[l])
        z = jnp.einsum('bsd,dn->bsn', h, w_in[l])
        zq, zf_f, zf_b, zi, zg, zu, zv, ga, gb = jnp.split(z, offsets, axis=-1)
        a_out = hgrn2_mixer(zq, zf_f, zf_b, zi, zg, lb_fwd_all[l], lb_bwd_all[l], hg_norm[l])
        b_out = spatial_gating(zu, zv, sg_w[l], sg_b[l], sg_ln_g[l], sg_ln_b[l])
        merged = (jax.nn.sigmoid(ga) * jnp.einsum('bsv,vd->bsd', a_out, w_a[l])
                  + jax.nn.sigmoid(gb) * jnp.einsum('bsw,wd->bsd', b_out, w_b[l]))
        mix = jnp.einsum('bsd,de->bse', merged, w_out[l])
        x = x + rms_norm(mix, norm_mix_post[l])
        h2 = rms_norm(x, norm_ffn_pre[l])
        ff = jnp.einsum('bsf,fd->bsd',
                        jax.nn.silu(jnp.einsum('bsd,df->bsf', h2, w_gate[l]))
                        * jnp.einsum('bsd,df->bsf', h2, w_up[l]), w_down[l])
        x = x + rms_norm(ff, norm_ffn_post[l])
        x = x + (jnp.einsum('bse,ed->bsd', p[l], w_ple[l])
                 * jax.nn.sigmoid(jnp.einsum('bsd,de->bse', x, w_ple_gate[l])))
    return x


import jax as _jax
import jax.numpy as _jnp

TWIN_FORMAT = 'train_step'
FWD_PARAMS = ['x', 'p', 'norm_mix_pre', 'w_in', 'lb_gamma_fwd', 'lb_gamma_bwd', 'hg_norm', 'sg_w', 'sg_b', 'sg_ln_g', 'sg_ln_b', 'w_a', 'w_b', 'w_out', 'norm_mix_post', 'norm_ffn_pre', 'w_gate', 'w_up', 'w_down', 'norm_ffn_post', 'w_ple', 'w_ple_gate']
TWIN_WEIGHTS = ['norm_mix_pre', 'w_in', 'lb_gamma_fwd', 'lb_gamma_bwd', 'hg_norm', 'sg_w', 'sg_b', 'sg_ln_g', 'sg_ln_b', 'w_a', 'w_b', 'w_out', 'norm_mix_post', 'norm_ffn_pre', 'w_gate', 'w_up', 'w_down', 'norm_ffn_post', 'w_ple', 'w_ple_gate']
TWIN_DIFF_INPUT = 'x'
TWIN_INPUTS = ['x', 'p', 'norm_mix_pre', 'w_in', 'lb_gamma_fwd', 'lb_gamma_bwd', 'hg_norm', 'sg_w', 'sg_b', 'sg_ln_g', 'sg_ln_b', 'w_a', 'w_b', 'w_out', 'norm_mix_post', 'norm_ffn_pre', 'w_gate', 'w_up', 'w_down', 'norm_ffn_post', 'w_ple', 'w_ple_gate', 'loss_target', 'm_norm_mix_pre', 'm_w_in', 'm_lb_gamma_fwd', 'm_lb_gamma_bwd', 'm_hg_norm', 'm_sg_w', 'm_sg_b', 'm_sg_ln_g', 'm_sg_ln_b', 'm_w_a', 'm_w_b', 'm_w_out', 'm_norm_mix_post', 'm_norm_ffn_pre', 'm_w_gate', 'm_w_up', 'm_w_down', 'm_norm_ffn_post', 'm_w_ple', 'm_w_ple_gate', 'v_norm_mix_pre', 'v_w_in', 'v_lb_gamma_fwd', 'v_lb_gamma_bwd', 'v_hg_norm', 'v_sg_w', 'v_sg_b', 'v_sg_ln_g', 'v_sg_ln_b', 'v_w_a', 'v_w_b', 'v_w_out', 'v_norm_mix_post', 'v_norm_ffn_pre', 'v_w_gate', 'v_w_up', 'v_w_down', 'v_norm_ffn_post', 'v_w_ple', 'v_w_ple_gate']
TWIN_OUTPUTS = ['loss', 'grad_x', 'grad_norm_mix_pre', 'grad_w_in', 'grad_lb_gamma_fwd', 'grad_lb_gamma_bwd', 'grad_hg_norm', 'grad_sg_w', 'grad_sg_b', 'grad_sg_ln_g', 'grad_sg_ln_b', 'grad_w_a', 'grad_w_b', 'grad_w_out', 'grad_norm_mix_post', 'grad_norm_ffn_pre', 'grad_w_gate', 'grad_w_up', 'grad_w_down', 'grad_norm_ffn_post', 'grad_w_ple', 'grad_w_ple_gate', 'delta_norm_mix_pre', 'delta_w_in', 'delta_lb_gamma_fwd', 'delta_lb_gamma_bwd', 'delta_hg_norm', 'delta_sg_w', 'delta_sg_b', 'delta_sg_ln_g', 'delta_sg_ln_b', 'delta_w_a', 'delta_w_b', 'delta_w_out', 'delta_norm_mix_post', 'delta_norm_ffn_pre', 'delta_w_gate', 'delta_w_up', 'delta_w_down', 'delta_norm_ffn_post', 'delta_w_ple', 'delta_w_ple_gate', 'new_m_norm_mix_pre', 'new_m_w_in', 'new_m_lb_gamma_fwd', 'new_m_lb_gamma_bwd', 'new_m_hg_norm', 'new_m_sg_w', 'new_m_sg_b', 'new_m_sg_ln_g', 'new_m_sg_ln_b', 'new_m_w_a', 'new_m_w_b', 'new_m_w_out', 'new_m_norm_mix_post', 'new_m_norm_ffn_pre', 'new_m_w_gate', 'new_m_w_up', 'new_m_w_down', 'new_m_norm_ffn_post', 'new_m_w_ple', 'new_m_w_ple_gate', 'new_v_norm_mix_pre', 'new_v_w_in', 'new_v_lb_gamma_fwd', 'new_v_lb_gamma_bwd', 'new_v_hg_norm', 'new_v_sg_w', 'new_v_sg_b', 'new_v_sg_ln_g', 'new_v_sg_ln_b', 'new_v_w_a', 'new_v_w_b', 'new_v_w_out', 'new_v_norm_mix_post', 'new_v_norm_ffn_pre', 'new_v_w_gate', 'new_v_w_up', 'new_v_w_down', 'new_v_norm_ffn_post', 'new_v_w_ple', 'new_v_w_ple_gate']
TWIN_LEAF_KINDS = {'loss': 'loss', 'grad_x': 'grad_x', 'grad_norm_mix_pre': 'grad_w', 'grad_w_in': 'grad_w', 'grad_lb_gamma_fwd': 'grad_w', 'grad_lb_gamma_bwd': 'grad_w', 'grad_hg_norm': 'grad_w', 'grad_sg_w': 'grad_w', 'grad_sg_b': 'grad_w', 'grad_sg_ln_g': 'grad_w', 'grad_sg_ln_b': 'grad_w', 'grad_w_a': 'grad_w', 'grad_w_b': 'grad_w', 'grad_w_out': 'grad_w', 'grad_norm_mix_post': 'grad_w', 'grad_norm_ffn_pre': 'grad_w', 'grad_w_gate': 'grad_w', 'grad_w_up': 'grad_w', 'grad_w_down': 'grad_w', 'grad_norm_ffn_post': 'grad_w', 'grad_w_ple': 'grad_w', 'grad_w_ple_gate': 'grad_w', 'delta_norm_mix_pre': 'delta_w', 'delta_w_in': 'delta_w', 'delta_lb_gamma_fwd': 'delta_w', 'delta_lb_gamma_bwd': 'delta_w', 'delta_hg_norm': 'delta_w', 'delta_sg_w': 'delta_w', 'delta_sg_b': 'delta_w', 'delta_sg_ln_g': 'delta_w', 'delta_sg_ln_b': 'delta_w', 'delta_w_a': 'delta_w', 'delta_w_b': 'delta_w', 'delta_w_out': 'delta_w', 'delta_norm_mix_post': 'delta_w', 'delta_norm_ffn_pre': 'delta_w', 'delta_w_gate': 'delta_w', 'delta_w_up': 'delta_w', 'delta_w_down': 'delta_w', 'delta_norm_ffn_post': 'delta_w', 'delta_w_ple': 'delta_w', 'delta_w_ple_gate': 'delta_w', 'new_m_norm_mix_pre': 'new_m', 'new_m_w_in': 'new_m', 'new_m_lb_gamma_fwd': 'new_m', 'new_m_lb_gamma_bwd': 'new_m', 'new_m_hg_norm': 'new_m', 'new_m_sg_w': 'new_m', 'new_m_sg_b': 'new_m', 'new_m_sg_ln_g': 'new_m', 'new_m_sg_ln_b': 'new_m', 'new_m_w_a': 'new_m', 'new_m_w_b': 'new_m', 'new_m_w_out': 'new_m', 'new_m_norm_mix_post': 'new_m', 'new_m_norm_ffn_pre': 'new_m', 'new_m_w_gate': 'new_m', 'new_m_w_up': 'new_m', 'new_m_w_down': 'new_m', 'new_m_norm_ffn_post': 'new_m', 'new_m_w_ple': 'new_m', 'new_m_w_ple_gate': 'new_m', 'new_v_norm_mix_pre': 'new_v', 'new_v_w_in': 'new_v', 'new_v_lb_gamma_fwd': 'new_v', 'new_v_lb_gamma_bwd': 'new_v', 'new_v_hg_norm': 'new_v', 'new_v_sg_w': 'new_v', 'new_v_sg_b': 'new_v', 'new_v_sg_ln_g': 'new_v', 'new_v_sg_ln_b': 'new_v', 'new_v_w_a': 'new_v', 'new_v_w_b': 'new_v', 'new_v_w_out': 'new_v', 'new_v_norm_mix_post': 'new_v', 'new_v_norm_ffn_pre': 'new_v', 'new_v_w_gate': 'new_v', 'new_v_w_up': 'new_v', 'new_v_w_down': 'new_v', 'new_v_norm_ffn_post': 'new_v', 'new_v_w_ple': 'new_v', 'new_v_w_ple_gate': 'new_v'}


def _forward(args):
    return _fwd_reference(*[args[k] for k in FWD_PARAMS])


def _output_shape():
    def fwd():
        inp = _fwd_setup_inputs(0)
        return _fwd_reference(*[inp[k] for k in FWD_PARAMS])
    out = _jax.eval_shape(fwd)
    return out.shape, out.dtype

N_MICROBATCH = 1
ADAM_LR = 0.001
ADAM_B1 = 0.9
ADAM_B2 = 0.999
ADAM_EPS = 1e-08
ADAM_WD = 0.01
ADAM_STEP = 10
PER_EXAMPLE_BATCH_AXIS = {'x': 0, 'p': 1, 'loss_target': 0}
SHARED_INPUTS = []
_WEIGHT_DTYPES = {'norm_mix_pre': _jnp.float32, 'w_in': _jnp.float32, 'lb_gamma_fwd': _jnp.float32, 'lb_gamma_bwd': _jnp.float32, 'hg_norm': _jnp.float32, 'sg_w': _jnp.float32, 'sg_b': _jnp.float32, 'sg_ln_g': _jnp.float32, 'sg_ln_b': _jnp.float32, 'w_a': _jnp.float32, 'w_b': _jnp.float32, 'w_out': _jnp.float32, 'norm_mix_post': _jnp.float32, 'norm_ffn_pre': _jnp.float32, 'w_gate': _jnp.float32, 'w_up': _jnp.float32, 'w_down': _jnp.float32, 'norm_ffn_post': _jnp.float32, 'w_ple': _jnp.float32, 'w_ple_gate': _jnp.float32}
MOMENT_SCALE = {'norm_mix_pre': 2.844119e+00, 'w_in': 1.041526e+00, 'lb_gamma_fwd': 3.515823e-02, 'lb_gamma_bwd': 3.588479e-02, 'hg_norm': 1.434172e+00, 'sg_w': 9.228203e-01, 'sg_b': 9.256312e-01, 'sg_ln_g': 1.659751e+00, 'sg_ln_b': 1.441772e+00, 'w_a': 1.414854e+00, 'w_b': 7.761961e+00, 'w_out': 7.697429e+00, 'norm_mix_post': 1.314702e+02, 'norm_ffn_pre': 3.331924e+00, 'w_gate': 1.037850e+00, 'w_up': 1.821728e+00, 'w_down': 3.062937e+00, 'norm_ffn_post': 1.301684e+02, 'w_ple': 1.665300e+00, 'w_ple_gate': 2.072780e+00}


def _to_microbatches(a, axis):
    t = _jnp.moveaxis(a, axis, 0)
    t = t.reshape((N_MICROBATCH, t.shape[0] // N_MICROBATCH) + t.shape[1:])
    return _jnp.moveaxis(t, 1, axis + 1)


def setup_inputs(seed: int = 0) -> dict:
    inp = _fwd_setup_inputs(seed)
    key = _jax.random.fold_in(_jax.random.key(seed), 7919)
    shape, _ = _output_shape()
    out = dict(inp)
    out["loss_target"] = _jax.random.normal(_jax.random.fold_in(key, 0), shape, _jnp.float32)
    for i, name in enumerate(TWIN_WEIGHTS):
        w = inp[name].astype(_jnp.float32)
        if MOMENT_SCALE is None:
            s = _jnp.sqrt(_jnp.mean(_jnp.square(w)) + 1e-30)
        else:
            s = MOMENT_SCALE[name]
        km, kv = _jax.random.split(_jax.random.fold_in(key, i + 1))
        out[name] = w
        out["m_" + name] = s * _jax.random.normal(km, w.shape, _jnp.float32)
        out["v_" + name] = (s * s) * _jax.random.uniform(kv, w.shape, _jnp.float32, 0.5, 1.5)
    if N_MICROBATCH > 1:
        for name, axis in PER_EXAMPLE_BATCH_AXIS.items():
            out[name] = _to_microbatches(out[name], axis)
    return {'x': out['x'], 'p': out['p'], 'norm_mix_pre': out['norm_mix_pre'], 'w_in': out['w_in'], 'lb_gamma_fwd': out['lb_gamma_fwd'], 'lb_gamma_bwd': out['lb_gamma_bwd'], 'hg_norm': out['hg_norm'], 'sg_w': out['sg_w'], 'sg_b': out['sg_b'], 'sg_ln_g': out['sg_ln_g'], 'sg_ln_b': out['sg_ln_b'], 'w_a': out['w_a'], 'w_b': out['w_b'], 'w_out': out['w_out'], 'norm_mix_post': out['norm_mix_post'], 'norm_ffn_pre': out['norm_ffn_pre'], 'w_gate': out['w_gate'], 'w_up': out['w_up'], 'w_down': out['w_down'], 'norm_ffn_post': out['norm_ffn_post'], 'w_ple': out['w_ple'], 'w_ple_gate': out['w_ple_gate'], 'loss_target': out['loss_target'], 'm_norm_mix_pre': out['m_norm_mix_pre'], 'm_w_in': out['m_w_in'], 'm_lb_gamma_fwd': out['m_lb_gamma_fwd'], 'm_lb_gamma_bwd': out['m_lb_gamma_bwd'], 'm_hg_norm': out['m_hg_norm'], 'm_sg_w': out['m_sg_w'], 'm_sg_b': out['m_sg_b'], 'm_sg_ln_g': out['m_sg_ln_g'], 'm_sg_ln_b': out['m_sg_ln_b'], 'm_w_a': out['m_w_a'], 'm_w_b': out['m_w_b'], 'm_w_out': out['m_w_out'], 'm_norm_mix_post': out['m_norm_mix_post'], 'm_norm_ffn_pre': out['m_norm_ffn_pre'], 'm_w_gate': out['m_w_gate'], 'm_w_up': out['m_w_up'], 'm_w_down': out['m_w_down'], 'm_norm_ffn_post': out['m_norm_ffn_post'], 'm_w_ple': out['m_w_ple'], 'm_w_ple_gate': out['m_w_ple_gate'], 'v_norm_mix_pre': out['v_norm_mix_pre'], 'v_w_in': out['v_w_in'], 'v_lb_gamma_fwd': out['v_lb_gamma_fwd'], 'v_lb_gamma_bwd': out['v_lb_gamma_bwd'], 'v_hg_norm': out['v_hg_norm'], 'v_sg_w': out['v_sg_w'], 'v_sg_b': out['v_sg_b'], 'v_sg_ln_g': out['v_sg_ln_g'], 'v_sg_ln_b': out['v_sg_ln_b'], 'v_w_a': out['v_w_a'], 'v_w_b': out['v_w_b'], 'v_w_out': out['v_w_out'], 'v_norm_mix_post': out['v_norm_mix_post'], 'v_norm_ffn_pre': out['v_norm_ffn_pre'], 'v_w_gate': out['v_w_gate'], 'v_w_up': out['v_w_up'], 'v_w_down': out['v_w_down'], 'v_norm_ffn_post': out['v_norm_ffn_post'], 'v_w_ple': out['v_w_ple'], 'v_w_ple_gate': out['v_w_ple_gate']}


def _loss(weights, diff, rest, loss_target):
    with _jax.named_scope("forward"):
        args = {**rest, TWIN_DIFF_INPUT: diff, **{k: w.astype(_WEIGHT_DTYPES[k]) for k, w in weights.items()}}
        y = _forward(args)
    with _jax.named_scope("loss_head"):
        err = _jnp.square(y.astype(_jnp.float32) - loss_target)
        return 0.5 * _jnp.sum(_jnp.mean(err, axis=-1)) if err.ndim else 0.5 * err


def _adamw(w, g, m, v):
    m = ADAM_B1 * m + (1.0 - ADAM_B1) * g
    v = ADAM_B2 * v + (1.0 - ADAM_B2) * _jnp.square(g)
    m_hat = m / (1.0 - ADAM_B1 ** ADAM_STEP)
    v_hat = v / (1.0 - ADAM_B2 ** ADAM_STEP)
    delta = -ADAM_LR * (m_hat / (_jnp.sqrt(v_hat) + ADAM_EPS) + ADAM_WD * w)
    return delta, m, v


def reference(x, p, norm_mix_pre, w_in, lb_gamma_fwd, lb_gamma_bwd, hg_norm, sg_w, sg_b, sg_ln_g, sg_ln_b, w_a, w_b, w_out, norm_mix_post, norm_ffn_pre, w_gate, w_up, w_down, norm_ffn_post, w_ple, w_ple_gate, loss_target, m_norm_mix_pre, m_w_in, m_lb_gamma_fwd, m_lb_gamma_bwd, m_hg_norm, m_sg_w, m_sg_b, m_sg_ln_g, m_sg_ln_b, m_w_a, m_w_b, m_w_out, m_norm_mix_post, m_norm_ffn_pre, m_w_gate, m_w_up, m_w_down, m_norm_ffn_post, m_w_ple, m_w_ple_gate, v_norm_mix_pre, v_w_in, v_lb_gamma_fwd, v_lb_gamma_bwd, v_hg_norm, v_sg_w, v_sg_b, v_sg_ln_g, v_sg_ln_b, v_w_a, v_w_b, v_w_out, v_norm_mix_post, v_norm_ffn_pre, v_w_gate, v_w_up, v_w_down, v_norm_ffn_post, v_w_ple, v_w_ple_gate):
    given = dict(x=x, p=p, norm_mix_pre=norm_mix_pre, w_in=w_in, lb_gamma_fwd=lb_gamma_fwd, lb_gamma_bwd=lb_gamma_bwd, hg_norm=hg_norm, sg_w=sg_w, sg_b=sg_b, sg_ln_g=sg_ln_g, sg_ln_b=sg_ln_b, w_a=w_a, w_b=w_b, w_out=w_out, norm_mix_post=norm_mix_post, norm_ffn_pre=norm_ffn_pre, w_gate=w_gate, w_up=w_up, w_down=w_down, norm_ffn_post=norm_ffn_post, w_ple=w_ple, w_ple_gate=w_ple_gate, loss_target=loss_target, m_norm_mix_pre=m_norm_mix_pre, m_w_in=m_w_in, m_lb_gamma_fwd=m_lb_gamma_fwd, m_lb_gamma_bwd=m_lb_gamma_bwd, m_hg_norm=m_hg_norm, m_sg_w=m_sg_w, m_sg_b=m_sg_b, m_sg_ln_g=m_sg_ln_g, m_sg_ln_b=m_sg_ln_b, m_w_a=m_w_a, m_w_b=m_w_b, m_w_out=m_w_out, m_norm_mix_post=m_norm_mix_post, m_norm_ffn_pre=m_norm_ffn_pre, m_w_gate=m_w_gate, m_w_up=m_w_up, m_w_down=m_w_down, m_norm_ffn_post=m_norm_ffn_post, m_w_ple=m_w_ple, m_w_ple_gate=m_w_ple_gate, v_norm_mix_pre=v_norm_mix_pre, v_w_in=v_w_in, v_lb_gamma_fwd=v_lb_gamma_fwd, v_lb_gamma_bwd=v_lb_gamma_bwd, v_hg_norm=v_hg_norm, v_sg_w=v_sg_w, v_sg_b=v_sg_b, v_sg_ln_g=v_sg_ln_g, v_sg_ln_b=v_sg_ln_b, v_w_a=v_w_a, v_w_b=v_w_b, v_w_out=v_w_out, v_norm_mix_post=v_norm_mix_post, v_norm_ffn_pre=v_norm_ffn_pre, v_w_gate=v_w_gate, v_w_up=v_w_up, v_w_down=v_w_down, v_norm_ffn_post=v_norm_ffn_post, v_w_ple=v_w_ple, v_w_ple_gate=v_w_ple_gate)
    weights = {n: given[n] for n in TWIN_WEIGHTS}
    shared = {n: given[n] for n in SHARED_INPUTS}
    per_example = {n: given[n] for n in ['x', 'p']}
    grad_fn = _jax.value_and_grad(_loss, argnums=(0, 1))

    def one_microbatch(ex, loss_target):
        ex = dict(ex)
        diff = ex.pop(TWIN_DIFF_INPUT)
        return grad_fn(weights, diff, {**shared, **ex}, loss_target)

    if N_MICROBATCH == 1:
        loss, (grad_w, grad_x) = one_microbatch(per_example, given["loss_target"])
    else:
        def body(carry, xs):
            loss_sum, grad_sum = carry
            l_k, (gw_k, gx_k) = one_microbatch(xs[0], xs[1])
            with _jax.named_scope("update"):
                return (loss_sum + l_k, _jax.tree.map(_jnp.add, grad_sum, gw_k)), gx_k

        init = (_jnp.zeros((), _jnp.float32), _jax.tree.map(_jnp.zeros_like, weights))
        (loss, grad_w), grad_x = _jax.lax.scan(body, init, (per_example, given["loss_target"]))
    with _jax.named_scope("update"):
        delta_w, new_m, new_v = {}, {}, {}
        for n in TWIN_WEIGHTS:
            delta_w[n], new_m[n], new_v[n] = _adamw(weights[n], grad_w[n], given["m_" + n], given["v_" + n])
    return (loss, grad_x, *[grad_w[n] for n in TWIN_WEIGHTS], *[delta_w[n] for n in TWIN_WEIGHTS],
            *[new_m[n] for n in TWIN_WEIGHTS], *[new_v[n] for n in TWIN_WEIGHTS])
```

```python
import jax
import jax.numpy as jnp
from jax import lax
from jax.experimental import pallas as pl
from jax.experimental.pallas import tpu as pltpu

F32 = jnp.float32
BF16 = jnp.bfloat16

D = 1024
N_IN = 8192
HEADS = 8
HEAD_DIM = 128
SG_CHUNK = 128
SG_WIDTH = 512
SG_GROUP_DIM = 64
FFN = 2816
PLE_DIM = 256
EPS = 1e-6
DEPTH = 2
ZQ, ZFF, ZFB, ZI, ZG, GA, GB = 0, 1, 2, 3, 4, 6, 7
ZU, ZV = 10, 11

NDEV = 8
FFN_SHARD = FFN // NDEV
FFN_SHARD_PAD = 384
FFN_PAD = NDEV * FFN_SHARD_PAD

LR, B1, B2, AEPS, WD, STEP = 0.001, 0.9, 0.999, 1e-08, 0.01, 10

ROW_TILE = 256
HG_CHUNK = 64
HG_BLOCK_FWD = 256
HG_BLOCK_BWD = 128
EXP_CLAMP = 80.0
TINY = float(jnp.finfo(jnp.float32).tiny)
VMEM_LIMIT = 56 * 1024 * 1024

BIG = ["w_in", "w_a", "w_b", "w_out", "w_gate", "w_up", "w_down", "w_ple", "w_ple_gate"]
SMALL = ["norm_mix_pre", "lb_gamma_fwd", "lb_gamma_bwd", "hg_norm", "sg_w", "sg_b", "sg_ln_g", "sg_ln_b",
         "norm_mix_post", "norm_ffn_pre", "norm_ffn_post"]
WEIGHTS = ["norm_mix_pre", "w_in", "lb_gamma_fwd", "lb_gamma_bwd", "hg_norm", "sg_w", "sg_b", "sg_ln_g", "sg_ln_b",
           "w_a", "w_b", "w_out", "norm_mix_post", "norm_ffn_pre", "w_gate", "w_up", "w_down", "norm_ffn_post",
           "w_ple", "w_ple_gate"]
INPUTS = (["x", "p"] + WEIGHTS + ["loss_target"] + ["m_" + n for n in WEIGHTS] + ["v_" + n for n in WEIGHTS])


def _params(sem):
    return pltpu.CompilerParams(dimension_semantics=sem, vmem_limit_bytes=VMEM_LIMIT)


def _dot(a, b):
    return lax.dot_general(a, b, (((1,), (0,)), ((), ())), preferred_element_type=F32)


def _dot_nt(a, b):
    return lax.dot_general(a, b, (((1,), (1,)), ((), ())), preferred_element_type=F32)


def _dot_tn(a, b):
    return lax.dot_general(a, b, (((0,), (0,)), ((), ())), preferred_element_type=F32)


def _sigmoid(x):
    return jax.nn.sigmoid(x)


def _silu(x):
    return x * _sigmoid(x)


def _silu_grad(x):
    s = _sigmoid(x)
    return s * (1.0 + x * (1.0 - s))


def _gelu(x):
    return 0.5 * x * (1.0 + lax.erf(x * 0.7071067811865476))


def _gelu_grad(x):
    return 0.5 * (1.0 + lax.erf(x * 0.7071067811865476)) + x * jnp.exp(-0.5 * x * x) * 0.3989422804014327


def _mean(x):
    return jnp.mean(x, axis=-1, keepdims=True)


def _colsum(x):
    return jnp.sum(x, axis=0, keepdims=True)


def _rms(x):
    r = lax.rsqrt(_mean(x * x) + EPS)
    return x * r, r


def _rms_bwd(dy, xh, r, g):
    dyg = dy * g
    return r * (dyg - xh * _mean(dyg * xh))


def rowwise(name, fn, m, ins=(), consts=(), outs=(), alias_outs=(), accs=(), tm=ROW_TILE):
    tm = min(tm, m)
    n_in, n_c, n_o, n_al, n_ac = len(ins), len(consts), len(outs), len(alias_outs), len(accs)
    held = [a for (a, _, _) in alias_outs if not isinstance(a, jax.ShapeDtypeStruct)]
    n_held = len(held)

    def body(*refs):
        in_refs = refs[:n_in + n_c]
        out_refs = refs[n_in + n_c + n_held:]
        vals = fn(*[r[...] for r in in_refs])
        if not isinstance(vals, (tuple, list)):
            vals = (vals,)
        for r, v in zip(out_refs[:n_o + n_al], vals[:n_o + n_al]):
            r[...] = v.astype(r.dtype)
        if n_ac:
            acc_refs = out_refs[n_o + n_al:]

            @pl.when(pl.program_id(0) == 0)
            def _():
                for r in acc_refs:
                    r[...] = jnp.zeros(r.shape, F32)

            for r, v in zip(acc_refs, vals[n_o + n_al:]):
                r[...] += v

    def col(cb):
        return lambda i: (i, cb)

    in_specs = [pl.BlockSpec((tm, w), col(cb)) for (_, w, cb) in ins]
    in_specs += [pl.BlockSpec(c.shape, lambda i, nd=c.ndim: (0,) * nd) for c in consts]
    in_specs += [pl.BlockSpec(memory_space=pl.ANY) for _ in held]
    out_shape = [jax.ShapeDtypeStruct((m, w), dt) for (w, dt) in outs]
    out_specs = [pl.BlockSpec((tm, w), col(0)) for (w, _) in outs]
    out_shape += [jax.ShapeDtypeStruct(a.shape, a.dtype) for (a, _, _) in alias_outs]
    out_specs += [pl.BlockSpec((tm, w), col(cb)) for (_, w, cb) in alias_outs]
    out_shape += [jax.ShapeDtypeStruct(s, F32) for s in accs]
    out_specs += [pl.BlockSpec(s, lambda i: (0, 0)) for s in accs]
    aliases, k_in = {}, n_in + n_c
    for k, (a, _, _) in enumerate(alias_outs):
        if not isinstance(a, jax.ShapeDtypeStruct):
            aliases[k_in] = n_o + k
            k_in += 1
    return pl.pallas_call(
        body, name=name, grid=(m // tm,), in_specs=in_specs, out_specs=out_specs, out_shape=out_shape,
        input_output_aliases=aliases,
        compiler_params=_params(("arbitrary",) if n_ac else ("parallel",)),
    )(*[a for (a, _, _) in ins], *consts, *held)


def _operand(arr, bshape, imap):
    if isinstance(arr, tuple):
        arr, lead = arr
        return arr, pl.BlockSpec((None,) + bshape, lambda *g: (lead,) + imap(*g))
    return arr, pl.BlockSpec(bshape, imap)


def _shape2(arr):
    return arr[0].shape[1:] if isinstance(arr, tuple) else arr.shape


def mm(name, a, b, mode, out_dtype=F32, out=None, tm=1024, tn=512, tk=1024):
    sa, sb = _shape2(a), _shape2(b)
    if mode == "nn":
        (M, K), N = sa, sb[1]
    elif mode == "nt":
        (M, K), N = sa, sb[0]
    else:
        (K, M), N = sa, sb[1]
    tm, tn, tk = min(tm, M), min(tn, N), min(tk, K)
    assert M % tm == 0 and N % tn == 0 and K % tk == 0, (name, M, N, K)
    nk = K // tk
    if mode == "nn":
        a_arr, a_spec = _operand(a, (tm, tk), lambda i, j, k: (i, k))
        b_arr, b_spec = _operand(b, (tk, tn), lambda i, j, k: (k, j))
        dot = _dot
    elif mode == "nt":
        a_arr, a_spec = _operand(a, (tm, tk), lambda i, j, k: (i, k))
        b_arr, b_spec = _operand(b, (tn, tk), lambda i, j, k: (j, k))
        dot = _dot_nt
    else:
        a_arr, a_spec = _operand(a, (tk, tm), lambda i, j, k: (k, i))
        b_arr, b_spec = _operand(b, (tk, tn), lambda i, j, k: (k, j))
        dot = _dot_tn

    def body(a_ref, b_ref, *rest):
        o_ref, acc_ref = rest[-2], rest[-1]
        k = pl.program_id(2)
        part = dot(a_ref[...].astype(BF16), b_ref[...].astype(BF16))

        @pl.when(k == 0)
        def _():
            acc_ref[...] = part

        @pl.when(k > 0)
        def _():
            acc_ref[...] += part

        @pl.when(k == nk - 1)
        def _():
            o_ref[...] = acc_ref[...].astype(o_ref.dtype)

    operands, in_specs, aliases = [a_arr, b_arr], [a_spec, b_spec], {}
    if out is None:
        out_shape = jax.ShapeDtypeStruct((M, N), out_dtype)
        out_spec = pl.BlockSpec((tm, tn), lambda i, j, k: (i, j))
    else:
        layers, layer, buf = out
        out_shape = jax.ShapeDtypeStruct((layers, M, N), out_dtype)
        out_spec = pl.BlockSpec((None, tm, tn), lambda i, j, k: (layer, i, j))
        if buf is not None:
            operands.append(buf)
            in_specs.append(pl.BlockSpec(memory_space=pl.ANY))
            aliases = {2: 0}
    return pl.pallas_call(
        body, name=name, grid=(M // tm, N // tn, nk), in_specs=in_specs, out_specs=out_spec, out_shape=out_shape,
        scratch_shapes=[pltpu.VMEM((tm, tn), F32)], input_output_aliases=aliases,
        compiler_params=_params(("parallel", "parallel", "arbitrary")),
    )(*operands)


def _cumsum_rows(x, reverse):
    n = x.shape[0]
    row = lax.broadcasted_iota(jnp.int32, x.shape, 0)
    s = 1
    while s < n:
        if reverse:
            x = x + jnp.where(row < n - s, pltpu.roll(x, n - s, 0), 0.0)
        else:
            x = x + jnp.where(row >= s, pltpu.roll(x, s, 0), 0.0)
        s *= 2
    return x


def _hg_prep(zq, zf, lb, reverse):
    n = zq.shape[0]
    q = _silu(zq)
    sig, sn = _sigmoid(zf), _sigmoid(-zf)
    f = lb + (1.0 - lb) * sig
    k = (1.0 - lb) * sn
    b = _cumsum_rows(jnp.log(jnp.maximum(f, TINY)), reverse)
    b_last = b[0:1] if reverse else b[n - 1:n]
    b_ref = b[n // 2:n // 2 + 1]
    e1 = jnp.exp(b)
    e2 = jnp.exp(jnp.clip(b - b_ref, -EXP_CLAMP, EXP_CLAMP))
    e3 = jnp.exp(jnp.clip(b_ref - b, -EXP_CLAMP, EXP_CLAMP))
    e4 = jnp.exp(b_last - b)
    return dict(q=q, k=k, sig=sig, sn=sn, f=f, e1=e1, e2=e2, e3=e3, e4=e4, e_last=jnp.exp(b_last),
                qe=(q * e1).astype(BF16), qt=(q * e2).astype(BF16), kt=(k * e3).astype(BF16),
                ks=(k * e4).astype(BF16))


def _hg_mask(n, reverse):
    t = lax.broadcasted_iota(jnp.int32, (n, n), 0)
    s = lax.broadcasted_iota(jnp.int32, (n, n), 1)
    return (s >= t) if reverse else (s <= t)


def hgrn_fwd(name, z, lb_f, lb_b):
    m = z.shape[0]
    C, T = HG_CHUNK, min(HG_BLOCK_FWD, m)
    nb, cpb = m // T, T // C

    def body(zq_f, zf_f, zi_f, zq_b, zf_b, zi_b, lbf_ref, lbb_ref, of_ref, ob_ref, sf_ref, sb_ref, st_ref):
        @pl.when(pl.program_id(0) == 0)
        def _():
            st_ref[...] = jnp.zeros(st_ref.shape, F32)

        dirs = ((zq_f, zf_f, zi_f, lbf_ref, of_ref, sf_ref), (zq_b, zf_b, zi_b, lbb_ref, ob_ref, sb_ref))

        def chunk(ci, carry):
            for d, (zq, zf, zi, lb_ref, o_ref, s_ref) in enumerate(dirs):
                cc = ci if d == 0 else cpb - 1 - ci
                rows = pl.ds(pl.multiple_of(cc * C, C), C)
                pre = _hg_prep(zq[rows, :], zf[rows, :], lb_ref[...], d == 1)
                v = zi[rows, :].astype(BF16)
                mask = _hg_mask(C, d == 1)
                for h in range(HEADS):
                    sl = slice(h * HEAD_DIM, (h + 1) * HEAD_DIM)
                    st = st_ref[d, h]
                    s_ref[cc, h] = st
                    a = jnp.where(mask, _dot_nt(pre["qt"][:, sl], pre["kt"][:, sl]), 0.0)
                    o_ref[rows, sl] = _dot_nt(pre["qe"][:, sl], st.astype(BF16)) + _dot(a.astype(BF16), v[:, sl])
                    st_ref[d, h] = st * pre["e_last"][:, sl] + _dot_tn(v[:, sl], pre["ks"][:, sl])
            return carry

        lax.fori_loop(0, cpb, chunk, 0)

    def zspec(cb, rev):
        return pl.BlockSpec((T, D), (lambda i: (nb - 1 - i, cb)) if rev else (lambda i: (i, cb)))

    def ospec(rev):
        return pl.BlockSpec((T, D), (lambda i: (nb - 1 - i, 0)) if rev else (lambda i: (i, 0)))

    def sspec(rev):
        shape = (cpb, HEADS, HEAD_DIM, HEAD_DIM)
        return pl.BlockSpec(shape, (lambda i: (nb - 1 - i, 0, 0, 0)) if rev else (lambda i: (i, 0, 0, 0)))

    lbspec = pl.BlockSpec((1, D), lambda i: (0, 0))
    states = jax.ShapeDtypeStruct((m // C, HEADS, HEAD_DIM, HEAD_DIM), F32)
    return pl.pallas_call(
        body, name=name, grid=(nb,),
        in_specs=[zspec(ZQ, False), zspec(ZFF, False), zspec(ZI, False), zspec(ZQ, True), zspec(ZFB, True),
                  zspec(ZI, True), lbspec, lbspec],
        out_specs=[ospec(False), ospec(True), sspec(False), sspec(True)],
        out_shape=[jax.ShapeDtypeStruct((m, D), F32), jax.ShapeDtypeStruct((m, D), F32), states, states],
        scratch_shapes=[pltpu.VMEM((2, HEADS, HEAD_DIM, HEAD_DIM), F32)],
        compiler_params=_params(("arbitrary",)),
    )(z, z, z, z, z, z, lb_f, lb_b)


def hgrn_bwd(name, z, d_o, s_f, s_b, lb_f, lb_b):
    m = z.shape[0]
    C, T = HG_CHUNK, min(HG_BLOCK_BWD, m)
    nb, cpb = m // T, T // C

    def body(zq_f, zf_f, zi_f, do_f, sf_ref, zq_b, zf_b, zi_b, do_b, sb_ref, lbf_ref, lbb_ref,
             dqf_ref, dvf_ref, dqb_ref, dvb_ref, dzf_f, dzf_b, dlbf_ref, dlbb_ref,
             dst_ref, dki_ref, dks_ref, rr_ref):
        @pl.when(pl.program_id(0) == 0)
        def _():
            dst_ref[...] = jnp.zeros(dst_ref.shape, F32)
            dlbf_ref[...] = jnp.zeros(dlbf_ref.shape, F32)
            dlbb_ref[...] = jnp.zeros(dlbb_ref.shape, F32)

        dirs = ((zq_f, zf_f, zi_f, do_f, sf_ref, lbf_ref, dqf_ref, dvf_ref, dzf_f, dlbf_ref),
                (zq_b, zf_b, zi_b, do_b, sb_ref, lbb_ref, dqb_ref, dvb_ref, dzf_b, dlbb_ref))

        def chunk(ci, carry):
            for d, (zq, zf, zi, do_ref, s_ref, lb_ref, dq_ref, dv_ref, dzf_ref, dlb_ref) in enumerate(dirs):
                rev = d == 1
                cc = cpb - 1 - ci if d == 0 else ci
                rows = pl.ds(pl.multiple_of(cc * C, C), C)
                lb = lb_ref[...]
                pre = _hg_prep(zq[rows, :], zf[rows, :], lb, rev)
                v = zi[rows, :].astype(BF16)
                do = do_ref[rows, :]
                mask = _hg_mask(C, rev)
                for h in range(HEADS):
                    sl = slice(h * HEAD_DIM, (h + 1) * HEAD_DIM)
                    st_prev = s_ref[cc, h]
                    dst = dst_ref[d, h]
                    dst16 = dst.astype(BF16)
                    qt, kt, ks, qe = pre["qt"][:, sl], pre["kt"][:, sl], pre["ks"][:, sl], pre["qe"][:, sl]
                    a = jnp.where(mask, _dot_nt(qt, kt), 0.0).astype(BF16)
                    da = jnp.where(mask, _dot_nt(do[:, sl], v[:, sl]), 0.0).astype(BF16)
                    dq_ref[rows, sl] = (_dot(da, kt) * pre["e2"][:, sl]
                                        + _dot(do[:, sl], st_prev.astype(BF16)) * pre["e1"][:, sl])
                    dki_ref[d, :, sl] = _dot_tn(da, qt) * pre["e3"][:, sl]
                    dks_ref[d, :, sl] = _dot(v[:, sl], dst16) * pre["e4"][:, sl]
                    dv_ref[rows, sl] = _dot_tn(a, do[:, sl]) + _dot_nt(ks, dst16)
                    rr_ref[d, :, sl] = pre["e_last"][:, sl] * _colsum(dst * st_prev)
                    dst_ref[d, h] = dst * pre["e_last"][:, sl] + _dot_tn(do[:, sl], qe)
                dki, dks = dki_ref[d], dks_ref[d]
                x = pre["q"] * dq_ref[rows, :] - pre["k"] * dki
                y = pre["k"] * dks
                dg = _cumsum_rows(x, not rev) + (_cumsum_rows(y, rev) - y) + rr_ref[d]
                inv_f = jnp.where(pre["f"] > TINY, 1.0 / pre["f"], 0.0)
                u = dg * inv_f - (dki + dks)
                dzf_ref[rows, :] = ((1.0 - lb) * pre["sig"] * pre["sn"] * u).astype(dzf_ref.dtype)
                dlb_ref[...] += _colsum(pre["sn"] * u)
            return carry

        lax.fori_loop(0, cpb, chunk, 0)

    def rspec(cb, rev):
        return pl.BlockSpec((T, D), (lambda i: (i, cb)) if rev else (lambda i: (nb - 1 - i, cb)))

    def sspec(rev):
        shape = (cpb, HEADS, HEAD_DIM, HEAD_DIM)
        return pl.BlockSpec(shape, (lambda i: (i, 0, 0, 0)) if rev else (lambda i: (nb - 1 - i, 0, 0, 0)))

    lbspec = pl.BlockSpec((1, D), lambda i: (0, 0))
    big = jax.ShapeDtypeStruct((m, D), F32)
    row = jax.ShapeDtypeStruct((1, D), F32)
    return pl.pallas_call(
        body, name=name, grid=(nb,),
        in_specs=[rspec(ZQ, False), rspec(ZFF, False), rspec(ZI, False), rspec(0, False), sspec(False),
                  rspec(ZQ, True), rspec(ZFB, True), rspec(ZI, True), rspec(0, True), sspec(True),
                  lbspec, lbspec],
        out_specs=[rspec(0, False), rspec(0, False), rspec(0, True), rspec(0, True),
                   rspec(0, False), rspec(0, True), lbspec, lbspec],
        out_shape=[big, big, big, big, jax.ShapeDtypeStruct((m, D), BF16), jax.ShapeDtypeStruct((m, D), BF16),
                   row, row],
        scratch_shapes=[pltpu.VMEM((2, HEADS, HEAD_DIM, HEAD_DIM), F32), pltpu.VMEM((2, C, D), F32),
                        pltpu.VMEM((2, C, D), F32), pltpu.VMEM((2, 1, D), F32)],
        compiler_params=_params(("arbitrary",)),
    )(z, z, z, d_o, s_f, z, z, z, d_o, s_b, lb_f, lb_b)


def _heads(fn, *arrs):
    res = [fn(*[a[:, h * HEAD_DIM:(h + 1) * HEAD_DIM] for a in arrs]) for h in range(HEADS)]
    return [jnp.concatenate(parts, axis=1) for parts in zip(*res)]


def _hg_post(o_f, o_b, zg, g):
    def head(of, ob, zgh, gh):
        on, _ = _rms(of + ob)
        return (on * gh * _silu(zgh),)
    return _heads(head, o_f, o_b, zg, g)[0]


def _hg_post_bwd(da, o_f, o_b, zg, g):
    def head(dah, of, ob, zgh, gh):
        on, r = _rms(of + ob)
        sg = _silu(zgh)
        d_on = dah * sg
        return _rms_bwd(d_on, on, r, gh), dah * on * gh * _silu_grad(zgh), d_on * on
    d_o, dzg, dg = _heads(head, da, o_f, o_b, zg, g)
    return d_o, dzg, _colsum(dg)


def _sg_parts(zv, ln_g, ln_b):
    vg = _gelu(zv)
    xc = vg - _mean(vg)
    rstd = lax.rsqrt(_mean(xc * xc) + EPS)
    vh = xc * rstd
    return vh, rstd, vh * ln_g + ln_b


def _sg_lane_group(shape):
    return lax.broadcasted_iota(jnp.int32, shape, 1) < SG_GROUP_DIM


def _sg_mix(w, v16, transpose):
    rows = v16.shape[0]
    out = []
    for c in range(rows // SG_CHUNK):
        parts = []
        for j in range(SG_WIDTH // 128):
            vj = v16[c * SG_CHUNK:(c + 1) * SG_CHUNK, j * 128:(j + 1) * 128]
            w0 = w[(2 * j) * SG_CHUNK:(2 * j + 1) * SG_CHUNK]
            w1 = w[(2 * j + 1) * SG_CHUNK:(2 * j + 2) * SG_CHUNK]
            dot = _dot_tn if transpose else _dot
            parts.append(jnp.where(_sg_lane_group((SG_CHUNK, 128)), dot(w0, vj), dot(w1, vj)))
        out.append(jnp.concatenate(parts, axis=1))
    return jnp.concatenate(out, axis=0)


def _sg_fwd(zu, zv, w, bias, ln_g, ln_b):
    _, _, v = _sg_parts(zv, ln_g, ln_b)
    reps = zu.shape[0] // SG_CHUNK
    return _gelu(zu) * (_sg_mix(w, v.astype(BF16), False) + jnp.concatenate([bias] * reps, axis=0))


def _sg_bwd(db, zu, zv, w, bias, ln_g, ln_b):
    vh, rstd, v = _sg_parts(zv, ln_g, ln_b)
    v16 = v.astype(BF16)
    reps = zu.shape[0] // SG_CHUNK
    sg = _sg_mix(w, v16, False) + jnp.concatenate([bias] * reps, axis=0)
    dzu = db * sg * _gelu_grad(zu)
    dsg = db * _gelu(zu)
    dsg16 = dsg.astype(BF16)
    dv = _sg_mix(w, dsg16, True)
    low = _sg_lane_group((SG_CHUNK, 128))
    dw = []
    for g in range(SG_WIDTH // SG_GROUP_DIM):
        j, keep = g // 2, (low if g % 2 == 0 else jnp.logical_not(low))
        acc = jnp.zeros((SG_CHUNK, SG_CHUNK), F32)
        for c in range(reps):
            rows = slice(c * SG_CHUNK, (c + 1) * SG_CHUNK)
            dj = jnp.where(keep, dsg16[rows, j * 128:(j + 1) * 128], jnp.zeros((), BF16))
            acc = acc + _dot_nt(dj, v16[rows, j * 128:(j + 1) * 128])
        dw.append(acc)
    dbias = sum(dsg[c * SG_CHUNK:(c + 1) * SG_CHUNK] for c in range(reps))
    dvh = dv * ln_g
    dvg = rstd * (dvh - _mean(dvh) - vh * _mean(dvh * vh))
    dzuv = jnp.concatenate([dzu, dvg * _gelu_grad(zv)], axis=1)
    return (dzuv, jnp.concatenate(dw, axis=0), dbias, _colsum(dv * vh), _colsum(dv))


def lower_bounds(name, gamma_f, gamma_b):
    def body(gf_ref, gb_ref, lf_ref, lb_ref):
        for g_ref, o_ref in ((gf_ref, lf_ref), (gb_ref, lb_ref)):
            g0, g1 = g_ref[0:1, :], g_ref[1:2, :]
            mx = jnp.maximum(g0, g1)
            e0, e1 = jnp.exp(g0 - mx), jnp.exp(g1 - mx)
            sm0, sm1 = e0 / (e0 + e1), e1 / (e0 + e1)
            o_ref[0:1, :] = sm0 - sm0
            o_ref[1:2, :] = (sm0 + sm1) - sm0
    shp = jax.ShapeDtypeStruct(gamma_f.shape, F32)
    return pl.pallas_call(body, name=name, out_shape=[shp, shp])(gamma_f, gamma_b)


def lower_bounds_bwd(name, gamma_f, gamma_b, dlb_f, dlb_b):
    def body(gf_ref, gb_ref, df_ref, db_ref, of_ref, ob_ref):
        for g_ref, d_ref, o_ref in ((gf_ref, df_ref, of_ref), (gb_ref, db_ref, ob_ref)):
            g0, g1 = g_ref[0:1, :], g_ref[1:2, :]
            mx = jnp.maximum(g0, g1)
            e0, e1 = jnp.exp(g0 - mx), jnp.exp(g1 - mx)
            sm0, sm1 = e0 / (e0 + e1), e1 / (e0 + e1)
            d1 = d_ref[1:2, :] * sm0 * sm1
            o_ref[0:1, :] = -d1
            o_ref[1:2, :] = d1
    shp = jax.ShapeDtypeStruct(gamma_f.shape, F32)
    return pl.pallas_call(body, name=name, out_shape=[shp, shp])(gamma_f, gamma_b, dlb_f, dlb_b)


def _row(a, l):
    return a[l:l + 1]


def local_step(x, p, target, W, S):
    m = x.shape[0]
    lb_f, lb_b = lower_bounds("lower_bounds", S["lb_gamma_fwd"], S["lb_gamma_bwd"])
    saved = []
    for l in range(DEPTH):
        t = f"l{l}_"
        g_pre, g_post = _row(S["norm_mix_pre"], l), _row(S["norm_mix_post"], l)
        g_fpre, g_fpost = _row(S["norm_ffn_pre"], l), _row(S["norm_ffn_post"], l)
        hg_g = _row(S["hg_norm"], l)
        sg_w = S["sg_w"][l].reshape(SG_WIDTH // SG_GROUP_DIM * SG_CHUNK, SG_CHUNK).astype(BF16)
        sg_bias = jnp.repeat(S["sg_b"][l].T, SG_GROUP_DIM, axis=1)
        ln_g, ln_b = _row(S["sg_ln_g"], l), _row(S["sg_ln_b"], l)
        lbf, lbb = _row(lb_f, l), _row(lb_b, l)

        (h,) = rowwise(t + "pre_norm", lambda xv, g: (_rms(xv)[0] * g,), m, ins=[(x, D, 0)], consts=[g_pre],
                       outs=[(D, BF16)])
        z = mm(t + "in_proj", h, (W["w_in"], l), "nn")
        o_f, o_b, s_f, s_b = hgrn_fwd(t + "hgrn_fwd", z, lbf, lbb)
        (a_out,) = rowwise(t + "hgrn_post", _hg_post, m, ins=[(o_f, D, 0), (o_b, D, 0), (z, D, ZG)], consts=[hg_g],
                           outs=[(D, BF16)])
        (b_out,) = rowwise(t + "sgu_fwd", _sg_fwd, m, ins=[(z, SG_WIDTH, ZU), (z, SG_WIDTH, ZV)],
                           consts=[sg_w, sg_bias, ln_g, ln_b], outs=[(SG_WIDTH, BF16)])
        pa = mm(t + "proj_a", a_out, (W["w_a"], l), "nn")
        pb = mm(t + "proj_b", b_out, (W["w_b"], l), "nn")
        (merged,) = rowwise(t + "merge", lambda a, b, ga, gb: (_sigmoid(ga) * a + _sigmoid(gb) * b,), m,
                            ins=[(pa, D, 0), (pb, D, 0), (z, D, GA), (z, D, GB)], outs=[(D, BF16)])
        mix = mm(t + "out_proj", merged, (W["w_out"], l), "nn")

        def post_pre(xv, mixv, gp, gf):
            x1 = xv + _rms(mixv)[0] * gp
            return x1, _rms(x1)[0] * gf
        x1, h2 = rowwise(t + "mix_post_ffn_pre", post_pre, m, ins=[(x, D, 0), (mix, D, 0)], consts=[g_post, g_fpre],
                         outs=[(D, F32), (D, BF16)])
        gu = mm(t + "ffn_in", h2, (W["w_gu"], l), "nn")
        (hid,) = rowwise(t + "ffn_act", lambda gt, up: (_silu(gt) * up,), m, ins=[(gu, FFN_PAD, 0), (gu, FFN_PAD, 1)],
                         outs=[(FFN_PAD, BF16)])
        ff = mm(t + "ffn_out", hid, (W["w_down"], l), "nn")
        (x2,) = rowwise(t + "ffn_post", lambda xv, f, g: (xv + _rms(f)[0] * g,), m, ins=[(x1, D, 0), (ff, D, 0)],
                        consts=[g_fpost], outs=[(D, F32)])
        e = mm(t + "ple_proj", (p, l), (W["w_ple"], l), "nn")
        tg = mm(t + "ple_gate", x2, (W["w_ple_gate"], l), "nn")
        (x3,) = rowwise(t + "ple_add", lambda xv, ev, tv: (xv + ev * _sigmoid(tv),), m,
                        ins=[(x2, D, 0), (e, D, 0), (tg, D, 0)], outs=[(D, F32)])
        saved.append(dict(x=x, h=h, z=z, o_f=o_f, o_b=o_b, s_f=s_f, s_b=s_b, a_out=a_out, b_out=b_out, pa=pa, pb=pb,
                          merged=merged, mix=mix, x1=x1, h2=h2, gu=gu, hid=hid, ff=ff, x2=x2, e=e, tg=tg,
                          sg_w=sg_w, sg_bias=sg_bias))
        x = x3

    def loss_fn(y, tv):
        err = y - tv
        return err * (1.0 / D), _colsum(err * err)
    dx, loss_cols = rowwise("loss", loss_fn, m, ins=[(x, D, 0), (target, D, 0)], outs=[(D, F32)], accs=[(1, D)])

    gw = {n: None for n in ["w_in", "w_a", "w_b", "w_out", "w_gu", "w_down", "w_ple", "w_ple_gate"]}
    gs = {n: [None] * DEPTH for n in SMALL}
    dlb_f, dlb_b = [None] * DEPTH, [None] * DEPTH

    def wgrad(nm, tag, a, b, l):
        gw[nm] = mm(tag, a, b, "tn", out=(DEPTH, l, gw[nm]))

    for l in reversed(range(DEPTH)):
        t = f"l{l}_bwd_"
        sv = saved[l]
        g_pre, g_post = _row(S["norm_mix_pre"], l), _row(S["norm_mix_post"], l)
        g_fpre, g_fpost = _row(S["norm_ffn_pre"], l), _row(S["norm_ffn_post"], l)
        hg_g = _row(S["hg_norm"], l)
        ln_g, ln_b = _row(S["sg_ln_g"], l), _row(S["sg_ln_b"], l)
        lbf, lbb = _row(lb_f, l), _row(lb_b, l)

        def ple_bwd(d3, ev, tv):
            s = _sigmoid(tv)
            return d3 * s, d3 * ev * s * (1.0 - s)
        de, dt = rowwise(t + "ple", ple_bwd, m, ins=[(dx, D, 0), (sv["e"], D, 0), (sv["tg"], D, 0)],
                         outs=[(D, BF16), (D, BF16)])
        wgrad("w_ple", t + "w_ple", (p, l), de, l)
        wgrad("w_ple_gate", t + "w_ple_gate", sv["x2"], dt, l)
        dx2p = mm(t + "ple_gate_dx", dt, (W["w_ple_gate"], l), "nt")

        def ffn_post_bwd(d3, d2p, f, g):
            d2 = d3 + d2p
            fh, r = _rms(f)
            return d2, _rms_bwd(d2, fh, r, g), _colsum(d2 * fh)
        dx2, dff, gs["norm_ffn_post"][l] = rowwise(
            t + "ffn_post", ffn_post_bwd, m, ins=[(dx, D, 0), (dx2p, D, 0), (sv["ff"], D, 0)], consts=[g_fpost],
            outs=[(D, F32), (D, BF16)], accs=[(1, D)])
        wgrad("w_down", t + "w_down", sv["hid"], dff, l)
        dhid = mm(t + "ffn_out_dx", dff, (W["w_down"], l), "nt")

        def act_bwd(dh, gt, up):
            return (jnp.concatenate([dh * up * _silu_grad(gt), dh * _silu(gt)], axis=1),)
        (dgu,) = rowwise(t + "ffn_act", act_bwd, m, ins=[(dhid, FFN_PAD, 0), (sv["gu"], FFN_PAD, 0),
                                                       (sv["gu"], FFN_PAD, 1)], outs=[(2 * FFN_PAD, BF16)], tm=128)
        wgrad("w_gu", t + "w_gu", sv["h2"], dgu, l)
        dh2 = mm(t + "ffn_in_dx", dgu, (W["w_gu"], l), "nt")

        def pre_post_bwd(d2, dh, x1v, mixv, gf, gp):
            xh, r1 = _rms(x1v)
            d1 = d2 + _rms_bwd(dh, xh, r1, gf)
            mh, rm = _rms(mixv)
            return d1, _rms_bwd(d1, mh, rm, gp), _colsum(dh * xh), _colsum(d1 * mh)
        dx1, dmix, gs["norm_ffn_pre"][l], gs["norm_mix_post"][l] = rowwise(
            t + "mix_post_ffn_pre", pre_post_bwd, m, ins=[(dx2, D, 0), (dh2, D, 0), (sv["x1"], D, 0), (sv["mix"], D, 0)],
            consts=[g_fpre, g_post], outs=[(D, F32), (D, BF16)], accs=[(1, D), (1, D)])
        wgrad("w_out", t + "w_out", sv["merged"], dmix, l)
        dmerged = mm(t + "out_proj_dx", dmix, (W["w_out"], l), "nt")

        def merge_bwd(dm, a, b, gab):
            sa, sb = _sigmoid(gab[:, :D]), _sigmoid(gab[:, D:])
            dgab = jnp.concatenate([dm * a * sa * (1.0 - sa), dm * b * sb * (1.0 - sb)], axis=1)
            return dm * sa, dm * sb, dgab
        dpa, dpb, dz = rowwise(
            t + "merge", merge_bwd, m, ins=[(dmerged, D, 0), (sv["pa"], D, 0), (sv["pb"], D, 0), (sv["z"], 2 * D, 3)],
            outs=[(D, BF16), (D, BF16)], alias_outs=[(jax.ShapeDtypeStruct((m, N_IN), BF16), 2 * D, 3)])
        wgrad("w_a", t + "w_a", sv["a_out"], dpa, l)
        wgrad("w_b", t + "w_b", sv["b_out"], dpb, l)
        da = mm(t + "proj_a_dx", dpa, (W["w_a"], l), "nt")
        db = mm(t + "proj_b_dx", dpb, (W["w_b"], l), "nt")

        dz, dsw, dbias, gs["sg_ln_g"][l], gs["sg_ln_b"][l] = rowwise(
            t + "sgu", _sg_bwd, m, ins=[(db, SG_WIDTH, 0), (sv["z"], SG_WIDTH, ZU), (sv["z"], SG_WIDTH, ZV)],
            consts=[sv["sg_w"], sv["sg_bias"], ln_g, ln_b], alias_outs=[(dz, 2 * SG_WIDTH, 5)],
            accs=[(SG_WIDTH // SG_GROUP_DIM * SG_CHUNK, SG_CHUNK), (SG_CHUNK, SG_WIDTH), (1, SG_WIDTH), (1, SG_WIDTH)])
        gs["sg_w"][l] = dsw.reshape(1, SG_WIDTH // SG_GROUP_DIM, SG_CHUNK, SG_CHUNK)
        gs["sg_b"][l] = dbias.reshape(SG_CHUNK, SG_WIDTH // SG_GROUP_DIM, SG_GROUP_DIM).sum(-1).T[None]

        d_o, dz, gs["hg_norm"][l] = rowwise(
            t + "hgrn_post", _hg_post_bwd, m, ins=[(da, D, 0), (sv["o_f"], D, 0), (sv["o_b"], D, 0), (sv["z"], D, ZG)],
            consts=[hg_g], outs=[(D, BF16)], alias_outs=[(dz, D, ZG)], accs=[(1, D)])
        dq_f, dv_f, dq_b, dv_b, dzf_f, dzf_b, dlb_f[l], dlb_b[l] = hgrn_bwd(
            t + "hgrn", sv["z"], d_o, sv["s_f"], sv["s_b"], lbf, lbb)

        def combine(dqf, dqb, dvf, dvb, dff_, dfb_, zq):
            return (jnp.concatenate([((dqf + dqb) * _silu_grad(zq)).astype(BF16), dff_, dfb_,
                                     (dvf + dvb).astype(BF16)], axis=1),)
        (dz,) = rowwise(t + "hgrn_combine", combine, m,
                        ins=[(dq_f, D, 0), (dq_b, D, 0), (dv_f, D, 0), (dv_b, D, 0), (dzf_f, D, 0), (dzf_b, D, 0),
                             (sv["z"], D, ZQ)], alias_outs=[(dz, 4 * D, 0)], tm=128)
        wgrad("w_in", t + "w_in", sv["h"], dz, l)
        dh = mm(t + "in_proj_dx", dz, (W["w_in"], l), "nt")

        def pre_bwd(d1, dhv, xv, g):
            xh, r = _rms(xv)
            return d1 + _rms_bwd(dhv, xh, r, g), _colsum(dhv * xh)
        dx, gs["norm_mix_pre"][l] = rowwise(t + "pre_norm", pre_bwd, m, ins=[(dx1, D, 0), (dh, D, 0), (sv["x"], D, 0)],
                                            consts=[g_pre], outs=[(D, F32)], accs=[(1, D)])
        saved[l] = None

    gs["lb_gamma_fwd"], gs["lb_gamma_bwd"] = lower_bounds_bwd(
        "lower_bounds_bwd", S["lb_gamma_fwd"], S["lb_gamma_bwd"], jnp.concatenate(dlb_f, axis=0),
        jnp.concatenate(dlb_b, axis=0))
    small = {n: (g if not isinstance(g, list) else jnp.concatenate(g, axis=0)).reshape(S[n].shape)
             for n, g in gs.items()}
    return loss_cols, dx, gw, small


MESH = pl.DeviceIdType.MESH


def _slab(ref, axis, start, size):
    idx = [slice(None)] * 3
    idx[axis] = pl.ds(start, size)
    return ref.at[tuple(idx)]


def exchange(name, srcs, dst_shapes, items):
    ns, nd, ni = len(srcs), len(dst_shapes), len(items)

    def body(*refs):
        src, dst = refs[:ns], refs[ns:ns + nd]
        send_sem, recv_sem, loc_sem = refs[ns + nd:]
        x, y, c = lax.axis_index("x"), lax.axis_index("y"), lax.axis_index("c")
        me = 4 * x + 2 * y + c
        waits = []
        for n, (kind, si, di, axis, size, base) in enumerate(items):
            def views(to_dev, from_dev):
                if kind == "gather":
                    return src[si], _slab(dst[di], axis, base + pl.multiple_of(from_dev * size, 128), size)
                return _slab(src[si], axis, base + pl.multiple_of(to_dev * size, 128), size), dst[di].at[from_dev]

            s_own, d_own = views(me, me)
            own = pltpu.make_async_copy(s_own, d_own, loc_sem.at[n])
            own.start()
            waits.append(own)
            for k in range(1, NDEV):
                px = 1 - x if k & 4 else x
                py = 1 - y if k & 2 else y
                pc = 1 - c if k & 1 else c
                peer = 4 * px + 2 * py + pc
                s_out, _ = views(peer, me)
                sem = n * (NDEV - 1) + k - 1
                pltpu.make_async_remote_copy(s_out, d_own, send_sem.at[sem], recv_sem.at[sem], device_id=(px, py, pc),
                                             device_id_type=MESH).start()
                _, d_in = views(me, peer)
                waits.append(pltpu.make_async_remote_copy(s_out, d_in, send_sem.at[sem], recv_sem.at[sem],
                                                          device_id=(px, py, pc), device_id_type=MESH))
        for w in waits:
            w.wait()

    any_spec = pl.BlockSpec(memory_space=pl.ANY)
    return pl.pallas_call(
        body, name=name, in_specs=[any_spec] * ns, out_specs=[any_spec] * nd,
        out_shape=[jax.ShapeDtypeStruct(s, dt) for (s, dt) in dst_shapes],
        scratch_shapes=[pltpu.SemaphoreType.DMA((ni * (NDEV - 1),)), pltpu.SemaphoreType.DMA((ni * (NDEV - 1),)),
                        pltpu.SemaphoreType.DMA((ni,))],
        compiler_params=pltpu.CompilerParams(has_side_effects=True),
    )(*srcs)


def allreduce_small(name, part):
    rows, width = part.shape

    def body(p_ref, o_ref, buf, send_sem, recv_sem):
        x, y, c = lax.axis_index("x"), lax.axis_index("y"), lax.axis_index("c")
        me = 4 * x + 2 * y + c
        buf[me] = p_ref[...]
        waits = []
        for k in range(1, NDEV):
            px = 1 - x if k & 4 else x
            py = 1 - y if k & 2 else y
            pc = 1 - c if k & 1 else c
            peer = 4 * px + 2 * py + pc
            pltpu.make_async_remote_copy(p_ref, buf.at[me], send_sem.at[k - 1], recv_sem.at[k - 1],
                                         device_id=(px, py, pc), device_id_type=MESH).start()
            waits.append(pltpu.make_async_remote_copy(p_ref, buf.at[peer], send_sem.at[k - 1], recv_sem.at[k - 1],
                                                      device_id=(px, py, pc), device_id_type=MESH))
        for w in waits:
            w.wait()
        acc = buf[0]
        for j in range(1, NDEV):
            acc = acc + buf[j]
        o_ref[...] = acc

    vmem = pl.BlockSpec(memory_space=pltpu.VMEM)
    return pl.pallas_call(
        body, name=name, in_specs=[vmem], out_specs=vmem, out_shape=jax.ShapeDtypeStruct((rows, width), F32),
        scratch_shapes=[pltpu.VMEM((NDEV, rows, width), F32), pltpu.SemaphoreType.DMA((NDEV - 1,)),
                        pltpu.SemaphoreType.DMA((NDEV - 1,))],
        compiler_params=pltpu.CompilerParams(vmem_limit_bytes=VMEM_LIMIT, has_side_effects=True),
    )(part)


def cast_pad(name, w, rows_p, cols_p):
    _, r, c = w.shape

    def body(w_ref, o_ref):
        if (rows_p, cols_p) != (r, c):
            o_ref[...] = jnp.zeros(o_ref.shape, BF16)
        o_ref[0:r, 0:c] = w_ref[...].astype(BF16)

    return pl.pallas_call(
        body, name=name, grid=(DEPTH,), in_specs=[pl.BlockSpec((None, r, c), lambda l: (l, 0, 0))],
        out_specs=pl.BlockSpec((None, rows_p, cols_p), lambda l: (l, 0, 0)),
        out_shape=jax.ShapeDtypeStruct((DEPTH, rows_p, cols_p), BF16), compiler_params=_params(("parallel",)),
    )(w)


def adam(name, w, m_, v_, tr, g=None, slots=None):
    L, r, c = w.shape
    assert r % tr == 0

    def body(g_ref, w_ref, m_ref, v_ref, g_out, d_out, m_out, v_out):
        if slots is None:
            gv = g_ref[...]
        else:
            gv = g_ref[0][:, :c]
            for j in range(1, NDEV):
                gv = gv + g_ref[j][:, :c]
        m2 = B1 * m_ref[...] + (1.0 - B1) * gv
        v2 = B2 * v_ref[...] + (1.0 - B2) * (gv * gv)
        m_hat = m2 / (1.0 - B1 ** STEP)
        v_hat = v2 / (1.0 - B2 ** STEP)
        g_out[...] = gv
        d_out[...] = -LR * (m_hat / (jnp.sqrt(v_hat) + AEPS) + WD * w_ref[...])
        m_out[...] = m2
        v_out[...] = v2

    spec = pl.BlockSpec((None, tr, c), lambda l, i: (l, i, 0))
    if slots is None:
        g_arr, g_spec = g, spec
    else:
        g_arr, g_spec = slots, pl.BlockSpec((NDEV, None, tr, slots.shape[3]), lambda l, i: (0, l, i, 0))
    shp = jax.ShapeDtypeStruct(w.shape, F32)
    return pl.pallas_call(
        body, name=name, grid=(L, r // tr), in_specs=[g_spec, spec, spec, spec], out_specs=[spec] * 4,
        out_shape=[shp] * 4, compiler_params=_params(("parallel", "parallel")),
    )(g_arr, w, m_, v_)


def _pack(arrs):
    parts = []
    for a in arrs:
        a2 = a.reshape(-1, D)
        parts.append(jnp.pad(a2, ((0, -a2.shape[0] % 8), (0, 0))))
    return jnp.concatenate(parts, axis=0)


def _unpack(buf, shapes):
    out, off = [], 0
    for s in shapes:
        rows = 1
        for d_ in s:
            rows *= d_
        rows //= D
        out.append(buf[off:off + rows].reshape(s))
        off += rows + (-rows % 8)
    return out


def _big_layout():
    return {
        "w_in": (2, 1024, "w_in", 0), "w_a": (1, 128, "w_a", 0), "w_b": (2, 128, "w_b", 0),
        "w_out": (1, 128, "w_out", 0), "w_gate": (2, FFN_SHARD_PAD, "w_gu", 0),
        "w_up": (2, FFN_SHARD_PAD, "w_gu", FFN_PAD), "w_down": (1, FFN_SHARD_PAD, "w_down", 0),
        "w_ple": (2, 128, "w_ple", 0), "w_ple_gate": (1, 128, "w_ple_gate", 0),
    }


def kernel(x, p, norm_mix_pre, w_in, lb_gamma_fwd, lb_gamma_bwd, hg_norm, sg_w, sg_b, sg_ln_g, sg_ln_b, w_a, w_b, w_out, norm_mix_post, norm_ffn_pre, w_gate, w_up, w_down, norm_ffn_post, w_ple, w_ple_gate, loss_target, m_norm_mix_pre, m_w_in, m_lb_gamma_fwd, m_lb_gamma_bwd, m_hg_norm, m_sg_w, m_sg_b, m_sg_ln_g, m_sg_ln_b, m_w_a, m_w_b, m_w_out, m_norm_mix_post, m_norm_ffn_pre, m_w_gate, m_w_up, m_w_down, m_norm_ffn_post, m_w_ple, m_w_ple_gate, v_norm_mix_pre, v_w_in, v_lb_gamma_fwd, v_lb_gamma_bwd, v_hg_norm, v_sg_w, v_sg_b, v_sg_ln_g, v_sg_ln_b, v_w_a, v_w_b, v_w_out, v_norm_mix_post, v_norm_ffn_pre, v_w_gate, v_w_up, v_w_down, v_norm_ffn_post, v_w_ple, v_w_ple_gate):
    a = dict(zip(INPUTS, (x, p, norm_mix_pre, w_in, lb_gamma_fwd, lb_gamma_bwd, hg_norm, sg_w, sg_b, sg_ln_g, sg_ln_b, w_a, w_b, w_out, norm_mix_post, norm_ffn_pre, w_gate, w_up, w_down, norm_ffn_post, w_ple, w_ple_gate, loss_target, m_norm_mix_pre, m_w_in, m_lb_gamma_fwd, m_lb_gamma_bwd, m_hg_norm, m_sg_w, m_sg_b, m_sg_ln_g, m_sg_ln_b, m_w_a, m_w_b, m_w_out, m_norm_mix_post, m_norm_ffn_pre, m_w_gate, m_w_up, m_w_down, m_norm_ffn_post, m_w_ple, m_w_ple_gate, v_norm_mix_pre, v_w_in, v_lb_gamma_fwd, v_lb_gamma_bwd, v_hg_norm, v_sg_w, v_sg_b, v_sg_ln_g, v_sg_ln_b, v_w_a, v_w_b, v_w_out, v_norm_mix_post, v_norm_ffn_pre, v_w_gate, v_w_up, v_w_down, v_norm_ffn_post, v_w_ple, v_w_ple_gate)))
    m = x.shape[1]
    layout = _big_layout()

    shards, gathered, items = [], {}, []
    for n in BIG:
        axis, size, dst, base = layout[n]
        _, r, c = a[n].shape
        rp, cp = (size, c) if axis == 1 else (r, size)
        shards.append(cast_pad("cast_" + n, a[n], rp, cp))
        full = (DEPTH, rp * NDEV, cp) if axis == 1 else (DEPTH, rp, cp * NDEV)
        if dst in gathered:
            full = (full[0], full[1], full[2] * 2)
        gathered[dst] = (full, BF16)
        items.append(("gather", len(shards) - 1, dst, axis, size, base))
    dst_names = list(gathered)
    items = [(k, si, dst_names.index(dn), ax, sz, bs) for (k, si, dn, ax, sz, bs) in items]
    W = dict(zip(dst_names, exchange("gather_weights", shards, [gathered[n] for n in dst_names], items)))

    loss_cols, dx, gw, gs = local_step(x[0], p[:, 0], loss_target[0], W, {n: a[n] for n in SMALL})
    loss = lax.psum(jnp.sum(loss_cols) * (0.5 / D), ("x", "y", "c"))

    srcs, slot_shapes, items = list(gw), [], []
    for n in BIG:
        axis, size, dst, base = layout[n]
        _, r, c = a[n].shape
        rp, cp = (size, c) if axis == 1 else (r, size)
        slot_shapes.append(((NDEV, DEPTH, rp, cp), F32))
        items.append(("scatter", srcs.index(dst), len(slot_shapes) - 1, axis, size, base))
    slots = dict(zip(BIG, exchange("scatter_grads", [gw[n] for n in srcs], slot_shapes, items)))

    small_shapes = [a[n].shape for n in SMALL]
    g_small = allreduce_small("allreduce_small", _pack([gs[n] for n in SMALL]))

    res = {}
    row_tiles = {"w_in": 128, "w_a": 128, "w_b": 512, "w_out": 128, "w_gate": 128, "w_up": 128, "w_down": 88,
                 "w_ple": 256, "w_ple_gate": 128}
    for n in BIG:
        res[n] = adam("adam_" + n, a[n], a["m_" + n], a["v_" + n], row_tiles[n], slots=slots[n])
    packed = [_pack([a[pre + n] for n in SMALL])[None] for pre in ("", "m_", "v_")]
    small_res = adam("adam_small", packed[0], packed[1], packed[2], packed[0].shape[1], g=g_small[None])
    small_res = [_unpack(r_[0], small_shapes) for r_ in small_res]
    for i, n in enumerate(SMALL):
        res[n] = tuple(small_res[k][i] for k in range(4))

    outs = [loss, dx.reshape(1, m, D)]
    for k in range(4):
        outs += [res[n][k] for n in WEIGHTS]
    return tuple(outs)
```

```python
import jax
import jax.numpy as jnp
from jax import lax
from jax.experimental import pallas as pl
from jax.experimental.pallas import tpu as pltpu

F32 = jnp.float32
BF16 = jnp.bfloat16

D = 1024
N_IN = 8192
HEADS = 8
HEAD_DIM = 128
SG_CHUNK = 128
SG_WIDTH = 512
SG_GROUP_DIM = 64
FFN = 2816
PLE_DIM = 256
EPS = 1e-6
DEPTH = 2
ZQ, ZFF, ZFB, ZI, ZG, GA, GB = 0, 1, 2, 3, 4, 6, 7
ZU, ZV = 10, 11

NDEV = 8
FFN_SHARD = FFN // NDEV
FFN_SHARD_PAD = 384
FFN_PAD = NDEV * FFN_SHARD_PAD

LR, B1, B2, AEPS, WD, STEP = 0.001, 0.9, 0.999, 1e-08, 0.01, 10

ROW_TILE = 256
HG_CHUNK = 64
HG_BLOCK_FWD = 256
HG_BLOCK_BWD = 128
EXP_CLAMP = 80.0
TINY = float(jnp.finfo(jnp.float32).tiny)
VMEM_LIMIT = 56 * 1024 * 1024

BIG = ["w_in", "w_a", "w_b", "w_out", "w_gate", "w_up", "w_down", "w_ple", "w_ple_gate"]
SMALL = ["norm_mix_pre", "lb_gamma_fwd", "lb_gamma_bwd", "hg_norm", "sg_w", "sg_b", "sg_ln_g", "sg_ln_b",
         "norm_mix_post", "norm_ffn_pre", "norm_ffn_post"]
WEIGHTS = ["norm_mix_pre", "w_in", "lb_gamma_fwd", "lb_gamma_bwd", "hg_norm", "sg_w", "sg_b", "sg_ln_g", "sg_ln_b",
           "w_a", "w_b", "w_out", "norm_mix_post", "norm_ffn_pre", "w_gate", "w_up", "w_down", "norm_ffn_post",
           "w_ple", "w_ple_gate"]
INPUTS = (["x", "p"] + WEIGHTS + ["loss_target"] + ["m_" + n for n in WEIGHTS] + ["v_" + n for n in WEIGHTS])
LAYOUT = {
    "w_in": (1, 1024, "w_in", 0), "w_a": (0, 128, "w_a", 0), "w_b": (1, 128, "w_b", 0),
    "w_out": (0, 128, "w_out", 0), "w_gate": (1, FFN_SHARD_PAD, "w_gu", 0),
    "w_up": (1, FFN_SHARD_PAD, "w_gu", FFN_PAD), "w_down": (0, FFN_SHARD_PAD, "w_down", 0),
    "w_ple": (1, 128, "w_ple", 0), "w_ple_gate": (0, 128, "w_ple_gate", 0),
}
GATHERED = {"w_in": (D, N_IN), "w_a": (D, D), "w_b": (SG_WIDTH, D), "w_out": (D, D), "w_gu": (D, 2 * FFN_PAD),
            "w_down": (FFN_PAD, D), "w_ple": (PLE_DIM, D), "w_ple_gate": (D, D)}


def _params(sem):
    return pltpu.CompilerParams(dimension_semantics=sem, vmem_limit_bytes=VMEM_LIMIT)


def _dot(a, b):
    return lax.dot_general(a, b, (((1,), (0,)), ((), ())), preferred_element_type=F32)


def _dot_nt(a, b):
    return lax.dot_general(a, b, (((1,), (1,)), ((), ())), preferred_element_type=F32)


def _dot_tn(a, b):
    return lax.dot_general(a, b, (((0,), (0,)), ((), ())), preferred_element_type=F32)


def _sigmoid(x):
    return jax.nn.sigmoid(x)


def _silu(x):
    return x * _sigmoid(x)


def _silu_grad(x):
    s = _sigmoid(x)
    return s * (1.0 + x * (1.0 - s))


def _gelu(x):
    return 0.5 * x * (1.0 + lax.erf(x * 0.7071067811865476))


def _gelu_grad(x):
    return 0.5 * (1.0 + lax.erf(x * 0.7071067811865476)) + x * jnp.exp(-0.5 * x * x) * 0.3989422804014327


def _mean(x):
    return jnp.mean(x, axis=-1, keepdims=True)


def _colsum(x):
    return jnp.sum(x, axis=0, keepdims=True)


def _rms(x):
    r = lax.rsqrt(_mean(x * x) + EPS)
    return x * r, r


def _rms_bwd(dy, xh, r, g):
    dyg = dy * g
    return r * (dyg - xh * _mean(dyg * xh))


MESH = pl.DeviceIdType.MESH
ANY = pl.BlockSpec(memory_space=pl.ANY)


def _slab(ref, axis, start, size):
    idx = [slice(None)] * 2
    idx[axis] = pl.ds(start, size)
    return ref.at[tuple(idx)]


class Exchange:
    def __init__(self, srcs, dsts, items):
        self.srcs, self.dsts, self.items = list(srcs), list(dsts), list(items)

    def specs(self):
        n = len(self.items)
        sems = [pltpu.SemaphoreType.DMA((n * (NDEV - 1),)), pltpu.SemaphoreType.DMA((n * (NDEV - 1),)),
                pltpu.SemaphoreType.DMA((n,))]
        return ([ANY] * len(self.srcs), [ANY] * len(self.dsts),
                [jax.ShapeDtypeStruct(s, dt) for (s, dt) in self.dsts], sems)

    def copies(self, src, dst, send_sem, recv_sem, loc_sem):
        x, y, c = lax.axis_index("x"), lax.axis_index("y"), lax.axis_index("c")
        me = 4 * x + 2 * y + c
        starts, waits = [], []
        for n, (kind, si, di, axis, size, base, layer) in enumerate(self.items):
            def views(to_dev, from_dev):
                if kind == "gather":
                    return (src[si].at[layer],
                            _slab(dst[di], axis, base + pl.multiple_of(from_dev * size, 128), size))
                return _slab(src[si], axis, base + pl.multiple_of(to_dev * size, 128), size), dst[di].at[from_dev]

            s_own, d_own = views(me, me)
            own = pltpu.make_async_copy(s_own, d_own, loc_sem.at[n])
            starts.append(own)
            waits.append(own)
            for k in range(1, NDEV):
                px = 1 - x if k & 4 else x
                py = 1 - y if k & 2 else y
                pc = 1 - c if k & 1 else c
                peer = 4 * px + 2 * py + pc
                s_out, _ = views(peer, me)
                _, d_in = views(me, peer)
                sem = n * (NDEV - 1) + k - 1
                starts.append(pltpu.make_async_remote_copy(s_out, d_own, send_sem.at[sem], recv_sem.at[sem],
                                                           device_id=(px, py, pc), device_id_type=MESH))
                waits.append(pltpu.make_async_remote_copy(s_out, d_in, send_sem.at[sem], recv_sem.at[sem],
                                                          device_id=(px, py, pc), device_id_type=MESH))
        return starts, waits


def exchange(name, exch):
    e_in, e_out, e_shape, e_scr = exch.specs()
    ns, nd = len(e_in), len(e_out)

    def body(*refs):
        starts, waits = exch.copies(refs[:ns], refs[ns:ns + nd], *refs[ns + nd:])
        for cp in starts:
            cp.start()
        for cp in waits:
            cp.wait()

    return pl.pallas_call(body, name=name, in_specs=e_in, out_specs=e_out, out_shape=e_shape, scratch_shapes=e_scr,
                          compiler_params=pltpu.CompilerParams(has_side_effects=True))(*exch.srcs)


def hosted_call(body, exch, name, grid, in_specs, out_specs, out_shape, scratch_shapes, operands, semantics,
                aliases=None):
    aliases = aliases or {}
    if exch is None:
        res = pl.pallas_call(body, name=name, grid=grid, in_specs=in_specs, out_specs=out_specs, out_shape=out_shape,
                             scratch_shapes=scratch_shapes, input_output_aliases=aliases,
                             compiler_params=_params(semantics))(*operands)
        return list(res), []
    n_in, n_out, n_scr = len(in_specs), len(out_specs), len(scratch_shapes)
    e_in, e_out, e_shape, e_scr = exch.specs()
    ns, nd = len(e_in), len(e_out)

    def at_step(last):
        cond = None
        for ax, n in enumerate(grid):
            c = pl.program_id(ax) == (n - 1 if last else 0)
            cond = c if cond is None else jnp.logical_and(cond, c)
        return cond

    def wrapped(*refs):
        ins, src = refs[:n_in], refs[n_in:n_in + ns]
        o0 = n_in + ns
        outs, dst = refs[o0:o0 + n_out], refs[o0 + n_out:o0 + n_out + nd]
        s0 = o0 + n_out + nd
        scr, sems = refs[s0:s0 + n_scr], refs[s0 + n_scr:]

        @pl.when(at_step(False))
        def _():
            for cp in exch.copies(src, dst, *sems)[0]:
                cp.start()

        body(*ins, *outs, *scr)

        @pl.when(at_step(True))
        def _():
            for cp in exch.copies(src, dst, *sems)[1]:
                cp.wait()

    res = pl.pallas_call(
        wrapped, name=name, grid=grid, in_specs=list(in_specs) + e_in, out_specs=list(out_specs) + e_out,
        out_shape=list(out_shape) + e_shape, scratch_shapes=list(scratch_shapes) + e_scr,
        input_output_aliases=aliases,
        compiler_params=pltpu.CompilerParams(dimension_semantics=("arbitrary",) * len(grid),
                                             vmem_limit_bytes=VMEM_LIMIT, has_side_effects=True),
    )(*operands, *exch.srcs)
    return list(res[:n_out]), list(res[n_out:])


def allreduce_small(name, part):
    rows, width = part.shape

    def body(p_ref, o_ref, buf, send_sem, recv_sem):
        x, y, c = lax.axis_index("x"), lax.axis_index("y"), lax.axis_index("c")
        me = 4 * x + 2 * y + c
        buf[me] = p_ref[...]
        waits = []
        for k in range(1, NDEV):
            px = 1 - x if k & 4 else x
            py = 1 - y if k & 2 else y
            pc = 1 - c if k & 1 else c
            peer = 4 * px + 2 * py + pc
            pltpu.make_async_remote_copy(p_ref, buf.at[me], send_sem.at[k - 1], recv_sem.at[k - 1],
                                         device_id=(px, py, pc), device_id_type=MESH).start()
            waits.append(pltpu.make_async_remote_copy(p_ref, buf.at[peer], send_sem.at[k - 1], recv_sem.at[k - 1],
                                                      device_id=(px, py, pc), device_id_type=MESH))
        for w in waits:
            w.wait()
        acc = buf[0]
        for j in range(1, NDEV):
            acc = acc + buf[j]
        o_ref[...] = acc

    vmem = pl.BlockSpec(memory_space=pltpu.VMEM)
    return pl.pallas_call(
        body, name=name, in_specs=[vmem], out_specs=vmem, out_shape=jax.ShapeDtypeStruct((rows, width), F32),
        scratch_shapes=[pltpu.VMEM((NDEV, rows, width), F32), pltpu.SemaphoreType.DMA((NDEV - 1,)),
                        pltpu.SemaphoreType.DMA((NDEV - 1,))],
        compiler_params=pltpu.CompilerParams(vmem_limit_bytes=VMEM_LIMIT, has_side_effects=True),
    )(part)


def rowwise(name, fn, m, ins=(), consts=(), outs=(), alias_outs=(), accs=(), tm=ROW_TILE):
    tm = min(tm, m)
    n_in, n_c, n_o, n_al, n_ac = len(ins), len(consts), len(outs), len(alias_outs), len(accs)
    held = [a for (a, _, _) in alias_outs if not isinstance(a, jax.ShapeDtypeStruct)]
    n_held = len(held)

    def body(*refs):
        in_refs = refs[:n_in + n_c]
        out_refs = refs[n_in + n_c + n_held:]
        vals = fn(*[r[...] for r in in_refs])
        if not isinstance(vals, (tuple, list)):
            vals = (vals,)
        for r, v in zip(out_refs[:n_o + n_al], vals[:n_o + n_al]):
            r[...] = v.astype(r.dtype)
        if n_ac:
            acc_refs = out_refs[n_o + n_al:]

            @pl.when(pl.program_id(0) == 0)
            def _():
                for r in acc_refs:
                    r[...] = jnp.zeros(r.shape, F32)

            for r, v in zip(acc_refs, vals[n_o + n_al:]):
                r[...] += v

    def col(cb):
        return lambda i: (i, cb)

    in_specs = [pl.BlockSpec((tm, w), col(cb)) for (_, w, cb) in ins]
    in_specs += [pl.BlockSpec(c.shape, lambda i, nd=c.ndim: (0,) * nd) for c in consts]
    in_specs += [ANY for _ in held]
    out_shape = [jax.ShapeDtypeStruct((m, w), dt) for (w, dt) in outs]
    out_specs = [pl.BlockSpec((tm, w), col(0)) for (w, _) in outs]
    out_shape += [jax.ShapeDtypeStruct(a.shape, a.dtype) for (a, _, _) in alias_outs]
    out_specs += [pl.BlockSpec((tm, w), col(cb)) for (_, w, cb) in alias_outs]
    out_shape += [jax.ShapeDtypeStruct(s, F32) for s in accs]
    out_specs += [pl.BlockSpec(s, lambda i: (0, 0)) for s in accs]
    aliases, k_in = {}, n_in + n_c
    for k, (a, _, _) in enumerate(alias_outs):
        if not isinstance(a, jax.ShapeDtypeStruct):
            aliases[k_in] = n_o + k
            k_in += 1
    return pl.pallas_call(
        body, name=name, grid=(m // tm,), in_specs=in_specs, out_specs=out_specs, out_shape=out_shape,
        input_output_aliases=aliases,
        compiler_params=_params(("arbitrary",) if n_ac else ("parallel",)),
    )(*[a for (a, _, _) in ins], *consts, *held)


def _operand(arr, bshape, imap):
    if isinstance(arr, tuple):
        arr, lead = arr
        return arr, pl.BlockSpec((None,) + bshape, lambda *g: (lead,) + imap(*g))
    return arr, pl.BlockSpec(bshape, imap)


def _shape2(arr):
    return arr[0].shape[1:] if isinstance(arr, tuple) else arr.shape


def mm(name, a, b, mode, out_dtype=F32, tm=1024, tn=512, tk=1024, exch=None):
    sa, sb = _shape2(a), _shape2(b)
    if mode == "nn":
        (M, K), N = sa, sb[1]
    elif mode == "nt":
        (M, K), N = sa, sb[0]
    else:
        (K, M), N = sa, sb[1]
    tm, tn, tk = min(tm, M), min(tn, N), min(tk, K)
    assert M % tm == 0 and N % tn == 0 and K % tk == 0, (name, M, N, K)
    nk = K // tk
    if mode == "nn":
        a_arr, a_spec = _operand(a, (tm, tk), lambda i, j, k: (i, k))
        b_arr, b_spec = _operand(b, (tk, tn), lambda i, j, k: (k, j))
        dot = _dot
    elif mode == "nt":
        a_arr, a_spec = _operand(a, (tm, tk), lambda i, j, k: (i, k))
        b_arr, b_spec = _operand(b, (tn, tk), lambda i, j, k: (j, k))
        dot = _dot_nt
    else:
        a_arr, a_spec = _operand(a, (tk, tm), lambda i, j, k: (k, i))
        b_arr, b_spec = _operand(b, (tk, tn), lambda i, j, k: (k, j))
        dot = _dot_tn

    def body(a_ref, b_ref, o_ref, *acc):
        part = dot(a_ref[...].astype(BF16), b_ref[...].astype(BF16))
        if nk == 1:
            o_ref[...] = part.astype(o_ref.dtype)
            return
        acc_ref, k = acc[0], pl.program_id(2)

        @pl.when(k == 0)
        def _():
            acc_ref[...] = part

        @pl.when(k > 0)
        def _():
            acc_ref[...] += part

        @pl.when(k == nk - 1)
        def _():
            o_ref[...] = acc_ref[...].astype(o_ref.dtype)

    outs, extra = hosted_call(
        body, exch, name, (M // tm, N // tn, nk), [a_spec, b_spec], [pl.BlockSpec((tm, tn), lambda i, j, k: (i, j))],
        [jax.ShapeDtypeStruct((M, N), out_dtype)], [pltpu.VMEM((tm, tn), F32)] if nk > 1 else [], [a_arr, b_arr],
        ("parallel", "parallel", "arbitrary"))
    return outs[0] if exch is None else (outs[0], extra)


def _cumsum_rows(x, reverse):
    n = x.shape[0]
    row = lax.broadcasted_iota(jnp.int32, x.shape, 0)
    s = 1
    while s < n:
        if reverse:
            x = x + jnp.where(row < n - s, pltpu.roll(x, n - s, 0), 0.0)
        else:
            x = x + jnp.where(row >= s, pltpu.roll(x, s, 0), 0.0)
        s *= 2
    return x


def _hg_prep(zq, zf, lb, reverse):
    n = zq.shape[0]
    q = _silu(zq)
    sig, sn = _sigmoid(zf), _sigmoid(-zf)
    f = lb + (1.0 - lb) * sig
    k = (1.0 - lb) * sn
    b = _cumsum_rows(jnp.log(jnp.maximum(f, TINY)), reverse)
    b_last = b[0:1] if reverse else b[n - 1:n]
    b_ref = b[n // 2:n // 2 + 1]
    e1 = jnp.exp(b)
    e2 = jnp.exp(jnp.clip(b - b_ref, -EXP_CLAMP, EXP_CLAMP))
    e3 = jnp.exp(jnp.clip(b_ref - b, -EXP_CLAMP, EXP_CLAMP))
    e4 = jnp.exp(b_last - b)
    return dict(q=q, k=k, sig=sig, sn=sn, f=f, e1=e1, e2=e2, e3=e3, e4=e4, e_last=jnp.exp(b_last),
                qe=(q * e1).astype(BF16), qt=(q * e2).astype(BF16), kt=(k * e3).astype(BF16),
                ks=(k * e4).astype(BF16))


def _hg_mask(n, reverse):
    t = lax.broadcasted_iota(jnp.int32, (n, n), 0)
    s = lax.broadcasted_iota(jnp.int32, (n, n), 1)
    return (s >= t) if reverse else (s <= t)


def hgrn_fwd(name, z, lb_f, lb_b, exch=None):
    m = z.shape[0]
    C, T = HG_CHUNK, min(HG_BLOCK_FWD, m)
    nb, cpb = m // T, T // C

    def body(zq_f, zf_f, zi_f, zq_b, zf_b, zi_b, lbf_ref, lbb_ref, of_ref, ob_ref, sf_ref, sb_ref, st_ref):
        @pl.when(pl.program_id(0) == 0)
        def _():
            st_ref[...] = jnp.zeros(st_ref.shape, F32)

        dirs = ((zq_f, zf_f, zi_f, lbf_ref, of_ref, sf_ref), (zq_b, zf_b, zi_b, lbb_ref, ob_ref, sb_ref))

        def chunk(ci, carry):
            for d, (zq, zf, zi, lb_ref, o_ref, s_ref) in enumerate(dirs):
                cc = ci if d == 0 else cpb - 1 - ci
                rows = pl.ds(pl.multiple_of(cc * C, C), C)
                pre = _hg_prep(zq[rows, :], zf[rows, :], lb_ref[...], d == 1)
                v = zi[rows, :].astype(BF16)
                mask = _hg_mask(C, d == 1)
                for h in range(HEADS):
                    sl = slice(h * HEAD_DIM, (h + 1) * HEAD_DIM)
                    st = st_ref[d, h]
                    s_ref[cc, h] = st
                    a = jnp.where(mask, _dot_nt(pre["qt"][:, sl], pre["kt"][:, sl]), 0.0)
                    o_ref[rows, sl] = _dot_nt(pre["qe"][:, sl], st.astype(BF16)) + _dot(a.astype(BF16), v[:, sl])
                    st_ref[d, h] = st * pre["e_last"][:, sl] + _dot_tn(v[:, sl], pre["ks"][:, sl])
            return carry

        lax.fori_loop(0, cpb, chunk, 0)

    def zspec(cb, rev):
        return pl.BlockSpec((T, D), (lambda i: (nb - 1 - i, cb)) if rev else (lambda i: (i, cb)))

    def sspec(rev):
        shape = (cpb, HEADS, HEAD_DIM, HEAD_DIM)
        return pl.BlockSpec(shape, (lambda i: (nb - 1 - i, 0, 0, 0)) if rev else (lambda i: (i, 0, 0, 0)))

    lbspec = pl.BlockSpec((1, D), lambda i: (0, 0))
    states = jax.ShapeDtypeStruct((m // C, HEADS, HEAD_DIM, HEAD_DIM), F32)
    outs, extra = hosted_call(
        body, exch, name, (nb,),
        [zspec(ZQ, False), zspec(ZFF, False), zspec(ZI, False), zspec(ZQ, True), zspec(ZFB, True), zspec(ZI, True),
         lbspec, lbspec],
        [zspec(0, False), zspec(0, True), sspec(False), sspec(True)],
        [jax.ShapeDtypeStruct((m, D), F32), jax.ShapeDtypeStruct((m, D), F32), states, states],
        [pltpu.VMEM((2, HEADS, HEAD_DIM, HEAD_DIM), F32)], [z, z, z, z, z, z, lb_f, lb_b], ("arbitrary",))
    return outs, extra


def hgrn_bwd(name, z, d_o, s_f, s_b, lb_f, lb_b, exch=None):
    m = z.shape[0]
    C, T = HG_CHUNK, min(HG_BLOCK_BWD, m)
    nb, cpb = m // T, T // C

    def body(zq_f, zf_f, zi_f, do_f, sf_ref, zq_b, zf_b, zi_b, do_b, sb_ref, lbf_ref, lbb_ref,
             dqf_ref, dvf_ref, dqb_ref, dvb_ref, dzf_f, dzf_b, dlbf_ref, dlbb_ref,
             dst_ref, dki_ref, dks_ref, rr_ref):
        @pl.when(pl.program_id(0) == 0)
        def _():
            dst_ref[...] = jnp.zeros(dst_ref.shape, F32)
            dlbf_ref[...] = jnp.zeros(dlbf_ref.shape, F32)
            dlbb_ref[...] = jnp.zeros(dlbb_ref.shape, F32)

        dirs = ((zq_f, zf_f, zi_f, do_f, sf_ref, lbf_ref, dqf_ref, dvf_ref, dzf_f, dlbf_ref),
                (zq_b, zf_b, zi_b, do_b, sb_ref, lbb_ref, dqb_ref, dvb_ref, dzf_b, dlbb_ref))

        def chunk(ci, carry):
            for d, (zq, zf, zi, do_ref, s_ref, lb_ref, dq_ref, dv_ref, dzf_ref, dlb_ref) in enumerate(dirs):
                rev = d == 1
                cc = cpb - 1 - ci if d == 0 else ci
                rows = pl.ds(pl.multiple_of(cc * C, C), C)
                lb = lb_ref[...]
                pre = _hg_prep(zq[rows, :], zf[rows, :], lb, rev)
                v = zi[rows, :].astype(BF16)
                do = do_ref[rows, :]
                mask = _hg_mask(C, rev)
                for h in range(HEADS):
                    sl = slice(h * HEAD_DIM, (h + 1) * HEAD_DIM)
                    st_prev = s_ref[cc, h]
                    dst = dst_ref[d, h]
                    dst16 = dst.astype(BF16)
                    qt, kt, ks, qe = pre["qt"][:, sl], pre["kt"][:, sl], pre["ks"][:, sl], pre["qe"][:, sl]
                    a = jnp.where(mask, _dot_nt(qt, kt), 0.0).astype(BF16)
                    da = jnp.where(mask, _dot_nt(do[:, sl], v[:, sl]), 0.0).astype(BF16)
                    dq_ref[rows, sl] = (_dot(da, kt) * pre["e2"][:, sl]
                                        + _dot(do[:, sl], st_prev.astype(BF16)) * pre["e1"][:, sl])
                    dki_ref[d, :, sl] = _dot_tn(da, qt) * pre["e3"][:, sl]
                    dks_ref[d, :, sl] = _dot(v[:, sl], dst16) * pre["e4"][:, sl]
                    dv_ref[rows, sl] = _dot_tn(a, do[:, sl]) + _dot_nt(ks, dst16)
                    rr_ref[d, :, sl] = pre["e_last"][:, sl] * _colsum(dst * st_prev)
                    dst_ref[d, h] = dst * pre["e_last"][:, sl] + _dot_tn(do[:, sl], qe)
                dki, dks = dki_ref[d], dks_ref[d]
                x = pre["q"] * dq_ref[rows, :] - pre["k"] * dki
                y = pre["k"] * dks
                dg = _cumsum_rows(x, not rev) + (_cumsum_rows(y, rev) - y) + rr_ref[d]
                inv_f = jnp.where(pre["f"] > TINY, 1.0 / pre["f"], 0.0)
                u = dg * inv_f - (dki + dks)
                dzf_ref[rows, :] = ((1.0 - lb) * pre["sig"] * pre["sn"] * u).astype(dzf_ref.dtype)
                dlb_ref[...] += _colsum(pre["sn"] * u)
            return carry

        lax.fori_loop(0, cpb, chunk, 0)

    def rspec(cb, rev):
        return pl.BlockSpec((T, D), (lambda i: (i, cb)) if rev else (lambda i: (nb - 1 - i, cb)))

    def sspec(rev):
        shape = (cpb, HEADS, HEAD_DIM, HEAD_DIM)
        return pl.BlockSpec(shape, (lambda i: (i, 0, 0, 0)) if rev else (lambda i: (nb - 1 - i, 0, 0, 0)))

    lbspec = pl.BlockSpec((1, D), lambda i: (0, 0))
    big = jax.ShapeDtypeStruct((m, D), F32)
    row = jax.ShapeDtypeStruct((1, D), F32)
    outs, extra = hosted_call(
        body, exch, name, (nb,),
        [rspec(ZQ, False), rspec(ZFF, False), rspec(ZI, False), rspec(0, False), sspec(False),
         rspec(ZQ, True), rspec(ZFB, True), rspec(ZI, True), rspec(0, True), sspec(True), lbspec, lbspec],
        [rspec(0, False), rspec(0, False), rspec(0, True), rspec(0, True), rspec(0, False), rspec(0, True),
         lbspec, lbspec],
        [big, big, big, big, jax.ShapeDtypeStruct((m, D), BF16), jax.ShapeDtypeStruct((m, D), BF16), row, row],
        [pltpu.VMEM((2, HEADS, HEAD_DIM, HEAD_DIM), F32), pltpu.VMEM((2, C, D), F32), pltpu.VMEM((2, C, D), F32),
         pltpu.VMEM((2, 1, D), F32)],
        [z, z, z, d_o, s_f, z, z, z, d_o, s_b, lb_f, lb_b], ("arbitrary",))
    return outs, extra


def _heads(fn, *arrs):
    res = [fn(*[a[:, h * HEAD_DIM:(h + 1) * HEAD_DIM] for a in arrs]) for h in range(HEADS)]
    return [jnp.concatenate(parts, axis=1) for parts in zip(*res)]


def _hg_post(o_f, o_b, zg, g):
    def head(of, ob, zgh, gh):
        on, _ = _rms(of + ob)
        return (on * gh * _silu(zgh),)
    return _heads(head, o_f, o_b, zg, g)[0]


def _hg_post_bwd(da, o_f, o_b, zg, g):
    def head(dah, of, ob, zgh, gh):
        on, r = _rms(of + ob)
        sg = _silu(zgh)
        d_on = dah * sg
        return _rms_bwd(d_on, on, r, gh), dah * on * gh * _silu_grad(zgh), d_on * on
    d_o, dzg, dg = _heads(head, da, o_f, o_b, zg, g)
    return d_o, dzg, _colsum(dg)


def _sg_parts(zv, ln_g, ln_b):
    vg = _gelu(zv)
    xc = vg - _mean(vg)
    rstd = lax.rsqrt(_mean(xc * xc) + EPS)
    vh = xc * rstd
    return vh, rstd, vh * ln_g + ln_b


def _sg_lane_group(shape):
    return lax.broadcasted_iota(jnp.int32, shape, 1) < SG_GROUP_DIM


def _sg_mix(w, v16, transpose):
    rows = v16.shape[0]
    out = []
    for c in range(rows // SG_CHUNK):
        parts = []
        for j in range(SG_WIDTH // 128):
            vj = v16[c * SG_CHUNK:(c + 1) * SG_CHUNK, j * 128:(j + 1) * 128]
            w0 = w[(2 * j) * SG_CHUNK:(2 * j + 1) * SG_CHUNK]
            w1 = w[(2 * j + 1) * SG_CHUNK:(2 * j + 2) * SG_CHUNK]
            dot = _dot_tn if transpose else _dot
            parts.append(jnp.where(_sg_lane_group((SG_CHUNK, 128)), dot(w0, vj), dot(w1, vj)))
        out.append(jnp.concatenate(parts, axis=1))
    return jnp.concatenate(out, axis=0)


def _sg_fwd(zu, zv, w, bias, ln_g, ln_b):
    _, _, v = _sg_parts(zv, ln_g, ln_b)
    reps = zu.shape[0] // SG_CHUNK
    return _gelu(zu) * (_sg_mix(w, v.astype(BF16), False) + jnp.concatenate([bias] * reps, axis=0))


def _sg_bwd(db, zu, zv, w, bias, ln_g, ln_b):
    vh, rstd, v = _sg_parts(zv, ln_g, ln_b)
    v16 = v.astype(BF16)
    reps = zu.shape[0] // SG_CHUNK
    sg = _sg_mix(w, v16, False) + jnp.concatenate([bias] * reps, axis=0)
    dzu = db * sg * _gelu_grad(zu)
    dsg = db * _gelu(zu)
    dsg16 = dsg.astype(BF16)
    dv = _sg_mix(w, dsg16, True)
    low = _sg_lane_group((SG_CHUNK, 128))
    dw = []
    for g in range(SG_WIDTH // SG_GROUP_DIM):
        j, keep = g // 2, (low if g % 2 == 0 else jnp.logical_not(low))
        acc = jnp.zeros((SG_CHUNK, SG_CHUNK), F32)
        for c in range(reps):
            rows = slice(c * SG_CHUNK, (c + 1) * SG_CHUNK)
            dj = jnp.where(keep, dsg16[rows, j * 128:(j + 1) * 128], jnp.zeros((), BF16))
            acc = acc + _dot_nt(dj, v16[rows, j * 128:(j + 1) * 128])
        dw.append(acc)
    dbias = sum(dsg[c * SG_CHUNK:(c + 1) * SG_CHUNK] for c in range(reps))
    dvh = dv * ln_g
    dvg = rstd * (dvh - _mean(dvh) - vh * _mean(dvh * vh))
    dzuv = jnp.concatenate([dzu, dvg * _gelu_grad(zv)], axis=1)
    return (dzuv, jnp.concatenate(dw, axis=0), dbias, _colsum(dv * vh), _colsum(dv))


def lower_bounds(name, gamma_f, gamma_b):
    def body(gf_ref, gb_ref, lf_ref, lb_ref):
        for g_ref, o_ref in ((gf_ref, lf_ref), (gb_ref, lb_ref)):
            g0, g1 = g_ref[0:1, :], g_ref[1:2, :]
            mx = jnp.maximum(g0, g1)
            e0, e1 = jnp.exp(g0 - mx), jnp.exp(g1 - mx)
            sm0, sm1 = e0 / (e0 + e1), e1 / (e0 + e1)
            o_ref[0:1, :] = sm0 - sm0
            o_ref[1:2, :] = (sm0 + sm1) - sm0
    shp = jax.ShapeDtypeStruct(gamma_f.shape, F32)
    return pl.pallas_call(body, name=name, out_shape=[shp, shp])(gamma_f, gamma_b)


def lower_bounds_bwd(name, gamma_f, gamma_b, dlb_f, dlb_b):
    def body(gf_ref, gb_ref, df_ref, db_ref, of_ref, ob_ref):
        for g_ref, d_ref, o_ref in ((gf_ref, df_ref, of_ref), (gb_ref, db_ref, ob_ref)):
            g0, g1 = g_ref[0:1, :], g_ref[1:2, :]
            mx = jnp.maximum(g0, g1)
            e0, e1 = jnp.exp(g0 - mx), jnp.exp(g1 - mx)
            sm0, sm1 = e0 / (e0 + e1), e1 / (e0 + e1)
            d1 = d_ref[1:2, :] * sm0 * sm1
            o_ref[0:1, :] = -d1
            o_ref[1:2, :] = d1
    shp = jax.ShapeDtypeStruct(gamma_f.shape, F32)
    return pl.pallas_call(body, name=name, out_shape=[shp, shp])(gamma_f, gamma_b, dlb_f, dlb_b)


def _row(a, l):
    return a[l:l + 1]


class LocalPlan:
    def __init__(self, weights):
        self.W = weights
        self.grads = [dict() for _ in range(DEPTH)]

    def exch(self, host):
        return None

    def done(self, host, outs):
        pass


def local_step(x, p, target, S, plan):
    m = x.shape[0]
    lb_f, lb_b = lower_bounds("lower_bounds", S["lb_gamma_fwd"], S["lb_gamma_bwd"])
    saved = []
    for l in range(DEPTH):
        t = f"l{l}_"
        W = plan.W[l]
        tn = (512, 1024)[l]
        g_pre, g_post = _row(S["norm_mix_pre"], l), _row(S["norm_mix_post"], l)
        g_fpre, g_fpost = _row(S["norm_ffn_pre"], l), _row(S["norm_ffn_post"], l)
        hg_g = _row(S["hg_norm"], l)
        sg_w = S["sg_w"][l].reshape(SG_WIDTH // SG_GROUP_DIM * SG_CHUNK, SG_CHUNK).astype(BF16)
        sg_bias = jnp.repeat(S["sg_b"][l].T, SG_GROUP_DIM, axis=1)
        ln_g, ln_b = _row(S["sg_ln_g"], l), _row(S["sg_ln_b"], l)
        lbf, lbb = _row(lb_f, l), _row(lb_b, l)

        (h,) = rowwise(t + "pre_norm", lambda xv, g: (_rms(xv)[0] * g,), m, ins=[(x, D, 0)], consts=[g_pre],
                       outs=[(D, BF16)])
        ex = plan.exch(t + "in_proj")
        z = mm(t + "in_proj", h, W["w_in"], "nn", tn=tn, exch=ex)
        if ex is not None:
            z, extra = z
            plan.done(t + "in_proj", extra)
        (o_f, o_b, s_f, s_b), extra = hgrn_fwd(t + "hgrn_fwd", z, lbf, lbb, exch=plan.exch(t + "hgrn_fwd"))
        plan.done(t + "hgrn_fwd", extra)
        (a_out,) = rowwise(t + "hgrn_post", _hg_post, m, ins=[(o_f, D, 0), (o_b, D, 0), (z, D, ZG)], consts=[hg_g],
                           outs=[(D, BF16)])
        (b_out,) = rowwise(t + "sgu_fwd", _sg_fwd, m, ins=[(z, SG_WIDTH, ZU), (z, SG_WIDTH, ZV)],
                           consts=[sg_w, sg_bias, ln_g, ln_b], outs=[(SG_WIDTH, BF16)])
        pa = mm(t + "proj_a", a_out, W["w_a"], "nn")
        pb = mm(t + "proj_b", b_out, W["w_b"], "nn")
        (merged,) = rowwise(t + "merge", lambda a, b, ga, gb: (_sigmoid(ga) * a + _sigmoid(gb) * b,), m,
                            ins=[(pa, D, 0), (pb, D, 0), (z, D, GA), (z, D, GB)], outs=[(D, BF16)])
        mix = mm(t + "out_proj", merged, W["w_out"], "nn")

        def post_pre(xv, mixv, gp, gf):
            x1 = xv + _rms(mixv)[0] * gp
            return x1, _rms(x1)[0] * gf
        x1, h2 = rowwise(t + "mix_post_ffn_pre", post_pre, m, ins=[(x, D, 0), (mix, D, 0)], consts=[g_post, g_fpre],
                         outs=[(D, F32), (D, BF16)])
        gu = mm(t + "ffn_in", h2, W["w_gu"], "nn", tn=tn)
        (hid,) = rowwise(t + "ffn_act", lambda gt, up: (_silu(gt) * up,), m, ins=[(gu, FFN_PAD, 0), (gu, FFN_PAD, 1)],
                         outs=[(FFN_PAD, BF16)])
        ff = mm(t + "ffn_out", hid, W["w_down"], "nn", tn=tn)
        (x2,) = rowwise(t + "ffn_post", lambda xv, f, g: (xv + _rms(f)[0] * g,), m, ins=[(x1, D, 0), (ff, D, 0)],
                        consts=[g_fpost], outs=[(D, F32)])
        e = mm(t + "ple_proj", (p, l), W["w_ple"], "nn")
        tg = mm(t + "ple_gate", x2, W["w_ple_gate"], "nn")
        (x3,) = rowwise(t + "ple_add", lambda xv, ev, tv: (xv + ev * _sigmoid(tv),), m,
                        ins=[(x2, D, 0), (e, D, 0), (tg, D, 0)], outs=[(D, F32)])
        saved.append(dict(x=x, h=h, z=z, o_f=o_f, o_b=o_b, s_f=s_f, s_b=s_b, a_out=a_out, b_out=b_out, pa=pa, pb=pb,
                          merged=merged, mix=mix, x1=x1, h2=h2, gu=gu, hid=hid, ff=ff, x2=x2, e=e, tg=tg,
                          sg_w=sg_w, sg_bias=sg_bias))
        x = x3

    def loss_fn(y, tv):
        err = y - tv
        return err * (1.0 / D), _colsum(err * err)
    dx, loss_cols = rowwise("loss", loss_fn, m, ins=[(x, D, 0), (target, D, 0)], outs=[(D, F32)], accs=[(1, D)])

    gs = {n: [None] * DEPTH for n in SMALL}
    dlb_f, dlb_b = [None] * DEPTH, [None] * DEPTH

    for l in reversed(range(DEPTH)):
        t = f"l{l}_bwd_"
        sv, W = saved[l], plan.W[l]
        tn = (512, 1024)[l]
        g_pre, g_post = _row(S["norm_mix_pre"], l), _row(S["norm_mix_post"], l)
        g_fpre, g_fpost = _row(S["norm_ffn_pre"], l), _row(S["norm_ffn_post"], l)
        hg_g = _row(S["hg_norm"], l)
        ln_g, ln_b = _row(S["sg_ln_g"], l), _row(S["sg_ln_b"], l)
        lbf, lbb = _row(lb_f, l), _row(lb_b, l)

        def wgrad(nm, tag, a, b, tn_=512):
            plan.grads[l][nm] = mm(tag, a, b, "tn", out_dtype=BF16, tn=tn_)

        def ple_bwd(d3, ev, tv):
            s = _sigmoid(tv)
            return d3 * s, d3 * ev * s * (1.0 - s)
        de, dt = rowwise(t + "ple", ple_bwd, m, ins=[(dx, D, 0), (sv["e"], D, 0), (sv["tg"], D, 0)],
                         outs=[(D, BF16), (D, BF16)])
        wgrad("w_ple", t + "w_ple", (p, l), de)
        wgrad("w_ple_gate", t + "w_ple_gate", sv["x2"], dt)
        dx2p = mm(t + "ple_gate_dx", dt, W["w_ple_gate"], "nt")

        def ffn_post_bwd(d3, d2p, f, g):
            d2 = d3 + d2p
            fh, r = _rms(f)
            return d2, _rms_bwd(d2, fh, r, g), _colsum(d2 * fh)
        dx2, dff, gs["norm_ffn_post"][l] = rowwise(
            t + "ffn_post", ffn_post_bwd, m, ins=[(dx, D, 0), (dx2p, D, 0), (sv["ff"], D, 0)], consts=[g_fpost],
            outs=[(D, F32), (D, BF16)], accs=[(1, D)])
        wgrad("w_down", t + "w_down", sv["hid"], dff, tn)
        dhid = mm(t + "ffn_out_dx", dff, W["w_down"], "nt", tn=tn)

        def act_bwd(dh, gt, up):
            return (jnp.concatenate([dh * up * _silu_grad(gt), dh * _silu(gt)], axis=1),)
        (dgu,) = rowwise(t + "ffn_act", act_bwd, m, ins=[(dhid, FFN_PAD, 0), (sv["gu"], FFN_PAD, 0),
                                                       (sv["gu"], FFN_PAD, 1)], outs=[(2 * FFN_PAD, BF16)], tm=128)
        wgrad("w_gu", t + "w_gu", sv["h2"], dgu, tn)
        dh2 = mm(t + "ffn_in_dx", dgu, W["w_gu"], "nt", tn=tn)

        def pre_post_bwd(d2, dh, x1v, mixv, gf, gp):
            xh, r1 = _rms(x1v)
            d1 = d2 + _rms_bwd(dh, xh, r1, gf)
            mh, rm = _rms(mixv)
            return d1, _rms_bwd(d1, mh, rm, gp), _colsum(dh * xh), _colsum(d1 * mh)
        dx1, dmix, gs["norm_ffn_pre"][l], gs["norm_mix_post"][l] = rowwise(
            t + "mix_post_ffn_pre", pre_post_bwd, m, ins=[(dx2, D, 0), (dh2, D, 0), (sv["x1"], D, 0), (sv["mix"], D, 0)],
            consts=[g_fpre, g_post], outs=[(D, F32), (D, BF16)], accs=[(1, D), (1, D)])
        wgrad("w_out", t + "w_out", sv["merged"], dmix)
        dmerged = mm(t + "out_proj_dx", dmix, W["w_out"], "nt")

        def merge_bwd(dm, a, b, gab):
            sa, sb = _sigmoid(gab[:, :D]), _sigmoid(gab[:, D:])
            dgab = jnp.concatenate([dm * a * sa * (1.0 - sa), dm * b * sb * (1.0 - sb)], axis=1)
            return dm * sa, dm * sb, dgab
        dpa, dpb, dz = rowwise(
            t + "merge", merge_bwd, m, ins=[(dmerged, D, 0), (sv["pa"], D, 0), (sv["pb"], D, 0), (sv["z"], 2 * D, 3)],
            outs=[(D, BF16), (D, BF16)], alias_outs=[(jax.ShapeDtypeStruct((m, N_IN), BF16), 2 * D, 3)])
        wgrad("w_a", t + "w_a", sv["a_out"], dpa)
        wgrad("w_b", t + "w_b", sv["b_out"], dpb)
        da = mm(t + "proj_a_dx", dpa, W["w_a"], "nt")
        db = mm(t + "proj_b_dx", dpb, W["w_b"], "nt")

        dz, dsw, dbias, gs["sg_ln_g"][l], gs["sg_ln_b"][l] = rowwise(
            t + "sgu", _sg_bwd, m, ins=[(db, SG_WIDTH, 0), (sv["z"], SG_WIDTH, ZU), (sv["z"], SG_WIDTH, ZV)],
            consts=[sv["sg_w"], sv["sg_bias"], ln_g, ln_b], alias_outs=[(dz, 2 * SG_WIDTH, 5)],
            accs=[(SG_WIDTH // SG_GROUP_DIM * SG_CHUNK, SG_CHUNK), (SG_CHUNK, SG_WIDTH), (1, SG_WIDTH), (1, SG_WIDTH)])
        gs["sg_w"][l] = dsw.reshape(1, SG_WIDTH // SG_GROUP_DIM, SG_CHUNK, SG_CHUNK)
        gs["sg_b"][l] = dbias.reshape(SG_CHUNK, SG_WIDTH // SG_GROUP_DIM, SG_GROUP_DIM).sum(-1).T[None]

        d_o, dz, gs["hg_norm"][l] = rowwise(
            t + "hgrn_post", _hg_post_bwd, m, ins=[(da, D, 0), (sv["o_f"], D, 0), (sv["o_b"], D, 0), (sv["z"], D, ZG)],
            consts=[hg_g], outs=[(D, BF16)], alias_outs=[(dz, D, ZG)], accs=[(1, D)])
        (dq_f, dv_f, dq_b, dv_b, dzf_f, dzf_b, dlb_f[l], dlb_b[l]), extra = hgrn_bwd(
            t + "hgrn", sv["z"], d_o, sv["s_f"], sv["s_b"], lbf, lbb, exch=plan.exch(t + "hgrn"))
        plan.done(t + "hgrn", extra)

        def combine(dqf, dqb, dvf, dvb, dff_, dfb_, zq):
            return (jnp.concatenate([((dqf + dqb) * _silu_grad(zq)).astype(BF16), dff_, dfb_,
                                     (dvf + dvb).astype(BF16)], axis=1),)
        (dz,) = rowwise(t + "hgrn_combine", combine, m,
                        ins=[(dq_f, D, 0), (dq_b, D, 0), (dv_f, D, 0), (dv_b, D, 0), (dzf_f, D, 0), (dzf_b, D, 0),
                             (sv["z"], D, ZQ)], alias_outs=[(dz, 4 * D, 0)], tm=128)
        wgrad("w_in", t + "w_in", sv["h"], dz, tn)
        ex = plan.exch(t + "in_proj_dx")
        dh = mm(t + "in_proj_dx", dz, W["w_in"], "nt", tn=tn, exch=ex)
        if ex is not None:
            dh, extra = dh
            plan.done(t + "in_proj_dx", extra)

        def pre_bwd(d1, dhv, xv, g):
            xh, r = _rms(xv)
            return d1 + _rms_bwd(dhv, xh, r, g), _colsum(dhv * xh)
        dx, gs["norm_mix_pre"][l] = rowwise(t + "pre_norm", pre_bwd, m, ins=[(dx1, D, 0), (dh, D, 0), (sv["x"], D, 0)],
                                            consts=[g_pre], outs=[(D, F32)], accs=[(1, D)])
        saved[l] = None

    gs["lb_gamma_fwd"], gs["lb_gamma_bwd"] = lower_bounds_bwd(
        "lower_bounds_bwd", S["lb_gamma_fwd"], S["lb_gamma_bwd"], jnp.concatenate(dlb_f, axis=0),
        jnp.concatenate(dlb_b, axis=0))
    small = {n: (g if not isinstance(g, list) else jnp.concatenate(g, axis=0)).reshape(S[n].shape)
             for n, g in gs.items()}
    return loss_cols, dx, small


def cast_pad(name, w, rows_p, cols_p):
    _, r, c = w.shape

    def body(w_ref, o_ref):
        if (rows_p, cols_p) != (r, c):
            o_ref[...] = jnp.zeros(o_ref.shape, BF16)
        o_ref[0:r, 0:c] = w_ref[...].astype(BF16)

    return pl.pallas_call(
        body, name=name, grid=(DEPTH,), in_specs=[pl.BlockSpec((None, r, c), lambda l: (l, 0, 0))],
        out_specs=pl.BlockSpec((None, rows_p, cols_p), lambda l: (l, 0, 0)),
        out_shape=jax.ShapeDtypeStruct((DEPTH, rows_p, cols_p), BF16), compiler_params=_params(("parallel",)),
    )(w)


def _shard_shape(n, shape):
    axis, size, _, _ = LAYOUT[n]
    _, r, c = shape
    return (size, c) if axis == 0 else (r, size)


class DistPlan:
    def __init__(self, shards):
        self.shards = shards
        self.W = [dict() for _ in range(DEPTH)]
        self.grads = [dict() for _ in range(DEPTH)]
        self.slots = [dict() for _ in range(DEPTH)]
        rest = [n for n in BIG if n != "w_in"]
        self.schedule = {
            "l0_in_proj": ("gather", [(0, n) for n in rest]),
            "l0_hgrn_fwd": ("gather", [(1, n) for n in BIG]),
            "l0_bwd_hgrn": ("scatter", [(1, n) for n in BIG] + [(0, n) for n in rest]),
            "l0_bwd_in_proj_dx": ("scatter", [(0, "w_in")]),
        }
        self.pending = {}
        self.done("start", exchange("gather_l0_w_in", self._gather("start", [(0, "w_in")])))

    def _gather(self, host, parts):
        srcs, dsts, items, keys = [], [], [], []
        for layer, n in parts:
            axis, size, dst, base = LAYOUT[n]
            if (layer, dst) not in keys:
                keys.append((layer, dst))
                dsts.append((GATHERED[dst], BF16))
            srcs.append(self.shards[n])
            items.append(("gather", len(srcs) - 1, keys.index((layer, dst)), axis, size, base, layer))
        self.pending[host] = ("gather", keys)
        return Exchange(srcs, dsts, items)

    def _scatter(self, host, parts):
        srcs, src_keys, dsts, items = [], [], [], []
        for layer, n in parts:
            axis, size, dst, base = LAYOUT[n]
            if (layer, dst) not in src_keys:
                src_keys.append((layer, dst))
                srcs.append(self.grads[layer][dst])
            dsts.append(((NDEV,) + _shard_shape(n, self.shards[n].shape), BF16))
            items.append(("scatter", src_keys.index((layer, dst)), len(dsts) - 1, axis, size, base, None))
        self.pending[host] = ("scatter", parts)
        return Exchange(srcs, dsts, items)

    def exch(self, host):
        if host not in self.schedule:
            return None
        kind, parts = self.schedule[host]
        return self._gather(host, parts) if kind == "gather" else self._scatter(host, parts)

    def done(self, host, outs):
        if host not in self.pending:
            return
        kind, keys = self.pending.pop(host)
        for (layer, n), arr in zip(keys, outs):
            (self.W if kind == "gather" else self.slots)[layer][n] = arr


def adam(name, w, m_, v_, tr, g=None, slots=None):
    L, r, c = w.shape
    assert r % tr == 0
    nt = r // tr

    def body(*refs):
        n_g = 1 if slots is None else L
        g_refs = refs[:n_g]
        w_ref, m_ref, v_ref, g_out, d_out, m_out, v_out = refs[n_g:]

        def update(gv):
            m2 = B1 * m_ref[...] + (1.0 - B1) * gv
            v2 = B2 * v_ref[...] + (1.0 - B2) * (gv * gv)
            m_hat = m2 / (1.0 - B1 ** STEP)
            v_hat = v2 / (1.0 - B2 ** STEP)
            g_out[...] = gv
            d_out[...] = -LR * (m_hat / (jnp.sqrt(v_hat) + AEPS) + WD * w_ref[...])
            m_out[...] = m2
            v_out[...] = v2

        if slots is None:
            update(g_refs[0][...])
            return
        for layer, s_ref in enumerate(g_refs):
            @pl.when(pl.program_id(0) == layer)
            def _():
                gv = s_ref[0][:, :c].astype(F32)
                for j in range(1, NDEV):
                    gv = gv + s_ref[j][:, :c].astype(F32)
                update(gv)

    spec = pl.BlockSpec((None, tr, c), lambda l, i: (l, i, 0))
    if slots is None:
        g_arrs, g_specs = [g], [spec]
    else:
        assert L == 2
        g_arrs = list(slots)
        cp = slots[0].shape[2]
        g_specs = [pl.BlockSpec((NDEV, tr, cp), lambda l, i: (0, i * (1 - l) + (nt - 1) * l, 0)),
                   pl.BlockSpec((NDEV, tr, cp), lambda l, i: (0, i * l, 0))]
    shp = jax.ShapeDtypeStruct(w.shape, F32)
    return pl.pallas_call(
        body, name=name, grid=(L, nt), in_specs=g_specs + [spec, spec, spec], out_specs=[spec] * 4,
        out_shape=[shp] * 4, compiler_params=_params(("arbitrary", "arbitrary")),
    )(*g_arrs, w, m_, v_)


def _pack(arrs):
    parts = []
    for a in arrs:
        a2 = a.reshape(-1, D)
        parts.append(jnp.pad(a2, ((0, -a2.shape[0] % 8), (0, 0))))
    return jnp.concatenate(parts, axis=0)


def _unpack(buf, shapes):
    out, off = [], 0
    for s in shapes:
        rows = 1
        for d_ in s:
            rows *= d_
        rows //= D
        out.append(buf[off:off + rows].reshape(s))
        off += rows + (-rows % 8)
    return out


def kernel(x, p, norm_mix_pre, w_in, lb_gamma_fwd, lb_gamma_bwd, hg_norm, sg_w, sg_b, sg_ln_g, sg_ln_b, w_a, w_b, w_out, norm_mix_post, norm_ffn_pre, w_gate, w_up, w_down, norm_ffn_post, w_ple, w_ple_gate, loss_target, m_norm_mix_pre, m_w_in, m_lb_gamma_fwd, m_lb_gamma_bwd, m_hg_norm, m_sg_w, m_sg_b, m_sg_ln_g, m_sg_ln_b, m_w_a, m_w_b, m_w_out, m_norm_mix_post, m_norm_ffn_pre, m_w_gate, m_w_up, m_w_down, m_norm_ffn_post, m_w_ple, m_w_ple_gate, v_norm_mix_pre, v_w_in, v_lb_gamma_fwd, v_lb_gamma_bwd, v_hg_norm, v_sg_w, v_sg_b, v_sg_ln_g, v_sg_ln_b, v_w_a, v_w_b, v_w_out, v_norm_mix_post, v_norm_ffn_pre, v_w_gate, v_w_up, v_w_down, v_norm_ffn_post, v_w_ple, v_w_ple_gate):
    a = dict(zip(INPUTS, (x, p, norm_mix_pre, w_in, lb_gamma_fwd, lb_gamma_bwd, hg_norm, sg_w, sg_b, sg_ln_g, sg_ln_b, w_a, w_b, w_out, norm_mix_post, norm_ffn_pre, w_gate, w_up, w_down, norm_ffn_post, w_ple, w_ple_gate, loss_target, m_norm_mix_pre, m_w_in, m_lb_gamma_fwd, m_lb_gamma_bwd, m_hg_norm, m_sg_w, m_sg_b, m_sg_ln_g, m_sg_ln_b, m_w_a, m_w_b, m_w_out, m_norm_mix_post, m_norm_ffn_pre, m_w_gate, m_w_up, m_w_down, m_norm_ffn_post, m_w_ple, m_w_ple_gate, v_norm_mix_pre, v_w_in, v_lb_gamma_fwd, v_lb_gamma_bwd, v_hg_norm, v_sg_w, v_sg_b, v_sg_ln_g, v_sg_ln_b, v_w_a, v_w_b, v_w_out, v_norm_mix_post, v_norm_ffn_pre, v_w_gate, v_w_up, v_w_down, v_norm_ffn_post, v_w_ple, v_w_ple_gate)))
    m = x.shape[1]

    shards = {n: cast_pad("cast_" + n, a[n], *_shard_shape(n, a[n].shape)) for n in BIG}
    plan = DistPlan(shards)
    loss_cols, dx, gs = local_step(x[0], p[:, 0], loss_target[0], {n: a[n] for n in SMALL}, plan)
    loss = lax.psum(jnp.sum(loss_cols) * (0.5 / D), ("x", "y", "c"))

    small_shapes = [a[n].shape for n in SMALL]
    g_small = allreduce_small("allreduce_small", _pack([gs[n] for n in SMALL]))

    res = {}
    row_tiles = {"w_in": 128, "w_a": 128, "w_b": 512, "w_out": 128, "w_gate": 128, "w_up": 128, "w_down": 88,
                 "w_ple": 256, "w_ple_gate": 128}
    for n in BIG:
        res[n] = adam("adam_" + n, a[n], a["m_" + n], a["v_" + n], row_tiles[n],
                      slots=[plan.slots[l][n] for l in range(DEPTH)])
    packed = [_pack([a[pre + n] for n in SMALL])[None] for pre in ("", "m_", "v_")]
    small_res = adam("adam_small", packed[0], packed[1], packed[2], packed[0].shape[1], g=g_small[None])
    small_res = [_unpack(r_[0], small_shapes) for r_ in small_res]
    for i, n in enumerate(SMALL):
        res[n] = tuple(small_res[k][i] for k in range(4))

    outs = [loss, dx.reshape(1, m, D)]
    for k in range(4):
        outs += [res[n][k] for n in WEIGHTS]
    return tuple(outs)
```

```python
import jax
import jax.numpy as jnp
from jax import lax
from jax.experimental import pallas as pl
from jax.experimental.pallas import tpu as pltpu

F32 = jnp.float32
BF16 = jnp.bfloat16

D = 1024
N_IN = 8192
HEADS = 8
HEAD_DIM = 128
SG_CHUNK = 128
SG_WIDTH = 512
SG_GROUP_DIM = 64
FFN = 2816
PLE_DIM = 256
EPS = 1e-6
DEPTH = 2
ZQ, ZFF, ZFB, ZI, ZG, GA, GB = 0, 1, 2, 3, 4, 6, 7
ZU, ZV = 10, 11

NDEV = 8
FFN_SHARD = FFN // NDEV
FFN_SHARD_PAD = 384
FFN_PAD = NDEV * FFN_SHARD_PAD

LR, B1, B2, AEPS, WD, STEP = 0.001, 0.9, 0.999, 1e-08, 0.01, 10

ROW_TILE = 256
HG_CHUNK = 64
HG_BLOCK_FWD = 256
HG_BLOCK_BWD = 128
EXP_CLAMP = 80.0
TINY = float(jnp.finfo(jnp.float32).tiny)
VMEM_LIMIT = 56 * 1024 * 1024

BIG = ["w_in", "w_a", "w_b", "w_out", "w_gate", "w_up", "w_down", "w_ple", "w_ple_gate"]
SMALL = ["norm_mix_pre", "lb_gamma_fwd", "lb_gamma_bwd", "hg_norm", "sg_w", "sg_b", "sg_ln_g", "sg_ln_b",
         "norm_mix_post", "norm_ffn_pre", "norm_ffn_post"]
WEIGHTS = ["norm_mix_pre", "w_in", "lb_gamma_fwd", "lb_gamma_bwd", "hg_norm", "sg_w", "sg_b", "sg_ln_g", "sg_ln_b",
           "w_a", "w_b", "w_out", "norm_mix_post", "norm_ffn_pre", "w_gate", "w_up", "w_down", "norm_ffn_post",
           "w_ple", "w_ple_gate"]
INPUTS = (["x", "p"] + WEIGHTS + ["loss_target"] + ["m_" + n for n in WEIGHTS] + ["v_" + n for n in WEIGHTS])
LAYOUT = {
    "w_in": (1, 1024, "w_in", 0), "w_a": (0, 128, "w_a", 0), "w_b": (1, 128, "w_b", 0),
    "w_out": (0, 128, "w_out", 0), "w_gate": (1, FFN_SHARD_PAD, "w_gu", 0),
    "w_up": (1, FFN_SHARD_PAD, "w_gu", FFN_PAD), "w_down": (0, FFN_SHARD_PAD, "w_down", 0),
    "w_ple": (1, 128, "w_ple", 0), "w_ple_gate": (0, 128, "w_ple_gate", 0),
}
GATHERED = {"w_in": (D, N_IN), "w_a": (D, D), "w_b": (SG_WIDTH, D), "w_out": (D, D), "w_gu": (D, 2 * FFN_PAD),
            "w_down": (FFN_PAD, D), "w_ple": (PLE_DIM, D), "w_ple_gate": (D, D)}


def _params(sem):
    return pltpu.CompilerParams(dimension_semantics=sem, vmem_limit_bytes=VMEM_LIMIT)


def _dot(a, b):
    return lax.dot_general(a, b, (((1,), (0,)), ((), ())), preferred_element_type=F32)


def _dot_nt(a, b):
    return lax.dot_general(a, b, (((1,), (1,)), ((), ())), preferred_element_type=F32)


def _dot_tn(a, b):
    return lax.dot_general(a, b, (((0,), (0,)), ((), ())), preferred_element_type=F32)


def _sigmoid(x):
    return jax.nn.sigmoid(x)


def _silu(x):
    return x * _sigmoid(x)


def _silu_grad(x):
    s = _sigmoid(x)
    return s * (1.0 + x * (1.0 - s))


def _gelu(x):
    return 0.5 * x * (1.0 + lax.erf(x * 0.7071067811865476))


def _gelu_grad(x):
    return 0.5 * (1.0 + lax.erf(x * 0.7071067811865476)) + x * jnp.exp(-0.5 * x * x) * 0.3989422804014327


def _mean(x):
    return jnp.mean(x, axis=-1, keepdims=True)


def _colsum(x):
    return jnp.sum(x, axis=0, keepdims=True)


def _rms(x):
    r = lax.rsqrt(_mean(x * x) + EPS)
    return x * r, r


def _rms_bwd(dy, xh, r, g):
    dyg = dy * g
    return r * (dyg - xh * _mean(dyg * xh))


MESH = pl.DeviceIdType.MESH
ANY = pl.BlockSpec(memory_space=pl.ANY)


def _slab(ref, axis, start, size):
    idx = [slice(None)] * 2
    idx[axis] = pl.ds(start, size)
    return ref.at[tuple(idx)]


class Exchange:
    def __init__(self, srcs, dsts, items):
        self.srcs, self.dsts, self.items = list(srcs), list(dsts), list(items)

    def specs(self):
        n = len(self.items)
        sems = [pltpu.SemaphoreType.DMA((n * (NDEV - 1),)), pltpu.SemaphoreType.DMA((n * (NDEV - 1),)),
                pltpu.SemaphoreType.DMA((n,))]
        return ([ANY] * len(self.srcs), [ANY] * len(self.dsts),
                [jax.ShapeDtypeStruct(s, dt) for (s, dt) in self.dsts], sems)

    def copies(self, src, dst, send_sem, recv_sem, loc_sem):
        x, y, c = lax.axis_index("x"), lax.axis_index("y"), lax.axis_index("c")
        me = 4 * x + 2 * y + c
        starts, waits = [], []
        for n, (kind, si, di, axis, size, base, layer) in enumerate(self.items):
            def views(to_dev, from_dev):
                if kind == "gather":
                    return (src[si].at[layer],
                            _slab(dst[di], axis, base + pl.multiple_of(from_dev * size, 128), size))
                return _slab(src[si], axis, base + pl.multiple_of(to_dev * size, 128), size), dst[di].at[from_dev]

            s_own, d_own = views(me, me)
            own = pltpu.make_async_copy(s_own, d_own, loc_sem.at[n])
            starts.append(own)
            waits.append(own)
            for k in range(1, NDEV):
                px = 1 - x if k & 4 else x
                py = 1 - y if k & 2 else y
                pc = 1 - c if k & 1 else c
                peer = 4 * px + 2 * py + pc
                s_out, _ = views(peer, me)
                _, d_in = views(me, peer)
                sem = n * (NDEV - 1) + k - 1
                starts.append(pltpu.make_async_remote_copy(s_out, d_own, send_sem.at[sem], recv_sem.at[sem],
                                                           device_id=(px, py, pc), device_id_type=MESH))
                waits.append(pltpu.make_async_remote_copy(s_out, d_in, send_sem.at[sem], recv_sem.at[sem],
                                                          device_id=(px, py, pc), device_id_type=MESH))
        return starts, waits


def exchange(name, exch):
    e_in, e_out, e_shape, e_scr = exch.specs()
    ns, nd = len(e_in), len(e_out)

    def body(*refs):
        starts, waits = exch.copies(refs[:ns], refs[ns:ns + nd], *refs[ns + nd:])
        for cp in starts:
            cp.start()
        for cp in waits:
            cp.wait()

    return pl.pallas_call(body, name=name, in_specs=e_in, out_specs=e_out, out_shape=e_shape, scratch_shapes=e_scr,
                          compiler_params=pltpu.CompilerParams(has_side_effects=True))(*exch.srcs)


def hosted_call(body, exch, name, grid, in_specs, out_specs, out_shape, scratch_shapes, operands, semantics,
                aliases=None):
    aliases = aliases or {}
    if exch is None:
        res = pl.pallas_call(body, name=name, grid=grid, in_specs=in_specs, out_specs=out_specs, out_shape=out_shape,
                             scratch_shapes=scratch_shapes, input_output_aliases=aliases,
                             compiler_params=_params(semantics))(*operands)
        return list(res), []
    n_in, n_out, n_scr = len(in_specs), len(out_specs), len(scratch_shapes)
    e_in, e_out, e_shape, e_scr = exch.specs()
    ns, nd = len(e_in), len(e_out)

    def at_step(last):
        cond = None
        for ax, n in enumerate(grid):
            c = pl.program_id(ax) == (n - 1 if last else 0)
            cond = c if cond is None else jnp.logical_and(cond, c)
        return cond

    def wrapped(*refs):
        ins, src = refs[:n_in], refs[n_in:n_in + ns]
        o0 = n_in + ns
        outs, dst = refs[o0:o0 + n_out], refs[o0 + n_out:o0 + n_out + nd]
        s0 = o0 + n_out + nd
        scr, sems = refs[s0:s0 + n_scr], refs[s0 + n_scr:]

        @pl.when(at_step(False))
        def _():
            for cp in exch.copies(src, dst, *sems)[0]:
                cp.start()

        body(*ins, *outs, *scr)

        @pl.when(at_step(True))
        def _():
            for cp in exch.copies(src, dst, *sems)[1]:
                cp.wait()

    res = pl.pallas_call(
        wrapped, name=name, grid=grid, in_specs=list(in_specs) + e_in, out_specs=list(out_specs) + e_out,
        out_shape=list(out_shape) + e_shape, scratch_shapes=list(scratch_shapes) + e_scr,
        input_output_aliases=aliases,
        compiler_params=pltpu.CompilerParams(dimension_semantics=("arbitrary",) * len(grid),
                                             vmem_limit_bytes=VMEM_LIMIT, has_side_effects=True),
    )(*operands, *exch.srcs)
    return list(res[:n_out]), list(res[n_out:])


def allreduce_small(name, part):
    rows, width = part.shape

    def body(p_ref, o_ref, buf, send_sem, recv_sem):
        x, y, c = lax.axis_index("x"), lax.axis_index("y"), lax.axis_index("c")
        me = 4 * x + 2 * y + c
        buf[me] = p_ref[...]
        waits = []
        for k in range(1, NDEV):
            px = 1 - x if k & 4 else x
            py = 1 - y if k & 2 else y
            pc = 1 - c if k & 1 else c
            peer = 4 * px + 2 * py + pc
            pltpu.make_async_remote_copy(p_ref, buf.at[me], send_sem.at[k - 1], recv_sem.at[k - 1],
                                         device_id=(px, py, pc), device_id_type=MESH).start()
            waits.append(pltpu.make_async_remote_copy(p_ref, buf.at[peer], send_sem.at[k - 1], recv_sem.at[k - 1],
                                                      device_id=(px, py, pc), device_id_type=MESH))
        for w in waits:
            w.wait()
        acc = buf[0]
        for j in range(1, NDEV):
            acc = acc + buf[j]
        o_ref[...] = acc

    vmem = pl.BlockSpec(memory_space=pltpu.VMEM)
    return pl.pallas_call(
        body, name=name, in_specs=[vmem], out_specs=vmem, out_shape=jax.ShapeDtypeStruct((rows, width), F32),
        scratch_shapes=[pltpu.VMEM((NDEV, rows, width), F32), pltpu.SemaphoreType.DMA((NDEV - 1,)),
                        pltpu.SemaphoreType.DMA((NDEV - 1,))],
        compiler_params=pltpu.CompilerParams(vmem_limit_bytes=VMEM_LIMIT, has_side_effects=True),
    )(part)


def rowwise(name, fn, m, ins=(), consts=(), outs=(), alias_outs=(), accs=(), tm=ROW_TILE):
    tm = min(tm, m)
    n_in, n_c, n_o, n_al, n_ac = len(ins), len(consts), len(outs), len(alias_outs), len(accs)
    held = [a for (a, _, _) in alias_outs if not isinstance(a, jax.ShapeDtypeStruct)]
    n_held = len(held)

    def body(*refs):
        in_refs = refs[:n_in + n_c]
        out_refs = refs[n_in + n_c + n_held:]
        vals = fn(*[r[...] for r in in_refs])
        if not isinstance(vals, (tuple, list)):
            vals = (vals,)
        for r, v in zip(out_refs[:n_o + n_al], vals[:n_o + n_al]):
            r[...] = v.astype(r.dtype)
        if n_ac:
            acc_refs = out_refs[n_o + n_al:]

            @pl.when(pl.program_id(0) == 0)
            def _():
                for r in acc_refs:
                    r[...] = jnp.zeros(r.shape, F32)

            for r, v in zip(acc_refs, vals[n_o + n_al:]):
                r[...] += v

    def col(cb):
        return lambda i: (i, cb)

    in_specs = [pl.BlockSpec((tm, w), col(cb)) for (_, w, cb) in ins]
    in_specs += [pl.BlockSpec(c.shape, lambda i, nd=c.ndim: (0,) * nd) for c in consts]
    in_specs += [ANY for _ in held]
    out_shape = [jax.ShapeDtypeStruct((m, w), dt) for (w, dt) in outs]
    out_specs = [pl.BlockSpec((tm, w), col(0)) for (w, _) in outs]
    out_shape += [jax.ShapeDtypeStruct(a.shape, a.dtype) for (a, _, _) in alias_outs]
    out_specs += [pl.BlockSpec((tm, w), col(cb)) for (_, w, cb) in alias_outs]
    out_shape += [jax.ShapeDtypeStruct(s, F32) for s in accs]
    out_specs += [pl.BlockSpec(s, lambda i: (0, 0)) for s in accs]
    aliases, k_in = {}, n_in + n_c
    for k, (a, _, _) in enumerate(alias_outs):
        if not isinstance(a, jax.ShapeDtypeStruct):
            aliases[k_in] = n_o + k
            k_in += 1
    return pl.pallas_call(
        body, name=name, grid=(m // tm,), in_specs=in_specs, out_specs=out_specs, out_shape=out_shape,
        input_output_aliases=aliases,
        compiler_params=_params(("arbitrary",) if n_ac else ("parallel",)),
    )(*[a for (a, _, _) in ins], *consts, *held)


def _operand(arr, bshape, imap):
    if isinstance(arr, tuple):
        arr, lead = arr
        return arr, pl.BlockSpec((None,) + bshape, lambda *g: (lead,) + imap(*g))
    return arr, pl.BlockSpec(bshape, imap)


def _shape2(arr):
    return arr[0].shape[1:] if isinstance(arr, tuple) else arr.shape


def mm(name, a, b, mode, out_dtype=F32, tm=1024, tn=1024, tk=1024, exch=None):
    sa, sb = _shape2(a), _shape2(b)
    if mode == "nn":
        (M, K), N = sa, sb[1]
    elif mode == "nt":
        (M, K), N = sa, sb[0]
    else:
        (K, M), N = sa, sb[1]
    tm, tn, tk = min(tm, M), min(tn, N), min(tk, K)
    assert M % tm == 0 and N % tn == 0 and K % tk == 0, (name, M, N, K)
    nk = K // tk
    if mode == "nn":
        a_arr, a_spec = _operand(a, (tm, tk), lambda i, j, k: (i, k))
        b_arr, b_spec = _operand(b, (tk, tn), lambda i, j, k: (k, j))
        dot = _dot
    elif mode == "nt":
        a_arr, a_spec = _operand(a, (tm, tk), lambda i, j, k: (i, k))
        b_arr, b_spec = _operand(b, (tn, tk), lambda i, j, k: (j, k))
        dot = _dot_nt
    else:
        a_arr, a_spec = _operand(a, (tk, tm), lambda i, j, k: (k, i))
        b_arr, b_spec = _operand(b, (tk, tn), lambda i, j, k: (k, j))
        dot = _dot_tn

    def body(a_ref, b_ref, o_ref, *acc):
        part = dot(a_ref[...].astype(BF16), b_ref[...].astype(BF16))
        if nk == 1:
            o_ref[...] = part.astype(o_ref.dtype)
            return
        acc_ref, k = acc[0], pl.program_id(2)

        @pl.when(k == 0)
        def _():
            acc_ref[...] = part

        @pl.when(k > 0)
        def _():
            acc_ref[...] += part

        @pl.when(k == nk - 1)
        def _():
            o_ref[...] = acc_ref[...].astype(o_ref.dtype)

    outs, extra = hosted_call(
        body, exch, name, (M // tm, N // tn, nk), [a_spec, b_spec], [pl.BlockSpec((tm, tn), lambda i, j, k: (i, j))],
        [jax.ShapeDtypeStruct((M, N), out_dtype)], [pltpu.VMEM((tm, tn), F32)] if nk > 1 else [], [a_arr, b_arr],
        ("parallel", "parallel", "arbitrary"))
    return outs[0] if exch is None else (outs[0], extra)


def _cumsum_rows(x):
    n = x.shape[0]
    tri = (lax.broadcasted_iota(jnp.int32, (n, n), 1) <= lax.broadcasted_iota(jnp.int32, (n, n), 0)).astype(BF16)
    hi = x.astype(BF16)
    rest = x - hi.astype(F32)
    mid = rest.astype(BF16)
    lo = (rest - mid.astype(F32)).astype(BF16)
    return _dot(tri, hi) + _dot(tri, mid) + _dot(tri, lo)


def _hg_prep(zq, zf, lb, reverse):
    n = zq.shape[0]
    q = _silu(zq)
    sig = _sigmoid(zf)
    sn = 1.0 - sig
    f = lb + (1.0 - lb) * sig
    k = (1.0 - lb) * sn
    g = jnp.log(jnp.maximum(f, TINY))
    b = _cumsum_rows(g)
    if reverse:
        b = b[n - 1:n] - b + g
    b_last = b[0:1] if reverse else b[n - 1:n]
    b_ref = b[n // 2:n // 2 + 1]
    e1 = jnp.exp(b)
    e2 = jnp.exp(jnp.clip(b - b_ref, -EXP_CLAMP, EXP_CLAMP))
    e3 = jnp.exp(jnp.clip(b_ref - b, -EXP_CLAMP, EXP_CLAMP))
    e4 = jnp.exp(b_last - b)
    return dict(q=q, k=k, sig=sig, sn=sn, f=f, e1=e1, e2=e2, e3=e3, e4=e4, e_last=jnp.exp(b_last),
                qe=(q * e1).astype(BF16), qt=(q * e2).astype(BF16), kt=(k * e3).astype(BF16),
                ks=(k * e4).astype(BF16))


def _hg_mask(n, reverse):
    t = lax.broadcasted_iota(jnp.int32, (n, n), 0)
    s = lax.broadcasted_iota(jnp.int32, (n, n), 1)
    return (s >= t) if reverse else (s <= t)


def hgrn_fwd(name, z, lb_f, lb_b, exch=None):
    m = z.shape[0]
    C, T = HG_CHUNK, min(HG_BLOCK_FWD, m)
    nb, cpb = m // T, T // C

    def body(zq_f, zf_f, zi_f, zq_b, zf_b, zi_b, lbf_ref, lbb_ref, of_ref, ob_ref, sf_ref, sb_ref, st_ref):
        @pl.when(pl.program_id(0) == 0)
        def _():
            st_ref[...] = jnp.zeros(st_ref.shape, F32)

        dirs = ((zq_f, zf_f, zi_f, lbf_ref, of_ref, sf_ref), (zq_b, zf_b, zi_b, lbb_ref, ob_ref, sb_ref))

        def chunk(ci, carry):
            for d, (zq, zf, zi, lb_ref, o_ref, s_ref) in enumerate(dirs):
                cc = ci if d == 0 else cpb - 1 - ci
                rows = pl.ds(pl.multiple_of(cc * C, C), C)
                pre = _hg_prep(zq[rows, :], zf[rows, :], lb_ref[...], d == 1)
                v = zi[rows, :].astype(BF16)
                mask = _hg_mask(C, d == 1)
                for h in range(HEADS):
                    sl = slice(h * HEAD_DIM, (h + 1) * HEAD_DIM)
                    st = st_ref[d, h]
                    s_ref[cc, h] = st
                    a = jnp.where(mask, _dot_nt(pre["qt"][:, sl], pre["kt"][:, sl]), 0.0)
                    o_ref[rows, sl] = _dot_nt(pre["qe"][:, sl], st.astype(BF16)) + _dot(a.astype(BF16), v[:, sl])
                    st_ref[d, h] = st * pre["e_last"][:, sl] + _dot_tn(v[:, sl], pre["ks"][:, sl])
            return carry

        lax.fori_loop(0, cpb, chunk, 0)

    def zspec(cb, rev):
        return pl.BlockSpec((T, D), (lambda i: (nb - 1 - i, cb)) if rev else (lambda i: (i, cb)))

    def sspec(rev):
        shape = (cpb, HEADS, HEAD_DIM, HEAD_DIM)
        return pl.BlockSpec(shape, (lambda i: (nb - 1 - i, 0, 0, 0)) if rev else (lambda i: (i, 0, 0, 0)))

    lbspec = pl.BlockSpec((1, D), lambda i: (0, 0))
    states = jax.ShapeDtypeStruct((m // C, HEADS, HEAD_DIM, HEAD_DIM), F32)
    outs, extra = hosted_call(
        body, exch, name, (nb,),
        [zspec(ZQ, False), zspec(ZFF, False), zspec(ZI, False), zspec(ZQ, True), zspec(ZFB, True), zspec(ZI, True),
         lbspec, lbspec],
        [zspec(0, False), zspec(0, True), sspec(False), sspec(True)],
        [jax.ShapeDtypeStruct((m, D), F32), jax.ShapeDtypeStruct((m, D), F32), states, states],
        [pltpu.VMEM((2, HEADS, HEAD_DIM, HEAD_DIM), F32)], [z, z, z, z, z, z, lb_f, lb_b], ("arbitrary",))
    return outs, extra


def hgrn_bwd(name, z, d_o, s_f, s_b, lb_f, lb_b, exch=None):
    m = z.shape[0]
    C, T = HG_CHUNK, min(HG_BLOCK_BWD, m)
    nb, cpb = m // T, T // C

    def body(zq_f, zf_f, zi_f, do_f, sf_ref, zq_b, zf_b, zi_b, do_b, sb_ref, lbf_ref, lbb_ref,
             dqf_ref, dvf_ref, dqb_ref, dvb_ref, dzf_f, dzf_b, dlbf_ref, dlbb_ref,
             dst_ref, dki_ref, dks_ref, rr_ref, dqs_ref):
        @pl.when(pl.program_id(0) == 0)
        def _():
            dst_ref[...] = jnp.zeros(dst_ref.shape, F32)
            dlbf_ref[...] = jnp.zeros(dlbf_ref.shape, F32)
            dlbb_ref[...] = jnp.zeros(dlbb_ref.shape, F32)

        dirs = ((zq_f, zf_f, zi_f, do_f, sf_ref, lbf_ref, dqf_ref, dvf_ref, dzf_f, dlbf_ref),
                (zq_b, zf_b, zi_b, do_b, sb_ref, lbb_ref, dqb_ref, dvb_ref, dzf_b, dlbb_ref))

        def chunk(ci, carry):
            for d, (zq, zf, zi, do_ref, s_ref, lb_ref, dq_ref, dv_ref, dzf_ref, dlb_ref) in enumerate(dirs):
                rev = d == 1
                cc = cpb - 1 - ci if d == 0 else ci
                rows = pl.ds(pl.multiple_of(cc * C, C), C)
                lb = lb_ref[...]
                pre = _hg_prep(zq[rows, :], zf[rows, :], lb, rev)
                v = zi[rows, :].astype(BF16)
                do = do_ref[rows, :]
                mask = _hg_mask(C, rev)
                for h in range(HEADS):
                    sl = slice(h * HEAD_DIM, (h + 1) * HEAD_DIM)
                    st_prev = s_ref[cc, h]
                    dst = dst_ref[d, h]
                    dst16 = dst.astype(BF16)
                    qt, kt, ks, qe = pre["qt"][:, sl], pre["kt"][:, sl], pre["ks"][:, sl], pre["qe"][:, sl]
                    a = jnp.where(mask, _dot_nt(qt, kt), 0.0).astype(BF16)
                    da = jnp.where(mask, _dot_nt(do[:, sl], v[:, sl]), 0.0).astype(BF16)
                    dq = (_dot(da, kt) * pre["e2"][:, sl]
                          + _dot(do[:, sl], st_prev.astype(BF16)) * pre["e1"][:, sl])
                    dqs_ref[d, :, sl] = dq
                    dq_ref[rows, sl] = dq.astype(dq_ref.dtype)
                    dki_ref[d, :, sl] = _dot_tn(da, qt) * pre["e3"][:, sl]
                    dks_ref[d, :, sl] = _dot(v[:, sl], dst16) * pre["e4"][:, sl]
                    dv_ref[rows, sl] = (_dot_tn(a, do[:, sl]) + _dot_nt(ks, dst16)).astype(dv_ref.dtype)
                    rr_ref[d, :, sl] = pre["e_last"][:, sl] * _colsum(dst * st_prev)
                    dst_ref[d, h] = dst * pre["e_last"][:, sl] + _dot_tn(do[:, sl], qe)
                dki, dks = dki_ref[d], dks_ref[d]
                x = pre["q"] * dqs_ref[d] - pre["k"] * dki
                y = pre["k"] * dks
                if rev:
                    dg = _cumsum_rows(x - y) + _colsum(y) + rr_ref[d]
                else:
                    dg = _cumsum_rows(y - x) + (x - y) + _colsum(x) + rr_ref[d]
                inv_f = jnp.where(pre["f"] > TINY, 1.0 / pre["f"], 0.0)
                u = dg * inv_f - (dki + dks)
                dzf_ref[rows, :] = ((1.0 - lb) * pre["sig"] * pre["sn"] * u).astype(dzf_ref.dtype)
                dlb_ref[...] += _colsum(pre["sn"] * u)
            return carry

        lax.fori_loop(0, cpb, chunk, 0)

    def rspec(cb, rev):
        return pl.BlockSpec((T, D), (lambda i: (i, cb)) if rev else (lambda i: (nb - 1 - i, cb)))

    def sspec(rev):
        shape = (cpb, HEADS, HEAD_DIM, HEAD_DIM)
        return pl.BlockSpec(shape, (lambda i: (i, 0, 0, 0)) if rev else (lambda i: (nb - 1 - i, 0, 0, 0)))

    lbspec = pl.BlockSpec((1, D), lambda i: (0, 0))
    half = jax.ShapeDtypeStruct((m, D), BF16)
    row = jax.ShapeDtypeStruct((1, D), F32)
    outs, extra = hosted_call(
        body, exch, name, (nb,),
        [rspec(ZQ, False), rspec(ZFF, False), rspec(ZI, False), rspec(0, False), sspec(False),
         rspec(ZQ, True), rspec(ZFB, True), rspec(ZI, True), rspec(0, True), sspec(True), lbspec, lbspec],
        [rspec(0, False), rspec(0, False), rspec(0, True), rspec(0, True), rspec(0, False), rspec(0, True),
         lbspec, lbspec],
        [half, half, half, half, half, half, row, row],
        [pltpu.VMEM((2, HEADS, HEAD_DIM, HEAD_DIM), F32), pltpu.VMEM((2, C, D), F32), pltpu.VMEM((2, C, D), F32),
         pltpu.VMEM((2, 1, D), F32), pltpu.VMEM((2, C, D), F32)],
        [z, z, z, d_o, s_f, z, z, z, d_o, s_b, lb_f, lb_b], ("arbitrary",))
    return outs, extra


def _heads(fn, *arrs):
    res = [fn(*[a[:, h * HEAD_DIM:(h + 1) * HEAD_DIM] for a in arrs]) for h in range(HEADS)]
    return [jnp.concatenate(parts, axis=1) for parts in zip(*res)]


def _hg_post(o_f, o_b, zg, g):
    def head(of, ob, zgh, gh):
        on, _ = _rms(of + ob)
        return (on * gh * _silu(zgh),)
    return _heads(head, o_f, o_b, zg, g)[0]


def _hg_post_bwd(da, o_f, o_b, zg, g):
    def head(dah, of, ob, zgh, gh):
        on, r = _rms(of + ob)
        sg = _silu(zgh)
        d_on = dah * sg
        return _rms_bwd(d_on, on, r, gh), dah * on * gh * _silu_grad(zgh), d_on * on
    d_o, dzg, dg = _heads(head, da, o_f, o_b, zg, g)
    return d_o, dzg, _colsum(dg)


def _sg_parts(zv, ln_g, ln_b):
    vg = _gelu(zv)
    xc = vg - _mean(vg)
    rstd = lax.rsqrt(_mean(xc * xc) + EPS)
    vh = xc * rstd
    return vh, rstd, vh * ln_g + ln_b


def _sg_lane_group(shape):
    return lax.broadcasted_iota(jnp.int32, shape, 1) < SG_GROUP_DIM


def _sg_mix(w, v16, transpose):
    rows = v16.shape[0]
    out = []
    for c in range(rows // SG_CHUNK):
        parts = []
        for j in range(SG_WIDTH // 128):
            vj = v16[c * SG_CHUNK:(c + 1) * SG_CHUNK, j * 128:(j + 1) * 128]
            w0 = w[(2 * j) * SG_CHUNK:(2 * j + 1) * SG_CHUNK]
            w1 = w[(2 * j + 1) * SG_CHUNK:(2 * j + 2) * SG_CHUNK]
            dot = _dot_tn if transpose else _dot
            parts.append(jnp.where(_sg_lane_group((SG_CHUNK, 128)), dot(w0, vj), dot(w1, vj)))
        out.append(jnp.concatenate(parts, axis=1))
    return jnp.concatenate(out, axis=0)


def _sg_fwd(zu, zv, w, bias, ln_g, ln_b):
    _, _, v = _sg_parts(zv, ln_g, ln_b)
    reps = zu.shape[0] // SG_CHUNK
    return _gelu(zu) * (_sg_mix(w, v.astype(BF16), False) + jnp.concatenate([bias] * reps, axis=0))


def _sg_bwd(db, zu, zv, w, bias, ln_g, ln_b):
    vh, rstd, v = _sg_parts(zv, ln_g, ln_b)
    v16 = v.astype(BF16)
    reps = zu.shape[0] // SG_CHUNK
    sg = _sg_mix(w, v16, False) + jnp.concatenate([bias] * reps, axis=0)
    dzu = db * sg * _gelu_grad(zu)
    dsg = db * _gelu(zu)
    dsg16 = dsg.astype(BF16)
    dv = _sg_mix(w, dsg16, True)
    low = _sg_lane_group((SG_CHUNK, 128))
    dw = []
    for g in range(SG_WIDTH // SG_GROUP_DIM):
        j, keep = g // 2, (low if g % 2 == 0 else jnp.logical_not(low))
        acc = jnp.zeros((SG_CHUNK, SG_CHUNK), F32)
        for c in range(reps):
            rows = slice(c * SG_CHUNK, (c + 1) * SG_CHUNK)
            dj = jnp.where(keep, dsg16[rows, j * 128:(j + 1) * 128], jnp.zeros((), BF16))
            acc = acc + _dot_nt(dj, v16[rows, j * 128:(j + 1) * 128])
        dw.append(acc)
    dbias = sum(dsg[c * SG_CHUNK:(c + 1) * SG_CHUNK] for c in range(reps))
    dvh = dv * ln_g
    dvg = rstd * (dvh - _mean(dvh) - vh * _mean(dvh * vh))
    dzuv = jnp.concatenate([dzu, dvg * _gelu_grad(zv)], axis=1)
    return (dzuv, jnp.concatenate(dw, axis=0), dbias, _colsum(dv * vh), _colsum(dv))


def lower_bounds(name, gamma_f, gamma_b):
    def body(gf_ref, gb_ref, lf_ref, lb_ref):
        for g_ref, o_ref in ((gf_ref, lf_ref), (gb_ref, lb_ref)):
            g0, g1 = g_ref[0:1, :], g_ref[1:2, :]
            mx = jnp.maximum(g0, g1)
            e0, e1 = jnp.exp(g0 - mx), jnp.exp(g1 - mx)
            sm0, sm1 = e0 / (e0 + e1), e1 / (e0 + e1)
            o_ref[0:1, :] = sm0 - sm0
            o_ref[1:2, :] = (sm0 + sm1) - sm0
    shp = jax.ShapeDtypeStruct(gamma_f.shape, F32)
    return pl.pallas_call(body, name=name, out_shape=[shp, shp])(gamma_f, gamma_b)


def lower_bounds_bwd(name, gamma_f, gamma_b, dlb_f, dlb_b):
    def body(gf_ref, gb_ref, df_ref, db_ref, of_ref, ob_ref):
        for g_ref, d_ref, o_ref in ((gf_ref, df_ref, of_ref), (gb_ref, db_ref, ob_ref)):
            g0, g1 = g_ref[0:1, :], g_ref[1:2, :]
            mx = jnp.maximum(g0, g1)
            e0, e1 = jnp.exp(g0 - mx), jnp.exp(g1 - mx)
            sm0, sm1 = e0 / (e0 + e1), e1 / (e0 + e1)
            d1 = d_ref[1:2, :] * sm0 * sm1
            o_ref[0:1, :] = -d1
            o_ref[1:2, :] = d1
    shp = jax.ShapeDtypeStruct(gamma_f.shape, F32)
    return pl.pallas_call(body, name=name, out_shape=[shp, shp])(gamma_f, gamma_b, dlb_f, dlb_b)


def _row(a, l):
    return a[l:l + 1]


class LocalPlan:
    def __init__(self, weights):
        self.W = weights
        self.grads = [dict() for _ in range(DEPTH)]

    def exch(self, host):
        return None

    def done(self, host, outs):
        pass


def local_step(x, p, target, S, plan):
    m = x.shape[0]
    lb_f, lb_b = lower_bounds("lower_bounds", S["lb_gamma_fwd"], S["lb_gamma_bwd"])
    saved = []
    for l in range(DEPTH):
        t = f"l{l}_"
        W = plan.W[l]
        tm = (1024, 2048)[l]
        g_pre, g_post = _row(S["norm_mix_pre"], l), _row(S["norm_mix_post"], l)
        g_fpre, g_fpost = _row(S["norm_ffn_pre"], l), _row(S["norm_ffn_post"], l)
        hg_g = _row(S["hg_norm"], l)
        sg_w = S["sg_w"][l].reshape(SG_WIDTH // SG_GROUP_DIM * SG_CHUNK, SG_CHUNK).astype(BF16)
        sg_bias = jnp.repeat(S["sg_b"][l].T, SG_GROUP_DIM, axis=1)
        ln_g, ln_b = _row(S["sg_ln_g"], l), _row(S["sg_ln_b"], l)
        lbf, lbb = _row(lb_f, l), _row(lb_b, l)

        (h,) = rowwise(t + "pre_norm", lambda xv, g: (_rms(xv)[0] * g,), m, ins=[(x, D, 0)], consts=[g_pre],
                       outs=[(D, BF16)])
        ex = plan.exch(t + "in_proj")
        z = mm(t + "in_proj", h, W["w_in"], "nn", tm=tm, exch=ex)
        if ex is not None:
            z, extra = z
            plan.done(t + "in_proj", extra)
        (o_f, o_b, s_f, s_b), extra = hgrn_fwd(t + "hgrn_fwd", z, lbf, lbb, exch=plan.exch(t + "hgrn_fwd"))
        plan.done(t + "hgrn_fwd", extra)
        (a_out,) = rowwise(t + "hgrn_post", _hg_post, m, ins=[(o_f, D, 0), (o_b, D, 0), (z, D, ZG)], consts=[hg_g],
                           outs=[(D, BF16)])
        (b_out,) = rowwise(t + "sgu_fwd", _sg_fwd, m, ins=[(z, SG_WIDTH, ZU), (z, SG_WIDTH, ZV)],
                           consts=[sg_w, sg_bias, ln_g, ln_b], outs=[(SG_WIDTH, BF16)])
        pa = mm(t + "proj_a", a_out, W["w_a"], "nn", BF16)
        pb = mm(t + "proj_b", b_out, W["w_b"], "nn", BF16)
        (merged,) = rowwise(t + "merge", lambda a, b, ga, gb: (_sigmoid(ga) * a + _sigmoid(gb) * b,), m,
                            ins=[(pa, D, 0), (pb, D, 0), (z, D, GA), (z, D, GB)], outs=[(D, BF16)])
        mix = mm(t + "out_proj", merged, W["w_out"], "nn")

        def post_pre(xv, mixv, gp, gf):
            x1 = xv + _rms(mixv)[0] * gp
            return x1, _rms(x1)[0] * gf
        x1, h2 = rowwise(t + "mix_post_ffn_pre", post_pre, m, ins=[(x, D, 0), (mix, D, 0)], consts=[g_post, g_fpre],
                         outs=[(D, F32), (D, BF16)])
        gu = mm(t + "ffn_in", h2, W["w_gu"], "nn", BF16, tm=tm)
        (hid,) = rowwise(t + "ffn_act", lambda gt, up: (_silu(gt.astype(F32)) * up,), m, ins=[(gu, FFN_PAD, 0), (gu, FFN_PAD, 1)],
                         outs=[(FFN_PAD, BF16)])
        ff = mm(t + "ffn_out", hid, W["w_down"], "nn", tm=tm)
        (x2,) = rowwise(t + "ffn_post", lambda xv, f, g: (xv + _rms(f)[0] * g,), m, ins=[(x1, D, 0), (ff, D, 0)],
                        consts=[g_fpost], outs=[(D, F32)])
        e = mm(t + "ple_proj", (p, l), W["w_ple"], "nn")
        tg = mm(t + "ple_gate", x2, W["w_ple_gate"], "nn")
        (x3,) = rowwise(t + "ple_add", lambda xv, ev, tv: (xv + ev * _sigmoid(tv),), m,
                        ins=[(x2, D, 0), (e, D, 0), (tg, D, 0)], outs=[(D, F32)])
        saved.append(dict(x=x, h=h, z=z, o_f=o_f, o_b=o_b, s_f=s_f, s_b=s_b, a_out=a_out, b_out=b_out, pa=pa, pb=pb,
                          merged=merged, mix=mix, x1=x1, h2=h2, gu=gu, hid=hid, ff=ff, x2=x2, e=e, tg=tg,
                          sg_w=sg_w, sg_bias=sg_bias))
        x = x3

    def loss_fn(y, tv):
        err = y - tv
        return err * (1.0 / D), _colsum(err * err)
    dx, loss_cols = rowwise("loss", loss_fn, m, ins=[(x, D, 0), (target, D, 0)], outs=[(D, F32)], accs=[(1, D)])

    gs = {n: [None] * DEPTH for n in SMALL}
    dlb_f, dlb_b = [None] * DEPTH, [None] * DEPTH

    for l in reversed(range(DEPTH)):
        t = f"l{l}_bwd_"
        sv, W = saved[l], plan.W[l]
        tm, tk = (1024, 2048)[l], (1024, 2048)[l]
        g_pre, g_post = _row(S["norm_mix_pre"], l), _row(S["norm_mix_post"], l)
        g_fpre, g_fpost = _row(S["norm_ffn_pre"], l), _row(S["norm_ffn_post"], l)
        hg_g = _row(S["hg_norm"], l)
        ln_g, ln_b = _row(S["sg_ln_g"], l), _row(S["sg_ln_b"], l)
        lbf, lbb = _row(lb_f, l), _row(lb_b, l)

        def wgrad(nm, tag, a, b):
            plan.grads[l][nm] = mm(tag, a, b, "tn", BF16, tk=tk)

        def ple_bwd(d3, ev, tv):
            s = _sigmoid(tv)
            return d3 * s, d3 * ev * s * (1.0 - s)
        de, dt = rowwise(t + "ple", ple_bwd, m, ins=[(dx, D, 0), (sv["e"], D, 0), (sv["tg"], D, 0)],
                         outs=[(D, BF16), (D, BF16)])
        wgrad("w_ple", t + "w_ple", (p, l), de)
        wgrad("w_ple_gate", t + "w_ple_gate", sv["x2"], dt)
        dx2p = mm(t + "ple_gate_dx", dt, W["w_ple_gate"], "nt")

        def ffn_post_bwd(d3, d2p, f, g):
            d2 = d3 + d2p
            fh, r = _rms(f)
            return d2, _rms_bwd(d2, fh, r, g), _colsum(d2 * fh)
        dx2, dff, gs["norm_ffn_post"][l] = rowwise(
            t + "ffn_post", ffn_post_bwd, m, ins=[(dx, D, 0), (dx2p, D, 0), (sv["ff"], D, 0)], consts=[g_fpost],
            outs=[(D, F32), (D, BF16)], accs=[(1, D)])
        wgrad("w_down", t + "w_down", sv["hid"], dff)
        dhid = mm(t + "ffn_out_dx", dff, W["w_down"], "nt", BF16, tm=tm)

        def act_bwd(dh, gt, up):
            dh, gt = dh.astype(F32), gt.astype(F32)
            return (jnp.concatenate([dh * up * _silu_grad(gt), dh * _silu(gt)], axis=1),)
        (dgu,) = rowwise(t + "ffn_act", act_bwd, m, ins=[(dhid, FFN_PAD, 0), (sv["gu"], FFN_PAD, 0),
                                                       (sv["gu"], FFN_PAD, 1)], outs=[(2 * FFN_PAD, BF16)], tm=128)
        wgrad("w_gu", t + "w_gu", sv["h2"], dgu)
        dh2 = mm(t + "ffn_in_dx", dgu, W["w_gu"], "nt", tm=tm)

        def pre_post_bwd(d2, dh, x1v, mixv, gf, gp):
            xh, r1 = _rms(x1v)
            d1 = d2 + _rms_bwd(dh, xh, r1, gf)
            mh, rm = _rms(mixv)
            return d1, _rms_bwd(d1, mh, rm, gp), _colsum(dh * xh), _colsum(d1 * mh)
        dx1, dmix, gs["norm_ffn_pre"][l], gs["norm_mix_post"][l] = rowwise(
            t + "mix_post_ffn_pre", pre_post_bwd, m, ins=[(dx2, D, 0), (dh2, D, 0), (sv["x1"], D, 0), (sv["mix"], D, 0)],
            consts=[g_fpre, g_post], outs=[(D, F32), (D, BF16)], accs=[(1, D), (1, D)])
        wgrad("w_out", t + "w_out", sv["merged"], dmix)
        dmerged = mm(t + "out_proj_dx", dmix, W["w_out"], "nt", BF16)

        def merge_bwd(dm, a, b, gab):
            dm = dm.astype(F32)
            sa, sb = _sigmoid(gab[:, :D]), _sigmoid(gab[:, D:])
            dgab = jnp.concatenate([dm * a * sa * (1.0 - sa), dm * b * sb * (1.0 - sb)], axis=1)
            return dm * sa, dm * sb, dgab
        dpa, dpb, dz = rowwise(
            t + "merge", merge_bwd, m, ins=[(dmerged, D, 0), (sv["pa"], D, 0), (sv["pb"], D, 0), (sv["z"], 2 * D, 3)],
            outs=[(D, BF16), (D, BF16)], alias_outs=[(jax.ShapeDtypeStruct((m, N_IN), BF16), 2 * D, 3)])
        wgrad("w_a", t + "w_a", sv["a_out"], dpa)
        wgrad("w_b", t + "w_b", sv["b_out"], dpb)
        da = mm(t + "proj_a_dx", dpa, W["w_a"], "nt")
        db = mm(t + "proj_b_dx", dpb, W["w_b"], "nt")

        dz, dsw, dbias, gs["sg_ln_g"][l], gs["sg_ln_b"][l] = rowwise(
            t + "sgu", _sg_bwd, m, ins=[(db, SG_WIDTH, 0), (sv["z"], SG_WIDTH, ZU), (sv["z"], SG_WIDTH, ZV)],
            consts=[sv["sg_w"], sv["sg_bias"], ln_g, ln_b], alias_outs=[(dz, 2 * SG_WIDTH, 5)],
            accs=[(SG_WIDTH // SG_GROUP_DIM * SG_CHUNK, SG_CHUNK), (SG_CHUNK, SG_WIDTH), (1, SG_WIDTH), (1, SG_WIDTH)])
        gs["sg_w"][l] = dsw.reshape(1, SG_WIDTH // SG_GROUP_DIM, SG_CHUNK, SG_CHUNK)
        gs["sg_b"][l] = dbias.reshape(SG_CHUNK, SG_WIDTH // SG_GROUP_DIM, SG_GROUP_DIM).sum(-1).T[None]

        d_o, dz, gs["hg_norm"][l] = rowwise(
            t + "hgrn_post", _hg_post_bwd, m, ins=[(da, D, 0), (sv["o_f"], D, 0), (sv["o_b"], D, 0), (sv["z"], D, ZG)],
            consts=[hg_g], outs=[(D, BF16)], alias_outs=[(dz, D, ZG)], accs=[(1, D)])
        (dq_f, dv_f, dq_b, dv_b, dzf_f, dzf_b, dlb_f[l], dlb_b[l]), extra = hgrn_bwd(
            t + "hgrn", sv["z"], d_o, sv["s_f"], sv["s_b"], lbf, lbb, exch=plan.exch(t + "hgrn"))
        plan.done(t + "hgrn", extra)

        def combine(dqf, dqb, dvf, dvb, dff_, dfb_, zq):
            dq = dqf.astype(F32) + dqb.astype(F32)
            dv = dvf.astype(F32) + dvb.astype(F32)
            return (jnp.concatenate([(dq * _silu_grad(zq)).astype(BF16), dff_, dfb_, dv.astype(BF16)], axis=1),)
        (dz,) = rowwise(t + "hgrn_combine", combine, m,
                        ins=[(dq_f, D, 0), (dq_b, D, 0), (dv_f, D, 0), (dv_b, D, 0), (dzf_f, D, 0), (dzf_b, D, 0),
                             (sv["z"], D, ZQ)], alias_outs=[(dz, 4 * D, 0)], tm=128)
        wgrad("w_in", t + "w_in", sv["h"], dz)
        ex = plan.exch(t + "in_proj_dx")
        dh = mm(t + "in_proj_dx", dz, W["w_in"], "nt", tm=tm, exch=ex)
        if ex is not None:
            dh, extra = dh
            plan.done(t + "in_proj_dx", extra)

        def pre_bwd(d1, dhv, xv, g):
            xh, r = _rms(xv)
            return d1 + _rms_bwd(dhv, xh, r, g), _colsum(dhv * xh)
        dx, gs["norm_mix_pre"][l] = rowwise(t + "pre_norm", pre_bwd, m, ins=[(dx1, D, 0), (dh, D, 0), (sv["x"], D, 0)],
                                            consts=[g_pre], outs=[(D, F32)], accs=[(1, D)])
        saved[l] = None

    gs["lb_gamma_fwd"], gs["lb_gamma_bwd"] = lower_bounds_bwd(
        "lower_bounds_bwd", S["lb_gamma_fwd"], S["lb_gamma_bwd"], jnp.concatenate(dlb_f, axis=0),
        jnp.concatenate(dlb_b, axis=0))
    small = {n: (g if not isinstance(g, list) else jnp.concatenate(g, axis=0)).reshape(S[n].shape)
             for n, g in gs.items()}
    return loss_cols, dx, small


def cast_pad(name, w, rows_p, cols_p):
    _, r, c = w.shape

    def body(w_ref, o_ref):
        if (rows_p, cols_p) != (r, c):
            o_ref[...] = jnp.zeros(o_ref.shape, BF16)
        o_ref[0:r, 0:c] = w_ref[...].astype(BF16)

    return pl.pallas_call(
        body, name=name, grid=(DEPTH,), in_specs=[pl.BlockSpec((None, r, c), lambda l: (l, 0, 0))],
        out_specs=pl.BlockSpec((None, rows_p, cols_p), lambda l: (l, 0, 0)),
        out_shape=jax.ShapeDtypeStruct((DEPTH, rows_p, cols_p), BF16), compiler_params=_params(("parallel",)),
    )(w)


def _shard_shape(n, shape):
    axis, size, _, _ = LAYOUT[n]
    _, r, c = shape
    return (size, c) if axis == 0 else (r, size)


class DistPlan:
    def __init__(self, shards):
        self.shards = shards
        self.W = [dict() for _ in range(DEPTH)]
        self.grads = [dict() for _ in range(DEPTH)]
        self.slots = [dict() for _ in range(DEPTH)]
        rest = [n for n in BIG if n != "w_in"]
        self.schedule = {
            "l0_in_proj": ("gather", [(0, n) for n in rest]),
            "l0_hgrn_fwd": ("gather", [(1, n) for n in BIG]),
            "l0_bwd_hgrn": ("scatter", [(1, n) for n in BIG] + [(0, n) for n in rest]),
            "l0_bwd_in_proj_dx": ("scatter", [(0, "w_in")]),
        }
        self.pending = {}
        self.done("start", exchange("gather_l0_w_in", self._gather("start", [(0, "w_in")])))

    def _gather(self, host, parts):
        srcs, dsts, items, keys = [], [], [], []
        for layer, n in parts:
            axis, size, dst, base = LAYOUT[n]
            if (layer, dst) not in keys:
                keys.append((layer, dst))
                dsts.append((GATHERED[dst], BF16))
            srcs.append(self.shards[n])
            items.append(("gather", len(srcs) - 1, keys.index((layer, dst)), axis, size, base, layer))
        self.pending[host] = ("gather", keys)
        return Exchange(srcs, dsts, items)

    def _scatter(self, host, parts):
        srcs, src_keys, dsts, items = [], [], [], []
        for layer, n in parts:
            axis, size, dst, base = LAYOUT[n]
            if (layer, dst) not in src_keys:
                src_keys.append((layer, dst))
                srcs.append(self.grads[layer][dst])
            dsts.append(((NDEV,) + _shard_shape(n, self.shards[n].shape), BF16))
            items.append(("scatter", src_keys.index((layer, dst)), len(dsts) - 1, axis, size, base, None))
        self.pending[host] = ("scatter", parts)
        return Exchange(srcs, dsts, items)

    def exch(self, host):
        if host not in self.schedule:
            return None
        kind, parts = self.schedule[host]
        return self._gather(host, parts) if kind == "gather" else self._scatter(host, parts)

    def done(self, host, outs):
        if host not in self.pending:
            return
        kind, keys = self.pending.pop(host)
        for (layer, n), arr in zip(keys, outs):
            (self.W if kind == "gather" else self.slots)[layer][n] = arr


def adam(name, w, m_, v_, tr, g=None, slots=None):
    L, r, c = w.shape
    assert r % tr == 0
    nt = r // tr

    def body(*refs):
        n_g = 1 if slots is None else L
        g_refs = refs[:n_g]
        w_ref, m_ref, v_ref, g_out, d_out, m_out, v_out = refs[n_g:]

        def update(gv):
            m2 = B1 * m_ref[...] + (1.0 - B1) * gv
            v2 = B2 * v_ref[...] + (1.0 - B2) * (gv * gv)
            m_hat = m2 / (1.0 - B1 ** STEP)
            v_hat = v2 / (1.0 - B2 ** STEP)
            g_out[...] = gv
            d_out[...] = -LR * (m_hat / (jnp.sqrt(v_hat) + AEPS) + WD * w_ref[...])
            m_out[...] = m2
            v_out[...] = v2

        if slots is None:
            update(g_refs[0][...])
            return
        for layer, s_ref in enumerate(g_refs):
            @pl.when(pl.program_id(0) == layer)
            def _():
                gv = s_ref[0][:, :c].astype(F32)
                for j in range(1, NDEV):
                    gv = gv + s_ref[j][:, :c].astype(F32)
                update(gv)

    spec = pl.BlockSpec((None, tr, c), lambda l, i: (l, i, 0))
    if slots is None:
        g_arrs, g_specs = [g], [spec]
    else:
        assert L == 2
        g_arrs = list(slots)
        cp = slots[0].shape[2]
        g_specs = [pl.BlockSpec((NDEV, tr, cp), lambda l, i: (0, i * (1 - l) + (nt - 1) * l, 0)),
                   pl.BlockSpec((NDEV, tr, cp), lambda l, i: (0, i * l, 0))]
    shp = jax.ShapeDtypeStruct(w.shape, F32)
    return pl.pallas_call(
        body, name=name, grid=(L, nt), in_specs=g_specs + [spec, spec, spec], out_specs=[spec] * 4,
        out_shape=[shp] * 4, compiler_params=_params(("arbitrary", "arbitrary")),
    )(*g_arrs, w, m_, v_)


def _pack(arrs):
    parts = []
    for a in arrs:
        a2 = a.reshape(-1, D)
        parts.append(jnp.pad(a2, ((0, -a2.shape[0] % 8), (0, 0))))
    return jnp.concatenate(parts, axis=0)


def _unpack(buf, shapes):
    out, off = [], 0
    for s in shapes:
        rows = 1
        for d_ in s:
            rows *= d_
        rows //= D
        out.append(buf[off:off + rows].reshape(s))
        off += rows + (-rows % 8)
    return out


def kernel(x, p, norm_mix_pre, w_in, lb_gamma_fwd, lb_gamma_bwd, hg_norm, sg_w, sg_b, sg_ln_g, sg_ln_b, w_a, w_b, w_out, norm_mix_post, norm_ffn_pre, w_gate, w_up, w_down, norm_ffn_post, w_ple, w_ple_gate, loss_target, m_norm_mix_pre, m_w_in, m_lb_gamma_fwd, m_lb_gamma_bwd, m_hg_norm, m_sg_w, m_sg_b, m_sg_ln_g, m_sg_ln_b, m_w_a, m_w_b, m_w_out, m_norm_mix_post, m_norm_ffn_pre, m_w_gate, m_w_up, m_w_down, m_norm_ffn_post, m_w_ple, m_w_ple_gate, v_norm_mix_pre, v_w_in, v_lb_gamma_fwd, v_lb_gamma_bwd, v_hg_norm, v_sg_w, v_sg_b, v_sg_ln_g, v_sg_ln_b, v_w_a, v_w_b, v_w_out, v_norm_mix_post, v_norm_ffn_pre, v_w_gate, v_w_up, v_w_down, v_norm_ffn_post, v_w_ple, v_w_ple_gate):
    a = dict(zip(INPUTS, (x, p, norm_mix_pre, w_in, lb_gamma_fwd, lb_gamma_bwd, hg_norm, sg_w, sg_b, sg_ln_g, sg_ln_b, w_a, w_b, w_out, norm_mix_post, norm_ffn_pre, w_gate, w_up, w_down, norm_ffn_post, w_ple, w_ple_gate, loss_target, m_norm_mix_pre, m_w_in, m_lb_gamma_fwd, m_lb_gamma_bwd, m_hg_norm, m_sg_w, m_sg_b, m_sg_ln_g, m_sg_ln_b, m_w_a, m_w_b, m_w_out, m_norm_mix_post, m_norm_ffn_pre, m_w_gate, m_w_up, m_w_down, m_norm_ffn_post, m_w_ple, m_w_ple_gate, v_norm_mix_pre, v_w_in, v_lb_gamma_fwd, v_lb_gamma_bwd, v_hg_norm, v_sg_w, v_sg_b, v_sg_ln_g, v_sg_ln_b, v_w_a, v_w_b, v_w_out, v_norm_mix_post, v_norm_ffn_pre, v_w_gate, v_w_up, v_w_down, v_norm_ffn_post, v_w_ple, v_w_ple_gate)))
    m = x.shape[1]

    shards = {n: cast_pad("cast_" + n, a[n], *_shard_shape(n, a[n].shape)) for n in BIG}
    plan = DistPlan(shards)
    loss_cols, dx, gs = local_step(x[0], p[:, 0], loss_target[0], {n: a[n] for n in SMALL}, plan)
    loss = lax.psum(jnp.sum(loss_cols) * (0.5 / D), ("x", "y", "c"))

    small_shapes = [a[n].shape for n in SMALL]
    g_small = allreduce_small("allreduce_small", _pack([gs[n] for n in SMALL]))

    res = {}
    row_tiles = {"w_in": 128, "w_a": 128, "w_b": 512, "w_out": 128, "w_gate": 128, "w_up": 128, "w_down": 88,
                 "w_ple": 256, "w_ple_gate": 128}
    for n in BIG:
        res[n] = adam("adam_" + n, a[n], a["m_" + n], a["v_" + n], row_tiles[n],
                      slots=[plan.slots[l][n] for l in range(DEPTH)])
    packed = [_pack([a[pre + n] for n in SMALL])[None] for pre in ("", "m_", "v_")]
    small_res = adam("adam_small", packed[0], packed[1], packed[2], packed[0].shape[1], g=g_small[None])
    small_res = [_unpack(r_[0], small_shapes) for r_ in small_res]
    for i, n in enumerate(SMALL):
        res[n] = tuple(small_res[k][i] for k in range(4))

    outs = [loss, dx.reshape(1, m, D)]
    for k in range(4):
        outs += [res[n][k] for n in WEIGHTS]
    return tuple(outs)
```

```python
import jax
import jax.numpy as jnp
from jax import lax
from jax.experimental import pallas as pl
from jax.experimental.pallas import tpu as pltpu

F32 = jnp.float32
BF16 = jnp.bfloat16

D = 1024
N_IN = 8192
HEADS = 8
HEAD_DIM = 128
SG_CHUNK = 128
SG_WIDTH = 512
SG_GROUP_DIM = 64
FFN = 2816
PLE_DIM = 256
EPS = 1e-6
DEPTH = 2
ZQ, ZFF, ZFB, ZI, ZG, GA, GB = 0, 1, 2, 3, 4, 6, 7
ZU, ZV = 10, 11

NDEV = 8
FFN_SHARD = FFN // NDEV
FFN_SHARD_PAD = 384
FFN_PAD = NDEV * FFN_SHARD_PAD

LR, B1, B2, AEPS, WD, STEP = 0.001, 0.9, 0.999, 1e-08, 0.01, 10

ROW_TILE = 256
HG_CHUNK = 64
HG_BLOCK_FWD = 256
HG_BLOCK_BWD = 128
EXP_CLAMP = 80.0
TINY = float(jnp.finfo(jnp.float32).tiny)
VMEM_LIMIT = 56 * 1024 * 1024

BIG = ["w_in", "w_a", "w_b", "w_out", "w_gate", "w_up", "w_down", "w_ple", "w_ple_gate"]
SMALL = ["norm_mix_pre", "lb_gamma_fwd", "lb_gamma_bwd", "hg_norm", "sg_w", "sg_b", "sg_ln_g", "sg_ln_b",
         "norm_mix_post", "norm_ffn_pre", "norm_ffn_post"]
WEIGHTS = ["norm_mix_pre", "w_in", "lb_gamma_fwd", "lb_gamma_bwd", "hg_norm", "sg_w", "sg_b", "sg_ln_g", "sg_ln_b",
           "w_a", "w_b", "w_out", "norm_mix_post", "norm_ffn_pre", "w_gate", "w_up", "w_down", "norm_ffn_post",
           "w_ple", "w_ple_gate"]
INPUTS = (["x", "p"] + WEIGHTS + ["loss_target"] + ["m_" + n for n in WEIGHTS] + ["v_" + n for n in WEIGHTS])
LAYOUT = {
    "w_in": (1, 1024, "w_in", 0), "w_a": (0, 128, "w_a", 0), "w_b": (1, 128, "w_b", 0),
    "w_out": (0, 128, "w_out", 0), "w_gate": (1, FFN_SHARD_PAD, "w_gu", 0),
    "w_up": (1, FFN_SHARD_PAD, "w_gu", FFN_PAD), "w_down": (0, FFN_SHARD_PAD, "w_down", 0),
    "w_ple": (1, 128, "w_ple", 0), "w_ple_gate": (0, 128, "w_ple_gate", 0),
}
GATHERED = {"w_in": (D, N_IN), "w_a": (D, D), "w_b": (SG_WIDTH, D), "w_out": (D, D), "w_gu": (D, 2 * FFN_PAD),
            "w_down": (FFN_PAD, D), "w_ple": (PLE_DIM, D), "w_ple_gate": (D, D)}


def _params(sem):
    return pltpu.CompilerParams(dimension_semantics=sem, vmem_limit_bytes=VMEM_LIMIT)


def _dot(a, b):
    return lax.dot_general(a, b, (((1,), (0,)), ((), ())), preferred_element_type=F32)


def _dot_nt(a, b):
    return lax.dot_general(a, b, (((1,), (1,)), ((), ())), preferred_element_type=F32)


def _dot_tn(a, b):
    return lax.dot_general(a, b, (((0,), (0,)), ((), ())), preferred_element_type=F32)


def _sigmoid(x):
    return jax.nn.sigmoid(x)


def _silu(x):
    return x * _sigmoid(x)


def _silu_grad(x):
    s = _sigmoid(x)
    return s * (1.0 + x * (1.0 - s))


def _gelu(x):
    return 0.5 * x * (1.0 + lax.erf(x * 0.7071067811865476))


def _gelu_grad(x):
    return 0.5 * (1.0 + lax.erf(x * 0.7071067811865476)) + x * jnp.exp(-0.5 * x * x) * 0.3989422804014327


def _mean(x):
    return jnp.mean(x, axis=-1, keepdims=True)


def _colsum(x):
    return jnp.sum(x, axis=0, keepdims=True)


def _rms(x):
    r = lax.rsqrt(_mean(x * x) + EPS)
    return x * r, r


def _rms_bwd(dy, xh, r, g):
    dyg = dy * g
    return r * (dyg - xh * _mean(dyg * xh))


MESH = pl.DeviceIdType.MESH
ANY = pl.BlockSpec(memory_space=pl.ANY)


def _slab(ref, axis, start, size):
    idx = [slice(None)] * 2
    idx[axis] = pl.ds(start, size)
    return ref.at[tuple(idx)]


class Exchange:
    def __init__(self, srcs, dsts, items):
        self.srcs, self.dsts, self.items = list(srcs), list(dsts), list(items)

    def specs(self):
        n = len(self.items)
        sems = [pltpu.SemaphoreType.DMA((n * (NDEV - 1),)), pltpu.SemaphoreType.DMA((n * (NDEV - 1),)),
                pltpu.SemaphoreType.DMA((n,))]
        return ([ANY] * len(self.srcs), [ANY] * len(self.dsts),
                [jax.ShapeDtypeStruct(s, dt) for (s, dt) in self.dsts], sems)

    def copies(self, src, dst, send_sem, recv_sem, loc_sem):
        x, y, c = lax.axis_index("x"), lax.axis_index("y"), lax.axis_index("c")
        me = 4 * x + 2 * y + c
        starts, waits = [], []
        for n, (kind, si, di, axis, size, base, layer) in enumerate(self.items):
            def views(to_dev, from_dev):
                if kind == "gather":
                    return (src[si].at[layer],
                            _slab(dst[di], axis, base + pl.multiple_of(from_dev * size, 128), size))
                if kind == "copies":
                    return src[si], dst[di].at[from_dev]
                return _slab(src[si], axis, base + pl.multiple_of(to_dev * size, 128), size), dst[di].at[from_dev]

            s_own, d_own = views(me, me)
            own = pltpu.make_async_copy(s_own, d_own, loc_sem.at[n])
            starts.append(own)
            waits.append(own)
            for k in range(1, NDEV):
                px = 1 - x if k & 4 else x
                py = 1 - y if k & 2 else y
                pc = 1 - c if k & 1 else c
                peer = 4 * px + 2 * py + pc
                s_out, _ = views(peer, me)
                _, d_in = views(me, peer)
                sem = n * (NDEV - 1) + k - 1
                starts.append(pltpu.make_async_remote_copy(s_out, d_own, send_sem.at[sem], recv_sem.at[sem],
                                                           device_id=(px, py, pc), device_id_type=MESH))
                waits.append(pltpu.make_async_remote_copy(s_out, d_in, send_sem.at[sem], recv_sem.at[sem],
                                                          device_id=(px, py, pc), device_id_type=MESH))
        return starts, waits


def exchange(name, exch):
    e_in, e_out, e_shape, e_scr = exch.specs()
    ns, nd = len(e_in), len(e_out)

    def body(*refs):
        starts, waits = exch.copies(refs[:ns], refs[ns:ns + nd], *refs[ns + nd:])
        for cp in starts:
            cp.start()
        for cp in waits:
            cp.wait()

    return pl.pallas_call(body, name=name, in_specs=e_in, out_specs=e_out, out_shape=e_shape, scratch_shapes=e_scr,
                          compiler_params=pltpu.CompilerParams(has_side_effects=True))(*exch.srcs)


def hosted_call(body, exch, name, grid, in_specs, out_specs, out_shape, scratch_shapes, operands, semantics,
                aliases=None):
    aliases = aliases or {}
    if exch is None:
        res = pl.pallas_call(body, name=name, grid=grid, in_specs=in_specs, out_specs=out_specs, out_shape=out_shape,
                             scratch_shapes=scratch_shapes, input_output_aliases=aliases,
                             compiler_params=_params(semantics))(*operands)
        return list(res), []
    n_in, n_out, n_scr = len(in_specs), len(out_specs), len(scratch_shapes)
    e_in, e_out, e_shape, e_scr = exch.specs()
    ns, nd = len(e_in), len(e_out)

    def at_step(last):
        cond = None
        for ax, n in enumerate(grid):
            c = pl.program_id(ax) == (n - 1 if last else 0)
            cond = c if cond is None else jnp.logical_and(cond, c)
        return cond

    def wrapped(*refs):
        ins, src = refs[:n_in], refs[n_in:n_in + ns]
        o0 = n_in + ns
        outs, dst = refs[o0:o0 + n_out], refs[o0 + n_out:o0 + n_out + nd]
        s0 = o0 + n_out + nd
        scr, sems = refs[s0:s0 + n_scr], refs[s0 + n_scr:]

        @pl.when(at_step(False))
        def _():
            for cp in exch.copies(src, dst, *sems)[0]:
                cp.start()

        body(*ins, *outs, *scr)

        @pl.when(at_step(True))
        def _():
            for cp in exch.copies(src, dst, *sems)[1]:
                cp.wait()

    res = pl.pallas_call(
        wrapped, name=name, grid=grid, in_specs=list(in_specs) + e_in, out_specs=list(out_specs) + e_out,
        out_shape=list(out_shape) + e_shape, scratch_shapes=list(scratch_shapes) + e_scr,
        input_output_aliases=aliases,
        compiler_params=pltpu.CompilerParams(dimension_semantics=("arbitrary",) * len(grid),
                                             vmem_limit_bytes=VMEM_LIMIT, has_side_effects=True),
    )(*operands, *exch.srcs)
    return list(res[:n_out]), list(res[n_out:])


def allreduce_small(name, part):
    rows, width = part.shape

    def body(p_ref, o_ref, buf, send_sem, recv_sem):
        x, y, c = lax.axis_index("x"), lax.axis_index("y"), lax.axis_index("c")
        me = 4 * x + 2 * y + c
        buf[me] = p_ref[...]
        waits = []
        for k in range(1, NDEV):
            px = 1 - x if k & 4 else x
            py = 1 - y if k & 2 else y
            pc = 1 - c if k & 1 else c
            peer = 4 * px + 2 * py + pc
            pltpu.make_async_remote_copy(p_ref, buf.at[me], send_sem.at[k - 1], recv_sem.at[k - 1],
                                         device_id=(px, py, pc), device_id_type=MESH).start()
            waits.append(pltpu.make_async_remote_copy(p_ref, buf.at[peer], send_sem.at[k - 1], recv_sem.at[k - 1],
                                                      device_id=(px, py, pc), device_id_type=MESH))
        for w in waits:
            w.wait()
        acc = buf[0]
        for j in range(1, NDEV):
            acc = acc + buf[j]
        o_ref[...] = acc

    vmem = pl.BlockSpec(memory_space=pltpu.VMEM)
    return pl.pallas_call(
        body, name=name, in_specs=[vmem], out_specs=vmem, out_shape=jax.ShapeDtypeStruct((rows, width), F32),
        scratch_shapes=[pltpu.VMEM((NDEV, rows, width), F32), pltpu.SemaphoreType.DMA((NDEV - 1,)),
                        pltpu.SemaphoreType.DMA((NDEV - 1,))],
        compiler_params=pltpu.CompilerParams(vmem_limit_bytes=VMEM_LIMIT, has_side_effects=True),
    )(part)


def rowwise(name, fn, m, ins=(), consts=(), outs=(), alias_outs=(), accs=(), tm=ROW_TILE):
    tm = min(tm, m)
    n_in, n_c, n_o, n_al, n_ac = len(ins), len(consts), len(outs), len(alias_outs), len(accs)
    held = [a for (a, _, _) in alias_outs if not isinstance(a, jax.ShapeDtypeStruct)]
    n_held = len(held)

    def body(*refs):
        in_refs = refs[:n_in + n_c]
        out_refs = refs[n_in + n_c + n_held:]
        vals = fn(*[r[...] for r in in_refs])
        if not isinstance(vals, (tuple, list)):
            vals = (vals,)
        for r, v in zip(out_refs[:n_o + n_al], vals[:n_o + n_al]):
            r[...] = v.astype(r.dtype)
        if n_ac:
            acc_refs = out_refs[n_o + n_al:]

            @pl.when(pl.program_id(0) == 0)
            def _():
                for r in acc_refs:
                    r[...] = jnp.zeros(r.shape, F32)

            for r, v in zip(acc_refs, vals[n_o + n_al:]):
                r[...] += v

    def col(cb):
        return lambda i: (i, cb)

    in_specs = [pl.BlockSpec((tm, w), col(cb)) for (_, w, cb) in ins]
    in_specs += [pl.BlockSpec(c.shape, lambda i, nd=c.ndim: (0,) * nd) for c in consts]
    in_specs += [ANY for _ in held]
    out_shape = [jax.ShapeDtypeStruct((m, w), dt) for (w, dt) in outs]
    out_specs = [pl.BlockSpec((tm, w), col(0)) for (w, _) in outs]
    out_shape += [jax.ShapeDtypeStruct(a.shape, a.dtype) for (a, _, _) in alias_outs]
    out_specs += [pl.BlockSpec((tm, w), col(cb)) for (_, w, cb) in alias_outs]
    out_shape += [jax.ShapeDtypeStruct(s, F32) for s in accs]
    out_specs += [pl.BlockSpec(s, lambda i: (0, 0)) for s in accs]
    aliases, k_in = {}, n_in + n_c
    for k, (a, _, _) in enumerate(alias_outs):
        if not isinstance(a, jax.ShapeDtypeStruct):
            aliases[k_in] = n_o + k
            k_in += 1
    return pl.pallas_call(
        body, name=name, grid=(m // tm,), in_specs=in_specs, out_specs=out_specs, out_shape=out_shape,
        input_output_aliases=aliases,
        compiler_params=_params(("arbitrary",) if n_ac else ("parallel",)),
    )(*[a for (a, _, _) in ins], *consts, *held)


def _operand(arr, bshape, imap):
    if isinstance(arr, tuple):
        arr, lead = arr
        return arr, pl.BlockSpec((None,) + bshape, lambda *g: (lead,) + imap(*g))
    return arr, pl.BlockSpec(bshape, imap)


def _shape2(arr):
    return arr[0].shape[1:] if isinstance(arr, tuple) else arr.shape


def mm(name, a, b, mode, out_dtype=F32, tm=1024, tn=1024, tk=1024, exch=None):
    sa, sb = _shape2(a), _shape2(b)
    if mode == "nn":
        (M, K), N = sa, sb[1]
    elif mode == "nt":
        (M, K), N = sa, sb[0]
    else:
        (K, M), N = sa, sb[1]
    tm, tn, tk = min(tm, M), min(tn, N), min(tk, K)
    assert M % tm == 0 and N % tn == 0 and K % tk == 0, (name, M, N, K)
    nk = K // tk
    if mode == "nn":
        a_arr, a_spec = _operand(a, (tm, tk), lambda i, j, k: (i, k))
        b_arr, b_spec = _operand(b, (tk, tn), lambda i, j, k: (k, j))
        dot = _dot
    elif mode == "nt":
        a_arr, a_spec = _operand(a, (tm, tk), lambda i, j, k: (i, k))
        b_arr, b_spec = _operand(b, (tn, tk), lambda i, j, k: (j, k))
        dot = _dot_nt
    else:
        a_arr, a_spec = _operand(a, (tk, tm), lambda i, j, k: (k, i))
        b_arr, b_spec = _operand(b, (tk, tn), lambda i, j, k: (k, j))
        dot = _dot_tn

    def body(a_ref, b_ref, o_ref, *acc):
        part = dot(a_ref[...].astype(BF16), b_ref[...].astype(BF16))
        if nk == 1:
            o_ref[...] = part.astype(o_ref.dtype)
            return
        acc_ref, k = acc[0], pl.program_id(2)

        @pl.when(k == 0)
        def _():
            acc_ref[...] = part

        @pl.when(k > 0)
        def _():
            acc_ref[...] += part

        @pl.when(k == nk - 1)
        def _():
            o_ref[...] = acc_ref[...].astype(o_ref.dtype)

    outs, extra = hosted_call(
        body, exch, name, (M // tm, N // tn, nk), [a_spec, b_spec], [pl.BlockSpec((tm, tn), lambda i, j, k: (i, j))],
        [jax.ShapeDtypeStruct((M, N), out_dtype)], [pltpu.VMEM((tm, tn), F32)] if nk > 1 else [], [a_arr, b_arr],
        ("parallel", "parallel", "arbitrary"))
    return outs[0] if exch is None else (outs[0], extra)


def _cumsum_rows(x):
    n = x.shape[0]
    row = lax.broadcasted_iota(jnp.int32, x.shape, 0)
    s = 1
    while s < n:
        x = x + jnp.where(row >= s, pltpu.roll(x, s, 0), 0.0)
        s *= 2
    return x


def _hg_prep(zq, zf, lb, reverse):
    n = zq.shape[0]
    q = _silu(zq)
    sig = _sigmoid(zf)
    sn = 1.0 - sig
    f = lb + (1.0 - lb) * sig
    k = (1.0 - lb) * sn
    g = jnp.log(jnp.maximum(f, TINY))
    b = _cumsum_rows(g)
    if reverse:
        b = b[n - 1:n] - b + g
    b_last = b[0:1] if reverse else b[n - 1:n]
    b_ref = b[n // 2:n // 2 + 1]
    e1 = jnp.exp(b)
    e2 = jnp.exp(jnp.clip(b - b_ref, -EXP_CLAMP, EXP_CLAMP))
    e3 = jnp.exp(jnp.clip(b_ref - b, -EXP_CLAMP, EXP_CLAMP))
    e4 = jnp.exp(b_last - b)
    return dict(q=q, k=k, sig=sig, sn=sn, f=f, e1=e1, e2=e2, e3=e3, e4=e4, e_last=jnp.exp(b_last),
                qe=(q * e1).astype(BF16), qt=(q * e2).astype(BF16), kt=(k * e3).astype(BF16),
                ks=(k * e4).astype(BF16))


def _hg_mask(n, reverse):
    t = lax.broadcasted_iota(jnp.int32, (n, n), 0)
    s = lax.broadcasted_iota(jnp.int32, (n, n), 1)
    return (s >= t) if reverse else (s <= t)


def hgrn_fwd(name, z, lb_f, lb_b, exch=None, unroll=False):
    m = z.shape[0]
    C, T = HG_CHUNK, min(HG_BLOCK_FWD, m)
    nb, cpb = m // T, T // C

    def body(zq_f, zf_f, zi_f, zq_b, zf_b, zi_b, lbf_ref, lbb_ref, of_ref, ob_ref, sf_ref, sb_ref, st_ref):
        @pl.when(pl.program_id(0) == 0)
        def _():
            st_ref[...] = jnp.zeros(st_ref.shape, F32)

        dirs = ((zq_f, zf_f, zi_f, lbf_ref, of_ref, sf_ref), (zq_b, zf_b, zi_b, lbb_ref, ob_ref, sb_ref))

        def chunk(ci, carry):
            for d, (zq, zf, zi, lb_ref, o_ref, s_ref) in enumerate(dirs):
                cc = ci if d == 0 else cpb - 1 - ci
                rows = pl.ds(pl.multiple_of(cc * C, C), C)
                pre = _hg_prep(zq[rows, :], zf[rows, :], lb_ref[...], d == 1)
                v = zi[rows, :].astype(BF16)
                mask = _hg_mask(C, d == 1)
                for h in range(HEADS):
                    sl = slice(h * HEAD_DIM, (h + 1) * HEAD_DIM)
                    st = st_ref[d, h]
                    s_ref[cc, h] = st
                    a = jnp.where(mask, _dot_nt(pre["qt"][:, sl], pre["kt"][:, sl]), 0.0)
                    o_ref[rows, sl] = _dot_nt(pre["qe"][:, sl], st.astype(BF16)) + _dot(a.astype(BF16), v[:, sl])
                    st_ref[d, h] = st * pre["e_last"][:, sl] + _dot_tn(v[:, sl], pre["ks"][:, sl])
            return carry

        lax.fori_loop(0, cpb, chunk, 0, unroll=unroll)

    def zspec(cb, rev):
        return pl.BlockSpec((T, D), (lambda i: (nb - 1 - i, cb)) if rev else (lambda i: (i, cb)))

    def sspec(rev):
        shape = (cpb, HEADS, HEAD_DIM, HEAD_DIM)
        return pl.BlockSpec(shape, (lambda i: (nb - 1 - i, 0, 0, 0)) if rev else (lambda i: (i, 0, 0, 0)))

    lbspec = pl.BlockSpec((1, D), lambda i: (0, 0))
    states = jax.ShapeDtypeStruct((m // C, HEADS, HEAD_DIM, HEAD_DIM), F32)
    outs, extra = hosted_call(
        body, exch, name, (nb,),
        [zspec(ZQ, False), zspec(ZFF, False), zspec(ZI, False), zspec(ZQ, True), zspec(ZFB, True), zspec(ZI, True),
         lbspec, lbspec],
        [zspec(0, False), zspec(0, True), sspec(False), sspec(True)],
        [jax.ShapeDtypeStruct((m, D), F32), jax.ShapeDtypeStruct((m, D), F32), states, states],
        [pltpu.VMEM((2, HEADS, HEAD_DIM, HEAD_DIM), F32)], [z, z, z, z, z, z, lb_f, lb_b], ("arbitrary",))
    return outs, extra


def hgrn_bwd(name, z, d_o, s_f, s_b, lb_f, lb_b, exch=None, unroll=False):
    m = z.shape[0]
    C, T = HG_CHUNK, min(HG_BLOCK_BWD, m)
    nb, cpb = m // T, T // C

    def body(zq_f, zf_f, zi_f, do_f, sf_ref, zq_b, zf_b, zi_b, do_b, sb_ref, lbf_ref, lbb_ref,
             dqf_ref, dvf_ref, dqb_ref, dvb_ref, dzf_f, dzf_b, dlbf_ref, dlbb_ref,
             dst_ref, dki_ref, dks_ref, rr_ref, dqs_ref):
        @pl.when(pl.program_id(0) == 0)
        def _():
            dst_ref[...] = jnp.zeros(dst_ref.shape, F32)
            dlbf_ref[...] = jnp.zeros(dlbf_ref.shape, F32)
            dlbb_ref[...] = jnp.zeros(dlbb_ref.shape, F32)

        dirs = ((zq_f, zf_f, zi_f, do_f, sf_ref, lbf_ref, dqf_ref, dvf_ref, dzf_f, dlbf_ref),
                (zq_b, zf_b, zi_b, do_b, sb_ref, lbb_ref, dqb_ref, dvb_ref, dzf_b, dlbb_ref))

        def chunk(ci, carry):
            for d, (zq, zf, zi, do_ref, s_ref, lb_ref, dq_ref, dv_ref, dzf_ref, dlb_ref) in enumerate(dirs):
                rev = d == 1
                cc = cpb - 1 - ci if d == 0 else ci
                rows = pl.ds(pl.multiple_of(cc * C, C), C)
                lb = lb_ref[...]
                pre = _hg_prep(zq[rows, :], zf[rows, :], lb, rev)
                v = zi[rows, :].astype(BF16)
                do = do_ref[rows, :]
                mask = _hg_mask(C, rev)
                for h in range(HEADS):
                    sl = slice(h * HEAD_DIM, (h + 1) * HEAD_DIM)
                    st_prev = s_ref[cc, h]
                    dst = dst_ref[d, h]
                    dst16 = dst.astype(BF16)
                    qt, kt, ks, qe = pre["qt"][:, sl], pre["kt"][:, sl], pre["ks"][:, sl], pre["qe"][:, sl]
                    a = jnp.where(mask, _dot_nt(qt, kt), 0.0).astype(BF16)
                    da = jnp.where(mask, _dot_nt(do[:, sl], v[:, sl]), 0.0).astype(BF16)
                    dq = (_dot(da, kt) * pre["e2"][:, sl]
                          + _dot(do[:, sl], st_prev.astype(BF16)) * pre["e1"][:, sl])
                    dqs_ref[d, :, sl] = dq
                    dq_ref[rows, sl] = dq.astype(dq_ref.dtype)
                    dki_ref[d, :, sl] = _dot_tn(da, qt) * pre["e3"][:, sl]
                    dks_ref[d, :, sl] = _dot(v[:, sl], dst16) * pre["e4"][:, sl]
                    dv_ref[rows, sl] = (_dot_tn(a, do[:, sl]) + _dot_nt(ks, dst16)).astype(dv_ref.dtype)
                    rr_ref[d, :, sl] = pre["e_last"][:, sl] * _colsum(dst * st_prev)
                    dst_ref[d, h] = dst * pre["e_last"][:, sl] + _dot_tn(do[:, sl], qe)
                dki, dks = dki_ref[d], dks_ref[d]
                x = pre["q"] * dqs_ref[d] - pre["k"] * dki
                y = pre["k"] * dks
                if rev:
                    dg = _cumsum_rows(x - y) + _colsum(y) + rr_ref[d]
                else:
                    dg = _cumsum_rows(y - x) + (x - y) + _colsum(x) + rr_ref[d]
                inv_f = jnp.where(pre["f"] > TINY, 1.0 / pre["f"], 0.0)
                u = dg * inv_f - (dki + dks)
                dzf_ref[rows, :] = ((1.0 - lb) * pre["sig"] * pre["sn"] * u).astype(dzf_ref.dtype)
                dlb_ref[...] += _colsum(pre["sn"] * u)
            return carry

        lax.fori_loop(0, cpb, chunk, 0, unroll=unroll)

    def rspec(cb, rev):
        return pl.BlockSpec((T, D), (lambda i: (i, cb)) if rev else (lambda i: (nb - 1 - i, cb)))

    def sspec(rev):
        shape = (cpb, HEADS, HEAD_DIM, HEAD_DIM)
        return pl.BlockSpec(shape, (lambda i: (i, 0, 0, 0)) if rev else (lambda i: (nb - 1 - i, 0, 0, 0)))

    lbspec = pl.BlockSpec((1, D), lambda i: (0, 0))
    half = jax.ShapeDtypeStruct((m, D), BF16)
    row = jax.ShapeDtypeStruct((1, D), F32)
    outs, extra = hosted_call(
        body, exch, name, (nb,),
        [rspec(ZQ, False), rspec(ZFF, False), rspec(ZI, False), rspec(0, False), sspec(False),
         rspec(ZQ, True), rspec(ZFB, True), rspec(ZI, True), rspec(0, True), sspec(True), lbspec, lbspec],
        [rspec(0, False), rspec(0, False), rspec(0, True), rspec(0, True), rspec(0, False), rspec(0, True),
         lbspec, lbspec],
        [half, half, half, half, half, half, row, row],
        [pltpu.VMEM((2, HEADS, HEAD_DIM, HEAD_DIM), F32), pltpu.VMEM((2, C, D), F32), pltpu.VMEM((2, C, D), F32),
         pltpu.VMEM((2, 1, D), F32), pltpu.VMEM((2, C, D), F32)],
        [z, z, z, d_o, s_f, z, z, z, d_o, s_b, lb_f, lb_b], ("arbitrary",))
    return outs, extra


def _heads(fn, *arrs):
    res = [fn(*[a[:, h * HEAD_DIM:(h + 1) * HEAD_DIM] for a in arrs]) for h in range(HEADS)]
    return [jnp.concatenate(parts, axis=1) for parts in zip(*res)]


def _hg_post(o_f, o_b, zg, g):
    def head(of, ob, zgh, gh):
        on, _ = _rms(of + ob)
        return (on * gh * _silu(zgh),)
    return _heads(head, o_f, o_b, zg, g)[0]


def _hg_post_bwd(da, o_f, o_b, zg, g):
    def head(dah, of, ob, zgh, gh):
        on, r = _rms(of + ob)
        sg = _silu(zgh)
        d_on = dah * sg
        return _rms_bwd(d_on, on, r, gh), dah * on * gh * _silu_grad(zgh), d_on * on
    d_o, dzg, dg = _heads(head, da, o_f, o_b, zg, g)
    return d_o, dzg, _colsum(dg)


def _sg_parts(zv, ln_g, ln_b):
    vg = _gelu(zv)
    xc = vg - _mean(vg)
    rstd = lax.rsqrt(_mean(xc * xc) + EPS)
    vh = xc * rstd
    return vh, rstd, vh * ln_g + ln_b


def _sg_lane_group(shape):
    return lax.broadcasted_iota(jnp.int32, shape, 1) < SG_GROUP_DIM


def _sg_mix(w, v16, transpose):
    rows = v16.shape[0]
    out = []
    for c in range(rows // SG_CHUNK):
        parts = []
        for j in range(SG_WIDTH // 128):
            vj = v16[c * SG_CHUNK:(c + 1) * SG_CHUNK, j * 128:(j + 1) * 128]
            w0 = w[(2 * j) * SG_CHUNK:(2 * j + 1) * SG_CHUNK]
            w1 = w[(2 * j + 1) * SG_CHUNK:(2 * j + 2) * SG_CHUNK]
            dot = _dot_tn if transpose else _dot
            parts.append(jnp.where(_sg_lane_group((SG_CHUNK, 128)), dot(w0, vj), dot(w1, vj)))
        out.append(jnp.concatenate(parts, axis=1))
    return jnp.concatenate(out, axis=0)


def _sg_fwd(zu, zv, w, bias, ln_g, ln_b):
    _, _, v = _sg_parts(zv, ln_g, ln_b)
    reps = zu.shape[0] // SG_CHUNK
    return _gelu(zu) * (_sg_mix(w, v.astype(BF16), False) + jnp.concatenate([bias] * reps, axis=0))


def _sg_bwd(db, zu, zv, w, bias, ln_g, ln_b):
    vh, rstd, v = _sg_parts(zv, ln_g, ln_b)
    v16 = v.astype(BF16)
    reps = zu.shape[0] // SG_CHUNK
    sg = _sg_mix(w, v16, False) + jnp.concatenate([bias] * reps, axis=0)
    dzu = db * sg * _gelu_grad(zu)
    dsg = db * _gelu(zu)
    dsg16 = dsg.astype(BF16)
    dv = _sg_mix(w, dsg16, True)
    low = _sg_lane_group((SG_CHUNK, 128))
    dw = []
    for g in range(SG_WIDTH // SG_GROUP_DIM):
        j, keep = g // 2, (low if g % 2 == 0 else jnp.logical_not(low))
        acc = jnp.zeros((SG_CHUNK, SG_CHUNK), F32)
        for c in range(reps):
            rows = slice(c * SG_CHUNK, (c + 1) * SG_CHUNK)
            dj = jnp.where(keep, dsg16[rows, j * 128:(j + 1) * 128], jnp.zeros((), BF16))
            acc = acc + _dot_nt(dj, v16[rows, j * 128:(j + 1) * 128])
        dw.append(acc)
    dbias = sum(dsg[c * SG_CHUNK:(c + 1) * SG_CHUNK] for c in range(reps))
    dvh = dv * ln_g
    dvg = rstd * (dvh - _mean(dvh) - vh * _mean(dvh * vh))
    dzuv = jnp.concatenate([dzu, dvg * _gelu_grad(zv)], axis=1)
    return (dzuv, jnp.concatenate(dw, axis=0), dbias, _colsum(dv * vh), _colsum(dv))


def lower_bounds(name, gamma_f, gamma_b):
    def body(gf_ref, gb_ref, lf_ref, lb_ref):
        for g_ref, o_ref in ((gf_ref, lf_ref), (gb_ref, lb_ref)):
            g0, g1 = g_ref[0:1, :], g_ref[1:2, :]
            mx = jnp.maximum(g0, g1)
            e0, e1 = jnp.exp(g0 - mx), jnp.exp(g1 - mx)
            sm0, sm1 = e0 / (e0 + e1), e1 / (e0 + e1)
            o_ref[0:1, :] = sm0 - sm0
            o_ref[1:2, :] = (sm0 + sm1) - sm0
    shp = jax.ShapeDtypeStruct(gamma_f.shape, F32)
    return pl.pallas_call(body, name=name, out_shape=[shp, shp])(gamma_f, gamma_b)


def lower_bounds_bwd(name, gamma_f, gamma_b, dlb_f, dlb_b):
    def body(gf_ref, gb_ref, df_ref, db_ref, of_ref, ob_ref):
        for g_ref, d_ref, o_ref in ((gf_ref, df_ref, of_ref), (gb_ref, db_ref, ob_ref)):
            g0, g1 = g_ref[0:1, :], g_ref[1:2, :]
            mx = jnp.maximum(g0, g1)
            e0, e1 = jnp.exp(g0 - mx), jnp.exp(g1 - mx)
            sm0, sm1 = e0 / (e0 + e1), e1 / (e0 + e1)
            d1 = d_ref[1:2, :] * sm0 * sm1
            o_ref[0:1, :] = -d1
            o_ref[1:2, :] = d1
    shp = jax.ShapeDtypeStruct(gamma_f.shape, F32)
    return pl.pallas_call(body, name=name, out_shape=[shp, shp])(gamma_f, gamma_b, dlb_f, dlb_b)


def _row(a, l):
    return a[l:l + 1]


class LocalPlan:
    def __init__(self, weights):
        self.W = weights
        self.grads = [dict() for _ in range(DEPTH)]

    def exch(self, host):
        return None

    def done(self, host, outs):
        pass

    def early_small(self, packed):
        pass


def local_step(x, p, target, S, plan):
    m = x.shape[0]
    lb_f, lb_b = lower_bounds("lower_bounds", S["lb_gamma_fwd"], S["lb_gamma_bwd"])
    saved = []
    for l in range(DEPTH):
        t = f"l{l}_"
        W = plan.W[l]
        tm = 2048
        in_tile = ((2048, 1024), (1024, 2048))[l]
        ffn_tile = ((2048, 1024), (2048, 2048))[l]
        g_pre, g_post = _row(S["norm_mix_pre"], l), _row(S["norm_mix_post"], l)
        g_fpre, g_fpost = _row(S["norm_ffn_pre"], l), _row(S["norm_ffn_post"], l)
        hg_g = _row(S["hg_norm"], l)
        sg_w = S["sg_w"][l].reshape(SG_WIDTH // SG_GROUP_DIM * SG_CHUNK, SG_CHUNK).astype(BF16)
        sg_bias = jnp.repeat(S["sg_b"][l].T, SG_GROUP_DIM, axis=1)
        ln_g, ln_b = _row(S["sg_ln_g"], l), _row(S["sg_ln_b"], l)
        lbf, lbb = _row(lb_f, l), _row(lb_b, l)

        (h,) = rowwise(t + "pre_norm", lambda xv, g: (_rms(xv)[0] * g,), m, ins=[(x, D, 0)], consts=[g_pre],
                       outs=[(D, BF16)])
        ex = plan.exch(t + "in_proj")
        z = mm(t + "in_proj", h, W["w_in"], "nn", tm=in_tile[0], tn=in_tile[1], exch=ex)
        if ex is not None:
            z, extra = z
            plan.done(t + "in_proj", extra)
        (o_f, o_b, s_f, s_b), extra = hgrn_fwd(t + "hgrn_fwd", z, lbf, lbb, exch=plan.exch(t + "hgrn_fwd"),
                                               unroll=(l == 0))
        plan.done(t + "hgrn_fwd", extra)
        (a_out,) = rowwise(t + "hgrn_post", _hg_post, m, ins=[(o_f, D, 0), (o_b, D, 0), (z, D, ZG)], consts=[hg_g],
                           outs=[(D, BF16)])
        (b_out,) = rowwise(t + "sgu_fwd", _sg_fwd, m, ins=[(z, SG_WIDTH, ZU), (z, SG_WIDTH, ZV)],
                           consts=[sg_w, sg_bias, ln_g, ln_b], outs=[(SG_WIDTH, BF16)])
        pa = mm(t + "proj_a", a_out, W["w_a"], "nn", BF16)
        pb = mm(t + "proj_b", b_out, W["w_b"], "nn", BF16)
        (merged,) = rowwise(t + "merge", lambda a, b, ga, gb: (_sigmoid(ga) * a + _sigmoid(gb) * b,), m,
                            ins=[(pa, D, 0), (pb, D, 0), (z, D, GA), (z, D, GB)], outs=[(D, BF16)])
        mix = mm(t + "out_proj", merged, W["w_out"], "nn")

        def post_pre(xv, mixv, gp, gf):
            x1 = xv + _rms(mixv)[0] * gp
            return x1, _rms(x1)[0] * gf
        x1, h2 = rowwise(t + "mix_post_ffn_pre", post_pre, m, ins=[(x, D, 0), (mix, D, 0)], consts=[g_post, g_fpre],
                         outs=[(D, F32), (D, BF16)])
        gu = mm(t + "ffn_in", h2, W["w_gu"], "nn", BF16, tm=ffn_tile[0], tn=ffn_tile[1])
        (hid,) = rowwise(t + "ffn_act", lambda gt, up: (_silu(gt.astype(F32)) * up,), m, ins=[(gu, FFN_PAD, 0), (gu, FFN_PAD, 1)],
                         outs=[(FFN_PAD, BF16)])
        ff = mm(t + "ffn_out", hid, W["w_down"], "nn", tm=tm)
        (x2,) = rowwise(t + "ffn_post", lambda xv, f, g: (xv + _rms(f)[0] * g,), m, ins=[(x1, D, 0), (ff, D, 0)],
                        consts=[g_fpost], outs=[(D, F32)])
        e = mm(t + "ple_proj", (p, l), W["w_ple"], "nn")
        tg = mm(t + "ple_gate", x2, W["w_ple_gate"], "nn")
        (x3,) = rowwise(t + "ple_add", lambda xv, ev, tv: (xv + ev * _sigmoid(tv),), m,
                        ins=[(x2, D, 0), (e, D, 0), (tg, D, 0)], outs=[(D, F32)])
        saved.append(dict(x=x, h=h, z=z, o_f=o_f, o_b=o_b, s_f=s_f, s_b=s_b, a_out=a_out, b_out=b_out, pa=pa, pb=pb,
                          merged=merged, mix=mix, x1=x1, h2=h2, gu=gu, hid=hid, ff=ff, x2=x2, e=e, tg=tg,
                          sg_w=sg_w, sg_bias=sg_bias))
        x = x3

    def loss_fn(y, tv):
        err = y - tv
        return err * (1.0 / D), _colsum(err * err)
    dx, loss_cols = rowwise("loss", loss_fn, m, ins=[(x, D, 0), (target, D, 0)], outs=[(D, F32)], accs=[(1, D)])

    gs = {n: [None] * DEPTH for n in SMALL}
    dlb_f, dlb_b = [None] * DEPTH, [None] * DEPTH

    for l in reversed(range(DEPTH)):
        t = f"l{l}_bwd_"
        sv, W = saved[l], plan.W[l]
        tm, tk = 2048, (2048, 4096)[l]
        g_pre, g_post = _row(S["norm_mix_pre"], l), _row(S["norm_mix_post"], l)
        g_fpre, g_fpost = _row(S["norm_ffn_pre"], l), _row(S["norm_ffn_post"], l)
        hg_g = _row(S["hg_norm"], l)
        ln_g, ln_b = _row(S["sg_ln_g"], l), _row(S["sg_ln_b"], l)
        lbf, lbb = _row(lb_f, l), _row(lb_b, l)

        def wgrad(nm, tag, a, b):
            a_dtype = (a[0] if isinstance(a, tuple) else a).dtype
            plan.grads[l][nm] = mm(tag, a, b, "tn", BF16, tk=tk if a_dtype == BF16 else 2048)

        def ple_bwd(d3, ev, tv):
            s = _sigmoid(tv)
            return d3 * s, d3 * ev * s * (1.0 - s)
        de, dt = rowwise(t + "ple", ple_bwd, m, ins=[(dx, D, 0), (sv["e"], D, 0), (sv["tg"], D, 0)],
                         outs=[(D, BF16), (D, BF16)])
        wgrad("w_ple", t + "w_ple", (p, l), de)
        wgrad("w_ple_gate", t + "w_ple_gate", sv["x2"], dt)
        dx2p = mm(t + "ple_gate_dx", dt, W["w_ple_gate"], "nt")

        def ffn_post_bwd(d3, d2p, f, g):
            d2 = d3 + d2p
            fh, r = _rms(f)
            return d2, _rms_bwd(d2, fh, r, g), _colsum(d2 * fh)
        dx2, dff, gs["norm_ffn_post"][l] = rowwise(
            t + "ffn_post", ffn_post_bwd, m, ins=[(dx, D, 0), (dx2p, D, 0), (sv["ff"], D, 0)], consts=[g_fpost],
            outs=[(D, F32), (D, BF16)], accs=[(1, D)])
        wgrad("w_down", t + "w_down", sv["hid"], dff)
        dhid = mm(t + "ffn_out_dx", dff, W["w_down"], "nt", BF16, tm=tm)

        def act_bwd(dh, gt, up):
            dh, gt = dh.astype(F32), gt.astype(F32)
            return (jnp.concatenate([dh * up * _silu_grad(gt), dh * _silu(gt)], axis=1),)
        (dgu,) = rowwise(t + "ffn_act", act_bwd, m, ins=[(dhid, FFN_PAD, 0), (sv["gu"], FFN_PAD, 0),
                                                       (sv["gu"], FFN_PAD, 1)], outs=[(2 * FFN_PAD, BF16)], tm=128)
        wgrad("w_gu", t + "w_gu", sv["h2"], dgu)
        dh2 = mm(t + "ffn_in_dx", dgu, W["w_gu"], "nt", tm=tm)

        def pre_post_bwd(d2, dh, x1v, mixv, gf, gp):
            xh, r1 = _rms(x1v)
            d1 = d2 + _rms_bwd(dh, xh, r1, gf)
            mh, rm = _rms(mixv)
            return d1, _rms_bwd(d1, mh, rm, gp), _colsum(dh * xh), _colsum(d1 * mh)
        dx1, dmix, gs["norm_ffn_pre"][l], gs["norm_mix_post"][l] = rowwise(
            t + "mix_post_ffn_pre", pre_post_bwd, m, ins=[(dx2, D, 0), (dh2, D, 0), (sv["x1"], D, 0), (sv["mix"], D, 0)],
            consts=[g_fpre, g_post], outs=[(D, F32), (D, BF16)], accs=[(1, D), (1, D)])
        wgrad("w_out", t + "w_out", sv["merged"], dmix)
        dmerged = mm(t + "out_proj_dx", dmix, W["w_out"], "nt", BF16)

        def merge_bwd(dm, a, b, gab):
            dm = dm.astype(F32)
            sa, sb = _sigmoid(gab[:, :D]), _sigmoid(gab[:, D:])
            dgab = jnp.concatenate([dm * a * sa * (1.0 - sa), dm * b * sb * (1.0 - sb)], axis=1)
            return dm * sa, dm * sb, dgab
        dpa, dpb, dz = rowwise(
            t + "merge", merge_bwd, m, ins=[(dmerged, D, 0), (sv["pa"], D, 0), (sv["pb"], D, 0), (sv["z"], 2 * D, 3)],
            outs=[(D, BF16), (D, BF16)], alias_outs=[(jax.ShapeDtypeStruct((m, N_IN), BF16), 2 * D, 3)])
        wgrad("w_a", t + "w_a", sv["a_out"], dpa)
        wgrad("w_b", t + "w_b", sv["b_out"], dpb)
        da = mm(t + "proj_a_dx", dpa, W["w_a"], "nt")
        db = mm(t + "proj_b_dx", dpb, W["w_b"], "nt")

        dz, dsw, dbias, gs["sg_ln_g"][l], gs["sg_ln_b"][l] = rowwise(
            t + "sgu", _sg_bwd, m, ins=[(db, SG_WIDTH, 0), (sv["z"], SG_WIDTH, ZU), (sv["z"], SG_WIDTH, ZV)],
            consts=[sv["sg_w"], sv["sg_bias"], ln_g, ln_b], alias_outs=[(dz, 2 * SG_WIDTH, 5)],
            accs=[(SG_WIDTH // SG_GROUP_DIM * SG_CHUNK, SG_CHUNK), (SG_CHUNK, SG_WIDTH), (1, SG_WIDTH), (1, SG_WIDTH)])
        gs["sg_w"][l] = dsw.reshape(1, SG_WIDTH // SG_GROUP_DIM, SG_CHUNK, SG_CHUNK)
        gs["sg_b"][l] = dbias.reshape(SG_CHUNK, SG_WIDTH // SG_GROUP_DIM, SG_GROUP_DIM).sum(-1).T[None]

        d_o, dz, gs["hg_norm"][l] = rowwise(
            t + "hgrn_post", _hg_post_bwd, m, ins=[(da, D, 0), (sv["o_f"], D, 0), (sv["o_b"], D, 0), (sv["z"], D, ZG)],
            consts=[hg_g], outs=[(D, BF16)], alias_outs=[(dz, D, ZG)], accs=[(1, D)])
        if l == 0:
            part = {n: (g if not isinstance(g, list) else jnp.concatenate(
                [jnp.zeros((1,) + g[1].shape[1:], F32) if gl is None else gl for gl in g], axis=0))
                for n, g in gs.items()}
            plan.early_small(_pack([part[n].reshape(S[n].shape) for n in SMALL]))
        (dq_f, dv_f, dq_b, dv_b, dzf_f, dzf_b, dlb_f[l], dlb_b[l]), extra = hgrn_bwd(
            t + "hgrn", sv["z"], d_o, sv["s_f"], sv["s_b"], lbf, lbb, exch=plan.exch(t + "hgrn"), unroll=(l == 0))
        plan.done(t + "hgrn", extra)

        def combine(dqf, dqb, dvf, dvb, dff_, dfb_, zq):
            dq = dqf.astype(F32) + dqb.astype(F32)
            dv = dvf.astype(F32) + dvb.astype(F32)
            return (jnp.concatenate([(dq * _silu_grad(zq)).astype(BF16), dff_, dfb_, dv.astype(BF16)], axis=1),)
        (dz,) = rowwise(t + "hgrn_combine", combine, m,
                        ins=[(dq_f, D, 0), (dq_b, D, 0), (dv_f, D, 0), (dv_b, D, 0), (dzf_f, D, 0), (dzf_b, D, 0),
                             (sv["z"], D, ZQ)], alias_outs=[(dz, 4 * D, 0)], tm=128)
        wgrad("w_in", t + "w_in", sv["h"], dz)
        ex = plan.exch(t + "in_proj_dx")
        dh = mm(t + "in_proj_dx", dz, W["w_in"], "nt", tm=tm, exch=ex)
        if ex is not None:
            dh, extra = dh
            plan.done(t + "in_proj_dx", extra)

        def pre_bwd(d1, dhv, xv, g):
            xh, r = _rms(xv)
            return d1 + _rms_bwd(dhv, xh, r, g), _colsum(dhv * xh)
        dx, gs["norm_mix_pre"][l] = rowwise(t + "pre_norm", pre_bwd, m, ins=[(dx1, D, 0), (dh, D, 0), (sv["x"], D, 0)],
                                            consts=[g_pre], outs=[(D, F32)], accs=[(1, D)])
        saved[l] = None
        if l == DEPTH - 1:
            none = jnp.zeros((1, D), F32)
            gs["lb_gamma_fwd"], gs["lb_gamma_bwd"] = lower_bounds_bwd(
                "lower_bounds_bwd", S["lb_gamma_fwd"], S["lb_gamma_bwd"], jnp.concatenate([none, dlb_f[l]], axis=0),
                jnp.concatenate([none, dlb_b[l]], axis=0))

    small ={n: (g if not isinstance(g, list) else jnp.concatenate(g, axis=0)).reshape(S[n].shape)
             for n, g in gs.items()}
    return loss_cols, dx, small


def cast_pad(name, w, rows_p, cols_p):
    _, r, c = w.shape

    def body(w_ref, o_ref):
        if (rows_p, cols_p) != (r, c):
            o_ref[...] = jnp.zeros(o_ref.shape, BF16)
        o_ref[0:r, 0:c] = w_ref[...].astype(BF16)

    return pl.pallas_call(
        body, name=name, grid=(DEPTH,), in_specs=[pl.BlockSpec((None, r, c), lambda l: (l, 0, 0))],
        out_specs=pl.BlockSpec((None, rows_p, cols_p), lambda l: (l, 0, 0)),
        out_shape=jax.ShapeDtypeStruct((DEPTH, rows_p, cols_p), BF16), compiler_params=_params(("parallel",)),
    )(w)


def _shard_shape(n, shape):
    axis, size, _, _ = LAYOUT[n]
    _, r, c = shape
    return (size, c) if axis == 0 else (r, size)


class DistPlan:
    def __init__(self, shards):
        self.shards = shards
        self.W = [dict() for _ in range(DEPTH)]
        self.grads = [dict() for _ in range(DEPTH)]
        self.slots = [dict() for _ in range(DEPTH)]
        rest = [n for n in BIG if n != "w_in"]
        self.schedule = {
            "l0_in_proj": ("gather", [(0, n) for n in rest]),
            "l0_hgrn_fwd": ("gather", [(1, n) for n in BIG]),
            "l0_bwd_hgrn": ("scatter", [(1, n) for n in BIG] + [(0, n) for n in rest]),
            "l0_bwd_in_proj_dx": ("scatter", [(0, "w_in")]),
        }
        self.pending = {}
        self.small_part = self.small_slots = None
        self.done("start", exchange("gather_l0_w_in", self._gather("start", [(0, "w_in")])))

    def _gather(self, host, parts):
        srcs, dsts, items, keys = [], [], [], []
        for layer, n in parts:
            axis, size, dst, base = LAYOUT[n]
            if (layer, dst) not in keys:
                keys.append((layer, dst))
                dsts.append((GATHERED[dst], BF16))
            srcs.append(self.shards[n])
            items.append(("gather", len(srcs) - 1, keys.index((layer, dst)), axis, size, base, layer))
        self.pending[host] = ("gather", keys)
        return Exchange(srcs, dsts, items)

    def _scatter(self, host, parts):
        srcs, src_keys, dsts, items = [], [], [], []
        for layer, n in parts:
            axis, size, dst, base = LAYOUT[n]
            if (layer, dst) not in src_keys:
                src_keys.append((layer, dst))
                srcs.append(self.grads[layer][dst])
            dsts.append(((NDEV,) + _shard_shape(n, self.shards[n].shape), BF16))
            items.append(("scatter", src_keys.index((layer, dst)), len(dsts) - 1, axis, size, base, None))
        keys = list(parts)
        if host == "l0_bwd_hgrn" and self.small_part is not None:
            srcs.append(self.small_part)
            dsts.append(((NDEV,) + self.small_part.shape, F32))
            items.append(("copies", len(srcs) - 1, len(dsts) - 1, 0, 0, 0, None))
            keys.append(("small", None))
        self.pending[host] = ("scatter", keys)
        return Exchange(srcs, dsts, items)

    def early_small(self, packed):
        self.small_part = packed

    def exch(self, host):
        if host not in self.schedule:
            return None
        kind, parts = self.schedule[host]
        return self._gather(host, parts) if kind == "gather" else self._scatter(host, parts)

    def done(self, host, outs):
        if host not in self.pending:
            return
        kind, keys = self.pending.pop(host)
        for (layer, n), arr in zip(keys, outs):
            if layer == "small":
                self.small_slots = arr
            else:
                (self.W if kind == "gather" else self.slots)[layer][n] = arr


def adam(name, w, m_, v_, tr, g=None, slots=None):
    L, r, c = w.shape
    assert r % tr == 0
    nt = r // tr
    n_s = 0 if slots is None else L

    def body(*refs):
        s_refs = refs[:n_s]
        g_ref = refs[n_s] if g is not None else None
        w_ref, m_ref, v_ref, g_out, d_out, m_out, v_out = refs[n_s + (g is not None):]

        def update(gv):
            if g_ref is not None:
                gv = gv + g_ref[...] if gv is not None else g_ref[...]
            m2 = B1 * m_ref[...] + (1.0 - B1) * gv
            v2 = B2 * v_ref[...] + (1.0 - B2) * (gv * gv)
            m_hat = m2 / (1.0 - B1 ** STEP)
            v_hat = v2 / (1.0 - B2 ** STEP)
            g_out[...] = gv
            d_out[...] = -LR * (m_hat / (jnp.sqrt(v_hat) + AEPS) + WD * w_ref[...])
            m_out[...] = m2
            v_out[...] = v2

        if slots is None:
            update(None)
            return
        for layer, s_ref in enumerate(s_refs):
            @pl.when(pl.program_id(0) == layer)
            def _():
                gv = s_ref[0][:, :c].astype(F32)
                for j in range(1, NDEV):
                    gv = gv + s_ref[j][:, :c].astype(F32)
                update(gv)

    spec = pl.BlockSpec((None, tr, c), lambda l, i: (l, i, 0))
    arrs, specs = [], []
    if slots is not None:
        assert len(slots) == L and L <= 2
        arrs = list(slots)
        cp = slots[0].shape[2]
        specs = [pl.BlockSpec((NDEV, tr, cp), lambda l, i: (0, i * (1 - l) + (nt - 1) * l, 0)),
                 pl.BlockSpec((NDEV, tr, cp), lambda l, i: (0, i * l, 0))][:L]
    if g is not None:
        arrs.append(g)
        specs.append(spec)
    shp = jax.ShapeDtypeStruct(w.shape, F32)
    return pl.pallas_call(
        body, name=name, grid=(L, nt), in_specs=specs + [spec, spec, spec], out_specs=[spec] * 4,
        out_shape=[shp] * 4, compiler_params=_params(("arbitrary", "arbitrary")),
    )(*arrs, w, m_, v_)


def _pack(arrs):
    parts = []
    for a in arrs:
        a2 = a.reshape(-1, D)
        parts.append(jnp.pad(a2, ((0, -a2.shape[0] % 8), (0, 0))))
    return jnp.concatenate(parts, axis=0)


def _unpack(buf, shapes):
    out, off = [], 0
    for s in shapes:
        rows = 1
        for d_ in s:
            rows *= d_
        rows //= D
        out.append(buf[off:off + rows].reshape(s))
        off += rows + (-rows % 8)
    return out


def kernel(x, p, norm_mix_pre, w_in, lb_gamma_fwd, lb_gamma_bwd, hg_norm, sg_w, sg_b, sg_ln_g, sg_ln_b, w_a, w_b, w_out, norm_mix_post, norm_ffn_pre, w_gate, w_up, w_down, norm_ffn_post, w_ple, w_ple_gate, loss_target, m_norm_mix_pre, m_w_in, m_lb_gamma_fwd, m_lb_gamma_bwd, m_hg_norm, m_sg_w, m_sg_b, m_sg_ln_g, m_sg_ln_b, m_w_a, m_w_b, m_w_out, m_norm_mix_post, m_norm_ffn_pre, m_w_gate, m_w_up, m_w_down, m_norm_ffn_post, m_w_ple, m_w_ple_gate, v_norm_mix_pre, v_w_in, v_lb_gamma_fwd, v_lb_gamma_bwd, v_hg_norm, v_sg_w, v_sg_b, v_sg_ln_g, v_sg_ln_b, v_w_a, v_w_b, v_w_out, v_norm_mix_post, v_norm_ffn_pre, v_w_gate, v_w_up, v_w_down, v_norm_ffn_post, v_w_ple, v_w_ple_gate):
    a = dict(zip(INPUTS, (x, p, norm_mix_pre, w_in, lb_gamma_fwd, lb_gamma_bwd, hg_norm, sg_w, sg_b, sg_ln_g, sg_ln_b, w_a, w_b, w_out, norm_mix_post, norm_ffn_pre, w_gate, w_up, w_down, norm_ffn_post, w_ple, w_ple_gate, loss_target, m_norm_mix_pre, m_w_in, m_lb_gamma_fwd, m_lb_gamma_bwd, m_hg_norm, m_sg_w, m_sg_b, m_sg_ln_g, m_sg_ln_b, m_w_a, m_w_b, m_w_out, m_norm_mix_post, m_norm_ffn_pre, m_w_gate, m_w_up, m_w_down, m_norm_ffn_post, m_w_ple, m_w_ple_gate, v_norm_mix_pre, v_w_in, v_lb_gamma_fwd, v_lb_gamma_bwd, v_hg_norm, v_sg_w, v_sg_b, v_sg_ln_g, v_sg_ln_b, v_w_a, v_w_b, v_w_out, v_norm_mix_post, v_norm_ffn_pre, v_w_gate, v_w_up, v_w_down, v_norm_ffn_post, v_w_ple, v_w_ple_gate)))
    m = x.shape[1]

    shards = {n: cast_pad("cast_" + n, a[n], *_shard_shape(n, a[n].shape)) for n in BIG}
    plan = DistPlan(shards)
    loss_cols, dx, gs = local_step(x[0], p[:, 0], loss_target[0], {n: a[n] for n in SMALL}, plan)
    loss = lax.psum(jnp.sum(loss_cols) * (0.5 / D), ("x", "y", "c"))

    small_shapes = [a[n].shape for n in SMALL]
    rows = plan.small_slots.shape[1]
    late = allreduce_small("allreduce_small", jnp.pad(gs["norm_mix_pre"][0:1], ((0, 7), (0, 0))))
    g_late = jnp.pad(late, ((0, rows - 8), (0, 0)))[None]

    res = {}
    row_tiles = {"w_in": 128, "w_a": 128, "w_b": 512, "w_out": 128, "w_gate": 128, "w_up": 128, "w_down": 88,
                 "w_ple": 256, "w_ple_gate": 128}
    for n in BIG:
        res[n] = adam("adam_" + n, a[n], a["m_" + n], a["v_" + n], row_tiles[n],
                      slots=[plan.slots[l][n] for l in range(DEPTH)])
    packed = [_pack([a[pre + n] for n in SMALL])[None] for pre in ("", "m_", "v_")]
    small_res = adam("adam_small", packed[0], packed[1], packed[2], rows // 2, g=g_late, slots=[plan.small_slots])
    small_res = [_unpack(r_[0], small_shapes) for r_ in small_res]
    for i, n in enumerate(SMALL):
        res[n] = tuple(small_res[k][i] for k in range(4))

    outs = [loss, dx.reshape(1, m, D)]
    for k in range(4):
        outs += [res[n][k] for n in WEIGHTS]
    return tuple(outs)
```

```python
import jax
import jax.numpy as jnp
from jax import lax
from jax.experimental import pallas as pl
from jax.experimental.pallas import tpu as pltpu

F32 = jnp.float32
BF16 = jnp.bfloat16

D = 1024
N_IN = 8192
HEADS = 8
HEAD_DIM = 128
SG_CHUNK = 128
SG_WIDTH = 512
SG_GROUP_DIM = 64
FFN = 2816
PLE_DIM = 256
EPS = 1e-6
DEPTH = 2
ZQ, ZFF, ZFB, ZI, ZG, GA, GB = 0, 1, 2, 3, 4, 6, 7
ZU, ZV = 10, 11

NDEV = 8
FFN_SHARD = FFN // NDEV
FFN_SHARD_PAD = 384
FFN_PAD = NDEV * FFN_SHARD_PAD

LR, B1, B2, AEPS, WD, STEP = 0.001, 0.9, 0.999, 1e-08, 0.01, 10

ROW_TILE = 256
HG_CHUNK = 64
HG_BLOCK_FWD = 256
HG_BLOCK_BWD = 128
EXP_CLAMP = 80.0
TINY = float(jnp.finfo(jnp.float32).tiny)
VMEM_LIMIT = 56 * 1024 * 1024

BIG = ["w_in", "w_a", "w_b", "w_out", "w_gate", "w_up", "w_down", "w_ple", "w_ple_gate"]
SMALL = ["norm_mix_pre", "lb_gamma_fwd", "lb_gamma_bwd", "hg_norm", "sg_w", "sg_b", "sg_ln_g", "sg_ln_b",
         "norm_mix_post", "norm_ffn_pre", "norm_ffn_post"]
WEIGHTS = ["norm_mix_pre", "w_in", "lb_gamma_fwd", "lb_gamma_bwd", "hg_norm", "sg_w", "sg_b", "sg_ln_g", "sg_ln_b",
           "w_a", "w_b", "w_out", "norm_mix_post", "norm_ffn_pre", "w_gate", "w_up", "w_down", "norm_ffn_post",
           "w_ple", "w_ple_gate"]
INPUTS = (["x", "p"] + WEIGHTS + ["loss_target"] + ["m_" + n for n in WEIGHTS] + ["v_" + n for n in WEIGHTS])
LAYOUT = {
    "w_in": (1, 1024, "w_in", 0), "w_a": (0, 128, "w_a", 0), "w_b": (1, 128, "w_b", 0),
    "w_out": (0, 128, "w_out", 0), "w_gate": (1, FFN_SHARD_PAD, "w_gu", 0),
    "w_up": (1, FFN_SHARD_PAD, "w_gu", FFN_PAD), "w_down": (0, FFN_SHARD_PAD, "w_down", 0),
    "w_ple": (1, 128, "w_ple", 0), "w_ple_gate": (0, 128, "w_ple_gate", 0),
}
GATHERED = {"w_in": (D, N_IN), "w_a": (D, D), "w_b": (SG_WIDTH, D), "w_out": (D, D), "w_gu": (D, 2 * FFN_PAD),
            "w_down": (FFN_PAD, D), "w_ple": (PLE_DIM, D), "w_ple_gate": (D, D)}


def _params(sem):
    return pltpu.CompilerParams(dimension_semantics=sem, vmem_limit_bytes=VMEM_LIMIT)


def _dot(a, b):
    return lax.dot_general(a, b, (((1,), (0,)), ((), ())), preferred_element_type=F32)


def _dot_nt(a, b):
    return lax.dot_general(a, b, (((1,), (1,)), ((), ())), preferred_element_type=F32)


def _dot_tn(a, b):
    return lax.dot_general(a, b, (((0,), (0,)), ((), ())), preferred_element_type=F32)


def _sigmoid(x):
    return jax.nn.sigmoid(x)


def _silu(x):
    return x * _sigmoid(x)


def _silu_grad(x):
    s = _sigmoid(x)
    return s * (1.0 + x * (1.0 - s))


def _gelu(x):
    return 0.5 * x * (1.0 + lax.erf(x * 0.7071067811865476))


def _gelu_grad(x):
    return 0.5 * (1.0 + lax.erf(x * 0.7071067811865476)) + x * jnp.exp(-0.5 * x * x) * 0.3989422804014327


def _mean(x):
    return jnp.mean(x, axis=-1, keepdims=True)


def _colsum(x):
    return jnp.sum(x, axis=0, keepdims=True)


def _rms(x):
    r = lax.rsqrt(_mean(x * x) + EPS)
    return x * r, r


def _rms_bwd(dy, xh, r, g):
    dyg = dy * g
    return r * (dyg - xh * _mean(dyg * xh))


MESH = pl.DeviceIdType.MESH
ANY = pl.BlockSpec(memory_space=pl.ANY)


def _slab(ref, axis, start, size):
    idx = [slice(None)] * 2
    idx[axis] = pl.ds(start, size)
    return ref.at[tuple(idx)]


class Exchange:
    def __init__(self, srcs, dsts, items):
        self.srcs, self.dsts, self.items = list(srcs), list(dsts), list(items)

    def specs(self):
        n = len(self.items)
        sems = [pltpu.SemaphoreType.DMA((n * (NDEV - 1),)), pltpu.SemaphoreType.DMA((n * (NDEV - 1),)),
                pltpu.SemaphoreType.DMA((n,))]
        return ([ANY] * len(self.srcs), [ANY] * len(self.dsts),
                [jax.ShapeDtypeStruct(s, dt) for (s, dt) in self.dsts], sems)

    def copies(self, src, dst, send_sem, recv_sem, loc_sem):
        x, y, c = lax.axis_index("x"), lax.axis_index("y"), lax.axis_index("c")
        me = 4 * x + 2 * y + c
        starts, waits = [], []
        for n, (kind, si, di, axis, size, base, layer) in enumerate(self.items):
            def views(to_dev, from_dev):
                if kind == "gather":
                    return (src[si].at[layer],
                            _slab(dst[di], axis, base + pl.multiple_of(from_dev * size, 128), size))
                if kind == "copies":
                    return src[si], dst[di].at[from_dev]
                return _slab(src[si], axis, base + pl.multiple_of(to_dev * size, 128), size), dst[di].at[from_dev]

            s_own, d_own = views(me, me)
            own = pltpu.make_async_copy(s_own, d_own, loc_sem.at[n])
            starts.append(own)
            waits.append(own)
            for k in range(1, NDEV):
                px = 1 - x if k & 4 else x
                py = 1 - y if k & 2 else y
                pc = 1 - c if k & 1 else c
                peer = 4 * px + 2 * py + pc
                s_out, _ = views(peer, me)
                _, d_in = views(me, peer)
                sem = n * (NDEV - 1) + k - 1
                starts.append(pltpu.make_async_remote_copy(s_out, d_own, send_sem.at[sem], recv_sem.at[sem],
                                                           device_id=(px, py, pc), device_id_type=MESH))
                waits.append(pltpu.make_async_remote_copy(s_out, d_in, send_sem.at[sem], recv_sem.at[sem],
                                                          device_id=(px, py, pc), device_id_type=MESH))
        return starts, waits


def exchange(name, exch):
    e_in, e_out, e_shape, e_scr = exch.specs()
    ns, nd = len(e_in), len(e_out)

    def body(*refs):
        starts, waits = exch.copies(refs[:ns], refs[ns:ns + nd], *refs[ns + nd:])
        for cp in starts:
            cp.start()
        for cp in waits:
            cp.wait()

    return pl.pallas_call(body, name=name, in_specs=e_in, out_specs=e_out, out_shape=e_shape, scratch_shapes=e_scr,
                          compiler_params=pltpu.CompilerParams(has_side_effects=True))(*exch.srcs)


def hosted_call(body, exch, name, grid, in_specs, out_specs, out_shape, scratch_shapes, operands, semantics,
                aliases=None):
    aliases = aliases or {}
    if exch is None:
        res = pl.pallas_call(body, name=name, grid=grid, in_specs=in_specs, out_specs=out_specs, out_shape=out_shape,
                             scratch_shapes=scratch_shapes, input_output_aliases=aliases,
                             compiler_params=_params(semantics))(*operands)
        return list(res), []
    n_in, n_out, n_scr = len(in_specs), len(out_specs), len(scratch_shapes)
    e_in, e_out, e_shape, e_scr = exch.specs()
    ns, nd = len(e_in), len(e_out)

    def at_step(last):
        cond = None
        for ax, n in enumerate(grid):
            c = pl.program_id(ax) == (n - 1 if last else 0)
            cond = c if cond is None else jnp.logical_and(cond, c)
        return cond

    def wrapped(*refs):
        ins, src = refs[:n_in], refs[n_in:n_in + ns]
        o0 = n_in + ns
        outs, dst = refs[o0:o0 + n_out], refs[o0 + n_out:o0 + n_out + nd]
        s0 = o0 + n_out + nd
        scr, sems = refs[s0:s0 + n_scr], refs[s0 + n_scr:]

        @pl.when(at_step(False))
        def _():
            for cp in exch.copies(src, dst, *sems)[0]:
                cp.start()

        body(*ins, *outs, *scr)

        @pl.when(at_step(True))
        def _():
            for cp in exch.copies(src, dst, *sems)[1]:
                cp.wait()

    res = pl.pallas_call(
        wrapped, name=name, grid=grid, in_specs=list(in_specs) + e_in, out_specs=list(out_specs) + e_out,
        out_shape=list(out_shape) + e_shape, scratch_shapes=list(scratch_shapes) + e_scr,
        input_output_aliases=aliases,
        compiler_params=pltpu.CompilerParams(dimension_semantics=("arbitrary",) * len(grid),
                                             vmem_limit_bytes=VMEM_LIMIT, has_side_effects=True),
    )(*operands, *exch.srcs)
    return list(res[:n_out]), list(res[n_out:])


def allreduce_small(name, part):
    rows, width = part.shape

    def body(p_ref, o_ref, buf, send_sem, recv_sem):
        x, y, c = lax.axis_index("x"), lax.axis_index("y"), lax.axis_index("c")
        me = 4 * x + 2 * y + c
        buf[me] = p_ref[...]
        waits = []
        for k in range(1, NDEV):
            px = 1 - x if k & 4 else x
            py = 1 - y if k & 2 else y
            pc = 1 - c if k & 1 else c
            peer = 4 * px + 2 * py + pc
            pltpu.make_async_remote_copy(p_ref, buf.at[me], send_sem.at[k - 1], recv_sem.at[k - 1],
                                         device_id=(px, py, pc), device_id_type=MESH).start()
            waits.append(pltpu.make_async_remote_copy(p_ref, buf.at[peer], send_sem.at[k - 1], recv_sem.at[k - 1],
                                                      device_id=(px, py, pc), device_id_type=MESH))
        for w in waits:
            w.wait()
        acc = buf[0]
        for j in range(1, NDEV):
            acc = acc + buf[j]
        o_ref[...] = acc

    vmem = pl.BlockSpec(memory_space=pltpu.VMEM)
    return pl.pallas_call(
        body, name=name, in_specs=[vmem], out_specs=vmem, out_shape=jax.ShapeDtypeStruct((rows, width), F32),
        scratch_shapes=[pltpu.VMEM((NDEV, rows, width), F32), pltpu.SemaphoreType.DMA((NDEV - 1,)),
                        pltpu.SemaphoreType.DMA((NDEV - 1,))],
        compiler_params=pltpu.CompilerParams(vmem_limit_bytes=VMEM_LIMIT, has_side_effects=True),
    )(part)


def rowwise(name, fn, m, ins=(), consts=(), outs=(), alias_outs=(), accs=(), tm=ROW_TILE):
    tm = min(tm, m)
    n_in, n_c, n_o, n_al, n_ac = len(ins), len(consts), len(outs), len(alias_outs), len(accs)
    held = [a for (a, _, _) in alias_outs if not isinstance(a, jax.ShapeDtypeStruct)]
    n_held = len(held)

    def body(*refs):
        in_refs = refs[:n_in + n_c]
        out_refs = refs[n_in + n_c + n_held:]
        vals = fn(*[r[...] for r in in_refs])
        if not isinstance(vals, (tuple, list)):
            vals = (vals,)
        for r, v in zip(out_refs[:n_o + n_al], vals[:n_o + n_al]):
            r[...] = v.astype(r.dtype)
        if n_ac:
            acc_refs = out_refs[n_o + n_al:]

            @pl.when(pl.program_id(0) == 0)
            def _():
                for r in acc_refs:
                    r[...] = jnp.zeros(r.shape, F32)

            for r, v in zip(acc_refs, vals[n_o + n_al:]):
                r[...] += v

    def col(cb):
        return lambda i: (i, cb)

    in_specs = [pl.BlockSpec((tm, w), col(cb)) for (_, w, cb) in ins]
    in_specs += [pl.BlockSpec(c.shape, lambda i, nd=c.ndim: (0,) * nd) for c in consts]
    in_specs += [ANY for _ in held]
    out_shape = [jax.ShapeDtypeStruct((m, w), dt) for (w, dt) in outs]
    out_specs = [pl.BlockSpec((tm, w), col(0)) for (w, _) in outs]
    out_shape += [jax.ShapeDtypeStruct(a.shape, a.dtype) for (a, _, _) in alias_outs]
    out_specs += [pl.BlockSpec((tm, w), col(cb)) for (_, w, cb) in alias_outs]
    out_shape += [jax.ShapeDtypeStruct(s, F32) for s in accs]
    out_specs += [pl.BlockSpec(s, lambda i: (0, 0)) for s in accs]
    aliases, k_in = {}, n_in + n_c
    for k, (a, _, _) in enumerate(alias_outs):
        if not isinstance(a, jax.ShapeDtypeStruct):
            aliases[k_in] = n_o + k
            k_in += 1
    return pl.pallas_call(
        body, name=name, grid=(m // tm,), in_specs=in_specs, out_specs=out_specs, out_shape=out_shape,
        input_output_aliases=aliases,
        compiler_params=_params(("arbitrary",) if n_ac else ("parallel",)),
    )(*[a for (a, _, _) in ins], *consts, *held)


def _operand(arr, bshape, imap):
    if isinstance(arr, tuple):
        arr, lead = arr
        return arr, pl.BlockSpec((None,) + bshape, lambda *g: (lead,) + imap(*g))
    return arr, pl.BlockSpec(bshape, imap)


def _shape2(arr):
    return arr[0].shape[1:] if isinstance(arr, tuple) else arr.shape


def mm(name, a, b, mode, out_dtype=F32, tm=1024, tn=1024, tk=1024, exch=None):
    sa, sb = _shape2(a), _shape2(b)
    if mode == "nn":
        (M, K), N = sa, sb[1]
    elif mode == "nt":
        (M, K), N = sa, sb[0]
    else:
        (K, M), N = sa, sb[1]
    tm, tn, tk = min(tm, M), min(tn, N), min(tk, K)
    assert M % tm == 0 and N % tn == 0 and K % tk == 0, (name, M, N, K)
    nk = K // tk
    if mode == "nn":
        a_arr, a_spec = _operand(a, (tm, tk), lambda i, j, k: (i, k))
        b_arr, b_spec = _operand(b, (tk, tn), lambda i, j, k: (k, j))
        dot = _dot
    elif mode == "nt":
        a_arr, a_spec = _operand(a, (tm, tk), lambda i, j, k: (i, k))
        b_arr, b_spec = _operand(b, (tn, tk), lambda i, j, k: (j, k))
        dot = _dot_nt
    else:
        a_arr, a_spec = _operand(a, (tk, tm), lambda i, j, k: (k, i))
        b_arr, b_spec = _operand(b, (tk, tn), lambda i, j, k: (k, j))
        dot = _dot_tn

    def body(a_ref, b_ref, o_ref, *acc):
        part = dot(a_ref[...].astype(BF16), b_ref[...].astype(BF16))
        if nk == 1:
            o_ref[...] = part.astype(o_ref.dtype)
            return
        acc_ref, k = acc[0], pl.program_id(2)

        @pl.when(k == 0)
        def _():
            acc_ref[...] = part

        @pl.when(k > 0)
        def _():
            acc_ref[...] += part

        @pl.when(k == nk - 1)
        def _():
            o_ref[...] = acc_ref[...].astype(o_ref.dtype)

    outs, extra = hosted_call(
        body, exch, name, (M // tm, N // tn, nk), [a_spec, b_spec], [pl.BlockSpec((tm, tn), lambda i, j, k: (i, j))],
        [jax.ShapeDtypeStruct((M, N), out_dtype)], [pltpu.VMEM((tm, tn), F32)] if nk > 1 else [], [a_arr, b_arr],
        ("parallel", "parallel", "arbitrary"))
    return outs[0] if exch is None else (outs[0], extra)


def _cumsum_rows(x):
    n = x.shape[0]
    row = lax.broadcasted_iota(jnp.int32, x.shape, 0)
    s = 1
    while s < n:
        x = x + jnp.where(row >= s, pltpu.roll(x, s, 0), 0.0)
        s *= 2
    return x


def _hg_prep(zq, zf, lb, reverse):
    n = zq.shape[0]
    q = _silu(zq)
    sig = _sigmoid(zf)
    sn = 1.0 - sig
    f = lb + (1.0 - lb) * sig
    k = (1.0 - lb) * sn
    g = jnp.log(jnp.maximum(f, TINY))
    b = _cumsum_rows(g)
    if reverse:
        b = b[n - 1:n] - b + g
    b_last = b[0:1] if reverse else b[n - 1:n]
    b_ref = b[n // 2:n // 2 + 1]
    e1 = jnp.exp(b)
    e2 = jnp.exp(jnp.clip(b - b_ref, -EXP_CLAMP, EXP_CLAMP))
    e3 = jnp.exp(jnp.clip(b_ref - b, -EXP_CLAMP, EXP_CLAMP))
    e4 = jnp.exp(b_last - b)
    return dict(q=q, k=k, sig=sig, sn=sn, f=f, e1=e1, e2=e2, e3=e3, e4=e4, e_last=jnp.exp(b_last),
                qe=(q * e1).astype(BF16), qt=(q * e2).astype(BF16), kt=(k * e3).astype(BF16),
                ks=(k * e4).astype(BF16))


def _hg_mask(n, reverse):
    t = lax.broadcasted_iota(jnp.int32, (n, n), 0)
    s = lax.broadcasted_iota(jnp.int32, (n, n), 1)
    return (s >= t) if reverse else (s <= t)


def hgrn_fwd(name, z, lb_f, lb_b, exch=None, unroll=False):
    m = z.shape[0]
    C, T = HG_CHUNK, min(HG_BLOCK_FWD, m)
    nb, cpb = m // T, T // C

    def body(zq_f, zf_f, zi_f, zq_b, zf_b, zi_b, lbf_ref, lbb_ref, of_ref, ob_ref, sf_ref, sb_ref, st_ref):
        @pl.when(pl.program_id(0) == 0)
        def _():
            st_ref[...] = jnp.zeros(st_ref.shape, F32)

        dirs = ((zq_f, zf_f, zi_f, lbf_ref, of_ref, sf_ref), (zq_b, zf_b, zi_b, lbb_ref, ob_ref, sb_ref))

        def chunk(ci, carry):
            work = []
            for d, (zq, zf, zi, lb_ref, o_ref, s_ref) in enumerate(dirs):
                cc = ci if d == 0 else cpb - 1 - ci
                rows = pl.ds(pl.multiple_of(cc * C, C), C)
                pre = _hg_prep(zq[rows, :], zf[rows, :], lb_ref[...], d == 1)
                v = zi[rows, :].astype(BF16)
                work.append((cc, rows, pre, v, [st_ref[d, h] for h in range(HEADS)]))
            heads = [(d, h, slice(h * HEAD_DIM, (h + 1) * HEAD_DIM)) for d in range(2) for h in range(HEADS)]
            first = {}
            for d, h, sl in heads:
                _, _, pre, v, sts = work[d]
                first[d, h] = (_dot_nt(pre["qt"][:, sl], pre["kt"][:, sl]),
                               _dot_nt(pre["qe"][:, sl], sts[h].astype(BF16)),
                               _dot_tn(v[:, sl], pre["ks"][:, sl]))
            results = [([], []), ([], [])]
            for d, h, sl in heads:
                _, _, pre, v, sts = work[d]
                scores, o_inter, st_add = first[d, h]
                a = jnp.where(_hg_mask(C, d == 1), scores, 0.0).astype(BF16)
                results[d][0].append(o_inter + _dot(a, v[:, sl]))
                results[d][1].append(sts[h] * pre["e_last"][:, sl] + st_add)
            results = [(jnp.concatenate(o_parts, axis=1), new_sts) for (o_parts, new_sts) in results]
            for d, (zq, zf, zi, lb_ref, o_ref, s_ref) in enumerate(dirs):
                cc, rows, _, _, sts = work[d]
                o_ref[rows, :] = results[d][0]
                for h in range(HEADS):
                    s_ref[cc, h] = sts[h]
                    st_ref[d, h] = results[d][1][h]
            return carry

        lax.fori_loop(0, cpb, chunk, 0, unroll=unroll)

    def zspec(cb, rev):
        return pl.BlockSpec((T, D), (lambda i: (nb - 1 - i, cb)) if rev else (lambda i: (i, cb)))

    def sspec(rev):
        shape = (cpb, HEADS, HEAD_DIM, HEAD_DIM)
        return pl.BlockSpec(shape, (lambda i: (nb - 1 - i, 0, 0, 0)) if rev else (lambda i: (i, 0, 0, 0)))

    lbspec = pl.BlockSpec((1, D), lambda i: (0, 0))
    states = jax.ShapeDtypeStruct((m // C, HEADS, HEAD_DIM, HEAD_DIM), F32)
    outs, extra = hosted_call(
        body, exch, name, (nb,),
        [zspec(ZQ, False), zspec(ZFF, False), zspec(ZI, False), zspec(ZQ, True), zspec(ZFB, True), zspec(ZI, True),
         lbspec, lbspec],
        [zspec(0, False), zspec(0, True), sspec(False), sspec(True)],
        [jax.ShapeDtypeStruct((m, D), F32), jax.ShapeDtypeStruct((m, D), F32), states, states],
        [pltpu.VMEM((2, HEADS, HEAD_DIM, HEAD_DIM), F32)], [z, z, z, z, z, z, lb_f, lb_b], ("arbitrary",))
    return outs, extra


def hgrn_bwd(name, z, d_o, s_f, s_b, lb_f, lb_b, exch=None, unroll=False):
    m = z.shape[0]
    C, T = HG_CHUNK, min(HG_BLOCK_BWD, m)
    nb, cpb = m // T, T // C

    def body(zq_f, zf_f, zi_f, do_f, sf_ref, zq_b, zf_b, zi_b, do_b, sb_ref, lbf_ref, lbb_ref,
             dqf_ref, dvf_ref, dqb_ref, dvb_ref, dzf_f, dzf_b, dlbf_ref, dlbb_ref,
             dst_ref):
        @pl.when(pl.program_id(0) == 0)
        def _():
            dst_ref[...] = jnp.zeros(dst_ref.shape, F32)
            dlbf_ref[...] = jnp.zeros(dlbf_ref.shape, F32)
            dlbb_ref[...] = jnp.zeros(dlbb_ref.shape, F32)

        dirs = ((zq_f, zf_f, zi_f, do_f, sf_ref, lbf_ref, dqf_ref, dvf_ref, dzf_f, dlbf_ref),
                (zq_b, zf_b, zi_b, do_b, sb_ref, lbb_ref, dqb_ref, dvb_ref, dzf_b, dlbb_ref))

        def chunk(ci, carry):
            work = []
            for d, (zq, zf, zi, do_ref, s_ref, lb_ref, dq_ref, dv_ref, dzf_ref, dlb_ref) in enumerate(dirs):
                cc = cpb - 1 - ci if d == 0 else ci
                rows = pl.ds(pl.multiple_of(cc * C, C), C)
                lb = lb_ref[...]
                pre = _hg_prep(zq[rows, :], zf[rows, :], lb, d == 1)
                work.append((rows, lb, pre, zi[rows, :].astype(BF16), do_ref[rows, :],
                             [s_ref[cc, h] for h in range(HEADS)], [dst_ref[d, h] for h in range(HEADS)],
                             dlb_ref[...]))
            heads = [(d, h, slice(h * HEAD_DIM, (h + 1) * HEAD_DIM)) for d in range(2) for h in range(HEADS)]
            first = {}
            for d, h, sl in heads:
                _, _, pre, v, do, st_prevs, dsts, _ = work[d]
                dst16 = dsts[h].astype(BF16)
                first[d, h] = (_dot_nt(pre["qt"][:, sl], pre["kt"][:, sl]),
                               _dot_nt(do[:, sl], v[:, sl]),
                               _dot(do[:, sl], st_prevs[h].astype(BF16)),
                               _dot(v[:, sl], dst16),
                               _dot_nt(pre["ks"][:, sl], dst16),
                               _dot_tn(do[:, sl], pre["qe"][:, sl]))
            parts = [[[] for _ in range(6)] for _ in range(2)]
            for d, h, sl in heads:
                _, _, pre, v, do, st_prevs, dsts, _ = work[d]
                scores, dscores, dq_inter, dk_state, dv_state, dst_add = first[d, h]
                mask = _hg_mask(C, d == 1)
                a = jnp.where(mask, scores, 0.0).astype(BF16)
                da = jnp.where(mask, dscores, 0.0).astype(BF16)
                dq_p, dki_p, dks_p, dv_p, rr_p, new_dsts = parts[d]
                dq_p.append(_dot(da, pre["kt"][:, sl]) * pre["e2"][:, sl] + dq_inter * pre["e1"][:, sl])
                dki_p.append(_dot_tn(da, pre["qt"][:, sl]) * pre["e3"][:, sl])
                dks_p.append(dk_state * pre["e4"][:, sl])
                dv_p.append(_dot_tn(a, do[:, sl]) + dv_state)
                rr_p.append(pre["e_last"][:, sl] * _colsum(dsts[h] * st_prevs[h]))
                new_dsts.append(dsts[h] * pre["e_last"][:, sl] + dst_add)
            results = []
            for d, (rows, lb, pre, v, do, st_prevs, dsts, dlb_old) in enumerate(work):
                rev = d == 1
                dq_p, dki_p, dks_p, dv_p, rr_p, new_dsts = parts[d]
                dq, dki, dks, dv, rr = (jnp.concatenate(p_, axis=1) for p_ in (dq_p, dki_p, dks_p, dv_p, rr_p))
                x = pre["q"] * dq - pre["k"] * dki
                y = pre["k"] * dks
                if rev:
                    dg = _cumsum_rows(x - y) + _colsum(y) + rr
                else:
                    dg = _cumsum_rows(y - x) + (x - y) + _colsum(x) + rr
                inv_f = jnp.where(pre["f"] > TINY, 1.0 / pre["f"], 0.0)
                u = dg * inv_f - (dki + dks)
                results.append((dq, dv, (1.0 - lb) * pre["sig"] * pre["sn"] * u, dlb_old + _colsum(pre["sn"] * u),
                                new_dsts))
            for d, (zq, zf, zi, do_ref, s_ref, lb_ref, dq_ref, dv_ref, dzf_ref, dlb_ref) in enumerate(dirs):
                rows = work[d][0]
                dq, dv, dzf, dlb, new_dsts = results[d]
                dq_ref[rows, :] = dq.astype(dq_ref.dtype)
                dv_ref[rows, :] = dv.astype(dv_ref.dtype)
                dzf_ref[rows, :] = dzf.astype(dzf_ref.dtype)
                dlb_ref[...] = dlb
                for h in range(HEADS):
                    dst_ref[d, h] = new_dsts[h]
            return carry

        lax.fori_loop(0, cpb, chunk, 0, unroll=unroll)

    def rspec(cb, rev):
        return pl.BlockSpec((T, D), (lambda i: (i, cb)) if rev else (lambda i: (nb - 1 - i, cb)))

    def sspec(rev):
        shape = (cpb, HEADS, HEAD_DIM, HEAD_DIM)
        return pl.BlockSpec(shape, (lambda i: (i, 0, 0, 0)) if rev else (lambda i: (nb - 1 - i, 0, 0, 0)))

    lbspec = pl.BlockSpec((1, D), lambda i: (0, 0))
    half = jax.ShapeDtypeStruct((m, D), BF16)
    row = jax.ShapeDtypeStruct((1, D), F32)
    outs, extra = hosted_call(
        body, exch, name, (nb,),
        [rspec(ZQ, False), rspec(ZFF, False), rspec(ZI, False), rspec(0, False), sspec(False),
         rspec(ZQ, True), rspec(ZFB, True), rspec(ZI, True), rspec(0, True), sspec(True), lbspec, lbspec],
        [rspec(0, False), rspec(0, False), rspec(0, True), rspec(0, True), rspec(0, False), rspec(0, True),
         lbspec, lbspec],
        [half, half, half, half, half, half, row, row],
        [pltpu.VMEM((2, HEADS, HEAD_DIM, HEAD_DIM), F32)],
        [z, z, z, d_o, s_f, z, z, z, d_o, s_b, lb_f, lb_b], ("arbitrary",))
    return outs, extra


def _heads(fn, *arrs):
    res = [fn(*[a[:, h * HEAD_DIM:(h + 1) * HEAD_DIM] for a in arrs]) for h in range(HEADS)]
    return [jnp.concatenate(parts, axis=1) for parts in zip(*res)]


def _hg_post(o_f, o_b, zg, g):
    def head(of, ob, zgh, gh):
        on, _ = _rms(of + ob)
        return (on * gh * _silu(zgh),)
    return _heads(head, o_f, o_b, zg, g)[0]


def _hg_post_bwd(da, o_f, o_b, zg, g):
    def head(dah, of, ob, zgh, gh):
        on, r = _rms(of + ob)
        sg = _silu(zgh)
        d_on = dah * sg
        return _rms_bwd(d_on, on, r, gh), dah * on * gh * _silu_grad(zgh), d_on * on
    d_o, dzg, dg = _heads(head, da, o_f, o_b, zg, g)
    return d_o, dzg, _colsum(dg)


def _sg_parts(zv, ln_g, ln_b):
    vg = _gelu(zv)
    xc = vg - _mean(vg)
    rstd = lax.rsqrt(_mean(xc * xc) + EPS)
    vh = xc * rstd
    return vh, rstd, vh * ln_g + ln_b


def _sg_lane_group(shape):
    return lax.broadcasted_iota(jnp.int32, shape, 1) < SG_GROUP_DIM


def _sg_mix(w, v16, transpose):
    rows = v16.shape[0]
    out = []
    for c in range(rows // SG_CHUNK):
        parts = []
        for j in range(SG_WIDTH // 128):
            vj = v16[c * SG_CHUNK:(c + 1) * SG_CHUNK, j * 128:(j + 1) * 128]
            w0 = w[(2 * j) * SG_CHUNK:(2 * j + 1) * SG_CHUNK]
            w1 = w[(2 * j + 1) * SG_CHUNK:(2 * j + 2) * SG_CHUNK]
            dot = _dot_tn if transpose else _dot
            parts.append(jnp.where(_sg_lane_group((SG_CHUNK, 128)), dot(w0, vj), dot(w1, vj)))
        out.append(jnp.concatenate(parts, axis=1))
    return jnp.concatenate(out, axis=0)


def _sg_fwd(zu, zv, w, bias, ln_g, ln_b):
    _, _, v = _sg_parts(zv, ln_g, ln_b)
    reps = zu.shape[0] // SG_CHUNK
    return _gelu(zu) * (_sg_mix(w, v.astype(BF16), False) + jnp.concatenate([bias] * reps, axis=0))


def _sg_bwd(db, zu, zv, w, bias, ln_g, ln_b):
    vh, rstd, v = _sg_parts(zv, ln_g, ln_b)
    v16 = v.astype(BF16)
    reps = zu.shape[0] // SG_CHUNK
    sg = _sg_mix(w, v16, False) + jnp.concatenate([bias] * reps, axis=0)
    dzu = db * sg * _gelu_grad(zu)
    dsg = db * _gelu(zu)
    dsg16 = dsg.astype(BF16)
    dv = _sg_mix(w, dsg16, True)
    low = _sg_lane_group((SG_CHUNK, 128))
    dw = []
    for g in range(SG_WIDTH // SG_GROUP_DIM):
        j, keep = g // 2, (low if g % 2 == 0 else jnp.logical_not(low))
        acc = jnp.zeros((SG_CHUNK, SG_CHUNK), F32)
        for c in range(reps):
            rows = slice(c * SG_CHUNK, (c + 1) * SG_CHUNK)
            dj = jnp.where(keep, dsg16[rows, j * 128:(j + 1) * 128], jnp.zeros((), BF16))
            acc = acc + _dot_nt(dj, v16[rows, j * 128:(j + 1) * 128])
        dw.append(acc)
    dbias = sum(dsg[c * SG_CHUNK:(c + 1) * SG_CHUNK] for c in range(reps))
    dvh = dv * ln_g
    dvg = rstd * (dvh - _mean(dvh) - vh * _mean(dvh * vh))
    dzuv = jnp.concatenate([dzu, dvg * _gelu_grad(zv)], axis=1)
    return (dzuv, jnp.concatenate(dw, axis=0), dbias, _colsum(dv * vh), _colsum(dv))


def lower_bounds(name, gamma_f, gamma_b):
    def body(gf_ref, gb_ref, lf_ref, lb_ref):
        for g_ref, o_ref in ((gf_ref, lf_ref), (gb_ref, lb_ref)):
            g0, g1 = g_ref[0:1, :], g_ref[1:2, :]
            mx = jnp.maximum(g0, g1)
            e0, e1 = jnp.exp(g0 - mx), jnp.exp(g1 - mx)
            sm0, sm1 = e0 / (e0 + e1), e1 / (e0 + e1)
            o_ref[0:1, :] = sm0 - sm0
            o_ref[1:2, :] = (sm0 + sm1) - sm0
    shp = jax.ShapeDtypeStruct(gamma_f.shape, F32)
    return pl.pallas_call(body, name=name, out_shape=[shp, shp])(gamma_f, gamma_b)


def lower_bounds_bwd(name, gamma_f, gamma_b, dlb_f, dlb_b):
    def body(gf_ref, gb_ref, df_ref, db_ref, of_ref, ob_ref):
        for g_ref, d_ref, o_ref in ((gf_ref, df_ref, of_ref), (gb_ref, db_ref, ob_ref)):
            g0, g1 = g_ref[0:1, :], g_ref[1:2, :]
            mx = jnp.maximum(g0, g1)
            e0, e1 = jnp.exp(g0 - mx), jnp.exp(g1 - mx)
            sm0, sm1 = e0 / (e0 + e1), e1 / (e0 + e1)
            d1 = d_ref[1:2, :] * sm0 * sm1
            o_ref[0:1, :] = -d1
            o_ref[1:2, :] = d1
    shp = jax.ShapeDtypeStruct(gamma_f.shape, F32)
    return pl.pallas_call(body, name=name, out_shape=[shp, shp])(gamma_f, gamma_b, dlb_f, dlb_b)


def _row(a, l):
    return a[l:l + 1]


class LocalPlan:
    def __init__(self, weights):
        self.W = weights
        self.grads = [dict() for _ in range(DEPTH)]

    def exch(self, host):
        return None

    def done(self, host, outs):
        pass

    def early_small(self, packed):
        pass


def local_step(x, p, target, S, plan):
    m = x.shape[0]
    lb_f, lb_b = lower_bounds("lower_bounds", S["lb_gamma_fwd"], S["lb_gamma_bwd"])
    saved = []
    for l in range(DEPTH):
        t = f"l{l}_"
        W = plan.W[l]
        tm = 2048
        in_tile = (1024, 2048)
        ffn_tile = (2048, 2048)
        g_pre, g_post = _row(S["norm_mix_pre"], l), _row(S["norm_mix_post"], l)
        g_fpre, g_fpost = _row(S["norm_ffn_pre"], l), _row(S["norm_ffn_post"], l)
        hg_g = _row(S["hg_norm"], l)
        sg_w = S["sg_w"][l].reshape(SG_WIDTH // SG_GROUP_DIM * SG_CHUNK, SG_CHUNK).astype(BF16)
        sg_bias = jnp.repeat(S["sg_b"][l].T, SG_GROUP_DIM, axis=1)
        ln_g, ln_b = _row(S["sg_ln_g"], l), _row(S["sg_ln_b"], l)
        lbf, lbb = _row(lb_f, l), _row(lb_b, l)

        (h,) = rowwise(t + "pre_norm", lambda xv, g: (_rms(xv)[0] * g,), m, ins=[(x, D, 0)], consts=[g_pre],
                       outs=[(D, BF16)])
        ex = plan.exch(t + "in_proj")
        z = mm(t + "in_proj", h, W["w_in"], "nn", tm=in_tile[0], tn=in_tile[1], exch=ex)
        if ex is not None:
            z, extra = z
            plan.done(t + "in_proj", extra)
        (o_f, o_b, s_f, s_b), extra = hgrn_fwd(t + "hgrn_fwd", z, lbf, lbb, exch=plan.exch(t + "hgrn_fwd"))
        plan.done(t + "hgrn_fwd", extra)
        (a_out,) = rowwise(t + "hgrn_post", _hg_post, m, ins=[(o_f, D, 0), (o_b, D, 0), (z, D, ZG)], consts=[hg_g],
                           outs=[(D, BF16)])
        (b_out,) = rowwise(t + "sgu_fwd", _sg_fwd, m, ins=[(z, SG_WIDTH, ZU), (z, SG_WIDTH, ZV)],
                           consts=[sg_w, sg_bias, ln_g, ln_b], outs=[(SG_WIDTH, BF16)])
        pa = mm(t + "proj_a", a_out, W["w_a"], "nn", BF16)
        pb = mm(t + "proj_b", b_out, W["w_b"], "nn", BF16)
        (merged,) = rowwise(t + "merge", lambda a, b, ga, gb: (_sigmoid(ga) * a + _sigmoid(gb) * b,), m,
                            ins=[(pa, D, 0), (pb, D, 0), (z, D, GA), (z, D, GB)], outs=[(D, BF16)])
        mix = mm(t + "out_proj", merged, W["w_out"], "nn")

        def post_pre(xv, mixv, gp, gf):
            x1 = xv + _rms(mixv)[0] * gp
            return x1, _rms(x1)[0] * gf
        x1, h2 = rowwise(t + "mix_post_ffn_pre", post_pre, m, ins=[(x, D, 0), (mix, D, 0)], consts=[g_post, g_fpre],
                         outs=[(D, F32), (D, BF16)])
        gu = mm(t + "ffn_in", h2, W["w_gu"], "nn", BF16, tm=ffn_tile[0], tn=ffn_tile[1])
        (hid,) = rowwise(t + "ffn_act", lambda gt, up: (_silu(gt.astype(F32)) * up,), m, ins=[(gu, FFN_PAD, 0), (gu, FFN_PAD, 1)],
                         outs=[(FFN_PAD, BF16)])
        ff = mm(t + "ffn_out", hid, W["w_down"], "nn", tm=tm)
        (x2,) = rowwise(t + "ffn_post", lambda xv, f, g: (xv + _rms(f)[0] * g,), m, ins=[(x1, D, 0), (ff, D, 0)],
                        consts=[g_fpost], outs=[(D, F32)])
        e = mm(t + "ple_proj", (p, l), W["w_ple"], "nn")
        tg = mm(t + "ple_gate", x2, W["w_ple_gate"], "nn")
        (x3,) = rowwise(t + "ple_add", lambda xv, ev, tv: (xv + ev * _sigmoid(tv),), m,
                        ins=[(x2, D, 0), (e, D, 0), (tg, D, 0)], outs=[(D, F32)])
        saved.append(dict(x=x, h=h, z=z, o_f=o_f, o_b=o_b, s_f=s_f, s_b=s_b, a_out=a_out, b_out=b_out, pa=pa, pb=pb,
                          merged=merged, mix=mix, x1=x1, h2=h2, gu=gu, hid=hid, ff=ff, x2=x2, e=e, tg=tg,
                          sg_w=sg_w, sg_bias=sg_bias))
        x = x3

    def loss_fn(y, tv):
        err = y - tv
        return err * (1.0 / D), _colsum(err * err)
    dx, loss_cols = rowwise("loss", loss_fn, m, ins=[(x, D, 0), (target, D, 0)], outs=[(D, F32)], accs=[(1, D)])

    gs = {n: [None] * DEPTH for n in SMALL}
    dlb_f, dlb_b = [None] * DEPTH, [None] * DEPTH

    for l in reversed(range(DEPTH)):
        t = f"l{l}_bwd_"
        sv, W = saved[l], plan.W[l]
        tm, tk = 2048, 4096
        g_pre, g_post = _row(S["norm_mix_pre"], l), _row(S["norm_mix_post"], l)
        g_fpre, g_fpost = _row(S["norm_ffn_pre"], l), _row(S["norm_ffn_post"], l)
        hg_g = _row(S["hg_norm"], l)
        ln_g, ln_b = _row(S["sg_ln_g"], l), _row(S["sg_ln_b"], l)
        lbf, lbb = _row(lb_f, l), _row(lb_b, l)

        def wgrad(nm, tag, a, b):
            a_dtype = (a[0] if isinstance(a, tuple) else a).dtype
            plan.grads[l][nm] = mm(tag, a, b, "tn", BF16, tk=tk if a_dtype == BF16 else 2048)

        def ple_bwd(d3, ev, tv):
            s = _sigmoid(tv)
            return d3 * s, d3 * ev * s * (1.0 - s)
        de, dt = rowwise(t + "ple", ple_bwd, m, ins=[(dx, D, 0), (sv["e"], D, 0), (sv["tg"], D, 0)],
                         outs=[(D, BF16), (D, BF16)])
        wgrad("w_ple", t + "w_ple", (p, l), de)
        wgrad("w_ple_gate", t + "w_ple_gate", sv["x2"], dt)
        dx2p = mm(t + "ple_gate_dx", dt, W["w_ple_gate"], "nt")

        def ffn_post_bwd(d3, d2p, f, g):
            d2 = d3 + d2p
            fh, r = _rms(f)
            return d2, _rms_bwd(d2, fh, r, g), _colsum(d2 * fh)
        dx2, dff, gs["norm_ffn_post"][l] = rowwise(
            t + "ffn_post", ffn_post_bwd, m, ins=[(dx, D, 0), (dx2p, D, 0), (sv["ff"], D, 0)], consts=[g_fpost],
            outs=[(D, F32), (D, BF16)], accs=[(1, D)])
        wgrad("w_down", t + "w_down", sv["hid"], dff)
        dhid = mm(t + "ffn_out_dx", dff, W["w_down"], "nt", BF16, tm=tm)

        def act_bwd(dh, gt, up):
            dh, gt = dh.astype(F32), gt.astype(F32)
            return (jnp.concatenate([dh * up * _silu_grad(gt), dh * _silu(gt)], axis=1),)
        (dgu,) = rowwise(t + "ffn_act", act_bwd, m, ins=[(dhid, FFN_PAD, 0), (sv["gu"], FFN_PAD, 0),
                                                       (sv["gu"], FFN_PAD, 1)], outs=[(2 * FFN_PAD, BF16)], tm=128)
        wgrad("w_gu", t + "w_gu", sv["h2"], dgu)
        dh2 = mm(t + "ffn_in_dx", dgu, W["w_gu"], "nt", tm=tm)

        def pre_post_bwd(d2, dh, x1v, mixv, gf, gp):
            xh, r1 = _rms(x1v)
            d1 = d2 + _rms_bwd(dh, xh, r1, gf)
            mh, rm = _rms(mixv)
            return d1, _rms_bwd(d1, mh, rm, gp), _colsum(dh * xh), _colsum(d1 * mh)
        dx1, dmix, gs["norm_ffn_pre"][l], gs["norm_mix_post"][l] = rowwise(
            t + "mix_post_ffn_pre", pre_post_bwd, m, ins=[(dx2, D, 0), (dh2, D, 0), (sv["x1"], D, 0), (sv["mix"], D, 0)],
            consts=[g_fpre, g_post], outs=[(D, F32), (D, BF16)], accs=[(1, D), (1, D)])
        wgrad("w_out", t + "w_out", sv["merged"], dmix)
        dmerged = mm(t + "out_proj_dx", dmix, W["w_out"], "nt", BF16)

        def merge_bwd(dm, a, b, gab):
            dm = dm.astype(F32)
            sa, sb = _sigmoid(gab[:, :D]), _sigmoid(gab[:, D:])
            dgab = jnp.concatenate([dm * a * sa * (1.0 - sa), dm * b * sb * (1.0 - sb)], axis=1)
            return dm * sa, dm * sb, dgab
        dpa, dpb, dz = rowwise(
            t + "merge", merge_bwd, m, ins=[(dmerged, D, 0), (sv["pa"], D, 0), (sv["pb"], D, 0), (sv["z"], 2 * D, 3)],
            outs=[(D, BF16), (D, BF16)], alias_outs=[(jax.ShapeDtypeStruct((m, N_IN), BF16), 2 * D, 3)])
        wgrad("w_a", t + "w_a", sv["a_out"], dpa)
        wgrad("w_b", t + "w_b", sv["b_out"], dpb)
        da = mm(t + "proj_a_dx", dpa, W["w_a"], "nt")
        db = mm(t + "proj_b_dx", dpb, W["w_b"], "nt")

        dz, dsw, dbias, gs["sg_ln_g"][l], gs["sg_ln_b"][l] = rowwise(
            t + "sgu", _sg_bwd, m, ins=[(db, SG_WIDTH, 0), (sv["z"], SG_WIDTH, ZU), (sv["z"], SG_WIDTH, ZV)],
            consts=[sv["sg_w"], sv["sg_bias"], ln_g, ln_b], alias_outs=[(dz, 2 * SG_WIDTH, 5)],
            accs=[(SG_WIDTH // SG_GROUP_DIM * SG_CHUNK, SG_CHUNK), (SG_CHUNK, SG_WIDTH), (1, SG_WIDTH), (1, SG_WIDTH)])
        gs["sg_w"][l] = dsw.reshape(1, SG_WIDTH // SG_GROUP_DIM, SG_CHUNK, SG_CHUNK)
        gs["sg_b"][l] = dbias.reshape(SG_CHUNK, SG_WIDTH // SG_GROUP_DIM, SG_GROUP_DIM).sum(-1).T[None]

        d_o, dz, gs["hg_norm"][l] = rowwise(
            t + "hgrn_post", _hg_post_bwd, m, ins=[(da, D, 0), (sv["o_f"], D, 0), (sv["o_b"], D, 0), (sv["z"], D, ZG)],
            consts=[hg_g], outs=[(D, BF16)], alias_outs=[(dz, D, ZG)], accs=[(1, D)])
        if l == 0:
            part = {n: (g if not isinstance(g, list) else jnp.concatenate(
                [jnp.zeros((1,) + g[1].shape[1:], F32) if gl is None else gl for gl in g], axis=0))
                for n, g in gs.items()}
            plan.early_small(_pack([part[n].reshape(S[n].shape) for n in SMALL]))
        (dq_f, dv_f, dq_b, dv_b, dzf_f, dzf_b, dlb_f[l], dlb_b[l]), extra = hgrn_bwd(
            t + "hgrn", sv["z"], d_o, sv["s_f"], sv["s_b"], lbf, lbb, exch=plan.exch(t + "hgrn"))
        plan.done(t + "hgrn", extra)

        def combine(dqf, dqb, dvf, dvb, dff_, dfb_, zq):
            dq = dqf.astype(F32) + dqb.astype(F32)
            dv = dvf.astype(F32) + dvb.astype(F32)
            return (jnp.concatenate([(dq * _silu_grad(zq)).astype(BF16), dff_, dfb_, dv.astype(BF16)], axis=1),)
        (dz,) = rowwise(t + "hgrn_combine", combine, m,
                        ins=[(dq_f, D, 0), (dq_b, D, 0), (dv_f, D, 0), (dv_b, D, 0), (dzf_f, D, 0), (dzf_b, D, 0),
                             (sv["z"], D, ZQ)], alias_outs=[(dz, 4 * D, 0)], tm=128)
        wgrad("w_in", t + "w_in", sv["h"], dz)
        ex = plan.exch(t + "in_proj_dx")
        dh = mm(t + "in_proj_dx", dz, W["w_in"], "nt", tm=tm, exch=ex)
        if ex is not None:
            dh, extra = dh
            plan.done(t + "in_proj_dx", extra)

        def pre_bwd(d1, dhv, xv, g):
            xh, r = _rms(xv)
            return d1 + _rms_bwd(dhv, xh, r, g), _colsum(dhv * xh)
        dx, gs["norm_mix_pre"][l] = rowwise(t + "pre_norm", pre_bwd, m, ins=[(dx1, D, 0), (dh, D, 0), (sv["x"], D, 0)],
                                            consts=[g_pre], outs=[(D, F32)], accs=[(1, D)])
        saved[l] = None
        if l == DEPTH - 1:
            none = jnp.zeros((1, D), F32)
            gs["lb_gamma_fwd"], gs["lb_gamma_bwd"] = lower_bounds_bwd(
                "lower_bounds_bwd", S["lb_gamma_fwd"], S["lb_gamma_bwd"], jnp.concatenate([none, dlb_f[l]], axis=0),
                jnp.concatenate([none, dlb_b[l]], axis=0))

    small ={n: (g if not isinstance(g, list) else jnp.concatenate(g, axis=0)).reshape(S[n].shape)
             for n, g in gs.items()}
    return loss_cols, dx, small


def cast_pad(name, w, rows_p, cols_p):
    _, r, c = w.shape

    def body(w_ref, o_ref):
        if (rows_p, cols_p) != (r, c):
            o_ref[...] = jnp.zeros(o_ref.shape, BF16)
        o_ref[0:r, 0:c] = w_ref[...].astype(BF16)

    return pl.pallas_call(
        body, name=name, grid=(DEPTH,), in_specs=[pl.BlockSpec((None, r, c), lambda l: (l, 0, 0))],
        out_specs=pl.BlockSpec((None, rows_p, cols_p), lambda l: (l, 0, 0)),
        out_shape=jax.ShapeDtypeStruct((DEPTH, rows_p, cols_p), BF16), compiler_params=_params(("parallel",)),
    )(w)


def _shard_shape(n, shape):
    axis, size, _, _ = LAYOUT[n]
    _, r, c = shape
    return (size, c) if axis == 0 else (r, size)


class DistPlan:
    def __init__(self, shards):
        self.shards = shards
        self.W = [dict() for _ in range(DEPTH)]
        self.grads = [dict() for _ in range(DEPTH)]
        self.slots = [dict() for _ in range(DEPTH)]
        rest = [n for n in BIG if n != "w_in"]
        self.schedule = {
            "l0_in_proj": ("gather", [(0, n) for n in rest]),
            "l0_hgrn_fwd": ("gather", [(1, n) for n in BIG]),
            "l0_bwd_hgrn": ("scatter", [(1, n) for n in BIG] + [(0, n) for n in rest]),
            "l0_bwd_in_proj_dx": ("scatter", [(0, "w_in")]),
        }
        self.pending = {}
        self.small_part = self.small_slots = None
        self.done("start", exchange("gather_l0_w_in", self._gather("start", [(0, "w_in")])))

    def _gather(self, host, parts):
        srcs, dsts, items, keys = [], [], [], []
        for layer, n in parts:
            axis, size, dst, base = LAYOUT[n]
            if (layer, dst) not in keys:
                keys.append((layer, dst))
                dsts.append((GATHERED[dst], BF16))
            srcs.append(self.shards[n])
            items.append(("gather", len(srcs) - 1, keys.index((layer, dst)), axis, size, base, layer))
        self.pending[host] = ("gather", keys)
        return Exchange(srcs, dsts, items)

    def _scatter(self, host, parts):
        srcs, src_keys, dsts, items = [], [], [], []
        for layer, n in parts:
            axis, size, dst, base = LAYOUT[n]
            if (layer, dst) not in src_keys:
                src_keys.append((layer, dst))
                srcs.append(self.grads[layer][dst])
            dsts.append(((NDEV,) + _shard_shape(n, self.shards[n].shape), BF16))
            items.append(("scatter", src_keys.index((layer, dst)), len(dsts) - 1, axis, size, base, None))
        keys = list(parts)
        if host == "l0_bwd_hgrn" and self.small_part is not None:
            srcs.append(self.small_part)
            dsts.append(((NDEV,) + self.small_part.shape, F32))
            items.append(("copies", len(srcs) - 1, len(dsts) - 1, 0, 0, 0, None))
            keys.append(("small", None))
        self.pending[host] = ("scatter", keys)
        return Exchange(srcs, dsts, items)

    def early_small(self, packed):
        self.small_part = packed

    def exch(self, host):
        if host not in self.schedule:
            return None
        kind, parts = self.schedule[host]
        return self._gather(host, parts) if kind == "gather" else self._scatter(host, parts)

    def done(self, host, outs):
        if host not in self.pending:
            return
        kind, keys = self.pending.pop(host)
        for (layer, n), arr in zip(keys, outs):
            if layer == "small":
                self.small_slots = arr
            else:
                (self.W if kind == "gather" else self.slots)[layer][n] = arr


def adam(name, w, m_, v_, tr, g=None, slots=None):
    L, r, c = w.shape
    assert r % tr == 0
    nt = r // tr
    n_s = 0 if slots is None else L

    def body(*refs):
        s_refs = refs[:n_s]
        g_ref = refs[n_s] if g is not None else None
        w_ref, m_ref, v_ref, g_out, d_out, m_out, v_out = refs[n_s + (g is not None):]

        def update(gv):
            if g_ref is not None:
                gv = gv + g_ref[...] if gv is not None else g_ref[...]
            m2 = B1 * m_ref[...] + (1.0 - B1) * gv
            v2 = B2 * v_ref[...] + (1.0 - B2) * (gv * gv)
            m_hat = m2 / (1.0 - B1 ** STEP)
            v_hat = v2 / (1.0 - B2 ** STEP)
            g_out[...] = gv
            d_out[...] = -LR * (m_hat / (jnp.sqrt(v_hat) + AEPS) + WD * w_ref[...])
            m_out[...] = m2
            v_out[...] = v2

        if slots is None:
            update(None)
            return
        for layer, s_ref in enumerate(s_refs):
            @pl.when(pl.program_id(0) == layer)
            def _():
                gv = s_ref[0][:, :c].astype(F32)
                for j in range(1, NDEV):
                    gv = gv + s_ref[j][:, :c].astype(F32)
                update(gv)

    spec = pl.BlockSpec((None, tr, c), lambda l, i: (l, i, 0))
    arrs, specs = [], []
    if slots is not None:
        assert len(slots) == L and L <= 2
        arrs = list(slots)
        cp = slots[0].shape[2]
        specs = [pl.BlockSpec((NDEV, tr, cp), lambda l, i: (0, i * (1 - l) + (nt - 1) * l, 0)),
                 pl.BlockSpec((NDEV, tr, cp), lambda l, i: (0, i * l, 0))][:L]
    if g is not None:
        arrs.append(g)
        specs.append(spec)
    shp = jax.ShapeDtypeStruct(w.shape, F32)
    return pl.pallas_call(
        body, name=name, grid=(L, nt), in_specs=specs + [spec, spec, spec], out_specs=[spec] * 4,
        out_shape=[shp] * 4, compiler_params=_params(("arbitrary", "arbitrary")),
    )(*arrs, w, m_, v_)


def _pack(arrs):
    parts = []
    for a in arrs:
        a2 = a.reshape(-1, D)
        parts.append(jnp.pad(a2, ((0, -a2.shape[0] % 8), (0, 0))))
    return jnp.concatenate(parts, axis=0)


def _unpack(buf, shapes):
    out, off = [], 0
    for s in shapes:
        rows = 1
        for d_ in s:
            rows *= d_
        rows //= D
        out.append(buf[off:off + rows].reshape(s))
        off += rows + (-rows % 8)
    return out


def kernel(x, p, norm_mix_pre, w_in, lb_gamma_fwd, lb_gamma_bwd, hg_norm, sg_w, sg_b, sg_ln_g, sg_ln_b, w_a, w_b, w_out, norm_mix_post, norm_ffn_pre, w_gate, w_up, w_down, norm_ffn_post, w_ple, w_ple_gate, loss_target, m_norm_mix_pre, m_w_in, m_lb_gamma_fwd, m_lb_gamma_bwd, m_hg_norm, m_sg_w, m_sg_b, m_sg_ln_g, m_sg_ln_b, m_w_a, m_w_b, m_w_out, m_norm_mix_post, m_norm_ffn_pre, m_w_gate, m_w_up, m_w_down, m_norm_ffn_post, m_w_ple, m_w_ple_gate, v_norm_mix_pre, v_w_in, v_lb_gamma_fwd, v_lb_gamma_bwd, v_hg_norm, v_sg_w, v_sg_b, v_sg_ln_g, v_sg_ln_b, v_w_a, v_w_b, v_w_out, v_norm_mix_post, v_norm_ffn_pre, v_w_gate, v_w_up, v_w_down, v_norm_ffn_post, v_w_ple, v_w_ple_gate):
    a = dict(zip(INPUTS, (x, p, norm_mix_pre, w_in, lb_gamma_fwd, lb_gamma_bwd, hg_norm, sg_w, sg_b, sg_ln_g, sg_ln_b, w_a, w_b, w_out, norm_mix_post, norm_ffn_pre, w_gate, w_up, w_down, norm_ffn_post, w_ple, w_ple_gate, loss_target, m_norm_mix_pre, m_w_in, m_lb_gamma_fwd, m_lb_gamma_bwd, m_hg_norm, m_sg_w, m_sg_b, m_sg_ln_g, m_sg_ln_b, m_w_a, m_w_b, m_w_out, m_norm_mix_post, m_norm_ffn_pre, m_w_gate, m_w_up, m_w_down, m_norm_ffn_post, m_w_ple, m_w_ple_gate, v_norm_mix_pre, v_w_in, v_lb_gamma_fwd, v_lb_gamma_bwd, v_hg_norm, v_sg_w, v_sg_b, v_sg_ln_g, v_sg_ln_b, v_w_a, v_w_b, v_w_out, v_norm_mix_post, v_norm_ffn_pre, v_w_gate, v_w_up, v_w_down, v_norm_ffn_post, v_w_ple, v_w_ple_gate)))
    m = x.shape[1]

    shards = {n: cast_pad("cast_" + n, a[n], *_shard_shape(n, a[n].shape)) for n in BIG}
    plan = DistPlan(shards)
    loss_cols, dx, gs = local_step(x[0], p[:, 0], loss_target[0], {n: a[n] for n in SMALL}, plan)
    loss = lax.psum(jnp.sum(loss_cols) * (0.5 / D), ("x", "y", "c"))

    small_shapes = [a[n].shape for n in SMALL]
    rows = plan.small_slots.shape[1]
    late = allreduce_small("allreduce_small", jnp.pad(gs["norm_mix_pre"][0:1], ((0, 7), (0, 0))))
    g_late = jnp.pad(late, ((0, rows - 8), (0, 0)))[None]

    res = {}
    row_tiles = {"w_in": 128, "w_a": 128, "w_b": 512, "w_out": 128, "w_gate": 128, "w_up": 128, "w_down": 88,
                 "w_ple": 256, "w_ple_gate": 128}
    for n in BIG:
        res[n] = adam("adam_" + n, a[n], a["m_" + n], a["v_" + n], row_tiles[n],
                      slots=[plan.slots[l][n] for l in range(DEPTH)])
    packed = [_pack([a[pre + n] for n in SMALL])[None] for pre in ("", "m_", "v_")]
    small_res = adam("adam_small", packed[0], packed[1], packed[2], rows // 2, g=g_late, slots=[plan.small_slots])
    small_res = [_unpack(r_[0], small_shapes) for r_ in small_res]
    for i, n in enumerate(SMALL):
        res[n] = tuple(small_res[k][i] for k in range(4))

    outs = [loss, dx.reshape(1, m, D)]
    for k in range(4):
        outs += [res[n][k] for n in WEIGHTS]
    return tuple(outs)
```

```python
import jax
import jax.numpy as jnp
from jax import lax
from jax.experimental import pallas as pl
from jax.experimental.pallas import tpu as pltpu

F32 = jnp.float32
BF16 = jnp.bfloat16

D = 1024
N_IN = 8192
HEADS = 8
HEAD_DIM = 128
SG_CHUNK = 128
SG_WIDTH = 512
SG_GROUP_DIM = 64
FFN = 2816
PLE_DIM = 256
EPS = 1e-6
DEPTH = 2
ZQ, ZFF, ZFB, ZI, ZG, GA, GB = 0, 1, 2, 3, 4, 6, 7
ZU, ZV = 10, 11

NDEV = 8
FFN_SHARD = FFN // NDEV
FFN_SHARD_PAD = 384
FFN_PAD = NDEV * FFN_SHARD_PAD

LR, B1, B2, AEPS, WD, STEP = 0.001, 0.9, 0.999, 1e-08, 0.01, 10

ROW_TILE = 256
HG_CHUNK = 64
HG_BLOCK_FWD = 256
HG_BLOCK_BWD = 128
EXP_CLAMP = 80.0
TINY = float(jnp.finfo(jnp.float32).tiny)
VMEM_LIMIT = 56 * 1024 * 1024

BIG = ["w_in", "w_a", "w_b", "w_out", "w_gate", "w_up", "w_down", "w_ple", "w_ple_gate"]
SMALL = ["norm_mix_pre", "lb_gamma_fwd", "lb_gamma_bwd", "hg_norm", "sg_w", "sg_b", "sg_ln_g", "sg_ln_b",
         "norm_mix_post", "norm_ffn_pre", "norm_ffn_post"]
WEIGHTS = ["norm_mix_pre", "w_in", "lb_gamma_fwd", "lb_gamma_bwd", "hg_norm", "sg_w", "sg_b", "sg_ln_g", "sg_ln_b",
           "w_a", "w_b", "w_out", "norm_mix_post", "norm_ffn_pre", "w_gate", "w_up", "w_down", "norm_ffn_post",
           "w_ple", "w_ple_gate"]
INPUTS = (["x", "p"] + WEIGHTS + ["loss_target"] + ["m_" + n for n in WEIGHTS] + ["v_" + n for n in WEIGHTS])
LAYOUT = {
    "w_in": (1, 1024, "w_in", 0), "w_a": (0, 128, "w_a", 0), "w_b": (1, 128, "w_b", 0),
    "w_out": (0, 128, "w_out", 0), "w_gate": (1, FFN_SHARD_PAD, "w_gu", 0),
    "w_up": (1, FFN_SHARD_PAD, "w_gu", FFN_PAD), "w_down": (0, FFN_SHARD_PAD, "w_down", 0),
    "w_ple": (1, 128, "w_ple", 0), "w_ple_gate": (0, 128, "w_ple_gate", 0),
}
GATHERED = {"w_in": (D, N_IN), "w_a": (D, D), "w_b": (SG_WIDTH, D), "w_out": (D, D), "w_gu": (D, 2 * FFN_PAD),
            "w_down": (FFN_PAD, D), "w_ple": (PLE_DIM, D), "w_ple_gate": (D, D)}


def _params(sem):
    return pltpu.CompilerParams(dimension_semantics=sem, vmem_limit_bytes=VMEM_LIMIT)


def _dot(a, b):
    return lax.dot_general(a, b, (((1,), (0,)), ((), ())), preferred_element_type=F32)


def _dot_nt(a, b):
    return lax.dot_general(a, b, (((1,), (1,)), ((), ())), preferred_element_type=F32)


def _dot_tn(a, b):
    return lax.dot_general(a, b, (((0,), (0,)), ((), ())), preferred_element_type=F32)


def _sigmoid(x):
    return jax.nn.sigmoid(x)


def _silu(x):
    return x * _sigmoid(x)


def _silu_grad(x):
    s = _sigmoid(x)
    return s * (1.0 + x * (1.0 - s))


def _gelu(x):
    return 0.5 * x * (1.0 + lax.erf(x * 0.7071067811865476))


def _gelu_grad(x):
    return 0.5 * (1.0 + lax.erf(x * 0.7071067811865476)) + x * jnp.exp(-0.5 * x * x) * 0.3989422804014327


def _mean(x):
    return jnp.mean(x, axis=-1, keepdims=True)


def _colsum(x):
    return jnp.sum(x, axis=0, keepdims=True)


def _rms(x):
    r = lax.rsqrt(_mean(x * x) + EPS)
    return x * r, r


def _rms_bwd(dy, xh, r, g):
    dyg = dy * g
    return r * (dyg - xh * _mean(dyg * xh))


MESH = pl.DeviceIdType.MESH
ANY = pl.BlockSpec(memory_space=pl.ANY)


def _slab(ref, axis, start, size):
    idx = [slice(None)] * 2
    idx[axis] = pl.ds(start, size)
    return ref.at[tuple(idx)]


class Exchange:
    def __init__(self, srcs, dsts, items):
        self.srcs, self.dsts, self.items = list(srcs), list(dsts), list(items)

    def specs(self):
        n = len(self.items)
        sems = [pltpu.SemaphoreType.DMA((n * (NDEV - 1),)), pltpu.SemaphoreType.DMA((n * (NDEV - 1),)),
                pltpu.SemaphoreType.DMA((n,))]
        return ([ANY] * len(self.srcs), [ANY] * len(self.dsts),
                [jax.ShapeDtypeStruct(s, dt) for (s, dt) in self.dsts], sems)

    def copies(self, src, dst, send_sem, recv_sem, loc_sem):
        x, y, c = lax.axis_index("x"), lax.axis_index("y"), lax.axis_index("c")
        me = 4 * x + 2 * y + c
        starts, waits = [], []
        for n, (kind, si, di, axis, size, base, layer) in enumerate(self.items):
            def views(to_dev, from_dev):
                if kind == "gather":
                    return (src[si].at[layer],
                            _slab(dst[di], axis, base + pl.multiple_of(from_dev * size, 128), size))
                if kind == "copies":
                    return src[si], dst[di].at[from_dev]
                return _slab(src[si], axis, base + pl.multiple_of(to_dev * size, 128), size), dst[di].at[from_dev]

            s_own, d_own = views(me, me)
            own = pltpu.make_async_copy(s_own, d_own, loc_sem.at[n])
            starts.append(own)
            waits.append(own)
            for k in range(1, NDEV):
                px = 1 - x if k & 4 else x
                py = 1 - y if k & 2 else y
                pc = 1 - c if k & 1 else c
                peer = 4 * px + 2 * py + pc
                s_out, _ = views(peer, me)
                _, d_in = views(me, peer)
                sem = n * (NDEV - 1) + k - 1
                starts.append(pltpu.make_async_remote_copy(s_out, d_own, send_sem.at[sem], recv_sem.at[sem],
                                                           device_id=(px, py, pc), device_id_type=MESH))
                waits.append(pltpu.make_async_remote_copy(s_out, d_in, send_sem.at[sem], recv_sem.at[sem],
                                                          device_id=(px, py, pc), device_id_type=MESH))
        return starts, waits


def exchange(name, exch):
    e_in, e_out, e_shape, e_scr = exch.specs()
    ns, nd = len(e_in), len(e_out)

    def body(*refs):
        starts, waits = exch.copies(refs[:ns], refs[ns:ns + nd], *refs[ns + nd:])
        for cp in starts:
            cp.start()
        for cp in waits:
            cp.wait()

    return pl.pallas_call(body, name=name, in_specs=e_in, out_specs=e_out, out_shape=e_shape, scratch_shapes=e_scr,
                          compiler_params=pltpu.CompilerParams(has_side_effects=True))(*exch.srcs)


def hosted_call(body, exch, name, grid, in_specs, out_specs, out_shape, scratch_shapes, operands, semantics,
                aliases=None):
    aliases = aliases or {}
    if exch is None:
        res = pl.pallas_call(body, name=name, grid=grid, in_specs=in_specs, out_specs=out_specs, out_shape=out_shape,
                             scratch_shapes=scratch_shapes, input_output_aliases=aliases,
                             compiler_params=_params(semantics))(*operands)
        return list(res), []
    n_in, n_out, n_scr = len(in_specs), len(out_specs), len(scratch_shapes)
    e_in, e_out, e_shape, e_scr = exch.specs()
    ns, nd = len(e_in), len(e_out)

    def at_step(last):
        cond = None
        for ax, n in enumerate(grid):
            c = pl.program_id(ax) == (n - 1 if last else 0)
            cond = c if cond is None else jnp.logical_and(cond, c)
        return cond

    def wrapped(*refs):
        ins, src = refs[:n_in], refs[n_in:n_in + ns]
        o0 = n_in + ns
        outs, dst = refs[o0:o0 + n_out], refs[o0 + n_out:o0 + n_out + nd]
        s0 = o0 + n_out + nd
        scr, sems = refs[s0:s0 + n_scr], refs[s0 + n_scr:]

        @pl.when(at_step(False))
        def _():
            for cp in exch.copies(src, dst, *sems)[0]:
                cp.start()

        body(*ins, *outs, *scr)

        @pl.when(at_step(True))
        def _():
            for cp in exch.copies(src, dst, *sems)[1]:
                cp.wait()

    res = pl.pallas_call(
        wrapped, name=name, grid=grid, in_specs=list(in_specs) + e_in, out_specs=list(out_specs) + e_out,
        out_shape=list(out_shape) + e_shape, scratch_shapes=list(scratch_shapes) + e_scr,
        input_output_aliases=aliases,
        compiler_params=pltpu.CompilerParams(dimension_semantics=("arbitrary",) * len(grid),
                                             vmem_limit_bytes=VMEM_LIMIT, has_side_effects=True),
    )(*operands, *exch.srcs)
    return list(res[:n_out]), list(res[n_out:])


def allreduce_small(name, part):
    rows, width = part.shape

    def body(p_ref, o_ref, buf, send_sem, recv_sem):
        x, y, c = lax.axis_index("x"), lax.axis_index("y"), lax.axis_index("c")
        me = 4 * x + 2 * y + c
        buf[me] = p_ref[...]
        waits = []
        for k in range(1, NDEV):
            px = 1 - x if k & 4 else x
            py = 1 - y if k & 2 else y
            pc = 1 - c if k & 1 else c
            peer = 4 * px + 2 * py + pc
            pltpu.make_async_remote_copy(p_ref, buf.at[me], send_sem.at[k - 1], recv_sem.at[k - 1],
                                         device_id=(px, py, pc), device_id_type=MESH).start()
            waits.append(pltpu.make_async_remote_copy(p_ref, buf.at[peer], send_sem.at[k - 1], recv_sem.at[k - 1],
                                                      device_id=(px, py, pc), device_id_type=MESH))
        for w in waits:
            w.wait()
        acc = buf[0]
        for j in range(1, NDEV):
            acc = acc + buf[j]
        o_ref[...] = acc

    vmem = pl.BlockSpec(memory_space=pltpu.VMEM)
    return pl.pallas_call(
        body, name=name, in_specs=[vmem], out_specs=vmem, out_shape=jax.ShapeDtypeStruct((rows, width), F32),
        scratch_shapes=[pltpu.VMEM((NDEV, rows, width), F32), pltpu.SemaphoreType.DMA((NDEV - 1,)),
                        pltpu.SemaphoreType.DMA((NDEV - 1,))],
        compiler_params=pltpu.CompilerParams(vmem_limit_bytes=VMEM_LIMIT, has_side_effects=True),
    )(part)


def rowwise(name, fn, m, ins=(), consts=(), outs=(), alias_outs=(), accs=(), tm=ROW_TILE):
    tm = min(tm, m)
    n_in, n_c, n_o, n_al, n_ac = len(ins), len(consts), len(outs), len(alias_outs), len(accs)
    held = [a for (a, _, _) in alias_outs if not isinstance(a, jax.ShapeDtypeStruct)]
    n_held = len(held)

    def body(*refs):
        in_refs = refs[:n_in + n_c]
        out_refs = refs[n_in + n_c + n_held:]
        vals = fn(*[r[...] for r in in_refs])
        if not isinstance(vals, (tuple, list)):
            vals = (vals,)
        for r, v in zip(out_refs[:n_o + n_al], vals[:n_o + n_al]):
            r[...] = v.astype(r.dtype)
        if n_ac:
            acc_refs = out_refs[n_o + n_al:]

            @pl.when(pl.program_id(0) == 0)
            def _():
                for r in acc_refs:
                    r[...] = jnp.zeros(r.shape, F32)

            for r, v in zip(acc_refs, vals[n_o + n_al:]):
                r[...] += v

    def col(cb):
        return lambda i: (i, cb)

    in_specs = [pl.BlockSpec((tm, w), col(cb)) for (_, w, cb) in ins]
    in_specs += [pl.BlockSpec(c.shape, lambda i, nd=c.ndim: (0,) * nd) for c in consts]
    in_specs += [ANY for _ in held]
    out_shape = [jax.ShapeDtypeStruct((m, w), dt) for (w, dt) in outs]
    out_specs = [pl.BlockSpec((tm, w), col(0)) for (w, _) in outs]
    out_shape += [jax.ShapeDtypeStruct(a.shape, a.dtype) for (a, _, _) in alias_outs]
    out_specs += [pl.BlockSpec((tm, w), col(cb)) for (_, w, cb) in alias_outs]
    out_shape += [jax.ShapeDtypeStruct(s, F32) for s in accs]
    out_specs += [pl.BlockSpec(s, lambda i: (0, 0)) for s in accs]
    aliases, k_in = {}, n_in + n_c
    for k, (a, _, _) in enumerate(alias_outs):
        if not isinstance(a, jax.ShapeDtypeStruct):
            aliases[k_in] = n_o + k
            k_in += 1
    return pl.pallas_call(
        body, name=name, grid=(m // tm,), in_specs=in_specs, out_specs=out_specs, out_shape=out_shape,
        input_output_aliases=aliases,
        compiler_params=_params(("arbitrary",) if n_ac else ("parallel",)),
    )(*[a for (a, _, _) in ins], *consts, *held)


def _operand(arr, bshape, imap):
    if isinstance(arr, tuple):
        arr, lead = arr
        return arr, pl.BlockSpec((None,) + bshape, lambda *g: (lead,) + imap(*g))
    return arr, pl.BlockSpec(bshape, imap)


def _shape2(arr):
    return arr[0].shape[1:] if isinstance(arr, tuple) else arr.shape


def mm(name, a, b, mode, out_dtype=F32, tm=1024, tn=1024, tk=1024, exch=None):
    sa, sb = _shape2(a), _shape2(b)
    if mode == "nn":
        (M, K), N = sa, sb[1]
    elif mode == "nt":
        (M, K), N = sa, sb[0]
    else:
        (K, M), N = sa, sb[1]
    tm, tn, tk = min(tm, M), min(tn, N), min(tk, K)
    assert M % tm == 0 and N % tn == 0 and K % tk == 0, (name, M, N, K)
    nk = K // tk
    if mode == "nn":
        a_arr, a_spec = _operand(a, (tm, tk), lambda i, j, k: (i, k))
        b_arr, b_spec = _operand(b, (tk, tn), lambda i, j, k: (k, j))
        dot = _dot
    elif mode == "nt":
        a_arr, a_spec = _operand(a, (tm, tk), lambda i, j, k: (i, k))
        b_arr, b_spec = _operand(b, (tn, tk), lambda i, j, k: (j, k))
        dot = _dot_nt
    else:
        a_arr, a_spec = _operand(a, (tk, tm), lambda i, j, k: (k, i))
        b_arr, b_spec = _operand(b, (tk, tn), lambda i, j, k: (k, j))
        dot = _dot_tn

    def body(a_ref, b_ref, o_ref, *acc):
        part = dot(a_ref[...].astype(BF16), b_ref[...].astype(BF16))
        if nk == 1:
            o_ref[...] = part.astype(o_ref.dtype)
            return
        acc_ref, k = acc[0], pl.program_id(2)

        @pl.when(k == 0)
        def _():
            acc_ref[...] = part

        @pl.when(k > 0)
        def _():
            acc_ref[...] += part

        @pl.when(k == nk - 1)
        def _():
            o_ref[...] = acc_ref[...].astype(o_ref.dtype)

    outs, extra = hosted_call(
        body, exch, name, (M // tm, N // tn, nk), [a_spec, b_spec], [pl.BlockSpec((tm, tn), lambda i, j, k: (i, j))],
        [jax.ShapeDtypeStruct((M, N), out_dtype)], [pltpu.VMEM((tm, tn), F32)] if nk > 1 else [], [a_arr, b_arr],
        ("parallel", "parallel", "arbitrary"))
    return outs[0] if exch is None else (outs[0], extra)


def ffn_out_fused(name, gu, w_down, tm=1024, tk=1024):
    m, hidden = gu.shape[0], gu.shape[1] // 2
    tm = min(tm, m)
    nk = hidden // tk

    def body(g_ref, u_ref, w_ref, ff_ref, hid_ref, acc_ref):
        k = pl.program_id(1)
        hid = (_silu(g_ref[...].astype(F32)) * u_ref[...]).astype(BF16)
        hid_ref[...] = hid
        part = _dot(hid, w_ref[...])

        @pl.when(k == 0)
        def _():
            acc_ref[...] = part

        @pl.when(k > 0)
        def _():
            acc_ref[...] += part

        @pl.when(k == nk - 1)
        def _():
            ff_ref[...] = acc_ref[...]

    return pl.pallas_call(
        body, name=name, grid=(m // tm, nk),
        in_specs=[pl.BlockSpec((tm, tk), lambda i, k: (i, k)), pl.BlockSpec((tm, tk), lambda i, k: (i, k + nk)),
                  pl.BlockSpec((tk, D), lambda i, k: (k, 0))],
        out_specs=[pl.BlockSpec((tm, D), lambda i, k: (i, 0)), pl.BlockSpec((tm, tk), lambda i, k: (i, k))],
        out_shape=[jax.ShapeDtypeStruct((m, D), F32), jax.ShapeDtypeStruct((m, hidden), BF16)],
        scratch_shapes=[pltpu.VMEM((tm, D), F32)], compiler_params=_params(("parallel", "arbitrary")),
    )(gu, gu, w_down)


def ffn_in_dx_fused(name, dhid, gu, w_gu, tm=512, tk=1024):
    m, hidden = dhid.shape
    tm = min(tm, m)
    nk = hidden // tk

    def body(dh_ref, g_ref, u_ref, wg_ref, wu_ref, o_ref, dg_ref, du_ref, acc_ref):
        k = pl.program_id(1)
        dh, g = dh_ref[...].astype(F32), g_ref[...].astype(F32)
        dgate = (dh * u_ref[...] * _silu_grad(g)).astype(BF16)
        dup = (dh * _silu(g)).astype(BF16)
        dg_ref[...] = dgate
        du_ref[...] = dup
        part = _dot_nt(dgate, wg_ref[...]) + _dot_nt(dup, wu_ref[...])

        @pl.when(k == 0)
        def _():
            acc_ref[...] = part

        @pl.when(k > 0)
        def _():
            acc_ref[...] += part

        @pl.when(k == nk - 1)
        def _():
            o_ref[...] = acc_ref[...]

    tile = pl.BlockSpec((tm, tk), lambda i, k: (i, k))
    half = jax.ShapeDtypeStruct((m, hidden), BF16)
    return pl.pallas_call(
        body, name=name, grid=(m // tm, nk),
        in_specs=[tile, tile, pl.BlockSpec((tm, tk), lambda i, k: (i, k + nk)),
                  pl.BlockSpec((D, tk), lambda i, k: (0, k)), pl.BlockSpec((D, tk), lambda i, k: (0, k + nk))],
        out_specs=[pl.BlockSpec((tm, D), lambda i, k: (i, 0)), tile, tile],
        out_shape=[jax.ShapeDtypeStruct((m, D), F32), half, half],
        scratch_shapes=[pltpu.VMEM((tm, D), F32)], compiler_params=_params(("parallel", "arbitrary")),
    )(dhid, gu, gu, w_gu, w_gu)


def _cumsum_rows(x):
    n = x.shape[0]
    row = lax.broadcasted_iota(jnp.int32, x.shape, 0)
    s = 1
    while s < n:
        x = x + jnp.where(row >= s, pltpu.roll(x, s, 0), 0.0)
        s *= 2
    return x


def _hg_prep(zq, zf, lb, reverse):
    n = zq.shape[0]
    q = _silu(zq)
    sig = _sigmoid(zf)
    sn = 1.0 - sig
    f = lb + (1.0 - lb) * sig
    k = (1.0 - lb) * sn
    g = jnp.log(jnp.maximum(f, TINY))
    b = _cumsum_rows(g)
    if reverse:
        b = b[n - 1:n] - b + g
    b_last = b[0:1] if reverse else b[n - 1:n]
    b_ref = b[n // 2:n // 2 + 1]
    e1 = jnp.exp(b)
    e2 = jnp.exp(jnp.clip(b - b_ref, -EXP_CLAMP, EXP_CLAMP))
    e3 = jnp.exp(jnp.clip(b_ref - b, -EXP_CLAMP, EXP_CLAMP))
    e4 = jnp.exp(b_last - b)
    return dict(q=q, k=k, sig=sig, sn=sn, f=f, e1=e1, e2=e2, e3=e3, e4=e4, e_last=jnp.exp(b_last),
                qe=(q * e1).astype(BF16), qt=(q * e2).astype(BF16), kt=(k * e3).astype(BF16),
                ks=(k * e4).astype(BF16))


def _hg_mask(n, reverse):
    t = lax.broadcasted_iota(jnp.int32, (n, n), 0)
    s = lax.broadcasted_iota(jnp.int32, (n, n), 1)
    return (s >= t) if reverse else (s <= t)


def hgrn_fwd(name, z, lb_f, lb_b, exch=None, unroll=False):
    m = z.shape[0]
    C, T = HG_CHUNK, min(HG_BLOCK_FWD, m)
    nb, cpb = m // T, T // C

    def body(zq_f, zf_f, zi_f, zq_b, zf_b, zi_b, lbf_ref, lbb_ref, of_ref, ob_ref, sf_ref, sb_ref, st_ref):
        @pl.when(pl.program_id(0) == 0)
        def _():
            st_ref[...] = jnp.zeros(st_ref.shape, F32)

        dirs = ((zq_f, zf_f, zi_f, lbf_ref, of_ref, sf_ref), (zq_b, zf_b, zi_b, lbb_ref, ob_ref, sb_ref))

        def chunk(ci, carry):
            work = []
            for d, (zq, zf, zi, lb_ref, o_ref, s_ref) in enumerate(dirs):
                cc = ci if d == 0 else cpb - 1 - ci
                rows = pl.ds(pl.multiple_of(cc * C, C), C)
                pre = _hg_prep(zq[rows, :], zf[rows, :], lb_ref[...], d == 1)
                v = zi[rows, :].astype(BF16)
                work.append((cc, rows, pre, v, [st_ref[d, h] for h in range(HEADS)]))
            heads = [(d, h, slice(h * HEAD_DIM, (h + 1) * HEAD_DIM)) for d in range(2) for h in range(HEADS)]
            first = {}
            for d, h, sl in heads:
                _, _, pre, v, sts = work[d]
                first[d, h] = (_dot_nt(pre["qt"][:, sl], pre["kt"][:, sl]),
                               _dot_nt(pre["qe"][:, sl], sts[h].astype(BF16)),
                               _dot_tn(v[:, sl], pre["ks"][:, sl]))
            results = [([], []), ([], [])]
            for d, h, sl in heads:
                _, _, pre, v, sts = work[d]
                scores, o_inter, st_add = first[d, h]
                a = jnp.where(_hg_mask(C, d == 1), scores, 0.0).astype(BF16)
                results[d][0].append(o_inter + _dot(a, v[:, sl]))
                results[d][1].append(sts[h] * pre["e_last"][:, sl] + st_add)
            results = [(jnp.concatenate(o_parts, axis=1), new_sts) for (o_parts, new_sts) in results]
            for d, (zq, zf, zi, lb_ref, o_ref, s_ref) in enumerate(dirs):
                cc, rows, _, _, sts = work[d]
                o_ref[rows, :] = results[d][0]
                for h in range(HEADS):
                    s_ref[cc, h] = sts[h]
                    st_ref[d, h] = results[d][1][h]
            return carry

        lax.fori_loop(0, cpb, chunk, 0, unroll=unroll)

    def zspec(cb, rev):
        return pl.BlockSpec((T, D), (lambda i: (nb - 1 - i, cb)) if rev else (lambda i: (i, cb)))

    def sspec(rev):
        shape = (cpb, HEADS, HEAD_DIM, HEAD_DIM)
        return pl.BlockSpec(shape, (lambda i: (nb - 1 - i, 0, 0, 0)) if rev else (lambda i: (i, 0, 0, 0)))

    lbspec = pl.BlockSpec((1, D), lambda i: (0, 0))
    states = jax.ShapeDtypeStruct((m // C, HEADS, HEAD_DIM, HEAD_DIM), F32)
    outs, extra = hosted_call(
        body, exch, name, (nb,),
        [zspec(ZQ, False), zspec(ZFF, False), zspec(ZI, False), zspec(ZQ, True), zspec(ZFB, True), zspec(ZI, True),
         lbspec, lbspec],
        [zspec(0, False), zspec(0, True), sspec(False), sspec(True)],
        [jax.ShapeDtypeStruct((m, D), F32), jax.ShapeDtypeStruct((m, D), F32), states, states],
        [pltpu.VMEM((2, HEADS, HEAD_DIM, HEAD_DIM), F32)], [z, z, z, z, z, z, lb_f, lb_b], ("arbitrary",))
    return outs, extra


def hgrn_bwd(name, z, d_o, s_f, s_b, lb_f, lb_b, exch=None, unroll=False):
    m = z.shape[0]
    C, T = HG_CHUNK, min(HG_BLOCK_BWD, m)
    nb, cpb = m // T, T // C

    def body(zq_f, zf_f, zi_f, do_f, sf_ref, zq_b, zf_b, zi_b, do_b, sb_ref, lbf_ref, lbb_ref,
             dqf_ref, dvf_ref, dqb_ref, dvb_ref, dzf_f, dzf_b, dlbf_ref, dlbb_ref,
             dst_ref):
        @pl.when(pl.program_id(0) == 0)
        def _():
            dst_ref[...] = jnp.zeros(dst_ref.shape, F32)
            dlbf_ref[...] = jnp.zeros(dlbf_ref.shape, F32)
            dlbb_ref[...] = jnp.zeros(dlbb_ref.shape, F32)

        dirs = ((zq_f, zf_f, zi_f, do_f, sf_ref, lbf_ref, dqf_ref, dvf_ref, dzf_f, dlbf_ref),
                (zq_b, zf_b, zi_b, do_b, sb_ref, lbb_ref, dqb_ref, dvb_ref, dzf_b, dlbb_ref))

        def chunk(ci, carry):
            work = []
            for d, (zq, zf, zi, do_ref, s_ref, lb_ref, dq_ref, dv_ref, dzf_ref, dlb_ref) in enumerate(dirs):
                cc = cpb - 1 - ci if d == 0 else ci
                rows = pl.ds(pl.multiple_of(cc * C, C), C)
                lb = lb_ref[...]
                pre = _hg_prep(zq[rows, :], zf[rows, :], lb, d == 1)
                work.append((rows, lb, pre, zi[rows, :].astype(BF16), do_ref[rows, :],
                             [s_ref[cc, h] for h in range(HEADS)], [dst_ref[d, h] for h in range(HEADS)],
                             dlb_ref[...]))
            heads = [(d, h, slice(h * HEAD_DIM, (h + 1) * HEAD_DIM)) for d in range(2) for h in range(HEADS)]
            first = {}
            for d, h, sl in heads:
                _, _, pre, v, do, st_prevs, dsts, _ = work[d]
                dst16 = dsts[h].astype(BF16)
                first[d, h] = (_dot_nt(pre["qt"][:, sl], pre["kt"][:, sl]),
                               _dot_nt(do[:, sl], v[:, sl]),
                               _dot(do[:, sl], st_prevs[h].astype(BF16)),
                               _dot(v[:, sl], dst16),
                               _dot_nt(pre["ks"][:, sl], dst16),
                               _dot_tn(do[:, sl], pre["qe"][:, sl]))
            parts = [[[] for _ in range(6)] for _ in range(2)]
            for d, h, sl in heads:
                _, _, pre, v, do, st_prevs, dsts, _ = work[d]
                scores, dscores, dq_inter, dk_state, dv_state, dst_add = first[d, h]
                mask = _hg_mask(C, d == 1)
                a = jnp.where(mask, scores, 0.0).astype(BF16)
                da = jnp.where(mask, dscores, 0.0).astype(BF16)
                dq_p, dki_p, dks_p, dv_p, rr_p, new_dsts = parts[d]
                dq_p.append(_dot(da, pre["kt"][:, sl]) * pre["e2"][:, sl] + dq_inter * pre["e1"][:, sl])
                dki_p.append(_dot_tn(da, pre["qt"][:, sl]) * pre["e3"][:, sl])
                dks_p.append(dk_state * pre["e4"][:, sl])
                dv_p.append(_dot_tn(a, do[:, sl]) + dv_state)
                rr_p.append(pre["e_last"][:, sl] * _colsum(dsts[h] * st_prevs[h]))
                new_dsts.append(dsts[h] * pre["e_last"][:, sl] + dst_add)
            results = []
            for d, (rows, lb, pre, v, do, st_prevs, dsts, dlb_old) in enumerate(work):
                rev = d == 1
                dq_p, dki_p, dks_p, dv_p, rr_p, new_dsts = parts[d]
                dq, dki, dks, dv, rr = (jnp.concatenate(p_, axis=1) for p_ in (dq_p, dki_p, dks_p, dv_p, rr_p))
                x = pre["q"] * dq - pre["k"] * dki
                y = pre["k"] * dks
                if rev:
                    dg = _cumsum_rows(x - y) + _colsum(y) + rr
                else:
                    dg = _cumsum_rows(y - x) + (x - y) + _colsum(x) + rr
                inv_f = jnp.where(pre["f"] > TINY, 1.0 / pre["f"], 0.0)
                u = dg * inv_f - (dki + dks)
                results.append((dq, dv, (1.0 - lb) * pre["sig"] * pre["sn"] * u, dlb_old + _colsum(pre["sn"] * u),
                                new_dsts))
            for d, (zq, zf, zi, do_ref, s_ref, lb_ref, dq_ref, dv_ref, dzf_ref, dlb_ref) in enumerate(dirs):
                rows = work[d][0]
                dq, dv, dzf, dlb, new_dsts = results[d]
                dq_ref[rows, :] = dq.astype(dq_ref.dtype)
                dv_ref[rows, :] = dv.astype(dv_ref.dtype)
                dzf_ref[rows, :] = dzf.astype(dzf_ref.dtype)
                dlb_ref[...] = dlb
                for h in range(HEADS):
                    dst_ref[d, h] = new_dsts[h]
            return carry

        lax.fori_loop(0, cpb, chunk, 0, unroll=unroll)

    def rspec(cb, rev):
        return pl.BlockSpec((T, D), (lambda i: (i, cb)) if rev else (lambda i: (nb - 1 - i, cb)))

    def sspec(rev):
        shape = (cpb, HEADS, HEAD_DIM, HEAD_DIM)
        return pl.BlockSpec(shape, (lambda i: (i, 0, 0, 0)) if rev else (lambda i: (nb - 1 - i, 0, 0, 0)))

    lbspec = pl.BlockSpec((1, D), lambda i: (0, 0))
    half = jax.ShapeDtypeStruct((m, D), BF16)
    row = jax.ShapeDtypeStruct((1, D), F32)
    outs, extra = hosted_call(
        body, exch, name, (nb,),
        [rspec(ZQ, False), rspec(ZFF, False), rspec(ZI, False), rspec(0, False), sspec(False),
         rspec(ZQ, True), rspec(ZFB, True), rspec(ZI, True), rspec(0, True), sspec(True), lbspec, lbspec],
        [rspec(0, False), rspec(0, False), rspec(0, True), rspec(0, True), rspec(0, False), rspec(0, True),
         lbspec, lbspec],
        [half, half, half, half, half, half, row, row],
        [pltpu.VMEM((2, HEADS, HEAD_DIM, HEAD_DIM), F32)],
        [z, z, z, d_o, s_f, z, z, z, d_o, s_b, lb_f, lb_b], ("arbitrary",))
    return outs, extra


def _heads(fn, *arrs):
    res = [fn(*[a[:, h * HEAD_DIM:(h + 1) * HEAD_DIM] for a in arrs]) for h in range(HEADS)]
    return [jnp.concatenate(parts, axis=1) for parts in zip(*res)]


def _hg_post(o_f, o_b, zg, g):
    def head(of, ob, zgh, gh):
        on, _ = _rms(of + ob)
        return (on * gh * _silu(zgh),)
    return _heads(head, o_f, o_b, zg, g)[0]


def _hg_post_bwd(da, o_f, o_b, zg, g):
    def head(dah, of, ob, zgh, gh):
        on, r = _rms(of + ob)
        sg = _silu(zgh)
        d_on = dah * sg
        return _rms_bwd(d_on, on, r, gh), dah * on * gh * _silu_grad(zgh), d_on * on
    d_o, dzg, dg = _heads(head, da, o_f, o_b, zg, g)
    return d_o, dzg, _colsum(dg)


def _sg_parts(zv, ln_g, ln_b):
    vg = _gelu(zv)
    xc = vg - _mean(vg)
    rstd = lax.rsqrt(_mean(xc * xc) + EPS)
    vh = xc * rstd
    return vh, rstd, vh * ln_g + ln_b


def _sg_lane_group(shape):
    return lax.broadcasted_iota(jnp.int32, shape, 1) < SG_GROUP_DIM


def _sg_mix(w, v16, transpose):
    rows = v16.shape[0]
    out = []
    for c in range(rows // SG_CHUNK):
        parts = []
        for j in range(SG_WIDTH // 128):
            vj = v16[c * SG_CHUNK:(c + 1) * SG_CHUNK, j * 128:(j + 1) * 128]
            w0 = w[(2 * j) * SG_CHUNK:(2 * j + 1) * SG_CHUNK]
            w1 = w[(2 * j + 1) * SG_CHUNK:(2 * j + 2) * SG_CHUNK]
            dot = _dot_tn if transpose else _dot
            parts.append(jnp.where(_sg_lane_group((SG_CHUNK, 128)), dot(w0, vj), dot(w1, vj)))
        out.append(jnp.concatenate(parts, axis=1))
    return jnp.concatenate(out, axis=0)


def _sg_fwd(zu, zv, w, bias, ln_g, ln_b):
    _, _, v = _sg_parts(zv, ln_g, ln_b)
    reps = zu.shape[0] // SG_CHUNK
    return _gelu(zu) * (_sg_mix(w, v.astype(BF16), False) + jnp.concatenate([bias] * reps, axis=0))


def _sg_bwd(db, zu, zv, w, bias, ln_g, ln_b):
    vh, rstd, v = _sg_parts(zv, ln_g, ln_b)
    v16 = v.astype(BF16)
    reps = zu.shape[0] // SG_CHUNK
    sg = _sg_mix(w, v16, False) + jnp.concatenate([bias] * reps, axis=0)
    dzu = db * sg * _gelu_grad(zu)
    dsg = db * _gelu(zu)
    dsg16 = dsg.astype(BF16)
    dv = _sg_mix(w, dsg16, True)
    low = _sg_lane_group((SG_CHUNK, 128))
    dw = []
    for g in range(SG_WIDTH // SG_GROUP_DIM):
        j, keep = g // 2, (low if g % 2 == 0 else jnp.logical_not(low))
        acc = jnp.zeros((SG_CHUNK, SG_CHUNK), F32)
        for c in range(reps):
            rows = slice(c * SG_CHUNK, (c + 1) * SG_CHUNK)
            dj = jnp.where(keep, dsg16[rows, j * 128:(j + 1) * 128], jnp.zeros((), BF16))
            acc = acc + _dot_nt(dj, v16[rows, j * 128:(j + 1) * 128])
        dw.append(acc)
    dbias = sum(dsg[c * SG_CHUNK:(c + 1) * SG_CHUNK] for c in range(reps))
    dvh = dv * ln_g
    dvg = rstd * (dvh - _mean(dvh) - vh * _mean(dvh * vh))
    dzuv = jnp.concatenate([dzu, dvg * _gelu_grad(zv)], axis=1)
    return (dzuv, jnp.concatenate(dw, axis=0), dbias, _colsum(dv * vh), _colsum(dv))


def lower_bounds(name, gamma_f, gamma_b):
    def body(gf_ref, gb_ref, lf_ref, lb_ref):
        for g_ref, o_ref in ((gf_ref, lf_ref), (gb_ref, lb_ref)):
            g0, g1 = g_ref[0:1, :], g_ref[1:2, :]
            mx = jnp.maximum(g0, g1)
            e0, e1 = jnp.exp(g0 - mx), jnp.exp(g1 - mx)
            sm0, sm1 = e0 / (e0 + e1), e1 / (e0 + e1)
            o_ref[0:1, :] = sm0 - sm0
            o_ref[1:2, :] = (sm0 + sm1) - sm0
    shp = jax.ShapeDtypeStruct(gamma_f.shape, F32)
    return pl.pallas_call(body, name=name, out_shape=[shp, shp])(gamma_f, gamma_b)


def lower_bounds_bwd(name, gamma_f, gamma_b, dlb_f, dlb_b):
    def body(gf_ref, gb_ref, df_ref, db_ref, of_ref, ob_ref):
        for g_ref, d_ref, o_ref in ((gf_ref, df_ref, of_ref), (gb_ref, db_ref, ob_ref)):
            g0, g1 = g_ref[0:1, :], g_ref[1:2, :]
            mx = jnp.maximum(g0, g1)
            e0, e1 = jnp.exp(g0 - mx), jnp.exp(g1 - mx)
            sm0, sm1 = e0 / (e0 + e1), e1 / (e0 + e1)
            d1 = d_ref[1:2, :] * sm0 * sm1
            o_ref[0:1, :] = -d1
            o_ref[1:2, :] = d1
    shp = jax.ShapeDtypeStruct(gamma_f.shape, F32)
    return pl.pallas_call(body, name=name, out_shape=[shp, shp])(gamma_f, gamma_b, dlb_f, dlb_b)


def _row(a, l):
    return a[l:l + 1]


class LocalPlan:
    def __init__(self, weights):
        self.W = weights
        self.grads = [dict() for _ in range(DEPTH)]

    def exch(self, host):
        return None

    def done(self, host, outs):
        pass

    def early_small(self, packed):
        pass


def local_step(x, p, target, S, plan):
    m = x.shape[0]

    def hmm(tag, *args, **kw):
        ex = plan.exch(tag)
        res = mm(tag, *args, exch=ex, **kw)
        if ex is None:
            return res
        plan.done(tag, res[1])
        return res[0]

    lb_f, lb_b = lower_bounds("lower_bounds", S["lb_gamma_fwd"], S["lb_gamma_bwd"])
    saved = []
    for l in range(DEPTH):
        t = f"l{l}_"
        W = plan.W[l]
        tm = 2048
        in_tile = (1024, 2048)
        ffn_tile = (2048, 2048)
        g_pre, g_post = _row(S["norm_mix_pre"], l), _row(S["norm_mix_post"], l)
        g_fpre, g_fpost = _row(S["norm_ffn_pre"], l), _row(S["norm_ffn_post"], l)
        hg_g = _row(S["hg_norm"], l)
        sg_w = S["sg_w"][l].reshape(SG_WIDTH // SG_GROUP_DIM * SG_CHUNK, SG_CHUNK).astype(BF16)
        sg_bias = jnp.repeat(S["sg_b"][l].T, SG_GROUP_DIM, axis=1)
        ln_g, ln_b = _row(S["sg_ln_g"], l), _row(S["sg_ln_b"], l)
        lbf, lbb = _row(lb_f, l), _row(lb_b, l)

        (h,) = rowwise(t + "pre_norm", lambda xv, g: (_rms(xv)[0] * g,), m, ins=[(x, D, 0)], consts=[g_pre],
                       outs=[(D, BF16)])
        z = hmm(t + "in_proj", h, W["w_in"], "nn", tm=in_tile[0], tn=in_tile[1])
        (o_f, o_b, s_f, s_b), extra = hgrn_fwd(t + "hgrn_fwd", z, lbf, lbb, exch=plan.exch(t + "hgrn_fwd"))
        plan.done(t + "hgrn_fwd", extra)
        (a_out,) = rowwise(t + "hgrn_post", _hg_post, m, ins=[(o_f, D, 0), (o_b, D, 0), (z, D, ZG)], consts=[hg_g],
                           outs=[(D, BF16)])
        (b_out,) = rowwise(t + "sgu_fwd", _sg_fwd, m, ins=[(z, SG_WIDTH, ZU), (z, SG_WIDTH, ZV)],
                           consts=[sg_w, sg_bias, ln_g, ln_b], outs=[(SG_WIDTH, BF16)])
        pa = mm(t + "proj_a", a_out, W["w_a"], "nn", BF16)
        pb = mm(t + "proj_b", b_out, W["w_b"], "nn", BF16)
        (merged,) = rowwise(t + "merge", lambda a, b, ga, gb: (_sigmoid(ga) * a + _sigmoid(gb) * b,), m,
                            ins=[(pa, D, 0), (pb, D, 0), (z, D, GA), (z, D, GB)], outs=[(D, BF16)])
        mix = mm(t + "out_proj", merged, W["w_out"], "nn")

        def post_pre(xv, mixv, gp, gf):
            x1 = xv + _rms(mixv)[0] * gp
            return x1, _rms(x1)[0] * gf
        x1, h2 = rowwise(t + "mix_post_ffn_pre", post_pre, m, ins=[(x, D, 0), (mix, D, 0)], consts=[g_post, g_fpre],
                         outs=[(D, F32), (D, BF16)])
        gu = hmm(t + "ffn_in", h2, W["w_gu"], "nn", BF16, tm=ffn_tile[0], tn=ffn_tile[1])
        ff, hid = ffn_out_fused(t + "ffn_out", gu, W["w_down"])
        (x2,) = rowwise(t + "ffn_post", lambda xv, f, g: (xv + _rms(f)[0] * g,), m, ins=[(x1, D, 0), (ff, D, 0)],
                        consts=[g_fpost], outs=[(D, F32)])
        e = mm(t + "ple_proj", (p, l), W["w_ple"], "nn")
        tg = mm(t + "ple_gate", x2, W["w_ple_gate"], "nn")
        (x3,) = rowwise(t + "ple_add", lambda xv, ev, tv: (xv + ev * _sigmoid(tv),), m,
                        ins=[(x2, D, 0), (e, D, 0), (tg, D, 0)], outs=[(D, F32)])
        saved.append(dict(x=x, h=h, z=z, o_f=o_f, o_b=o_b, s_f=s_f, s_b=s_b, a_out=a_out, b_out=b_out, pa=pa, pb=pb,
                          merged=merged, mix=mix, x1=x1, h2=h2, gu=gu, hid=hid, ff=ff, x2=x2, e=e, tg=tg,
                          sg_w=sg_w, sg_bias=sg_bias))
        x = x3

    def loss_fn(y, tv):
        err = y - tv
        return err * (1.0 / D), _colsum(err * err)
    dx, loss_cols = rowwise("loss", loss_fn, m, ins=[(x, D, 0), (target, D, 0)], outs=[(D, F32)], accs=[(1, D)])

    gs = {n: [None] * DEPTH for n in SMALL}
    dlb_f, dlb_b = [None] * DEPTH, [None] * DEPTH

    for l in reversed(range(DEPTH)):
        t = f"l{l}_bwd_"
        sv, W = saved[l], plan.W[l]
        tm, tk = 2048, 4096
        g_pre, g_post = _row(S["norm_mix_pre"], l), _row(S["norm_mix_post"], l)
        g_fpre, g_fpost = _row(S["norm_ffn_pre"], l), _row(S["norm_ffn_post"], l)
        hg_g = _row(S["hg_norm"], l)
        ln_g, ln_b = _row(S["sg_ln_g"], l), _row(S["sg_ln_b"], l)
        lbf, lbb = _row(lb_f, l), _row(lb_b, l)

        def wgrad(nm, tag, a, b):
            a_dtype = (a[0] if isinstance(a, tuple) else a).dtype
            plan.grads[l][nm] = mm(tag, a, b, "tn", BF16, tk=tk if a_dtype == BF16 else 2048)

        def ple_bwd(d3, ev, tv):
            s = _sigmoid(tv)
            return d3 * s, d3 * ev * s * (1.0 - s)
        de, dt = rowwise(t + "ple", ple_bwd, m, ins=[(dx, D, 0), (sv["e"], D, 0), (sv["tg"], D, 0)],
                         outs=[(D, BF16), (D, BF16)])
        wgrad("w_ple", t + "w_ple", (p, l), de)
        wgrad("w_ple_gate", t + "w_ple_gate", sv["x2"], dt)
        dx2p = mm(t + "ple_gate_dx", dt, W["w_ple_gate"], "nt")

        def ffn_post_bwd(d3, d2p, f, g):
            d2 = d3 + d2p
            fh, r = _rms(f)
            return d2, _rms_bwd(d2, fh, r, g), _colsum(d2 * fh)
        dx2, dff, gs["norm_ffn_post"][l] = rowwise(
            t + "ffn_post", ffn_post_bwd, m, ins=[(dx, D, 0), (dx2p, D, 0), (sv["ff"], D, 0)], consts=[g_fpost],
            outs=[(D, F32), (D, BF16)], accs=[(1, D)])
        wgrad("w_down", t + "w_down", sv["hid"], dff)
        dhid = mm(t + "ffn_out_dx", dff, W["w_down"], "nt", BF16, tm=tm)

        dh2, dgate, dup = ffn_in_dx_fused(t + "ffn_in_dx", dhid, sv["gu"], W["w_gu"])
        wgrad("w_gate", t + "w_gate", sv["h2"], dgate)
        wgrad("w_up", t + "w_up", sv["h2"], dup)

        def pre_post_bwd(d2, dh, x1v, mixv, gf, gp):
            xh, r1 = _rms(x1v)
            d1 = d2 + _rms_bwd(dh, xh, r1, gf)
            mh, rm = _rms(mixv)
            return d1, _rms_bwd(d1, mh, rm, gp), _colsum(dh * xh), _colsum(d1 * mh)
        dx1, dmix, gs["norm_ffn_pre"][l], gs["norm_mix_post"][l] = rowwise(
            t + "mix_post_ffn_pre", pre_post_bwd, m, ins=[(dx2, D, 0), (dh2, D, 0), (sv["x1"], D, 0), (sv["mix"], D, 0)],
            consts=[g_fpre, g_post], outs=[(D, F32), (D, BF16)], accs=[(1, D), (1, D)])
        wgrad("w_out", t + "w_out", sv["merged"], dmix)
        dmerged = mm(t + "out_proj_dx", dmix, W["w_out"], "nt", BF16)

        def merge_bwd(dm, a, b, gab):
            dm = dm.astype(F32)
            sa, sb = _sigmoid(gab[:, :D]), _sigmoid(gab[:, D:])
            dgab = jnp.concatenate([dm * a * sa * (1.0 - sa), dm * b * sb * (1.0 - sb)], axis=1)
            return dm * sa, dm * sb, dgab
        dpa, dpb, dz = rowwise(
            t + "merge", merge_bwd, m, ins=[(dmerged, D, 0), (sv["pa"], D, 0), (sv["pb"], D, 0), (sv["z"], 2 * D, 3)],
            outs=[(D, BF16), (D, BF16)], alias_outs=[(jax.ShapeDtypeStruct((m, N_IN), BF16), 2 * D, 3)])
        wgrad("w_a", t + "w_a", sv["a_out"], dpa)
        wgrad("w_b", t + "w_b", sv["b_out"], dpb)
        da = mm(t + "proj_a_dx", dpa, W["w_a"], "nt")
        db = mm(t + "proj_b_dx", dpb, W["w_b"], "nt")

        dz, dsw, dbias, gs["sg_ln_g"][l], gs["sg_ln_b"][l] = rowwise(
            t + "sgu", _sg_bwd, m, ins=[(db, SG_WIDTH, 0), (sv["z"], SG_WIDTH, ZU), (sv["z"], SG_WIDTH, ZV)],
            consts=[sv["sg_w"], sv["sg_bias"], ln_g, ln_b], alias_outs=[(dz, 2 * SG_WIDTH, 5)],
            accs=[(SG_WIDTH // SG_GROUP_DIM * SG_CHUNK, SG_CHUNK), (SG_CHUNK, SG_WIDTH), (1, SG_WIDTH), (1, SG_WIDTH)])
        gs["sg_w"][l] = dsw.reshape(1, SG_WIDTH // SG_GROUP_DIM, SG_CHUNK, SG_CHUNK)
        gs["sg_b"][l] = dbias.reshape(SG_CHUNK, SG_WIDTH // SG_GROUP_DIM, SG_GROUP_DIM).sum(-1).T[None]

        d_o, dz, gs["hg_norm"][l] = rowwise(
            t + "hgrn_post", _hg_post_bwd, m, ins=[(da, D, 0), (sv["o_f"], D, 0), (sv["o_b"], D, 0), (sv["z"], D, ZG)],
            consts=[hg_g], outs=[(D, BF16)], alias_outs=[(dz, D, ZG)], accs=[(1, D)])
        if l == 0:
            part = {n: (g if not isinstance(g, list) else jnp.concatenate(
                [jnp.zeros((1,) + g[1].shape[1:], F32) if gl is None else gl for gl in g], axis=0))
                for n, g in gs.items()}
            plan.early_small(_pack([part[n].reshape(S[n].shape) for n in SMALL]))
        (dq_f, dv_f, dq_b, dv_b, dzf_f, dzf_b, dlb_f[l], dlb_b[l]), extra = hgrn_bwd(
            t + "hgrn", sv["z"], d_o, sv["s_f"], sv["s_b"], lbf, lbb, exch=plan.exch(t + "hgrn"))
        plan.done(t + "hgrn", extra)

        def combine(dqf, dqb, dvf, dvb, dff_, dfb_, zq):
            dq = dqf.astype(F32) + dqb.astype(F32)
            dv = dvf.astype(F32) + dvb.astype(F32)
            return (jnp.concatenate([(dq * _silu_grad(zq)).astype(BF16), dff_, dfb_, dv.astype(BF16)], axis=1),)
        (dz,) = rowwise(t + "hgrn_combine", combine, m,
                        ins=[(dq_f, D, 0), (dq_b, D, 0), (dv_f, D, 0), (dv_b, D, 0), (dzf_f, D, 0), (dzf_b, D, 0),
                             (sv["z"], D, ZQ)], alias_outs=[(dz, 4 * D, 0)], tm=128)
        wgrad("w_in", t + "w_in", sv["h"], dz)
        dh = hmm(t + "in_proj_dx", dz, W["w_in"], "nt", tm=tm)

        def pre_bwd(d1, dhv, xv, g):
            xh, r = _rms(xv)
            return d1 + _rms_bwd(dhv, xh, r, g), _colsum(dhv * xh)
        dx, gs["norm_mix_pre"][l] = rowwise(t + "pre_norm", pre_bwd, m, ins=[(dx1, D, 0), (dh, D, 0), (sv["x"], D, 0)],
                                            consts=[g_pre], outs=[(D, F32)], accs=[(1, D)])
        saved[l] = None
        if l == DEPTH - 1:
            none = jnp.zeros((1, D), F32)
            gs["lb_gamma_fwd"], gs["lb_gamma_bwd"] = lower_bounds_bwd(
                "lower_bounds_bwd", S["lb_gamma_fwd"], S["lb_gamma_bwd"], jnp.concatenate([none, dlb_f[l]], axis=0),
                jnp.concatenate([none, dlb_b[l]], axis=0))

    small ={n: (g if not isinstance(g, list) else jnp.concatenate(g, axis=0)).reshape(S[n].shape)
             for n, g in gs.items()}
    return loss_cols, dx, small


def cast_pad(name, w, rows_p, cols_p):
    _, r, c = w.shape

    def body(w_ref, o_ref):
        if (rows_p, cols_p) != (r, c):
            o_ref[...] = jnp.zeros(o_ref.shape, BF16)
        o_ref[0:r, 0:c] = w_ref[...].astype(BF16)

    return pl.pallas_call(
        body, name=name, grid=(DEPTH,), in_specs=[pl.BlockSpec((None, r, c), lambda l: (l, 0, 0))],
        out_specs=pl.BlockSpec((None, rows_p, cols_p), lambda l: (l, 0, 0)),
        out_shape=jax.ShapeDtypeStruct((DEPTH, rows_p, cols_p), BF16), compiler_params=_params(("parallel",)),
    )(w)


def _shard_shape(n, shape):
    axis, size, _, _ = LAYOUT[n]
    _, r, c = shape
    return (size, c) if axis == 0 else (r, size)


class DistPlan:
    def __init__(self, shards):
        self.shards = shards
        self.W = [dict() for _ in range(DEPTH)]
        self.grads = [dict() for _ in range(DEPTH)]
        self.slots = [dict() for _ in range(DEPTH)]
        rest = [n for n in BIG if n != "w_in"]
        ffn = ["w_gate", "w_up", "w_down"]
        self.schedule = {
            "l0_in_proj": ("gather", [(0, n) for n in rest]),
            "l0_hgrn_fwd": ("gather", [(1, n) for n in BIG if n not in ffn]),
            "l0_ffn_in": ("gather", [(1, n) for n in ffn]),
            "l1_bwd_hgrn": ("scatter", [(1, n) for n in rest]),
            "l1_bwd_in_proj_dx": ("scatter", [(1, "w_in")]),
            "l0_bwd_hgrn": ("scatter", [(0, n) for n in rest]),
            "l0_bwd_in_proj_dx": ("scatter", [(0, "w_in")]),
        }
        self.pending = {}
        self.small_part = self.small_slots = None
        self.done("start", exchange("gather_l0_w_in", self._gather("start", [(0, "w_in")])))

    def _gather(self, host, parts):
        srcs, dsts, items, keys = [], [], [], []
        for layer, n in parts:
            axis, size, dst, base = LAYOUT[n]
            if (layer, dst) not in keys:
                keys.append((layer, dst))
                dsts.append((GATHERED[dst], BF16))
            srcs.append(self.shards[n])
            items.append(("gather", len(srcs) - 1, keys.index((layer, dst)), axis, size, base, layer))
        self.pending[host] = ("gather", keys)
        return Exchange(srcs, dsts, items)

    def _scatter(self, host, parts):
        srcs, dsts, items = [], [], []
        for layer, n in parts:
            axis, size, _, _ = LAYOUT[n]
            srcs.append(self.grads[layer][n])
            dsts.append(((NDEV,) + _shard_shape(n, self.shards[n].shape), BF16))
            items.append(("scatter", len(srcs) - 1, len(dsts) - 1, axis, size, 0, None))
        keys = list(parts)
        if host == "l0_bwd_hgrn" and self.small_part is not None:
            srcs.append(self.small_part)
            dsts.append(((NDEV,) + self.small_part.shape, F32))
            items.append(("copies", len(srcs) - 1, len(dsts) - 1, 0, 0, 0, None))
            keys.append(("small", None))
        self.pending[host] = ("scatter", keys)
        return Exchange(srcs, dsts, items)

    def early_small(self, packed):
        self.small_part = packed

    def exch(self, host):
        if host not in self.schedule:
            return None
        kind, parts = self.schedule[host]
        return self._gather(host, parts) if kind == "gather" else self._scatter(host, parts)

    def done(self, host, outs):
        if host not in self.pending:
            return
        kind, keys = self.pending.pop(host)
        for (layer, n), arr in zip(keys, outs):
            if layer == "small":
                self.small_slots = arr
            else:
                (self.W if kind == "gather" else self.slots)[layer][n] = arr


def adam(name, w, m_, v_, tr, g=None, slots=None):
    L, r, c = w.shape
    assert r % tr == 0
    nt = r // tr
    n_s = 0 if slots is None else L

    def body(*refs):
        s_refs = refs[:n_s]
        g_ref = refs[n_s] if g is not None else None
        w_ref, m_ref, v_ref, g_out, d_out, m_out, v_out = refs[n_s + (g is not None):]

        def update(gv):
            if g_ref is not None:
                gv = gv + g_ref[...] if gv is not None else g_ref[...]
            m2 = B1 * m_ref[...] + (1.0 - B1) * gv
            v2 = B2 * v_ref[...] + (1.0 - B2) * (gv * gv)
            m_hat = m2 / (1.0 - B1 ** STEP)
            v_hat = v2 / (1.0 - B2 ** STEP)
            g_out[...] = gv
            d_out[...] = -LR * (m_hat / (jnp.sqrt(v_hat) + AEPS) + WD * w_ref[...])
            m_out[...] = m2
            v_out[...] = v2

        if slots is None:
            update(None)
            return
        for layer, s_ref in enumerate(s_refs):
            @pl.when(pl.program_id(0) == layer)
            def _():
                gv = s_ref[0][:, :c].astype(F32)
                for j in range(1, NDEV):
                    gv = gv + s_ref[j][:, :c].astype(F32)
                update(gv)

    spec = pl.BlockSpec((None, tr, c), lambda l, i: (l, i, 0))
    arrs, specs = [], []
    if slots is not None:
        assert len(slots) == L and L <= 2
        arrs = list(slots)
        cp = slots[0].shape[2]
        specs = [pl.BlockSpec((NDEV, tr, cp), lambda l, i: (0, i * (1 - l) + (nt - 1) * l, 0)),
                 pl.BlockSpec((NDEV, tr, cp), lambda l, i: (0, i * l, 0))][:L]
    if g is not None:
        arrs.append(g)
        specs.append(spec)
    shp = jax.ShapeDtypeStruct(w.shape, F32)
    return pl.pallas_call(
        body, name=name, grid=(L, nt), in_specs=specs + [spec, spec, spec], out_specs=[spec] * 4,
        out_shape=[shp] * 4, compiler_params=_params(("arbitrary", "arbitrary")),
    )(*arrs, w, m_, v_)


def _pack(arrs):
    parts = []
    for a in arrs:
        a2 = a.reshape(-1, D)
        parts.append(jnp.pad(a2, ((0, -a2.shape[0] % 8), (0, 0))))
    return jnp.concatenate(parts, axis=0)


def _unpack(buf, shapes):
    out, off = [], 0
    for s in shapes:
        rows = 1
        for d_ in s:
            rows *= d_
        rows //= D
        out.append(buf[off:off + rows].reshape(s))
        off += rows + (-rows % 8)
    return out


def kernel(x, p, norm_mix_pre, w_in, lb_gamma_fwd, lb_gamma_bwd, hg_norm, sg_w, sg_b, sg_ln_g, sg_ln_b, w_a, w_b, w_out, norm_mix_post, norm_ffn_pre, w_gate, w_up, w_down, norm_ffn_post, w_ple, w_ple_gate, loss_target, m_norm_mix_pre, m_w_in, m_lb_gamma_fwd, m_lb_gamma_bwd, m_hg_norm, m_sg_w, m_sg_b, m_sg_ln_g, m_sg_ln_b, m_w_a, m_w_b, m_w_out, m_norm_mix_post, m_norm_ffn_pre, m_w_gate, m_w_up, m_w_down, m_norm_ffn_post, m_w_ple, m_w_ple_gate, v_norm_mix_pre, v_w_in, v_lb_gamma_fwd, v_lb_gamma_bwd, v_hg_norm, v_sg_w, v_sg_b, v_sg_ln_g, v_sg_ln_b, v_w_a, v_w_b, v_w_out, v_norm_mix_post, v_norm_ffn_pre, v_w_gate, v_w_up, v_w_down, v_norm_ffn_post, v_w_ple, v_w_ple_gate):
    a = dict(zip(INPUTS, (x, p, norm_mix_pre, w_in, lb_gamma_fwd, lb_gamma_bwd, hg_norm, sg_w, sg_b, sg_ln_g, sg_ln_b, w_a, w_b, w_out, norm_mix_post, norm_ffn_pre, w_gate, w_up, w_down, norm_ffn_post, w_ple, w_ple_gate, loss_target, m_norm_mix_pre, m_w_in, m_lb_gamma_fwd, m_lb_gamma_bwd, m_hg_norm, m_sg_w, m_sg_b, m_sg_ln_g, m_sg_ln_b, m_w_a, m_w_b, m_w_out, m_norm_mix_post, m_norm_ffn_pre, m_w_gate, m_w_up, m_w_down, m_norm_ffn_post, m_w_ple, m_w_ple_gate, v_norm_mix_pre, v_w_in, v_lb_gamma_fwd, v_lb_gamma_bwd, v_hg_norm, v_sg_w, v_sg_b, v_sg_ln_g, v_sg_ln_b, v_w_a, v_w_b, v_w_out, v_norm_mix_post, v_norm_ffn_pre, v_w_gate, v_w_up, v_w_down, v_norm_ffn_post, v_w_ple, v_w_ple_gate)))
    m = x.shape[1]

    shards = {n: cast_pad("cast_" + n, a[n], *_shard_shape(n, a[n].shape)) for n in BIG}
    plan = DistPlan(shards)
    loss_cols, dx, gs = local_step(x[0], p[:, 0], loss_target[0], {n: a[n] for n in SMALL}, plan)
    loss = lax.psum(jnp.sum(loss_cols) * (0.5 / D), ("x", "y", "c"))

    small_shapes = [a[n].shape for n in SMALL]
    rows = plan.small_slots.shape[1]
    late = allreduce_small("allreduce_small", jnp.pad(gs["norm_mix_pre"][0:1], ((0, 7), (0, 0))))
    g_late = jnp.pad(late, ((0, rows - 8), (0, 0)))[None]

    res = {}
    row_tiles = {"w_in": 128, "w_a": 128, "w_b": 512, "w_out": 128, "w_gate": 128, "w_up": 128, "w_down": 88,
                 "w_ple": 256, "w_ple_gate": 128}
    for n in BIG:
        res[n] = adam("adam_" + n, a[n], a["m_" + n], a["v_" + n], row_tiles[n],
                      slots=[plan.slots[l][n] for l in range(DEPTH)])
    packed = [_pack([a[pre + n] for n in SMALL])[None] for pre in ("", "m_", "v_")]
    small_res = adam("adam_small", packed[0], packed[1], packed[2], rows // 2, g=g_late, slots=[plan.small_slots])
    small_res = [_unpack(r_[0], small_shapes) for r_ in small_res]
    for i, n in enumerate(SMALL):
        res[n] = tuple(small_res[k][i] for k in range(4))

    outs = [loss, dx.reshape(1, m, D)]
    for k in range(4):
        outs += [res[n][k] for n in WEIGHTS]
    return tuple(outs)
```

```python
import jax
import jax.numpy as jnp
from jax import lax
from jax.experimental import pallas as pl
from jax.experimental.pallas import tpu as pltpu

F32 = jnp.float32
BF16 = jnp.bfloat16

D = 1024
N_IN = 8192
HEADS = 8
HEAD_DIM = 128
SG_CHUNK = 128
SG_WIDTH = 512
SG_GROUP_DIM = 64
FFN = 2816
PLE_DIM = 256
EPS = 1e-6
DEPTH = 2
ZQ, ZFF, ZFB, ZI, ZG, GA, GB = 0, 1, 2, 3, 4, 6, 7
ZU, ZV = 10, 11

NDEV = 8
FFN_SHARD = FFN // NDEV
FFN_SHARD_PAD = 384
FFN_PAD = NDEV * FFN_SHARD_PAD

LR, B1, B2, AEPS, WD, STEP = 0.001, 0.9, 0.999, 1e-08, 0.01, 10

ROW_TILE = 256
HG_CHUNK = 64
HG_BLOCK_FWD = 256
HG_BLOCK_BWD = 128
EXP_CLAMP = 80.0
TINY = float(jnp.finfo(jnp.float32).tiny)
VMEM_LIMIT = 56 * 1024 * 1024

BIG = ["w_in", "w_a", "w_b", "w_out", "w_gate", "w_up", "w_down", "w_ple", "w_ple_gate"]
SMALL = ["norm_mix_pre", "lb_gamma_fwd", "lb_gamma_bwd", "hg_norm", "sg_w", "sg_b", "sg_ln_g", "sg_ln_b",
         "norm_mix_post", "norm_ffn_pre", "norm_ffn_post"]
WEIGHTS = ["norm_mix_pre", "w_in", "lb_gamma_fwd", "lb_gamma_bwd", "hg_norm", "sg_w", "sg_b", "sg_ln_g", "sg_ln_b",
           "w_a", "w_b", "w_out", "norm_mix_post", "norm_ffn_pre", "w_gate", "w_up", "w_down", "norm_ffn_post",
           "w_ple", "w_ple_gate"]
INPUTS = (["x", "p"] + WEIGHTS + ["loss_target"] + ["m_" + n for n in WEIGHTS] + ["v_" + n for n in WEIGHTS])
LAYOUT = {
    "w_in": (1, 1024, "w_in", 0), "w_a": (0, 128, "w_a", 0), "w_b": (1, 128, "w_b", 0),
    "w_out": (0, 128, "w_out", 0), "w_gate": (1, FFN_SHARD_PAD, "w_gu", 0),
    "w_up": (1, FFN_SHARD_PAD, "w_gu", FFN_PAD), "w_down": (0, FFN_SHARD_PAD, "w_down", 0),
    "w_ple": (1, 128, "w_ple", 0), "w_ple_gate": (0, 128, "w_ple_gate", 0),
}
GATHERED = {"w_in": (D, N_IN), "w_a": (D, D), "w_b": (SG_WIDTH, D), "w_out": (D, D), "w_gu": (D, 2 * FFN_PAD),
            "w_down": (FFN_PAD, D), "w_ple": (PLE_DIM, D), "w_ple_gate": (D, D)}


def _params(sem):
    return pltpu.CompilerParams(dimension_semantics=sem, vmem_limit_bytes=VMEM_LIMIT)


def _dot(a, b):
    return lax.dot_general(a, b, (((1,), (0,)), ((), ())), preferred_element_type=F32)


def _dot_nt(a, b):
    return lax.dot_general(a, b, (((1,), (1,)), ((), ())), preferred_element_type=F32)


def _dot_tn(a, b):
    return lax.dot_general(a, b, (((0,), (0,)), ((), ())), preferred_element_type=F32)


def _sigmoid(x):
    return jax.nn.sigmoid(x)


def _silu(x):
    return x * _sigmoid(x)


def _silu_grad(x):
    s = _sigmoid(x)
    return s * (1.0 + x * (1.0 - s))


def _gelu(x):
    return 0.5 * x * (1.0 + lax.erf(x * 0.7071067811865476))


def _gelu_grad(x):
    return 0.5 * (1.0 + lax.erf(x * 0.7071067811865476)) + x * jnp.exp(-0.5 * x * x) * 0.3989422804014327


def _mean(x):
    return jnp.mean(x, axis=-1, keepdims=True)


def _colsum(x):
    return jnp.sum(x, axis=0, keepdims=True)


def _rms(x):
    r = lax.rsqrt(_mean(x * x) + EPS)
    return x * r, r


def _rms_bwd(dy, xh, r, g):
    dyg = dy * g
    return r * (dyg - xh * _mean(dyg * xh))


MESH = pl.DeviceIdType.MESH
ANY = pl.BlockSpec(memory_space=pl.ANY)


def _slab(ref, axis, start, size):
    idx = [slice(None)] * 2
    idx[axis] = pl.ds(start, size)
    return ref.at[tuple(idx)]


class Exchange:
    def __init__(self, srcs, dsts, items):
        self.srcs, self.dsts, self.items = list(srcs), list(dsts), list(items)

    def specs(self):
        n = len(self.items)
        sems = [pltpu.SemaphoreType.DMA((n * (NDEV - 1),)), pltpu.SemaphoreType.DMA((n * (NDEV - 1),)),
                pltpu.SemaphoreType.DMA((n,))]
        return ([ANY] * len(self.srcs), [ANY] * len(self.dsts),
                [jax.ShapeDtypeStruct(s, dt) for (s, dt) in self.dsts], sems)

    def copies(self, src, dst, send_sem, recv_sem, loc_sem):
        x, y, c = lax.axis_index("x"), lax.axis_index("y"), lax.axis_index("c")
        me = 4 * x + 2 * y + c
        starts, waits = [], []
        for n, (kind, si, di, axis, size, base, layer) in enumerate(self.items):
            def views(to_dev, from_dev):
                if kind == "gather":
                    return (src[si].at[layer],
                            _slab(dst[di], axis, base + pl.multiple_of(from_dev * size, 128), size))
                if kind == "copies":
                    return src[si], dst[di].at[from_dev]
                return _slab(src[si], axis, base + pl.multiple_of(to_dev * size, 128), size), dst[di].at[from_dev]

            s_own, d_own = views(me, me)
            own = pltpu.make_async_copy(s_own, d_own, loc_sem.at[n])
            starts.append(own)
            waits.append(own)
            for k in range(1, NDEV):
                px = 1 - x if k & 4 else x
                py = 1 - y if k & 2 else y
                pc = 1 - c if k & 1 else c
                peer = 4 * px + 2 * py + pc
                s_out, _ = views(peer, me)
                _, d_in = views(me, peer)
                sem = n * (NDEV - 1) + k - 1
                starts.append(pltpu.make_async_remote_copy(s_out, d_own, send_sem.at[sem], recv_sem.at[sem],
                                                           device_id=(px, py, pc), device_id_type=MESH))
                waits.append(pltpu.make_async_remote_copy(s_out, d_in, send_sem.at[sem], recv_sem.at[sem],
                                                          device_id=(px, py, pc), device_id_type=MESH))
        return starts, waits


def exchange(name, exch):
    e_in, e_out, e_shape, e_scr = exch.specs()
    ns, nd = len(e_in), len(e_out)

    def body(*refs):
        starts, waits = exch.copies(refs[:ns], refs[ns:ns + nd], *refs[ns + nd:])
        for cp in starts:
            cp.start()
        for cp in waits:
            cp.wait()

    return pl.pallas_call(body, name=name, in_specs=e_in, out_specs=e_out, out_shape=e_shape, scratch_shapes=e_scr,
                          compiler_params=pltpu.CompilerParams(has_side_effects=True))(*exch.srcs)


def hosted_call(body, exch, name, grid, in_specs, out_specs, out_shape, scratch_shapes, operands, semantics,
                aliases=None):
    aliases = aliases or {}
    if exch is None:
        res = pl.pallas_call(body, name=name, grid=grid, in_specs=in_specs, out_specs=out_specs, out_shape=out_shape,
                             scratch_shapes=scratch_shapes, input_output_aliases=aliases,
                             compiler_params=_params(semantics))(*operands)
        return list(res), []
    n_in, n_out, n_scr = len(in_specs), len(out_specs), len(scratch_shapes)
    e_in, e_out, e_shape, e_scr = exch.specs()
    ns, nd = len(e_in), len(e_out)

    def at_step(last):
        cond = None
        for ax, n in enumerate(grid):
            c = pl.program_id(ax) == (n - 1 if last else 0)
            cond = c if cond is None else jnp.logical_and(cond, c)
        return cond

    def wrapped(*refs):
        ins, src = refs[:n_in], refs[n_in:n_in + ns]
        o0 = n_in + ns
        outs, dst = refs[o0:o0 + n_out], refs[o0 + n_out:o0 + n_out + nd]
        s0 = o0 + n_out + nd
        scr, sems = refs[s0:s0 + n_scr], refs[s0 + n_scr:]

        @pl.when(at_step(False))
        def _():
            for cp in exch.copies(src, dst, *sems)[0]:
                cp.start()

        body(*ins, *outs, *scr)

        @pl.when(at_step(True))
        def _():
            for cp in exch.copies(src, dst, *sems)[1]:
                cp.wait()

    res = pl.pallas_call(
        wrapped, name=name, grid=grid, in_specs=list(in_specs) + e_in, out_specs=list(out_specs) + e_out,
        out_shape=list(out_shape) + e_shape, scratch_shapes=list(scratch_shapes) + e_scr,
        input_output_aliases=aliases,
        compiler_params=pltpu.CompilerParams(dimension_semantics=("arbitrary",) * len(grid),
                                             vmem_limit_bytes=VMEM_LIMIT, has_side_effects=True),
    )(*operands, *exch.srcs)
    return list(res[:n_out]), list(res[n_out:])


def allreduce_small(name, part):
    rows, width = part.shape

    def body(p_ref, o_ref, buf, send_sem, recv_sem):
        x, y, c = lax.axis_index("x"), lax.axis_index("y"), lax.axis_index("c")
        me = 4 * x + 2 * y + c
        buf[me] = p_ref[...]
        waits = []
        for k in range(1, NDEV):
            px = 1 - x if k & 4 else x
            py = 1 - y if k & 2 else y
            pc = 1 - c if k & 1 else c
            peer = 4 * px + 2 * py + pc
            pltpu.make_async_remote_copy(p_ref, buf.at[me], send_sem.at[k - 1], recv_sem.at[k - 1],
                                         device_id=(px, py, pc), device_id_type=MESH).start()
            waits.append(pltpu.make_async_remote_copy(p_ref, buf.at[peer], send_sem.at[k - 1], recv_sem.at[k - 1],
                                                      device_id=(px, py, pc), device_id_type=MESH))
        for w in waits:
            w.wait()
        acc = buf[0]
        for j in range(1, NDEV):
            acc = acc + buf[j]
        o_ref[...] = acc

    vmem = pl.BlockSpec(memory_space=pltpu.VMEM)
    return pl.pallas_call(
        body, name=name, in_specs=[vmem], out_specs=vmem, out_shape=jax.ShapeDtypeStruct((rows, width), F32),
        scratch_shapes=[pltpu.VMEM((NDEV, rows, width), F32), pltpu.SemaphoreType.DMA((NDEV - 1,)),
                        pltpu.SemaphoreType.DMA((NDEV - 1,))],
        compiler_params=pltpu.CompilerParams(vmem_limit_bytes=VMEM_LIMIT, has_side_effects=True),
    )(part)


def rowwise(name, fn, m, ins=(), consts=(), outs=(), alias_outs=(), accs=(), tm=ROW_TILE):
    tm = min(tm, m)
    n_in, n_c, n_o, n_al, n_ac = len(ins), len(consts), len(outs), len(alias_outs), len(accs)
    held = [a for (a, _, _) in alias_outs if not isinstance(a, jax.ShapeDtypeStruct)]
    n_held = len(held)

    def body(*refs):
        in_refs = refs[:n_in + n_c]
        out_refs = refs[n_in + n_c + n_held:]
        vals = fn(*[r[...] for r in in_refs])
        if not isinstance(vals, (tuple, list)):
            vals = (vals,)
        for r, v in zip(out_refs[:n_o + n_al], vals[:n_o + n_al]):
            r[...] = v.astype(r.dtype)
        if n_ac:
            acc_refs = out_refs[n_o + n_al:]

            @pl.when(pl.program_id(0) == 0)
            def _():
                for r in acc_refs:
                    r[...] = jnp.zeros(r.shape, F32)

            for r, v in zip(acc_refs, vals[n_o + n_al:]):
                r[...] += v

    def col(cb):
        return lambda i: (i, cb)

    in_specs = [pl.BlockSpec((tm, w), col(cb)) for (_, w, cb) in ins]
    in_specs += [pl.BlockSpec(c.shape, lambda i, nd=c.ndim: (0,) * nd) for c in consts]
    in_specs += [ANY for _ in held]
    out_shape = [jax.ShapeDtypeStruct((m, w), dt) for (w, dt) in outs]
    out_specs = [pl.BlockSpec((tm, w), col(0)) for (w, _) in outs]
    out_shape += [jax.ShapeDtypeStruct(a.shape, a.dtype) for (a, _, _) in alias_outs]
    out_specs += [pl.BlockSpec((tm, w), col(cb)) for (_, w, cb) in alias_outs]
    out_shape += [jax.ShapeDtypeStruct(s, F32) for s in accs]
    out_specs += [pl.BlockSpec(s, lambda i: (0, 0)) for s in accs]
    aliases, k_in = {}, n_in + n_c
    for k, (a, _, _) in enumerate(alias_outs):
        if not isinstance(a, jax.ShapeDtypeStruct):
            aliases[k_in] = n_o + k
            k_in += 1
    return pl.pallas_call(
        body, name=name, grid=(m // tm,), in_specs=in_specs, out_specs=out_specs, out_shape=out_shape,
        input_output_aliases=aliases,
        compiler_params=_params(("arbitrary",) if n_ac else ("parallel",)),
    )(*[a for (a, _, _) in ins], *consts, *held)


def _operand(arr, bshape, imap):
    if isinstance(arr, tuple):
        arr, lead = arr
        return arr, pl.BlockSpec((None,) + bshape, lambda *g: (lead,) + imap(*g))
    return arr, pl.BlockSpec(bshape, imap)


def _shape2(arr):
    return arr[0].shape[1:] if isinstance(arr, tuple) else arr.shape


def mm(name, a, b, mode, out_dtype=F32, tm=1024, tn=1024, tk=1024, exch=None):
    sa, sb = _shape2(a), _shape2(b)
    if mode == "nn":
        (M, K), N = sa, sb[1]
    elif mode == "nt":
        (M, K), N = sa, sb[0]
    else:
        (K, M), N = sa, sb[1]
    tm, tn, tk = min(tm, M), min(tn, N), min(tk, K)
    assert M % tm == 0 and N % tn == 0 and K % tk == 0, (name, M, N, K)
    nk = K // tk
    if mode == "nn":
        a_arr, a_spec = _operand(a, (tm, tk), lambda i, j, k: (i, k))
        b_arr, b_spec = _operand(b, (tk, tn), lambda i, j, k: (k, j))
        dot = _dot
    elif mode == "nt":
        a_arr, a_spec = _operand(a, (tm, tk), lambda i, j, k: (i, k))
        b_arr, b_spec = _operand(b, (tn, tk), lambda i, j, k: (j, k))
        dot = _dot_nt
    else:
        a_arr, a_spec = _operand(a, (tk, tm), lambda i, j, k: (k, i))
        b_arr, b_spec = _operand(b, (tk, tn), lambda i, j, k: (k, j))
        dot = _dot_tn

    def body(a_ref, b_ref, o_ref, *acc):
        part = dot(a_ref[...].astype(BF16), b_ref[...].astype(BF16))
        if nk == 1:
            o_ref[...] = part.astype(o_ref.dtype)
            return
        acc_ref, k = acc[0], pl.program_id(2)

        @pl.when(k == 0)
        def _():
            acc_ref[...] = part

        @pl.when(k > 0)
        def _():
            acc_ref[...] += part

        @pl.when(k == nk - 1)
        def _():
            o_ref[...] = acc_ref[...].astype(o_ref.dtype)

    outs, extra = hosted_call(
        body, exch, name, (M // tm, N // tn, nk), [a_spec, b_spec], [pl.BlockSpec((tm, tn), lambda i, j, k: (i, j))],
        [jax.ShapeDtypeStruct((M, N), out_dtype)], [pltpu.VMEM((tm, tn), F32)] if nk > 1 else [], [a_arr, b_arr],
        ("parallel", "parallel", "arbitrary"))
    return outs[0] if exch is None else (outs[0], extra)


def mm_fused(name, m, a_ins, bs, mode, kdim, prologue=None, a_outs=(), e_ins=(), consts=(), epilogue=None, outs=(),
             alias_outs=(), accs=(), tm=512, tk=1024, exch=None):
    tm = min(tm, m)
    nk = kdim // tk
    n = bs[0][0].shape[1 if mode == "nn" else 0]
    n_a, n_b, n_e, n_c = len(a_ins), len(bs), len(e_ins), len(consts)
    n_ao, n_o, n_al, n_ac = len(a_outs), len(outs), len(alias_outs), len(accs)
    held = [a for (a, _, _) in alias_outs if not isinstance(a, jax.ShapeDtypeStruct)]
    dot = _dot if mode == "nn" else _dot_nt

    def body(*refs):
        a_refs, b_refs = refs[:n_a], refs[n_a:n_a + n_b]
        e_refs = refs[n_a + n_b:n_a + n_b + n_e + n_c]
        o0 = n_a + n_b + n_e + n_c + len(held)
        ao_refs = refs[o0:o0 + n_ao]
        out_refs = refs[o0 + n_ao:o0 + n_ao + n_o + n_al]
        acc_refs = refs[o0 + n_ao + n_o + n_al:o0 + n_ao + n_o + n_al + n_ac]
        scr = refs[o0 + n_ao + n_o + n_al + n_ac:]
        i, k = pl.program_id(0), pl.program_id(1)
        tiles = [r[...] for r in a_refs]
        a_list, extra = (tiles, []) if prologue is None else prologue(*tiles)
        for r, v in zip(ao_refs, extra):
            r[...] = v.astype(r.dtype)
        part = None
        for a, b_ref in zip(a_list, b_refs):
            prod = dot(a.astype(BF16), b_ref[...].astype(BF16))
            part = prod if part is None else part + prod

        def finish(total):
            vals = epilogue(total, *[r[...] for r in e_refs])
            if not isinstance(vals, (tuple, list)):
                vals = (vals,)
            for r, v in zip(out_refs, vals[:n_o + n_al]):
                r[...] = v.astype(r.dtype)
            for r, v in zip(acc_refs, vals[n_o + n_al:]):
                @pl.when(i == 0)
                def _():
                    r[...] = v

                @pl.when(i > 0)
                def _():
                    r[...] += v

        if nk == 1:
            finish(part)
            return
        acc_ref = scr[0]

        @pl.when(k == 0)
        def _():
            acc_ref[...] = part

        @pl.when(k > 0)
        def _():
            acc_ref[...] += part

        @pl.when(k == nk - 1)
        def _():
            finish(acc_ref[...])

    in_specs = [pl.BlockSpec((tm, tk), lambda i, k, off=off: (i, k + off)) for (_, off) in a_ins]
    if mode == "nn":
        in_specs += [pl.BlockSpec((tk, n), lambda i, k, off=off: (k + off, 0)) for (_, off) in bs]
    else:
        in_specs += [pl.BlockSpec((n, tk), lambda i, k, off=off: (0, k + off)) for (_, off) in bs]
    in_specs += [pl.BlockSpec((tm, w), lambda i, k, cb=cb: (i, cb)) for (_, w, cb) in e_ins]
    in_specs += [pl.BlockSpec(c.shape, lambda i, k, nd=c.ndim: (0,) * nd) for c in consts]
    in_specs += [ANY for _ in held]
    out_shape = [jax.ShapeDtypeStruct((m, kdim), dt) for dt in a_outs]
    out_specs = [pl.BlockSpec((tm, tk), lambda i, k: (i, k)) for _ in a_outs]
    out_shape += [jax.ShapeDtypeStruct((m, w), dt) for (w, dt) in outs]
    out_specs += [pl.BlockSpec((tm, w), lambda i, k: (i, 0)) for (w, _) in outs]
    out_shape += [jax.ShapeDtypeStruct(a.shape, a.dtype) for (a, _, _) in alias_outs]
    out_specs += [pl.BlockSpec((tm, w), lambda i, k, cb=cb: (i, cb)) for (_, w, cb) in alias_outs]
    out_shape += [jax.ShapeDtypeStruct(s_, F32) for s_ in accs]
    out_specs += [pl.BlockSpec(s_, lambda i, k: (0, 0)) for s_ in accs]
    aliases, k_in = {}, n_a + n_b + n_e + n_c
    for j, (a, _, _) in enumerate(alias_outs):
        if not isinstance(a, jax.ShapeDtypeStruct):
            aliases[k_in] = n_ao + n_o + j
            k_in += 1
    operands = [a for (a, _) in a_ins] + [b for (b, _) in bs] + [a for (a, _, _) in e_ins] + list(consts) + held
    res, extra = hosted_call(
        body, exch, name, (m // tm, nk), in_specs, out_specs, out_shape,
        [pltpu.VMEM((tm, n), F32)] if nk > 1 else [], operands,
        ("arbitrary" if n_ac else "parallel", "arbitrary"), aliases)
    return res if exch is None else (res, extra)


def _cumsum_rows(x):
    n = x.shape[0]
    row = lax.broadcasted_iota(jnp.int32, x.shape, 0)
    s = 1
    while s < n:
        x = x + jnp.where(row >= s, pltpu.roll(x, s, 0), 0.0)
        s *= 2
    return x


def _hg_prep(zq, zf, lb, reverse):
    n = zq.shape[0]
    q = _silu(zq)
    sig = _sigmoid(zf)
    sn = 1.0 - sig
    f = lb + (1.0 - lb) * sig
    k = (1.0 - lb) * sn
    g = jnp.log(jnp.maximum(f, TINY))
    b = _cumsum_rows(g)
    if reverse:
        b = b[n - 1:n] - b + g
    b_last = b[0:1] if reverse else b[n - 1:n]
    b_ref = b[n // 2:n // 2 + 1]
    e1 = jnp.exp(b)
    e2 = jnp.exp(jnp.clip(b - b_ref, -EXP_CLAMP, EXP_CLAMP))
    e3 = jnp.exp(jnp.clip(b_ref - b, -EXP_CLAMP, EXP_CLAMP))
    e4 = jnp.exp(b_last - b)
    return dict(q=q, k=k, sig=sig, sn=sn, f=f, e1=e1, e2=e2, e3=e3, e4=e4, e_last=jnp.exp(b_last),
                qe=(q * e1).astype(BF16), qt=(q * e2).astype(BF16), kt=(k * e3).astype(BF16),
                ks=(k * e4).astype(BF16))


def _hg_mask(n, reverse):
    t = lax.broadcasted_iota(jnp.int32, (n, n), 0)
    s = lax.broadcasted_iota(jnp.int32, (n, n), 1)
    return (s >= t) if reverse else (s <= t)


def hgrn_fwd(name, z, lb_f, lb_b, exch=None, unroll=False):
    m = z.shape[0]
    C, T = HG_CHUNK, min(HG_BLOCK_FWD, m)
    nb, cpb = m // T, T // C

    def body(zq_f, zf_f, zi_f, zq_b, zf_b, zi_b, lbf_ref, lbb_ref, of_ref, ob_ref, sf_ref, sb_ref, st_ref):
        @pl.when(pl.program_id(0) == 0)
        def _():
            st_ref[...] = jnp.zeros(st_ref.shape, F32)

        dirs = ((zq_f, zf_f, zi_f, lbf_ref, of_ref, sf_ref), (zq_b, zf_b, zi_b, lbb_ref, ob_ref, sb_ref))

        def chunk(ci, carry):
            work = []
            for d, (zq, zf, zi, lb_ref, o_ref, s_ref) in enumerate(dirs):
                cc = ci if d == 0 else cpb - 1 - ci
                rows = pl.ds(pl.multiple_of(cc * C, C), C)
                pre = _hg_prep(zq[rows, :], zf[rows, :], lb_ref[...], d == 1)
                v = zi[rows, :].astype(BF16)
                work.append((cc, rows, pre, v, [st_ref[d, h] for h in range(HEADS)]))
            heads = [(d, h, slice(h * HEAD_DIM, (h + 1) * HEAD_DIM)) for d in range(2) for h in range(HEADS)]
            first = {}
            for d, h, sl in heads:
                _, _, pre, v, sts = work[d]
                first[d, h] = (_dot_nt(pre["qt"][:, sl], pre["kt"][:, sl]),
                               _dot_nt(pre["qe"][:, sl], sts[h].astype(BF16)),
                               _dot_tn(v[:, sl], pre["ks"][:, sl]))
            results = [([], []), ([], [])]
            for d, h, sl in heads:
                _, _, pre, v, sts = work[d]
                scores, o_inter, st_add = first[d, h]
                a = jnp.where(_hg_mask(C, d == 1), scores, 0.0).astype(BF16)
                results[d][0].append(o_inter + _dot(a, v[:, sl]))
                results[d][1].append(sts[h] * pre["e_last"][:, sl] + st_add)
            results = [(jnp.concatenate(o_parts, axis=1), new_sts) for (o_parts, new_sts) in results]
            for d, (zq, zf, zi, lb_ref, o_ref, s_ref) in enumerate(dirs):
                cc, rows, _, _, sts = work[d]
                o_ref[rows, :] = results[d][0]
                for h in range(HEADS):
                    s_ref[cc, h] = sts[h]
                    st_ref[d, h] = results[d][1][h]
            return carry

        lax.fori_loop(0, cpb, chunk, 0, unroll=unroll)

    def zspec(cb, rev):
        return pl.BlockSpec((T, D), (lambda i: (nb - 1 - i, cb)) if rev else (lambda i: (i, cb)))

    def sspec(rev):
        shape = (cpb, HEADS, HEAD_DIM, HEAD_DIM)
        return pl.BlockSpec(shape, (lambda i: (nb - 1 - i, 0, 0, 0)) if rev else (lambda i: (i, 0, 0, 0)))

    lbspec = pl.BlockSpec((1, D), lambda i: (0, 0))
    states = jax.ShapeDtypeStruct((m // C, HEADS, HEAD_DIM, HEAD_DIM), F32)
    outs, extra = hosted_call(
        body, exch, name, (nb,),
        [zspec(ZQ, False), zspec(ZFF, False), zspec(ZI, False), zspec(ZQ, True), zspec(ZFB, True), zspec(ZI, True),
         lbspec, lbspec],
        [zspec(0, False), zspec(0, True), sspec(False), sspec(True)],
        [jax.ShapeDtypeStruct((m, D), F32), jax.ShapeDtypeStruct((m, D), F32), states, states],
        [pltpu.VMEM((2, HEADS, HEAD_DIM, HEAD_DIM), F32)], [z, z, z, z, z, z, lb_f, lb_b], ("arbitrary",))
    return outs, extra


def hgrn_bwd(name, z, d_o, s_f, s_b, lb_f, lb_b, exch=None, unroll=False):
    m = z.shape[0]
    C, T = HG_CHUNK, min(HG_BLOCK_BWD, m)
    nb, cpb = m // T, T // C

    def body(zq_f, zf_f, zi_f, do_f, sf_ref, zq_b, zf_b, zi_b, do_b, sb_ref, lbf_ref, lbb_ref,
             dqf_ref, dvf_ref, dqb_ref, dvb_ref, dzf_f, dzf_b, dlbf_ref, dlbb_ref,
             dst_ref):
        @pl.when(pl.program_id(0) == 0)
        def _():
            dst_ref[...] = jnp.zeros(dst_ref.shape, F32)
            dlbf_ref[...] = jnp.zeros(dlbf_ref.shape, F32)
            dlbb_ref[...] = jnp.zeros(dlbb_ref.shape, F32)

        dirs = ((zq_f, zf_f, zi_f, do_f, sf_ref, lbf_ref, dqf_ref, dvf_ref, dzf_f, dlbf_ref),
                (zq_b, zf_b, zi_b, do_b, sb_ref, lbb_ref, dqb_ref, dvb_ref, dzf_b, dlbb_ref))

        def chunk(ci, carry):
            work = []
            for d, (zq, zf, zi, do_ref, s_ref, lb_ref, dq_ref, dv_ref, dzf_ref, dlb_ref) in enumerate(dirs):
                cc = cpb - 1 - ci if d == 0 else ci
                rows = pl.ds(pl.multiple_of(cc * C, C), C)
                lb = lb_ref[...]
                pre = _hg_prep(zq[rows, :], zf[rows, :], lb, d == 1)
                work.append((rows, lb, pre, zi[rows, :].astype(BF16), do_ref[rows, :],
                             [s_ref[cc, h] for h in range(HEADS)], [dst_ref[d, h] for h in range(HEADS)],
                             dlb_ref[...]))
            heads = [(d, h, slice(h * HEAD_DIM, (h + 1) * HEAD_DIM)) for d in range(2) for h in range(HEADS)]
            first = {}
            for d, h, sl in heads:
                _, _, pre, v, do, st_prevs, dsts, _ = work[d]
                dst16 = dsts[h].astype(BF16)
                first[d, h] = (_dot_nt(pre["qt"][:, sl], pre["kt"][:, sl]),
                               _dot_nt(do[:, sl], v[:, sl]),
                               _dot(do[:, sl], st_prevs[h].astype(BF16)),
                               _dot(v[:, sl], dst16),
                               _dot_nt(pre["ks"][:, sl], dst16),
                               _dot_tn(do[:, sl], pre["qe"][:, sl]))
            parts = [[[] for _ in range(6)] for _ in range(2)]
            for d, h, sl in heads:
                _, _, pre, v, do, st_prevs, dsts, _ = work[d]
                scores, dscores, dq_inter, dk_state, dv_state, dst_add = first[d, h]
                mask = _hg_mask(C, d == 1)
                a = jnp.where(mask, scores, 0.0).astype(BF16)
                da = jnp.where(mask, dscores, 0.0).astype(BF16)
                dq_p, dki_p, dks_p, dv_p, rr_p, new_dsts = parts[d]
                dq_p.append(_dot(da, pre["kt"][:, sl]) * pre["e2"][:, sl] + dq_inter * pre["e1"][:, sl])
                dki_p.append(_dot_tn(da, pre["qt"][:, sl]) * pre["e3"][:, sl])
                dks_p.append(dk_state * pre["e4"][:, sl])
                dv_p.append(_dot_tn(a, do[:, sl]) + dv_state)
                rr_p.append(pre["e_last"][:, sl] * _colsum(dsts[h] * st_prevs[h]))
                new_dsts.append(dsts[h] * pre["e_last"][:, sl] + dst_add)
            results = []
            for d, (rows, lb, pre, v, do, st_prevs, dsts, dlb_old) in enumerate(work):
                rev = d == 1
                dq_p, dki_p, dks_p, dv_p, rr_p, new_dsts = parts[d]
                dq, dki, dks, dv, rr = (jnp.concatenate(p_, axis=1) for p_ in (dq_p, dki_p, dks_p, dv_p, rr_p))
                x = pre["q"] * dq - pre["k"] * dki
                y = pre["k"] * dks
                if rev:
                    dg = _cumsum_rows(x - y) + _colsum(y) + rr
                else:
                    dg = _cumsum_rows(y - x) + (x - y) + _colsum(x) + rr
                inv_f = jnp.where(pre["f"] > TINY, 1.0 / pre["f"], 0.0)
                u = dg * inv_f - (dki + dks)
                results.append((dq, dv, (1.0 - lb) * pre["sig"] * pre["sn"] * u, dlb_old + _colsum(pre["sn"] * u),
                                new_dsts))
            for d, (zq, zf, zi, do_ref, s_ref, lb_ref, dq_ref, dv_ref, dzf_ref, dlb_ref) in enumerate(dirs):
                rows = work[d][0]
                dq, dv, dzf, dlb, new_dsts = results[d]
                dq_ref[rows, :] = dq.astype(dq_ref.dtype)
                dv_ref[rows, :] = dv.astype(dv_ref.dtype)
                dzf_ref[rows, :] = dzf.astype(dzf_ref.dtype)
                dlb_ref[...] = dlb
                for h in range(HEADS):
                    dst_ref[d, h] = new_dsts[h]
            return carry

        lax.fori_loop(0, cpb, chunk, 0, unroll=unroll)

    def rspec(cb, rev):
        return pl.BlockSpec((T, D), (lambda i: (i, cb)) if rev else (lambda i: (nb - 1 - i, cb)))

    def sspec(rev):
        shape = (cpb, HEADS, HEAD_DIM, HEAD_DIM)
        return pl.BlockSpec(shape, (lambda i: (i, 0, 0, 0)) if rev else (lambda i: (nb - 1 - i, 0, 0, 0)))

    lbspec = pl.BlockSpec((1, D), lambda i: (0, 0))
    half = jax.ShapeDtypeStruct((m, D), BF16)
    row = jax.ShapeDtypeStruct((1, D), F32)
    outs, extra = hosted_call(
        body, exch, name, (nb,),
        [rspec(ZQ, False), rspec(ZFF, False), rspec(ZI, False), rspec(0, False), sspec(False),
         rspec(ZQ, True), rspec(ZFB, True), rspec(ZI, True), rspec(0, True), sspec(True), lbspec, lbspec],
        [rspec(0, False), rspec(0, False), rspec(0, True), rspec(0, True), rspec(0, False), rspec(0, True),
         lbspec, lbspec],
        [half, half, half, half, half, half, row, row],
        [pltpu.VMEM((2, HEADS, HEAD_DIM, HEAD_DIM), F32)],
        [z, z, z, d_o, s_f, z, z, z, d_o, s_b, lb_f, lb_b], ("arbitrary",))
    return outs, extra


def _heads(fn, *arrs):
    res = [fn(*[a[:, h * HEAD_DIM:(h + 1) * HEAD_DIM] for a in arrs]) for h in range(HEADS)]
    return [jnp.concatenate(parts, axis=1) for parts in zip(*res)]


def _hg_post(o_f, o_b, zg, g):
    def head(of, ob, zgh, gh):
        on, _ = _rms(of + ob)
        return (on * gh * _silu(zgh),)
    return _heads(head, o_f, o_b, zg, g)[0]


def _hg_post_bwd(da, o_f, o_b, zg, g):
    def head(dah, of, ob, zgh, gh):
        on, r = _rms(of + ob)
        sg = _silu(zgh)
        d_on = dah * sg
        return _rms_bwd(d_on, on, r, gh), dah * on * gh * _silu_grad(zgh), d_on * on
    d_o, dzg, dg = _heads(head, da, o_f, o_b, zg, g)
    return d_o, dzg, _colsum(dg)


def _sg_parts(zv, ln_g, ln_b):
    vg = _gelu(zv)
    xc = vg - _mean(vg)
    rstd = lax.rsqrt(_mean(xc * xc) + EPS)
    vh = xc * rstd
    return vh, rstd, vh * ln_g + ln_b


def _sg_lane_group(shape):
    return lax.broadcasted_iota(jnp.int32, shape, 1) < SG_GROUP_DIM


def _sg_mix(w, v16, transpose):
    rows = v16.shape[0]
    out = []
    for c in range(rows // SG_CHUNK):
        parts = []
        for j in range(SG_WIDTH // 128):
            vj = v16[c * SG_CHUNK:(c + 1) * SG_CHUNK, j * 128:(j + 1) * 128]
            w0 = w[(2 * j) * SG_CHUNK:(2 * j + 1) * SG_CHUNK]
            w1 = w[(2 * j + 1) * SG_CHUNK:(2 * j + 2) * SG_CHUNK]
            dot = _dot_tn if transpose else _dot
            parts.append(jnp.where(_sg_lane_group((SG_CHUNK, 128)), dot(w0, vj), dot(w1, vj)))
        out.append(jnp.concatenate(parts, axis=1))
    return jnp.concatenate(out, axis=0)


def _sg_fwd(zu, zv, w, bias, ln_g, ln_b):
    _, _, v = _sg_parts(zv, ln_g, ln_b)
    reps = zu.shape[0] // SG_CHUNK
    return _gelu(zu) * (_sg_mix(w, v.astype(BF16), False) + jnp.concatenate([bias] * reps, axis=0))


def _sg_bwd(db, zu, zv, w, bias, ln_g, ln_b):
    vh, rstd, v = _sg_parts(zv, ln_g, ln_b)
    v16 = v.astype(BF16)
    reps = zu.shape[0] // SG_CHUNK
    sg = _sg_mix(w, v16, False) + jnp.concatenate([bias] * reps, axis=0)
    dzu = db * sg * _gelu_grad(zu)
    dsg = db * _gelu(zu)
    dsg16 = dsg.astype(BF16)
    dv = _sg_mix(w, dsg16, True)
    low = _sg_lane_group((SG_CHUNK, 128))
    dw = []
    for g in range(SG_WIDTH // SG_GROUP_DIM):
        j, keep = g // 2, (low if g % 2 == 0 else jnp.logical_not(low))
        acc = jnp.zeros((SG_CHUNK, SG_CHUNK), F32)
        for c in range(reps):
            rows = slice(c * SG_CHUNK, (c + 1) * SG_CHUNK)
            dj = jnp.where(keep, dsg16[rows, j * 128:(j + 1) * 128], jnp.zeros((), BF16))
            acc = acc + _dot_nt(dj, v16[rows, j * 128:(j + 1) * 128])
        dw.append(acc)
    dbias = sum(dsg[c * SG_CHUNK:(c + 1) * SG_CHUNK] for c in range(reps))
    dvh = dv * ln_g
    dvg = rstd * (dvh - _mean(dvh) - vh * _mean(dvh * vh))
    dzuv = jnp.concatenate([dzu, dvg * _gelu_grad(zv)], axis=1)
    return (dzuv, jnp.concatenate(dw, axis=0), dbias, _colsum(dv * vh), _colsum(dv))


def lower_bounds(name, gamma_f, gamma_b):
    def body(gf_ref, gb_ref, lf_ref, lb_ref):
        for g_ref, o_ref in ((gf_ref, lf_ref), (gb_ref, lb_ref)):
            g0, g1 = g_ref[0:1, :], g_ref[1:2, :]
            mx = jnp.maximum(g0, g1)
            e0, e1 = jnp.exp(g0 - mx), jnp.exp(g1 - mx)
            sm0, sm1 = e0 / (e0 + e1), e1 / (e0 + e1)
            o_ref[0:1, :] = sm0 - sm0
            o_ref[1:2, :] = (sm0 + sm1) - sm0
    shp = jax.ShapeDtypeStruct(gamma_f.shape, F32)
    return pl.pallas_call(body, name=name, out_shape=[shp, shp])(gamma_f, gamma_b)


def lower_bounds_bwd(name, gamma_f, gamma_b, dlb_f, dlb_b):
    def body(gf_ref, gb_ref, df_ref, db_ref, of_ref, ob_ref):
        for g_ref, d_ref, o_ref in ((gf_ref, df_ref, of_ref), (gb_ref, db_ref, ob_ref)):
            g0, g1 = g_ref[0:1, :], g_ref[1:2, :]
            mx = jnp.maximum(g0, g1)
            e0, e1 = jnp.exp(g0 - mx), jnp.exp(g1 - mx)
            sm0, sm1 = e0 / (e0 + e1), e1 / (e0 + e1)
            d1 = d_ref[1:2, :] * sm0 * sm1
            o_ref[0:1, :] = -d1
            o_ref[1:2, :] = d1
    shp = jax.ShapeDtypeStruct(gamma_f.shape, F32)
    return pl.pallas_call(body, name=name, out_shape=[shp, shp])(gamma_f, gamma_b, dlb_f, dlb_b)


def _row(a, l):
    return a[l:l + 1]


class LocalPlan:
    def __init__(self, weights):
        self.W = weights
        self.grads = [dict() for _ in range(DEPTH)]

    def exch(self, host):
        return None

    def done(self, host, outs):
        pass

    def early_small(self, packed):
        pass


def local_step(x, p, target, S, plan):
    m = x.shape[0]

    def hmm(tag, *args, **kw):
        ex = plan.exch(tag)
        res = mm(tag, *args, exch=ex, **kw)
        if ex is None:
            return res
        plan.done(tag, res[1])
        return res[0]

    lb_f, lb_b = lower_bounds("lower_bounds", S["lb_gamma_fwd"], S["lb_gamma_bwd"])
    saved = []
    for l in range(DEPTH):
        t = f"l{l}_"
        W = plan.W[l]
        tm = 2048
        in_tile = (1024, 2048)
        ffn_tile = (2048, 2048)
        g_pre, g_post = _row(S["norm_mix_pre"], l), _row(S["norm_mix_post"], l)
        g_fpre, g_fpost = _row(S["norm_ffn_pre"], l), _row(S["norm_ffn_post"], l)
        hg_g = _row(S["hg_norm"], l)
        sg_w = S["sg_w"][l].reshape(SG_WIDTH // SG_GROUP_DIM * SG_CHUNK, SG_CHUNK).astype(BF16)
        sg_bias = jnp.repeat(S["sg_b"][l].T, SG_GROUP_DIM, axis=1)
        ln_g, ln_b = _row(S["sg_ln_g"], l), _row(S["sg_ln_b"], l)
        lbf, lbb = _row(lb_f, l), _row(lb_b, l)

        if l == 0:
            (h,) = rowwise(t + "pre_norm", lambda xv, g: (_rms(xv)[0] * g,), m, ins=[(x, D, 0)], consts=[g_pre],
                           outs=[(D, BF16)])
        z = hmm(t + "in_proj", h, W["w_in"], "nn", tm=in_tile[0], tn=in_tile[1])
        (o_f, o_b, s_f, s_b), extra = hgrn_fwd(t + "hgrn_fwd", z, lbf, lbb, exch=plan.exch(t + "hgrn_fwd"))
        plan.done(t + "hgrn_fwd", extra)
        (a_out,) = rowwise(t + "hgrn_post", _hg_post, m, ins=[(o_f, D, 0), (o_b, D, 0), (z, D, ZG)], consts=[hg_g],
                           outs=[(D, BF16)])
        (b_out,) = rowwise(t + "sgu_fwd", _sg_fwd, m, ins=[(z, SG_WIDTH, ZU), (z, SG_WIDTH, ZV)],
                           consts=[sg_w, sg_bias, ln_g, ln_b], outs=[(SG_WIDTH, BF16)])
        pa = mm(t + "proj_a", a_out, W["w_a"], "nn", BF16)
        pb = mm(t + "proj_b", b_out, W["w_b"], "nn", BF16)

        def merge_pro(a, b, ga, gb):
            mg = (_sigmoid(ga) * a + _sigmoid(gb) * b).astype(BF16)
            return [mg], [mg]

        def post_pre(mixv, xv, gp, gf):
            x1 = xv + _rms(mixv)[0] * gp
            return mixv, x1, _rms(x1)[0] * gf
        merged, mix, x1, h2 = mm_fused(
            t + "out_proj", m, [(pa, 0), (pb, 0), (z, GA), (z, GB)], [(W["w_out"], 0)], "nn", D, prologue=merge_pro,
            a_outs=[BF16], e_ins=[(x, D, 0)], consts=[g_post, g_fpre], epilogue=post_pre,
            outs=[(D, F32), (D, F32), (D, BF16)])
        gu = hmm(t + "ffn_in", h2, W["w_gu"], "nn", BF16, tm=ffn_tile[0], tn=ffn_tile[1])

        def act_pro(gt, up):
            hd = (_silu(gt.astype(F32)) * up).astype(BF16)
            return [hd], [hd]
        hid, ff, x2 = mm_fused(
            t + "ffn_out", m, [(gu, 0), (gu, FFN_PAD // 1024)], [(W["w_down"], 0)], "nn", FFN_PAD, prologue=act_pro,
            a_outs=[BF16], e_ins=[(x1, D, 0)], consts=[g_fpost],
            epilogue=lambda f, xv, g: (f, xv + _rms(f)[0] * g), outs=[(D, F32), (D, F32)])
        e = mm(t + "ple_proj", (p, l), W["w_ple"], "nn")

        if l + 1 < DEPTH:
            def ple_add(tv, xv, ev, g):
                x3 = xv + ev * _sigmoid(tv)
                return tv, x3, _rms(x3)[0] * g
            tg, x3, h_next = mm_fused(
                t + "ple_gate", m, [(x2, 0)], [(W["w_ple_gate"], 0)], "nn", D, e_ins=[(x2, D, 0), (e, D, 0)],
                consts=[_row(S["norm_mix_pre"], l + 1)], epilogue=ple_add, outs=[(D, F32), (D, F32), (D, BF16)])
        else:
            tg, x3 = mm_fused(
                t + "ple_gate", m, [(x2, 0)], [(W["w_ple_gate"], 0)], "nn", D, e_ins=[(x2, D, 0), (e, D, 0)],
                epilogue=lambda tv, xv, ev: (tv, xv + ev * _sigmoid(tv)), outs=[(D, F32), (D, F32)])
            h_next = None
        saved.append(dict(x=x, h=h, z=z, o_f=o_f, o_b=o_b, s_f=s_f, s_b=s_b, a_out=a_out, b_out=b_out, pa=pa, pb=pb,
                          merged=merged, mix=mix, x1=x1, h2=h2, gu=gu, hid=hid, ff=ff, x2=x2, e=e, tg=tg,
                          sg_w=sg_w, sg_bias=sg_bias))
        x, h = x3, h_next

    def loss_fn(y, tv):
        err = y - tv
        return err * (1.0 / D), _colsum(err * err)
    dx, loss_cols = rowwise("loss", loss_fn, m, ins=[(x, D, 0), (target, D, 0)], outs=[(D, F32)], accs=[(1, D)])

    gs = {n: [None] * DEPTH for n in SMALL}
    dlb_f, dlb_b = [None] * DEPTH, [None] * DEPTH

    for l in reversed(range(DEPTH)):
        t = f"l{l}_bwd_"
        sv, W = saved[l], plan.W[l]
        tm, tk = 2048, 4096
        g_pre, g_post = _row(S["norm_mix_pre"], l), _row(S["norm_mix_post"], l)
        g_fpre, g_fpost = _row(S["norm_ffn_pre"], l), _row(S["norm_ffn_post"], l)
        hg_g = _row(S["hg_norm"], l)
        ln_g, ln_b = _row(S["sg_ln_g"], l), _row(S["sg_ln_b"], l)
        lbf, lbb = _row(lb_f, l), _row(lb_b, l)

        def wgrad(nm, tag, a, b):
            a_dtype = (a[0] if isinstance(a, tuple) else a).dtype
            plan.grads[l][nm] = mm(tag, a, b, "tn", BF16, tk=tk if a_dtype == BF16 else 2048)

        def ple_pro(d3, ev, tv):
            s = _sigmoid(tv)
            de_, dt_ = (d3 * s).astype(BF16), (d3 * ev * s * (1.0 - s)).astype(BF16)
            return [dt_], [dt_, de_]

        def ffn_post_bwd(d2p, d3, f, g):
            d2 = d3 + d2p
            fh, r = _rms(f)
            return d2, _rms_bwd(d2, fh, r, g), _colsum(d2 * fh)
        dt, de, dx2, dff, gs["norm_ffn_post"][l] = mm_fused(
            t + "ple_gate_dx", m, [(dx, 0), (sv["e"], 0), (sv["tg"], 0)], [(W["w_ple_gate"], 0)], "nt", D,
            prologue=ple_pro, a_outs=[BF16, BF16], e_ins=[(dx, D, 0), (sv["ff"], D, 0)], consts=[g_fpost],
            epilogue=ffn_post_bwd, outs=[(D, F32), (D, BF16)], accs=[(1, D)])
        wgrad("w_ple", t + "w_ple", (p, l), de)
        wgrad("w_ple_gate", t + "w_ple_gate", sv["x2"], dt)
        wgrad("w_down", t + "w_down", sv["hid"], dff)
        dhid = mm(t + "ffn_out_dx", dff, W["w_down"], "nt", BF16, tm=tm)

        def act_bwd(dh, gt, up):
            dh, gt = dh.astype(F32), gt.astype(F32)
            dg_, du_ = (dh * up * _silu_grad(gt)).astype(BF16), (dh * _silu(gt)).astype(BF16)
            return [dg_, du_], [dg_, du_]

        def pre_post_bwd(dh, d2, x1v, mixv, gf, gp):
            xh, r1 = _rms(x1v)
            d1 = d2 + _rms_bwd(dh, xh, r1, gf)
            mh, rm = _rms(mixv)
            return d1, _rms_bwd(d1, mh, rm, gp), _colsum(dh * xh), _colsum(d1 * mh)
        off = FFN_PAD // 1024
        dgate, dup, dx1, dmix, gs["norm_ffn_pre"][l], gs["norm_mix_post"][l] = mm_fused(
            t + "ffn_in_dx", m, [(dhid, 0), (sv["gu"], 0), (sv["gu"], off)], [(W["w_gu"], 0), (W["w_gu"], off)], "nt",
            FFN_PAD, prologue=act_bwd, a_outs=[BF16, BF16], e_ins=[(dx2, D, 0), (sv["x1"], D, 0), (sv["mix"], D, 0)],
            consts=[g_fpre, g_post], epilogue=pre_post_bwd, outs=[(D, F32), (D, BF16)], accs=[(1, D), (1, D)])
        wgrad("w_gate", t + "w_gate", sv["h2"], dgate)
        wgrad("w_up", t + "w_up", sv["h2"], dup)
        wgrad("w_out", t + "w_out", sv["merged"], dmix)

        def merge_bwd(dm, a, b, gab):
            sa, sb = _sigmoid(gab[:, :D]), _sigmoid(gab[:, D:])
            dgab = jnp.concatenate([dm * a * sa * (1.0 - sa), dm * b * sb * (1.0 - sb)], axis=1)
            return dm * sa, dm * sb, dgab
        dpa, dpb, dz = mm_fused(
            t + "out_proj_dx", m, [(dmix, 0)], [(W["w_out"], 0)], "nt", D,
            e_ins=[(sv["pa"], D, 0), (sv["pb"], D, 0), (sv["z"], 2 * D, 3)], epilogue=merge_bwd,
            outs=[(D, BF16), (D, BF16)], alias_outs=[(jax.ShapeDtypeStruct((m, N_IN), BF16), 2 * D, 3)])
        wgrad("w_a", t + "w_a", sv["a_out"], dpa)
        wgrad("w_b", t + "w_b", sv["b_out"], dpb)
        db = mm(t + "proj_b_dx", dpb, W["w_b"], "nt")

        dz, dsw, dbias, gs["sg_ln_g"][l], gs["sg_ln_b"][l] = rowwise(
            t + "sgu", _sg_bwd, m, ins=[(db, SG_WIDTH, 0), (sv["z"], SG_WIDTH, ZU), (sv["z"], SG_WIDTH, ZV)],
            consts=[sv["sg_w"], sv["sg_bias"], ln_g, ln_b], alias_outs=[(dz, 2 * SG_WIDTH, 5)],
            accs=[(SG_WIDTH // SG_GROUP_DIM * SG_CHUNK, SG_CHUNK), (SG_CHUNK, SG_WIDTH), (1, SG_WIDTH), (1, SG_WIDTH)])
        gs["sg_w"][l] = dsw.reshape(1, SG_WIDTH // SG_GROUP_DIM, SG_CHUNK, SG_CHUNK)
        gs["sg_b"][l] = dbias.reshape(SG_CHUNK, SG_WIDTH // SG_GROUP_DIM, SG_GROUP_DIM).sum(-1).T[None]

        d_o, dz, gs["hg_norm"][l] = mm_fused(
            t + "proj_a_dx", m, [(dpa, 0)], [(W["w_a"], 0)], "nt", D,
            e_ins=[(sv["o_f"], D, 0), (sv["o_b"], D, 0), (sv["z"], D, ZG)], consts=[hg_g], epilogue=_hg_post_bwd,
            outs=[(D, BF16)], alias_outs=[(dz, D, ZG)], accs=[(1, D)], tm=256)
        if l == 0:
            part = {n: (g if not isinstance(g, list) else jnp.concatenate(
                [jnp.zeros((1,) + g[1].shape[1:], F32) if gl is None else gl for gl in g], axis=0))
                for n, g in gs.items()}
            plan.early_small(_pack([part[n].reshape(S[n].shape) for n in SMALL]))
        (dq_f, dv_f, dq_b, dv_b, dzf_f, dzf_b, dlb_f[l], dlb_b[l]), extra = hgrn_bwd(
            t + "hgrn", sv["z"], d_o, sv["s_f"], sv["s_b"], lbf, lbb, exch=plan.exch(t + "hgrn"))
        plan.done(t + "hgrn", extra)

        def combine(dqf, dqb, dvf, dvb, dff_, dfb_, zq):
            dq = dqf.astype(F32) + dqb.astype(F32)
            dv = dvf.astype(F32) + dvb.astype(F32)
            return (jnp.concatenate([(dq * _silu_grad(zq)).astype(BF16), dff_, dfb_, dv.astype(BF16)], axis=1),)
        (dz,) = rowwise(t + "hgrn_combine", combine, m,
                        ins=[(dq_f, D, 0), (dq_b, D, 0), (dv_f, D, 0), (dv_b, D, 0), (dzf_f, D, 0), (dzf_b, D, 0),
                             (sv["z"], D, ZQ)], alias_outs=[(dz, 4 * D, 0)], tm=128)
        wgrad("w_in", t + "w_in", sv["h"], dz)

        def pre_bwd(dhv, d1, xv, g):
            xh, r = _rms(xv)
            return d1 + _rms_bwd(dhv, xh, r, g), _colsum(dhv * xh)
        ex = plan.exch(t + "in_proj_dx")
        res = mm_fused(t + "in_proj_dx", m, [(dz, 0)], [(W["w_in"], 0)], "nt", N_IN,
                       e_ins=[(dx1, D, 0), (sv["x"], D, 0)], consts=[g_pre], epilogue=pre_bwd, outs=[(D, F32)],
                       accs=[(1, D)], tm=1024, exch=ex)
        if ex is not None:
            res, extra = res
            plan.done(t + "in_proj_dx", extra)
        dx, gs["norm_mix_pre"][l] = res
        saved[l] = None
        if l == DEPTH - 1:
            none = jnp.zeros((1, D), F32)
            gs["lb_gamma_fwd"], gs["lb_gamma_bwd"] = lower_bounds_bwd(
                "lower_bounds_bwd", S["lb_gamma_fwd"], S["lb_gamma_bwd"], jnp.concatenate([none, dlb_f[l]], axis=0),
                jnp.concatenate([none, dlb_b[l]], axis=0))

    small ={n: (g if not isinstance(g, list) else jnp.concatenate(g, axis=0)).reshape(S[n].shape)
             for n, g in gs.items()}
    return loss_cols, dx, small


def cast_pad(name, w, rows_p, cols_p):
    _, r, c = w.shape

    def body(w_ref, o_ref):
        if (rows_p, cols_p) != (r, c):
            o_ref[...] = jnp.zeros(o_ref.shape, BF16)
        o_ref[0:r, 0:c] = w_ref[...].astype(BF16)

    return pl.pallas_call(
        body, name=name, grid=(DEPTH,), in_specs=[pl.BlockSpec((None, r, c), lambda l: (l, 0, 0))],
        out_specs=pl.BlockSpec((None, rows_p, cols_p), lambda l: (l, 0, 0)),
        out_shape=jax.ShapeDtypeStruct((DEPTH, rows_p, cols_p), BF16), compiler_params=_params(("parallel",)),
    )(w)


def _shard_shape(n, shape):
    axis, size, _, _ = LAYOUT[n]
    _, r, c = shape
    return (size, c) if axis == 0 else (r, size)


class DistPlan:
    def __init__(self, shards):
        self.shards = shards
        self.W = [dict() for _ in range(DEPTH)]
        self.grads = [dict() for _ in range(DEPTH)]
        self.slots = [dict() for _ in range(DEPTH)]
        rest = [n for n in BIG if n != "w_in"]
        ffn = ["w_gate", "w_up", "w_down"]
        self.schedule = {
            "l0_in_proj": ("gather", [(0, n) for n in rest]),
            "l0_hgrn_fwd": ("gather", [(1, n) for n in BIG if n not in ffn]),
            "l0_ffn_in": ("gather", [(1, n) for n in ffn]),
            "l1_bwd_hgrn": ("scatter", [(1, n) for n in rest]),
            "l1_bwd_in_proj_dx": ("scatter", [(1, "w_in")]),
            "l0_bwd_hgrn": ("scatter", [(0, n) for n in rest]),
            "l0_bwd_in_proj_dx": ("scatter", [(0, "w_in")]),
        }
        self.pending = {}
        self.small_part = self.small_slots = None
        self.done("start", exchange("gather_l0_w_in", self._gather("start", [(0, "w_in")])))

    def _gather(self, host, parts):
        srcs, dsts, items, keys = [], [], [], []
        for layer, n in parts:
            axis, size, dst, base = LAYOUT[n]
            if (layer, dst) not in keys:
                keys.append((layer, dst))
                dsts.append((GATHERED[dst], BF16))
            srcs.append(self.shards[n])
            items.append(("gather", len(srcs) - 1, keys.index((layer, dst)), axis, size, base, layer))
        self.pending[host] = ("gather", keys)
        return Exchange(srcs, dsts, items)

    def _scatter(self, host, parts):
        srcs, dsts, items = [], [], []
        for layer, n in parts:
            axis, size, _, _ = LAYOUT[n]
            srcs.append(self.grads[layer][n])
            dsts.append(((NDEV,) + _shard_shape(n, self.shards[n].shape), BF16))
            items.append(("scatter", len(srcs) - 1, len(dsts) - 1, axis, size, 0, None))
        keys = list(parts)
        if host == "l0_bwd_hgrn" and self.small_part is not None:
            srcs.append(self.small_part)
            dsts.append(((NDEV,) + self.small_part.shape, F32))
            items.append(("copies", len(srcs) - 1, len(dsts) - 1, 0, 0, 0, None))
            keys.append(("small", None))
        self.pending[host] = ("scatter", keys)
        return Exchange(srcs, dsts, items)

    def early_small(self, packed):
        self.small_part = packed

    def exch(self, host):
        if host not in self.schedule:
            return None
        kind, parts = self.schedule[host]
        return self._gather(host, parts) if kind == "gather" else self._scatter(host, parts)

    def done(self, host, outs):
        if host not in self.pending:
            return
        kind, keys = self.pending.pop(host)
        for (layer, n), arr in zip(keys, outs):
            if layer == "small":
                self.small_slots = arr
            else:
                (self.W if kind == "gather" else self.slots)[layer][n] = arr


def adam(name, w, m_, v_, tr, g=None, slots=None):
    L, r, c = w.shape
    assert r % tr == 0
    nt = r // tr
    n_s = 0 if slots is None else L

    def body(*refs):
        s_refs = refs[:n_s]
        g_ref = refs[n_s] if g is not None else None
        w_ref, m_ref, v_ref, g_out, d_out, m_out, v_out = refs[n_s + (g is not None):]

        def update(gv):
            if g_ref is not None:
                gv = gv + g_ref[...] if gv is not None else g_ref[...]
            m2 = B1 * m_ref[...] + (1.0 - B1) * gv
            v2 = B2 * v_ref[...] + (1.0 - B2) * (gv * gv)
            m_hat = m2 / (1.0 - B1 ** STEP)
            v_hat = v2 / (1.0 - B2 ** STEP)
            g_out[...] = gv
            d_out[...] = -LR * (m_hat / (jnp.sqrt(v_hat) + AEPS) + WD * w_ref[...])
            m_out[...] = m2
            v_out[...] = v2

        if slots is None:
            update(None)
            return
        for layer, s_ref in enumerate(s_refs):
            @pl.when(pl.program_id(0) == layer)
            def _():
                gv = s_ref[0][:, :c].astype(F32)
                for j in range(1, NDEV):
                    gv = gv + s_ref[j][:, :c].astype(F32)
                update(gv)

    spec = pl.BlockSpec((None, tr, c), lambda l, i: (l, i, 0))
    arrs, specs = [], []
    if slots is not None:
        assert len(slots) == L and L <= 2
        arrs = list(slots)
        cp = slots[0].shape[2]
        specs = [pl.BlockSpec((NDEV, tr, cp), lambda l, i: (0, i * (1 - l) + (nt - 1) * l, 0)),
                 pl.BlockSpec((NDEV, tr, cp), lambda l, i: (0, i * l, 0))][:L]
    if g is not None:
        arrs.append(g)
        specs.append(spec)
    shp = jax.ShapeDtypeStruct(w.shape, F32)
    return pl.pallas_call(
        body, name=name, grid=(L, nt), in_specs=specs + [spec, spec, spec], out_specs=[spec] * 4,
        out_shape=[shp] * 4, compiler_params=_params(("arbitrary", "arbitrary")),
    )(*arrs, w, m_, v_)


def _pack(arrs):
    parts = []
    for a in arrs:
        a2 = a.reshape(-1, D)
        parts.append(jnp.pad(a2, ((0, -a2.shape[0] % 8), (0, 0))))
    return jnp.concatenate(parts, axis=0)


def _unpack(buf, shapes):
    out, off = [], 0
    for s in shapes:
        rows = 1
        for d_ in s:
            rows *= d_
        rows //= D
        out.append(buf[off:off + rows].reshape(s))
        off += rows + (-rows % 8)
    return out


def kernel(x, p, norm_mix_pre, w_in, lb_gamma_fwd, lb_gamma_bwd, hg_norm, sg_w, sg_b, sg_ln_g, sg_ln_b, w_a, w_b, w_out, norm_mix_post, norm_ffn_pre, w_gate, w_up, w_down, norm_ffn_post, w_ple, w_ple_gate, loss_target, m_norm_mix_pre, m_w_in, m_lb_gamma_fwd, m_lb_gamma_bwd, m_hg_norm, m_sg_w, m_sg_b, m_sg_ln_g, m_sg_ln_b, m_w_a, m_w_b, m_w_out, m_norm_mix_post, m_norm_ffn_pre, m_w_gate, m_w_up, m_w_down, m_norm_ffn_post, m_w_ple, m_w_ple_gate, v_norm_mix_pre, v_w_in, v_lb_gamma_fwd, v_lb_gamma_bwd, v_hg_norm, v_sg_w, v_sg_b, v_sg_ln_g, v_sg_ln_b, v_w_a, v_w_b, v_w_out, v_norm_mix_post, v_norm_ffn_pre, v_w_gate, v_w_up, v_w_down, v_norm_ffn_post, v_w_ple, v_w_ple_gate):
    a = dict(zip(INPUTS, (x, p, norm_mix_pre, w_in, lb_gamma_fwd, lb_gamma_bwd, hg_norm, sg_w, sg_b, sg_ln_g, sg_ln_b, w_a, w_b, w_out, norm_mix_post, norm_ffn_pre, w_gate, w_up, w_down, norm_ffn_post, w_ple, w_ple_gate, loss_target, m_norm_mix_pre, m_w_in, m_lb_gamma_fwd, m_lb_gamma_bwd, m_hg_norm, m_sg_w, m_sg_b, m_sg_ln_g, m_sg_ln_b, m_w_a, m_w_b, m_w_out, m_norm_mix_post, m_norm_ffn_pre, m_w_gate, m_w_up, m_w_down, m_norm_ffn_post, m_w_ple, m_w_ple_gate, v_norm_mix_pre, v_w_in, v_lb_gamma_fwd, v_lb_gamma_bwd, v_hg_norm, v_sg_w, v_sg_b, v_sg_ln_g, v_sg_ln_b, v_w_a, v_w_b, v_w_out, v_norm_mix_post, v_norm_ffn_pre, v_w_gate, v_w_up, v_w_down, v_norm_ffn_post, v_w_ple, v_w_ple_gate)))
    m = x.shape[1]

    shards = {n: cast_pad("cast_" + n, a[n], *_shard_shape(n, a[n].shape)) for n in BIG}
    plan = DistPlan(shards)
    loss_cols, dx, gs = local_step(x[0], p[:, 0], loss_target[0], {n: a[n] for n in SMALL}, plan)
    loss = lax.psum(jnp.sum(loss_cols) * (0.5 / D), ("x", "y", "c"))

    small_shapes = [a[n].shape for n in SMALL]
    rows = plan.small_slots.shape[1]
    late = allreduce_small("allreduce_small", jnp.pad(gs["norm_mix_pre"][0:1], ((0, 7), (0, 0))))
    g_late = jnp.pad(late, ((0, rows - 8), (0, 0)))[None]

    res = {}
    row_tiles = {"w_in": 128, "w_a": 128, "w_b": 512, "w_out": 128, "w_gate": 128, "w_up": 128, "w_down": 88,
                 "w_ple": 256, "w_ple_gate": 128}
    for n in BIG:
        res[n] = adam("adam_" + n, a[n], a["m_" + n], a["v_" + n], row_tiles[n],
                      slots=[plan.slots[l][n] for l in range(DEPTH)])
    packed = [_pack([a[pre + n] for n in SMALL])[None] for pre in ("", "m_", "v_")]
    small_res = adam("adam_small", packed[0], packed[1], packed[2], rows // 2, g=g_late, slots=[plan.small_slots])
    small_res = [_unpack(r_[0], small_shapes) for r_ in small_res]
    for i, n in enumerate(SMALL):
        res[n] = tuple(small_res[k][i] for k in range(4))

    outs = [loss, dx.reshape(1, m, D)]
    for k in range(4):
        outs += [res[n][k] for n in WEIGHTS]
    return tuple(outs)
```

```python
import jax
import jax.numpy as jnp
from jax import lax
from jax.experimental import pallas as pl
from jax.experimental.pallas import tpu as pltpu

F32 = jnp.float32
BF16 = jnp.bfloat16

D = 1024
N_IN = 8192
HEADS = 8
HEAD_DIM = 128
SG_CHUNK = 128
SG_WIDTH = 512
SG_GROUP_DIM = 64
FFN = 2816
PLE_DIM = 256
EPS = 1e-6
DEPTH = 2
ZQ, ZFF, ZFB, ZI, ZG, GA, GB = 0, 1, 2, 3, 4, 6, 7
ZU, ZV = 10, 11

NDEV = 8
FFN_SHARD = FFN // NDEV
FFN_SHARD_PAD = 384
FFN_PAD = NDEV * FFN_SHARD_PAD

LR, B1, B2, AEPS, WD, STEP = 0.001, 0.9, 0.999, 1e-08, 0.01, 10

ROW_TILE = 256
HG_CHUNK = 64
HG_BLOCK_FWD = 256
HG_BLOCK_BWD = 128
EXP_CLAMP = 80.0
PROLOGUE_CHUNK = 256
TINY = float(jnp.finfo(jnp.float32).tiny)
VMEM_LIMIT = 56 * 1024 * 1024

BIG = ["w_in", "w_a", "w_b", "w_out", "w_gate", "w_up", "w_down", "w_ple", "w_ple_gate"]
SMALL = ["norm_mix_pre", "lb_gamma_fwd", "lb_gamma_bwd", "hg_norm", "sg_w", "sg_b", "sg_ln_g", "sg_ln_b",
         "norm_mix_post", "norm_ffn_pre", "norm_ffn_post"]
WEIGHTS = ["norm_mix_pre", "w_in", "lb_gamma_fwd", "lb_gamma_bwd", "hg_norm", "sg_w", "sg_b", "sg_ln_g", "sg_ln_b",
           "w_a", "w_b", "w_out", "norm_mix_post", "norm_ffn_pre", "w_gate", "w_up", "w_down", "norm_ffn_post",
           "w_ple", "w_ple_gate"]
INPUTS = (["x", "p"] + WEIGHTS + ["loss_target"] + ["m_" + n for n in WEIGHTS] + ["v_" + n for n in WEIGHTS])
LAYOUT = {
    "w_in": (1, 1024, "w_in", 0), "w_a": (0, 128, "w_a", 0), "w_b": (1, 128, "w_b", 0),
    "w_out": (0, 128, "w_out", 0), "w_gate": (1, FFN_SHARD_PAD, "w_gu", 0),
    "w_up": (1, FFN_SHARD_PAD, "w_gu", FFN_PAD), "w_down": (0, FFN_SHARD_PAD, "w_down", 0),
    "w_ple": (1, 128, "w_ple", 0), "w_ple_gate": (0, 128, "w_ple_gate", 0),
}
GATHERED = {"w_in": (D, N_IN), "w_a": (D, D), "w_b": (SG_WIDTH, D), "w_out": (D, D), "w_gu": (D, 2 * FFN_PAD),
            "w_down": (FFN_PAD, D), "w_ple": (PLE_DIM, D), "w_ple_gate": (D, D)}


def _params(sem):
    return pltpu.CompilerParams(dimension_semantics=sem, vmem_limit_bytes=VMEM_LIMIT)


def _dot(a, b):
    return lax.dot_general(a, b, (((1,), (0,)), ((), ())), preferred_element_type=F32)


def _dot_nt(a, b):
    return lax.dot_general(a, b, (((1,), (1,)), ((), ())), preferred_element_type=F32)


def _dot_tn(a, b):
    return lax.dot_general(a, b, (((0,), (0,)), ((), ())), preferred_element_type=F32)


def _sigmoid(x):
    return jax.nn.sigmoid(x)


def _silu(x):
    return x * _sigmoid(x)


def _silu_grad(x):
    s = _sigmoid(x)
    return s * (1.0 + x * (1.0 - s))


def _gelu(x):
    return 0.5 * x * (1.0 + lax.erf(x * 0.7071067811865476))


def _gelu_grad(x):
    return 0.5 * (1.0 + lax.erf(x * 0.7071067811865476)) + x * jnp.exp(-0.5 * x * x) * 0.3989422804014327


def _mean(x):
    return jnp.mean(x, axis=-1, keepdims=True)


def _colsum(x):
    return jnp.sum(x, axis=0, keepdims=True)


def _rms(x):
    r = lax.rsqrt(_mean(x * x) + EPS)
    return x * r, r


def _rms_bwd(dy, xh, r, g):
    dyg = dy * g
    return r * (dyg - xh * _mean(dyg * xh))


MESH = pl.DeviceIdType.MESH
ANY = pl.BlockSpec(memory_space=pl.ANY)


def _slab(ref, axis, start, size):
    idx = [slice(None)] * 2
    idx[axis] = pl.ds(start, size)
    return ref.at[tuple(idx)]


class Exchange:
    def __init__(self, srcs, dsts, items):
        self.srcs, self.dsts, self.items = list(srcs), list(dsts), list(items)

    def specs(self):
        n = len(self.items)
        sems = [pltpu.SemaphoreType.DMA((n * (NDEV - 1),)), pltpu.SemaphoreType.DMA((n * (NDEV - 1),)),
                pltpu.SemaphoreType.DMA((n,))]
        return ([ANY] * len(self.srcs), [ANY] * len(self.dsts),
                [jax.ShapeDtypeStruct(s, dt) for (s, dt) in self.dsts], sems)

    def copies(self, src, dst, send_sem, recv_sem, loc_sem):
        x, y, c = lax.axis_index("x"), lax.axis_index("y"), lax.axis_index("c")
        me = 4 * x + 2 * y + c
        starts, waits = [], []
        for n, (kind, si, di, axis, size, base, layer) in enumerate(self.items):
            def views(to_dev, from_dev):
                if kind == "gather":
                    return (src[si].at[layer],
                            _slab(dst[di], axis, base + pl.multiple_of(from_dev * size, 128), size))
                if kind == "copies":
                    return src[si], dst[di].at[from_dev]
                return _slab(src[si], axis, base + pl.multiple_of(to_dev * size, 128), size), dst[di].at[from_dev]

            s_own, d_own = views(me, me)
            own = pltpu.make_async_copy(s_own, d_own, loc_sem.at[n])
            starts.append(own)
            waits.append(own)
            for k in range(1, NDEV):
                px = 1 - x if k & 4 else x
                py = 1 - y if k & 2 else y
                pc = 1 - c if k & 1 else c
                peer = 4 * px + 2 * py + pc
                s_out, _ = views(peer, me)
                _, d_in = views(me, peer)
                sem = n * (NDEV - 1) + k - 1
                starts.append(pltpu.make_async_remote_copy(s_out, d_own, send_sem.at[sem], recv_sem.at[sem],
                                                           device_id=(px, py, pc), device_id_type=MESH))
                waits.append(pltpu.make_async_remote_copy(s_out, d_in, send_sem.at[sem], recv_sem.at[sem],
                                                          device_id=(px, py, pc), device_id_type=MESH))
        return starts, waits


def exchange(name, exch):
    e_in, e_out, e_shape, e_scr = exch.specs()
    ns, nd = len(e_in), len(e_out)

    def body(*refs):
        starts, waits = exch.copies(refs[:ns], refs[ns:ns + nd], *refs[ns + nd:])
        for cp in starts:
            cp.start()
        for cp in waits:
            cp.wait()

    return pl.pallas_call(body, name=name, in_specs=e_in, out_specs=e_out, out_shape=e_shape, scratch_shapes=e_scr,
                          compiler_params=pltpu.CompilerParams(has_side_effects=True))(*exch.srcs)


def gather_two_level(name, shards, layer, axis, size, full_shape):
    def body(src, dst, send_sem, recv_sem, loc_sem):
        x, y, c = lax.axis_index("x"), lax.axis_index("y"), lax.axis_index("c")
        mine = src.at[layer]
        chips = [(1 - x, y), (x, 1 - y), (1 - x, 1 - y)]

        def slab(px, py, pc):
            return _slab(dst, axis, pl.multiple_of((4 * px + 2 * py + pc) * size, 128), size)

        def copy(k, from_ref, block, to):
            return pltpu.make_async_remote_copy(from_ref, slab(*block), send_sem.at[k], recv_sem.at[k], device_id=to,
                                                device_id_type=MESH)

        own = pltpu.make_async_copy(mine, slab(x, y, c), loc_sem)
        own.start()
        first = [copy(0, mine, (x, y, c), (x, y, 1 - c))]
        first += [copy(1 + j, mine, (x, y, c), (*chip, c)) for j, chip in enumerate(chips)]
        for cp in first:
            cp.start()
        passed = []
        for j, chip in enumerate(chips):
            copy(1 + j, mine, (*chip, c), (x, y, c)).wait_recv()
            fwd = copy(4 + j, slab(*chip, c), (*chip, c), (x, y, 1 - c))
            fwd.start()
            passed.append(fwd)
        copy(0, mine, (x, y, 1 - c), (x, y, c)).wait_recv()
        for j, chip in enumerate(chips):
            copy(4 + j, mine, (*chip, 1 - c), (x, y, c)).wait_recv()
        for cp in first + passed:
            cp.wait_send()
        own.wait()

    return pl.pallas_call(
        body, name=name, in_specs=[ANY], out_specs=ANY, out_shape=jax.ShapeDtypeStruct(full_shape, shards.dtype),
        scratch_shapes=[pltpu.SemaphoreType.DMA((NDEV - 1,)), pltpu.SemaphoreType.DMA((NDEV - 1,)),
                        pltpu.SemaphoreType.DMA(())],
        compiler_params=pltpu.CompilerParams(has_side_effects=True))(shards)


def hosted_call(body, exch, name, grid, in_specs, out_specs, out_shape, scratch_shapes, operands, semantics,
                aliases=None):
    aliases = aliases or {}
    if exch is None:
        res = pl.pallas_call(body, name=name, grid=grid, in_specs=in_specs, out_specs=out_specs, out_shape=out_shape,
                             scratch_shapes=scratch_shapes, input_output_aliases=aliases,
                             compiler_params=_params(semantics))(*operands)
        return list(res), []
    n_in, n_out, n_scr = len(in_specs), len(out_specs), len(scratch_shapes)
    e_in, e_out, e_shape, e_scr = exch.specs()
    ns, nd = len(e_in), len(e_out)

    def at_step(last):
        cond = None
        for ax, n in enumerate(grid):
            c = pl.program_id(ax) == (n - 1 if last else 0)
            cond = c if cond is None else jnp.logical_and(cond, c)
        return cond

    def wrapped(*refs):
        ins, src = refs[:n_in], refs[n_in:n_in + ns]
        o0 = n_in + ns
        outs, dst = refs[o0:o0 + n_out], refs[o0 + n_out:o0 + n_out + nd]
        s0 = o0 + n_out + nd
        scr, sems = refs[s0:s0 + n_scr], refs[s0 + n_scr:]

        @pl.when(at_step(False))
        def _():
            for cp in exch.copies(src, dst, *sems)[0]:
                cp.start()

        body(*ins, *outs, *scr)

        @pl.when(at_step(True))
        def _():
            for cp in exch.copies(src, dst, *sems)[1]:
                cp.wait()

    res = pl.pallas_call(
        wrapped, name=name, grid=grid, in_specs=list(in_specs) + e_in, out_specs=list(out_specs) + e_out,
        out_shape=list(out_shape) + e_shape, scratch_shapes=list(scratch_shapes) + e_scr,
        input_output_aliases=aliases,
        compiler_params=pltpu.CompilerParams(dimension_semantics=("arbitrary",) * len(grid),
                                             vmem_limit_bytes=VMEM_LIMIT, has_side_effects=True),
    )(*operands, *exch.srcs)
    return list(res[:n_out]), list(res[n_out:])


def allreduce_small(name, part):
    rows, width = part.shape

    def body(p_ref, o_ref, buf, send_sem, recv_sem):
        x, y, c = lax.axis_index("x"), lax.axis_index("y"), lax.axis_index("c")
        me = 4 * x + 2 * y + c
        buf[me] = p_ref[...]
        waits = []
        for k in range(1, NDEV):
            px = 1 - x if k & 4 else x
            py = 1 - y if k & 2 else y
            pc = 1 - c if k & 1 else c
            peer = 4 * px + 2 * py + pc
            pltpu.make_async_remote_copy(p_ref, buf.at[me], send_sem.at[k - 1], recv_sem.at[k - 1],
                                         device_id=(px, py, pc), device_id_type=MESH).start()
            waits.append(pltpu.make_async_remote_copy(p_ref, buf.at[peer], send_sem.at[k - 1], recv_sem.at[k - 1],
                                                      device_id=(px, py, pc), device_id_type=MESH))
        for w in waits:
            w.wait()
        acc = buf[0]
        for j in range(1, NDEV):
            acc = acc + buf[j]
        o_ref[...] = acc

    vmem = pl.BlockSpec(memory_space=pltpu.VMEM)
    return pl.pallas_call(
        body, name=name, in_specs=[vmem], out_specs=vmem, out_shape=jax.ShapeDtypeStruct((rows, width), F32),
        scratch_shapes=[pltpu.VMEM((NDEV, rows, width), F32), pltpu.SemaphoreType.DMA((NDEV - 1,)),
                        pltpu.SemaphoreType.DMA((NDEV - 1,))],
        compiler_params=pltpu.CompilerParams(vmem_limit_bytes=VMEM_LIMIT, has_side_effects=True),
    )(part)


def rowwise(name, fn, m, ins=(), consts=(), outs=(), alias_outs=(), accs=(), tm=ROW_TILE):
    tm = min(tm, m)
    n_in, n_c, n_o, n_al, n_ac = len(ins), len(consts), len(outs), len(alias_outs), len(accs)
    held = [a for (a, _, _) in alias_outs if not isinstance(a, jax.ShapeDtypeStruct)]
    n_held = len(held)

    def body(*refs):
        in_refs = refs[:n_in + n_c]
        out_refs = refs[n_in + n_c + n_held:]
        vals = fn(*[r[...] for r in in_refs])
        if not isinstance(vals, (tuple, list)):
            vals = (vals,)
        for r, v in zip(out_refs[:n_o + n_al], vals[:n_o + n_al]):
            r[...] = v.astype(r.dtype)
        if n_ac:
            acc_refs = out_refs[n_o + n_al:]

            @pl.when(pl.program_id(0) == 0)
            def _():
                for r in acc_refs:
                    r[...] = jnp.zeros(r.shape, F32)

            for r, v in zip(acc_refs, vals[n_o + n_al:]):
                r[...] += v

    def col(cb):
        return lambda i: (i, cb)

    in_specs = [pl.BlockSpec((tm, w), col(cb)) for (_, w, cb) in ins]
    in_specs += [pl.BlockSpec(c.shape, lambda i, nd=c.ndim: (0,) * nd) for c in consts]
    in_specs += [ANY for _ in held]
    out_shape = [jax.ShapeDtypeStruct((m, w), dt) for (w, dt) in outs]
    out_specs = [pl.BlockSpec((tm, w), col(0)) for (w, _) in outs]
    out_shape += [jax.ShapeDtypeStruct(a.shape, a.dtype) for (a, _, _) in alias_outs]
    out_specs += [pl.BlockSpec((tm, w), col(cb)) for (_, w, cb) in alias_outs]
    out_shape += [jax.ShapeDtypeStruct(s, F32) for s in accs]
    out_specs += [pl.BlockSpec(s, lambda i: (0, 0)) for s in accs]
    aliases, k_in = {}, n_in + n_c
    for k, (a, _, _) in enumerate(alias_outs):
        if not isinstance(a, jax.ShapeDtypeStruct):
            aliases[k_in] = n_o + k
            k_in += 1
    return pl.pallas_call(
        body, name=name, grid=(m // tm,), in_specs=in_specs, out_specs=out_specs, out_shape=out_shape,
        input_output_aliases=aliases,
        compiler_params=_params(("arbitrary",) if n_ac else ("parallel",)),
    )(*[a for (a, _, _) in ins], *consts, *held)


def _operand(arr, bshape, imap):
    if isinstance(arr, tuple):
        arr, lead = arr
        return arr, pl.BlockSpec((None,) + bshape, lambda *g: (lead,) + imap(*g))
    return arr, pl.BlockSpec(bshape, imap)


def _shape2(arr):
    return arr[0].shape[1:] if isinstance(arr, tuple) else arr.shape


def mm(name, a, b, mode, out_dtype=F32, tm=1024, tn=1024, tk=1024, exch=None):
    sa, sb = _shape2(a), _shape2(b)
    if mode == "nn":
        (M, K), N = sa, sb[1]
    elif mode == "nt":
        (M, K), N = sa, sb[0]
    else:
        (K, M), N = sa, sb[1]
    tm, tn, tk = min(tm, M), min(tn, N), min(tk, K)
    assert M % tm == 0 and N % tn == 0 and K % tk == 0, (name, M, N, K)
    nk = K // tk
    if mode == "nn":
        a_arr, a_spec = _operand(a, (tm, tk), lambda i, j, k: (i, k))
        b_arr, b_spec = _operand(b, (tk, tn), lambda i, j, k: (k, j))
        dot = _dot
    elif mode == "nt":
        a_arr, a_spec = _operand(a, (tm, tk), lambda i, j, k: (i, k))
        b_arr, b_spec = _operand(b, (tn, tk), lambda i, j, k: (j, k))
        dot = _dot_nt
    else:
        a_arr, a_spec = _operand(a, (tk, tm), lambda i, j, k: (k, i))
        b_arr, b_spec = _operand(b, (tk, tn), lambda i, j, k: (k, j))
        dot = _dot_tn

    def body(a_ref, b_ref, o_ref, *acc):
        part = dot(a_ref[...].astype(BF16), b_ref[...].astype(BF16))
        if nk == 1:
            o_ref[...] = part.astype(o_ref.dtype)
            return
        acc_ref, k = acc[0], pl.program_id(2)

        @pl.when(k == 0)
        def _():
            acc_ref[...] = part

        @pl.when(k > 0)
        def _():
            acc_ref[...] += part

        @pl.when(k == nk - 1)
        def _():
            o_ref[...] = acc_ref[...].astype(o_ref.dtype)

    outs, extra = hosted_call(
        body, exch, name, (M // tm, N // tn, nk), [a_spec, b_spec], [pl.BlockSpec((tm, tn), lambda i, j, k: (i, j))],
        [jax.ShapeDtypeStruct((M, N), out_dtype)], [pltpu.VMEM((tm, tn), F32)] if nk > 1 else [], [a_arr, b_arr],
        ("parallel", "parallel", "arbitrary"))
    return outs[0] if exch is None else (outs[0], extra)


def mm_fused(name, m, a_ins, bs, mode, kdim, prologue=None, a_outs=(), e_ins=(), consts=(), epilogue=None, outs=(),
             alias_outs=(), accs=(), a_to_epilogue=(), tm=512, tk=1024, exch=None):
    tm = min(tm, m)
    nk = kdim // tk
    assert nk == 1 or not a_to_epilogue
    n = bs[0][0].shape[1 if mode == "nn" else 0]
    n_a, n_b, n_e, n_c = len(a_ins), len(bs), len(e_ins), len(consts)
    n_ao, n_o, n_al, n_ac = len(a_outs), len(outs), len(alias_outs), len(accs)
    held = [a for (a, _, _) in alias_outs if not isinstance(a, jax.ShapeDtypeStruct)]
    dot = _dot if mode == "nn" else _dot_nt

    def body(*refs):
        a_refs, b_refs = refs[:n_a], refs[n_a:n_a + n_b]
        e_refs = refs[n_a + n_b:n_a + n_b + n_e + n_c]
        o0 = n_a + n_b + n_e + n_c + len(held)
        ao_refs = refs[o0:o0 + n_ao]
        out_refs = refs[o0 + n_ao:o0 + n_ao + n_o + n_al]
        acc_refs = refs[o0 + n_ao + n_o + n_al:o0 + n_ao + n_o + n_al + n_ac]
        scr = refs[o0 + n_ao + n_o + n_al + n_ac:]
        i, k = pl.program_id(0), pl.program_id(1)
        ck = tk if prologue is None else min(tk, PROLOGUE_CHUNK)
        part = None
        for c0 in range(0, tk, ck):
            cols = slice(c0, c0 + ck)
            tiles = [r[:, cols] for r in a_refs]
            a_list, extra = (tiles, []) if prologue is None else prologue(*tiles)
            for r, v in zip(ao_refs, extra):
                r[:, cols] = v.astype(r.dtype)
            for a, b_ref in zip(a_list, b_refs):
                b = b_ref[cols, :] if mode == "nn" else b_ref[:, cols]
                prod = dot(a.astype(BF16), b.astype(BF16))
                part = prod if part is None else part + prod

        def finish(total):
            vals = epilogue(total, *[a_refs[j][...] for j in a_to_epilogue], *[r[...] for r in e_refs])
            if not isinstance(vals, (tuple, list)):
                vals = (vals,)
            for r, v in zip(out_refs, vals[:n_o + n_al]):
                r[...] = v.astype(r.dtype)
            for r, v in zip(acc_refs, vals[n_o + n_al:]):
                @pl.when(i == 0)
                def _():
                    r[...] = v

                @pl.when(i > 0)
                def _():
                    r[...] += v

        if nk == 1:
            finish(part)
            return
        acc_ref = scr[0]

        @pl.when(k == 0)
        def _():
            acc_ref[...] = part

        @pl.when(k > 0)
        def _():
            acc_ref[...] += part

        @pl.when(k == nk - 1)
        def _():
            finish(acc_ref[...])

    in_specs = [pl.BlockSpec((tm, tk), lambda i, k, off=off: (i, k + off)) for (_, off) in a_ins]
    if mode == "nn":
        in_specs += [pl.BlockSpec((tk, n), lambda i, k, off=off: (k + off, 0)) for (_, off) in bs]
    else:
        in_specs += [pl.BlockSpec((n, tk), lambda i, k, off=off: (0, k + off)) for (_, off) in bs]
    in_specs += [pl.BlockSpec((tm, w), lambda i, k, cb=cb: (i, cb)) for (_, w, cb) in e_ins]
    in_specs += [pl.BlockSpec(c.shape, lambda i, k, nd=c.ndim: (0,) * nd) for c in consts]
    in_specs += [ANY for _ in held]
    out_shape = [jax.ShapeDtypeStruct((m, kdim), dt) for dt in a_outs]
    out_specs = [pl.BlockSpec((tm, tk), lambda i, k: (i, k)) for _ in a_outs]
    out_shape += [jax.ShapeDtypeStruct((m, w), dt) for (w, dt) in outs]
    out_specs += [pl.BlockSpec((tm, w), lambda i, k: (i, 0)) for (w, _) in outs]
    out_shape += [jax.ShapeDtypeStruct(a.shape, a.dtype) for (a, _, _) in alias_outs]
    out_specs += [pl.BlockSpec((tm, w), lambda i, k, cb=cb: (i, cb)) for (_, w, cb) in alias_outs]
    out_shape += [jax.ShapeDtypeStruct(s_, F32) for s_ in accs]
    out_specs += [pl.BlockSpec(s_, lambda i, k: (0, 0)) for s_ in accs]
    aliases, k_in = {}, n_a + n_b + n_e + n_c
    for j, (a, _, _) in enumerate(alias_outs):
        if not isinstance(a, jax.ShapeDtypeStruct):
            aliases[k_in] = n_ao + n_o + j
            k_in += 1
    operands = [a for (a, _) in a_ins] + [b for (b, _) in bs] + [a for (a, _, _) in e_ins] + list(consts) + held
    res, extra = hosted_call(
        body, exch, name, (m // tm, nk), in_specs, out_specs, out_shape,
        [pltpu.VMEM((tm, n), F32)] if nk > 1 else [], operands,
        ("arbitrary" if n_ac else "parallel", "arbitrary"), aliases)
    return res if exch is None else (res, extra)


def _cumsum_rows(x):
    n = x.shape[0]
    row = lax.broadcasted_iota(jnp.int32, x.shape, 0)
    s = 1
    while s < n:
        x = x + jnp.where(row >= s, pltpu.roll(x, s, 0), 0.0)
        s *= 2
    return x


def _hg_prep(zq, zf, lb, reverse):
    n = zq.shape[0]
    q = _silu(zq)
    sig = _sigmoid(zf)
    sn = 1.0 - sig
    f = lb + (1.0 - lb) * sig
    k = (1.0 - lb) * sn
    g = jnp.log(jnp.maximum(f, TINY))
    b = _cumsum_rows(g)
    if reverse:
        b = b[n - 1:n] - b + g
    b_last = b[0:1] if reverse else b[n - 1:n]
    b_ref = b[n // 2:n // 2 + 1]
    e1 = jnp.exp(b)
    e2 = jnp.exp(jnp.clip(b - b_ref, -EXP_CLAMP, EXP_CLAMP))
    e3 = jnp.exp(jnp.clip(b_ref - b, -EXP_CLAMP, EXP_CLAMP))
    e4 = jnp.exp(b_last - b)
    return dict(q=q, k=k, sig=sig, sn=sn, f=f, e1=e1, e2=e2, e3=e3, e4=e4, e_last=jnp.exp(b_last),
                qe=(q * e1).astype(BF16), qt=(q * e2).astype(BF16), kt=(k * e3).astype(BF16),
                ks=(k * e4).astype(BF16))


def _hg_mask(n, reverse):
    t = lax.broadcasted_iota(jnp.int32, (n, n), 0)
    s = lax.broadcasted_iota(jnp.int32, (n, n), 1)
    return (s >= t) if reverse else (s <= t)


def hgrn_fwd(name, z, lb_f, lb_b, exch=None, unroll=False):
    m = z.shape[0]
    C, T = HG_CHUNK, min(HG_BLOCK_FWD, m)
    nb, cpb = m // T, T // C

    def body(zq_f, zf_f, zi_f, zq_b, zf_b, zi_b, lbf_ref, lbb_ref, of_ref, ob_ref, sf_ref, sb_ref, st_ref):
        @pl.when(pl.program_id(0) == 0)
        def _():
            st_ref[...] = jnp.zeros(st_ref.shape, F32)

        dirs = ((zq_f, zf_f, zi_f, lbf_ref, of_ref, sf_ref), (zq_b, zf_b, zi_b, lbb_ref, ob_ref, sb_ref))

        def chunk(ci, carry):
            work = []
            for d, (zq, zf, zi, lb_ref, o_ref, s_ref) in enumerate(dirs):
                cc = ci if d == 0 else cpb - 1 - ci
                rows = pl.ds(pl.multiple_of(cc * C, C), C)
                pre = _hg_prep(zq[rows, :], zf[rows, :], lb_ref[...], d == 1)
                v = zi[rows, :].astype(BF16)
                work.append((cc, rows, pre, v, [st_ref[d, h] for h in range(HEADS)]))
            heads = [(d, h, slice(h * HEAD_DIM, (h + 1) * HEAD_DIM)) for d in range(2) for h in range(HEADS)]
            first = {}
            for d, h, sl in heads:
                _, _, pre, v, sts = work[d]
                first[d, h] = (_dot_nt(pre["qt"][:, sl], pre["kt"][:, sl]),
                               _dot_nt(pre["qe"][:, sl], sts[h].astype(BF16)),
                               _dot_tn(v[:, sl], pre["ks"][:, sl]))
            results = [([], []), ([], [])]
            for d, h, sl in heads:
                _, _, pre, v, sts = work[d]
                scores, o_inter, st_add = first[d, h]
                a = jnp.where(_hg_mask(C, d == 1), scores, 0.0).astype(BF16)
                results[d][0].append(o_inter + _dot(a, v[:, sl]))
                results[d][1].append(sts[h] * pre["e_last"][:, sl] + st_add)
            results = [(jnp.concatenate(o_parts, axis=1), new_sts) for (o_parts, new_sts) in results]
            for d, (zq, zf, zi, lb_ref, o_ref, s_ref) in enumerate(dirs):
                cc, rows, _, _, sts = work[d]
                o_ref[rows, :] = results[d][0]
                for h in range(HEADS):
                    s_ref[cc, h] = sts[h]
                    st_ref[d, h] = results[d][1][h]
            return carry

        lax.fori_loop(0, cpb, chunk, 0, unroll=unroll)

    def zspec(cb, rev):
        return pl.BlockSpec((T, D), (lambda i: (nb - 1 - i, cb)) if rev else (lambda i: (i, cb)))

    def sspec(rev):
        shape = (cpb, HEADS, HEAD_DIM, HEAD_DIM)
        return pl.BlockSpec(shape, (lambda i: (nb - 1 - i, 0, 0, 0)) if rev else (lambda i: (i, 0, 0, 0)))

    lbspec = pl.BlockSpec((1, D), lambda i: (0, 0))
    states = jax.ShapeDtypeStruct((m // C, HEADS, HEAD_DIM, HEAD_DIM), F32)
    outs, extra = hosted_call(
        body, exch, name, (nb,),
        [zspec(ZQ, False), zspec(ZFF, False), zspec(ZI, False), zspec(ZQ, True), zspec(ZFB, True), zspec(ZI, True),
         lbspec, lbspec],
        [zspec(0, False), zspec(0, True), sspec(False), sspec(True)],
        [jax.ShapeDtypeStruct((m, D), F32), jax.ShapeDtypeStruct((m, D), F32), states, states],
        [pltpu.VMEM((2, HEADS, HEAD_DIM, HEAD_DIM), F32)], [z, z, z, z, z, z, lb_f, lb_b], ("arbitrary",))
    return outs, extra


def hgrn_bwd(name, z, d_o, s_f, s_b, lb_f, lb_b, exch=None, unroll=False):
    m = z.shape[0]
    C, T = HG_CHUNK, min(HG_BLOCK_BWD, m)
    nb, cpb = m // T, T // C

    def body(zq_f, zf_f, zi_f, do_f, sf_ref, zq_b, zf_b, zi_b, do_b, sb_ref, lbf_ref, lbb_ref,
             dqf_ref, dvf_ref, dqb_ref, dvb_ref, dzf_f, dzf_b, dlbf_ref, dlbb_ref,
             dst_ref):
        @pl.when(pl.program_id(0) == 0)
        def _():
            dst_ref[...] = jnp.zeros(dst_ref.shape, F32)
            dlbf_ref[...] = jnp.zeros(dlbf_ref.shape, F32)
            dlbb_ref[...] = jnp.zeros(dlbb_ref.shape, F32)

        dirs = ((zq_f, zf_f, zi_f, do_f, sf_ref, lbf_ref, dqf_ref, dvf_ref, dzf_f, dlbf_ref),
                (zq_b, zf_b, zi_b, do_b, sb_ref, lbb_ref, dqb_ref, dvb_ref, dzf_b, dlbb_ref))

        def chunk(ci, carry):
            work = []
            for d, (zq, zf, zi, do_ref, s_ref, lb_ref, dq_ref, dv_ref, dzf_ref, dlb_ref) in enumerate(dirs):
                cc = cpb - 1 - ci if d == 0 else ci
                rows = pl.ds(pl.multiple_of(cc * C, C), C)
                lb = lb_ref[...]
                pre = _hg_prep(zq[rows, :], zf[rows, :], lb, d == 1)
                work.append((rows, lb, pre, zi[rows, :].astype(BF16), do_ref[rows, :],
                             [s_ref[cc, h] for h in range(HEADS)], [dst_ref[d, h] for h in range(HEADS)],
                             dlb_ref[...]))
            heads = [(d, h, slice(h * HEAD_DIM, (h + 1) * HEAD_DIM)) for d in range(2) for h in range(HEADS)]
            first = {}
            for d, h, sl in heads:
                _, _, pre, v, do, st_prevs, dsts, _ = work[d]
                dst16 = dsts[h].astype(BF16)
                first[d, h] = (_dot_nt(pre["qt"][:, sl], pre["kt"][:, sl]),
                               _dot_nt(do[:, sl], v[:, sl]),
                               _dot(do[:, sl], st_prevs[h].astype(BF16)),
                               _dot(v[:, sl], dst16),
                               _dot_nt(pre["ks"][:, sl], dst16),
                               _dot_tn(do[:, sl], pre["qe"][:, sl]))
            parts = [[[] for _ in range(6)] for _ in range(2)]
            for d, h, sl in heads:
                _, _, pre, v, do, st_prevs, dsts, _ = work[d]
                scores, dscores, dq_inter, dk_state, dv_state, dst_add = first[d, h]
                mask = _hg_mask(C, d == 1)
                a = jnp.where(mask, scores, 0.0).astype(BF16)
                da = jnp.where(mask, dscores, 0.0).astype(BF16)
                dq_p, dki_p, dks_p, dv_p, rr_p, new_dsts = parts[d]
                dq_p.append(_dot(da, pre["kt"][:, sl]) * pre["e2"][:, sl] + dq_inter * pre["e1"][:, sl])
                dki_p.append(_dot_tn(da, pre["qt"][:, sl]) * pre["e3"][:, sl])
                dks_p.append(dk_state * pre["e4"][:, sl])
                dv_p.append(_dot_tn(a, do[:, sl]) + dv_state)
                rr_p.append(pre["e_last"][:, sl] * _colsum(dsts[h] * st_prevs[h]))
                new_dsts.append(dsts[h] * pre["e_last"][:, sl] + dst_add)
            results = []
            for d, (rows, lb, pre, v, do, st_prevs, dsts, dlb_old) in enumerate(work):
                rev = d == 1
                dq_p, dki_p, dks_p, dv_p, rr_p, new_dsts = parts[d]
                dq, dki, dks, dv, rr = (jnp.concatenate(p_, axis=1) for p_ in (dq_p, dki_p, dks_p, dv_p, rr_p))
                x = pre["q"] * dq - pre["k"] * dki
                y = pre["k"] * dks
                if rev:
                    dg = _cumsum_rows(x - y) + _colsum(y) + rr
                else:
                    dg = _cumsum_rows(y - x) + (x - y) + _colsum(x) + rr
                inv_f = jnp.where(pre["f"] > TINY, 1.0 / pre["f"], 0.0)
                u = dg * inv_f - (dki + dks)
                results.append((dq, dv, (1.0 - lb) * pre["sig"] * pre["sn"] * u, dlb_old + _colsum(pre["sn"] * u),
                                new_dsts))
            for d, (zq, zf, zi, do_ref, s_ref, lb_ref, dq_ref, dv_ref, dzf_ref, dlb_ref) in enumerate(dirs):
                rows = work[d][0]
                dq, dv, dzf, dlb, new_dsts = results[d]
                dq_ref[rows, :] = dq.astype(dq_ref.dtype)
                dv_ref[rows, :] = dv.astype(dv_ref.dtype)
                dzf_ref[rows, :] = dzf.astype(dzf_ref.dtype)
                dlb_ref[...] = dlb
                for h in range(HEADS):
                    dst_ref[d, h] = new_dsts[h]
            return carry

        lax.fori_loop(0, cpb, chunk, 0, unroll=unroll)

    def rspec(cb, rev):
        return pl.BlockSpec((T, D), (lambda i: (i, cb)) if rev else (lambda i: (nb - 1 - i, cb)))

    def sspec(rev):
        shape = (cpb, HEADS, HEAD_DIM, HEAD_DIM)
        return pl.BlockSpec(shape, (lambda i: (i, 0, 0, 0)) if rev else (lambda i: (nb - 1 - i, 0, 0, 0)))

    lbspec = pl.BlockSpec((1, D), lambda i: (0, 0))
    half = jax.ShapeDtypeStruct((m, D), BF16)
    row = jax.ShapeDtypeStruct((1, D), F32)
    outs, extra = hosted_call(
        body, exch, name, (nb,),
        [rspec(ZQ, False), rspec(ZFF, False), rspec(ZI, False), rspec(0, False), sspec(False),
         rspec(ZQ, True), rspec(ZFB, True), rspec(ZI, True), rspec(0, True), sspec(True), lbspec, lbspec],
        [rspec(0, False), rspec(0, False), rspec(0, True), rspec(0, True), rspec(0, False), rspec(0, True),
         lbspec, lbspec],
        [half, half, half, half, half, half, row, row],
        [pltpu.VMEM((2, HEADS, HEAD_DIM, HEAD_DIM), F32)],
        [z, z, z, d_o, s_f, z, z, z, d_o, s_b, lb_f, lb_b], ("arbitrary",))
    return outs, extra


def _heads(fn, *arrs):
    res = [fn(*[a[:, h * HEAD_DIM:(h + 1) * HEAD_DIM] for a in arrs]) for h in range(HEADS)]
    return [jnp.concatenate(parts, axis=1) for parts in zip(*res)]


def _hg_post(o_f, o_b, zg, g):
    def head(of, ob, zgh, gh):
        on, _ = _rms(of + ob)
        return (on * gh * _silu(zgh),)
    return _heads(head, o_f, o_b, zg, g)[0]


def _hg_post_bwd(da, o_f, o_b, zg, g):
    def head(dah, of, ob, zgh, gh):
        on, r = _rms(of + ob)
        sg = _silu(zgh)
        d_on = dah * sg
        return _rms_bwd(d_on, on, r, gh), dah * on * gh * _silu_grad(zgh), d_on * on
    d_o, dzg, dg = _heads(head, da, o_f, o_b, zg, g)
    return d_o, dzg, _colsum(dg)


def _sg_parts(zv, ln_g, ln_b):
    vg = _gelu(zv)
    xc = vg - _mean(vg)
    rstd = lax.rsqrt(_mean(xc * xc) + EPS)
    vh = xc * rstd
    return vh, rstd, vh * ln_g + ln_b


def _sg_lane_group(shape):
    return lax.broadcasted_iota(jnp.int32, shape, 1) < SG_GROUP_DIM


def _sg_mix(w, v16, transpose):
    rows = v16.shape[0]
    out = []
    for c in range(rows // SG_CHUNK):
        parts = []
        for j in range(SG_WIDTH // 128):
            vj = v16[c * SG_CHUNK:(c + 1) * SG_CHUNK, j * 128:(j + 1) * 128]
            w0 = w[(2 * j) * SG_CHUNK:(2 * j + 1) * SG_CHUNK]
            w1 = w[(2 * j + 1) * SG_CHUNK:(2 * j + 2) * SG_CHUNK]
            dot = _dot_tn if transpose else _dot
            parts.append(jnp.where(_sg_lane_group((SG_CHUNK, 128)), dot(w0, vj), dot(w1, vj)))
        out.append(jnp.concatenate(parts, axis=1))
    return jnp.concatenate(out, axis=0)


def _sg_fwd(zu, zv, w, bias, ln_g, ln_b):
    _, _, v = _sg_parts(zv, ln_g, ln_b)
    reps = zu.shape[0] // SG_CHUNK
    return _gelu(zu) * (_sg_mix(w, v.astype(BF16), False) + jnp.concatenate([bias] * reps, axis=0))


def _sg_bwd(db, zu, zv, w, bias, ln_g, ln_b):
    vh, rstd, v = _sg_parts(zv, ln_g, ln_b)
    v16 = v.astype(BF16)
    reps = zu.shape[0] // SG_CHUNK
    sg = _sg_mix(w, v16, False) + jnp.concatenate([bias] * reps, axis=0)
    dzu = db * sg * _gelu_grad(zu)
    dsg = db * _gelu(zu)
    dsg16 = dsg.astype(BF16)
    dv = _sg_mix(w, dsg16, True)
    low = _sg_lane_group((SG_CHUNK, 128))
    dw = []
    for g in range(SG_WIDTH // SG_GROUP_DIM):
        j, keep = g // 2, (low if g % 2 == 0 else jnp.logical_not(low))
        acc = jnp.zeros((SG_CHUNK, SG_CHUNK), F32)
        for c in range(reps):
            rows = slice(c * SG_CHUNK, (c + 1) * SG_CHUNK)
            dj = jnp.where(keep, dsg16[rows, j * 128:(j + 1) * 128], jnp.zeros((), BF16))
            acc = acc + _dot_nt(dj, v16[rows, j * 128:(j + 1) * 128])
        dw.append(acc)
    dbias = sum(dsg[c * SG_CHUNK:(c + 1) * SG_CHUNK] for c in range(reps))
    dvh = dv * ln_g
    dvg = rstd * (dvh - _mean(dvh) - vh * _mean(dvh * vh))
    dzuv = jnp.concatenate([dzu, dvg * _gelu_grad(zv)], axis=1)
    return (dzuv, jnp.concatenate(dw, axis=0), dbias, _colsum(dv * vh), _colsum(dv))


def lower_bounds(name, gamma_f, gamma_b):
    def body(gf_ref, gb_ref, lf_ref, lb_ref):
        for g_ref, o_ref in ((gf_ref, lf_ref), (gb_ref, lb_ref)):
            g0, g1 = g_ref[0:1, :], g_ref[1:2, :]
            mx = jnp.maximum(g0, g1)
            e0, e1 = jnp.exp(g0 - mx), jnp.exp(g1 - mx)
            sm0, sm1 = e0 / (e0 + e1), e1 / (e0 + e1)
            o_ref[0:1, :] = sm0 - sm0
            o_ref[1:2, :] = (sm0 + sm1) - sm0
    shp = jax.ShapeDtypeStruct(gamma_f.shape, F32)
    return pl.pallas_call(body, name=name, out_shape=[shp, shp])(gamma_f, gamma_b)


def lower_bounds_bwd(name, gamma_f, gamma_b, dlb_f, dlb_b):
    def body(gf_ref, gb_ref, df_ref, db_ref, of_ref, ob_ref):
        for g_ref, d_ref, o_ref in ((gf_ref, df_ref, of_ref), (gb_ref, db_ref, ob_ref)):
            g0, g1 = g_ref[0:1, :], g_ref[1:2, :]
            mx = jnp.maximum(g0, g1)
            e0, e1 = jnp.exp(g0 - mx), jnp.exp(g1 - mx)
            sm0, sm1 = e0 / (e0 + e1), e1 / (e0 + e1)
            d1 = d_ref[1:2, :] * sm0 * sm1
            o_ref[0:1, :] = -d1
            o_ref[1:2, :] = d1
    shp = jax.ShapeDtypeStruct(gamma_f.shape, F32)
    return pl.pallas_call(body, name=name, out_shape=[shp, shp])(gamma_f, gamma_b, dlb_f, dlb_b)


def _row(a, l):
    return a[l:l + 1]


class LocalPlan:
    def __init__(self, weights):
        self.W = weights
        self.grads = [dict() for _ in range(DEPTH)]

    def exch(self, host):
        return None

    def done(self, host, outs):
        pass

    def early_small(self, packed):
        pass


def local_step(x, p, target, S, plan):
    m = x.shape[0]

    def hmm(tag, *args, **kw):
        ex = plan.exch(tag)
        res = mm(tag, *args, exch=ex, **kw)
        if ex is None:
            return res
        plan.done(tag, res[1])
        return res[0]

    lb_f, lb_b = lower_bounds("lower_bounds", S["lb_gamma_fwd"], S["lb_gamma_bwd"])
    saved = []
    for l in range(DEPTH):
        t = f"l{l}_"
        W = plan.W[l]
        tm = 2048
        in_tile = (1024, 2048)
        ffn_tile = (2048, 2048)
        g_pre, g_post = _row(S["norm_mix_pre"], l), _row(S["norm_mix_post"], l)
        g_fpre, g_fpost = _row(S["norm_ffn_pre"], l), _row(S["norm_ffn_post"], l)
        hg_g = _row(S["hg_norm"], l)
        sg_w = S["sg_w"][l].reshape(SG_WIDTH // SG_GROUP_DIM * SG_CHUNK, SG_CHUNK).astype(BF16)
        sg_bias = jnp.repeat(S["sg_b"][l].T, SG_GROUP_DIM, axis=1)
        ln_g, ln_b = _row(S["sg_ln_g"], l), _row(S["sg_ln_b"], l)
        lbf, lbb = _row(lb_f, l), _row(lb_b, l)

        if l == 0:
            (h,) = rowwise(t + "pre_norm", lambda xv, g: (_rms(xv)[0] * g,), m, ins=[(x, D, 0)], consts=[g_pre],
                           outs=[(D, BF16)])
        z = hmm(t + "in_proj", h, W["w_in"], "nn", tm=in_tile[0], tn=in_tile[1])
        (o_f, o_b, s_f, s_b), extra = hgrn_fwd(t + "hgrn_fwd", z, lbf, lbb, exch=plan.exch(t + "hgrn_fwd"))
        plan.done(t + "hgrn_fwd", extra)
        (a_out,) = rowwise(t + "hgrn_post", _hg_post, m, ins=[(o_f, D, 0), (o_b, D, 0), (z, D, ZG)], consts=[hg_g],
                           outs=[(D, BF16)])
        (b_out,) = rowwise(t + "sgu_fwd", _sg_fwd, m, ins=[(z, SG_WIDTH, ZU), (z, SG_WIDTH, ZV)],
                           consts=[sg_w, sg_bias, ln_g, ln_b], outs=[(SG_WIDTH, BF16)])
        pa = mm(t + "proj_a", a_out, W["w_a"], "nn", BF16)
        pb = mm(t + "proj_b", b_out, W["w_b"], "nn", BF16)

        def merge_pro(a, b, ga, gb):
            mg = (_sigmoid(ga) * a + _sigmoid(gb) * b).astype(BF16)
            return [mg], [mg]

        def post_pre(mixv, xv, gp, gf):
            x1 = xv + _rms(mixv)[0] * gp
            return mixv, x1, _rms(x1)[0] * gf
        merged, mix, x1, h2 = mm_fused(
            t + "out_proj", m, [(pa, 0), (pb, 0), (z, GA), (z, GB)], [(W["w_out"], 0)], "nn", D, prologue=merge_pro,
            a_outs=[BF16], e_ins=[(x, D, 0)], consts=[g_post, g_fpre], epilogue=post_pre,
            outs=[(D, F32), (D, F32), (D, BF16)])
        gu = hmm(t + "ffn_in", h2, W["w_gu"], "nn", BF16, tm=ffn_tile[0], tn=ffn_tile[1])

        def act_pro(gt, up):
            hd = (_silu(gt.astype(F32)) * up).astype(BF16)
            return [hd], [hd]
        hid, ff, x2 = mm_fused(
            t + "ffn_out", m, [(gu, 0), (gu, FFN_PAD // 1024)], [(W["w_down"], 0)], "nn", FFN_PAD, prologue=act_pro,
            a_outs=[BF16], e_ins=[(x1, D, 0)], consts=[g_fpost],
            epilogue=lambda f, xv, g: (f, xv + _rms(f)[0] * g), outs=[(D, F32), (D, F32)])
        e = mm(t + "ple_proj", (p, l), W["w_ple"], "nn")

        if l + 1 < DEPTH:
            def ple_add(tv, xv, ev, g):
                x3 = xv + ev * _sigmoid(tv)
                return tv, x3, _rms(x3)[0] * g
            tg, x3, h_next = mm_fused(
                t + "ple_gate", m, [(x2, 0)], [(W["w_ple_gate"], 0)], "nn", D, a_to_epilogue=(0,), e_ins=[(e, D, 0)],
                consts=[_row(S["norm_mix_pre"], l + 1)], epilogue=ple_add, outs=[(D, F32), (D, F32), (D, BF16)])
        else:
            tg, x3 = mm_fused(
                t + "ple_gate", m, [(x2, 0)], [(W["w_ple_gate"], 0)], "nn", D, a_to_epilogue=(0,), e_ins=[(e, D, 0)],
                epilogue=lambda tv, xv, ev: (tv, xv + ev * _sigmoid(tv)), outs=[(D, F32), (D, F32)])
            h_next = None
        saved.append(dict(x=x, h=h, z=z, o_f=o_f, o_b=o_b, s_f=s_f, s_b=s_b, a_out=a_out, b_out=b_out, pa=pa, pb=pb,
                          merged=merged, mix=mix, x1=x1, h2=h2, gu=gu, hid=hid, ff=ff, x2=x2, e=e, tg=tg,
                          sg_w=sg_w, sg_bias=sg_bias))
        x, h = x3, h_next

    def loss_fn(y, tv):
        err = y - tv
        return err * (1.0 / D), _colsum(err * err)
    dx, loss_cols = rowwise("loss", loss_fn, m, ins=[(x, D, 0), (target, D, 0)], outs=[(D, F32)], accs=[(1, D)])

    gs = {n: [None] * DEPTH for n in SMALL}
    dlb_f, dlb_b = [None] * DEPTH, [None] * DEPTH

    for l in reversed(range(DEPTH)):
        t = f"l{l}_bwd_"
        sv, W = saved[l], plan.W[l]
        tm, tk = 2048, 4096
        g_pre, g_post = _row(S["norm_mix_pre"], l), _row(S["norm_mix_post"], l)
        g_fpre, g_fpost = _row(S["norm_ffn_pre"], l), _row(S["norm_ffn_post"], l)
        hg_g = _row(S["hg_norm"], l)
        ln_g, ln_b = _row(S["sg_ln_g"], l), _row(S["sg_ln_b"], l)
        lbf, lbb = _row(lb_f, l), _row(lb_b, l)

        def wgrad(nm, tag, a, b):
            a_dtype = (a[0] if isinstance(a, tuple) else a).dtype
            plan.grads[l][nm] = mm(tag, a, b, "tn", BF16, tk=tk if a_dtype == BF16 else 2048)

        def ple_pro(d3, ev, tv):
            s = _sigmoid(tv)
            de_, dt_ = (d3 * s).astype(BF16), (d3 * ev * s * (1.0 - s)).astype(BF16)
            return [dt_], [dt_, de_]

        def ffn_post_bwd(d2p, d3, f, g):
            d2 = d3 + d2p
            fh, r = _rms(f)
            return d2, _rms_bwd(d2, fh, r, g), _colsum(d2 * fh)
        dt, de, dx2, dff, gs["norm_ffn_post"][l] = mm_fused(
            t + "ple_gate_dx", m, [(dx, 0), (sv["e"], 0), (sv["tg"], 0)], [(W["w_ple_gate"], 0)], "nt", D,
            prologue=ple_pro, a_outs=[BF16, BF16], a_to_epilogue=(0,), e_ins=[(sv["ff"], D, 0)], consts=[g_fpost],
            epilogue=ffn_post_bwd, outs=[(D, F32), (D, BF16)], accs=[(1, D)])
        wgrad("w_ple", t + "w_ple", (p, l), de)
        wgrad("w_ple_gate", t + "w_ple_gate", sv["x2"], dt)
        wgrad("w_down", t + "w_down", sv["hid"], dff)
        dhid = mm(t + "ffn_out_dx", dff, W["w_down"], "nt", BF16, tm=tm)

        def act_bwd(dh, gt, up):
            dh, gt = dh.astype(F32), gt.astype(F32)
            dg_, du_ = (dh * up * _silu_grad(gt)).astype(BF16), (dh * _silu(gt)).astype(BF16)
            return [dg_, du_], [dg_, du_]

        def pre_post_bwd(dh, d2, x1v, mixv, gf, gp):
            xh, r1 = _rms(x1v)
            d1 = d2 + _rms_bwd(dh, xh, r1, gf)
            mh, rm = _rms(mixv)
            return d1, _rms_bwd(d1, mh, rm, gp), _colsum(dh * xh), _colsum(d1 * mh)
        off = FFN_PAD // 1024
        dgate, dup, dx1, dmix, gs["norm_ffn_pre"][l], gs["norm_mix_post"][l] = mm_fused(
            t + "ffn_in_dx", m, [(dhid, 0), (sv["gu"], 0), (sv["gu"], off)], [(W["w_gu"], 0), (W["w_gu"], off)], "nt",
            FFN_PAD, prologue=act_bwd, a_outs=[BF16, BF16], e_ins=[(dx2, D, 0), (sv["x1"], D, 0), (sv["mix"], D, 0)],
            consts=[g_fpre, g_post], epilogue=pre_post_bwd, outs=[(D, F32), (D, BF16)], accs=[(1, D), (1, D)])
        wgrad("w_gate", t + "w_gate", sv["h2"], dgate)
        wgrad("w_up", t + "w_up", sv["h2"], dup)
        wgrad("w_out", t + "w_out", sv["merged"], dmix)

        def merge_bwd(dm, a, b, gab):
            sa, sb = _sigmoid(gab[:, :D]), _sigmoid(gab[:, D:])
            dgab = jnp.concatenate([dm * a * sa * (1.0 - sa), dm * b * sb * (1.0 - sb)], axis=1)
            return dm * sa, dm * sb, dgab
        dpa, dpb, dz = mm_fused(
            t + "out_proj_dx", m, [(dmix, 0)], [(W["w_out"], 0)], "nt", D,
            e_ins=[(sv["pa"], D, 0), (sv["pb"], D, 0), (sv["z"], 2 * D, 3)], epilogue=merge_bwd,
            outs=[(D, BF16), (D, BF16)], alias_outs=[(jax.ShapeDtypeStruct((m, N_IN), BF16), 2 * D, 3)])
        wgrad("w_a", t + "w_a", sv["a_out"], dpa)
        wgrad("w_b", t + "w_b", sv["b_out"], dpb)
        db = mm(t + "proj_b_dx", dpb, W["w_b"], "nt")

        dz, dsw, dbias, gs["sg_ln_g"][l], gs["sg_ln_b"][l] = rowwise(
            t + "sgu", _sg_bwd, m, ins=[(db, SG_WIDTH, 0), (sv["z"], SG_WIDTH, ZU), (sv["z"], SG_WIDTH, ZV)],
            consts=[sv["sg_w"], sv["sg_bias"], ln_g, ln_b], alias_outs=[(dz, 2 * SG_WIDTH, 5)],
            accs=[(SG_WIDTH // SG_GROUP_DIM * SG_CHUNK, SG_CHUNK), (SG_CHUNK, SG_WIDTH), (1, SG_WIDTH), (1, SG_WIDTH)])
        gs["sg_w"][l] = dsw.reshape(1, SG_WIDTH // SG_GROUP_DIM, SG_CHUNK, SG_CHUNK)
        gs["sg_b"][l] = dbias.reshape(SG_CHUNK, SG_WIDTH // SG_GROUP_DIM, SG_GROUP_DIM).sum(-1).T[None]

        d_o, dz, gs["hg_norm"][l] = mm_fused(
            t + "proj_a_dx", m, [(dpa, 0)], [(W["w_a"], 0)], "nt", D,
            e_ins=[(sv["o_f"], D, 0), (sv["o_b"], D, 0), (sv["z"], D, ZG)], consts=[hg_g], epilogue=_hg_post_bwd,
            outs=[(D, BF16)], alias_outs=[(dz, D, ZG)], accs=[(1, D)], tm=256)
        if l == 0:
            part = {n: (g if not isinstance(g, list) else jnp.concatenate(
                [jnp.zeros((1,) + g[1].shape[1:], F32) if gl is None else gl for gl in g], axis=0))
                for n, g in gs.items()}
            plan.early_small(_pack([part[n].reshape(S[n].shape) for n in SMALL]))
        (dq_f, dv_f, dq_b, dv_b, dzf_f, dzf_b, dlb_f[l], dlb_b[l]), extra = hgrn_bwd(
            t + "hgrn", sv["z"], d_o, sv["s_f"], sv["s_b"], lbf, lbb, exch=plan.exch(t + "hgrn"))
        plan.done(t + "hgrn", extra)

        def combine(dqf, dqb, dvf, dvb, dff_, dfb_, zq):
            dq = dqf.astype(F32) + dqb.astype(F32)
            dv = dvf.astype(F32) + dvb.astype(F32)
            return (jnp.concatenate([(dq * _silu_grad(zq)).astype(BF16), dff_, dfb_, dv.astype(BF16)], axis=1),)
        (dz,) = rowwise(t + "hgrn_combine", combine, m,
                        ins=[(dq_f, D, 0), (dq_b, D, 0), (dv_f, D, 0), (dv_b, D, 0), (dzf_f, D, 0), (dzf_b, D, 0),
                             (sv["z"], D, ZQ)], alias_outs=[(dz, 4 * D, 0)], tm=128)
        wgrad("w_in", t + "w_in", sv["h"], dz)

        def pre_bwd(dhv, d1, xv, g):
            xh, r = _rms(xv)
            return d1 + _rms_bwd(dhv, xh, r, g), _colsum(dhv * xh)
        ex = plan.exch(t + "in_proj_dx")
        res = mm_fused(t + "in_proj_dx", m, [(dz, 0)], [(W["w_in"], 0)], "nt", N_IN,
                       e_ins=[(dx1, D, 0), (sv["x"], D, 0)], consts=[g_pre], epilogue=pre_bwd, outs=[(D, F32)],
                       accs=[(1, D)], tm=1024, exch=ex)
        if ex is not None:
            res, extra = res
            plan.done(t + "in_proj_dx", extra)
        dx, gs["norm_mix_pre"][l] = res
        saved[l] = None
        if l == DEPTH - 1:
            none = jnp.zeros((1, D), F32)
            gs["lb_gamma_fwd"], gs["lb_gamma_bwd"] = lower_bounds_bwd(
                "lower_bounds_bwd", S["lb_gamma_fwd"], S["lb_gamma_bwd"], jnp.concatenate([none, dlb_f[l]], axis=0),
                jnp.concatenate([none, dlb_b[l]], axis=0))

    small ={n: (g if not isinstance(g, list) else jnp.concatenate(g, axis=0)).reshape(S[n].shape)
             for n, g in gs.items()}
    return loss_cols, dx, small


def cast_pad(name, w, rows_p, cols_p):
    _, r, c = w.shape

    def body(w_ref, o_ref):
        if (rows_p, cols_p) != (r, c):
            o_ref[...] = jnp.zeros(o_ref.shape, BF16)
        o_ref[0:r, 0:c] = w_ref[...].astype(BF16)

    return pl.pallas_call(
        body, name=name, grid=(DEPTH,), in_specs=[pl.BlockSpec((None, r, c), lambda l: (l, 0, 0))],
        out_specs=pl.BlockSpec((None, rows_p, cols_p), lambda l: (l, 0, 0)),
        out_shape=jax.ShapeDtypeStruct((DEPTH, rows_p, cols_p), BF16), compiler_params=_params(("parallel",)),
    )(w)


def _shard_shape(n, shape):
    axis, size, _, _ = LAYOUT[n]
    _, r, c = shape
    return (size, c) if axis == 0 else (r, size)


class DistPlan:
    def __init__(self, shards):
        self.shards = shards
        self.W = [dict() for _ in range(DEPTH)]
        self.grads = [dict() for _ in range(DEPTH)]
        self.slots = [dict() for _ in range(DEPTH)]
        rest = [n for n in BIG if n != "w_in"]
        ffn = ["w_gate", "w_up", "w_down"]
        self.schedule = {
            "l0_in_proj": ("gather", [(0, n) for n in rest]),
            "l0_hgrn_fwd": ("gather", [(1, n) for n in BIG if n not in ffn]),
            "l0_ffn_in": ("gather", [(1, n) for n in ffn]),
            "l1_bwd_hgrn": ("scatter", [(1, n) for n in rest]),
            "l1_bwd_in_proj_dx": ("scatter", [(1, "w_in")]),
            "l0_bwd_hgrn": ("scatter", [(0, n) for n in rest]),
            "l0_bwd_in_proj_dx": ("scatter", [(0, "w_in")]),
        }
        self.pending = {}
        self.small_part = self.small_slots = None
        axis, size, dst, _ = LAYOUT["w_in"]
        self.W[0][dst] = gather_two_level("gather_l0_w_in", shards["w_in"], 0, axis, size, GATHERED[dst])

    def _gather(self, host, parts):
        srcs, dsts, items, keys = [], [], [], []
        for layer, n in parts:
            axis, size, dst, base = LAYOUT[n]
            if (layer, dst) not in keys:
                keys.append((layer, dst))
                dsts.append((GATHERED[dst], BF16))
            srcs.append(self.shards[n])
            items.append(("gather", len(srcs) - 1, keys.index((layer, dst)), axis, size, base, layer))
        self.pending[host] = ("gather", keys)
        return Exchange(srcs, dsts, items)

    def _scatter(self, host, parts):
        srcs, dsts, items = [], [], []
        for layer, n in parts:
            axis, size, _, _ = LAYOUT[n]
            srcs.append(self.grads[layer][n])
            dsts.append(((NDEV,) + _shard_shape(n, self.shards[n].shape), BF16))
            items.append(("scatter", len(srcs) - 1, len(dsts) - 1, axis, size, 0, None))
        keys = list(parts)
        if host == "l0_bwd_hgrn" and self.small_part is not None:
            srcs.append(self.small_part)
            dsts.append(((NDEV,) + self.small_part.shape, F32))
            items.append(("copies", len(srcs) - 1, len(dsts) - 1, 0, 0, 0, None))
            keys.append(("small", None))
        self.pending[host] = ("scatter", keys)
        return Exchange(srcs, dsts, items)

    def early_small(self, packed):
        self.small_part = packed

    def exch(self, host):
        if host not in self.schedule:
            return None
        kind, parts = self.schedule[host]
        return self._gather(host, parts) if kind == "gather" else self._scatter(host, parts)

    def done(self, host, outs):
        if host not in self.pending:
            return
        kind, keys = self.pending.pop(host)
        for (layer, n), arr in zip(keys, outs):
            if layer == "small":
                self.small_slots = arr
            else:
                (self.W if kind == "gather" else self.slots)[layer][n] = arr


def adam(name, w, m_, v_, tr, g=None, slots=None):
    L, r, c = w.shape
    assert r % tr == 0
    nt = r // tr
    n_s = 0 if slots is None else L

    def body(*refs):
        s_refs = refs[:n_s]
        g_ref = refs[n_s] if g is not None else None
        w_ref, m_ref, v_ref, g_out, d_out, m_out, v_out = refs[n_s + (g is not None):]

        def update(gv):
            if g_ref is not None:
                gv = gv + g_ref[...] if gv is not None else g_ref[...]
            m2 = B1 * m_ref[...] + (1.0 - B1) * gv
            v2 = B2 * v_ref[...] + (1.0 - B2) * (gv * gv)
            m_hat = m2 / (1.0 - B1 ** STEP)
            v_hat = v2 / (1.0 - B2 ** STEP)
            g_out[...] = gv
            d_out[...] = -LR * (m_hat / (jnp.sqrt(v_hat) + AEPS) + WD * w_ref[...])
            m_out[...] = m2
            v_out[...] = v2

        if slots is None:
            update(None)
            return
        for layer, s_ref in enumerate(s_refs):
            @pl.when(pl.program_id(0) == layer)
            def _():
                gv = s_ref[0][:, :c].astype(F32)
                for j in range(1, NDEV):
                    gv = gv + s_ref[j][:, :c].astype(F32)
                update(gv)

    spec = pl.BlockSpec((None, tr, c), lambda l, i: (l, i, 0))
    arrs, specs = [], []
    if slots is not None:
        assert len(slots) == L and L <= 2
        arrs = list(slots)
        cp = slots[0].shape[2]
        specs = [pl.BlockSpec((NDEV, tr, cp), lambda l, i: (0, i * (1 - l) + (nt - 1) * l, 0)),
                 pl.BlockSpec((NDEV, tr, cp), lambda l, i: (0, i * l, 0))][:L]
    if g is not None:
        arrs.append(g)
        specs.append(spec)
    shp = jax.ShapeDtypeStruct(w.shape, F32)
    return pl.pallas_call(
        body, name=name, grid=(L, nt), in_specs=specs + [spec, spec, spec], out_specs=[spec] * 4,
        out_shape=[shp] * 4, compiler_params=_params(("arbitrary", "arbitrary")),
    )(*arrs, w, m_, v_)


def _pack(arrs):
    parts = []
    for a in arrs:
        a2 = a.reshape(-1, D)
        parts.append(jnp.pad(a2, ((0, -a2.shape[0] % 8), (0, 0))))
    return jnp.concatenate(parts, axis=0)


def _unpack(buf, shapes):
    out, off = [], 0
    for s in shapes:
        rows = 1
        for d_ in s:
            rows *= d_
        rows //= D
        out.append(buf[off:off + rows].reshape(s))
        off += rows + (-rows % 8)
    return out


def kernel(x, p, norm_mix_pre, w_in, lb_gamma_fwd, lb_gamma_bwd, hg_norm, sg_w, sg_b, sg_ln_g, sg_ln_b, w_a, w_b, w_out, norm_mix_post, norm_ffn_pre, w_gate, w_up, w_down, norm_ffn_post, w_ple, w_ple_gate, loss_target, m_norm_mix_pre, m_w_in, m_lb_gamma_fwd, m_lb_gamma_bwd, m_hg_norm, m_sg_w, m_sg_b, m_sg_ln_g, m_sg_ln_b, m_w_a, m_w_b, m_w_out, m_norm_mix_post, m_norm_ffn_pre, m_w_gate, m_w_up, m_w_down, m_norm_ffn_post, m_w_ple, m_w_ple_gate, v_norm_mix_pre, v_w_in, v_lb_gamma_fwd, v_lb_gamma_bwd, v_hg_norm, v_sg_w, v_sg_b, v_sg_ln_g, v_sg_ln_b, v_w_a, v_w_b, v_w_out, v_norm_mix_post, v_norm_ffn_pre, v_w_gate, v_w_up, v_w_down, v_norm_ffn_post, v_w_ple, v_w_ple_gate):
    a = dict(zip(INPUTS, (x, p, norm_mix_pre, w_in, lb_gamma_fwd, lb_gamma_bwd, hg_norm, sg_w, sg_b, sg_ln_g, sg_ln_b, w_a, w_b, w_out, norm_mix_post, norm_ffn_pre, w_gate, w_up, w_down, norm_ffn_post, w_ple, w_ple_gate, loss_target, m_norm_mix_pre, m_w_in, m_lb_gamma_fwd, m_lb_gamma_bwd, m_hg_norm, m_sg_w, m_sg_b, m_sg_ln_g, m_sg_ln_b, m_w_a, m_w_b, m_w_out, m_norm_mix_post, m_norm_ffn_pre, m_w_gate, m_w_up, m_w_down, m_norm_ffn_post, m_w_ple, m_w_ple_gate, v_norm_mix_pre, v_w_in, v_lb_gamma_fwd, v_lb_gamma_bwd, v_hg_norm, v_sg_w, v_sg_b, v_sg_ln_g, v_sg_ln_b, v_w_a, v_w_b, v_w_out, v_norm_mix_post, v_norm_ffn_pre, v_w_gate, v_w_up, v_w_down, v_norm_ffn_post, v_w_ple, v_w_ple_gate)))
    m = x.shape[1]

    shards = {n: cast_pad("cast_" + n, a[n], *_shard_shape(n, a[n].shape)) for n in BIG}
    plan = DistPlan(shards)
    loss_cols, dx, gs = local_step(x[0], p[:, 0], loss_target[0], {n: a[n] for n in SMALL}, plan)
    loss = lax.psum(jnp.sum(loss_cols) * (0.5 / D), ("x", "y", "c"))

    small_shapes = [a[n].shape for n in SMALL]
    rows = plan.small_slots.shape[1]
    late = allreduce_small("allreduce_small", jnp.pad(gs["norm_mix_pre"][0:1], ((0, 7), (0, 0))))
    g_late = jnp.pad(late, ((0, rows - 8), (0, 0)))[None]

    res = {}
    row_tiles = {"w_in": 128, "w_a": 128, "w_b": 512, "w_out": 128, "w_gate": 128, "w_up": 128, "w_down": 88,
                 "w_ple": 256, "w_ple_gate": 128}
    for n in BIG:
        res[n] = adam("adam_" + n, a[n], a["m_" + n], a["v_" + n], row_tiles[n],
                      slots=[plan.slots[l][n] for l in range(DEPTH)])
    packed = [_pack([a[pre + n] for n in SMALL])[None] for pre in ("", "m_", "v_")]
    small_res = adam("adam_small", packed[0], packed[1], packed[2], rows // 2, g=g_late, slots=[plan.small_slots])
    small_res = [_unpack(r_[0], small_shapes) for r_ in small_res]
    for i, n in enumerate(SMALL):
        res[n] = tuple(small_res[k][i] for k in range(4))

    outs = [loss, dx.reshape(1, m, D)]
    for k in range(4):
        outs += [res[n][k] for n in WEIGHTS]
    return tuple(outs)
```

```python
import jax
import jax.numpy as jnp
from jax import lax
from jax.experimental import pallas as pl
from jax.experimental.pallas import tpu as pltpu

F32 = jnp.float32
BF16 = jnp.bfloat16

D = 1024
N_IN = 8192
HEADS = 8
HEAD_DIM = 128
SG_CHUNK = 128
SG_WIDTH = 512
SG_GROUP_DIM = 64
FFN = 2816
PLE_DIM = 256
EPS = 1e-6
DEPTH = 2
ZQ, ZFF, ZFB, ZI, ZG, GA, GB = 0, 1, 2, 3, 4, 6, 7
ZU, ZV = 10, 11

NDEV = 8
FFN_SHARD = FFN // NDEV
FFN_SHARD_PAD = 384
FFN_PAD = NDEV * FFN_SHARD_PAD

LR, B1, B2, AEPS, WD, STEP = 0.001, 0.9, 0.999, 1e-08, 0.01, 10

ROW_TILE = 256
HG_CHUNK = 64
HG_BLOCK_FWD = 256
HG_BLOCK_BWD = 128
EXP_CLAMP = 80.0
PROLOGUE_CHUNK = 256
TINY = float(jnp.finfo(jnp.float32).tiny)
VMEM_LIMIT = 56 * 1024 * 1024

BIG = ["w_in", "w_a", "w_b", "w_out", "w_gate", "w_up", "w_down", "w_ple", "w_ple_gate"]
SMALL = ["norm_mix_pre", "lb_gamma_fwd", "lb_gamma_bwd", "hg_norm", "sg_w", "sg_b", "sg_ln_g", "sg_ln_b",
         "norm_mix_post", "norm_ffn_pre", "norm_ffn_post"]
WEIGHTS = ["norm_mix_pre", "w_in", "lb_gamma_fwd", "lb_gamma_bwd", "hg_norm", "sg_w", "sg_b", "sg_ln_g", "sg_ln_b",
           "w_a", "w_b", "w_out", "norm_mix_post", "norm_ffn_pre", "w_gate", "w_up", "w_down", "norm_ffn_post",
           "w_ple", "w_ple_gate"]
INPUTS = (["x", "p"] + WEIGHTS + ["loss_target"] + ["m_" + n for n in WEIGHTS] + ["v_" + n for n in WEIGHTS])
LAYOUT = {
    "w_in": (1, 1024, "w_in", 0), "w_a": (0, 128, "w_a", 0), "w_b": (1, 128, "w_b", 0),
    "w_out": (0, 128, "w_out", 0), "w_gate": (1, FFN_SHARD_PAD, "w_gu", 0),
    "w_up": (1, FFN_SHARD_PAD, "w_gu", FFN_PAD), "w_down": (0, FFN_SHARD_PAD, "w_down", 0),
    "w_ple": (1, 128, "w_ple", 0), "w_ple_gate": (0, 128, "w_ple_gate", 0),
}
GATHERED = {"w_in": (D, N_IN), "w_a": (D, D), "w_b": (SG_WIDTH, D), "w_out": (D, D), "w_gu": (D, 2 * FFN_PAD),
            "w_down": (FFN_PAD, D), "w_ple": (PLE_DIM, D), "w_ple_gate": (D, D)}


def _params(sem):
    return pltpu.CompilerParams(dimension_semantics=sem, vmem_limit_bytes=VMEM_LIMIT)


def _dot(a, b):
    return lax.dot_general(a, b, (((1,), (0,)), ((), ())), preferred_element_type=F32)


def _dot_nt(a, b):
    return lax.dot_general(a, b, (((1,), (1,)), ((), ())), preferred_element_type=F32)


def _dot_tn(a, b):
    return lax.dot_general(a, b, (((0,), (0,)), ((), ())), preferred_element_type=F32)


def _sigmoid(x):
    return jax.nn.sigmoid(x)


def _silu(x):
    return x * _sigmoid(x)


def _silu_grad(x):
    s = _sigmoid(x)
    return s * (1.0 + x * (1.0 - s))


def _gelu(x):
    return 0.5 * x * (1.0 + lax.erf(x * 0.7071067811865476))


def _gelu_grad(x):
    return 0.5 * (1.0 + lax.erf(x * 0.7071067811865476)) + x * jnp.exp(-0.5 * x * x) * 0.3989422804014327


def _mean(x):
    return jnp.mean(x, axis=-1, keepdims=True)


def _colsum(x):
    return jnp.sum(x, axis=0, keepdims=True)


def _rms(x):
    r = lax.rsqrt(_mean(x * x) + EPS)
    return x * r, r


def _rms_bwd(dy, xh, r, g):
    dyg = dy * g
    return r * (dyg - xh * _mean(dyg * xh))


MESH = pl.DeviceIdType.MESH
ANY = pl.BlockSpec(memory_space=pl.ANY)


def _slab(ref, axis, start, size):
    idx = [slice(None)] * 2
    idx[axis] = pl.ds(start, size)
    return ref.at[tuple(idx)]


class Exchange:
    def __init__(self, srcs, dsts, items):
        self.srcs, self.dsts, self.items = list(srcs), list(dsts), list(items)

    def specs(self):
        n = len(self.items)
        sems = [pltpu.SemaphoreType.DMA((n * (NDEV - 1),)), pltpu.SemaphoreType.DMA((n * (NDEV - 1),)),
                pltpu.SemaphoreType.DMA((n,))]
        return ([ANY] * len(self.srcs), [ANY] * len(self.dsts),
                [jax.ShapeDtypeStruct(s, dt) for (s, dt) in self.dsts], sems)

    def copies(self, src, dst, send_sem, recv_sem, loc_sem):
        x, y, c = lax.axis_index("x"), lax.axis_index("y"), lax.axis_index("c")
        me = 4 * x + 2 * y + c
        starts, waits = [], []
        for n, (kind, si, di, axis, size, base, layer) in enumerate(self.items):
            def views(to_dev, from_dev):
                if kind == "gather":
                    return (src[si].at[layer],
                            _slab(dst[di], axis, base + pl.multiple_of(from_dev * size, 128), size))
                if kind == "copies":
                    return src[si], dst[di].at[from_dev]
                return _slab(src[si], axis, base + pl.multiple_of(to_dev * size, 128), size), dst[di].at[from_dev]

            s_own, d_own = views(me, me)
            own = pltpu.make_async_copy(s_own, d_own, loc_sem.at[n])
            starts.append(own)
            waits.append(own)
            for k in range(1, NDEV):
                px = 1 - x if k & 4 else x
                py = 1 - y if k & 2 else y
                pc = 1 - c if k & 1 else c
                peer = 4 * px + 2 * py + pc
                s_out, _ = views(peer, me)
                _, d_in = views(me, peer)
                sem = n * (NDEV - 1) + k - 1
                starts.append(pltpu.make_async_remote_copy(s_out, d_own, send_sem.at[sem], recv_sem.at[sem],
                                                           device_id=(px, py, pc), device_id_type=MESH))
                waits.append(pltpu.make_async_remote_copy(s_out, d_in, send_sem.at[sem], recv_sem.at[sem],
                                                          device_id=(px, py, pc), device_id_type=MESH))
        return starts, waits


def exchange(name, exch):
    e_in, e_out, e_shape, e_scr = exch.specs()
    ns, nd = len(e_in), len(e_out)

    def body(*refs):
        starts, waits = exch.copies(refs[:ns], refs[ns:ns + nd], *refs[ns + nd:])
        for cp in starts:
            cp.start()
        for cp in waits:
            cp.wait()

    return pl.pallas_call(body, name=name, in_specs=e_in, out_specs=e_out, out_shape=e_shape, scratch_shapes=e_scr,
                          compiler_params=pltpu.CompilerParams(has_side_effects=True))(*exch.srcs)


def gather_two_level(name, shards, layer, axis, size, full_shape):
    def body(src, dst, send_sem, recv_sem, loc_sem):
        x, y, c = lax.axis_index("x"), lax.axis_index("y"), lax.axis_index("c")
        mine = src.at[layer]
        chips = [(1 - x, y), (x, 1 - y), (1 - x, 1 - y)]

        def slab(px, py, pc):
            return _slab(dst, axis, pl.multiple_of((4 * px + 2 * py + pc) * size, 128), size)

        def copy(k, from_ref, block, to):
            return pltpu.make_async_remote_copy(from_ref, slab(*block), send_sem.at[k], recv_sem.at[k], device_id=to,
                                                device_id_type=MESH)

        own = pltpu.make_async_copy(mine, slab(x, y, c), loc_sem)
        own.start()
        first = [copy(0, mine, (x, y, c), (x, y, 1 - c))]
        first += [copy(1 + j, mine, (x, y, c), (*chip, c)) for j, chip in enumerate(chips)]
        for cp in first:
            cp.start()
        passed = []
        for j, chip in enumerate(chips):
            copy(1 + j, mine, (*chip, c), (x, y, c)).wait_recv()
            fwd = copy(4 + j, slab(*chip, c), (*chip, c), (x, y, 1 - c))
            fwd.start()
            passed.append(fwd)
        copy(0, mine, (x, y, 1 - c), (x, y, c)).wait_recv()
        for j, chip in enumerate(chips):
            copy(4 + j, mine, (*chip, 1 - c), (x, y, c)).wait_recv()
        for cp in first + passed:
            cp.wait_send()
        own.wait()

    return pl.pallas_call(
        body, name=name, in_specs=[ANY], out_specs=ANY, out_shape=jax.ShapeDtypeStruct(full_shape, shards.dtype),
        scratch_shapes=[pltpu.SemaphoreType.DMA((NDEV - 1,)), pltpu.SemaphoreType.DMA((NDEV - 1,)),
                        pltpu.SemaphoreType.DMA(())],
        compiler_params=pltpu.CompilerParams(has_side_effects=True))(shards)


def hosted_call(body, exch, name, grid, in_specs, out_specs, out_shape, scratch_shapes, operands, semantics,
                aliases=None):
    aliases = aliases or {}
    if exch is None:
        res = pl.pallas_call(body, name=name, grid=grid, in_specs=in_specs, out_specs=out_specs, out_shape=out_shape,
                             scratch_shapes=scratch_shapes, input_output_aliases=aliases,
                             compiler_params=_params(semantics))(*operands)
        return list(res), []
    n_in, n_out, n_scr = len(in_specs), len(out_specs), len(scratch_shapes)
    e_in, e_out, e_shape, e_scr = exch.specs()
    ns, nd = len(e_in), len(e_out)

    def at_step(last):
        cond = None
        for ax, n in enumerate(grid):
            c = pl.program_id(ax) == (n - 1 if last else 0)
            cond = c if cond is None else jnp.logical_and(cond, c)
        return cond

    def wrapped(*refs):
        ins, src = refs[:n_in], refs[n_in:n_in + ns]
        o0 = n_in + ns
        outs, dst = refs[o0:o0 + n_out], refs[o0 + n_out:o0 + n_out + nd]
        s0 = o0 + n_out + nd
        scr, sems = refs[s0:s0 + n_scr], refs[s0 + n_scr:]

        @pl.when(at_step(False))
        def _():
            for cp in exch.copies(src, dst, *sems)[0]:
                cp.start()

        body(*ins, *outs, *scr)

        @pl.when(at_step(True))
        def _():
            for cp in exch.copies(src, dst, *sems)[1]:
                cp.wait()

    res = pl.pallas_call(
        wrapped, name=name, grid=grid, in_specs=list(in_specs) + e_in, out_specs=list(out_specs) + e_out,
        out_shape=list(out_shape) + e_shape, scratch_shapes=list(scratch_shapes) + e_scr,
        input_output_aliases=aliases,
        compiler_params=pltpu.CompilerParams(dimension_semantics=("arbitrary",) * len(grid),
                                             vmem_limit_bytes=VMEM_LIMIT, has_side_effects=True),
    )(*operands, *exch.srcs)
    return list(res[:n_out]), list(res[n_out:])


def allreduce_small(name, part):
    rows, width = part.shape

    def body(p_ref, o_ref, buf, send_sem, recv_sem):
        x, y, c = lax.axis_index("x"), lax.axis_index("y"), lax.axis_index("c")
        me = 4 * x + 2 * y + c
        buf[me] = p_ref[...]
        waits = []
        for k in range(1, NDEV):
            px = 1 - x if k & 4 else x
            py = 1 - y if k & 2 else y
            pc = 1 - c if k & 1 else c
            peer = 4 * px + 2 * py + pc
            pltpu.make_async_remote_copy(p_ref, buf.at[me], send_sem.at[k - 1], recv_sem.at[k - 1],
                                         device_id=(px, py, pc), device_id_type=MESH).start()
            waits.append(pltpu.make_async_remote_copy(p_ref, buf.at[peer], send_sem.at[k - 1], recv_sem.at[k - 1],
                                                      device_id=(px, py, pc), device_id_type=MESH))
        for w in waits:
            w.wait()
        acc = buf[0]
        for j in range(1, NDEV):
            acc = acc + buf[j]
        o_ref[...] = acc

    vmem = pl.BlockSpec(memory_space=pltpu.VMEM)
    return pl.pallas_call(
        body, name=name, in_specs=[vmem], out_specs=vmem, out_shape=jax.ShapeDtypeStruct((rows, width), F32),
        scratch_shapes=[pltpu.VMEM((NDEV, rows, width), F32), pltpu.SemaphoreType.DMA((NDEV - 1,)),
                        pltpu.SemaphoreType.DMA((NDEV - 1,))],
        compiler_params=pltpu.CompilerParams(vmem_limit_bytes=VMEM_LIMIT, has_side_effects=True),
    )(part)


def rowwise(name, fn, m, ins=(), consts=(), outs=(), alias_outs=(), accs=(), tm=ROW_TILE):
    tm = min(tm, m)
    n_in, n_c, n_o, n_al, n_ac = len(ins), len(consts), len(outs), len(alias_outs), len(accs)
    held = [a for (a, _, _) in alias_outs if not isinstance(a, jax.ShapeDtypeStruct)]
    n_held = len(held)

    def body(*refs):
        in_refs = refs[:n_in + n_c]
        out_refs = refs[n_in + n_c + n_held:]
        vals = fn(*[r[...] for r in in_refs])
        if not isinstance(vals, (tuple, list)):
            vals = (vals,)
        for r, v in zip(out_refs[:n_o + n_al], vals[:n_o + n_al]):
            r[...] = v.astype(r.dtype)
        if n_ac:
            acc_refs = out_refs[n_o + n_al:]

            @pl.when(pl.program_id(0) == 0)
            def _():
                for r in acc_refs:
                    r[...] = jnp.zeros(r.shape, F32)

            for r, v in zip(acc_refs, vals[n_o + n_al:]):
                r[...] += v

    def col(cb):
        return lambda i: (i, cb)

    in_specs = [pl.BlockSpec((tm, w), col(cb)) for (_, w, cb) in ins]
    in_specs += [pl.BlockSpec(c.shape, lambda i, nd=c.ndim: (0,) * nd) for c in consts]
    in_specs += [ANY for _ in held]
    out_shape = [jax.ShapeDtypeStruct((m, w), dt) for (w, dt) in outs]
    out_specs = [pl.BlockSpec((tm, w), col(0)) for (w, _) in outs]
    out_shape += [jax.ShapeDtypeStruct(a.shape, a.dtype) for (a, _, _) in alias_outs]
    out_specs += [pl.BlockSpec((tm, w), col(cb)) for (_, w, cb) in alias_outs]
    out_shape += [jax.ShapeDtypeStruct(s, F32) for s in accs]
    out_specs += [pl.BlockSpec(s, lambda i: (0, 0)) for s in accs]
    aliases, k_in = {}, n_in + n_c
    for k, (a, _, _) in enumerate(alias_outs):
        if not isinstance(a, jax.ShapeDtypeStruct):
            aliases[k_in] = n_o + k
            k_in += 1
    return pl.pallas_call(
        body, name=name, grid=(m // tm,), in_specs=in_specs, out_specs=out_specs, out_shape=out_shape,
        input_output_aliases=aliases,
        compiler_params=_params(("arbitrary",) if n_ac else ("parallel",)),
    )(*[a for (a, _, _) in ins], *consts, *held)


def _operand(arr, bshape, imap):
    if isinstance(arr, tuple):
        arr, lead = arr
        return arr, pl.BlockSpec((None,) + bshape, lambda *g: (lead,) + imap(*g))
    return arr, pl.BlockSpec(bshape, imap)


def _shape2(arr):
    return arr[0].shape[1:] if isinstance(arr, tuple) else arr.shape


def mm(name, a, b, mode, out_dtype=F32, tm=1024, tn=1024, tk=1024, exch=None):
    sa, sb = _shape2(a), _shape2(b)
    if mode == "nn":
        (M, K), N = sa, sb[1]
    elif mode == "nt":
        (M, K), N = sa, sb[0]
    else:
        (K, M), N = sa, sb[1]
    tm, tn, tk = min(tm, M), min(tn, N), min(tk, K)
    assert M % tm == 0 and N % tn == 0 and K % tk == 0, (name, M, N, K)
    nk = K // tk
    if mode == "nn":
        a_arr, a_spec = _operand(a, (tm, tk), lambda i, j, k: (i, k))
        b_arr, b_spec = _operand(b, (tk, tn), lambda i, j, k: (k, j))
        dot = _dot
    elif mode == "nt":
        a_arr, a_spec = _operand(a, (tm, tk), lambda i, j, k: (i, k))
        b_arr, b_spec = _operand(b, (tn, tk), lambda i, j, k: (j, k))
        dot = _dot_nt
    else:
        a_arr, a_spec = _operand(a, (tk, tm), lambda i, j, k: (k, i))
        b_arr, b_spec = _operand(b, (tk, tn), lambda i, j, k: (k, j))
        dot = _dot_tn

    def body(a_ref, b_ref, o_ref, *acc):
        part = dot(a_ref[...].astype(BF16), b_ref[...].astype(BF16))
        if nk == 1:
            o_ref[...] = part.astype(o_ref.dtype)
            return
        acc_ref, k = acc[0], pl.program_id(2)

        @pl.when(k == 0)
        def _():
            acc_ref[...] = part

        @pl.when(k > 0)
        def _():
            acc_ref[...] += part

        @pl.when(k == nk - 1)
        def _():
            o_ref[...] = acc_ref[...].astype(o_ref.dtype)

    outs, extra = hosted_call(
        body, exch, name, (M // tm, N // tn, nk), [a_spec, b_spec], [pl.BlockSpec((tm, tn), lambda i, j, k: (i, j))],
        [jax.ShapeDtypeStruct((M, N), out_dtype)], [pltpu.VMEM((tm, tn), F32)] if nk > 1 else [], [a_arr, b_arr],
        ("parallel", "parallel", "arbitrary"))
    return outs[0] if exch is None else (outs[0], extra)


def mm_fused(name, m, a_ins, bs, mode, kdim, prologue=None, a_outs=(), e_ins=(), consts=(), epilogue=None, outs=(),
             alias_outs=(), accs=(), a_to_epilogue=(), a_consts=(), tm=512, tk=1024, exch=None):
    tm = min(tm, m)
    nk = kdim // tk
    assert nk == 1 or not a_to_epilogue
    n = bs[0][0].shape[1 if mode == "nn" else 0]
    n_a, n_b, n_e, n_c = len(a_ins) + len(a_consts), len(bs), len(e_ins), len(consts)
    n_ao, n_o, n_al, n_ac = len(a_outs), len(outs), len(alias_outs), len(accs)
    held = [a for (a, _, _) in alias_outs if not isinstance(a, jax.ShapeDtypeStruct)]
    dot = _dot if mode == "nn" else _dot_nt

    def body(*refs):
        a_refs, b_refs = refs[:n_a], refs[n_a:n_a + n_b]
        e_refs = refs[n_a + n_b:n_a + n_b + n_e + n_c]
        o0 = n_a + n_b + n_e + n_c + len(held)
        ao_refs = refs[o0:o0 + n_ao]
        out_refs = refs[o0 + n_ao:o0 + n_ao + n_o + n_al]
        acc_refs = refs[o0 + n_ao + n_o + n_al:o0 + n_ao + n_o + n_al + n_ac]
        scr = refs[o0 + n_ao + n_o + n_al + n_ac:]
        i, k = pl.program_id(0), pl.program_id(1)
        ck = tk if prologue is None else min(tk, PROLOGUE_CHUNK)
        part = None
        for c0 in range(0, tk, ck):
            cols = slice(c0, c0 + ck)
            tiles = [r[:, cols] for r in a_refs]
            a_list, extra = (tiles, []) if prologue is None else prologue(*tiles)
            for r, v in zip(ao_refs, extra):
                r[:, cols] = v.astype(r.dtype)
            for a, b_ref in zip(a_list, b_refs):
                b = b_ref[cols, :] if mode == "nn" else b_ref[:, cols]
                prod = dot(a.astype(BF16), b.astype(BF16))
                part = prod if part is None else part + prod

        def finish(total):
            vals = epilogue(total, *[a_refs[j][...] for j in a_to_epilogue], *[r[...] for r in e_refs])
            if not isinstance(vals, (tuple, list)):
                vals = (vals,)
            for r, v in zip(out_refs, vals[:n_o + n_al]):
                r[...] = v.astype(r.dtype)
            for r, v in zip(acc_refs, vals[n_o + n_al:]):
                @pl.when(i == 0)
                def _():
                    r[...] = v

                @pl.when(i > 0)
                def _():
                    r[...] += v

        if nk == 1:
            finish(part)
            return
        acc_ref = scr[0]

        @pl.when(k == 0)
        def _():
            acc_ref[...] = part

        @pl.when(k > 0)
        def _():
            acc_ref[...] += part

        @pl.when(k == nk - 1)
        def _():
            finish(acc_ref[...])

    in_specs = [pl.BlockSpec((tm, tk), lambda i, k, off=off: (i, k + off)) for (_, off) in a_ins]
    in_specs += [pl.BlockSpec((1, tk), lambda i, k: (0, k)) for _ in a_consts]
    if mode == "nn":
        in_specs += [pl.BlockSpec((tk, n), lambda i, k, off=off: (k + off, 0)) for (_, off) in bs]
    else:
        in_specs += [pl.BlockSpec((n, tk), lambda i, k, off=off: (0, k + off)) for (_, off) in bs]
    in_specs += [pl.BlockSpec((tm, w), lambda i, k, cb=cb: (i, cb)) for (_, w, cb) in e_ins]
    in_specs += [pl.BlockSpec(c.shape, lambda i, k, nd=c.ndim: (0,) * nd) for c in consts]
    in_specs += [ANY for _ in held]
    out_shape = [jax.ShapeDtypeStruct((m, kdim), dt) for dt in a_outs]
    out_specs = [pl.BlockSpec((tm, tk), lambda i, k: (i, k)) for _ in a_outs]
    out_shape += [jax.ShapeDtypeStruct((m, w), dt) for (w, dt) in outs]
    out_specs += [pl.BlockSpec((tm, w), lambda i, k: (i, 0)) for (w, _) in outs]
    out_shape += [jax.ShapeDtypeStruct(a.shape, a.dtype) for (a, _, _) in alias_outs]
    out_specs += [pl.BlockSpec((tm, w), lambda i, k, cb=cb: (i, cb)) for (_, w, cb) in alias_outs]
    out_shape += [jax.ShapeDtypeStruct(s_, F32) for s_ in accs]
    out_specs += [pl.BlockSpec(s_, lambda i, k: (0, 0)) for s_ in accs]
    aliases, k_in = {}, n_a + n_b + n_e + n_c
    for j, (a, _, _) in enumerate(alias_outs):
        if not isinstance(a, jax.ShapeDtypeStruct):
            aliases[k_in] = n_ao + n_o + j
            k_in += 1
    operands = [a for (a, _) in a_ins] + list(a_consts) + [b for (b, _) in bs] + [a for (a, _, _) in e_ins] + list(consts) + held
    res, extra = hosted_call(
        body, exch, name, (m // tm, nk), in_specs, out_specs, out_shape,
        [pltpu.VMEM((tm, n), F32)] if nk > 1 else [], operands,
        ("arbitrary" if n_ac else "parallel", "arbitrary"), aliases)
    return res if exch is None else (res, extra)


def _cumsum_rows(x):
    n = x.shape[0]
    row = lax.broadcasted_iota(jnp.int32, x.shape, 0)
    s = 1
    while s < n:
        x = x + jnp.where(row >= s, pltpu.roll(x, s, 0), 0.0)
        s *= 2
    return x


def _hg_prep(zq, zf, lb, reverse, b=None):
    n = zq.shape[0]
    q = _silu(zq)
    sig = _sigmoid(zf)
    sn = 1.0 - sig
    f = lb + (1.0 - lb) * sig
    k = (1.0 - lb) * sn
    if b is None:
        g = jnp.log(jnp.maximum(f, TINY))
        b = _cumsum_rows(g)
        if reverse:
            b = b[n - 1:n] - b + g
    b_last = b[0:1] if reverse else b[n - 1:n]
    b_ref = b[n // 2:n // 2 + 1]
    e1 = jnp.exp(b)
    e2 = jnp.exp(jnp.clip(b - b_ref, -EXP_CLAMP, EXP_CLAMP))
    e3 = jnp.exp(jnp.clip(b_ref - b, -EXP_CLAMP, EXP_CLAMP))
    e4 = jnp.exp(b_last - b)
    return dict(q=q, k=k, sig=sig, sn=sn, f=f, b=b, e1=e1, e2=e2, e3=e3, e4=e4, e_last=jnp.exp(b_last),
                qe=(q * e1).astype(BF16), qt=(q * e2).astype(BF16), kt=(k * e3).astype(BF16),
                ks=(k * e4).astype(BF16))


def _hg_mask(n, reverse):
    t = lax.broadcasted_iota(jnp.int32, (n, n), 0)
    s = lax.broadcasted_iota(jnp.int32, (n, n), 1)
    return (s >= t) if reverse else (s <= t)


def hgrn_fwd(name, z, lb_f, lb_b, exch=None, unroll=False):
    m = z.shape[0]
    C, T = HG_CHUNK, min(HG_BLOCK_FWD, m)
    nb, cpb = m // T, T // C

    def body(zq_f, zf_f, zi_f, zq_b, zf_b, zi_b, lbf_ref, lbb_ref, of_ref, ob_ref, sf_ref, sb_ref, bf_ref, bb_ref,
             st_ref):
        @pl.when(pl.program_id(0) == 0)
        def _():
            st_ref[...] = jnp.zeros(st_ref.shape, F32)

        dirs = ((zq_f, zf_f, zi_f, lbf_ref, of_ref, sf_ref), (zq_b, zf_b, zi_b, lbb_ref, ob_ref, sb_ref))
        b_refs = (bf_ref, bb_ref)

        def chunk(ci, carry):
            work = []
            for d, (zq, zf, zi, lb_ref, o_ref, s_ref) in enumerate(dirs):
                cc = ci if d == 0 else cpb - 1 - ci
                rows = pl.ds(pl.multiple_of(cc * C, C), C)
                pre = _hg_prep(zq[rows, :], zf[rows, :], lb_ref[...], d == 1)
                v = zi[rows, :].astype(BF16)
                work.append((cc, rows, pre, v, [st_ref[d, h] for h in range(HEADS)]))
            heads = [(d, h, slice(h * HEAD_DIM, (h + 1) * HEAD_DIM)) for d in range(2) for h in range(HEADS)]
            first = {}
            for d, h, sl in heads:
                _, _, pre, v, sts = work[d]
                first[d, h] = (_dot_nt(pre["qt"][:, sl], pre["kt"][:, sl]),
                               _dot_nt(pre["qe"][:, sl], sts[h].astype(BF16)),
                               _dot_tn(v[:, sl], pre["ks"][:, sl]))
            results = [([], []), ([], [])]
            for d, h, sl in heads:
                _, _, pre, v, sts = work[d]
                scores, o_inter, st_add = first[d, h]
                a = jnp.where(_hg_mask(C, d == 1), scores, 0.0).astype(BF16)
                results[d][0].append(o_inter + _dot(a, v[:, sl]))
                results[d][1].append(sts[h] * pre["e_last"][:, sl] + st_add)
            results = [(jnp.concatenate(o_parts, axis=1), new_sts) for (o_parts, new_sts) in results]
            for d, (zq, zf, zi, lb_ref, o_ref, s_ref) in enumerate(dirs):
                cc, rows, pre, _, sts = work[d]
                o_ref[rows, :] = results[d][0]
                b_refs[d][rows, :] = pre["b"]
                for h in range(HEADS):
                    s_ref[cc, h] = sts[h]
                    st_ref[d, h] = results[d][1][h]
            return carry

        lax.fori_loop(0, cpb, chunk, 0, unroll=unroll)

    def zspec(cb, rev):
        return pl.BlockSpec((T, D), (lambda i: (nb - 1 - i, cb)) if rev else (lambda i: (i, cb)))

    def sspec(rev):
        shape = (cpb, HEADS, HEAD_DIM, HEAD_DIM)
        return pl.BlockSpec(shape, (lambda i: (nb - 1 - i, 0, 0, 0)) if rev else (lambda i: (i, 0, 0, 0)))

    lbspec = pl.BlockSpec((1, D), lambda i: (0, 0))
    states = jax.ShapeDtypeStruct((m // C, HEADS, HEAD_DIM, HEAD_DIM), F32)
    outs, extra = hosted_call(
        body, exch, name, (nb,),
        [zspec(ZQ, False), zspec(ZFF, False), zspec(ZI, False), zspec(ZQ, True), zspec(ZFB, True), zspec(ZI, True),
         lbspec, lbspec],
        [zspec(0, False), zspec(0, True), sspec(False), sspec(True), zspec(0, False), zspec(0, True)],
        [jax.ShapeDtypeStruct((m, D), F32), jax.ShapeDtypeStruct((m, D), F32), states, states,
         jax.ShapeDtypeStruct((m, D), F32), jax.ShapeDtypeStruct((m, D), F32)],
        [pltpu.VMEM((2, HEADS, HEAD_DIM, HEAD_DIM), F32)], [z, z, z, z, z, z, lb_f, lb_b], ("arbitrary",))
    return outs, extra


def hgrn_bwd(name, z, d_o, s_f, s_b, b_f, b_b, lb_f, lb_b, exch=None, unroll=False):
    m = z.shape[0]
    C, T = HG_CHUNK, min(HG_BLOCK_BWD, m)
    nb, cpb = m // T, T // C

    def body(zq_f, zf_f, zi_f, do_f, sf_ref, zq_b, zf_b, zi_b, do_b, sb_ref, lbf_ref, lbb_ref, bf_ref, bb_ref,
             dqf_ref, dvf_ref, dqb_ref, dvb_ref, dzf_f, dzf_b, dlbf_ref, dlbb_ref,
             dst_ref):
        b_refs = (bf_ref, bb_ref)
        @pl.when(pl.program_id(0) == 0)
        def _():
            dst_ref[...] = jnp.zeros(dst_ref.shape, F32)
            dlbf_ref[...] = jnp.zeros(dlbf_ref.shape, F32)
            dlbb_ref[...] = jnp.zeros(dlbb_ref.shape, F32)

        dirs = ((zq_f, zf_f, zi_f, do_f, sf_ref, lbf_ref, dqf_ref, dvf_ref, dzf_f, dlbf_ref),
                (zq_b, zf_b, zi_b, do_b, sb_ref, lbb_ref, dqb_ref, dvb_ref, dzf_b, dlbb_ref))

        def chunk(ci, carry):
            work = []
            for d, (zq, zf, zi, do_ref, s_ref, lb_ref, dq_ref, dv_ref, dzf_ref, dlb_ref) in enumerate(dirs):
                cc = cpb - 1 - ci if d == 0 else ci
                rows = pl.ds(pl.multiple_of(cc * C, C), C)
                lb = lb_ref[...]
                pre = _hg_prep(zq[rows, :], zf[rows, :], lb, d == 1, b=b_refs[d][rows, :])
                work.append((rows, lb, pre, zi[rows, :].astype(BF16), do_ref[rows, :],
                             [s_ref[cc, h] for h in range(HEADS)], [dst_ref[d, h] for h in range(HEADS)],
                             dlb_ref[...]))
            heads = [(d, h, slice(h * HEAD_DIM, (h + 1) * HEAD_DIM)) for d in range(2) for h in range(HEADS)]
            first = {}
            for d, h, sl in heads:
                _, _, pre, v, do, st_prevs, dsts, _ = work[d]
                dst16 = dsts[h].astype(BF16)
                first[d, h] = (_dot_nt(pre["qt"][:, sl], pre["kt"][:, sl]),
                               _dot_nt(do[:, sl], v[:, sl]),
                               _dot(do[:, sl], st_prevs[h].astype(BF16)),
                               _dot(v[:, sl], dst16),
                               _dot_nt(pre["ks"][:, sl], dst16),
                               _dot_tn(do[:, sl], pre["qe"][:, sl]))
            parts = [[[] for _ in range(6)] for _ in range(2)]
            for d, h, sl in heads:
                _, _, pre, v, do, st_prevs, dsts, _ = work[d]
                scores, dscores, dq_inter, dk_state, dv_state, dst_add = first[d, h]
                mask = _hg_mask(C, d == 1)
                a = jnp.where(mask, scores, 0.0).astype(BF16)
                da = jnp.where(mask, dscores, 0.0).astype(BF16)
                dq_p, dki_p, dks_p, dv_p, rr_p, new_dsts = parts[d]
                dq_p.append(_dot(da, pre["kt"][:, sl]) * pre["e2"][:, sl] + dq_inter * pre["e1"][:, sl])
                dki_p.append(_dot_tn(da, pre["qt"][:, sl]) * pre["e3"][:, sl])
                dks_p.append(dk_state * pre["e4"][:, sl])
                dv_p.append(_dot_tn(a, do[:, sl]) + dv_state)
                rr_p.append(pre["e_last"][:, sl] * _colsum(dsts[h] * st_prevs[h]))
                new_dsts.append(dsts[h] * pre["e_last"][:, sl] + dst_add)
            results = []
            for d, (rows, lb, pre, v, do, st_prevs, dsts, dlb_old) in enumerate(work):
                rev = d == 1
                dq_p, dki_p, dks_p, dv_p, rr_p, new_dsts = parts[d]
                dq, dki, dks, dv, rr = (jnp.concatenate(p_, axis=1) for p_ in (dq_p, dki_p, dks_p, dv_p, rr_p))
                x = pre["q"] * dq - pre["k"] * dki
                y = pre["k"] * dks
                if rev:
                    dg = _cumsum_rows(x - y) + _colsum(y) + rr
                else:
                    dg = _cumsum_rows(y - x) + (x - y) + _colsum(x) + rr
                inv_f = jnp.where(pre["f"] > TINY, 1.0 / pre["f"], 0.0)
                u = dg * inv_f - (dki + dks)
                results.append((dq, dv, (1.0 - lb) * pre["sig"] * pre["sn"] * u, dlb_old + _colsum(pre["sn"] * u),
                                new_dsts))
            for d, (zq, zf, zi, do_ref, s_ref, lb_ref, dq_ref, dv_ref, dzf_ref, dlb_ref) in enumerate(dirs):
                rows = work[d][0]
                dq, dv, dzf, dlb, new_dsts = results[d]
                dq_ref[rows, :] = dq.astype(dq_ref.dtype)
                dv_ref[rows, :] = dv.astype(dv_ref.dtype)
                dzf_ref[rows, :] = dzf.astype(dzf_ref.dtype)
                dlb_ref[...] = dlb
                for h in range(HEADS):
                    dst_ref[d, h] = new_dsts[h]
            return carry

        lax.fori_loop(0, cpb, chunk, 0, unroll=unroll)

    def rspec(cb, rev):
        return pl.BlockSpec((T, D), (lambda i: (i, cb)) if rev else (lambda i: (nb - 1 - i, cb)))

    def sspec(rev):
        shape = (cpb, HEADS, HEAD_DIM, HEAD_DIM)
        return pl.BlockSpec(shape, (lambda i: (i, 0, 0, 0)) if rev else (lambda i: (nb - 1 - i, 0, 0, 0)))

    lbspec = pl.BlockSpec((1, D), lambda i: (0, 0))
    half = jax.ShapeDtypeStruct((m, D), BF16)
    row = jax.ShapeDtypeStruct((1, D), F32)
    outs, extra = hosted_call(
        body, exch, name, (nb,),
        [rspec(ZQ, False), rspec(ZFF, False), rspec(ZI, False), rspec(0, False), sspec(False),
         rspec(ZQ, True), rspec(ZFB, True), rspec(ZI, True), rspec(0, True), sspec(True), lbspec, lbspec,
         rspec(0, False), rspec(0, True)],
        [rspec(0, False), rspec(0, False), rspec(0, True), rspec(0, True), rspec(0, False), rspec(0, True),
         lbspec, lbspec],
        [half, half, half, half, half, half, row, row],
        [pltpu.VMEM((2, HEADS, HEAD_DIM, HEAD_DIM), F32)],
        [z, z, z, d_o, s_f, z, z, z, d_o, s_b, lb_f, lb_b, b_f, b_b], ("arbitrary",))
    return outs, extra


def _heads(fn, *arrs):
    res = [fn(*[a[:, h * HEAD_DIM:(h + 1) * HEAD_DIM] for a in arrs]) for h in range(arrs[0].shape[1] // HEAD_DIM)]
    return [jnp.concatenate(parts, axis=1) for parts in zip(*res)]


def _hg_post(o_f, o_b, zg, g):
    def head(of, ob, zgh, gh):
        on, _ = _rms(of + ob)
        return (on * gh * _silu(zgh),)
    return _heads(head, o_f, o_b, zg, g)[0]


def _hg_post_bwd(da, o_f, o_b, zg, g):
    def head(dah, of, ob, zgh, gh):
        on, r = _rms(of + ob)
        sg = _silu(zgh)
        d_on = dah * sg
        return _rms_bwd(d_on, on, r, gh), dah * on * gh * _silu_grad(zgh), d_on * on
    d_o, dzg, dg = _heads(head, da, o_f, o_b, zg, g)
    return d_o, dzg, _colsum(dg)


def _sg_parts(zv, ln_g, ln_b):
    vg = _gelu(zv)
    xc = vg - _mean(vg)
    rstd = lax.rsqrt(_mean(xc * xc) + EPS)
    vh = xc * rstd
    return vh, rstd, vh * ln_g + ln_b


def _sg_lane_group(shape):
    return lax.broadcasted_iota(jnp.int32, shape, 1) < SG_GROUP_DIM


def _sg_mix(w, v16, transpose):
    rows = v16.shape[0]
    out = []
    for c in range(rows // SG_CHUNK):
        parts = []
        for j in range(SG_WIDTH // 128):
            vj = v16[c * SG_CHUNK:(c + 1) * SG_CHUNK, j * 128:(j + 1) * 128]
            w0 = w[(2 * j) * SG_CHUNK:(2 * j + 1) * SG_CHUNK]
            w1 = w[(2 * j + 1) * SG_CHUNK:(2 * j + 2) * SG_CHUNK]
            dot = _dot_tn if transpose else _dot
            parts.append(jnp.where(_sg_lane_group((SG_CHUNK, 128)), dot(w0, vj), dot(w1, vj)))
        out.append(jnp.concatenate(parts, axis=1))
    return jnp.concatenate(out, axis=0)


def _sg_fwd(zu, zv, w, bias, ln_g, ln_b):
    _, _, v = _sg_parts(zv, ln_g, ln_b)
    reps = zu.shape[0] // SG_CHUNK
    return _gelu(zu) * (_sg_mix(w, v.astype(BF16), False) + jnp.concatenate([bias] * reps, axis=0))


def _sg_bwd(db, zu, zv, w, bias, ln_g, ln_b):
    vh, rstd, v = _sg_parts(zv, ln_g, ln_b)
    v16 = v.astype(BF16)
    reps = zu.shape[0] // SG_CHUNK
    sg = _sg_mix(w, v16, False) + jnp.concatenate([bias] * reps, axis=0)
    dzu = db * sg * _gelu_grad(zu)
    dsg = db * _gelu(zu)
    dsg16 = dsg.astype(BF16)
    dv = _sg_mix(w, dsg16, True)
    low = _sg_lane_group((SG_CHUNK, 128))
    dw = []
    for g in range(SG_WIDTH // SG_GROUP_DIM):
        j, keep = g // 2, (low if g % 2 == 0 else jnp.logical_not(low))
        acc = jnp.zeros((SG_CHUNK, SG_CHUNK), F32)
        for c in range(reps):
            rows = slice(c * SG_CHUNK, (c + 1) * SG_CHUNK)
            dj = jnp.where(keep, dsg16[rows, j * 128:(j + 1) * 128], jnp.zeros((), BF16))
            acc = acc + _dot_nt(dj, v16[rows, j * 128:(j + 1) * 128])
        dw.append(acc)
    dbias = sum(dsg[c * SG_CHUNK:(c + 1) * SG_CHUNK] for c in range(reps))
    dvh = dv * ln_g
    dvg = rstd * (dvh - _mean(dvh) - vh * _mean(dvh * vh))
    dzuv = jnp.concatenate([dzu, dvg * _gelu_grad(zv)], axis=1)
    return (dzuv, jnp.concatenate(dw, axis=0), dbias, _colsum(dv * vh), _colsum(dv))


def lower_bounds(name, gamma_f, gamma_b):
    def body(gf_ref, gb_ref, lf_ref, lb_ref):
        for g_ref, o_ref in ((gf_ref, lf_ref), (gb_ref, lb_ref)):
            g0, g1 = g_ref[0:1, :], g_ref[1:2, :]
            mx = jnp.maximum(g0, g1)
            e0, e1 = jnp.exp(g0 - mx), jnp.exp(g1 - mx)
            sm0, sm1 = e0 / (e0 + e1), e1 / (e0 + e1)
            o_ref[0:1, :] = sm0 - sm0
            o_ref[1:2, :] = (sm0 + sm1) - sm0
    shp = jax.ShapeDtypeStruct(gamma_f.shape, F32)
    return pl.pallas_call(body, name=name, out_shape=[shp, shp])(gamma_f, gamma_b)


def lower_bounds_bwd(name, gamma_f, gamma_b, dlb_f, dlb_b):
    def body(gf_ref, gb_ref, df_ref, db_ref, of_ref, ob_ref):
        for g_ref, d_ref, o_ref in ((gf_ref, df_ref, of_ref), (gb_ref, db_ref, ob_ref)):
            g0, g1 = g_ref[0:1, :], g_ref[1:2, :]
            mx = jnp.maximum(g0, g1)
            e0, e1 = jnp.exp(g0 - mx), jnp.exp(g1 - mx)
            sm0, sm1 = e0 / (e0 + e1), e1 / (e0 + e1)
            d1 = d_ref[1:2, :] * sm0 * sm1
            o_ref[0:1, :] = -d1
            o_ref[1:2, :] = d1
    shp = jax.ShapeDtypeStruct(gamma_f.shape, F32)
    return pl.pallas_call(body, name=name, out_shape=[shp, shp])(gamma_f, gamma_b, dlb_f, dlb_b)


def _row(a, l):
    return a[l:l + 1]


class LocalPlan:
    def __init__(self, weights):
        self.W = weights
        self.grads = [dict() for _ in range(DEPTH)]

    def exch(self, host):
        return None

    def done(self, host, outs):
        pass

    def early_small(self, packed):
        pass


def local_step(x, p, target, S, plan):
    m = x.shape[0]

    def hmm(tag, *args, **kw):
        ex = plan.exch(tag)
        res = mm(tag, *args, exch=ex, **kw)
        if ex is None:
            return res
        plan.done(tag, res[1])
        return res[0]

    lb_f, lb_b = lower_bounds("lower_bounds", S["lb_gamma_fwd"], S["lb_gamma_bwd"])
    saved = []
    for l in range(DEPTH):
        t = f"l{l}_"
        W = plan.W[l]
        tm = 2048
        in_tile = (1024, 2048)
        ffn_tile = (2048, 2048)
        g_pre, g_post = _row(S["norm_mix_pre"], l), _row(S["norm_mix_post"], l)
        g_fpre, g_fpost = _row(S["norm_ffn_pre"], l), _row(S["norm_ffn_post"], l)
        hg_g = _row(S["hg_norm"], l)
        sg_w = S["sg_w"][l].reshape(SG_WIDTH // SG_GROUP_DIM * SG_CHUNK, SG_CHUNK).astype(BF16)
        sg_bias = jnp.repeat(S["sg_b"][l].T, SG_GROUP_DIM, axis=1)
        ln_g, ln_b = _row(S["sg_ln_g"], l), _row(S["sg_ln_b"], l)
        lbf, lbb = _row(lb_f, l), _row(lb_b, l)

        if l == 0:
            (h,) = rowwise(t + "pre_norm", lambda xv, g: (_rms(xv)[0] * g,), m, ins=[(x, D, 0)], consts=[g_pre],
                           outs=[(D, BF16)])
        z = hmm(t + "in_proj", h, W["w_in"], "nn", tm=in_tile[0], tn=in_tile[1])
        (o_f, o_b, s_f, s_b, b_f, b_b), extra = hgrn_fwd(t + "hgrn_fwd", z, lbf, lbb, exch=plan.exch(t + "hgrn_fwd"))
        plan.done(t + "hgrn_fwd", extra)
        (b_out,) = rowwise(t + "sgu_fwd", _sg_fwd, m, ins=[(z, SG_WIDTH, ZU), (z, SG_WIDTH, ZV)],
                           consts=[sg_w, sg_bias, ln_g, ln_b], outs=[(SG_WIDTH, BF16)])

        def post_pro(of, ob, zg, g):
            ao = _hg_post(of, ob, zg, g).astype(BF16)
            return [ao], [ao]
        a_out, pa = mm_fused(t + "proj_a", m, [(o_f, 0), (o_b, 0), (z, ZG)], [(W["w_a"], 0)], "nn", D,
                             prologue=post_pro, a_outs=[BF16], a_consts=[hg_g], epilogue=lambda tot: (tot,),
                             outs=[(D, BF16)])
        pb = mm(t + "proj_b", b_out, W["w_b"], "nn", BF16)

        def merge_pro(a, b, ga, gb):
            mg = (_sigmoid(ga) * a + _sigmoid(gb) * b).astype(BF16)
            return [mg], [mg]

        def post_pre(mixv, xv, gp, gf):
            x1 = xv + _rms(mixv)[0] * gp
            return mixv, x1, _rms(x1)[0] * gf
        merged, mix, x1, h2 = mm_fused(
            t + "out_proj", m, [(pa, 0), (pb, 0), (z, GA), (z, GB)], [(W["w_out"], 0)], "nn", D, prologue=merge_pro,
            a_outs=[BF16], e_ins=[(x, D, 0)], consts=[g_post, g_fpre], epilogue=post_pre,
            outs=[(D, F32), (D, F32), (D, BF16)])
        gu = hmm(t + "ffn_in", h2, W["w_gu"], "nn", BF16, tm=ffn_tile[0], tn=ffn_tile[1])

        def act_pro(gt, up):
            hd = (_silu(gt.astype(F32)) * up).astype(BF16)
            return [hd], [hd]
        hid, ff, x2 = mm_fused(
            t + "ffn_out", m, [(gu, 0), (gu, FFN_PAD // 1024)], [(W["w_down"], 0)], "nn", FFN_PAD, prologue=act_pro,
            a_outs=[BF16], e_ins=[(x1, D, 0)], consts=[g_fpost],
            epilogue=lambda f, xv, g: (f, xv + _rms(f)[0] * g), outs=[(D, F32), (D, F32)])
        e = mm(t + "ple_proj", (p, l), W["w_ple"], "nn")

        if l + 1 < DEPTH:
            def ple_add(tv, xv, ev, g):
                x3 = xv + ev * _sigmoid(tv)
                return tv, x3, _rms(x3)[0] * g
            tg, x3, h_next = mm_fused(
                t + "ple_gate", m, [(x2, 0)], [(W["w_ple_gate"], 0)], "nn", D, a_to_epilogue=(0,), e_ins=[(e, D, 0)],
                consts=[_row(S["norm_mix_pre"], l + 1)], epilogue=ple_add, outs=[(D, F32), (D, F32), (D, BF16)])
        else:
            def ple_loss(tv, xv, ev, tgt):
                err = xv + ev * _sigmoid(tv) - tgt
                return tv, err * (1.0 / D), _colsum(err * err)
            tg, x3, loss_cols = mm_fused(
                t + "ple_gate", m, [(x2, 0)], [(W["w_ple_gate"], 0)], "nn", D, a_to_epilogue=(0,),
                e_ins=[(e, D, 0), (target, D, 0)], epilogue=ple_loss, outs=[(D, F32), (D, F32)], accs=[(1, D)])
            h_next = None
        saved.append(dict(x=x, h=h, z=z, o_f=o_f, o_b=o_b, s_f=s_f, s_b=s_b, b_f=b_f, b_b=b_b, a_out=a_out,
                          b_out=b_out, pa=pa, pb=pb,
                          merged=merged, mix=mix, x1=x1, h2=h2, gu=gu, hid=hid, ff=ff, x2=x2, e=e, tg=tg,
                          sg_w=sg_w, sg_bias=sg_bias))
        x, h = x3, h_next

    dx = x

    gs = {n: [None] * DEPTH for n in SMALL}
    dlb_f, dlb_b = [None] * DEPTH, [None] * DEPTH

    for l in reversed(range(DEPTH)):
        t = f"l{l}_bwd_"
        sv, W = saved[l], plan.W[l]
        tm, tk = 2048, 4096
        g_pre, g_post = _row(S["norm_mix_pre"], l), _row(S["norm_mix_post"], l)
        g_fpre, g_fpost = _row(S["norm_ffn_pre"], l), _row(S["norm_ffn_post"], l)
        hg_g = _row(S["hg_norm"], l)
        ln_g, ln_b = _row(S["sg_ln_g"], l), _row(S["sg_ln_b"], l)
        lbf, lbb = _row(lb_f, l), _row(lb_b, l)

        def wgrad(nm, tag, a, b):
            a_dtype = (a[0] if isinstance(a, tuple) else a).dtype
            plan.grads[l][nm] = mm(tag, a, b, "tn", BF16, tk=tk if a_dtype == BF16 else 2048)

        def ple_pro(d3, ev, tv):
            s = _sigmoid(tv)
            de_, dt_ = (d3 * s).astype(BF16), (d3 * ev * s * (1.0 - s)).astype(BF16)
            return [dt_], [dt_, de_]

        def ffn_post_bwd(d2p, d3, f, g):
            d2 = d3 + d2p
            fh, r = _rms(f)
            return d2, _rms_bwd(d2, fh, r, g), _colsum(d2 * fh)
        dt, de, dx2, dff, gs["norm_ffn_post"][l] = mm_fused(
            t + "ple_gate_dx", m, [(dx, 0), (sv["e"], 0), (sv["tg"], 0)], [(W["w_ple_gate"], 0)], "nt", D,
            prologue=ple_pro, a_outs=[BF16, BF16], a_to_epilogue=(0,), e_ins=[(sv["ff"], D, 0)], consts=[g_fpost],
            epilogue=ffn_post_bwd, outs=[(D, F32), (D, BF16)], accs=[(1, D)])
        wgrad("w_ple", t + "w_ple", (p, l), de)
        wgrad("w_ple_gate", t + "w_ple_gate", sv["x2"], dt)
        wgrad("w_down", t + "w_down", sv["hid"], dff)
        dhid = mm(t + "ffn_out_dx", dff, W["w_down"], "nt", BF16, tm=tm)

        def act_bwd(dh, gt, up):
            dh, gt = dh.astype(F32), gt.astype(F32)
            dg_, du_ = (dh * up * _silu_grad(gt)).astype(BF16), (dh * _silu(gt)).astype(BF16)
            return [dg_, du_], [dg_, du_]

        def pre_post_bwd(dh, d2, x1v, mixv, gf, gp):
            xh, r1 = _rms(x1v)
            d1 = d2 + _rms_bwd(dh, xh, r1, gf)
            mh, rm = _rms(mixv)
            return d1, _rms_bwd(d1, mh, rm, gp), _colsum(dh * xh), _colsum(d1 * mh)
        off = FFN_PAD // 1024
        dgate, dup, dx1, dmix, gs["norm_ffn_pre"][l], gs["norm_mix_post"][l] = mm_fused(
            t + "ffn_in_dx", m, [(dhid, 0), (sv["gu"], 0), (sv["gu"], off)], [(W["w_gu"], 0), (W["w_gu"], off)], "nt",
            FFN_PAD, prologue=act_bwd, a_outs=[BF16, BF16], e_ins=[(dx2, D, 0), (sv["x1"], D, 0), (sv["mix"], D, 0)],
            consts=[g_fpre, g_post], epilogue=pre_post_bwd, outs=[(D, F32), (D, BF16)], accs=[(1, D), (1, D)])
        wgrad("w_gate", t + "w_gate", sv["h2"], dgate)
        wgrad("w_up", t + "w_up", sv["h2"], dup)
        wgrad("w_out", t + "w_out", sv["merged"], dmix)

        def merge_bwd(dm, a, b, gab):
            sa, sb = _sigmoid(gab[:, :D]), _sigmoid(gab[:, D:])
            dgab = jnp.concatenate([dm * a * sa * (1.0 - sa), dm * b * sb * (1.0 - sb)], axis=1)
            return dm * sa, dm * sb, dgab
        dpa, dpb, dz = mm_fused(
            t + "out_proj_dx", m, [(dmix, 0)], [(W["w_out"], 0)], "nt", D,
            e_ins=[(sv["pa"], D, 0), (sv["pb"], D, 0), (sv["z"], 2 * D, 3)], epilogue=merge_bwd,
            outs=[(D, BF16), (D, BF16)], alias_outs=[(jax.ShapeDtypeStruct((m, N_IN), BF16), 2 * D, 3)])
        wgrad("w_a", t + "w_a", sv["a_out"], dpa)
        wgrad("w_b", t + "w_b", sv["b_out"], dpb)
        db = mm(t + "proj_b_dx", dpb, W["w_b"], "nt")

        dz, dsw, dbias, gs["sg_ln_g"][l], gs["sg_ln_b"][l] = rowwise(
            t + "sgu", _sg_bwd, m, ins=[(db, SG_WIDTH, 0), (sv["z"], SG_WIDTH, ZU), (sv["z"], SG_WIDTH, ZV)],
            consts=[sv["sg_w"], sv["sg_bias"], ln_g, ln_b], alias_outs=[(dz, 2 * SG_WIDTH, 5)],
            accs=[(SG_WIDTH // SG_GROUP_DIM * SG_CHUNK, SG_CHUNK), (SG_CHUNK, SG_WIDTH), (1, SG_WIDTH), (1, SG_WIDTH)])
        gs["sg_w"][l] = dsw.reshape(1, SG_WIDTH // SG_GROUP_DIM, SG_CHUNK, SG_CHUNK)
        gs["sg_b"][l] = dbias.reshape(SG_CHUNK, SG_WIDTH // SG_GROUP_DIM, SG_GROUP_DIM).sum(-1).T[None]

        d_o, dz, gs["hg_norm"][l] = mm_fused(
            t + "proj_a_dx", m, [(dpa, 0)], [(W["w_a"], 0)], "nt", D,
            e_ins=[(sv["o_f"], D, 0), (sv["o_b"], D, 0), (sv["z"], D, ZG)], consts=[hg_g], epilogue=_hg_post_bwd,
            outs=[(D, BF16)], alias_outs=[(dz, D, ZG)], accs=[(1, D)], tm=256)
        if l == 0:
            part = {n: (g if not isinstance(g, list) else jnp.concatenate(
                [jnp.zeros((1,) + g[1].shape[1:], F32) if gl is None else gl for gl in g], axis=0))
                for n, g in gs.items()}
            plan.early_small(_pack([part[n].reshape(S[n].shape) for n in SMALL]))
        (dq_f, dv_f, dq_b, dv_b, dzf_f, dzf_b, dlb_f[l], dlb_b[l]), extra = hgrn_bwd(
            t + "hgrn", sv["z"], d_o, sv["s_f"], sv["s_b"], sv["b_f"], sv["b_b"], lbf, lbb,
            exch=plan.exch(t + "hgrn"))
        plan.done(t + "hgrn", extra)

        def combine(dqf, dqb, dvf, dvb, dff_, dfb_, zq):
            dq = dqf.astype(F32) + dqb.astype(F32)
            dv = dvf.astype(F32) + dvb.astype(F32)
            return (jnp.concatenate([(dq * _silu_grad(zq)).astype(BF16), dff_, dfb_, dv.astype(BF16)], axis=1),)
        (dz,) = rowwise(t + "hgrn_combine", combine, m,
                        ins=[(dq_f, D, 0), (dq_b, D, 0), (dv_f, D, 0), (dv_b, D, 0), (dzf_f, D, 0), (dzf_b, D, 0),
                             (sv["z"], D, ZQ)], alias_outs=[(dz, 4 * D, 0)], tm=128)
        wgrad("w_in", t + "w_in", sv["h"], dz)

        def pre_bwd(dhv, d1, xv, g):
            xh, r = _rms(xv)
            return d1 + _rms_bwd(dhv, xh, r, g), _colsum(dhv * xh)
        ex = plan.exch(t + "in_proj_dx")
        res = mm_fused(t + "in_proj_dx", m, [(dz, 0)], [(W["w_in"], 0)], "nt", N_IN,
                       e_ins=[(dx1, D, 0), (sv["x"], D, 0)], consts=[g_pre], epilogue=pre_bwd, outs=[(D, F32)],
                       accs=[(1, D)], tm=1024, exch=ex)
        if ex is not None:
            res, extra = res
            plan.done(t + "in_proj_dx", extra)
        dx, gs["norm_mix_pre"][l] = res
        saved[l] = None
        if l == DEPTH - 1:
            none = jnp.zeros((1, D), F32)
            gs["lb_gamma_fwd"], gs["lb_gamma_bwd"] = lower_bounds_bwd(
                "lower_bounds_bwd", S["lb_gamma_fwd"], S["lb_gamma_bwd"], jnp.concatenate([none, dlb_f[l]], axis=0),
                jnp.concatenate([none, dlb_b[l]], axis=0))

    small ={n: (g if not isinstance(g, list) else jnp.concatenate(g, axis=0)).reshape(S[n].shape)
             for n, g in gs.items()}
    return loss_cols, dx, small


def cast_pad(name, w, rows_p, cols_p):
    _, r, c = w.shape

    def body(w_ref, o_ref):
        if (rows_p, cols_p) != (r, c):
            o_ref[...] = jnp.zeros(o_ref.shape, BF16)
        o_ref[0:r, 0:c] = w_ref[...].astype(BF16)

    return pl.pallas_call(
        body, name=name, grid=(DEPTH,), in_specs=[pl.BlockSpec((None, r, c), lambda l: (l, 0, 0))],
        out_specs=pl.BlockSpec((None, rows_p, cols_p), lambda l: (l, 0, 0)),
        out_shape=jax.ShapeDtypeStruct((DEPTH, rows_p, cols_p), BF16), compiler_params=_params(("parallel",)),
    )(w)


def _shard_shape(n, shape):
    axis, size, _, _ = LAYOUT[n]
    _, r, c = shape
    return (size, c) if axis == 0 else (r, size)


class DistPlan:
    def __init__(self, shards):
        self.shards = shards
        self.W = [dict() for _ in range(DEPTH)]
        self.grads = [dict() for _ in range(DEPTH)]
        self.slots = [dict() for _ in range(DEPTH)]
        rest = [n for n in BIG if n != "w_in"]
        ffn = ["w_gate", "w_up", "w_down"]
        self.schedule = {
            "l0_in_proj": ("gather", [(0, n) for n in rest]),
            "l0_hgrn_fwd": ("gather", [(1, n) for n in BIG if n not in ffn]),
            "l0_ffn_in": ("gather", [(1, n) for n in ffn]),
            "l1_bwd_hgrn": ("scatter", [(1, n) for n in rest]),
            "l1_bwd_in_proj_dx": ("scatter", [(1, "w_in")]),
            "l0_bwd_hgrn": ("scatter", [(0, n) for n in rest]),
            "l0_bwd_in_proj_dx": ("scatter", [(0, "w_in")]),
        }
        self.pending = {}
        self.small_part = self.small_slots = None
        axis, size, dst, _ = LAYOUT["w_in"]
        self.W[0][dst] = gather_two_level("gather_l0_w_in", shards["w_in"], 0, axis, size, GATHERED[dst])

    def _gather(self, host, parts):
        srcs, dsts, items, keys = [], [], [], []
        for layer, n in parts:
            axis, size, dst, base = LAYOUT[n]
            if (layer, dst) not in keys:
                keys.append((layer, dst))
                dsts.append((GATHERED[dst], BF16))
            srcs.append(self.shards[n])
            items.append(("gather", len(srcs) - 1, keys.index((layer, dst)), axis, size, base, layer))
        self.pending[host] = ("gather", keys)
        return Exchange(srcs, dsts, items)

    def _scatter(self, host, parts):
        srcs, dsts, items = [], [], []
        for layer, n in parts:
            axis, size, _, _ = LAYOUT[n]
            srcs.append(self.grads[layer][n])
            dsts.append(((NDEV,) + _shard_shape(n, self.shards[n].shape), BF16))
            items.append(("scatter", len(srcs) - 1, len(dsts) - 1, axis, size, 0, None))
        keys = list(parts)
        if host == "l0_bwd_hgrn" and self.small_part is not None:
            srcs.append(self.small_part)
            dsts.append(((NDEV,) + self.small_part.shape, F32))
            items.append(("copies", len(srcs) - 1, len(dsts) - 1, 0, 0, 0, None))
            keys.append(("small", None))
        self.pending[host] = ("scatter", keys)
        return Exchange(srcs, dsts, items)

    def early_small(self, packed):
        self.small_part = packed

    def exch(self, host):
        if host not in self.schedule:
            return None
        kind, parts = self.schedule[host]
        return self._gather(host, parts) if kind == "gather" else self._scatter(host, parts)

    def done(self, host, outs):
        if host not in self.pending:
            return
        kind, keys = self.pending.pop(host)
        for (layer, n), arr in zip(keys, outs):
            if layer == "small":
                self.small_slots = arr
            else:
                (self.W if kind == "gather" else self.slots)[layer][n] = arr


def adam(name, w, m_, v_, tr, g=None, slots=None):
    L, r, c = w.shape
    assert r % tr == 0
    nt = r // tr
    n_s = 0 if slots is None else L

    def body(*refs):
        s_refs = refs[:n_s]
        g_ref = refs[n_s] if g is not None else None
        w_ref, m_ref, v_ref, g_out, d_out, m_out, v_out = refs[n_s + (g is not None):]

        def update(gv):
            if g_ref is not None:
                gv = gv + g_ref[...] if gv is not None else g_ref[...]
            m2 = B1 * m_ref[...] + (1.0 - B1) * gv
            v2 = B2 * v_ref[...] + (1.0 - B2) * (gv * gv)
            m_hat = m2 / (1.0 - B1 ** STEP)
            v_hat = v2 / (1.0 - B2 ** STEP)
            g_out[...] = gv
            d_out[...] = -LR * (m_hat / (jnp.sqrt(v_hat) + AEPS) + WD * w_ref[...])
            m_out[...] = m2
            v_out[...] = v2

        if slots is None:
            update(None)
            return
        for layer, s_ref in enumerate(s_refs):
            @pl.when(pl.program_id(0) == layer)
            def _():
                gv = s_ref[0][:, :c].astype(F32)
                for j in range(1, NDEV):
                    gv = gv + s_ref[j][:, :c].astype(F32)
                update(gv)

    spec = pl.BlockSpec((None, tr, c), lambda l, i: (l, i, 0))
    arrs, specs = [], []
    if slots is not None:
        assert len(slots) == L and L <= 2
        arrs = list(slots)
        cp = slots[0].shape[2]
        specs = [pl.BlockSpec((NDEV, tr, cp), lambda l, i: (0, i * (1 - l) + (nt - 1) * l, 0)),
                 pl.BlockSpec((NDEV, tr, cp), lambda l, i: (0, i * l, 0))][:L]
    if g is not None:
        arrs.append(g)
        specs.append(spec)
    shp = jax.ShapeDtypeStruct(w.shape, F32)
    return pl.pallas_call(
        body, name=name, grid=(L, nt), in_specs=specs + [spec, spec, spec], out_specs=[spec] * 4,
        out_shape=[shp] * 4, compiler_params=_params(("arbitrary", "arbitrary")),
    )(*arrs, w, m_, v_)


def _pack(arrs):
    parts = []
    for a in arrs:
        a2 = a.reshape(-1, D)
        parts.append(jnp.pad(a2, ((0, -a2.shape[0] % 8), (0, 0))))
    return jnp.concatenate(parts, axis=0)


def _unpack(buf, shapes):
    out, off = [], 0
    for s in shapes:
        rows = 1
        for d_ in s:
            rows *= d_
        rows //= D
        out.append(buf[off:off + rows].reshape(s))
        off += rows + (-rows % 8)
    return out


def kernel(x, p, norm_mix_pre, w_in, lb_gamma_fwd, lb_gamma_bwd, hg_norm, sg_w, sg_b, sg_ln_g, sg_ln_b, w_a, w_b, w_out, norm_mix_post, norm_ffn_pre, w_gate, w_up, w_down, norm_ffn_post, w_ple, w_ple_gate, loss_target, m_norm_mix_pre, m_w_in, m_lb_gamma_fwd, m_lb_gamma_bwd, m_hg_norm, m_sg_w, m_sg_b, m_sg_ln_g, m_sg_ln_b, m_w_a, m_w_b, m_w_out, m_norm_mix_post, m_norm_ffn_pre, m_w_gate, m_w_up, m_w_down, m_norm_ffn_post, m_w_ple, m_w_ple_gate, v_norm_mix_pre, v_w_in, v_lb_gamma_fwd, v_lb_gamma_bwd, v_hg_norm, v_sg_w, v_sg_b, v_sg_ln_g, v_sg_ln_b, v_w_a, v_w_b, v_w_out, v_norm_mix_post, v_norm_ffn_pre, v_w_gate, v_w_up, v_w_down, v_norm_ffn_post, v_w_ple, v_w_ple_gate):
    a = dict(zip(INPUTS, (x, p, norm_mix_pre, w_in, lb_gamma_fwd, lb_gamma_bwd, hg_norm, sg_w, sg_b, sg_ln_g, sg_ln_b, w_a, w_b, w_out, norm_mix_post, norm_ffn_pre, w_gate, w_up, w_down, norm_ffn_post, w_ple, w_ple_gate, loss_target, m_norm_mix_pre, m_w_in, m_lb_gamma_fwd, m_lb_gamma_bwd, m_hg_norm, m_sg_w, m_sg_b, m_sg_ln_g, m_sg_ln_b, m_w_a, m_w_b, m_w_out, m_norm_mix_post, m_norm_ffn_pre, m_w_gate, m_w_up, m_w_down, m_norm_ffn_post, m_w_ple, m_w_ple_gate, v_norm_mix_pre, v_w_in, v_lb_gamma_fwd, v_lb_gamma_bwd, v_hg_norm, v_sg_w, v_sg_b, v_sg_ln_g, v_sg_ln_b, v_w_a, v_w_b, v_w_out, v_norm_mix_post, v_norm_ffn_pre, v_w_gate, v_w_up, v_w_down, v_norm_ffn_post, v_w_ple, v_w_ple_gate)))
    m = x.shape[1]

    shards = {n: cast_pad("cast_" + n, a[n], *_shard_shape(n, a[n].shape)) for n in BIG}
    plan = DistPlan(shards)
    loss_cols, dx, gs = local_step(x[0], p[:, 0], loss_target[0], {n: a[n] for n in SMALL}, plan)
    loss = lax.psum(jnp.sum(loss_cols) * (0.5 / D), ("x", "y", "c"))

    small_shapes = [a[n].shape for n in SMALL]
    rows = plan.small_slots.shape[1]
    late = allreduce_small("allreduce_small", jnp.pad(gs["norm_mix_pre"][0:1], ((0, 7), (0, 0))))
    g_late = jnp.pad(late, ((0, rows - 8), (0, 0)))[None]

    res = {}
    row_tiles = {"w_in": 128, "w_a": 128, "w_b": 512, "w_out": 128, "w_gate": 128, "w_up": 128, "w_down": 88,
                 "w_ple": 256, "w_ple_gate": 128}
    for n in BIG:
        res[n] = adam("adam_" + n, a[n], a["m_" + n], a["v_" + n], row_tiles[n],
                      slots=[plan.slots[l][n] for l in range(DEPTH)])
    packed = [_pack([a[pre + n] for n in SMALL])[None] for pre in ("", "m_", "v_")]
    small_res = adam("adam_small", packed[0], packed[1], packed[2], rows // 2, g=g_late, slots=[plan.small_slots])
    small_res = [_unpack(r_[0], small_shapes) for r_ in small_res]
    for i, n in enumerate(SMALL):
        res[n] = tuple(small_res[k][i] for k in range(4))

    outs = [loss, dx.reshape(1, m, D)]
    for k in range(4):
        outs += [res[n][k] for n in WEIGHTS]
    return tuple(outs)
```

```python
import jax
import jax.numpy as jnp
from jax import lax
from jax.experimental import pallas as pl
from jax.experimental.pallas import tpu as pltpu

F32 = jnp.float32
BF16 = jnp.bfloat16

D = 1024
N_IN = 8192
HEADS = 8
HEAD_DIM = 128
SG_CHUNK = 128
SG_WIDTH = 512
SG_GROUP_DIM = 64
FFN = 2816
PLE_DIM = 256
EPS = 1e-6
DEPTH = 2
ZQ, ZFF, ZFB, ZI, ZG, GA, GB = 0, 1, 2, 3, 4, 6, 7
ZU, ZV = 10, 11

NDEV = 8
FFN_SHARD = FFN // NDEV
FFN_SHARD_PAD = 384
FFN_PAD = NDEV * FFN_SHARD_PAD

LR, B1, B2, AEPS, WD, STEP = 0.001, 0.9, 0.999, 1e-08, 0.01, 10

ROW_TILE = 256
HG_CHUNK = 64
HG_BLOCK_FWD = 256
HG_BLOCK_BWD = 256
EXP_CLAMP = 80.0
PROLOGUE_CHUNK = 256
TINY = float(jnp.finfo(jnp.float32).tiny)
VMEM_LIMIT = 56 * 1024 * 1024

BIG = ["w_in", "w_a", "w_b", "w_out", "w_gate", "w_up", "w_down", "w_ple", "w_ple_gate"]
SMALL = ["norm_mix_pre", "lb_gamma_fwd", "lb_gamma_bwd", "hg_norm", "sg_w", "sg_b", "sg_ln_g", "sg_ln_b",
         "norm_mix_post", "norm_ffn_pre", "norm_ffn_post"]
WEIGHTS = ["norm_mix_pre", "w_in", "lb_gamma_fwd", "lb_gamma_bwd", "hg_norm", "sg_w", "sg_b", "sg_ln_g", "sg_ln_b",
           "w_a", "w_b", "w_out", "norm_mix_post", "norm_ffn_pre", "w_gate", "w_up", "w_down", "norm_ffn_post",
           "w_ple", "w_ple_gate"]
INPUTS = (["x", "p"] + WEIGHTS + ["loss_target"] + ["m_" + n for n in WEIGHTS] + ["v_" + n for n in WEIGHTS])
LAYOUT = {
    "w_in": (1, 1024, "w_in", 0), "w_a": (0, 128, "w_a", 0), "w_b": (1, 128, "w_b", 0),
    "w_out": (0, 128, "w_out", 0), "w_gate": (1, FFN_SHARD_PAD, "w_gu", 0),
    "w_up": (1, FFN_SHARD_PAD, "w_gu", FFN_PAD), "w_down": (0, FFN_SHARD_PAD, "w_down", 0),
    "w_ple": (1, 128, "w_ple", 0), "w_ple_gate": (0, 128, "w_ple_gate", 0),
}
GATHERED = {"w_in": (D, N_IN), "w_a": (D, D), "w_b": (SG_WIDTH, D), "w_out": (D, D), "w_gu": (D, 2 * FFN_PAD),
            "w_down": (FFN_PAD, D), "w_ple": (PLE_DIM, D), "w_ple_gate": (D, D)}


def _params(sem):
    return pltpu.CompilerParams(dimension_semantics=sem, vmem_limit_bytes=VMEM_LIMIT)


def _dot(a, b):
    return lax.dot_general(a, b, (((1,), (0,)), ((), ())), preferred_element_type=F32)


def _dot_nt(a, b):
    return lax.dot_general(a, b, (((1,), (1,)), ((), ())), preferred_element_type=F32)


def _dot_tn(a, b):
    return lax.dot_general(a, b, (((0,), (0,)), ((), ())), preferred_element_type=F32)


def _sigmoid(x):
    return jax.nn.sigmoid(x)


def _silu(x):
    return x * _sigmoid(x)


def _silu_grad(x):
    s = _sigmoid(x)
    return s * (1.0 + x * (1.0 - s))


def _gelu(x):
    return 0.5 * x * (1.0 + lax.erf(x * 0.7071067811865476))


def _gelu_grad(x):
    return 0.5 * (1.0 + lax.erf(x * 0.7071067811865476)) + x * jnp.exp(-0.5 * x * x) * 0.3989422804014327


def _mean(x):
    return jnp.mean(x, axis=-1, keepdims=True)


def _colsum(x):
    return jnp.sum(x, axis=0, keepdims=True)


def _rms(x):
    r = lax.rsqrt(_mean(x * x) + EPS)
    return x * r, r


def _rms_bwd(dy, xh, r, g):
    dyg = dy * g
    return r * (dyg - xh * _mean(dyg * xh))


MESH = pl.DeviceIdType.MESH
ANY = pl.BlockSpec(memory_space=pl.ANY)


def _slab(ref, axis, start, size):
    idx = [slice(None)] * 2
    idx[axis] = pl.ds(start, size)
    return ref.at[tuple(idx)]


class Exchange:
    def __init__(self, srcs, dsts, items):
        self.srcs, self.dsts, self.items = list(srcs), list(dsts), list(items)

    def specs(self):
        n = len(self.items)
        sems = [pltpu.SemaphoreType.DMA((n * (NDEV - 1),)), pltpu.SemaphoreType.DMA((n * (NDEV - 1),)),
                pltpu.SemaphoreType.DMA((n,))]
        return ([ANY] * len(self.srcs), [ANY] * len(self.dsts),
                [jax.ShapeDtypeStruct(s, dt) for (s, dt) in self.dsts], sems)

    def copies(self, src, dst, send_sem, recv_sem, loc_sem):
        x, y, c = lax.axis_index("x"), lax.axis_index("y"), lax.axis_index("c")
        me = 4 * x + 2 * y + c
        starts, waits = [], []
        for n, (kind, si, di, axis, size, base, layer) in enumerate(self.items):
            def views(to_dev, from_dev):
                if kind == "gather":
                    return (src[si].at[layer],
                            _slab(dst[di], axis, base + pl.multiple_of(from_dev * size, 128), size))
                if kind == "copies":
                    return src[si], dst[di].at[from_dev]
                return _slab(src[si], axis, base + pl.multiple_of(to_dev * size, 128), size), dst[di].at[from_dev]

            s_own, d_own = views(me, me)
            own = pltpu.make_async_copy(s_own, d_own, loc_sem.at[n])
            starts.append(own)
            waits.append(own)
            for k in range(1, NDEV):
                px = 1 - x if k & 4 else x
                py = 1 - y if k & 2 else y
                pc = 1 - c if k & 1 else c
                peer = 4 * px + 2 * py + pc
                s_out, _ = views(peer, me)
                _, d_in = views(me, peer)
                sem = n * (NDEV - 1) + k - 1
                starts.append(pltpu.make_async_remote_copy(s_out, d_own, send_sem.at[sem], recv_sem.at[sem],
                                                           device_id=(px, py, pc), device_id_type=MESH))
                waits.append(pltpu.make_async_remote_copy(s_out, d_in, send_sem.at[sem], recv_sem.at[sem],
                                                          device_id=(px, py, pc), device_id_type=MESH))
        return starts, waits


def exchange(name, exch):
    e_in, e_out, e_shape, e_scr = exch.specs()
    ns, nd = len(e_in), len(e_out)

    def body(*refs):
        starts, waits = exch.copies(refs[:ns], refs[ns:ns + nd], *refs[ns + nd:])
        for cp in starts:
            cp.start()
        for cp in waits:
            cp.wait()

    return pl.pallas_call(body, name=name, in_specs=e_in, out_specs=e_out, out_shape=e_shape, scratch_shapes=e_scr,
                          compiler_params=pltpu.CompilerParams(has_side_effects=True))(*exch.srcs)


def gather_two_level(name, shards, layer, axis, size, full_shape):
    def body(src, dst, send_sem, recv_sem, loc_sem):
        x, y, c = lax.axis_index("x"), lax.axis_index("y"), lax.axis_index("c")
        mine = src.at[layer]
        chips = [(1 - x, y), (x, 1 - y), (1 - x, 1 - y)]

        def slab(px, py, pc):
            return _slab(dst, axis, pl.multiple_of((4 * px + 2 * py + pc) * size, 128), size)

        def copy(k, from_ref, block, to):
            return pltpu.make_async_remote_copy(from_ref, slab(*block), send_sem.at[k], recv_sem.at[k], device_id=to,
                                                device_id_type=MESH)

        own = pltpu.make_async_copy(mine, slab(x, y, c), loc_sem)
        own.start()
        first = [copy(0, mine, (x, y, c), (x, y, 1 - c))]
        first += [copy(1 + j, mine, (x, y, c), (*chip, c)) for j, chip in enumerate(chips)]
        for cp in first:
            cp.start()
        passed = []
        for j, chip in enumerate(chips):
            copy(1 + j, mine, (*chip, c), (x, y, c)).wait_recv()
            fwd = copy(4 + j, slab(*chip, c), (*chip, c), (x, y, 1 - c))
            fwd.start()
            passed.append(fwd)
        copy(0, mine, (x, y, 1 - c), (x, y, c)).wait_recv()
        for j, chip in enumerate(chips):
            copy(4 + j, mine, (*chip, 1 - c), (x, y, c)).wait_recv()
        for cp in first + passed:
            cp.wait_send()
        own.wait()

    return pl.pallas_call(
        body, name=name, in_specs=[ANY], out_specs=ANY, out_shape=jax.ShapeDtypeStruct(full_shape, shards.dtype),
        scratch_shapes=[pltpu.SemaphoreType.DMA((NDEV - 1,)), pltpu.SemaphoreType.DMA((NDEV - 1,)),
                        pltpu.SemaphoreType.DMA(())],
        compiler_params=pltpu.CompilerParams(has_side_effects=True))(shards)


def hosted_call(body, exch, name, grid, in_specs, out_specs, out_shape, scratch_shapes, operands, semantics,
                aliases=None):
    aliases = aliases or {}
    if exch is None:
        res = pl.pallas_call(body, name=name, grid=grid, in_specs=in_specs, out_specs=out_specs, out_shape=out_shape,
                             scratch_shapes=scratch_shapes, input_output_aliases=aliases,
                             compiler_params=_params(semantics))(*operands)
        return list(res), []
    n_in, n_out, n_scr = len(in_specs), len(out_specs), len(scratch_shapes)
    e_in, e_out, e_shape, e_scr = exch.specs()
    ns, nd = len(e_in), len(e_out)

    def at_step(last):
        cond = None
        for ax, n in enumerate(grid):
            c = pl.program_id(ax) == (n - 1 if last else 0)
            cond = c if cond is None else jnp.logical_and(cond, c)
        return cond

    def wrapped(*refs):
        ins, src = refs[:n_in], refs[n_in:n_in + ns]
        o0 = n_in + ns
        outs, dst = refs[o0:o0 + n_out], refs[o0 + n_out:o0 + n_out + nd]
        s0 = o0 + n_out + nd
        scr, sems = refs[s0:s0 + n_scr], refs[s0 + n_scr:]

        @pl.when(at_step(False))
        def _():
            for cp in exch.copies(src, dst, *sems)[0]:
                cp.start()

        body(*ins, *outs, *scr)

        @pl.when(at_step(True))
        def _():
            for cp in exch.copies(src, dst, *sems)[1]:
                cp.wait()

    res = pl.pallas_call(
        wrapped, name=name, grid=grid, in_specs=list(in_specs) + e_in, out_specs=list(out_specs) + e_out,
        out_shape=list(out_shape) + e_shape, scratch_shapes=list(scratch_shapes) + e_scr,
        input_output_aliases=aliases,
        compiler_params=pltpu.CompilerParams(dimension_semantics=("arbitrary",) * len(grid),
                                             vmem_limit_bytes=VMEM_LIMIT, has_side_effects=True),
    )(*operands, *exch.srcs)
    return list(res[:n_out]), list(res[n_out:])


def allreduce_small(name, part):
    rows, width = part.shape

    def body(p_ref, o_ref, buf, send_sem, recv_sem):
        x, y, c = lax.axis_index("x"), lax.axis_index("y"), lax.axis_index("c")
        me = 4 * x + 2 * y + c
        buf[me] = p_ref[...]
        waits = []
        for k in range(1, NDEV):
            px = 1 - x if k & 4 else x
            py = 1 - y if k & 2 else y
            pc = 1 - c if k & 1 else c
            peer = 4 * px + 2 * py + pc
            pltpu.make_async_remote_copy(p_ref, buf.at[me], send_sem.at[k - 1], recv_sem.at[k - 1],
                                         device_id=(px, py, pc), device_id_type=MESH).start()
            waits.append(pltpu.make_async_remote_copy(p_ref, buf.at[peer], send_sem.at[k - 1], recv_sem.at[k - 1],
                                                      device_id=(px, py, pc), device_id_type=MESH))
        for w in waits:
            w.wait()
        acc = buf[0]
        for j in range(1, NDEV):
            acc = acc + buf[j]
        o_ref[...] = acc

    vmem = pl.BlockSpec(memory_space=pltpu.VMEM)
    return pl.pallas_call(
        body, name=name, in_specs=[vmem], out_specs=vmem, out_shape=jax.ShapeDtypeStruct((rows, width), F32),
        scratch_shapes=[pltpu.VMEM((NDEV, rows, width), F32), pltpu.SemaphoreType.DMA((NDEV - 1,)),
                        pltpu.SemaphoreType.DMA((NDEV - 1,))],
        compiler_params=pltpu.CompilerParams(vmem_limit_bytes=VMEM_LIMIT, has_side_effects=True),
    )(part)


def rowwise(name, fn, m, ins=(), consts=(), outs=(), alias_outs=(), accs=(), tm=ROW_TILE):
    tm = min(tm, m)
    n_in, n_c, n_o, n_al, n_ac = len(ins), len(consts), len(outs), len(alias_outs), len(accs)
    held = [a for (a, _, _) in alias_outs if not isinstance(a, jax.ShapeDtypeStruct)]
    n_held = len(held)

    def body(*refs):
        in_refs = refs[:n_in + n_c]
        out_refs = refs[n_in + n_c + n_held:]
        vals = fn(*[r[...] for r in in_refs])
        if not isinstance(vals, (tuple, list)):
            vals = (vals,)
        for r, v in zip(out_refs[:n_o + n_al], vals[:n_o + n_al]):
            r[...] = v.astype(r.dtype)
        if n_ac:
            acc_refs = out_refs[n_o + n_al:]

            @pl.when(pl.program_id(0) == 0)
            def _():
                for r in acc_refs:
                    r[...] = jnp.zeros(r.shape, F32)

            for r, v in zip(acc_refs, vals[n_o + n_al:]):
                r[...] += v

    def col(cb):
        return lambda i: (i, cb)

    in_specs = [pl.BlockSpec((tm, w), col(cb)) for (_, w, cb) in ins]
    in_specs += [pl.BlockSpec(c.shape, lambda i, nd=c.ndim: (0,) * nd) for c in consts]
    in_specs += [ANY for _ in held]
    out_shape = [jax.ShapeDtypeStruct((m, w), dt) for (w, dt) in outs]
    out_specs = [pl.BlockSpec((tm, w), col(0)) for (w, _) in outs]
    out_shape += [jax.ShapeDtypeStruct(a.shape, a.dtype) for (a, _, _) in alias_outs]
    out_specs += [pl.BlockSpec((tm, w), col(cb)) for (_, w, cb) in alias_outs]
    out_shape += [jax.ShapeDtypeStruct(s, F32) for s in accs]
    out_specs += [pl.BlockSpec(s, lambda i: (0, 0)) for s in accs]
    aliases, k_in = {}, n_in + n_c
    for k, (a, _, _) in enumerate(alias_outs):
        if not isinstance(a, jax.ShapeDtypeStruct):
            aliases[k_in] = n_o + k
            k_in += 1
    return pl.pallas_call(
        body, name=name, grid=(m // tm,), in_specs=in_specs, out_specs=out_specs, out_shape=out_shape,
        input_output_aliases=aliases,
        compiler_params=_params(("arbitrary",) if n_ac else ("parallel",)),
    )(*[a for (a, _, _) in ins], *consts, *held)


def _operand(arr, bshape, imap):
    if isinstance(arr, tuple):
        arr, lead = arr
        return arr, pl.BlockSpec((None,) + bshape, lambda *g: (lead,) + imap(*g))
    return arr, pl.BlockSpec(bshape, imap)


def _shape2(arr):
    return arr[0].shape[1:] if isinstance(arr, tuple) else arr.shape


def mm(name, a, b, mode, out_dtype=F32, tm=1024, tn=1024, tk=1024, exch=None):
    sa, sb = _shape2(a), _shape2(b)
    if mode == "nn":
        (M, K), N = sa, sb[1]
    elif mode == "nt":
        (M, K), N = sa, sb[0]
    else:
        (K, M), N = sa, sb[1]
    tm, tn, tk = min(tm, M), min(tn, N), min(tk, K)
    assert M % tm == 0 and N % tn == 0 and K % tk == 0, (name, M, N, K)
    nk = K // tk
    if mode == "nn":
        a_arr, a_spec = _operand(a, (tm, tk), lambda i, j, k: (i, k))
        b_arr, b_spec = _operand(b, (tk, tn), lambda i, j, k: (k, j))
        dot = _dot
    elif mode == "nt":
        a_arr, a_spec = _operand(a, (tm, tk), lambda i, j, k: (i, k))
        b_arr, b_spec = _operand(b, (tn, tk), lambda i, j, k: (j, k))
        dot = _dot_nt
    else:
        a_arr, a_spec = _operand(a, (tk, tm), lambda i, j, k: (k, i))
        b_arr, b_spec = _operand(b, (tk, tn), lambda i, j, k: (k, j))
        dot = _dot_tn

    def body(a_ref, b_ref, o_ref, *acc):
        part = dot(a_ref[...].astype(BF16), b_ref[...].astype(BF16))
        if nk == 1:
            o_ref[...] = part.astype(o_ref.dtype)
            return
        acc_ref, k = acc[0], pl.program_id(2)

        @pl.when(k == 0)
        def _():
            acc_ref[...] = part

        @pl.when(k > 0)
        def _():
            acc_ref[...] += part

        @pl.when(k == nk - 1)
        def _():
            o_ref[...] = acc_ref[...].astype(o_ref.dtype)

    outs, extra = hosted_call(
        body, exch, name, (M // tm, N // tn, nk), [a_spec, b_spec], [pl.BlockSpec((tm, tn), lambda i, j, k: (i, j))],
        [jax.ShapeDtypeStruct((M, N), out_dtype)], [pltpu.VMEM((tm, tn), F32)] if nk > 1 else [], [a_arr, b_arr],
        ("parallel", "parallel", "arbitrary"))
    return outs[0] if exch is None else (outs[0], extra)


def mm_fused(name, m, a_ins, bs, mode, kdim, prologue=None, a_outs=(), e_ins=(), consts=(), epilogue=None, outs=(),
             alias_outs=(), accs=(), a_to_epilogue=(), a_consts=(), tm=512, tk=1024, exch=None):
    tm = min(tm, m)
    nk = kdim // tk
    assert nk == 1 or not a_to_epilogue
    n = bs[0][0].shape[1 if mode == "nn" else 0]
    n_a, n_b, n_e, n_c = len(a_ins) + len(a_consts), len(bs), len(e_ins), len(consts)
    n_ao, n_o, n_al, n_ac = len(a_outs), len(outs), len(alias_outs), len(accs)
    held = [a for (a, _, _) in alias_outs if not isinstance(a, jax.ShapeDtypeStruct)]
    dot = _dot if mode == "nn" else _dot_nt

    def body(*refs):
        a_refs, b_refs = refs[:n_a], refs[n_a:n_a + n_b]
        e_refs = refs[n_a + n_b:n_a + n_b + n_e + n_c]
        o0 = n_a + n_b + n_e + n_c + len(held)
        ao_refs = refs[o0:o0 + n_ao]
        out_refs = refs[o0 + n_ao:o0 + n_ao + n_o + n_al]
        acc_refs = refs[o0 + n_ao + n_o + n_al:o0 + n_ao + n_o + n_al + n_ac]
        scr = refs[o0 + n_ao + n_o + n_al + n_ac:]
        i, k = pl.program_id(0), pl.program_id(1)
        ck = tk if prologue is None else min(tk, PROLOGUE_CHUNK)
        part = None
        for c0 in range(0, tk, ck):
            cols = slice(c0, c0 + ck)
            tiles = [r[:, cols] for r in a_refs]
            a_list, extra = (tiles, []) if prologue is None else prologue(*tiles)
            for r, v in zip(ao_refs, extra):
                r[:, cols] = v.astype(r.dtype)
            for a, b_ref in zip(a_list, b_refs):
                b = b_ref[cols, :] if mode == "nn" else b_ref[:, cols]
                prod = dot(a.astype(BF16), b.astype(BF16))
                part = prod if part is None else part + prod

        def finish(total):
            vals = epilogue(total, *[a_refs[j][...] for j in a_to_epilogue], *[r[...] for r in e_refs])
            if not isinstance(vals, (tuple, list)):
                vals = (vals,)
            for r, v in zip(out_refs, vals[:n_o + n_al]):
                r[...] = v.astype(r.dtype)
            for r, v in zip(acc_refs, vals[n_o + n_al:]):
                @pl.when(i == 0)
                def _():
                    r[...] = v

                @pl.when(i > 0)
                def _():
                    r[...] += v

        if nk == 1:
            finish(part)
            return
        acc_ref = scr[0]

        @pl.when(k == 0)
        def _():
            acc_ref[...] = part

        @pl.when(k > 0)
        def _():
            acc_ref[...] += part

        @pl.when(k == nk - 1)
        def _():
            finish(acc_ref[...])

    in_specs = [pl.BlockSpec((tm, tk), lambda i, k, off=off: (i, k + off)) for (_, off) in a_ins]
    in_specs += [pl.BlockSpec((1, tk), lambda i, k: (0, k)) for _ in a_consts]
    if mode == "nn":
        in_specs += [pl.BlockSpec((tk, n), lambda i, k, off=off: (k + off, 0)) for (_, off) in bs]
    else:
        in_specs += [pl.BlockSpec((n, tk), lambda i, k, off=off: (0, k + off)) for (_, off) in bs]
    in_specs += [pl.BlockSpec((tm, w), lambda i, k, cb=cb: (i, cb)) for (_, w, cb) in e_ins]
    in_specs += [pl.BlockSpec(c.shape, lambda i, k, nd=c.ndim: (0,) * nd) for c in consts]
    in_specs += [ANY for _ in held]
    out_shape = [jax.ShapeDtypeStruct((m, kdim), dt) for dt in a_outs]
    out_specs = [pl.BlockSpec((tm, tk), lambda i, k: (i, k)) for _ in a_outs]
    out_shape += [jax.ShapeDtypeStruct((m, w), dt) for (w, dt) in outs]
    out_specs += [pl.BlockSpec((tm, w), lambda i, k: (i, 0)) for (w, _) in outs]
    out_shape += [jax.ShapeDtypeStruct(a.shape, a.dtype) for (a, _, _) in alias_outs]
    out_specs += [pl.BlockSpec((tm, w), lambda i, k, cb=cb: (i, cb)) for (_, w, cb) in alias_outs]
    out_shape += [jax.ShapeDtypeStruct(s_, F32) for s_ in accs]
    out_specs += [pl.BlockSpec(s_, lambda i, k: (0, 0)) for s_ in accs]
    aliases, k_in = {}, n_a + n_b + n_e + n_c
    for j, (a, _, _) in enumerate(alias_outs):
        if not isinstance(a, jax.ShapeDtypeStruct):
            aliases[k_in] = n_ao + n_o + j
            k_in += 1
    operands = [a for (a, _) in a_ins] + list(a_consts) + [b for (b, _) in bs] + [a for (a, _, _) in e_ins] + list(consts) + held
    res, extra = hosted_call(
        body, exch, name, (m // tm, nk), in_specs, out_specs, out_shape,
        [pltpu.VMEM((tm, n), F32)] if nk > 1 else [], operands,
        ("arbitrary" if n_ac else "parallel", "arbitrary"), aliases)
    return res if exch is None else (res, extra)


def _cumsum_rows(x):
    n = x.shape[0]
    row = lax.broadcasted_iota(jnp.int32, x.shape, 0)
    s = 1
    while s < n:
        x = x + jnp.where(row >= s, pltpu.roll(x, s, 0), 0.0)
        s *= 2
    return x


def _hg_prep(zq, zf, lb, reverse, b=None):
    n = zq.shape[0]
    q = _silu(zq)
    sig = _sigmoid(zf)
    sn = 1.0 - sig
    f = lb + (1.0 - lb) * sig
    k = (1.0 - lb) * sn
    if b is None:
        g = jnp.log(jnp.maximum(f, TINY))
        b = _cumsum_rows(g)
        if reverse:
            b = b[n - 1:n] - b + g
    b_last = b[0:1] if reverse else b[n - 1:n]
    b_ref = b[n // 2:n // 2 + 1]
    e1 = jnp.exp(b)
    e2 = jnp.exp(jnp.clip(b - b_ref, -EXP_CLAMP, EXP_CLAMP))
    e3 = jnp.exp(jnp.clip(b_ref - b, -EXP_CLAMP, EXP_CLAMP))
    e4 = jnp.exp(b_last - b)
    return dict(q=q, k=k, sig=sig, sn=sn, f=f, b=b, e1=e1, e2=e2, e3=e3, e4=e4, e_last=jnp.exp(b_last),
                qe=(q * e1).astype(BF16), qt=(q * e2).astype(BF16), kt=(k * e3).astype(BF16),
                ks=(k * e4).astype(BF16))


def _hg_mask(n, reverse):
    t = lax.broadcasted_iota(jnp.int32, (n, n), 0)
    s = lax.broadcasted_iota(jnp.int32, (n, n), 1)
    return (s >= t) if reverse else (s <= t)


def hgrn_fwd(name, z, lb_f, lb_b, exch=None, unroll=False):
    m = z.shape[0]
    C, T = HG_CHUNK, min(HG_BLOCK_FWD, m)
    nb, cpb = m // T, T // C

    def body(zq_f, zf_f, zi_f, zq_b, zf_b, zi_b, lbf_ref, lbb_ref, of_ref, ob_ref, sf_ref, sb_ref, bf_ref, bb_ref,
             st_ref):
        @pl.when(pl.program_id(0) == 0)
        def _():
            st_ref[...] = jnp.zeros(st_ref.shape, F32)

        dirs = ((zq_f, zf_f, zi_f, lbf_ref, of_ref, sf_ref), (zq_b, zf_b, zi_b, lbb_ref, ob_ref, sb_ref))
        b_refs = (bf_ref, bb_ref)

        def chunk(ci, carry):
            work = []
            for d, (zq, zf, zi, lb_ref, o_ref, s_ref) in enumerate(dirs):
                cc = ci if d == 0 else cpb - 1 - ci
                rows = pl.ds(pl.multiple_of(cc * C, C), C)
                pre = _hg_prep(zq[rows, :], zf[rows, :], lb_ref[...], d == 1)
                v = zi[rows, :].astype(BF16)
                work.append((cc, rows, pre, v, [st_ref[d, h] for h in range(HEADS)]))
            heads = [(d, h, slice(h * HEAD_DIM, (h + 1) * HEAD_DIM)) for d in range(2) for h in range(HEADS)]
            first = {}
            for d, h, sl in heads:
                _, _, pre, v, sts = work[d]
                first[d, h] = (_dot_nt(pre["qt"][:, sl], pre["kt"][:, sl]),
                               _dot_nt(pre["qe"][:, sl], sts[h].astype(BF16)),
                               _dot_tn(v[:, sl], pre["ks"][:, sl]))
            results = [([], []), ([], [])]
            for d, h, sl in heads:
                _, _, pre, v, sts = work[d]
                scores, o_inter, st_add = first[d, h]
                a = jnp.where(_hg_mask(C, d == 1), scores, 0.0).astype(BF16)
                results[d][0].append(o_inter + _dot(a, v[:, sl]))
                results[d][1].append(sts[h] * pre["e_last"][:, sl] + st_add)
            results = [(jnp.concatenate(o_parts, axis=1), new_sts) for (o_parts, new_sts) in results]
            for d, (zq, zf, zi, lb_ref, o_ref, s_ref) in enumerate(dirs):
                cc, rows, pre, _, sts = work[d]
                o_ref[rows, :] = results[d][0]
                b_refs[d][rows, :] = pre["b"]
                for h in range(HEADS):
                    s_ref[cc, h] = sts[h]
                    st_ref[d, h] = results[d][1][h]
            return carry

        lax.fori_loop(0, cpb, chunk, 0, unroll=unroll)

    def zspec(cb, rev):
        return pl.BlockSpec((T, D), (lambda i: (nb - 1 - i, cb)) if rev else (lambda i: (i, cb)))

    def sspec(rev):
        shape = (cpb, HEADS, HEAD_DIM, HEAD_DIM)
        return pl.BlockSpec(shape, (lambda i: (nb - 1 - i, 0, 0, 0)) if rev else (lambda i: (i, 0, 0, 0)))

    lbspec = pl.BlockSpec((1, D), lambda i: (0, 0))
    states = jax.ShapeDtypeStruct((m // C, HEADS, HEAD_DIM, HEAD_DIM), F32)
    outs, extra = hosted_call(
        body, exch, name, (nb,),
        [zspec(ZQ, False), zspec(ZFF, False), zspec(ZI, False), zspec(ZQ, True), zspec(ZFB, True), zspec(ZI, True),
         lbspec, lbspec],
        [zspec(0, False), zspec(0, True), sspec(False), sspec(True), zspec(0, False), zspec(0, True)],
        [jax.ShapeDtypeStruct((m, D), F32), jax.ShapeDtypeStruct((m, D), F32), states, states,
         jax.ShapeDtypeStruct((m, D), F32), jax.ShapeDtypeStruct((m, D), F32)],
        [pltpu.VMEM((2, HEADS, HEAD_DIM, HEAD_DIM), F32)], [z, z, z, z, z, z, lb_f, lb_b], ("arbitrary",))
    return outs, extra


def hgrn_bwd(name, z, d_o, s_f, s_b, b_f, b_b, lb_f, lb_b, exch=None, unroll=False):
    m = z.shape[0]
    C, T = HG_CHUNK, min(HG_BLOCK_BWD, m)
    nb, cpb = m // T, T // C

    def body(zq_f, zf_f, zi_f, do_f, sf_ref, zq_b, zf_b, zi_b, do_b, sb_ref, lbf_ref, lbb_ref, bf_ref, bb_ref,
             dqf_ref, dvf_ref, dqb_ref, dvb_ref, dzf_f, dzf_b, dlbf_ref, dlbb_ref,
             dst_ref):
        b_refs = (bf_ref, bb_ref)
        @pl.when(pl.program_id(0) == 0)
        def _():
            dst_ref[...] = jnp.zeros(dst_ref.shape, F32)
            dlbf_ref[...] = jnp.zeros(dlbf_ref.shape, F32)
            dlbb_ref[...] = jnp.zeros(dlbb_ref.shape, F32)

        dirs = ((zq_f, zf_f, zi_f, do_f, sf_ref, lbf_ref, dqf_ref, dvf_ref, dzf_f, dlbf_ref),
                (zq_b, zf_b, zi_b, do_b, sb_ref, lbb_ref, dqb_ref, dvb_ref, dzf_b, dlbb_ref))

        def chunk(ci, carry):
            work = []
            for d, (zq, zf, zi, do_ref, s_ref, lb_ref, dq_ref, dv_ref, dzf_ref, dlb_ref) in enumerate(dirs):
                cc = cpb - 1 - ci if d == 0 else ci
                rows = pl.ds(pl.multiple_of(cc * C, C), C)
                lb = lb_ref[...]
                pre = _hg_prep(zq[rows, :], zf[rows, :], lb, d == 1, b=b_refs[d][rows, :])
                work.append((rows, lb, pre, zi[rows, :].astype(BF16), do_ref[rows, :],
                             [s_ref[cc, h] for h in range(HEADS)], [dst_ref[d, h] for h in range(HEADS)],
                             dlb_ref[...]))
            heads = [(d, h, slice(h * HEAD_DIM, (h + 1) * HEAD_DIM)) for d in range(2) for h in range(HEADS)]
            first = {}
            for d, h, sl in heads:
                _, _, pre, v, do, st_prevs, dsts, _ = work[d]
                dst16 = dsts[h].astype(BF16)
                first[d, h] = (_dot_nt(pre["qt"][:, sl], pre["kt"][:, sl]),
                               _dot_nt(do[:, sl], v[:, sl]),
                               _dot(do[:, sl], st_prevs[h].astype(BF16)),
                               _dot(v[:, sl], dst16),
                               _dot_nt(pre["ks"][:, sl], dst16),
                               _dot_tn(do[:, sl], pre["qe"][:, sl]))
            parts = [[[] for _ in range(6)] for _ in range(2)]
            for d, h, sl in heads:
                _, _, pre, v, do, st_prevs, dsts, _ = work[d]
                scores, dscores, dq_inter, dk_state, dv_state, dst_add = first[d, h]
                mask = _hg_mask(C, d == 1)
                a = jnp.where(mask, scores, 0.0).astype(BF16)
                da = jnp.where(mask, dscores, 0.0).astype(BF16)
                dq_p, dki_p, dks_p, dv_p, rr_p, new_dsts = parts[d]
                dq_p.append(_dot(da, pre["kt"][:, sl]) * pre["e2"][:, sl] + dq_inter * pre["e1"][:, sl])
                dki_p.append(_dot_tn(da, pre["qt"][:, sl]) * pre["e3"][:, sl])
                dks_p.append(dk_state * pre["e4"][:, sl])
                dv_p.append(_dot_tn(a, do[:, sl]) + dv_state)
                rr_p.append(pre["e_last"][:, sl] * _colsum(dsts[h] * st_prevs[h]))
                new_dsts.append(dsts[h] * pre["e_last"][:, sl] + dst_add)
            results = []
            for d, (rows, lb, pre, v, do, st_prevs, dsts, dlb_old) in enumerate(work):
                rev = d == 1
                dq_p, dki_p, dks_p, dv_p, rr_p, new_dsts = parts[d]
                dq, dki, dks, dv, rr = (jnp.concatenate(p_, axis=1) for p_ in (dq_p, dki_p, dks_p, dv_p, rr_p))
                x = pre["q"] * dq - pre["k"] * dki
                y = pre["k"] * dks
                if rev:
                    dg = _cumsum_rows(x - y) + _colsum(y) + rr
                else:
                    dg = _cumsum_rows(y - x) + (x - y) + _colsum(x) + rr
                inv_f = jnp.where(pre["f"] > TINY, 1.0 / pre["f"], 0.0)
                u = dg * inv_f - (dki + dks)
                results.append((dq, dv, (1.0 - lb) * pre["sig"] * pre["sn"] * u, dlb_old + _colsum(pre["sn"] * u),
                                new_dsts))
            for d, (zq, zf, zi, do_ref, s_ref, lb_ref, dq_ref, dv_ref, dzf_ref, dlb_ref) in enumerate(dirs):
                rows = work[d][0]
                dq, dv, dzf, dlb, new_dsts = results[d]
                dq_ref[rows, :] = dq.astype(dq_ref.dtype)
                dv_ref[rows, :] = dv.astype(dv_ref.dtype)
                dzf_ref[rows, :] = dzf.astype(dzf_ref.dtype)
                dlb_ref[...] = dlb
                for h in range(HEADS):
                    dst_ref[d, h] = new_dsts[h]
            return carry

        lax.fori_loop(0, cpb, chunk, 0, unroll=unroll)

    def rspec(cb, rev):
        return pl.BlockSpec((T, D), (lambda i: (i, cb)) if rev else (lambda i: (nb - 1 - i, cb)))

    def sspec(rev):
        shape = (cpb, HEADS, HEAD_DIM, HEAD_DIM)
        return pl.BlockSpec(shape, (lambda i: (i, 0, 0, 0)) if rev else (lambda i: (nb - 1 - i, 0, 0, 0)))

    lbspec = pl.BlockSpec((1, D), lambda i: (0, 0))
    half = jax.ShapeDtypeStruct((m, D), BF16)
    row = jax.ShapeDtypeStruct((1, D), F32)
    outs, extra = hosted_call(
        body, exch, name, (nb,),
        [rspec(ZQ, False), rspec(ZFF, False), rspec(ZI, False), rspec(0, False), sspec(False),
         rspec(ZQ, True), rspec(ZFB, True), rspec(ZI, True), rspec(0, True), sspec(True), lbspec, lbspec,
         rspec(0, False), rspec(0, True)],
        [rspec(0, False), rspec(0, False), rspec(0, True), rspec(0, True), rspec(0, False), rspec(0, True),
         lbspec, lbspec],
        [half, half, half, half, half, half, row, row],
        [pltpu.VMEM((2, HEADS, HEAD_DIM, HEAD_DIM), F32)],
        [z, z, z, d_o, s_f, z, z, z, d_o, s_b, lb_f, lb_b, b_f, b_b], ("arbitrary",))
    return outs, extra


def _heads(fn, *arrs):
    res = [fn(*[a[:, h * HEAD_DIM:(h + 1) * HEAD_DIM] for a in arrs]) for h in range(arrs[0].shape[1] // HEAD_DIM)]
    return [jnp.concatenate(parts, axis=1) for parts in zip(*res)]


def _hg_post(o_f, o_b, zg, g):
    def head(of, ob, zgh, gh):
        on, _ = _rms(of + ob)
        return (on * gh * _silu(zgh),)
    return _heads(head, o_f, o_b, zg, g)[0]


def _hg_post_bwd(da, o_f, o_b, zg, g):
    def head(dah, of, ob, zgh, gh):
        on, r = _rms(of + ob)
        sg = _silu(zgh)
        d_on = dah * sg
        return _rms_bwd(d_on, on, r, gh), dah * on * gh * _silu_grad(zgh), d_on * on
    d_o, dzg, dg = _heads(head, da, o_f, o_b, zg, g)
    return d_o, dzg, _colsum(dg)


def _sg_parts(zv, ln_g, ln_b):
    vg = _gelu(zv)
    xc = vg - _mean(vg)
    rstd = lax.rsqrt(_mean(xc * xc) + EPS)
    vh = xc * rstd
    return vh, rstd, vh * ln_g + ln_b


def _sg_lane_group(shape):
    return lax.broadcasted_iota(jnp.int32, shape, 1) < SG_GROUP_DIM


def _sg_mix(w, v16, transpose):
    rows = v16.shape[0]
    out = []
    for c in range(rows // SG_CHUNK):
        parts = []
        for j in range(SG_WIDTH // 128):
            vj = v16[c * SG_CHUNK:(c + 1) * SG_CHUNK, j * 128:(j + 1) * 128]
            w0 = w[(2 * j) * SG_CHUNK:(2 * j + 1) * SG_CHUNK]
            w1 = w[(2 * j + 1) * SG_CHUNK:(2 * j + 2) * SG_CHUNK]
            dot = _dot_tn if transpose else _dot
            parts.append(jnp.where(_sg_lane_group((SG_CHUNK, 128)), dot(w0, vj), dot(w1, vj)))
        out.append(jnp.concatenate(parts, axis=1))
    return jnp.concatenate(out, axis=0)


def _sg_fwd(zu, zv, w, bias, ln_g, ln_b):
    _, _, v = _sg_parts(zv, ln_g, ln_b)
    reps = zu.shape[0] // SG_CHUNK
    return _gelu(zu) * (_sg_mix(w, v.astype(BF16), False) + jnp.concatenate([bias] * reps, axis=0))


def _sg_bwd(db, zu, zv, w, bias, ln_g, ln_b):
    vh, rstd, v = _sg_parts(zv, ln_g, ln_b)
    v16 = v.astype(BF16)
    reps = zu.shape[0] // SG_CHUNK
    sg = _sg_mix(w, v16, False) + jnp.concatenate([bias] * reps, axis=0)
    dzu = db * sg * _gelu_grad(zu)
    dsg = db * _gelu(zu)
    dsg16 = dsg.astype(BF16)
    dv = _sg_mix(w, dsg16, True)
    low = _sg_lane_group((SG_CHUNK, 128))
    dw = []
    for g in range(SG_WIDTH // SG_GROUP_DIM):
        j, keep = g // 2, (low if g % 2 == 0 else jnp.logical_not(low))
        acc = jnp.zeros((SG_CHUNK, SG_CHUNK), F32)
        for c in range(reps):
            rows = slice(c * SG_CHUNK, (c + 1) * SG_CHUNK)
            dj = jnp.where(keep, dsg16[rows, j * 128:(j + 1) * 128], jnp.zeros((), BF16))
            acc = acc + _dot_nt(dj, v16[rows, j * 128:(j + 1) * 128])
        dw.append(acc)
    dbias = sum(dsg[c * SG_CHUNK:(c + 1) * SG_CHUNK] for c in range(reps))
    dvh = dv * ln_g
    dvg = rstd * (dvh - _mean(dvh) - vh * _mean(dvh * vh))
    dzuv = jnp.concatenate([dzu, dvg * _gelu_grad(zv)], axis=1)
    return (dzuv, jnp.concatenate(dw, axis=0), dbias, _colsum(dv * vh), _colsum(dv))


def lower_bounds(name, gamma_f, gamma_b):
    def body(gf_ref, gb_ref, lf_ref, lb_ref):
        for g_ref, o_ref in ((gf_ref, lf_ref), (gb_ref, lb_ref)):
            g0, g1 = g_ref[0:1, :], g_ref[1:2, :]
            mx = jnp.maximum(g0, g1)
            e0, e1 = jnp.exp(g0 - mx), jnp.exp(g1 - mx)
            sm0, sm1 = e0 / (e0 + e1), e1 / (e0 + e1)
            o_ref[0:1, :] = sm0 - sm0
            o_ref[1:2, :] = (sm0 + sm1) - sm0
    shp = jax.ShapeDtypeStruct(gamma_f.shape, F32)
    return pl.pallas_call(body, name=name, out_shape=[shp, shp])(gamma_f, gamma_b)


def lower_bounds_bwd(name, gamma_f, gamma_b, dlb_f, dlb_b):
    def body(gf_ref, gb_ref, df_ref, db_ref, of_ref, ob_ref):
        for g_ref, d_ref, o_ref in ((gf_ref, df_ref, of_ref), (gb_ref, db_ref, ob_ref)):
            g0, g1 = g_ref[0:1, :], g_ref[1:2, :]
            mx = jnp.maximum(g0, g1)
            e0, e1 = jnp.exp(g0 - mx), jnp.exp(g1 - mx)
            sm0, sm1 = e0 / (e0 + e1), e1 / (e0 + e1)
            d1 = d_ref[1:2, :] * sm0 * sm1
            o_ref[0:1, :] = -d1
            o_ref[1:2, :] = d1
    shp = jax.ShapeDtypeStruct(gamma_f.shape, F32)
    return pl.pallas_call(body, name=name, out_shape=[shp, shp])(gamma_f, gamma_b, dlb_f, dlb_b)


def _row(a, l):
    return a[l:l + 1]


class LocalPlan:
    def __init__(self, weights):
        self.W = weights
        self.grads = [dict() for _ in range(DEPTH)]

    def exch(self, host):
        return None

    def done(self, host, outs):
        pass

    def early_small(self, packed):
        pass


def local_step(x, p, target, S, plan):
    m = x.shape[0]

    def hmm(tag, *args, **kw):
        ex = plan.exch(tag)
        res = mm(tag, *args, exch=ex, **kw)
        if ex is None:
            return res
        plan.done(tag, res[1])
        return res[0]

    lb_f, lb_b = lower_bounds("lower_bounds", S["lb_gamma_fwd"], S["lb_gamma_bwd"])
    saved = []
    for l in range(DEPTH):
        t = f"l{l}_"
        W = plan.W[l]
        tm = 2048
        in_tile = (1024, 2048)
        ffn_tile = (2048, 2048)
        g_pre, g_post = _row(S["norm_mix_pre"], l), _row(S["norm_mix_post"], l)
        g_fpre, g_fpost = _row(S["norm_ffn_pre"], l), _row(S["norm_ffn_post"], l)
        hg_g = _row(S["hg_norm"], l)
        sg_w = S["sg_w"][l].reshape(SG_WIDTH // SG_GROUP_DIM * SG_CHUNK, SG_CHUNK).astype(BF16)
        sg_bias = jnp.repeat(S["sg_b"][l].T, SG_GROUP_DIM, axis=1)
        ln_g, ln_b = _row(S["sg_ln_g"], l), _row(S["sg_ln_b"], l)
        lbf, lbb = _row(lb_f, l), _row(lb_b, l)

        if l == 0:
            (h,) = rowwise(t + "pre_norm", lambda xv, g: (_rms(xv)[0] * g,), m, ins=[(x, D, 0)], consts=[g_pre],
                           outs=[(D, BF16)])
        z = hmm(t + "in_proj", h, W["w_in"], "nn", tm=in_tile[0], tn=in_tile[1])
        (o_f, o_b, s_f, s_b, b_f, b_b), extra = hgrn_fwd(t + "hgrn_fwd", z, lbf, lbb, exch=plan.exch(t + "hgrn_fwd"))
        plan.done(t + "hgrn_fwd", extra)
        (b_out,) = rowwise(t + "sgu_fwd", _sg_fwd, m, ins=[(z, SG_WIDTH, ZU), (z, SG_WIDTH, ZV)],
                           consts=[sg_w, sg_bias, ln_g, ln_b], outs=[(SG_WIDTH, BF16)])

        def post_pro(of, ob, zg, g):
            ao = _hg_post(of, ob, zg, g).astype(BF16)
            return [ao], [ao]
        a_out, pa = mm_fused(t + "proj_a", m, [(o_f, 0), (o_b, 0), (z, ZG)], [(W["w_a"], 0)], "nn", D,
                             prologue=post_pro, a_outs=[BF16], a_consts=[hg_g], epilogue=lambda tot: (tot,),
                             outs=[(D, BF16)])
        pb = mm(t + "proj_b", b_out, W["w_b"], "nn", BF16)

        def merge_pro(a, b, ga, gb):
            mg = (_sigmoid(ga) * a + _sigmoid(gb) * b).astype(BF16)
            return [mg], [mg]

        def post_pre(mixv, xv, gp, gf):
            x1 = xv + _rms(mixv)[0] * gp
            return mixv, x1, _rms(x1)[0] * gf
        merged, mix, x1, h2 = mm_fused(
            t + "out_proj", m, [(pa, 0), (pb, 0), (z, GA), (z, GB)], [(W["w_out"], 0)], "nn", D, prologue=merge_pro,
            a_outs=[BF16], e_ins=[(x, D, 0)], consts=[g_post, g_fpre], epilogue=post_pre,
            outs=[(D, F32), (D, F32), (D, BF16)])
        gu = hmm(t + "ffn_in", h2, W["w_gu"], "nn", BF16, tm=ffn_tile[0], tn=ffn_tile[1])

        def act_pro(gt, up):
            hd = (_silu(gt.astype(F32)) * up).astype(BF16)
            return [hd], [hd]
        hid, ff, x2 = mm_fused(
            t + "ffn_out", m, [(gu, 0), (gu, FFN_PAD // 1024)], [(W["w_down"], 0)], "nn", FFN_PAD, prologue=act_pro,
            a_outs=[BF16], e_ins=[(x1, D, 0)], consts=[g_fpost],
            epilogue=lambda f, xv, g: (f, xv + _rms(f)[0] * g), outs=[(D, F32), (D, F32)], tm=(512, 1024)[l])
        e = mm(t + "ple_proj", (p, l), W["w_ple"], "nn")

        if l + 1 < DEPTH:
            def ple_add(tv, xv, ev, g):
                x3 = xv + ev * _sigmoid(tv)
                return tv, x3, _rms(x3)[0] * g
            tg, x3, h_next = mm_fused(
                t + "ple_gate", m, [(x2, 0)], [(W["w_ple_gate"], 0)], "nn", D, a_to_epilogue=(0,), e_ins=[(e, D, 0)],
                consts=[_row(S["norm_mix_pre"], l + 1)], epilogue=ple_add, outs=[(D, F32), (D, F32), (D, BF16)])
        else:
            def ple_loss(tv, xv, ev, tgt):
                err = xv + ev * _sigmoid(tv) - tgt
                return tv, err * (1.0 / D), _colsum(err * err)
            tg, x3, loss_cols = mm_fused(
                t + "ple_gate", m, [(x2, 0)], [(W["w_ple_gate"], 0)], "nn", D, a_to_epilogue=(0,),
                e_ins=[(e, D, 0), (target, D, 0)], epilogue=ple_loss, outs=[(D, F32), (D, F32)], accs=[(1, D)])
            h_next = None
        saved.append(dict(x=x, h=h, z=z, o_f=o_f, o_b=o_b, s_f=s_f, s_b=s_b, b_f=b_f, b_b=b_b, a_out=a_out,
                          b_out=b_out, pa=pa, pb=pb,
                          merged=merged, mix=mix, x1=x1, h2=h2, gu=gu, hid=hid, ff=ff, x2=x2, e=e, tg=tg,
                          sg_w=sg_w, sg_bias=sg_bias))
        x, h = x3, h_next

    dx = x

    gs = {n: [None] * DEPTH for n in SMALL}
    dlb_f, dlb_b = [None] * DEPTH, [None] * DEPTH

    for l in reversed(range(DEPTH)):
        t = f"l{l}_bwd_"
        sv, W = saved[l], plan.W[l]
        tm, tk = 2048, 4096
        g_pre, g_post = _row(S["norm_mix_pre"], l), _row(S["norm_mix_post"], l)
        g_fpre, g_fpost = _row(S["norm_ffn_pre"], l), _row(S["norm_ffn_post"], l)
        hg_g = _row(S["hg_norm"], l)
        ln_g, ln_b = _row(S["sg_ln_g"], l), _row(S["sg_ln_b"], l)
        lbf, lbb = _row(lb_f, l), _row(lb_b, l)

        def wgrad(nm, tag, a, b):
            a_dtype = (a[0] if isinstance(a, tuple) else a).dtype
            plan.grads[l][nm] = mm(tag, a, b, "tn", BF16, tk=tk if a_dtype == BF16 else 2048)

        def ple_pro(d3, ev, tv):
            s = _sigmoid(tv)
            de_, dt_ = (d3 * s).astype(BF16), (d3 * ev * s * (1.0 - s)).astype(BF16)
            return [dt_], [dt_, de_]

        def ffn_post_bwd(d2p, d3, f, g):
            d2 = d3 + d2p
            fh, r = _rms(f)
            return d2, _rms_bwd(d2, fh, r, g), _colsum(d2 * fh)
        dt, de, dx2, dff, gs["norm_ffn_post"][l] = mm_fused(
            t + "ple_gate_dx", m, [(dx, 0), (sv["e"], 0), (sv["tg"], 0)], [(W["w_ple_gate"], 0)], "nt", D,
            prologue=ple_pro, a_outs=[BF16, BF16], a_to_epilogue=(0,), e_ins=[(sv["ff"], D, 0)], consts=[g_fpost],
            epilogue=ffn_post_bwd, outs=[(D, F32), (D, BF16)], accs=[(1, D)])
        wgrad("w_ple", t + "w_ple", (p, l), de)
        wgrad("w_ple_gate", t + "w_ple_gate", sv["x2"], dt)
        wgrad("w_down", t + "w_down", sv["hid"], dff)
        dhid = mm(t + "ffn_out_dx", dff, W["w_down"], "nt", BF16, tm=tm)

        def act_bwd(dh, gt, up):
            dh, gt = dh.astype(F32), gt.astype(F32)
            s = _sigmoid(gt)
            dg_ = (dh * up * (s * (1.0 + gt * (1.0 - s)))).astype(BF16)
            du_ = (dh * (gt * s)).astype(BF16)
            return [dg_, du_], [dg_, du_]

        def pre_post_bwd(dh, d2, x1v, mixv, gf, gp):
            xh, r1 = _rms(x1v)
            d1 = d2 + _rms_bwd(dh, xh, r1, gf)
            mh, rm = _rms(mixv)
            return d1, _rms_bwd(d1, mh, rm, gp), _colsum(dh * xh), _colsum(d1 * mh)
        off = FFN_PAD // 1024
        dgate, dup, dx1, dmix, gs["norm_ffn_pre"][l], gs["norm_mix_post"][l] = mm_fused(
            t + "ffn_in_dx", m, [(dhid, 0), (sv["gu"], 0), (sv["gu"], off)], [(W["w_gu"], 0), (W["w_gu"], off)], "nt",
            FFN_PAD, prologue=act_bwd, a_outs=[BF16, BF16], e_ins=[(dx2, D, 0), (sv["x1"], D, 0), (sv["mix"], D, 0)],
            consts=[g_fpre, g_post], epilogue=pre_post_bwd, outs=[(D, F32), (D, BF16)], accs=[(1, D), (1, D)])
        wgrad("w_gate", t + "w_gate", sv["h2"], dgate)
        wgrad("w_up", t + "w_up", sv["h2"], dup)
        wgrad("w_out", t + "w_out", sv["merged"], dmix)

        def merge_bwd(dm, a, b, gab):
            sa, sb = _sigmoid(gab[:, :D]), _sigmoid(gab[:, D:])
            dgab = jnp.concatenate([dm * a * sa * (1.0 - sa), dm * b * sb * (1.0 - sb)], axis=1)
            return dm * sa, dm * sb, dgab
        dpa, dpb, dz = mm_fused(
            t + "out_proj_dx", m, [(dmix, 0)], [(W["w_out"], 0)], "nt", D,
            e_ins=[(sv["pa"], D, 0), (sv["pb"], D, 0), (sv["z"], 2 * D, 3)], epilogue=merge_bwd,
            outs=[(D, BF16), (D, BF16)], alias_outs=[(jax.ShapeDtypeStruct((m, N_IN), BF16), 2 * D, 3)])
        wgrad("w_a", t + "w_a", sv["a_out"], dpa)
        wgrad("w_b", t + "w_b", sv["b_out"], dpb)
        db = mm(t + "proj_b_dx", dpb, W["w_b"], "nt")

        dz, dsw, dbias, gs["sg_ln_g"][l], gs["sg_ln_b"][l] = rowwise(
            t + "sgu", _sg_bwd, m, ins=[(db, SG_WIDTH, 0), (sv["z"], SG_WIDTH, ZU), (sv["z"], SG_WIDTH, ZV)],
            consts=[sv["sg_w"], sv["sg_bias"], ln_g, ln_b], alias_outs=[(dz, 2 * SG_WIDTH, 5)],
            accs=[(SG_WIDTH // SG_GROUP_DIM * SG_CHUNK, SG_CHUNK), (SG_CHUNK, SG_WIDTH), (1, SG_WIDTH), (1, SG_WIDTH)])
        gs["sg_w"][l] = dsw.reshape(1, SG_WIDTH // SG_GROUP_DIM, SG_CHUNK, SG_CHUNK)
        gs["sg_b"][l] = dbias.reshape(SG_CHUNK, SG_WIDTH // SG_GROUP_DIM, SG_GROUP_DIM).sum(-1).T[None]

        d_o, dz, gs["hg_norm"][l] = mm_fused(
            t + "proj_a_dx", m, [(dpa, 0)], [(W["w_a"], 0)], "nt", D,
            e_ins=[(sv["o_f"], D, 0), (sv["o_b"], D, 0), (sv["z"], D, ZG)], consts=[hg_g], epilogue=_hg_post_bwd,
            outs=[(D, BF16)], alias_outs=[(dz, D, ZG)], accs=[(1, D)], tm=256)
        if l == 0:
            part = {n: (g if not isinstance(g, list) else jnp.concatenate(
                [jnp.zeros((1,) + g[1].shape[1:], F32) if gl is None else gl for gl in g], axis=0))
                for n, g in gs.items()}
            plan.early_small(_pack([part[n].reshape(S[n].shape) for n in SMALL]))
        (dq_f, dv_f, dq_b, dv_b, dzf_f, dzf_b, dlb_f[l], dlb_b[l]), extra = hgrn_bwd(
            t + "hgrn", sv["z"], d_o, sv["s_f"], sv["s_b"], sv["b_f"], sv["b_b"], lbf, lbb,
            exch=plan.exch(t + "hgrn"))
        plan.done(t + "hgrn", extra)

        def combine(dqf, dqb, dvf, dvb, dff_, dfb_, zq):
            dq = dqf.astype(F32) + dqb.astype(F32)
            dv = dvf.astype(F32) + dvb.astype(F32)
            return (jnp.concatenate([(dq * _silu_grad(zq)).astype(BF16), dff_, dfb_, dv.astype(BF16)], axis=1),)
        (dz,) = rowwise(t + "hgrn_combine", combine, m,
                        ins=[(dq_f, D, 0), (dq_b, D, 0), (dv_f, D, 0), (dv_b, D, 0), (dzf_f, D, 0), (dzf_b, D, 0),
                             (sv["z"], D, ZQ)], alias_outs=[(dz, 4 * D, 0)], tm=128)
        wgrad("w_in", t + "w_in", sv["h"], dz)

        def pre_bwd(dhv, d1, xv, g):
            xh, r = _rms(xv)
            return d1 + _rms_bwd(dhv, xh, r, g), _colsum(dhv * xh)
        ex = plan.exch(t + "in_proj_dx")
        res = mm_fused(t + "in_proj_dx", m, [(dz, 0)], [(W["w_in"], 0)], "nt", N_IN,
                       e_ins=[(dx1, D, 0), (sv["x"], D, 0)], consts=[g_pre], epilogue=pre_bwd, outs=[(D, F32)],
                       accs=[(1, D)], tm=1024, exch=ex)
        if ex is not None:
            res, extra = res
            plan.done(t + "in_proj_dx", extra)
        dx, gs["norm_mix_pre"][l] = res
        saved[l] = None
        if l == DEPTH - 1:
            none = jnp.zeros((1, D), F32)
            gs["lb_gamma_fwd"], gs["lb_gamma_bwd"] = lower_bounds_bwd(
                "lower_bounds_bwd", S["lb_gamma_fwd"], S["lb_gamma_bwd"], jnp.concatenate([none, dlb_f[l]], axis=0),
                jnp.concatenate([none, dlb_b[l]], axis=0))

    small ={n: (g if not isinstance(g, list) else jnp.concatenate(g, axis=0)).reshape(S[n].shape)
             for n, g in gs.items()}
    return loss_cols, dx, small


def cast_pad(name, w, rows_p, cols_p):
    _, r, c = w.shape

    def body(w_ref, o_ref):
        if (rows_p, cols_p) != (r, c):
            o_ref[...] = jnp.zeros(o_ref.shape, BF16)
        o_ref[0:r, 0:c] = w_ref[...].astype(BF16)

    return pl.pallas_call(
        body, name=name, grid=(DEPTH,), in_specs=[pl.BlockSpec((None, r, c), lambda l: (l, 0, 0))],
        out_specs=pl.BlockSpec((None, rows_p, cols_p), lambda l: (l, 0, 0)),
        out_shape=jax.ShapeDtypeStruct((DEPTH, rows_p, cols_p), BF16), compiler_params=_params(("parallel",)),
    )(w)


def _shard_shape(n, shape):
    axis, size, _, _ = LAYOUT[n]
    _, r, c = shape
    return (size, c) if axis == 0 else (r, size)


class DistPlan:
    def __init__(self, shards):
        self.shards = shards
        self.W = [dict() for _ in range(DEPTH)]
        self.grads = [dict() for _ in range(DEPTH)]
        self.slots = [dict() for _ in range(DEPTH)]
        rest = [n for n in BIG if n != "w_in"]
        ffn = ["w_gate", "w_up", "w_down"]
        self.schedule = {
            "l0_in_proj": ("gather", [(0, n) for n in rest]),
            "l0_hgrn_fwd": ("gather", [(1, n) for n in BIG if n not in ffn]),
            "l0_ffn_in": ("gather", [(1, n) for n in ffn]),
            "l1_bwd_hgrn": ("scatter", [(1, n) for n in rest]),
            "l1_bwd_in_proj_dx": ("scatter", [(1, "w_in")]),
            "l0_bwd_hgrn": ("scatter", [(0, n) for n in rest]),
            "l0_bwd_in_proj_dx": ("scatter", [(0, "w_in")]),
        }
        self.pending = {}
        self.small_part = self.small_slots = None
        axis, size, dst, _ = LAYOUT["w_in"]
        self.W[0][dst] = gather_two_level("gather_l0_w_in", shards["w_in"], 0, axis, size, GATHERED[dst])

    def _gather(self, host, parts):
        srcs, dsts, items, keys = [], [], [], []
        for layer, n in parts:
            axis, size, dst, base = LAYOUT[n]
            if (layer, dst) not in keys:
                keys.append((layer, dst))
                dsts.append((GATHERED[dst], BF16))
            srcs.append(self.shards[n])
            items.append(("gather", len(srcs) - 1, keys.index((layer, dst)), axis, size, base, layer))
        self.pending[host] = ("gather", keys)
        return Exchange(srcs, dsts, items)

    def _scatter(self, host, parts):
        srcs, dsts, items = [], [], []
        for layer, n in parts:
            axis, size, _, _ = LAYOUT[n]
            srcs.append(self.grads[layer][n])
            dsts.append(((NDEV,) + _shard_shape(n, self.shards[n].shape), BF16))
            items.append(("scatter", len(srcs) - 1, len(dsts) - 1, axis, size, 0, None))
        keys = list(parts)
        if host == "l0_bwd_hgrn" and self.small_part is not None:
            srcs.append(self.small_part)
            dsts.append(((NDEV,) + self.small_part.shape, F32))
            items.append(("copies", len(srcs) - 1, len(dsts) - 1, 0, 0, 0, None))
            keys.append(("small", None))
        self.pending[host] = ("scatter", keys)
        return Exchange(srcs, dsts, items)

    def early_small(self, packed):
        self.small_part = packed

    def exch(self, host):
        if host not in self.schedule:
            return None
        kind, parts = self.schedule[host]
        return self._gather(host, parts) if kind == "gather" else self._scatter(host, parts)

    def done(self, host, outs):
        if host not in self.pending:
            return
        kind, keys = self.pending.pop(host)
        for (layer, n), arr in zip(keys, outs):
            if layer == "small":
                self.small_slots = arr
            else:
                (self.W if kind == "gather" else self.slots)[layer][n] = arr


def adam(name, w, m_, v_, tr, g=None, slots=None):
    L, r, c = w.shape
    assert r % tr == 0
    nt = r // tr
    n_s = 0 if slots is None else L

    def body(*refs):
        s_refs = refs[:n_s]
        g_ref = refs[n_s] if g is not None else None
        w_ref, m_ref, v_ref, g_out, d_out, m_out, v_out = refs[n_s + (g is not None):]

        def update(gv):
            if g_ref is not None:
                gv = gv + g_ref[...] if gv is not None else g_ref[...]
            m2 = B1 * m_ref[...] + (1.0 - B1) * gv
            v2 = B2 * v_ref[...] + (1.0 - B2) * (gv * gv)
            m_hat = m2 / (1.0 - B1 ** STEP)
            v_hat = v2 / (1.0 - B2 ** STEP)
            g_out[...] = gv
            d_out[...] = -LR * (m_hat / (jnp.sqrt(v_hat) + AEPS) + WD * w_ref[...])
            m_out[...] = m2
            v_out[...] = v2

        if slots is None:
            update(None)
            return
        for layer, s_ref in enumerate(s_refs):
            @pl.when(pl.program_id(0) == layer)
            def _():
                gv = s_ref[0][:, :c].astype(F32)
                for j in range(1, NDEV):
                    gv = gv + s_ref[j][:, :c].astype(F32)
                update(gv)

    spec = pl.BlockSpec((None, tr, c), lambda l, i: (l, i, 0))
    arrs, specs = [], []
    if slots is not None:
        assert len(slots) == L and L <= 2
        arrs = list(slots)
        cp = slots[0].shape[2]
        specs = [pl.BlockSpec((NDEV, tr, cp), lambda l, i: (0, i * (1 - l) + (nt - 1) * l, 0)),
                 pl.BlockSpec((NDEV, tr, cp), lambda l, i: (0, i * l, 0))][:L]
    if g is not None:
        arrs.append(g)
        specs.append(spec)
    shp = jax.ShapeDtypeStruct(w.shape, F32)
    return pl.pallas_call(
        body, name=name, grid=(L, nt), in_specs=specs + [spec, spec, spec], out_specs=[spec] * 4,
        out_shape=[shp] * 4, compiler_params=_params(("arbitrary", "arbitrary")),
    )(*arrs, w, m_, v_)


def _pack(arrs):
    parts = []
    for a in arrs:
        a2 = a.reshape(-1, D)
        parts.append(jnp.pad(a2, ((0, -a2.shape[0] % 8), (0, 0))))
    return jnp.concatenate(parts, axis=0)


def _unpack(buf, shapes):
    out, off = [], 0
    for s in shapes:
        rows = 1
        for d_ in s:
            rows *= d_
        rows //= D
        out.append(buf[off:off + rows].reshape(s))
        off += rows + (-rows % 8)
    return out


def kernel(x, p, norm_mix_pre, w_in, lb_gamma_fwd, lb_gamma_bwd, hg_norm, sg_w, sg_b, sg_ln_g, sg_ln_b, w_a, w_b, w_out, norm_mix_post, norm_ffn_pre, w_gate, w_up, w_down, norm_ffn_post, w_ple, w_ple_gate, loss_target, m_norm_mix_pre, m_w_in, m_lb_gamma_fwd, m_lb_gamma_bwd, m_hg_norm, m_sg_w, m_sg_b, m_sg_ln_g, m_sg_ln_b, m_w_a, m_w_b, m_w_out, m_norm_mix_post, m_norm_ffn_pre, m_w_gate, m_w_up, m_w_down, m_norm_ffn_post, m_w_ple, m_w_ple_gate, v_norm_mix_pre, v_w_in, v_lb_gamma_fwd, v_lb_gamma_bwd, v_hg_norm, v_sg_w, v_sg_b, v_sg_ln_g, v_sg_ln_b, v_w_a, v_w_b, v_w_out, v_norm_mix_post, v_norm_ffn_pre, v_w_gate, v_w_up, v_w_down, v_norm_ffn_post, v_w_ple, v_w_ple_gate):
    a = dict(zip(INPUTS, (x, p, norm_mix_pre, w_in, lb_gamma_fwd, lb_gamma_bwd, hg_norm, sg_w, sg_b, sg_ln_g, sg_ln_b, w_a, w_b, w_out, norm_mix_post, norm_ffn_pre, w_gate, w_up, w_down, norm_ffn_post, w_ple, w_ple_gate, loss_target, m_norm_mix_pre, m_w_in, m_lb_gamma_fwd, m_lb_gamma_bwd, m_hg_norm, m_sg_w, m_sg_b, m_sg_ln_g, m_sg_ln_b, m_w_a, m_w_b, m_w_out, m_norm_mix_post, m_norm_ffn_pre, m_w_gate, m_w_up, m_w_down, m_norm_ffn_post, m_w_ple, m_w_ple_gate, v_norm_mix_pre, v_w_in, v_lb_gamma_fwd, v_lb_gamma_bwd, v_hg_norm, v_sg_w, v_sg_b, v_sg_ln_g, v_sg_ln_b, v_w_a, v_w_b, v_w_out, v_norm_mix_post, v_norm_ffn_pre, v_w_gate, v_w_up, v_w_down, v_norm_ffn_post, v_w_ple, v_w_ple_gate)))
    m = x.shape[1]

    shards = {n: cast_pad("cast_" + n, a[n], *_shard_shape(n, a[n].shape)) for n in BIG}
    plan = DistPlan(shards)
    loss_cols, dx, gs = local_step(x[0], p[:, 0], loss_target[0], {n: a[n] for n in SMALL}, plan)
    loss = lax.psum(jnp.sum(loss_cols) * (0.5 / D), ("x", "y", "c"))

    small_shapes = [a[n].shape for n in SMALL]
    rows = plan.small_slots.shape[1]
    late = allreduce_small("allreduce_small", jnp.pad(gs["norm_mix_pre"][0:1], ((0, 7), (0, 0))))
    g_late = jnp.pad(late, ((0, rows - 8), (0, 0)))[None]

    res = {}
    row_tiles = {"w_in": 128, "w_a": 128, "w_b": 512, "w_out": 128, "w_gate": 128, "w_up": 128, "w_down": 88,
                 "w_ple": 256, "w_ple_gate": 128}
    for n in BIG:
        res[n] = adam("adam_" + n, a[n], a["m_" + n], a["v_" + n], row_tiles[n],
                      slots=[plan.slots[l][n] for l in range(DEPTH)])
    packed = [_pack([a[pre + n] for n in SMALL])[None] for pre in ("", "m_", "v_")]
    small_res = adam("adam_small", packed[0], packed[1], packed[2], rows // 2, g=g_late, slots=[plan.small_slots])
    small_res = [_unpack(r_[0], small_shapes) for r_ in small_res]
    for i, n in enumerate(SMALL):
        res[n] = tuple(small_res[k][i] for k in range(4))

    outs = [loss, dx.reshape(1, m, D)]
    for k in range(4):
        outs += [res[n][k] for n in WEIGHTS]
    return tuple(outs)
```

```python
import jax
import jax.numpy as jnp
from jax import lax
from jax.experimental import pallas as pl
from jax.experimental.pallas import tpu as pltpu

F32 = jnp.float32
BF16 = jnp.bfloat16

D = 1024
N_IN = 8192
HEADS = 8
HEAD_DIM = 128
SG_CHUNK = 128
SG_WIDTH = 512
SG_GROUP_DIM = 64
FFN = 2816
PLE_DIM = 256
EPS = 1e-6
DEPTH = 2
ZQ, ZFF, ZFB, ZI, ZG, GA, GB = 0, 1, 2, 3, 4, 6, 7
ZU, ZV = 10, 11

NDEV = 8
FFN_SHARD = FFN // NDEV
FFN_SHARD_PAD = 384
FFN_PAD = NDEV * FFN_SHARD_PAD

LR, B1, B2, AEPS, WD, STEP = 0.001, 0.9, 0.999, 1e-08, 0.01, 10

ROW_TILE = 256
HG_CHUNK = 64
HG_BLOCK_FWD = 256
HG_BLOCK_BWD = 256
EXP_CLAMP = 80.0
PROLOGUE_CHUNK = 256
TINY = float(jnp.finfo(jnp.float32).tiny)
VMEM_LIMIT = 56 * 1024 * 1024

BIG = ["w_in", "w_a", "w_b", "w_out", "w_gate", "w_up", "w_down", "w_ple", "w_ple_gate"]
SMALL = ["norm_mix_pre", "lb_gamma_fwd", "lb_gamma_bwd", "hg_norm", "sg_w", "sg_b", "sg_ln_g", "sg_ln_b",
         "norm_mix_post", "norm_ffn_pre", "norm_ffn_post"]
WEIGHTS = ["norm_mix_pre", "w_in", "lb_gamma_fwd", "lb_gamma_bwd", "hg_norm", "sg_w", "sg_b", "sg_ln_g", "sg_ln_b",
           "w_a", "w_b", "w_out", "norm_mix_post", "norm_ffn_pre", "w_gate", "w_up", "w_down", "norm_ffn_post",
           "w_ple", "w_ple_gate"]
INPUTS = (["x", "p"] + WEIGHTS + ["loss_target"] + ["m_" + n for n in WEIGHTS] + ["v_" + n for n in WEIGHTS])
LAYOUT = {
    "w_in": (1, 1024, "w_in", 0), "w_a": (0, 128, "w_a", 0), "w_b": (1, 128, "w_b", 0),
    "w_out": (0, 128, "w_out", 0), "w_gate": (1, FFN_SHARD_PAD, "w_gu", 0),
    "w_up": (1, FFN_SHARD_PAD, "w_gu", FFN_PAD), "w_down": (0, FFN_SHARD_PAD, "w_down", 0),
    "w_ple": (1, 128, "w_ple", 0), "w_ple_gate": (0, 128, "w_ple_gate", 0),
}
GATHERED = {"w_in": (D, N_IN), "w_a": (D, D), "w_b": (SG_WIDTH, D), "w_out": (D, D), "w_gu": (D, 2 * FFN_PAD),
            "w_down": (FFN_PAD, D), "w_ple": (PLE_DIM, D), "w_ple_gate": (D, D)}


def _params(sem):
    return pltpu.CompilerParams(dimension_semantics=sem, vmem_limit_bytes=VMEM_LIMIT)


def _dot(a, b):
    return lax.dot_general(a, b, (((1,), (0,)), ((), ())), preferred_element_type=F32)


def _dot_nt(a, b):
    return lax.dot_general(a, b, (((1,), (1,)), ((), ())), preferred_element_type=F32)


def _dot_tn(a, b):
    return lax.dot_general(a, b, (((0,), (0,)), ((), ())), preferred_element_type=F32)


def _sigmoid(x):
    return jax.nn.sigmoid(x)


def _silu(x):
    return x * _sigmoid(x)


def _silu_grad(x):
    s = _sigmoid(x)
    return s * (1.0 + x * (1.0 - s))


def _gelu(x):
    return 0.5 * x * (1.0 + lax.erf(x * 0.7071067811865476))


def _gelu_grad(x):
    return 0.5 * (1.0 + lax.erf(x * 0.7071067811865476)) + x * jnp.exp(-0.5 * x * x) * 0.3989422804014327


def _mean(x):
    return jnp.mean(x, axis=-1, keepdims=True)


def _colsum(x):
    return jnp.sum(x, axis=0, keepdims=True)


def _rms(x):
    r = lax.rsqrt(_mean(x * x) + EPS)
    return x * r, r


def _rms_bwd(dy, xh, r, g):
    dyg = dy * g
    return r * (dyg - xh * _mean(dyg * xh))


MESH = pl.DeviceIdType.MESH
ANY = pl.BlockSpec(memory_space=pl.ANY)


def _slab(ref, axis, start, size):
    idx = [slice(None)] * 2
    idx[axis] = pl.ds(start, size)
    return ref.at[tuple(idx)]


class Exchange:
    def __init__(self, srcs, dsts, items):
        self.srcs, self.dsts, self.items = list(srcs), list(dsts), list(items)

    def specs(self):
        n = len(self.items)
        sems = [pltpu.SemaphoreType.DMA((n * (NDEV - 1),)), pltpu.SemaphoreType.DMA((n * (NDEV - 1),)),
                pltpu.SemaphoreType.DMA((n,))]
        return ([ANY] * len(self.srcs), [ANY] * len(self.dsts),
                [jax.ShapeDtypeStruct(s, dt) for (s, dt) in self.dsts], sems)

    def copies(self, src, dst, send_sem, recv_sem, loc_sem):
        x, y, c = lax.axis_index("x"), lax.axis_index("y"), lax.axis_index("c")
        me = 4 * x + 2 * y + c
        starts, waits = [], []
        for n, (kind, si, di, axis, size, base, layer) in enumerate(self.items):
            def views(to_dev, from_dev):
                if kind == "gather":
                    return (src[si].at[layer],
                            _slab(dst[di], axis, base + pl.multiple_of(from_dev * size, 128), size))
                if kind == "copies":
                    return src[si], dst[di].at[from_dev]
                return _slab(src[si], axis, base + pl.multiple_of(to_dev * size, 128), size), dst[di].at[from_dev]

            s_own, d_own = views(me, me)
            own = pltpu.make_async_copy(s_own, d_own, loc_sem.at[n])
            starts.append(own)
            waits.append(own)
            for k in range(1, NDEV):
                px = 1 - x if k & 4 else x
                py = 1 - y if k & 2 else y
                pc = 1 - c if k & 1 else c
                peer = 4 * px + 2 * py + pc
                s_out, _ = views(peer, me)
                _, d_in = views(me, peer)
                sem = n * (NDEV - 1) + k - 1
                starts.append(pltpu.make_async_remote_copy(s_out, d_own, send_sem.at[sem], recv_sem.at[sem],
                                                           device_id=(px, py, pc), device_id_type=MESH))
                waits.append(pltpu.make_async_remote_copy(s_out, d_in, send_sem.at[sem], recv_sem.at[sem],
                                                          device_id=(px, py, pc), device_id_type=MESH))
        return starts, waits


def exchange(name, exch):
    e_in, e_out, e_shape, e_scr = exch.specs()
    ns, nd = len(e_in), len(e_out)

    def body(*refs):
        starts, waits = exch.copies(refs[:ns], refs[ns:ns + nd], *refs[ns + nd:])
        for cp in starts:
            cp.start()
        for cp in waits:
            cp.wait()

    return pl.pallas_call(body, name=name, in_specs=e_in, out_specs=e_out, out_shape=e_shape, scratch_shapes=e_scr,
                          compiler_params=pltpu.CompilerParams(has_side_effects=True))(*exch.srcs)


def gather_two_level(name, shards, layer, axis, size, full_shape):
    def body(src, dst, send_sem, recv_sem, loc_sem):
        x, y, c = lax.axis_index("x"), lax.axis_index("y"), lax.axis_index("c")
        mine = src.at[layer]
        chips = [(1 - x, y), (x, 1 - y), (1 - x, 1 - y)]

        def slab(px, py, pc):
            return _slab(dst, axis, pl.multiple_of((4 * px + 2 * py + pc) * size, 128), size)

        def copy(k, from_ref, block, to):
            return pltpu.make_async_remote_copy(from_ref, slab(*block), send_sem.at[k], recv_sem.at[k], device_id=to,
                                                device_id_type=MESH)

        own = pltpu.make_async_copy(mine, slab(x, y, c), loc_sem)
        own.start()
        first = [copy(0, mine, (x, y, c), (x, y, 1 - c))]
        first += [copy(1 + j, mine, (x, y, c), (*chip, c)) for j, chip in enumerate(chips)]
        for cp in first:
            cp.start()
        passed = []
        for j, chip in enumerate(chips):
            copy(1 + j, mine, (*chip, c), (x, y, c)).wait_recv()
            fwd = copy(4 + j, slab(*chip, c), (*chip, c), (x, y, 1 - c))
            fwd.start()
            passed.append(fwd)
        copy(0, mine, (x, y, 1 - c), (x, y, c)).wait_recv()
        for j, chip in enumerate(chips):
            copy(4 + j, mine, (*chip, 1 - c), (x, y, c)).wait_recv()
        for cp in first + passed:
            cp.wait_send()
        own.wait()

    return pl.pallas_call(
        body, name=name, in_specs=[ANY], out_specs=ANY, out_shape=jax.ShapeDtypeStruct(full_shape, shards.dtype),
        scratch_shapes=[pltpu.SemaphoreType.DMA((NDEV - 1,)), pltpu.SemaphoreType.DMA((NDEV - 1,)),
                        pltpu.SemaphoreType.DMA(())],
        compiler_params=pltpu.CompilerParams(has_side_effects=True))(shards)


def hosted_call(body, exch, name, grid, in_specs, out_specs, out_shape, scratch_shapes, operands, semantics,
                aliases=None):
    aliases = aliases or {}
    if exch is None:
        res = pl.pallas_call(body, name=name, grid=grid, in_specs=in_specs, out_specs=out_specs, out_shape=out_shape,
                             scratch_shapes=scratch_shapes, input_output_aliases=aliases,
                             compiler_params=_params(semantics))(*operands)
        return list(res), []
    n_in, n_out, n_scr = len(in_specs), len(out_specs), len(scratch_shapes)
    e_in, e_out, e_shape, e_scr = exch.specs()
    ns, nd = len(e_in), len(e_out)

    def at_step(last):
        cond = None
        for ax, n in enumerate(grid):
            c = pl.program_id(ax) == (n - 1 if last else 0)
            cond = c if cond is None else jnp.logical_and(cond, c)
        return cond

    def wrapped(*refs):
        ins, src = refs[:n_in], refs[n_in:n_in + ns]
        o0 = n_in + ns
        outs, dst = refs[o0:o0 + n_out], refs[o0 + n_out:o0 + n_out + nd]
        s0 = o0 + n_out + nd
        scr, sems = refs[s0:s0 + n_scr], refs[s0 + n_scr:]

        @pl.when(at_step(False))
        def _():
            for cp in exch.copies(src, dst, *sems)[0]:
                cp.start()

        body(*ins, *outs, *scr)

        @pl.when(at_step(True))
        def _():
            for cp in exch.copies(src, dst, *sems)[1]:
                cp.wait()

    res = pl.pallas_call(
        wrapped, name=name, grid=grid, in_specs=list(in_specs) + e_in, out_specs=list(out_specs) + e_out,
        out_shape=list(out_shape) + e_shape, scratch_shapes=list(scratch_shapes) + e_scr,
        input_output_aliases=aliases,
        compiler_params=pltpu.CompilerParams(dimension_semantics=("arbitrary",) * len(grid),
                                             vmem_limit_bytes=VMEM_LIMIT, has_side_effects=True),
    )(*operands, *exch.srcs)
    return list(res[:n_out]), list(res[n_out:])


def allreduce_small(name, part):
    rows, width = part.shape

    def body(p_ref, o_ref, buf, send_sem, recv_sem):
        x, y, c = lax.axis_index("x"), lax.axis_index("y"), lax.axis_index("c")
        me = 4 * x + 2 * y + c
        buf[me] = p_ref[...]
        waits = []
        for k in range(1, NDEV):
            px = 1 - x if k & 4 else x
            py = 1 - y if k & 2 else y
            pc = 1 - c if k & 1 else c
            peer = 4 * px + 2 * py + pc
            pltpu.make_async_remote_copy(p_ref, buf.at[me], send_sem.at[k - 1], recv_sem.at[k - 1],
                                         device_id=(px, py, pc), device_id_type=MESH).start()
            waits.append(pltpu.make_async_remote_copy(p_ref, buf.at[peer], send_sem.at[k - 1], recv_sem.at[k - 1],
                                                      device_id=(px, py, pc), device_id_type=MESH))
        for w in waits:
            w.wait()
        acc = buf[0]
        for j in range(1, NDEV):
            acc = acc + buf[j]
        o_ref[...] = acc

    vmem = pl.BlockSpec(memory_space=pltpu.VMEM)
    return pl.pallas_call(
        body, name=name, in_specs=[vmem], out_specs=vmem, out_shape=jax.ShapeDtypeStruct((rows, width), F32),
        scratch_shapes=[pltpu.VMEM((NDEV, rows, width), F32), pltpu.SemaphoreType.DMA((NDEV - 1,)),
                        pltpu.SemaphoreType.DMA((NDEV - 1,))],
        compiler_params=pltpu.CompilerParams(vmem_limit_bytes=VMEM_LIMIT, has_side_effects=True),
    )(part)


def rowwise(name, fn, m, ins=(), consts=(), outs=(), alias_outs=(), accs=(), tm=ROW_TILE):
    tm = min(tm, m)
    n_in, n_c, n_o, n_al, n_ac = len(ins), len(consts), len(outs), len(alias_outs), len(accs)
    held = [a for (a, _, _) in alias_outs if not isinstance(a, jax.ShapeDtypeStruct)]
    n_held = len(held)

    def body(*refs):
        in_refs = refs[:n_in + n_c]
        out_refs = refs[n_in + n_c + n_held:]
        vals = fn(*[r[...] for r in in_refs])
        if not isinstance(vals, (tuple, list)):
            vals = (vals,)
        for r, v in zip(out_refs[:n_o + n_al], vals[:n_o + n_al]):
            r[...] = v.astype(r.dtype)
        if n_ac:
            acc_refs = out_refs[n_o + n_al:]

            @pl.when(pl.program_id(0) == 0)
            def _():
                for r in acc_refs:
                    r[...] = jnp.zeros(r.shape, F32)

            for r, v in zip(acc_refs, vals[n_o + n_al:]):
                r[...] += v

    def col(cb):
        return lambda i: (i, cb)

    in_specs = [pl.BlockSpec((tm, w), col(cb)) for (_, w, cb) in ins]
    in_specs += [pl.BlockSpec(c.shape, lambda i, nd=c.ndim: (0,) * nd) for c in consts]
    in_specs += [ANY for _ in held]
    out_shape = [jax.ShapeDtypeStruct((m, w), dt) for (w, dt) in outs]
    out_specs = [pl.BlockSpec((tm, w), col(0)) for (w, _) in outs]
    out_shape += [jax.ShapeDtypeStruct(a.shape, a.dtype) for (a, _, _) in alias_outs]
    out_specs += [pl.BlockSpec((tm, w), col(cb)) for (_, w, cb) in alias_outs]
    out_shape += [jax.ShapeDtypeStruct(s, F32) for s in accs]
    out_specs += [pl.BlockSpec(s, lambda i: (0, 0)) for s in accs]
    aliases, k_in = {}, n_in + n_c
    for k, (a, _, _) in enumerate(alias_outs):
        if not isinstance(a, jax.ShapeDtypeStruct):
            aliases[k_in] = n_o + k
            k_in += 1
    return pl.pallas_call(
        body, name=name, grid=(m // tm,), in_specs=in_specs, out_specs=out_specs, out_shape=out_shape,
        input_output_aliases=aliases,
        compiler_params=_params(("arbitrary",) if n_ac else ("parallel",)),
    )(*[a for (a, _, _) in ins], *consts, *held)


def _operand(arr, bshape, imap):
    if isinstance(arr, tuple):
        arr, lead = arr
        return arr, pl.BlockSpec((None,) + bshape, lambda *g: (lead,) + imap(*g))
    return arr, pl.BlockSpec(bshape, imap)


def _shape2(arr):
    return arr[0].shape[1:] if isinstance(arr, tuple) else arr.shape


def mm(name, a, b, mode, out_dtype=F32, tm=1024, tn=1024, tk=1024, exch=None):
    sa, sb = _shape2(a), _shape2(b)
    if mode == "nn":
        (M, K), N = sa, sb[1]
    elif mode == "nt":
        (M, K), N = sa, sb[0]
    else:
        (K, M), N = sa, sb[1]
    tm, tn, tk = min(tm, M), min(tn, N), min(tk, K)
    assert M % tm == 0 and N % tn == 0 and K % tk == 0, (name, M, N, K)
    nk = K // tk
    if mode == "nn":
        a_arr, a_spec = _operand(a, (tm, tk), lambda i, j, k: (i, k))
        b_arr, b_spec = _operand(b, (tk, tn), lambda i, j, k: (k, j))
        dot = _dot
    elif mode == "nt":
        a_arr, a_spec = _operand(a, (tm, tk), lambda i, j, k: (i, k))
        b_arr, b_spec = _operand(b, (tn, tk), lambda i, j, k: (j, k))
        dot = _dot_nt
    else:
        a_arr, a_spec = _operand(a, (tk, tm), lambda i, j, k: (k, i))
        b_arr, b_spec = _operand(b, (tk, tn), lambda i, j, k: (k, j))
        dot = _dot_tn

    def body(a_ref, b_ref, o_ref, *acc):
        part = dot(a_ref[...].astype(BF16), b_ref[...].astype(BF16))
        if nk == 1:
            o_ref[...] = part.astype(o_ref.dtype)
            return
        acc_ref, k = acc[0], pl.program_id(2)

        @pl.when(k == 0)
        def _():
            acc_ref[...] = part

        @pl.when(k > 0)
        def _():
            acc_ref[...] += part

        @pl.when(k == nk - 1)
        def _():
            o_ref[...] = acc_ref[...].astype(o_ref.dtype)

    outs, extra = hosted_call(
        body, exch, name, (M // tm, N // tn, nk), [a_spec, b_spec], [pl.BlockSpec((tm, tn), lambda i, j, k: (i, j))],
        [jax.ShapeDtypeStruct((M, N), out_dtype)], [pltpu.VMEM((tm, tn), F32)] if nk > 1 else [], [a_arr, b_arr],
        ("parallel", "parallel", "arbitrary"))
    return outs[0] if exch is None else (outs[0], extra)


def mm_fused(name, m, a_ins, bs, mode, kdim, prologue=None, a_outs=(), e_ins=(), consts=(), epilogue=None, outs=(),
             alias_outs=(), accs=(), a_to_epilogue=(), a_consts=(), tm=512, tk=1024, resident=False, exch=None):
    tm = min(tm, m)
    nk = kdim // tk
    assert nk == 1 or not a_to_epilogue
    n = bs[0][0].shape[1 if mode == "nn" else 0]
    b_arrays = []
    for b_, _ in bs:
        if not (resident and any(b_ is u for u in b_arrays)):
            b_arrays.append(b_)
    b_of_pair = [next(j for j, u in enumerate(b_arrays) if u is b_) if resident else j for j, (b_, _) in enumerate(bs)]
    n_a, n_b, n_e, n_c = len(a_ins) + len(a_consts), len(b_arrays), len(e_ins), len(consts)
    n_ao, n_o, n_al, n_ac = len(a_outs), len(outs), len(alias_outs), len(accs)
    held = [a for (a, _, _) in alias_outs if not isinstance(a, jax.ShapeDtypeStruct)]
    dot = _dot if mode == "nn" else _dot_nt

    def body(*refs):
        a_refs, b_refs = refs[:n_a], refs[n_a:n_a + n_b]
        e_refs = refs[n_a + n_b:n_a + n_b + n_e + n_c]
        o0 = n_a + n_b + n_e + n_c + len(held)
        ao_refs = refs[o0:o0 + n_ao]
        out_refs = refs[o0 + n_ao:o0 + n_ao + n_o + n_al]
        acc_refs = refs[o0 + n_ao + n_o + n_al:o0 + n_ao + n_o + n_al + n_ac]
        scr = refs[o0 + n_ao + n_o + n_al + n_ac:]
        i, k = pl.program_id(0), pl.program_id(1)
        ck = tk if prologue is None else min(tk, PROLOGUE_CHUNK)
        part = None
        for c0 in range(0, tk, ck):
            cols = slice(c0, c0 + ck)
            tiles = [r[:, cols] for r in a_refs]
            a_list, extra = (tiles, []) if prologue is None else prologue(*tiles)
            for r, v in zip(ao_refs, extra):
                r[:, cols] = v.astype(r.dtype)
            for a, j_b, (_, off) in zip(a_list, b_of_pair, bs):
                b_ref = b_refs[j_b]
                if resident:
                    b = b_ref[pl.ds(pl.multiple_of((k + off) * tk + c0, ck), ck), :]
                else:
                    b = b_ref[cols, :] if mode == "nn" else b_ref[:, cols]
                prod = dot(a.astype(BF16), b.astype(BF16))
                part = prod if part is None else part + prod

        def finish(total):
            vals = epilogue(total, *[a_refs[j][...] for j in a_to_epilogue], *[r[...] for r in e_refs])
            if not isinstance(vals, (tuple, list)):
                vals = (vals,)
            for r, v in zip(out_refs, vals[:n_o + n_al]):
                r[...] = v.astype(r.dtype)
            for r, v in zip(acc_refs, vals[n_o + n_al:]):
                @pl.when(i == 0)
                def _():
                    r[...] = v

                @pl.when(i > 0)
                def _():
                    r[...] += v

        if nk == 1:
            finish(part)
            return
        acc_ref = scr[0]

        @pl.when(k == 0)
        def _():
            acc_ref[...] = part

        @pl.when(k > 0)
        def _():
            acc_ref[...] += part

        @pl.when(k == nk - 1)
        def _():
            finish(acc_ref[...])

    in_specs = [pl.BlockSpec((tm, tk), lambda i, k, off=off: (i, k + off)) for (_, off) in a_ins]
    in_specs += [pl.BlockSpec((1, tk), lambda i, k: (0, k)) for _ in a_consts]
    if resident:
        assert mode == "nn"
        in_specs += [pl.BlockSpec(b.shape, lambda i, k: (0, 0), pipeline_mode=pl.Buffered(1)) for b in b_arrays]
    elif mode == "nn":
        in_specs += [pl.BlockSpec((tk, n), lambda i, k, off=off: (k + off, 0)) for (_, off) in bs]
    else:
        in_specs += [pl.BlockSpec((n, tk), lambda i, k, off=off: (0, k + off)) for (_, off) in bs]
    in_specs += [pl.BlockSpec((tm, w), lambda i, k, cb=cb: (i, cb)) for (_, w, cb) in e_ins]
    in_specs += [pl.BlockSpec(c.shape, lambda i, k, nd=c.ndim: (0,) * nd) for c in consts]
    in_specs += [ANY for _ in held]
    out_shape = [jax.ShapeDtypeStruct((m, kdim), dt) for dt in a_outs]
    out_specs = [pl.BlockSpec((tm, tk), lambda i, k: (i, k)) for _ in a_outs]
    out_shape += [jax.ShapeDtypeStruct((m, w), dt) for (w, dt) in outs]
    out_specs += [pl.BlockSpec((tm, w), lambda i, k: (i, 0)) for (w, _) in outs]
    out_shape += [jax.ShapeDtypeStruct(a.shape, a.dtype) for (a, _, _) in alias_outs]
    out_specs += [pl.BlockSpec((tm, w), lambda i, k, cb=cb: (i, cb)) for (_, w, cb) in alias_outs]
    out_shape += [jax.ShapeDtypeStruct(s_, F32) for s_ in accs]
    out_specs += [pl.BlockSpec(s_, lambda i, k: (0, 0)) for s_ in accs]
    aliases, k_in = {}, n_a + n_b + n_e + n_c
    for j, (a, _, _) in enumerate(alias_outs):
        if not isinstance(a, jax.ShapeDtypeStruct):
            aliases[k_in] = n_ao + n_o + j
            k_in += 1
    operands = [a for (a, _) in a_ins] + list(a_consts) + b_arrays + [a for (a, _, _) in e_ins] + list(consts) + held
    res, extra = hosted_call(
        body, exch, name, (m // tm, nk), in_specs, out_specs, out_shape,
        [pltpu.VMEM((tm, n), F32)] if nk > 1 else [], operands,
        ("arbitrary" if n_ac else "parallel", "arbitrary"), aliases)
    return res if exch is None else (res, extra)


def _cumsum_rows(x):
    n = x.shape[0]
    row = lax.broadcasted_iota(jnp.int32, x.shape, 0)
    s = 1
    while s < n:
        x = x + jnp.where(row >= s, pltpu.roll(x, s, 0), 0.0)
        s *= 2
    return x


def _hg_prep(zq, zf, lb, reverse, b=None):
    n = zq.shape[0]
    q = _silu(zq)
    sig = _sigmoid(zf)
    sn = 1.0 - sig
    f = lb + (1.0 - lb) * sig
    k = (1.0 - lb) * sn
    if b is None:
        g = jnp.log(jnp.maximum(f, TINY))
        b = _cumsum_rows(g)
        if reverse:
            b = b[n - 1:n] - b + g
    b_last = b[0:1] if reverse else b[n - 1:n]
    b_ref = b[n // 2:n // 2 + 1]
    e1 = jnp.exp(b)
    e2 = jnp.exp(jnp.clip(b - b_ref, -EXP_CLAMP, EXP_CLAMP))
    e3 = jnp.exp(jnp.clip(b_ref - b, -EXP_CLAMP, EXP_CLAMP))
    e4 = jnp.exp(b_last - b)
    return dict(q=q, k=k, sig=sig, sn=sn, f=f, b=b, e1=e1, e2=e2, e3=e3, e4=e4, e_last=jnp.exp(b_last),
                qe=(q * e1).astype(BF16), qt=(q * e2).astype(BF16), kt=(k * e3).astype(BF16),
                ks=(k * e4).astype(BF16))


def _hg_mask(n, reverse):
    t = lax.broadcasted_iota(jnp.int32, (n, n), 0)
    s = lax.broadcasted_iota(jnp.int32, (n, n), 1)
    return (s >= t) if reverse else (s <= t)


def hgrn_fwd(name, z, lb_f, lb_b, exch=None, unroll=False):
    m = z.shape[0]
    C, T = HG_CHUNK, min(HG_BLOCK_FWD, m)
    nb, cpb = m // T, T // C

    def body(zq_f, zf_f, zi_f, zq_b, zf_b, zi_b, lbf_ref, lbb_ref, of_ref, ob_ref, sf_ref, sb_ref, bf_ref, bb_ref,
             st_ref):
        @pl.when(pl.program_id(0) == 0)
        def _():
            st_ref[...] = jnp.zeros(st_ref.shape, F32)

        dirs = ((zq_f, zf_f, zi_f, lbf_ref, of_ref, sf_ref), (zq_b, zf_b, zi_b, lbb_ref, ob_ref, sb_ref))
        b_refs = (bf_ref, bb_ref)

        def chunk(ci, carry):
            work = []
            for d, (zq, zf, zi, lb_ref, o_ref, s_ref) in enumerate(dirs):
                cc = ci if d == 0 else cpb - 1 - ci
                rows = pl.ds(pl.multiple_of(cc * C, C), C)
                pre = _hg_prep(zq[rows, :], zf[rows, :], lb_ref[...], d == 1)
                v = zi[rows, :].astype(BF16)
                work.append((cc, rows, pre, v, [st_ref[d, h] for h in range(HEADS)]))
            heads = [(d, h, slice(h * HEAD_DIM, (h + 1) * HEAD_DIM)) for d in range(2) for h in range(HEADS)]
            first = {}
            for d, h, sl in heads:
                _, _, pre, v, sts = work[d]
                first[d, h] = (_dot_nt(pre["qt"][:, sl], pre["kt"][:, sl]),
                               _dot_nt(pre["qe"][:, sl], sts[h].astype(BF16)),
                               _dot_tn(v[:, sl], pre["ks"][:, sl]))
            results = [([], []), ([], [])]
            for d, h, sl in heads:
                _, _, pre, v, sts = work[d]
                scores, o_inter, st_add = first[d, h]
                a = jnp.where(_hg_mask(C, d == 1), scores, 0.0).astype(BF16)
                results[d][0].append(o_inter + _dot(a, v[:, sl]))
                results[d][1].append(sts[h] * pre["e_last"][:, sl] + st_add)
            results = [(jnp.concatenate(o_parts, axis=1), new_sts) for (o_parts, new_sts) in results]
            for d, (zq, zf, zi, lb_ref, o_ref, s_ref) in enumerate(dirs):
                cc, rows, pre, _, sts = work[d]
                o_ref[rows, :] = results[d][0]
                b_refs[d][rows, :] = pre["b"]
                for h in range(HEADS):
                    s_ref[cc, h] = sts[h]
                    st_ref[d, h] = results[d][1][h]
            return carry

        lax.fori_loop(0, cpb, chunk, 0, unroll=unroll)

    def zspec(cb, rev):
        return pl.BlockSpec((T, D), (lambda i: (nb - 1 - i, cb)) if rev else (lambda i: (i, cb)))

    def sspec(rev):
        shape = (cpb, HEADS, HEAD_DIM, HEAD_DIM)
        return pl.BlockSpec(shape, (lambda i: (nb - 1 - i, 0, 0, 0)) if rev else (lambda i: (i, 0, 0, 0)))

    lbspec = pl.BlockSpec((1, D), lambda i: (0, 0))
    states = jax.ShapeDtypeStruct((m // C, HEADS, HEAD_DIM, HEAD_DIM), F32)
    outs, extra = hosted_call(
        body, exch, name, (nb,),
        [zspec(ZQ, False), zspec(ZFF, False), zspec(ZI, False), zspec(ZQ, True), zspec(ZFB, True), zspec(ZI, True),
         lbspec, lbspec],
        [zspec(0, False), zspec(0, True), sspec(False), sspec(True), zspec(0, False), zspec(0, True)],
        [jax.ShapeDtypeStruct((m, D), F32), jax.ShapeDtypeStruct((m, D), F32), states, states,
         jax.ShapeDtypeStruct((m, D), F32), jax.ShapeDtypeStruct((m, D), F32)],
        [pltpu.VMEM((2, HEADS, HEAD_DIM, HEAD_DIM), F32)], [z, z, z, z, z, z, lb_f, lb_b], ("arbitrary",))
    return outs, extra


def hgrn_bwd(name, z, d_o, s_f, s_b, b_f, b_b, lb_f, lb_b, exch=None, unroll=False):
    m = z.shape[0]
    C, T = HG_CHUNK, min(HG_BLOCK_BWD, m)
    nb, cpb = m // T, T // C

    def body(zq_f, zf_f, zi_f, do_f, sf_ref, zq_b, zf_b, zi_b, do_b, sb_ref, lbf_ref, lbb_ref, bf_ref, bb_ref,
             dqf_ref, dvf_ref, dqb_ref, dvb_ref, dzf_f, dzf_b, dlbf_ref, dlbb_ref,
             dst_ref):
        b_refs = (bf_ref, bb_ref)
        @pl.when(pl.program_id(0) == 0)
        def _():
            dst_ref[...] = jnp.zeros(dst_ref.shape, F32)
            dlbf_ref[...] = jnp.zeros(dlbf_ref.shape, F32)
            dlbb_ref[...] = jnp.zeros(dlbb_ref.shape, F32)

        dirs = ((zq_f, zf_f, zi_f, do_f, sf_ref, lbf_ref, dqf_ref, dvf_ref, dzf_f, dlbf_ref),
                (zq_b, zf_b, zi_b, do_b, sb_ref, lbb_ref, dqb_ref, dvb_ref, dzf_b, dlbb_ref))

        def chunk(ci, carry):
            work = []
            for d, (zq, zf, zi, do_ref, s_ref, lb_ref, dq_ref, dv_ref, dzf_ref, dlb_ref) in enumerate(dirs):
                cc = cpb - 1 - ci if d == 0 else ci
                rows = pl.ds(pl.multiple_of(cc * C, C), C)
                lb = lb_ref[...]
                pre = _hg_prep(zq[rows, :], zf[rows, :], lb, d == 1, b=b_refs[d][rows, :])
                work.append((rows, lb, pre, zi[rows, :].astype(BF16), do_ref[rows, :],
                             [s_ref[cc, h] for h in range(HEADS)], [dst_ref[d, h] for h in range(HEADS)],
                             dlb_ref[...]))
            heads = [(d, h, slice(h * HEAD_DIM, (h + 1) * HEAD_DIM)) for d in range(2) for h in range(HEADS)]
            first = {}
            for d, h, sl in heads:
                _, _, pre, v, do, st_prevs, dsts, _ = work[d]
                dst16 = dsts[h].astype(BF16)
                first[d, h] = (_dot_nt(pre["qt"][:, sl], pre["kt"][:, sl]),
                               _dot_nt(do[:, sl], v[:, sl]),
                               _dot(do[:, sl], st_prevs[h].astype(BF16)),
                               _dot(v[:, sl], dst16),
                               _dot_nt(pre["ks"][:, sl], dst16),
                               _dot_tn(do[:, sl], pre["qe"][:, sl]))
            parts = [[[] for _ in range(6)] for _ in range(2)]
            for d, h, sl in heads:
                _, _, pre, v, do, st_prevs, dsts, _ = work[d]
                scores, dscores, dq_inter, dk_state, dv_state, dst_add = first[d, h]
                mask = _hg_mask(C, d == 1)
                a = jnp.where(mask, scores, 0.0).astype(BF16)
                da = jnp.where(mask, dscores, 0.0).astype(BF16)
                dq_p, dki_p, dks_p, dv_p, rr_p, new_dsts = parts[d]
                dq_p.append(_dot(da, pre["kt"][:, sl]) * pre["e2"][:, sl] + dq_inter * pre["e1"][:, sl])
                dki_p.append(_dot_tn(da, pre["qt"][:, sl]) * pre["e3"][:, sl])
                dks_p.append(dk_state * pre["e4"][:, sl])
                dv_p.append(_dot_tn(a, do[:, sl]) + dv_state)
                rr_p.append(pre["e_last"][:, sl] * _colsum(dsts[h] * st_prevs[h]))
                new_dsts.append(dsts[h] * pre["e_last"][:, sl] + dst_add)
            results = []
            for d, (rows, lb, pre, v, do, st_prevs, dsts, dlb_old) in enumerate(work):
                rev = d == 1
                dq_p, dki_p, dks_p, dv_p, rr_p, new_dsts = parts[d]
                dq, dki, dks, dv, rr = (jnp.concatenate(p_, axis=1) for p_ in (dq_p, dki_p, dks_p, dv_p, rr_p))
                x = pre["q"] * dq - pre["k"] * dki
                y = pre["k"] * dks
                if rev:
                    dg = _cumsum_rows(x - y) + _colsum(y) + rr
                else:
                    dg = _cumsum_rows(y - x) + (x - y) + _colsum(x) + rr
                inv_f = jnp.where(pre["f"] > TINY, 1.0 / pre["f"], 0.0)
                u = dg * inv_f - (dki + dks)
                results.append((dq, dv, (1.0 - lb) * pre["sig"] * pre["sn"] * u, dlb_old + _colsum(pre["sn"] * u),
                                new_dsts))
            for d, (zq, zf, zi, do_ref, s_ref, lb_ref, dq_ref, dv_ref, dzf_ref, dlb_ref) in enumerate(dirs):
                rows = work[d][0]
                dq, dv, dzf, dlb, new_dsts = results[d]
                dq_ref[rows, :] = dq.astype(dq_ref.dtype)
                dv_ref[rows, :] = dv.astype(dv_ref.dtype)
                dzf_ref[rows, :] = dzf.astype(dzf_ref.dtype)
                dlb_ref[...] = dlb
                for h in range(HEADS):
                    dst_ref[d, h] = new_dsts[h]
            return carry

        lax.fori_loop(0, cpb, chunk, 0, unroll=unroll)

    def rspec(cb, rev):
        return pl.BlockSpec((T, D), (lambda i: (i, cb)) if rev else (lambda i: (nb - 1 - i, cb)))

    def sspec(rev):
        shape = (cpb, HEADS, HEAD_DIM, HEAD_DIM)
        return pl.BlockSpec(shape, (lambda i: (i, 0, 0, 0)) if rev else (lambda i: (nb - 1 - i, 0, 0, 0)))

    lbspec = pl.BlockSpec((1, D), lambda i: (0, 0))
    half = jax.ShapeDtypeStruct((m, D), BF16)
    row = jax.ShapeDtypeStruct((1, D), F32)
    outs, extra = hosted_call(
        body, exch, name, (nb,),
        [rspec(ZQ, False), rspec(ZFF, False), rspec(ZI, False), rspec(0, False), sspec(False),
         rspec(ZQ, True), rspec(ZFB, True), rspec(ZI, True), rspec(0, True), sspec(True), lbspec, lbspec,
         rspec(0, False), rspec(0, True)],
        [rspec(0, False), rspec(0, False), rspec(0, True), rspec(0, True), rspec(0, False), rspec(0, True),
         lbspec, lbspec],
        [half, half, half, half, half, half, row, row],
        [pltpu.VMEM((2, HEADS, HEAD_DIM, HEAD_DIM), F32)],
        [z, z, z, d_o, s_f, z, z, z, d_o, s_b, lb_f, lb_b, b_f, b_b], ("arbitrary",))
    return outs, extra


def _heads(fn, *arrs):
    res = [fn(*[a[:, h * HEAD_DIM:(h + 1) * HEAD_DIM] for a in arrs]) for h in range(arrs[0].shape[1] // HEAD_DIM)]
    return [jnp.concatenate(parts, axis=1) for parts in zip(*res)]


def _hg_post(o_f, o_b, zg, g):
    def head(of, ob, zgh, gh):
        on, _ = _rms(of + ob)
        return (on * gh * _silu(zgh),)
    return _heads(head, o_f, o_b, zg, g)[0]


def _hg_post_bwd(da, o_f, o_b, zg, g):
    def head(dah, of, ob, zgh, gh):
        on, r = _rms(of + ob)
        sg = _silu(zgh)
        d_on = dah * sg
        return _rms_bwd(d_on, on, r, gh), dah * on * gh * _silu_grad(zgh), d_on * on
    d_o, dzg, dg = _heads(head, da, o_f, o_b, zg, g)
    return d_o, dzg, _colsum(dg)


def _sg_parts(zv, ln_g, ln_b):
    vg = _gelu(zv)
    xc = vg - _mean(vg)
    rstd = lax.rsqrt(_mean(xc * xc) + EPS)
    vh = xc * rstd
    return vh, rstd, vh * ln_g + ln_b


def _sg_lane_group(shape):
    return lax.broadcasted_iota(jnp.int32, shape, 1) < SG_GROUP_DIM


def _sg_mix(w, v16, transpose):
    rows = v16.shape[0]
    out = []
    for c in range(rows // SG_CHUNK):
        parts = []
        for j in range(SG_WIDTH // 128):
            vj = v16[c * SG_CHUNK:(c + 1) * SG_CHUNK, j * 128:(j + 1) * 128]
            w0 = w[(2 * j) * SG_CHUNK:(2 * j + 1) * SG_CHUNK]
            w1 = w[(2 * j + 1) * SG_CHUNK:(2 * j + 2) * SG_CHUNK]
            dot = _dot_tn if transpose else _dot
            parts.append(jnp.where(_sg_lane_group((SG_CHUNK, 128)), dot(w0, vj), dot(w1, vj)))
        out.append(jnp.concatenate(parts, axis=1))
    return jnp.concatenate(out, axis=0)


def _sg_fwd(zu, zv, w, bias, ln_g, ln_b):
    _, _, v = _sg_parts(zv, ln_g, ln_b)
    reps = zu.shape[0] // SG_CHUNK
    return _gelu(zu) * (_sg_mix(w, v.astype(BF16), False) + jnp.concatenate([bias] * reps, axis=0))


def _sg_bwd(db, zu, zv, w, bias, ln_g, ln_b):
    vh, rstd, v = _sg_parts(zv, ln_g, ln_b)
    v16 = v.astype(BF16)
    reps = zu.shape[0] // SG_CHUNK
    sg = _sg_mix(w, v16, False) + jnp.concatenate([bias] * reps, axis=0)
    dzu = db * sg * _gelu_grad(zu)
    dsg = db * _gelu(zu)
    dsg16 = dsg.astype(BF16)
    dv = _sg_mix(w, dsg16, True)
    low = _sg_lane_group((SG_CHUNK, 128))
    dw = []
    for g in range(SG_WIDTH // SG_GROUP_DIM):
        j, keep = g // 2, (low if g % 2 == 0 else jnp.logical_not(low))
        acc = jnp.zeros((SG_CHUNK, SG_CHUNK), F32)
        for c in range(reps):
            rows = slice(c * SG_CHUNK, (c + 1) * SG_CHUNK)
            dj = jnp.where(keep, dsg16[rows, j * 128:(j + 1) * 128], jnp.zeros((), BF16))
            acc = acc + _dot_nt(dj, v16[rows, j * 128:(j + 1) * 128])
        dw.append(acc)
    dbias = sum(dsg[c * SG_CHUNK:(c + 1) * SG_CHUNK] for c in range(reps))
    dvh = dv * ln_g
    dvg = rstd * (dvh - _mean(dvh) - vh * _mean(dvh * vh))
    dzuv = jnp.concatenate([dzu, dvg * _gelu_grad(zv)], axis=1)
    return (dzuv, jnp.concatenate(dw, axis=0), dbias, _colsum(dv * vh), _colsum(dv))


def lower_bounds(name, gamma_f, gamma_b):
    def body(gf_ref, gb_ref, lf_ref, lb_ref):
        for g_ref, o_ref in ((gf_ref, lf_ref), (gb_ref, lb_ref)):
            g0, g1 = g_ref[0:1, :], g_ref[1:2, :]
            mx = jnp.maximum(g0, g1)
            e0, e1 = jnp.exp(g0 - mx), jnp.exp(g1 - mx)
            sm0, sm1 = e0 / (e0 + e1), e1 / (e0 + e1)
            o_ref[0:1, :] = sm0 - sm0
            o_ref[1:2, :] = (sm0 + sm1) - sm0
    shp = jax.ShapeDtypeStruct(gamma_f.shape, F32)
    return pl.pallas_call(body, name=name, out_shape=[shp, shp])(gamma_f, gamma_b)


def lower_bounds_bwd(name, gamma_f, gamma_b, dlb_f, dlb_b):
    def body(gf_ref, gb_ref, df_ref, db_ref, of_ref, ob_ref):
        for g_ref, d_ref, o_ref in ((gf_ref, df_ref, of_ref), (gb_ref, db_ref, ob_ref)):
            g0, g1 = g_ref[0:1, :], g_ref[1:2, :]
            mx = jnp.maximum(g0, g1)
            e0, e1 = jnp.exp(g0 - mx), jnp.exp(g1 - mx)
            sm0, sm1 = e0 / (e0 + e1), e1 / (e0 + e1)
            d1 = d_ref[1:2, :] * sm0 * sm1
            o_ref[0:1, :] = -d1
            o_ref[1:2, :] = d1
    shp = jax.ShapeDtypeStruct(gamma_f.shape, F32)
    return pl.pallas_call(body, name=name, out_shape=[shp, shp])(gamma_f, gamma_b, dlb_f, dlb_b)


def _row(a, l):
    return a[l:l + 1]


class LocalPlan:
    def __init__(self, weights):
        self.W = weights
        self.grads = [dict() for _ in range(DEPTH)]

    def exch(self, host):
        return None

    def done(self, host, outs):
        pass

    def early_small(self, packed):
        pass


def local_step(x, p, target, S, plan):
    m = x.shape[0]

    def hmm(tag, *args, **kw):
        ex = plan.exch(tag)
        res = mm(tag, *args, exch=ex, **kw)
        if ex is None:
            return res
        plan.done(tag, res[1])
        return res[0]

    lb_f, lb_b = lower_bounds("lower_bounds", S["lb_gamma_fwd"], S["lb_gamma_bwd"])
    saved = []
    for l in range(DEPTH):
        t = f"l{l}_"
        W = plan.W[l]
        tm = 2048
        in_tile = (1024, 2048)
        ffn_tile = (2048, 2048)
        g_pre, g_post = _row(S["norm_mix_pre"], l), _row(S["norm_mix_post"], l)
        g_fpre, g_fpost = _row(S["norm_ffn_pre"], l), _row(S["norm_ffn_post"], l)
        hg_g = _row(S["hg_norm"], l)
        sg_w = S["sg_w"][l].reshape(SG_WIDTH // SG_GROUP_DIM * SG_CHUNK, SG_CHUNK).astype(BF16)
        sg_bias = jnp.repeat(S["sg_b"][l].T, SG_GROUP_DIM, axis=1)
        ln_g, ln_b = _row(S["sg_ln_g"], l), _row(S["sg_ln_b"], l)
        lbf, lbb = _row(lb_f, l), _row(lb_b, l)

        if l == 0:
            (h,) = rowwise(t + "pre_norm", lambda xv, g: (_rms(xv)[0] * g,), m, ins=[(x, D, 0)], consts=[g_pre],
                           outs=[(D, BF16)])
        z = hmm(t + "in_proj", h, W["w_in"], "nn", tm=in_tile[0], tn=in_tile[1])
        (o_f, o_b, s_f, s_b, b_f, b_b), extra = hgrn_fwd(t + "hgrn_fwd", z, lbf, lbb, exch=plan.exch(t + "hgrn_fwd"))
        plan.done(t + "hgrn_fwd", extra)
        (b_out,) = rowwise(t + "sgu_fwd", _sg_fwd, m, ins=[(z, SG_WIDTH, ZU), (z, SG_WIDTH, ZV)],
                           consts=[sg_w, sg_bias, ln_g, ln_b], outs=[(SG_WIDTH, BF16)])

        def post_pro(of, ob, zg, g):
            ao = _hg_post(of, ob, zg, g).astype(BF16)
            return [ao], [ao]
        a_out, pa = mm_fused(t + "proj_a", m, [(o_f, 0), (o_b, 0), (z, ZG)], [(W["w_a"], 0)], "nn", D,
                             prologue=post_pro, a_outs=[BF16], a_consts=[hg_g], epilogue=lambda tot: (tot,),
                             outs=[(D, BF16)])
        pb = mm(t + "proj_b", b_out, W["w_b"], "nn", BF16)

        def merge_pro(a, b, ga, gb):
            mg = (_sigmoid(ga) * a + _sigmoid(gb) * b).astype(BF16)
            return [mg], [mg]

        def post_pre(mixv, xv, gp, gf):
            x1 = xv + _rms(mixv)[0] * gp
            return mixv, x1, _rms(x1)[0] * gf
        merged, mix, x1, h2 = mm_fused(
            t + "out_proj", m, [(pa, 0), (pb, 0), (z, GA), (z, GB)], [(W["w_out"], 0)], "nn", D, prologue=merge_pro,
            a_outs=[BF16], e_ins=[(x, D, 0)], consts=[g_post, g_fpre], epilogue=post_pre,
            outs=[(D, F32), (D, F32), (D, BF16)])
        gu = hmm(t + "ffn_in", h2, W["w_gu"], "nn", BF16, tm=ffn_tile[0], tn=ffn_tile[1])

        def act_pro(gt, up):
            hd = (_silu(gt.astype(F32)) * up).astype(BF16)
            return [hd], [hd]
        hid, ff, x2 = mm_fused(
            t + "ffn_out", m, [(gu, 0), (gu, FFN_PAD // 1024)], [(W["w_down"], 0)], "nn", FFN_PAD, prologue=act_pro,
            a_outs=[BF16], e_ins=[(x1, D, 0)], consts=[g_fpost],
            epilogue=lambda f, xv, g: (f, xv + _rms(f)[0] * g), outs=[(D, F32), (D, F32)], tm=1024, resident=True)
        e = mm(t + "ple_proj", (p, l), W["w_ple"], "nn")

        if l + 1 < DEPTH:
            def ple_add(tv, xv, ev, g):
                x3 = xv + ev * _sigmoid(tv)
                return tv, x3, _rms(x3)[0] * g
            tg, x3, h_next = mm_fused(
                t + "ple_gate", m, [(x2, 0)], [(W["w_ple_gate"], 0)], "nn", D, a_to_epilogue=(0,), e_ins=[(e, D, 0)],
                consts=[_row(S["norm_mix_pre"], l + 1)], epilogue=ple_add, outs=[(D, F32), (D, F32), (D, BF16)])
        else:
            def ple_loss(tv, xv, ev, tgt):
                err = xv + ev * _sigmoid(tv) - tgt
                return tv, err * (1.0 / D), _colsum(err * err)
            tg, x3, loss_cols = mm_fused(
                t + "ple_gate", m, [(x2, 0)], [(W["w_ple_gate"], 0)], "nn", D, a_to_epilogue=(0,),
                e_ins=[(e, D, 0), (target, D, 0)], epilogue=ple_loss, outs=[(D, F32), (D, F32)], accs=[(1, D)])
            h_next = None
        saved.append(dict(x=x, h=h, z=z, o_f=o_f, o_b=o_b, s_f=s_f, s_b=s_b, b_f=b_f, b_b=b_b, a_out=a_out,
                          b_out=b_out, pa=pa, pb=pb,
                          merged=merged, mix=mix, x1=x1, h2=h2, gu=gu, hid=hid, ff=ff, x2=x2, e=e, tg=tg,
                          sg_w=sg_w, sg_bias=sg_bias))
        x, h = x3, h_next

    dx = x

    gs = {n: [None] * DEPTH for n in SMALL}
    dlb_f, dlb_b = [None] * DEPTH, [None] * DEPTH

    for l in reversed(range(DEPTH)):
        t = f"l{l}_bwd_"
        sv, W = saved[l], plan.W[l]
        tm, tk = 2048, 4096
        g_pre, g_post = _row(S["norm_mix_pre"], l), _row(S["norm_mix_post"], l)
        g_fpre, g_fpost = _row(S["norm_ffn_pre"], l), _row(S["norm_ffn_post"], l)
        hg_g = _row(S["hg_norm"], l)
        ln_g, ln_b = _row(S["sg_ln_g"], l), _row(S["sg_ln_b"], l)
        lbf, lbb = _row(lb_f, l), _row(lb_b, l)

        def wgrad(nm, tag, a, b):
            a_dtype = (a[0] if isinstance(a, tuple) else a).dtype
            plan.grads[l][nm] = mm(tag, a, b, "tn", BF16, tk=tk if a_dtype == BF16 else 2048)

        def ple_pro(d3, ev, tv):
            s = _sigmoid(tv)
            de_, dt_ = (d3 * s).astype(BF16), (d3 * ev * s * (1.0 - s)).astype(BF16)
            return [dt_], [dt_, de_]

        def ffn_post_bwd(d2p, d3, f, g):
            d2 = d3 + d2p
            fh, r = _rms(f)
            return d2, _rms_bwd(d2, fh, r, g), _colsum(d2 * fh)
        dt, de, dx2, dff, gs["norm_ffn_post"][l] = mm_fused(
            t + "ple_gate_dx", m, [(dx, 0), (sv["e"], 0), (sv["tg"], 0)], [(W["w_ple_gate"], 0)], "nt", D,
            prologue=ple_pro, a_outs=[BF16, BF16], a_to_epilogue=(0,), e_ins=[(sv["ff"], D, 0)], consts=[g_fpost],
            epilogue=ffn_post_bwd, outs=[(D, F32), (D, BF16)], accs=[(1, D)])
        wgrad("w_ple", t + "w_ple", (p, l), de)
        wgrad("w_ple_gate", t + "w_ple_gate", sv["x2"], dt)
        wgrad("w_down", t + "w_down", sv["hid"], dff)
        dhid = mm(t + "ffn_out_dx", dff, W["w_down"], "nt", BF16, tm=tm)

        def act_bwd(dh, gt, up):
            dh, gt = dh.astype(F32), gt.astype(F32)
            s = _sigmoid(gt)
            dg_ = (dh * up * (s * (1.0 + gt * (1.0 - s)))).astype(BF16)
            du_ = (dh * (gt * s)).astype(BF16)
            return [dg_, du_], [dg_, du_]

        def pre_post_bwd(dh, d2, x1v, mixv, gf, gp):
            xh, r1 = _rms(x1v)
            d1 = d2 + _rms_bwd(dh, xh, r1, gf)
            mh, rm = _rms(mixv)
            return d1, _rms_bwd(d1, mh, rm, gp), _colsum(dh * xh), _colsum(d1 * mh)
        off = FFN_PAD // 1024
        w_gu_t = W["w_gu"].T
        dgate, dup, dx1, dmix, gs["norm_ffn_pre"][l], gs["norm_mix_post"][l] = mm_fused(
            t + "ffn_in_dx", m, [(dhid, 0), (sv["gu"], 0), (sv["gu"], off)], [(w_gu_t, 0), (w_gu_t, off)], "nn",
            FFN_PAD, prologue=act_bwd, a_outs=[BF16, BF16], e_ins=[(dx2, D, 0), (sv["x1"], D, 0), (sv["mix"], D, 0)],
            consts=[g_fpre, g_post], epilogue=pre_post_bwd, outs=[(D, F32), (D, BF16)], accs=[(1, D), (1, D)],
            resident=True)
        wgrad("w_gate", t + "w_gate", sv["h2"], dgate)
        wgrad("w_up", t + "w_up", sv["h2"], dup)
        wgrad("w_out", t + "w_out", sv["merged"], dmix)

        def merge_bwd(dm, a, b, gab):
            sa, sb = _sigmoid(gab[:, :D]), _sigmoid(gab[:, D:])
            dgab = jnp.concatenate([dm * a * sa * (1.0 - sa), dm * b * sb * (1.0 - sb)], axis=1)
            return dm * sa, dm * sb, dgab
        dpa, dpb, dz = mm_fused(
            t + "out_proj_dx", m, [(dmix, 0)], [(W["w_out"], 0)], "nt", D,
            e_ins=[(sv["pa"], D, 0), (sv["pb"], D, 0), (sv["z"], 2 * D, 3)], epilogue=merge_bwd,
            outs=[(D, BF16), (D, BF16)], alias_outs=[(jax.ShapeDtypeStruct((m, N_IN), BF16), 2 * D, 3)])
        wgrad("w_a", t + "w_a", sv["a_out"], dpa)
        wgrad("w_b", t + "w_b", sv["b_out"], dpb)
        db = mm(t + "proj_b_dx", dpb, W["w_b"], "nt")

        dz, dsw, dbias, gs["sg_ln_g"][l], gs["sg_ln_b"][l] = rowwise(
            t + "sgu", _sg_bwd, m, ins=[(db, SG_WIDTH, 0), (sv["z"], SG_WIDTH, ZU), (sv["z"], SG_WIDTH, ZV)],
            consts=[sv["sg_w"], sv["sg_bias"], ln_g, ln_b], alias_outs=[(dz, 2 * SG_WIDTH, 5)],
            accs=[(SG_WIDTH // SG_GROUP_DIM * SG_CHUNK, SG_CHUNK), (SG_CHUNK, SG_WIDTH), (1, SG_WIDTH), (1, SG_WIDTH)])
        gs["sg_w"][l] = dsw.reshape(1, SG_WIDTH // SG_GROUP_DIM, SG_CHUNK, SG_CHUNK)
        gs["sg_b"][l] = dbias.reshape(SG_CHUNK, SG_WIDTH // SG_GROUP_DIM, SG_GROUP_DIM).sum(-1).T[None]

        d_o, dz, gs["hg_norm"][l] = mm_fused(
            t + "proj_a_dx", m, [(dpa, 0)], [(W["w_a"], 0)], "nt", D,
            e_ins=[(sv["o_f"], D, 0), (sv["o_b"], D, 0), (sv["z"], D, ZG)], consts=[hg_g], epilogue=_hg_post_bwd,
            outs=[(D, BF16)], alias_outs=[(dz, D, ZG)], accs=[(1, D)], tm=256)
        if l == 0:
            part = {n: (g if not isinstance(g, list) else jnp.concatenate(
                [jnp.zeros((1,) + g[1].shape[1:], F32) if gl is None else gl for gl in g], axis=0))
                for n, g in gs.items()}
            plan.early_small(_pack([part[n].reshape(S[n].shape) for n in SMALL]))
        (dq_f, dv_f, dq_b, dv_b, dzf_f, dzf_b, dlb_f[l], dlb_b[l]), extra = hgrn_bwd(
            t + "hgrn", sv["z"], d_o, sv["s_f"], sv["s_b"], sv["b_f"], sv["b_b"], lbf, lbb,
            exch=plan.exch(t + "hgrn"))
        plan.done(t + "hgrn", extra)

        def combine(dqf, dqb, dvf, dvb, dff_, dfb_, zq):
            dq = dqf.astype(F32) + dqb.astype(F32)
            dv = dvf.astype(F32) + dvb.astype(F32)
            return (jnp.concatenate([(dq * _silu_grad(zq)).astype(BF16), dff_, dfb_, dv.astype(BF16)], axis=1),)
        (dz,) = rowwise(t + "hgrn_combine", combine, m,
                        ins=[(dq_f, D, 0), (dq_b, D, 0), (dv_f, D, 0), (dv_b, D, 0), (dzf_f, D, 0), (dzf_b, D, 0),
                             (sv["z"], D, ZQ)], alias_outs=[(dz, 4 * D, 0)], tm=128)
        wgrad("w_in", t + "w_in", sv["h"], dz)

        def pre_bwd(dhv, d1, xv, g):
            xh, r = _rms(xv)
            return d1 + _rms_bwd(dhv, xh, r, g), _colsum(dhv * xh)
        ex = plan.exch(t + "in_proj_dx")
        res = mm_fused(t + "in_proj_dx", m, [(dz, 0)], [(W["w_in"], 0)], "nt", N_IN,
                       e_ins=[(dx1, D, 0), (sv["x"], D, 0)], consts=[g_pre], epilogue=pre_bwd, outs=[(D, F32)],
                       accs=[(1, D)], tm=1024, exch=ex)
        if ex is not None:
            res, extra = res
            plan.done(t + "in_proj_dx", extra)
        dx, gs["norm_mix_pre"][l] = res
        saved[l] = None
        if l == DEPTH - 1:
            none = jnp.zeros((1, D), F32)
            gs["lb_gamma_fwd"], gs["lb_gamma_bwd"] = lower_bounds_bwd(
                "lower_bounds_bwd", S["lb_gamma_fwd"], S["lb_gamma_bwd"], jnp.concatenate([none, dlb_f[l]], axis=0),
                jnp.concatenate([none, dlb_b[l]], axis=0))

    small ={n: (g if not isinstance(g, list) else jnp.concatenate(g, axis=0)).reshape(S[n].shape)
             for n, g in gs.items()}
    return loss_cols, dx, small


def cast_pad(name, w, rows_p, cols_p):
    _, r, c = w.shape

    def body(w_ref, o_ref):
        if (rows_p, cols_p) != (r, c):
            o_ref[...] = jnp.zeros(o_ref.shape, BF16)
        o_ref[0:r, 0:c] = w_ref[...].astype(BF16)

    return pl.pallas_call(
        body, name=name, grid=(DEPTH,), in_specs=[pl.BlockSpec((None, r, c), lambda l: (l, 0, 0))],
        out_specs=pl.BlockSpec((None, rows_p, cols_p), lambda l: (l, 0, 0)),
        out_shape=jax.ShapeDtypeStruct((DEPTH, rows_p, cols_p), BF16), compiler_params=_params(("parallel",)),
    )(w)


def _shard_shape(n, shape):
    axis, size, _, _ = LAYOUT[n]
    _, r, c = shape
    return (size, c) if axis == 0 else (r, size)


class DistPlan:
    def __init__(self, shards):
        self.shards = shards
        self.W = [dict() for _ in range(DEPTH)]
        self.grads = [dict() for _ in range(DEPTH)]
        self.slots = [dict() for _ in range(DEPTH)]
        rest = [n for n in BIG if n != "w_in"]
        ffn = ["w_gate", "w_up", "w_down"]
        self.schedule = {
            "l0_in_proj": ("gather", [(0, n) for n in rest]),
            "l0_hgrn_fwd": ("gather", [(1, n) for n in BIG if n not in ffn]),
            "l0_ffn_in": ("gather", [(1, n) for n in ffn]),
            "l1_bwd_hgrn": ("scatter", [(1, n) for n in rest]),
            "l1_bwd_in_proj_dx": ("scatter", [(1, "w_in")]),
            "l0_bwd_hgrn": ("scatter", [(0, n) for n in rest]),
            "l0_bwd_in_proj_dx": ("scatter", [(0, "w_in")]),
        }
        self.pending = {}
        self.small_part = self.small_slots = None
        axis, size, dst, _ = LAYOUT["w_in"]
        self.W[0][dst] = gather_two_level("gather_l0_w_in", shards["w_in"], 0, axis, size, GATHERED[dst])

    def _gather(self, host, parts):
        srcs, dsts, items, keys = [], [], [], []
        for layer, n in parts:
            axis, size, dst, base = LAYOUT[n]
            if (layer, dst) not in keys:
                keys.append((layer, dst))
                dsts.append((GATHERED[dst], BF16))
            srcs.append(self.shards[n])
            items.append(("gather", len(srcs) - 1, keys.index((layer, dst)), axis, size, base, layer))
        self.pending[host] = ("gather", keys)
        return Exchange(srcs, dsts, items)

    def _scatter(self, host, parts):
        srcs, dsts, items = [], [], []
        for layer, n in parts:
            axis, size, _, _ = LAYOUT[n]
            srcs.append(self.grads[layer][n])
            dsts.append(((NDEV,) + _shard_shape(n, self.shards[n].shape), BF16))
            items.append(("scatter", len(srcs) - 1, len(dsts) - 1, axis, size, 0, None))
        keys = list(parts)
        if host == "l0_bwd_hgrn" and self.small_part is not None:
            srcs.append(self.small_part)
            dsts.append(((NDEV,) + self.small_part.shape, F32))
            items.append(("copies", len(srcs) - 1, len(dsts) - 1, 0, 0, 0, None))
            keys.append(("small", None))
        self.pending[host] = ("scatter", keys)
        return Exchange(srcs, dsts, items)

    def early_small(self, packed):
        self.small_part = packed

    def exch(self, host):
        if host not in self.schedule:
            return None
        kind, parts = self.schedule[host]
        return self._gather(host, parts) if kind == "gather" else self._scatter(host, parts)

    def done(self, host, outs):
        if host not in self.pending:
            return
        kind, keys = self.pending.pop(host)
        for (layer, n), arr in zip(keys, outs):
            if layer == "small":
                self.small_slots = arr
            else:
                (self.W if kind == "gather" else self.slots)[layer][n] = arr


def adam(name, w, m_, v_, tr, g=None, slots=None):
    L, r, c = w.shape
    assert r % tr == 0
    nt = r // tr
    n_s = 0 if slots is None else L

    def body(*refs):
        s_refs = refs[:n_s]
        g_ref = refs[n_s] if g is not None else None
        w_ref, m_ref, v_ref, g_out, d_out, m_out, v_out = refs[n_s + (g is not None):]

        def update(gv):
            if g_ref is not None:
                gv = gv + g_ref[...] if gv is not None else g_ref[...]
            m2 = B1 * m_ref[...] + (1.0 - B1) * gv
            v2 = B2 * v_ref[...] + (1.0 - B2) * (gv * gv)
            m_hat = m2 / (1.0 - B1 ** STEP)
            v_hat = v2 / (1.0 - B2 ** STEP)
            g_out[...] = gv
            d_out[...] = -LR * (m_hat / (jnp.sqrt(v_hat) + AEPS) + WD * w_ref[...])
            m_out[...] = m2
            v_out[...] = v2

        if slots is None:
            update(None)
            return
        for layer, s_ref in enumerate(s_refs):
            @pl.when(pl.program_id(0) == layer)
            def _():
                gv = s_ref[0][:, :c].astype(F32)
                for j in range(1, NDEV):
                    gv = gv + s_ref[j][:, :c].astype(F32)
                update(gv)

    spec = pl.BlockSpec((None, tr, c), lambda l, i: (l, i, 0))
    arrs, specs = [], []
    if slots is not None:
        assert len(slots) == L and L <= 2
        arrs = list(slots)
        cp = slots[0].shape[2]
        specs = [pl.BlockSpec((NDEV, tr, cp), lambda l, i: (0, i * (1 - l) + (nt - 1) * l, 0)),
                 pl.BlockSpec((NDEV, tr, cp), lambda l, i: (0, i * l, 0))][:L]
    if g is not None:
        arrs.append(g)
        specs.append(spec)
    shp = jax.ShapeDtypeStruct(w.shape, F32)
    return pl.pallas_call(
        body, name=name, grid=(L, nt), in_specs=specs + [spec, spec, spec], out_specs=[spec] * 4,
        out_shape=[shp] * 4, compiler_params=_params(("arbitrary", "arbitrary")),
    )(*arrs, w, m_, v_)


def _pack(arrs):
    parts = []
    for a in arrs:
        a2 = a.reshape(-1, D)
        parts.append(jnp.pad(a2, ((0, -a2.shape[0] % 8), (0, 0))))
    return jnp.concatenate(parts, axis=0)


def _unpack(buf, shapes):
    out, off = [], 0
    for s in shapes:
        rows = 1
        for d_ in s:
            rows *= d_
        rows //= D
        out.append(buf[off:off + rows].reshape(s))
        off += rows + (-rows % 8)
    return out


def kernel(x, p, norm_mix_pre, w_in, lb_gamma_fwd, lb_gamma_bwd, hg_norm, sg_w, sg_b, sg_ln_g, sg_ln_b, w_a, w_b, w_out, norm_mix_post, norm_ffn_pre, w_gate, w_up, w_down, norm_ffn_post, w_ple, w_ple_gate, loss_target, m_norm_mix_pre, m_w_in, m_lb_gamma_fwd, m_lb_gamma_bwd, m_hg_norm, m_sg_w, m_sg_b, m_sg_ln_g, m_sg_ln_b, m_w_a, m_w_b, m_w_out, m_norm_mix_post, m_norm_ffn_pre, m_w_gate, m_w_up, m_w_down, m_norm_ffn_post, m_w_ple, m_w_ple_gate, v_norm_mix_pre, v_w_in, v_lb_gamma_fwd, v_lb_gamma_bwd, v_hg_norm, v_sg_w, v_sg_b, v_sg_ln_g, v_sg_ln_b, v_w_a, v_w_b, v_w_out, v_norm_mix_post, v_norm_ffn_pre, v_w_gate, v_w_up, v_w_down, v_norm_ffn_post, v_w_ple, v_w_ple_gate):
    a = dict(zip(INPUTS, (x, p, norm_mix_pre, w_in, lb_gamma_fwd, lb_gamma_bwd, hg_norm, sg_w, sg_b, sg_ln_g, sg_ln_b, w_a, w_b, w_out, norm_mix_post, norm_ffn_pre, w_gate, w_up, w_down, norm_ffn_post, w_ple, w_ple_gate, loss_target, m_norm_mix_pre, m_w_in, m_lb_gamma_fwd, m_lb_gamma_bwd, m_hg_norm, m_sg_w, m_sg_b, m_sg_ln_g, m_sg_ln_b, m_w_a, m_w_b, m_w_out, m_norm_mix_post, m_norm_ffn_pre, m_w_gate, m_w_up, m_w_down, m_norm_ffn_post, m_w_ple, m_w_ple_gate, v_norm_mix_pre, v_w_in, v_lb_gamma_fwd, v_lb_gamma_bwd, v_hg_norm, v_sg_w, v_sg_b, v_sg_ln_g, v_sg_ln_b, v_w_a, v_w_b, v_w_out, v_norm_mix_post, v_norm_ffn_pre, v_w_gate, v_w_up, v_w_down, v_norm_ffn_post, v_w_ple, v_w_ple_gate)))
    m = x.shape[1]

    shards = {n: cast_pad("cast_" + n, a[n], *_shard_shape(n, a[n].shape)) for n in BIG}
    plan = DistPlan(shards)
    loss_cols, dx, gs = local_step(x[0], p[:, 0], loss_target[0], {n: a[n] for n in SMALL}, plan)
    loss = lax.psum(jnp.sum(loss_cols) * (0.5 / D), ("x", "y", "c"))

    small_shapes = [a[n].shape for n in SMALL]
    rows = plan.small_slots.shape[1]
    late = allreduce_small("allreduce_small", jnp.pad(gs["norm_mix_pre"][0:1], ((0, 7), (0, 0))))
    g_late = jnp.pad(late, ((0, rows - 8), (0, 0)))[None]

    res = {}
    row_tiles = {"w_in": 128, "w_a": 128, "w_b": 512, "w_out": 128, "w_gate": 128, "w_up": 128, "w_down": 88,
                 "w_ple": 256, "w_ple_gate": 128}
    for n in BIG:
        res[n] = adam("adam_" + n, a[n], a["m_" + n], a["v_" + n], row_tiles[n],
                      slots=[plan.slots[l][n] for l in range(DEPTH)])
    packed = [_pack([a[pre + n] for n in SMALL])[None] for pre in ("", "m_", "v_")]
    small_res = adam("adam_small", packed[0], packed[1], packed[2], rows // 2, g=g_late, slots=[plan.small_slots])
    small_res = [_unpack(r_[0], small_shapes) for r_ in small_res]
    for i, n in enumerate(SMALL):
        res[n] = tuple(small_res[k][i] for k in range(4))

    outs = [loss, dx.reshape(1, m, D)]
    for k in range(4):
        outs += [res[n][k] for n in WEIGHTS]
    return tuple(outs)
```

```python
import jax
import jax.numpy as jnp
from jax import lax
from jax.experimental import pallas as pl
from jax.experimental.pallas import tpu as pltpu

F32 = jnp.float32
BF16 = jnp.bfloat16

D = 1024
N_IN = 8192
HEADS = 8
HEAD_DIM = 128
SG_CHUNK = 128
SG_WIDTH = 512
SG_GROUP_DIM = 64
FFN = 2816
PLE_DIM = 256
EPS = 1e-6
DEPTH = 2
ZQ, ZFF, ZFB, ZI, ZG, GA, GB = 0, 1, 2, 3, 4, 6, 7
ZU, ZV = 10, 11

NDEV = 8
FFN_SHARD = FFN // NDEV
FFN_SHARD_PAD = 384
FFN_PAD = NDEV * FFN_SHARD_PAD

LR, B1, B2, AEPS, WD, STEP = 0.001, 0.9, 0.999, 1e-08, 0.01, 10

ROW_TILE = 256
HG_CHUNK = 64
HG_BLOCK_FWD = 256
HG_BLOCK_BWD = 256
EXP_CLAMP = 80.0
PROLOGUE_CHUNK = 256
TINY = float(jnp.finfo(jnp.float32).tiny)
VMEM_LIMIT = 56 * 1024 * 1024

BIG = ["w_in", "w_a", "w_b", "w_out", "w_gate", "w_up", "w_down", "w_ple", "w_ple_gate"]
SMALL = ["norm_mix_pre", "lb_gamma_fwd", "lb_gamma_bwd", "hg_norm", "sg_w", "sg_b", "sg_ln_g", "sg_ln_b",
         "norm_mix_post", "norm_ffn_pre", "norm_ffn_post"]
WEIGHTS = ["norm_mix_pre", "w_in", "lb_gamma_fwd", "lb_gamma_bwd", "hg_norm", "sg_w", "sg_b", "sg_ln_g", "sg_ln_b",
           "w_a", "w_b", "w_out", "norm_mix_post", "norm_ffn_pre", "w_gate", "w_up", "w_down", "norm_ffn_post",
           "w_ple", "w_ple_gate"]
INPUTS = (["x", "p"] + WEIGHTS + ["loss_target"] + ["m_" + n for n in WEIGHTS] + ["v_" + n for n in WEIGHTS])
LAYOUT = {
    "w_in": (1, 1024, "w_in", 0), "w_a": (0, 128, "w_a", 0), "w_b": (1, 128, "w_b", 0),
    "w_out": (0, 128, "w_out", 0), "w_gate": (1, FFN_SHARD_PAD, "w_gu", 0),
    "w_up": (1, FFN_SHARD_PAD, "w_gu", FFN_PAD), "w_down": (0, FFN_SHARD_PAD, "w_down", 0),
    "w_ple": (1, 128, "w_ple", 0), "w_ple_gate": (0, 128, "w_ple_gate", 0),
}
GATHERED = {"w_in": (D, N_IN), "w_a": (D, D), "w_b": (SG_WIDTH, D), "w_out": (D, D), "w_gu": (D, 2 * FFN_PAD),
            "w_down": (FFN_PAD, D), "w_ple": (PLE_DIM, D), "w_ple_gate": (D, D)}


def _params(sem):
    return pltpu.CompilerParams(dimension_semantics=sem, vmem_limit_bytes=VMEM_LIMIT)


def _dot(a, b):
    return lax.dot_general(a, b, (((1,), (0,)), ((), ())), preferred_element_type=F32)


def _dot_nt(a, b):
    return lax.dot_general(a, b, (((1,), (1,)), ((), ())), preferred_element_type=F32)


def _dot_tn(a, b):
    return lax.dot_general(a, b, (((0,), (0,)), ((), ())), preferred_element_type=F32)


def _sigmoid(x):
    return jax.nn.sigmoid(x)


def _silu(x):
    return x * _sigmoid(x)


def _silu_grad(x):
    s = _sigmoid(x)
    return s * (1.0 + x * (1.0 - s))


def _gelu(x):
    return 0.5 * x * (1.0 + lax.erf(x * 0.7071067811865476))


def _gelu_grad(x):
    return 0.5 * (1.0 + lax.erf(x * 0.7071067811865476)) + x * jnp.exp(-0.5 * x * x) * 0.3989422804014327


def _mean(x):
    return jnp.mean(x, axis=-1, keepdims=True)


def _colsum(x):
    return jnp.sum(x, axis=0, keepdims=True)


def _rms(x):
    r = lax.rsqrt(_mean(x * x) + EPS)
    return x * r, r


def _rms_bwd(dy, xh, r, g):
    dyg = dy * g
    return r * (dyg - xh * _mean(dyg * xh))


MESH = pl.DeviceIdType.MESH
ANY = pl.BlockSpec(memory_space=pl.ANY)


def _slab(ref, axis, start, size):
    idx = [slice(None)] * 2
    idx[axis] = pl.ds(start, size)
    return ref.at[tuple(idx)]


class Exchange:
    def __init__(self, srcs, dsts, items):
        self.srcs, self.dsts, self.items = list(srcs), list(dsts), list(items)

    def specs(self):
        n = len(self.items)
        sems = [pltpu.SemaphoreType.DMA((n * (NDEV - 1),)), pltpu.SemaphoreType.DMA((n * (NDEV - 1),)),
                pltpu.SemaphoreType.DMA((n,))]
        return ([ANY] * len(self.srcs), [ANY] * len(self.dsts),
                [jax.ShapeDtypeStruct(s, dt) for (s, dt) in self.dsts], sems)

    def copies(self, src, dst, send_sem, recv_sem, loc_sem):
        x, y, c = lax.axis_index("x"), lax.axis_index("y"), lax.axis_index("c")
        me = 4 * x + 2 * y + c
        starts, waits = [], []
        for n, (kind, si, di, axis, size, base, layer) in enumerate(self.items):
            def views(to_dev, from_dev):
                if kind == "gather":
                    return (src[si].at[layer],
                            _slab(dst[di], axis, base + pl.multiple_of(from_dev * size, 128), size))
                if kind == "copies":
                    return src[si], dst[di].at[from_dev]
                return _slab(src[si], axis, base + pl.multiple_of(to_dev * size, 128), size), dst[di].at[from_dev]

            s_own, d_own = views(me, me)
            own = pltpu.make_async_copy(s_own, d_own, loc_sem.at[n])
            starts.append(own)
            waits.append(own)
            for k in range(1, NDEV):
                px = 1 - x if k & 4 else x
                py = 1 - y if k & 2 else y
                pc = 1 - c if k & 1 else c
                peer = 4 * px + 2 * py + pc
                s_out, _ = views(peer, me)
                _, d_in = views(me, peer)
                sem = n * (NDEV - 1) + k - 1
                starts.append(pltpu.make_async_remote_copy(s_out, d_own, send_sem.at[sem], recv_sem.at[sem],
                                                           device_id=(px, py, pc), device_id_type=MESH))
                waits.append(pltpu.make_async_remote_copy(s_out, d_in, send_sem.at[sem], recv_sem.at[sem],
                                                          device_id=(px, py, pc), device_id_type=MESH))
        return starts, waits


def exchange(name, exch):
    e_in, e_out, e_shape, e_scr = exch.specs()
    ns, nd = len(e_in), len(e_out)

    def body(*refs):
        starts, waits = exch.copies(refs[:ns], refs[ns:ns + nd], *refs[ns + nd:])
        for cp in starts:
            cp.start()
        for cp in waits:
            cp.wait()

    return pl.pallas_call(body, name=name, in_specs=e_in, out_specs=e_out, out_shape=e_shape, scratch_shapes=e_scr,
                          compiler_params=pltpu.CompilerParams(has_side_effects=True))(*exch.srcs)


def gather_two_level(name, shards, layer, axis, size, full_shape):
    def body(src, dst, send_sem, recv_sem, loc_sem):
        x, y, c = lax.axis_index("x"), lax.axis_index("y"), lax.axis_index("c")
        mine = src.at[layer]
        chips = [(1 - x, y), (x, 1 - y), (1 - x, 1 - y)]

        def slab(px, py, pc):
            return _slab(dst, axis, pl.multiple_of((4 * px + 2 * py + pc) * size, 128), size)

        def copy(k, from_ref, block, to):
            return pltpu.make_async_remote_copy(from_ref, slab(*block), send_sem.at[k], recv_sem.at[k], device_id=to,
                                                device_id_type=MESH)

        own = pltpu.make_async_copy(mine, slab(x, y, c), loc_sem)
        own.start()
        first = [copy(0, mine, (x, y, c), (x, y, 1 - c))]
        first += [copy(1 + j, mine, (x, y, c), (*chip, c)) for j, chip in enumerate(chips)]
        for cp in first:
            cp.start()
        passed = []
        for j, chip in enumerate(chips):
            copy(1 + j, mine, (*chip, c), (x, y, c)).wait_recv()
            fwd = copy(4 + j, slab(*chip, c), (*chip, c), (x, y, 1 - c))
            fwd.start()
            passed.append(fwd)
        copy(0, mine, (x, y, 1 - c), (x, y, c)).wait_recv()
        for j, chip in enumerate(chips):
            copy(4 + j, mine, (*chip, 1 - c), (x, y, c)).wait_recv()
        for cp in first + passed:
            cp.wait_send()
        own.wait()

    return pl.pallas_call(
        body, name=name, in_specs=[ANY], out_specs=ANY, out_shape=jax.ShapeDtypeStruct(full_shape, shards.dtype),
        scratch_shapes=[pltpu.SemaphoreType.DMA((NDEV - 1,)), pltpu.SemaphoreType.DMA((NDEV - 1,)),
                        pltpu.SemaphoreType.DMA(())],
        compiler_params=pltpu.CompilerParams(has_side_effects=True))(shards)


def hosted_call(body, exch, name, grid, in_specs, out_specs, out_shape, scratch_shapes, operands, semantics,
                aliases=None):
    aliases = aliases or {}
    if exch is None:
        res = pl.pallas_call(body, name=name, grid=grid, in_specs=in_specs, out_specs=out_specs, out_shape=out_shape,
                             scratch_shapes=scratch_shapes, input_output_aliases=aliases,
                             compiler_params=_params(semantics))(*operands)
        return list(res), []
    n_in, n_out, n_scr = len(in_specs), len(out_specs), len(scratch_shapes)
    e_in, e_out, e_shape, e_scr = exch.specs()
    ns, nd = len(e_in), len(e_out)

    def at_step(last):
        cond = None
        for ax, n in enumerate(grid):
            c = pl.program_id(ax) == (n - 1 if last else 0)
            cond = c if cond is None else jnp.logical_and(cond, c)
        return cond

    def wrapped(*refs):
        ins, src = refs[:n_in], refs[n_in:n_in + ns]
        o0 = n_in + ns
        outs, dst = refs[o0:o0 + n_out], refs[o0 + n_out:o0 + n_out + nd]
        s0 = o0 + n_out + nd
        scr, sems = refs[s0:s0 + n_scr], refs[s0 + n_scr:]

        @pl.when(at_step(False))
        def _():
            for cp in exch.copies(src, dst, *sems)[0]:
                cp.start()

        body(*ins, *outs, *scr)

        @pl.when(at_step(True))
        def _():
            for cp in exch.copies(src, dst, *sems)[1]:
                cp.wait()

    res = pl.pallas_call(
        wrapped, name=name, grid=grid, in_specs=list(in_specs) + e_in, out_specs=list(out_specs) + e_out,
        out_shape=list(out_shape) + e_shape, scratch_shapes=list(scratch_shapes) + e_scr,
        input_output_aliases=aliases,
        compiler_params=pltpu.CompilerParams(dimension_semantics=("arbitrary",) * len(grid),
                                             vmem_limit_bytes=VMEM_LIMIT, has_side_effects=True),
    )(*operands, *exch.srcs)
    return list(res[:n_out]), list(res[n_out:])


def allreduce_small(name, part):
    rows, width = part.shape

    def body(p_ref, o_ref, buf, send_sem, recv_sem):
        x, y, c = lax.axis_index("x"), lax.axis_index("y"), lax.axis_index("c")
        me = 4 * x + 2 * y + c
        buf[me] = p_ref[...]
        waits = []
        for k in range(1, NDEV):
            px = 1 - x if k & 4 else x
            py = 1 - y if k & 2 else y
            pc = 1 - c if k & 1 else c
            peer = 4 * px + 2 * py + pc
            pltpu.make_async_remote_copy(p_ref, buf.at[me], send_sem.at[k - 1], recv_sem.at[k - 1],
                                         device_id=(px, py, pc), device_id_type=MESH).start()
            waits.append(pltpu.make_async_remote_copy(p_ref, buf.at[peer], send_sem.at[k - 1], recv_sem.at[k - 1],
                                                      device_id=(px, py, pc), device_id_type=MESH))
        for w in waits:
            w.wait()
        acc = buf[0]
        for j in range(1, NDEV):
            acc = acc + buf[j]
        o_ref[...] = acc

    vmem = pl.BlockSpec(memory_space=pltpu.VMEM)
    return pl.pallas_call(
        body, name=name, in_specs=[vmem], out_specs=vmem, out_shape=jax.ShapeDtypeStruct((rows, width), F32),
        scratch_shapes=[pltpu.VMEM((NDEV, rows, width), F32), pltpu.SemaphoreType.DMA((NDEV - 1,)),
                        pltpu.SemaphoreType.DMA((NDEV - 1,))],
        compiler_params=pltpu.CompilerParams(vmem_limit_bytes=VMEM_LIMIT, has_side_effects=True),
    )(part)


def rowwise(name, fn, m, ins=(), consts=(), outs=(), alias_outs=(), accs=(), tm=ROW_TILE):
    tm = min(tm, m)
    n_in, n_c, n_o, n_al, n_ac = len(ins), len(consts), len(outs), len(alias_outs), len(accs)
    held = [a for (a, _, _) in alias_outs if not isinstance(a, jax.ShapeDtypeStruct)]
    n_held = len(held)

    def body(*refs):
        in_refs = refs[:n_in + n_c]
        out_refs = refs[n_in + n_c + n_held:]
        vals = fn(*[r[...] for r in in_refs])
        if not isinstance(vals, (tuple, list)):
            vals = (vals,)
        for r, v in zip(out_refs[:n_o + n_al], vals[:n_o + n_al]):
            r[...] = v.astype(r.dtype)
        if n_ac:
            acc_refs = out_refs[n_o + n_al:]

            @pl.when(pl.program_id(0) == 0)
            def _():
                for r in acc_refs:
                    r[...] = jnp.zeros(r.shape, F32)

            for r, v in zip(acc_refs, vals[n_o + n_al:]):
                r[...] += v

    def col(cb):
        return lambda i: (i, cb)

    in_specs = [pl.BlockSpec((tm, w), col(cb)) for (_, w, cb) in ins]
    in_specs += [pl.BlockSpec(c.shape, lambda i, nd=c.ndim: (0,) * nd) for c in consts]
    in_specs += [ANY for _ in held]
    out_shape = [jax.ShapeDtypeStruct((m, w), dt) for (w, dt) in outs]
    out_specs = [pl.BlockSpec((tm, w), col(0)) for (w, _) in outs]
    out_shape += [jax.ShapeDtypeStruct(a.shape, a.dtype) for (a, _, _) in alias_outs]
    out_specs += [pl.BlockSpec((tm, w), col(cb)) for (_, w, cb) in alias_outs]
    out_shape += [jax.ShapeDtypeStruct(s, F32) for s in accs]
    out_specs += [pl.BlockSpec(s, lambda i: (0, 0)) for s in accs]
    aliases, k_in = {}, n_in + n_c
    for k, (a, _, _) in enumerate(alias_outs):
        if not isinstance(a, jax.ShapeDtypeStruct):
            aliases[k_in] = n_o + k
            k_in += 1
    return pl.pallas_call(
        body, name=name, grid=(m // tm,), in_specs=in_specs, out_specs=out_specs, out_shape=out_shape,
        input_output_aliases=aliases,
        compiler_params=_params(("arbitrary",) if n_ac else ("parallel",)),
    )(*[a for (a, _, _) in ins], *consts, *held)


def _operand(arr, bshape, imap):
    if isinstance(arr, tuple):
        arr, lead = arr
        return arr, pl.BlockSpec((None,) + bshape, lambda *g: (lead,) + imap(*g))
    return arr, pl.BlockSpec(bshape, imap)


def _shape2(arr):
    return arr[0].shape[1:] if isinstance(arr, tuple) else arr.shape


def mm(name, a, b, mode, out_dtype=F32, tm=1024, tn=1024, tk=1024, exch=None):
    sa, sb = _shape2(a), _shape2(b)
    if mode == "nn":
        (M, K), N = sa, sb[1]
    elif mode == "nt":
        (M, K), N = sa, sb[0]
    else:
        (K, M), N = sa, sb[1]
    tm, tn, tk = min(tm, M), min(tn, N), min(tk, K)
    assert M % tm == 0 and N % tn == 0 and K % tk == 0, (name, M, N, K)
    nk = K // tk
    if mode == "nn":
        a_arr, a_spec = _operand(a, (tm, tk), lambda i, j, k: (i, k))
        b_arr, b_spec = _operand(b, (tk, tn), lambda i, j, k: (k, j))
        dot = _dot
    elif mode == "nt":
        a_arr, a_spec = _operand(a, (tm, tk), lambda i, j, k: (i, k))
        b_arr, b_spec = _operand(b, (tn, tk), lambda i, j, k: (j, k))
        dot = _dot_nt
    else:
        a_arr, a_spec = _operand(a, (tk, tm), lambda i, j, k: (k, i))
        b_arr, b_spec = _operand(b, (tk, tn), lambda i, j, k: (k, j))
        dot = _dot_tn

    def body(a_ref, b_ref, o_ref, *acc):
        part = dot(a_ref[...].astype(BF16), b_ref[...].astype(BF16))
        if nk == 1:
            o_ref[...] = part.astype(o_ref.dtype)
            return
        acc_ref, k = acc[0], pl.program_id(2)

        @pl.when(k == 0)
        def _():
            acc_ref[...] = part

        @pl.when(k > 0)
        def _():
            acc_ref[...] += part

        @pl.when(k == nk - 1)
        def _():
            o_ref[...] = acc_ref[...].astype(o_ref.dtype)

    outs, extra = hosted_call(
        body, exch, name, (M // tm, N // tn, nk), [a_spec, b_spec], [pl.BlockSpec((tm, tn), lambda i, j, k: (i, j))],
        [jax.ShapeDtypeStruct((M, N), out_dtype)], [pltpu.VMEM((tm, tn), F32)] if nk > 1 else [], [a_arr, b_arr],
        ("parallel", "parallel", "arbitrary"))
    return outs[0] if exch is None else (outs[0], extra)


def mm_fused(name, m, a_ins, bs, mode, kdim, prologue=None, a_outs=(), e_ins=(), consts=(), epilogue=None, outs=(),
             alias_outs=(), accs=(), a_to_epilogue=(), a_consts=(), tm=512, tk=1024, resident=False, e_single=False,
             exch=None):
    tm = min(tm, m)
    nk = kdim // tk
    assert nk == 1 or not a_to_epilogue
    n = bs[0][0].shape[1 if mode == "nn" else 0]
    b_arrays = []
    for b_, _ in bs:
        if not (resident and any(b_ is u for u in b_arrays)):
            b_arrays.append(b_)
    b_of_pair = [next(j for j, u in enumerate(b_arrays) if u is b_) if resident else j for j, (b_, _) in enumerate(bs)]
    n_a, n_b, n_e, n_c = len(a_ins) + len(a_consts), len(b_arrays), len(e_ins), len(consts)
    n_ao, n_o, n_al, n_ac = len(a_outs), len(outs), len(alias_outs), len(accs)
    held = [a for (a, _, _) in alias_outs if not isinstance(a, jax.ShapeDtypeStruct)]
    dot = _dot if mode == "nn" else _dot_nt

    def body(*refs):
        a_refs, b_refs = refs[:n_a], refs[n_a:n_a + n_b]
        e_refs = refs[n_a + n_b:n_a + n_b + n_e + n_c]
        o0 = n_a + n_b + n_e + n_c + len(held)
        ao_refs = refs[o0:o0 + n_ao]
        out_refs = refs[o0 + n_ao:o0 + n_ao + n_o + n_al]
        acc_refs = refs[o0 + n_ao + n_o + n_al:o0 + n_ao + n_o + n_al + n_ac]
        scr = refs[o0 + n_ao + n_o + n_al + n_ac:]
        i, k = pl.program_id(0), pl.program_id(1)
        ck = tk if prologue is None else min(tk, PROLOGUE_CHUNK)
        part = None
        for c0 in range(0, tk, ck):
            cols = slice(c0, c0 + ck)
            tiles = [r[:, cols] for r in a_refs]
            a_list, extra = (tiles, []) if prologue is None else prologue(*tiles)
            for r, v in zip(ao_refs, extra):
                r[:, cols] = v.astype(r.dtype)
            for a, j_b, (_, off) in zip(a_list, b_of_pair, bs):
                b_ref = b_refs[j_b]
                if resident:
                    b = b_ref[pl.ds(pl.multiple_of((k + off) * tk + c0, ck), ck), :]
                else:
                    b = b_ref[cols, :] if mode == "nn" else b_ref[:, cols]
                prod = dot(a.astype(BF16), b.astype(BF16))
                part = prod if part is None else part + prod

        def finish(total):
            vals = epilogue(total, *[a_refs[j][...] for j in a_to_epilogue], *[r[...] for r in e_refs])
            if not isinstance(vals, (tuple, list)):
                vals = (vals,)
            for r, v in zip(out_refs, vals[:n_o + n_al]):
                r[...] = v.astype(r.dtype)
            for r, v in zip(acc_refs, vals[n_o + n_al:]):
                @pl.when(i == 0)
                def _():
                    r[...] = v

                @pl.when(i > 0)
                def _():
                    r[...] += v

        if nk == 1:
            finish(part)
            return
        acc_ref = scr[0]

        @pl.when(k == 0)
        def _():
            acc_ref[...] = part

        @pl.when(k > 0)
        def _():
            acc_ref[...] += part

        @pl.when(k == nk - 1)
        def _():
            finish(acc_ref[...])

    in_specs = [pl.BlockSpec((tm, tk), lambda i, k, off=off: (i, k + off)) for (_, off) in a_ins]
    in_specs += [pl.BlockSpec((1, tk), lambda i, k: (0, k)) for _ in a_consts]
    if resident:
        assert mode == "nn"
        in_specs += [pl.BlockSpec(b.shape, lambda i, k: (0, 0), pipeline_mode=pl.Buffered(1)) for b in b_arrays]
    elif mode == "nn":
        in_specs += [pl.BlockSpec((tk, n), lambda i, k, off=off: (k + off, 0)) for (_, off) in bs]
    else:
        in_specs += [pl.BlockSpec((n, tk), lambda i, k, off=off: (0, k + off)) for (_, off) in bs]
    e_mode = pl.Buffered(1) if (e_single and nk > 1) else None
    in_specs += [pl.BlockSpec((tm, w), lambda i, k, cb=cb: (i, cb), pipeline_mode=e_mode) for (_, w, cb) in e_ins]
    in_specs += [pl.BlockSpec(c.shape, lambda i, k, nd=c.ndim: (0,) * nd) for c in consts]
    in_specs += [ANY for _ in held]
    out_shape = [jax.ShapeDtypeStruct((m, kdim), dt) for dt in a_outs]
    out_specs = [pl.BlockSpec((tm, tk), lambda i, k: (i, k)) for _ in a_outs]
    out_shape += [jax.ShapeDtypeStruct((m, w), dt) for (w, dt) in outs]
    out_specs += [pl.BlockSpec((tm, w), lambda i, k: (i, 0)) for (w, _) in outs]
    out_shape += [jax.ShapeDtypeStruct(a.shape, a.dtype) for (a, _, _) in alias_outs]
    out_specs += [pl.BlockSpec((tm, w), lambda i, k, cb=cb: (i, cb)) for (_, w, cb) in alias_outs]
    out_shape += [jax.ShapeDtypeStruct(s_, F32) for s_ in accs]
    out_specs += [pl.BlockSpec(s_, lambda i, k: (0, 0)) for s_ in accs]
    aliases, k_in = {}, n_a + n_b + n_e + n_c
    for j, (a, _, _) in enumerate(alias_outs):
        if not isinstance(a, jax.ShapeDtypeStruct):
            aliases[k_in] = n_ao + n_o + j
            k_in += 1
    operands = [a for (a, _) in a_ins] + list(a_consts) + b_arrays + [a for (a, _, _) in e_ins] + list(consts) + held
    res, extra = hosted_call(
        body, exch, name, (m // tm, nk), in_specs, out_specs, out_shape,
        [pltpu.VMEM((tm, n), F32)] if nk > 1 else [], operands,
        ("arbitrary" if n_ac else "parallel", "arbitrary"), aliases)
    return res if exch is None else (res, extra)


def _cumsum_rows(x):
    n = x.shape[0]
    row = lax.broadcasted_iota(jnp.int32, x.shape, 0)
    s = 1
    while s < n:
        x = x + jnp.where(row >= s, pltpu.roll(x, s, 0), 0.0)
        s *= 2
    return x


def _hg_prep(zq, zf, lb, reverse, b=None):
    n = zq.shape[0]
    q = _silu(zq)
    sig = _sigmoid(zf)
    sn = 1.0 - sig
    f = lb + (1.0 - lb) * sig
    k = (1.0 - lb) * sn
    if b is None:
        g = jnp.log(jnp.maximum(f, TINY))
        b = _cumsum_rows(g)
        if reverse:
            b = b[n - 1:n] - b + g
    b_last = b[0:1] if reverse else b[n - 1:n]
    b_ref = b[n // 2:n // 2 + 1]
    e1 = jnp.exp(b)
    e2 = jnp.exp(jnp.clip(b - b_ref, -EXP_CLAMP, EXP_CLAMP))
    e3 = jnp.exp(jnp.clip(b_ref - b, -EXP_CLAMP, EXP_CLAMP))
    e4 = jnp.exp(b_last - b)
    return dict(q=q, k=k, sig=sig, sn=sn, f=f, b=b, e1=e1, e2=e2, e3=e3, e4=e4, e_last=jnp.exp(b_last),
                qe=(q * e1).astype(BF16), qt=(q * e2).astype(BF16), kt=(k * e3).astype(BF16),
                ks=(k * e4).astype(BF16))


def _hg_mask(n, reverse):
    t = lax.broadcasted_iota(jnp.int32, (n, n), 0)
    s = lax.broadcasted_iota(jnp.int32, (n, n), 1)
    return (s >= t) if reverse else (s <= t)


def hgrn_fwd(name, z, lb_f, lb_b, exch=None, unroll=False):
    m = z.shape[0]
    C, T = HG_CHUNK, min(HG_BLOCK_FWD, m)
    nb, cpb = m // T, T // C

    def body(zq_f, zf_f, zi_f, zq_b, zf_b, zi_b, lbf_ref, lbb_ref, of_ref, ob_ref, sf_ref, sb_ref, bf_ref, bb_ref,
             st_ref):
        @pl.when(pl.program_id(0) == 0)
        def _():
            st_ref[...] = jnp.zeros(st_ref.shape, F32)

        dirs = ((zq_f, zf_f, zi_f, lbf_ref, of_ref, sf_ref), (zq_b, zf_b, zi_b, lbb_ref, ob_ref, sb_ref))
        b_refs = (bf_ref, bb_ref)

        def chunk(ci, carry):
            work = []
            for d, (zq, zf, zi, lb_ref, o_ref, s_ref) in enumerate(dirs):
                cc = ci if d == 0 else cpb - 1 - ci
                rows = pl.ds(pl.multiple_of(cc * C, C), C)
                pre = _hg_prep(zq[rows, :], zf[rows, :], lb_ref[...], d == 1)
                v = zi[rows, :].astype(BF16)
                work.append((cc, rows, pre, v, [st_ref[d, h] for h in range(HEADS)]))
            heads = [(d, h, slice(h * HEAD_DIM, (h + 1) * HEAD_DIM)) for d in range(2) for h in range(HEADS)]
            first = {}
            for d, h, sl in heads:
                _, _, pre, v, sts = work[d]
                first[d, h] = (_dot_nt(pre["qt"][:, sl], pre["kt"][:, sl]),
                               _dot_nt(pre["qe"][:, sl], sts[h].astype(BF16)),
                               _dot_tn(v[:, sl], pre["ks"][:, sl]))
            results = [([], []), ([], [])]
            for d, h, sl in heads:
                _, _, pre, v, sts = work[d]
                scores, o_inter, st_add = first[d, h]
                a = jnp.where(_hg_mask(C, d == 1), scores, 0.0).astype(BF16)
                results[d][0].append(o_inter + _dot(a, v[:, sl]))
                results[d][1].append(sts[h] * pre["e_last"][:, sl] + st_add)
            results = [(jnp.concatenate(o_parts, axis=1), new_sts) for (o_parts, new_sts) in results]
            for d, (zq, zf, zi, lb_ref, o_ref, s_ref) in enumerate(dirs):
                cc, rows, pre, _, sts = work[d]
                o_ref[rows, :] = results[d][0]
                b_refs[d][rows, :] = pre["b"]
                for h in range(HEADS):
                    s_ref[cc, h] = sts[h]
                    st_ref[d, h] = results[d][1][h]
            return carry

        lax.fori_loop(0, cpb, chunk, 0, unroll=unroll)

    def zspec(cb, rev):
        return pl.BlockSpec((T, D), (lambda i: (nb - 1 - i, cb)) if rev else (lambda i: (i, cb)))

    def sspec(rev):
        shape = (cpb, HEADS, HEAD_DIM, HEAD_DIM)
        return pl.BlockSpec(shape, (lambda i: (nb - 1 - i, 0, 0, 0)) if rev else (lambda i: (i, 0, 0, 0)))

    lbspec = pl.BlockSpec((1, D), lambda i: (0, 0))
    states = jax.ShapeDtypeStruct((m // C, HEADS, HEAD_DIM, HEAD_DIM), F32)
    outs, extra = hosted_call(
        body, exch, name, (nb,),
        [zspec(ZQ, False), zspec(ZFF, False), zspec(ZI, False), zspec(ZQ, True), zspec(ZFB, True), zspec(ZI, True),
         lbspec, lbspec],
        [zspec(0, False), zspec(0, True), sspec(False), sspec(True), zspec(0, False), zspec(0, True)],
        [jax.ShapeDtypeStruct((m, D), F32), jax.ShapeDtypeStruct((m, D), F32), states, states,
         jax.ShapeDtypeStruct((m, D), F32), jax.ShapeDtypeStruct((m, D), F32)],
        [pltpu.VMEM((2, HEADS, HEAD_DIM, HEAD_DIM), F32)], [z, z, z, z, z, z, lb_f, lb_b], ("arbitrary",))
    return outs, extra


def hgrn_bwd(name, z, d_o, s_f, s_b, b_f, b_b, lb_f, lb_b, exch=None, unroll=False):
    m = z.shape[0]
    C, T = HG_CHUNK, min(HG_BLOCK_BWD, m)
    nb, cpb = m // T, T // C

    def body(zq_f, zf_f, zi_f, do_f, sf_ref, zq_b, zf_b, zi_b, do_b, sb_ref, lbf_ref, lbb_ref, bf_ref, bb_ref,
             dqf_ref, dvf_ref, dqb_ref, dvb_ref, dzf_f, dzf_b, dlbf_ref, dlbb_ref,
             dst_ref):
        b_refs = (bf_ref, bb_ref)
        @pl.when(pl.program_id(0) == 0)
        def _():
            dst_ref[...] = jnp.zeros(dst_ref.shape, F32)
            dlbf_ref[...] = jnp.zeros(dlbf_ref.shape, F32)
            dlbb_ref[...] = jnp.zeros(dlbb_ref.shape, F32)

        dirs = ((zq_f, zf_f, zi_f, do_f, sf_ref, lbf_ref, dqf_ref, dvf_ref, dzf_f, dlbf_ref),
                (zq_b, zf_b, zi_b, do_b, sb_ref, lbb_ref, dqb_ref, dvb_ref, dzf_b, dlbb_ref))

        def chunk(ci, carry):
            work = []
            for d, (zq, zf, zi, do_ref, s_ref, lb_ref, dq_ref, dv_ref, dzf_ref, dlb_ref) in enumerate(dirs):
                cc = cpb - 1 - ci if d == 0 else ci
                rows = pl.ds(pl.multiple_of(cc * C, C), C)
                lb = lb_ref[...]
                pre = _hg_prep(zq[rows, :], zf[rows, :], lb, d == 1, b=b_refs[d][rows, :])
                work.append((rows, lb, pre, zi[rows, :].astype(BF16), do_ref[rows, :],
                             [s_ref[cc, h] for h in range(HEADS)], [dst_ref[d, h] for h in range(HEADS)],
                             dlb_ref[...]))
            heads = [(d, h, slice(h * HEAD_DIM, (h + 1) * HEAD_DIM)) for d in range(2) for h in range(HEADS)]
            first = {}
            for d, h, sl in heads:
                _, _, pre, v, do, st_prevs, dsts, _ = work[d]
                dst16 = dsts[h].astype(BF16)
                first[d, h] = (_dot_nt(pre["qt"][:, sl], pre["kt"][:, sl]),
                               _dot_nt(do[:, sl], v[:, sl]),
                               _dot(do[:, sl], st_prevs[h].astype(BF16)),
                               _dot(v[:, sl], dst16),
                               _dot_nt(pre["ks"][:, sl], dst16),
                               _dot_tn(do[:, sl], pre["qe"][:, sl]))
            parts = [[[] for _ in range(6)] for _ in range(2)]
            for d, h, sl in heads:
                _, _, pre, v, do, st_prevs, dsts, _ = work[d]
                scores, dscores, dq_inter, dk_state, dv_state, dst_add = first[d, h]
                mask = _hg_mask(C, d == 1)
                a = jnp.where(mask, scores, 0.0).astype(BF16)
                da = jnp.where(mask, dscores, 0.0).astype(BF16)
                dq_p, dki_p, dks_p, dv_p, rr_p, new_dsts = parts[d]
                dq_p.append(_dot(da, pre["kt"][:, sl]) * pre["e2"][:, sl] + dq_inter * pre["e1"][:, sl])
                dki_p.append(_dot_tn(da, pre["qt"][:, sl]) * pre["e3"][:, sl])
                dks_p.append(dk_state * pre["e4"][:, sl])
                dv_p.append(_dot_tn(a, do[:, sl]) + dv_state)
                rr_p.append(pre["e_last"][:, sl] * _colsum(dsts[h] * st_prevs[h]))
                new_dsts.append(dsts[h] * pre["e_last"][:, sl] + dst_add)
            results = []
            for d, (rows, lb, pre, v, do, st_prevs, dsts, dlb_old) in enumerate(work):
                rev = d == 1
                dq_p, dki_p, dks_p, dv_p, rr_p, new_dsts = parts[d]
                dq, dki, dks, dv, rr = (jnp.concatenate(p_, axis=1) for p_ in (dq_p, dki_p, dks_p, dv_p, rr_p))
                x = pre["q"] * dq - pre["k"] * dki
                y = pre["k"] * dks
                if rev:
                    dg = _cumsum_rows(x - y) + _colsum(y) + rr
                else:
                    dg = _cumsum_rows(y - x) + (x - y) + _colsum(x) + rr
                inv_f = jnp.where(pre["f"] > TINY, 1.0 / pre["f"], 0.0)
                u = dg * inv_f - (dki + dks)
                results.append((dq, dv, (1.0 - lb) * pre["sig"] * pre["sn"] * u, dlb_old + _colsum(pre["sn"] * u),
                                new_dsts))
            for d, (zq, zf, zi, do_ref, s_ref, lb_ref, dq_ref, dv_ref, dzf_ref, dlb_ref) in enumerate(dirs):
                rows = work[d][0]
                dq, dv, dzf, dlb, new_dsts = results[d]
                dq_ref[rows, :] = dq.astype(dq_ref.dtype)
                dv_ref[rows, :] = dv.astype(dv_ref.dtype)
                dzf_ref[rows, :] = dzf.astype(dzf_ref.dtype)
                dlb_ref[...] = dlb
                for h in range(HEADS):
                    dst_ref[d, h] = new_dsts[h]
            return carry

        lax.fori_loop(0, cpb, chunk, 0, unroll=unroll)

    def rspec(cb, rev):
        return pl.BlockSpec((T, D), (lambda i: (i, cb)) if rev else (lambda i: (nb - 1 - i, cb)))

    def sspec(rev):
        shape = (cpb, HEADS, HEAD_DIM, HEAD_DIM)
        return pl.BlockSpec(shape, (lambda i: (i, 0, 0, 0)) if rev else (lambda i: (nb - 1 - i, 0, 0, 0)))

    lbspec = pl.BlockSpec((1, D), lambda i: (0, 0))
    half = jax.ShapeDtypeStruct((m, D), BF16)
    row = jax.ShapeDtypeStruct((1, D), F32)
    outs, extra = hosted_call(
        body, exch, name, (nb,),
        [rspec(ZQ, False), rspec(ZFF, False), rspec(ZI, False), rspec(0, False), sspec(False),
         rspec(ZQ, True), rspec(ZFB, True), rspec(ZI, True), rspec(0, True), sspec(True), lbspec, lbspec,
         rspec(0, False), rspec(0, True)],
        [rspec(0, False), rspec(0, False), rspec(0, True), rspec(0, True), rspec(0, False), rspec(0, True),
         lbspec, lbspec],
        [half, half, half, half, half, half, row, row],
        [pltpu.VMEM((2, HEADS, HEAD_DIM, HEAD_DIM), F32)],
        [z, z, z, d_o, s_f, z, z, z, d_o, s_b, lb_f, lb_b, b_f, b_b], ("arbitrary",))
    return outs, extra


def _heads(fn, *arrs):
    res = [fn(*[a[:, h * HEAD_DIM:(h + 1) * HEAD_DIM] for a in arrs]) for h in range(arrs[0].shape[1] // HEAD_DIM)]
    return [jnp.concatenate(parts, axis=1) for parts in zip(*res)]


def _hg_post(o_f, o_b, zg, g):
    def head(of, ob, zgh, gh):
        on, _ = _rms(of + ob)
        return (on * gh * _silu(zgh),)
    return _heads(head, o_f, o_b, zg, g)[0]


def _hg_post_bwd(da, o_f, o_b, zg, g):
    def head(dah, of, ob, zgh, gh):
        on, r = _rms(of + ob)
        sg = _silu(zgh)
        d_on = dah * sg
        return _rms_bwd(d_on, on, r, gh), dah * on * gh * _silu_grad(zgh), d_on * on
    d_o, dzg, dg = _heads(head, da, o_f, o_b, zg, g)
    return d_o, dzg, _colsum(dg)


def _sg_parts(zv, ln_g, ln_b):
    vg = _gelu(zv)
    xc = vg - _mean(vg)
    rstd = lax.rsqrt(_mean(xc * xc) + EPS)
    vh = xc * rstd
    return vh, rstd, vh * ln_g + ln_b


def _sg_lane_group(shape):
    return lax.broadcasted_iota(jnp.int32, shape, 1) < SG_GROUP_DIM


def _sg_mix(w, v16, transpose):
    rows = v16.shape[0]
    out = []
    for c in range(rows // SG_CHUNK):
        parts = []
        for j in range(SG_WIDTH // 128):
            vj = v16[c * SG_CHUNK:(c + 1) * SG_CHUNK, j * 128:(j + 1) * 128]
            w0 = w[(2 * j) * SG_CHUNK:(2 * j + 1) * SG_CHUNK]
            w1 = w[(2 * j + 1) * SG_CHUNK:(2 * j + 2) * SG_CHUNK]
            dot = _dot_tn if transpose else _dot
            parts.append(jnp.where(_sg_lane_group((SG_CHUNK, 128)), dot(w0, vj), dot(w1, vj)))
        out.append(jnp.concatenate(parts, axis=1))
    return jnp.concatenate(out, axis=0)


def _sg_fwd(zu, zv, w, bias, ln_g, ln_b):
    _, _, v = _sg_parts(zv, ln_g, ln_b)
    reps = zu.shape[0] // SG_CHUNK
    return _gelu(zu) * (_sg_mix(w, v.astype(BF16), False) + jnp.concatenate([bias] * reps, axis=0))


def _sg_bwd(db, zu, zv, w, bias, ln_g, ln_b):
    vh, rstd, v = _sg_parts(zv, ln_g, ln_b)
    v16 = v.astype(BF16)
    reps = zu.shape[0] // SG_CHUNK
    sg = _sg_mix(w, v16, False) + jnp.concatenate([bias] * reps, axis=0)
    dzu = db * sg * _gelu_grad(zu)
    dsg = db * _gelu(zu)
    dsg16 = dsg.astype(BF16)
    dv = _sg_mix(w, dsg16, True)
    low = _sg_lane_group((SG_CHUNK, 128))
    dw = []
    for g in range(SG_WIDTH // SG_GROUP_DIM):
        j, keep = g // 2, (low if g % 2 == 0 else jnp.logical_not(low))
        acc = jnp.zeros((SG_CHUNK, SG_CHUNK), F32)
        for c in range(reps):
            rows = slice(c * SG_CHUNK, (c + 1) * SG_CHUNK)
            dj = jnp.where(keep, dsg16[rows, j * 128:(j + 1) * 128], jnp.zeros((), BF16))
            acc = acc + _dot_nt(dj, v16[rows, j * 128:(j + 1) * 128])
        dw.append(acc)
    dbias = sum(dsg[c * SG_CHUNK:(c + 1) * SG_CHUNK] for c in range(reps))
    dvh = dv * ln_g
    dvg = rstd * (dvh - _mean(dvh) - vh * _mean(dvh * vh))
    dzuv = jnp.concatenate([dzu, dvg * _gelu_grad(zv)], axis=1)
    return (dzuv, jnp.concatenate(dw, axis=0), dbias, _colsum(dv * vh), _colsum(dv))


def lower_bounds(name, gamma_f, gamma_b):
    def body(gf_ref, gb_ref, lf_ref, lb_ref):
        for g_ref, o_ref in ((gf_ref, lf_ref), (gb_ref, lb_ref)):
            g0, g1 = g_ref[0:1, :], g_ref[1:2, :]
            mx = jnp.maximum(g0, g1)
            e0, e1 = jnp.exp(g0 - mx), jnp.exp(g1 - mx)
            sm0, sm1 = e0 / (e0 + e1), e1 / (e0 + e1)
            o_ref[0:1, :] = sm0 - sm0
            o_ref[1:2, :] = (sm0 + sm1) - sm0
    shp = jax.ShapeDtypeStruct(gamma_f.shape, F32)
    return pl.pallas_call(body, name=name, out_shape=[shp, shp])(gamma_f, gamma_b)


def lower_bounds_bwd(name, gamma_f, gamma_b, dlb_f, dlb_b):
    def body(gf_ref, gb_ref, df_ref, db_ref, of_ref, ob_ref):
        for g_ref, d_ref, o_ref in ((gf_ref, df_ref, of_ref), (gb_ref, db_ref, ob_ref)):
            g0, g1 = g_ref[0:1, :], g_ref[1:2, :]
            mx = jnp.maximum(g0, g1)
            e0, e1 = jnp.exp(g0 - mx), jnp.exp(g1 - mx)
            sm0, sm1 = e0 / (e0 + e1), e1 / (e0 + e1)
            d1 = d_ref[1:2, :] * sm0 * sm1
            o_ref[0:1, :] = -d1
            o_ref[1:2, :] = d1
    shp = jax.ShapeDtypeStruct(gamma_f.shape, F32)
    return pl.pallas_call(body, name=name, out_shape=[shp, shp])(gamma_f, gamma_b, dlb_f, dlb_b)


def _row(a, l):
    return a[l:l + 1]


class LocalPlan:
    def __init__(self, weights):
        self.W = weights
        self.grads = [dict() for _ in range(DEPTH)]

    def exch(self, host):
        return None

    def done(self, host, outs):
        pass

    def early_small(self, packed):
        pass


def local_step(x, p, target, S, plan):
    m = x.shape[0]

    def hmm(tag, *args, **kw):
        ex = plan.exch(tag)
        res = mm(tag, *args, exch=ex, **kw)
        if ex is None:
            return res
        plan.done(tag, res[1])
        return res[0]

    lb_f, lb_b = lower_bounds("lower_bounds", S["lb_gamma_fwd"], S["lb_gamma_bwd"])
    saved = []
    for l in range(DEPTH):
        t = f"l{l}_"
        W = plan.W[l]
        tm = 2048
        in_tile = (1024, 2048)
        ffn_tile = (2048, 2048)
        g_pre, g_post = _row(S["norm_mix_pre"], l), _row(S["norm_mix_post"], l)
        g_fpre, g_fpost = _row(S["norm_ffn_pre"], l), _row(S["norm_ffn_post"], l)
        hg_g = _row(S["hg_norm"], l)
        sg_w = S["sg_w"][l].reshape(SG_WIDTH // SG_GROUP_DIM * SG_CHUNK, SG_CHUNK).astype(BF16)
        sg_bias = jnp.repeat(S["sg_b"][l].T, SG_GROUP_DIM, axis=1)
        ln_g, ln_b = _row(S["sg_ln_g"], l), _row(S["sg_ln_b"], l)
        lbf, lbb = _row(lb_f, l), _row(lb_b, l)

        if l == 0:
            (h,) = rowwise(t + "pre_norm", lambda xv, g: (_rms(xv)[0] * g,), m, ins=[(x, D, 0)], consts=[g_pre],
                           outs=[(D, BF16)])
        z = hmm(t + "in_proj", h, W["w_in"], "nn", tm=in_tile[0], tn=in_tile[1])
        (o_f, o_b, s_f, s_b, b_f, b_b), extra = hgrn_fwd(t + "hgrn_fwd", z, lbf, lbb, exch=plan.exch(t + "hgrn_fwd"))
        plan.done(t + "hgrn_fwd", extra)
        (b_out,) = rowwise(t + "sgu_fwd", _sg_fwd, m, ins=[(z, SG_WIDTH, ZU), (z, SG_WIDTH, ZV)],
                           consts=[sg_w, sg_bias, ln_g, ln_b], outs=[(SG_WIDTH, BF16)])

        def post_pro(of, ob, zg, g):
            ao = _hg_post(of, ob, zg, g).astype(BF16)
            return [ao], [ao]
        a_out, pa = mm_fused(t + "proj_a", m, [(o_f, 0), (o_b, 0), (z, ZG)], [(W["w_a"], 0)], "nn", D,
                             prologue=post_pro, a_outs=[BF16], a_consts=[hg_g], epilogue=lambda tot: (tot,),
                             outs=[(D, BF16)])
        pb = mm(t + "proj_b", b_out, W["w_b"], "nn", BF16)

        def merge_pro(a, b, ga, gb):
            mg = (_sigmoid(ga) * a + _sigmoid(gb) * b).astype(BF16)
            return [mg], [mg]

        def post_pre(mixv, xv, gp, gf):
            x1 = xv + _rms(mixv)[0] * gp
            return mixv, x1, _rms(x1)[0] * gf
        merged, mix, x1, h2 = mm_fused(
            t + "out_proj", m, [(pa, 0), (pb, 0), (z, GA), (z, GB)], [(W["w_out"], 0)], "nn", D, prologue=merge_pro,
            a_outs=[BF16], e_ins=[(x, D, 0)], consts=[g_post, g_fpre], epilogue=post_pre,
            outs=[(D, F32), (D, F32), (D, BF16)])
        gu = hmm(t + "ffn_in", h2, W["w_gu"], "nn", BF16, tm=ffn_tile[0], tn=ffn_tile[1])

        def act_pro(gt, up):
            hd = (_silu(gt.astype(F32)) * up).astype(BF16)
            return [hd], [hd]
        hid, ff, x2 = mm_fused(
            t + "ffn_out", m, [(gu, 0), (gu, FFN_PAD // 1024)], [(W["w_down"], 0)], "nn", FFN_PAD, prologue=act_pro,
            a_outs=[BF16], e_ins=[(x1, D, 0)], consts=[g_fpost],
            epilogue=lambda f, xv, g: (f, xv + _rms(f)[0] * g), outs=[(D, F32), (D, F32)], tm=1024, resident=True)
        e = mm(t + "ple_proj", (p, l), W["w_ple"], "nn")

        if l + 1 < DEPTH:
            def ple_add(tv, xv, ev, g):
                x3 = xv + ev * _sigmoid(tv)
                return tv, x3, _rms(x3)[0] * g
            tg, x3, h_next = mm_fused(
                t + "ple_gate", m, [(x2, 0)], [(W["w_ple_gate"], 0)], "nn", D, a_to_epilogue=(0,), e_ins=[(e, D, 0)],
                consts=[_row(S["norm_mix_pre"], l + 1)], epilogue=ple_add, outs=[(D, F32), (D, F32), (D, BF16)])
        else:
            def ple_loss(tv, xv, ev, tgt):
                err = xv + ev * _sigmoid(tv) - tgt
                return tv, err * (1.0 / D), _colsum(err * err)
            tg, x3, loss_cols = mm_fused(
                t + "ple_gate", m, [(x2, 0)], [(W["w_ple_gate"], 0)], "nn", D, a_to_epilogue=(0,),
                e_ins=[(e, D, 0), (target, D, 0)], epilogue=ple_loss, outs=[(D, F32), (D, F32)], accs=[(1, D)])
            h_next = None
        saved.append(dict(x=x, h=h, z=z, o_f=o_f, o_b=o_b, s_f=s_f, s_b=s_b, b_f=b_f, b_b=b_b, a_out=a_out,
                          b_out=b_out, pa=pa, pb=pb,
                          merged=merged, mix=mix, x1=x1, h2=h2, gu=gu, hid=hid, ff=ff, x2=x2, e=e, tg=tg,
                          sg_w=sg_w, sg_bias=sg_bias))
        x, h = x3, h_next

    dx = x

    gs = {n: [None] * DEPTH for n in SMALL}
    dlb_f, dlb_b = [None] * DEPTH, [None] * DEPTH

    for l in reversed(range(DEPTH)):
        t = f"l{l}_bwd_"
        sv, W = saved[l], plan.W[l]
        tm, tk = 2048, 4096
        g_pre, g_post = _row(S["norm_mix_pre"], l), _row(S["norm_mix_post"], l)
        g_fpre, g_fpost = _row(S["norm_ffn_pre"], l), _row(S["norm_ffn_post"], l)
        hg_g = _row(S["hg_norm"], l)
        ln_g, ln_b = _row(S["sg_ln_g"], l), _row(S["sg_ln_b"], l)
        lbf, lbb = _row(lb_f, l), _row(lb_b, l)

        def wgrad(nm, tag, a, b):
            a_dtype = (a[0] if isinstance(a, tuple) else a).dtype
            plan.grads[l][nm] = mm(tag, a, b, "tn", BF16, tk=tk if a_dtype == BF16 else 2048)

        def ple_pro(d3, ev, tv):
            s = _sigmoid(tv)
            de_, dt_ = (d3 * s).astype(BF16), (d3 * ev * s * (1.0 - s)).astype(BF16)
            return [dt_], [dt_, de_]

        def ffn_post_bwd(d2p, d3, f, g):
            d2 = d3 + d2p
            fh, r = _rms(f)
            return d2, _rms_bwd(d2, fh, r, g), _colsum(d2 * fh)
        dt, de, dx2, dff, gs["norm_ffn_post"][l] = mm_fused(
            t + "ple_gate_dx", m, [(dx, 0), (sv["e"], 0), (sv["tg"], 0)], [(W["w_ple_gate"], 0)], "nt", D,
            prologue=ple_pro, a_outs=[BF16, BF16], a_to_epilogue=(0,), e_ins=[(sv["ff"], D, 0)], consts=[g_fpost],
            epilogue=ffn_post_bwd, outs=[(D, F32), (D, BF16)], accs=[(1, D)])
        wgrad("w_ple", t + "w_ple", (p, l), de)
        wgrad("w_ple_gate", t + "w_ple_gate", sv["x2"], dt)
        wgrad("w_down", t + "w_down", sv["hid"], dff)
        dhid = mm(t + "ffn_out_dx", dff, W["w_down"], "nt", BF16, tm=tm)

        def act_bwd(dh, gt, up):
            dh, gt = dh.astype(F32), gt.astype(F32)
            s = _sigmoid(gt)
            dg_ = (dh * up * (s * (1.0 + gt * (1.0 - s)))).astype(BF16)
            du_ = (dh * (gt * s)).astype(BF16)
            return [dg_, du_], [dg_, du_]

        def pre_post_bwd(dh, d2, x1v, mixv, gf, gp):
            xh, r1 = _rms(x1v)
            d1 = d2 + _rms_bwd(dh, xh, r1, gf)
            mh, rm = _rms(mixv)
            return d1, _rms_bwd(d1, mh, rm, gp), _colsum(dh * xh), _colsum(d1 * mh)
        off = FFN_PAD // 1024
        w_gu_t = W["w_gu"].T
        dgate, dup, dx1, dmix, gs["norm_ffn_pre"][l], gs["norm_mix_post"][l] = mm_fused(
            t + "ffn_in_dx", m, [(dhid, 0), (sv["gu"], 0), (sv["gu"], off)], [(w_gu_t, 0), (w_gu_t, off)], "nn",
            FFN_PAD, prologue=act_bwd, a_outs=[BF16, BF16], e_ins=[(dx2, D, 0), (sv["x1"], D, 0), (sv["mix"], D, 0)],
            consts=[g_fpre, g_post], epilogue=pre_post_bwd, outs=[(D, F32), (D, BF16)], accs=[(1, D), (1, D)],
            resident=True)
        wgrad("w_gate", t + "w_gate", sv["h2"], dgate)
        wgrad("w_up", t + "w_up", sv["h2"], dup)
        wgrad("w_out", t + "w_out", sv["merged"], dmix)

        def merge_bwd(dm, a, b, gab):
            sa, sb = _sigmoid(gab[:, :D]), _sigmoid(gab[:, D:])
            dgab = jnp.concatenate([dm * a * sa * (1.0 - sa), dm * b * sb * (1.0 - sb)], axis=1)
            return dm * sa, dm * sb, dgab
        dpa, dpb, dz = mm_fused(
            t + "out_proj_dx", m, [(dmix, 0)], [(W["w_out"], 0)], "nt", D,
            e_ins=[(sv["pa"], D, 0), (sv["pb"], D, 0), (sv["z"], 2 * D, 3)], epilogue=merge_bwd,
            outs=[(D, BF16), (D, BF16)], alias_outs=[(jax.ShapeDtypeStruct((m, N_IN), BF16), 2 * D, 3)])
        wgrad("w_a", t + "w_a", sv["a_out"], dpa)
        wgrad("w_b", t + "w_b", sv["b_out"], dpb)
        db = mm(t + "proj_b_dx", dpb, W["w_b"], "nt")

        dz, dsw, dbias, gs["sg_ln_g"][l], gs["sg_ln_b"][l] = rowwise(
            t + "sgu", _sg_bwd, m, ins=[(db, SG_WIDTH, 0), (sv["z"], SG_WIDTH, ZU), (sv["z"], SG_WIDTH, ZV)],
            consts=[sv["sg_w"], sv["sg_bias"], ln_g, ln_b], alias_outs=[(dz, 2 * SG_WIDTH, 5)],
            accs=[(SG_WIDTH // SG_GROUP_DIM * SG_CHUNK, SG_CHUNK), (SG_CHUNK, SG_WIDTH), (1, SG_WIDTH), (1, SG_WIDTH)])
        gs["sg_w"][l] = dsw.reshape(1, SG_WIDTH // SG_GROUP_DIM, SG_CHUNK, SG_CHUNK)
        gs["sg_b"][l] = dbias.reshape(SG_CHUNK, SG_WIDTH // SG_GROUP_DIM, SG_GROUP_DIM).sum(-1).T[None]

        d_o, dz, gs["hg_norm"][l] = mm_fused(
            t + "proj_a_dx", m, [(dpa, 0)], [(W["w_a"], 0)], "nt", D,
            e_ins=[(sv["o_f"], D, 0), (sv["o_b"], D, 0), (sv["z"], D, ZG)], consts=[hg_g], epilogue=_hg_post_bwd,
            outs=[(D, BF16)], alias_outs=[(dz, D, ZG)], accs=[(1, D)], tm=256)
        if l == 0:
            part = {n: (g if not isinstance(g, list) else jnp.concatenate(
                [jnp.zeros((1,) + g[1].shape[1:], F32) if gl is None else gl for gl in g], axis=0))
                for n, g in gs.items()}
            plan.early_small(_pack([part[n].reshape(S[n].shape) for n in SMALL]))
        (dq_f, dv_f, dq_b, dv_b, dzf_f, dzf_b, dlb_f[l], dlb_b[l]), extra = hgrn_bwd(
            t + "hgrn", sv["z"], d_o, sv["s_f"], sv["s_b"], sv["b_f"], sv["b_b"], lbf, lbb,
            exch=plan.exch(t + "hgrn"))
        plan.done(t + "hgrn", extra)

        def combine(dqf, dqb, dvf, dvb, dff_, dfb_, zq):
            dq = dqf.astype(F32) + dqb.astype(F32)
            dv = dvf.astype(F32) + dvb.astype(F32)
            return (jnp.concatenate([(dq * _silu_grad(zq)).astype(BF16), dff_, dfb_, dv.astype(BF16)], axis=1),)
        (dz,) = rowwise(t + "hgrn_combine", combine, m,
                        ins=[(dq_f, D, 0), (dq_b, D, 0), (dv_f, D, 0), (dv_b, D, 0), (dzf_f, D, 0), (dzf_b, D, 0),
                             (sv["z"], D, ZQ)], alias_outs=[(dz, 4 * D, 0)], tm=128)
        wgrad("w_in", t + "w_in", sv["h"], dz)

        def pre_bwd(dhv, d1, xv, g):
            xh, r = _rms(xv)
            return d1 + _rms_bwd(dhv, xh, r, g), _colsum(dhv * xh)
        ex = plan.exch(t + "in_proj_dx")
        if l == 0:
            res = mm_fused(t + "in_proj_dx", m, [(dz, 0)], [(W["w_in"], 0)], "nt", N_IN,
                           e_ins=[(dx1, D, 0), (sv["x"], D, 0)], consts=[g_pre], epilogue=pre_bwd, outs=[(D, F32)],
                           accs=[(1, D)], tm=1024, e_single=True, exch=ex)
        else:
            res = mm_fused(t + "in_proj_dx", m, [(dz, 0)], [(W["w_in"].T, 0)], "nn", N_IN,
                           e_ins=[(dx1, D, 0), (sv["x"], D, 0)], consts=[g_pre], epilogue=pre_bwd, outs=[(D, F32)],
                           accs=[(1, D)], tm=512, resident=True, exch=ex)
        if ex is not None:
            res, extra = res
            plan.done(t + "in_proj_dx", extra)
        dx, gs["norm_mix_pre"][l] = res
        saved[l] = None
        if l == DEPTH - 1:
            none = jnp.zeros((1, D), F32)
            gs["lb_gamma_fwd"], gs["lb_gamma_bwd"] = lower_bounds_bwd(
                "lower_bounds_bwd", S["lb_gamma_fwd"], S["lb_gamma_bwd"], jnp.concatenate([none, dlb_f[l]], axis=0),
                jnp.concatenate([none, dlb_b[l]], axis=0))

    small ={n: (g if not isinstance(g, list) else jnp.concatenate(g, axis=0)).reshape(S[n].shape)
             for n, g in gs.items()}
    return loss_cols, dx, small


def cast_pad(name, w, rows_p, cols_p):
    _, r, c = w.shape

    def body(w_ref, o_ref):
        if (rows_p, cols_p) != (r, c):
            o_ref[...] = jnp.zeros(o_ref.shape, BF16)
        o_ref[0:r, 0:c] = w_ref[...].astype(BF16)

    return pl.pallas_call(
        body, name=name, grid=(DEPTH,), in_specs=[pl.BlockSpec((None, r, c), lambda l: (l, 0, 0))],
        out_specs=pl.BlockSpec((None, rows_p, cols_p), lambda l: (l, 0, 0)),
        out_shape=jax.ShapeDtypeStruct((DEPTH, rows_p, cols_p), BF16), compiler_params=_params(("parallel",)),
    )(w)


def _shard_shape(n, shape):
    axis, size, _, _ = LAYOUT[n]
    _, r, c = shape
    return (size, c) if axis == 0 else (r, size)


class DistPlan:
    def __init__(self, shards):
        self.shards = shards
        self.W = [dict() for _ in range(DEPTH)]
        self.grads = [dict() for _ in range(DEPTH)]
        self.slots = [dict() for _ in range(DEPTH)]
        rest = [n for n in BIG if n != "w_in"]
        ffn = ["w_gate", "w_up", "w_down"]
        self.schedule = {
            "l0_in_proj": ("gather", [(0, n) for n in rest]),
            "l0_hgrn_fwd": ("gather", [(1, n) for n in BIG if n not in ffn]),
            "l0_ffn_in": ("gather", [(1, n) for n in ffn]),
            "l1_bwd_hgrn": ("scatter", [(1, n) for n in rest]),
            "l1_bwd_in_proj_dx": ("scatter", [(1, "w_in")]),
            "l0_bwd_hgrn": ("scatter", [(0, n) for n in rest]),
            "l0_bwd_in_proj_dx": ("scatter", [(0, "w_in")]),
        }
        self.pending = {}
        self.small_part = self.small_slots = None
        axis, size, dst, _ = LAYOUT["w_in"]
        self.W[0][dst] = gather_two_level("gather_l0_w_in", shards["w_in"], 0, axis, size, GATHERED[dst])

    def _gather(self, host, parts):
        srcs, dsts, items, keys = [], [], [], []
        for layer, n in parts:
            axis, size, dst, base = LAYOUT[n]
            if (layer, dst) not in keys:
                keys.append((layer, dst))
                dsts.append((GATHERED[dst], BF16))
            srcs.append(self.shards[n])
            items.append(("gather", len(srcs) - 1, keys.index((layer, dst)), axis, size, base, layer))
        self.pending[host] = ("gather", keys)
        return Exchange(srcs, dsts, items)

    def _scatter(self, host, parts):
        srcs, dsts, items = [], [], []
        for layer, n in parts:
            axis, size, _, _ = LAYOUT[n]
            srcs.append(self.grads[layer][n])
            dsts.append(((NDEV,) + _shard_shape(n, self.shards[n].shape), BF16))
            items.append(("scatter", len(srcs) - 1, len(dsts) - 1, axis, size, 0, None))
        keys = list(parts)
        if host == "l0_bwd_hgrn" and self.small_part is not None:
            srcs.append(self.small_part)
            dsts.append(((NDEV,) + self.small_part.shape, F32))
            items.append(("copies", len(srcs) - 1, len(dsts) - 1, 0, 0, 0, None))
            keys.append(("small", None))
        self.pending[host] = ("scatter", keys)
        return Exchange(srcs, dsts, items)

    def early_small(self, packed):
        self.small_part = packed

    def exch(self, host):
        if host not in self.schedule:
            return None
        kind, parts = self.schedule[host]
        return self._gather(host, parts) if kind == "gather" else self._scatter(host, parts)

    def done(self, host, outs):
        if host not in self.pending:
            return
        kind, keys = self.pending.pop(host)
        for (layer, n), arr in zip(keys, outs):
            if layer == "small":
                self.small_slots = arr
            else:
                (self.W if kind == "gather" else self.slots)[layer][n] = arr


def adam(name, w, m_, v_, tr, g=None, slots=None):
    L, r, c = w.shape
    assert r % tr == 0
    nt = r // tr
    n_s = 0 if slots is None else L

    def body(*refs):
        s_refs = refs[:n_s]
        g_ref = refs[n_s] if g is not None else None
        w_ref, m_ref, v_ref, g_out, d_out, m_out, v_out = refs[n_s + (g is not None):]

        def update(gv):
            if g_ref is not None:
                gv = gv + g_ref[...] if gv is not None else g_ref[...]
            m2 = B1 * m_ref[...] + (1.0 - B1) * gv
            v2 = B2 * v_ref[...] + (1.0 - B2) * (gv * gv)
            m_hat = m2 / (1.0 - B1 ** STEP)
            v_hat = v2 / (1.0 - B2 ** STEP)
            g_out[...] = gv
            d_out[...] = -LR * (m_hat / (jnp.sqrt(v_hat) + AEPS) + WD * w_ref[...])
            m_out[...] = m2
            v_out[...] = v2

        if slots is None:
            update(None)
            return
        for layer, s_ref in enumerate(s_refs):
            @pl.when(pl.program_id(0) == layer)
            def _():
                gv = s_ref[0][:, :c].astype(F32)
                for j in range(1, NDEV):
                    gv = gv + s_ref[j][:, :c].astype(F32)
                update(gv)

    spec = pl.BlockSpec((None, tr, c), lambda l, i: (l, i, 0))
    arrs, specs = [], []
    if slots is not None:
        assert len(slots) == L and L <= 2
        arrs = list(slots)
        cp = slots[0].shape[2]
        specs = [pl.BlockSpec((NDEV, tr, cp), lambda l, i: (0, i * (1 - l) + (nt - 1) * l, 0)),
                 pl.BlockSpec((NDEV, tr, cp), lambda l, i: (0, i * l, 0))][:L]
    if g is not None:
        arrs.append(g)
        specs.append(spec)
    shp = jax.ShapeDtypeStruct(w.shape, F32)
    return pl.pallas_call(
        body, name=name, grid=(L, nt), in_specs=specs + [spec, spec, spec], out_specs=[spec] * 4,
        out_shape=[shp] * 4, compiler_params=_params(("arbitrary", "arbitrary")),
    )(*arrs, w, m_, v_)


def _pack(arrs):
    parts = []
    for a in arrs:
        a2 = a.reshape(-1, D)
        parts.append(jnp.pad(a2, ((0, -a2.shape[0] % 8), (0, 0))))
    return jnp.concatenate(parts, axis=0)


def _unpack(buf, shapes):
    out, off = [], 0
    for s in shapes:
        rows = 1
        for d_ in s:
            rows *= d_
        rows //= D
        out.append(buf[off:off + rows].reshape(s))
        off += rows + (-rows % 8)
    return out


def kernel(x, p, norm_mix_pre, w_in, lb_gamma_fwd, lb_gamma_bwd, hg_norm, sg_w, sg_b, sg_ln_g, sg_ln_b, w_a, w_b, w_out, norm_mix_post, norm_ffn_pre, w_gate, w_up, w_down, norm_ffn_post, w_ple, w_ple_gate, loss_target, m_norm_mix_pre, m_w_in, m_lb_gamma_fwd, m_lb_gamma_bwd, m_hg_norm, m_sg_w, m_sg_b, m_sg_ln_g, m_sg_ln_b, m_w_a, m_w_b, m_w_out, m_norm_mix_post, m_norm_ffn_pre, m_w_gate, m_w_up, m_w_down, m_norm_ffn_post, m_w_ple, m_w_ple_gate, v_norm_mix_pre, v_w_in, v_lb_gamma_fwd, v_lb_gamma_bwd, v_hg_norm, v_sg_w, v_sg_b, v_sg_ln_g, v_sg_ln_b, v_w_a, v_w_b, v_w_out, v_norm_mix_post, v_norm_ffn_pre, v_w_gate, v_w_up, v_w_down, v_norm_ffn_post, v_w_ple, v_w_ple_gate):
    a = dict(zip(INPUTS, (x, p, norm_mix_pre, w_in, lb_gamma_fwd, lb_gamma_bwd, hg_norm, sg_w, sg_b, sg_ln_g, sg_ln_b, w_a, w_b, w_out, norm_mix_post, norm_ffn_pre, w_gate, w_up, w_down, norm_ffn_post, w_ple, w_ple_gate, loss_target, m_norm_mix_pre, m_w_in, m_lb_gamma_fwd, m_lb_gamma_bwd, m_hg_norm, m_sg_w, m_sg_b, m_sg_ln_g, m_sg_ln_b, m_w_a, m_w_b, m_w_out, m_norm_mix_post, m_norm_ffn_pre, m_w_gate, m_w_up, m_w_down, m_norm_ffn_post, m_w_ple, m_w_ple_gate, v_norm_mix_pre, v_w_in, v_lb_gamma_fwd, v_lb_gamma_bwd, v_hg_norm, v_sg_w, v_sg_b, v_sg_ln_g, v_sg_ln_b, v_w_a, v_w_b, v_w_out, v_norm_mix_post, v_norm_ffn_pre, v_w_gate, v_w_up, v_w_down, v_norm_ffn_post, v_w_ple, v_w_ple_gate)))
    m = x.shape[1]

    shards = {n: cast_pad("cast_" + n, a[n], *_shard_shape(n, a[n].shape)) for n in BIG}
    plan = DistPlan(shards)
    loss_cols, dx, gs = local_step(x[0], p[:, 0], loss_target[0], {n: a[n] for n in SMALL}, plan)
    loss = lax.psum(jnp.sum(loss_cols) * (0.5 / D), ("x", "y", "c"))

    small_shapes = [a[n].shape for n in SMALL]
    rows = plan.small_slots.shape[1]
    late = allreduce_small("allreduce_small", jnp.pad(gs["norm_mix_pre"][0:1], ((0, 7), (0, 0))))
    g_late = jnp.pad(late, ((0, rows - 8), (0, 0)))[None]

    res = {}
    row_tiles = {"w_in": 128, "w_a": 128, "w_b": 512, "w_out": 128, "w_gate": 128, "w_up": 128, "w_down": 88,
                 "w_ple": 256, "w_ple_gate": 128}
    for n in BIG:
        res[n] = adam("adam_" + n, a[n], a["m_" + n], a["v_" + n], row_tiles[n],
                      slots=[plan.slots[l][n] for l in range(DEPTH)])
    packed = [_pack([a[pre + n] for n in SMALL])[None] for pre in ("", "m_", "v_")]
    small_res = adam("adam_small", packed[0], packed[1], packed[2], rows // 2, g=g_late, slots=[plan.small_slots])
    small_res = [_unpack(r_[0], small_shapes) for r_ in small_res]
    for i, n in enumerate(SMALL):
        res[n] = tuple(small_res[k][i] for k in range(4))

    outs = [loss, dx.reshape(1, m, D)]
    for k in range(4):
        outs += [res[n][k] for n in WEIGHTS]
    return tuple(outs)
```

```python
import jax
import jax.numpy as jnp
from jax import lax
from jax.experimental import pallas as pl
from jax.experimental.pallas import tpu as pltpu

F32 = jnp.float32
BF16 = jnp.bfloat16

D = 1024
N_IN = 8192
HEADS = 8
HEAD_DIM = 128
SG_CHUNK = 128
SG_WIDTH = 512
SG_GROUP_DIM = 64
FFN = 2816
PLE_DIM = 256
EPS = 1e-6
DEPTH = 2
ZQ, ZFF, ZFB, ZI, ZG, GA, GB = 0, 1, 2, 3, 4, 6, 7
ZU, ZV = 10, 11

NDEV = 8
FFN_SHARD_PAD = 384
FFN_PAD = NDEV * FFN_SHARD_PAD

LR, B1, B2, AEPS, WD, STEP = 0.001, 0.9, 0.999, 1e-08, 0.01, 10

ROW_TILE = 256
BIG_TILE = 2048
WGRAD_TOKENS = 4096
HG_CHUNK = 64
HG_BLOCK_FWD = 256
HG_BLOCK_BWD = 256
EXP_CLAMP = 80.0
PROLOGUE_CHUNK = 256
TINY = float(jnp.finfo(jnp.float32).tiny)
VMEM_LIMIT = 56 * 1024 * 1024

BIG = ["w_in", "w_a", "w_b", "w_out", "w_gate", "w_up", "w_down", "w_ple", "w_ple_gate"]
SMALL = ["norm_mix_pre", "lb_gamma_fwd", "lb_gamma_bwd", "hg_norm", "sg_w", "sg_b", "sg_ln_g", "sg_ln_b",
         "norm_mix_post", "norm_ffn_pre", "norm_ffn_post"]
WEIGHTS = ["norm_mix_pre", "w_in", "lb_gamma_fwd", "lb_gamma_bwd", "hg_norm", "sg_w", "sg_b", "sg_ln_g", "sg_ln_b",
           "w_a", "w_b", "w_out", "norm_mix_post", "norm_ffn_pre", "w_gate", "w_up", "w_down", "norm_ffn_post",
           "w_ple", "w_ple_gate"]
INPUTS = (["x", "p"] + WEIGHTS + ["loss_target"] + ["m_" + n for n in WEIGHTS] + ["v_" + n for n in WEIGHTS])
LAYOUT = {
    "w_in": (1, 1024, "w_in", 0), "w_a": (0, 128, "w_a", 0), "w_b": (1, 128, "w_b", 0),
    "w_out": (0, 128, "w_out", 0), "w_gate": (1, FFN_SHARD_PAD, "w_gu", 0),
    "w_up": (1, FFN_SHARD_PAD, "w_gu", FFN_PAD), "w_down": (0, FFN_SHARD_PAD, "w_down", 0),
    "w_ple": (1, 128, "w_ple", 0), "w_ple_gate": (0, 128, "w_ple_gate", 0),
}
GATHERED = {"w_in": (D, N_IN), "w_a": (D, D), "w_b": (SG_WIDTH, D), "w_out": (D, D), "w_gu": (D, 2 * FFN_PAD),
            "w_down": (FFN_PAD, D), "w_ple": (PLE_DIM, D), "w_ple_gate": (D, D)}


def _params(sem):
    return pltpu.CompilerParams(dimension_semantics=sem, vmem_limit_bytes=VMEM_LIMIT)


def _dot(a, b):
    return lax.dot_general(a, b, (((1,), (0,)), ((), ())), preferred_element_type=F32)


def _dot_nt(a, b):
    return lax.dot_general(a, b, (((1,), (1,)), ((), ())), preferred_element_type=F32)


def _dot_tn(a, b):
    return lax.dot_general(a, b, (((0,), (0,)), ((), ())), preferred_element_type=F32)


def _sigmoid(x):
    return jax.nn.sigmoid(x)


def _silu(x):
    return x * _sigmoid(x)


def _silu_grad(x):
    s = _sigmoid(x)
    return s * (1.0 + x * (1.0 - s))


def _gelu(x):
    return 0.5 * x * (1.0 + lax.erf(x * 0.7071067811865476))


def _gelu_grad(x):
    return 0.5 * (1.0 + lax.erf(x * 0.7071067811865476)) + x * jnp.exp(-0.5 * x * x) * 0.3989422804014327


def _mean(x):
    return jnp.mean(x, axis=-1, keepdims=True)


def _colsum(x):
    return jnp.sum(x, axis=0, keepdims=True)


def _rms(x):
    r = lax.rsqrt(_mean(x * x) + EPS)
    return x * r, r


def _rms_bwd(dy, xh, r, g):
    dyg = dy * g
    return r * (dyg - xh * _mean(dyg * xh))


MESH = pl.DeviceIdType.MESH
ANY = pl.BlockSpec(memory_space=pl.ANY)


def _slab(ref, axis, start, size):
    idx = [slice(None)] * 2
    idx[axis] = pl.ds(start, size)
    return ref.at[tuple(idx)]


class Exchange:
    def __init__(self, srcs, dsts, items):
        self.srcs, self.dsts, self.items = list(srcs), list(dsts), list(items)

    def specs(self):
        n = len(self.items)
        sems = [pltpu.SemaphoreType.DMA((n * (NDEV - 1),)), pltpu.SemaphoreType.DMA((n * (NDEV - 1),)),
                pltpu.SemaphoreType.DMA((n,))]
        return ([ANY] * len(self.srcs), [ANY] * len(self.dsts),
                [jax.ShapeDtypeStruct(s, dt) for (s, dt) in self.dsts], sems)

    def copies(self, src, dst, send_sem, recv_sem, loc_sem):
        x, y, c = lax.axis_index("x"), lax.axis_index("y"), lax.axis_index("c")
        me = 4 * x + 2 * y + c
        starts, waits = [], []
        for n, (kind, si, di, axis, size, base, layer) in enumerate(self.items):
            def views(to_dev, from_dev):
                if kind == "gather":
                    return (src[si].at[layer],
                            _slab(dst[di], axis, base + pl.multiple_of(from_dev * size, 128), size))
                if kind == "copies":
                    return src[si], dst[di].at[from_dev]
                return _slab(src[si], axis, base + pl.multiple_of(to_dev * size, 128), size), dst[di].at[from_dev]

            s_own, d_own = views(me, me)
            own = pltpu.make_async_copy(s_own, d_own, loc_sem.at[n])
            starts.append(own)
            waits.append(own)
            for k in range(1, NDEV):
                px = 1 - x if k & 4 else x
                py = 1 - y if k & 2 else y
                pc = 1 - c if k & 1 else c
                peer = 4 * px + 2 * py + pc
                s_out, _ = views(peer, me)
                _, d_in = views(me, peer)
                sem = n * (NDEV - 1) + k - 1
                starts.append(pltpu.make_async_remote_copy(s_out, d_own, send_sem.at[sem], recv_sem.at[sem],
                                                           device_id=(px, py, pc), device_id_type=MESH))
                waits.append(pltpu.make_async_remote_copy(s_out, d_in, send_sem.at[sem], recv_sem.at[sem],
                                                          device_id=(px, py, pc), device_id_type=MESH))
        return starts, waits


def gather_two_level(name, shards, layer, axis, size, full_shape):
    def body(src, dst, send_sem, recv_sem, loc_sem):
        x, y, c = lax.axis_index("x"), lax.axis_index("y"), lax.axis_index("c")
        mine = src.at[layer]
        chips = [(1 - x, y), (x, 1 - y), (1 - x, 1 - y)]

        def slab(px, py, pc):
            return _slab(dst, axis, pl.multiple_of((4 * px + 2 * py + pc) * size, 128), size)

        def copy(k, from_ref, block, to):
            return pltpu.make_async_remote_copy(from_ref, slab(*block), send_sem.at[k], recv_sem.at[k], device_id=to,
                                                device_id_type=MESH)

        own = pltpu.make_async_copy(mine, slab(x, y, c), loc_sem)
        own.start()
        first = [copy(0, mine, (x, y, c), (x, y, 1 - c))]
        first += [copy(1 + j, mine, (x, y, c), (*chip, c)) for j, chip in enumerate(chips)]
        for cp in first:
            cp.start()
        passed = []
        for j, chip in enumerate(chips):
            copy(1 + j, mine, (*chip, c), (x, y, c)).wait_recv()
            fwd = copy(4 + j, slab(*chip, c), (*chip, c), (x, y, 1 - c))
            fwd.start()
            passed.append(fwd)
        copy(0, mine, (x, y, 1 - c), (x, y, c)).wait_recv()
        for j, chip in enumerate(chips):
            copy(4 + j, mine, (*chip, 1 - c), (x, y, c)).wait_recv()
        for cp in first + passed:
            cp.wait_send()
        own.wait()

    return pl.pallas_call(
        body, name=name, in_specs=[ANY], out_specs=ANY, out_shape=jax.ShapeDtypeStruct(full_shape, shards.dtype),
        scratch_shapes=[pltpu.SemaphoreType.DMA((NDEV - 1,)), pltpu.SemaphoreType.DMA((NDEV - 1,)),
                        pltpu.SemaphoreType.DMA(())],
        compiler_params=pltpu.CompilerParams(has_side_effects=True))(shards)


def hosted_call(body, exch, name, grid, in_specs, out_specs, out_shape, scratch_shapes, operands, semantics,
                aliases=None):
    aliases = aliases or {}
    if exch is None:
        res = pl.pallas_call(body, name=name, grid=grid, in_specs=in_specs, out_specs=out_specs, out_shape=out_shape,
                             scratch_shapes=scratch_shapes, input_output_aliases=aliases,
                             compiler_params=_params(semantics))(*operands)
        return list(res), []
    n_in, n_out, n_scr = len(in_specs), len(out_specs), len(scratch_shapes)
    e_in, e_out, e_shape, e_scr = exch.specs()
    ns, nd = len(e_in), len(e_out)

    def at_step(last):
        cond = None
        for ax, n in enumerate(grid):
            c = pl.program_id(ax) == (n - 1 if last else 0)
            cond = c if cond is None else jnp.logical_and(cond, c)
        return cond

    def wrapped(*refs):
        ins, src = refs[:n_in], refs[n_in:n_in + ns]
        o0 = n_in + ns
        outs, dst = refs[o0:o0 + n_out], refs[o0 + n_out:o0 + n_out + nd]
        s0 = o0 + n_out + nd
        scr, sems = refs[s0:s0 + n_scr], refs[s0 + n_scr:]

        @pl.when(at_step(False))
        def _():
            for cp in exch.copies(src, dst, *sems)[0]:
                cp.start()

        body(*ins, *outs, *scr)

        @pl.when(at_step(True))
        def _():
            for cp in exch.copies(src, dst, *sems)[1]:
                cp.wait()

    res = pl.pallas_call(
        wrapped, name=name, grid=grid, in_specs=list(in_specs) + e_in, out_specs=list(out_specs) + e_out,
        out_shape=list(out_shape) + e_shape, scratch_shapes=list(scratch_shapes) + e_scr,
        input_output_aliases=aliases,
        compiler_params=pltpu.CompilerParams(dimension_semantics=("arbitrary",) * len(grid),
                                             vmem_limit_bytes=VMEM_LIMIT, has_side_effects=True),
    )(*operands, *exch.srcs)
    return list(res[:n_out]), list(res[n_out:])


def allreduce_small(name, part):
    rows, width = part.shape

    def body(p_ref, o_ref, buf, send_sem, recv_sem):
        x, y, c = lax.axis_index("x"), lax.axis_index("y"), lax.axis_index("c")
        me = 4 * x + 2 * y + c
        buf[me] = p_ref[...]
        waits = []
        for k in range(1, NDEV):
            px = 1 - x if k & 4 else x
            py = 1 - y if k & 2 else y
            pc = 1 - c if k & 1 else c
            peer = 4 * px + 2 * py + pc
            pltpu.make_async_remote_copy(p_ref, buf.at[me], send_sem.at[k - 1], recv_sem.at[k - 1],
                                         device_id=(px, py, pc), device_id_type=MESH).start()
            waits.append(pltpu.make_async_remote_copy(p_ref, buf.at[peer], send_sem.at[k - 1], recv_sem.at[k - 1],
                                                      device_id=(px, py, pc), device_id_type=MESH))
        for w in waits:
            w.wait()
        acc = buf[0]
        for j in range(1, NDEV):
            acc = acc + buf[j]
        o_ref[...] = acc

    vmem = pl.BlockSpec(memory_space=pltpu.VMEM)
    return pl.pallas_call(
        body, name=name, in_specs=[vmem], out_specs=vmem, out_shape=jax.ShapeDtypeStruct((rows, width), F32),
        scratch_shapes=[pltpu.VMEM((NDEV, rows, width), F32), pltpu.SemaphoreType.DMA((NDEV - 1,)),
                        pltpu.SemaphoreType.DMA((NDEV - 1,))],
        compiler_params=pltpu.CompilerParams(vmem_limit_bytes=VMEM_LIMIT, has_side_effects=True),
    )(part)


def rowwise(name, fn, m, ins=(), consts=(), outs=(), alias_outs=(), accs=(), tm=ROW_TILE):
    tm = min(tm, m)
    n_in, n_c, n_o, n_al, n_ac = len(ins), len(consts), len(outs), len(alias_outs), len(accs)
    held = [a for (a, _, _) in alias_outs if not isinstance(a, jax.ShapeDtypeStruct)]
    n_held = len(held)

    def body(*refs):
        in_refs = refs[:n_in + n_c]
        out_refs = refs[n_in + n_c + n_held:]
        vals = fn(*[r[...] for r in in_refs])
        if not isinstance(vals, (tuple, list)):
            vals = (vals,)
        for r, v in zip(out_refs[:n_o + n_al], vals[:n_o + n_al]):
            r[...] = v.astype(r.dtype)
        if n_ac:
            acc_refs = out_refs[n_o + n_al:]

            @pl.when(pl.program_id(0) == 0)
            def _():
                for r in acc_refs:
                    r[...] = jnp.zeros(r.shape, F32)

            for r, v in zip(acc_refs, vals[n_o + n_al:]):
                r[...] += v

    def col(cb):
        return lambda i: (i, cb)

    in_specs = [pl.BlockSpec((tm, w), col(cb)) for (_, w, cb) in ins]
    in_specs += [pl.BlockSpec(c.shape, lambda i, nd=c.ndim: (0,) * nd) for c in consts]
    in_specs += [ANY for _ in held]
    out_shape = [jax.ShapeDtypeStruct((m, w), dt) for (w, dt) in outs]
    out_specs = [pl.BlockSpec((tm, w), col(0)) for (w, _) in outs]
    out_shape += [jax.ShapeDtypeStruct(a.shape, a.dtype) for (a, _, _) in alias_outs]
    out_specs += [pl.BlockSpec((tm, w), col(cb)) for (_, w, cb) in alias_outs]
    out_shape += [jax.ShapeDtypeStruct(s, F32) for s in accs]
    out_specs += [pl.BlockSpec(s, lambda i: (0, 0)) for s in accs]
    aliases, k_in = {}, n_in + n_c
    for k, (a, _, _) in enumerate(alias_outs):
        if not isinstance(a, jax.ShapeDtypeStruct):
            aliases[k_in] = n_o + k
            k_in += 1
    return pl.pallas_call(
        body, name=name, grid=(m // tm,), in_specs=in_specs, out_specs=out_specs, out_shape=out_shape,
        input_output_aliases=aliases,
        compiler_params=_params(("arbitrary",) if n_ac else ("parallel",)),
    )(*[a for (a, _, _) in ins], *consts, *held)


def _operand(arr, bshape, imap):
    if isinstance(arr, tuple):
        arr, lead = arr
        return arr, pl.BlockSpec((None,) + bshape, lambda *g: (lead,) + imap(*g))
    return arr, pl.BlockSpec(bshape, imap)


def _shape2(arr):
    return arr[0].shape[1:] if isinstance(arr, tuple) else arr.shape


def mm(name, a, b, mode, out_dtype=F32, tm=1024, tn=1024, tk=1024, exch=None):
    sa, sb = _shape2(a), _shape2(b)
    if mode == "nn":
        (M, K), N = sa, sb[1]
    elif mode == "nt":
        (M, K), N = sa, sb[0]
    else:
        (K, M), N = sa, sb[1]
    tm, tn, tk = min(tm, M), min(tn, N), min(tk, K)
    assert M % tm == 0 and N % tn == 0 and K % tk == 0, (name, M, N, K)
    nk = K // tk
    if mode == "nn":
        a_arr, a_spec = _operand(a, (tm, tk), lambda i, j, k: (i, k))
        b_arr, b_spec = _operand(b, (tk, tn), lambda i, j, k: (k, j))
        dot = _dot
    elif mode == "nt":
        a_arr, a_spec = _operand(a, (tm, tk), lambda i, j, k: (i, k))
        b_arr, b_spec = _operand(b, (tn, tk), lambda i, j, k: (j, k))
        dot = _dot_nt
    else:
        a_arr, a_spec = _operand(a, (tk, tm), lambda i, j, k: (k, i))
        b_arr, b_spec = _operand(b, (tk, tn), lambda i, j, k: (k, j))
        dot = _dot_tn

    def body(a_ref, b_ref, o_ref, *acc):
        part = dot(a_ref[...].astype(BF16), b_ref[...].astype(BF16))
        if nk == 1:
            o_ref[...] = part.astype(o_ref.dtype)
            return
        acc_ref, k = acc[0], pl.program_id(2)

        @pl.when(k == 0)
        def _():
            acc_ref[...] = part

        @pl.when(k > 0)
        def _():
            acc_ref[...] += part

        @pl.when(k == nk - 1)
        def _():
            o_ref[...] = acc_ref[...].astype(o_ref.dtype)

    outs, extra = hosted_call(
        body, exch, name, (M // tm, N // tn, nk), [a_spec, b_spec], [pl.BlockSpec((tm, tn), lambda i, j, k: (i, j))],
        [jax.ShapeDtypeStruct((M, N), out_dtype)], [pltpu.VMEM((tm, tn), F32)] if nk > 1 else [], [a_arr, b_arr],
        ("parallel", "parallel", "arbitrary"))
    return outs[0] if exch is None else (outs[0], extra)


def mm_fused(name, m, a_ins, bs, mode, kdim, prologue=None, a_outs=(), e_ins=(), consts=(), epilogue=None, outs=(),
             alias_outs=(), accs=(), a_to_epilogue=(), a_consts=(), tm=512, tk=1024, resident=False, exch=None):
    tm = min(tm, m)
    nk = kdim // tk
    assert nk == 1 or not a_to_epilogue
    n = bs[0][0].shape[1 if mode == "nn" else 0]
    b_arrays = []
    for b_, _ in bs:
        if not (resident and any(b_ is u for u in b_arrays)):
            b_arrays.append(b_)
    b_of_pair = [next(j for j, u in enumerate(b_arrays) if u is b_) if resident else j for j, (b_, _) in enumerate(bs)]
    n_a, n_b, n_e, n_c = len(a_ins) + len(a_consts), len(b_arrays), len(e_ins), len(consts)
    n_ao, n_o, n_al, n_ac = len(a_outs), len(outs), len(alias_outs), len(accs)
    held = [a for (a, _, _) in alias_outs if not isinstance(a, jax.ShapeDtypeStruct)]
    dot = _dot if mode == "nn" else _dot_nt

    def body(*refs):
        a_refs, b_refs = refs[:n_a], refs[n_a:n_a + n_b]
        e_refs = refs[n_a + n_b:n_a + n_b + n_e + n_c]
        o0 = n_a + n_b + n_e + n_c + len(held)
        ao_refs = refs[o0:o0 + n_ao]
        out_refs = refs[o0 + n_ao:o0 + n_ao + n_o + n_al]
        acc_refs = refs[o0 + n_ao + n_o + n_al:o0 + n_ao + n_o + n_al + n_ac]
        scr = refs[o0 + n_ao + n_o + n_al + n_ac:]
        i, k = pl.program_id(0), pl.program_id(1)
        ck = tk if prologue is None else min(tk, PROLOGUE_CHUNK)
        part = None
        for c0 in range(0, tk, ck):
            cols = slice(c0, c0 + ck)
            tiles = [r[:, cols] for r in a_refs]
            a_list, extra = (tiles, []) if prologue is None else prologue(*tiles)
            for r, v in zip(ao_refs, extra):
                r[:, cols] = v.astype(r.dtype)
            for a, j_b, (_, off) in zip(a_list, b_of_pair, bs):
                b_ref = b_refs[j_b]
                if resident:
                    b = b_ref[pl.ds(pl.multiple_of((k + off) * tk + c0, ck), ck), :]
                else:
                    b = b_ref[cols, :] if mode == "nn" else b_ref[:, cols]
                prod = dot(a.astype(BF16), b.astype(BF16))
                part = prod if part is None else part + prod

        def finish(total):
            vals = epilogue(total, *[a_refs[j][...] for j in a_to_epilogue], *[r[...] for r in e_refs])
            if not isinstance(vals, (tuple, list)):
                vals = (vals,)
            for r, v in zip(out_refs, vals[:n_o + n_al]):
                r[...] = v.astype(r.dtype)
            for r, v in zip(acc_refs, vals[n_o + n_al:]):
                @pl.when(i == 0)
                def _():
                    r[...] = v

                @pl.when(i > 0)
                def _():
                    r[...] += v

        if nk == 1:
            finish(part)
            return
        acc_ref = scr[0]

        @pl.when(k == 0)
        def _():
            acc_ref[...] = part

        @pl.when(k > 0)
        def _():
            acc_ref[...] += part

        @pl.when(k == nk - 1)
        def _():
            finish(acc_ref[...])

    in_specs = [pl.BlockSpec((tm, tk), lambda i, k, off=off: (i, k + off)) for (_, off) in a_ins]
    in_specs += [pl.BlockSpec((1, tk), lambda i, k: (0, k)) for _ in a_consts]
    if resident:
        assert mode == "nn"
        in_specs += [pl.BlockSpec(b.shape, lambda i, k: (0, 0), pipeline_mode=pl.Buffered(1)) for b in b_arrays]
    elif mode == "nn":
        in_specs += [pl.BlockSpec((tk, n), lambda i, k, off=off: (k + off, 0)) for (_, off) in bs]
    else:
        in_specs += [pl.BlockSpec((n, tk), lambda i, k, off=off: (0, k + off)) for (_, off) in bs]
    in_specs += [pl.BlockSpec((tm, w), lambda i, k, cb=cb: (i, cb)) for (_, w, cb) in e_ins]
    in_specs += [pl.BlockSpec(c.shape, lambda i, k, nd=c.ndim: (0,) * nd) for c in consts]
    in_specs += [ANY for _ in held]
    out_shape = [jax.ShapeDtypeStruct((m, kdim), dt) for dt in a_outs]
    out_specs = [pl.BlockSpec((tm, tk), lambda i, k: (i, k)) for _ in a_outs]
    out_shape += [jax.ShapeDtypeStruct((m, w), dt) for (w, dt) in outs]
    out_specs += [pl.BlockSpec((tm, w), lambda i, k: (i, 0)) for (w, _) in outs]
    out_shape += [jax.ShapeDtypeStruct(a.shape, a.dtype) for (a, _, _) in alias_outs]
    out_specs += [pl.BlockSpec((tm, w), lambda i, k, cb=cb: (i, cb)) for (_, w, cb) in alias_outs]
    out_shape += [jax.ShapeDtypeStruct(s_, F32) for s_ in accs]
    out_specs += [pl.BlockSpec(s_, lambda i, k: (0, 0)) for s_ in accs]
    aliases, k_in = {}, n_a + n_b + n_e + n_c
    for j, (a, _, _) in enumerate(alias_outs):
        if not isinstance(a, jax.ShapeDtypeStruct):
            aliases[k_in] = n_ao + n_o + j
            k_in += 1
    operands = [a for (a, _) in a_ins] + list(a_consts) + b_arrays + [a for (a, _, _) in e_ins] + list(consts) + held
    res, extra = hosted_call(
        body, exch, name, (m // tm, nk), in_specs, out_specs, out_shape,
        [pltpu.VMEM((tm, n), F32)] if nk > 1 else [], operands,
        ("arbitrary" if n_ac else "parallel", "arbitrary"), aliases)
    return res if exch is None else (res, extra)


def _cumsum_rows(x):
    n = x.shape[0]
    row = lax.broadcasted_iota(jnp.int32, x.shape, 0)
    s = 1
    while s < n:
        x = x + jnp.where(row >= s, pltpu.roll(x, s, 0), 0.0)
        s *= 2
    return x


def _hg_prep(zq, zf, lb, reverse, b=None):
    n = zq.shape[0]
    q = _silu(zq)
    sig = _sigmoid(zf)
    sn = 1.0 - sig
    f = lb + (1.0 - lb) * sig
    k = (1.0 - lb) * sn
    if b is None:
        g = jnp.log(jnp.maximum(f, TINY))
        b = _cumsum_rows(g)
        if reverse:
            b = b[n - 1:n] - b + g
    b_last = b[0:1] if reverse else b[n - 1:n]
    b_ref = b[n // 2:n // 2 + 1]
    e1 = jnp.exp(b)
    e2 = jnp.exp(jnp.clip(b - b_ref, -EXP_CLAMP, EXP_CLAMP))
    e3 = jnp.exp(jnp.clip(b_ref - b, -EXP_CLAMP, EXP_CLAMP))
    e4 = jnp.exp(b_last - b)
    return dict(q=q, k=k, sig=sig, sn=sn, f=f, b=b, e1=e1, e2=e2, e3=e3, e4=e4, e_last=jnp.exp(b_last),
                qe=(q * e1).astype(BF16), qt=(q * e2).astype(BF16), kt=(k * e3).astype(BF16),
                ks=(k * e4).astype(BF16))


def _hg_mask(n, reverse):
    t = lax.broadcasted_iota(jnp.int32, (n, n), 0)
    s = lax.broadcasted_iota(jnp.int32, (n, n), 1)
    return (s >= t) if reverse else (s <= t)


def hgrn_fwd(name, z, lb_f, lb_b, exch=None):
    m = z.shape[0]
    C, T = HG_CHUNK, min(HG_BLOCK_FWD, m)
    nb, cpb = m // T, T // C

    def body(zq_f, zf_f, zi_f, zq_b, zf_b, zi_b, lbf_ref, lbb_ref, of_ref, ob_ref, sf_ref, sb_ref, bf_ref, bb_ref,
             st_ref):
        @pl.when(pl.program_id(0) == 0)
        def _():
            st_ref[...] = jnp.zeros(st_ref.shape, F32)

        dirs = ((zq_f, zf_f, zi_f, lbf_ref, of_ref, sf_ref), (zq_b, zf_b, zi_b, lbb_ref, ob_ref, sb_ref))
        b_refs = (bf_ref, bb_ref)

        def chunk(ci, carry):
            work = []
            for d, (zq, zf, zi, lb_ref, o_ref, s_ref) in enumerate(dirs):
                cc = ci if d == 0 else cpb - 1 - ci
                rows = pl.ds(pl.multiple_of(cc * C, C), C)
                pre = _hg_prep(zq[rows, :], zf[rows, :], lb_ref[...], d == 1)
                v = zi[rows, :].astype(BF16)
                work.append((cc, rows, pre, v, [st_ref[d, h] for h in range(HEADS)]))
            heads = [(d, h, slice(h * HEAD_DIM, (h + 1) * HEAD_DIM)) for d in range(2) for h in range(HEADS)]
            first = {}
            for d, h, sl in heads:
                _, _, pre, v, sts = work[d]
                first[d, h] = (_dot_nt(pre["qt"][:, sl], pre["kt"][:, sl]),
                               _dot_nt(pre["qe"][:, sl], sts[h].astype(BF16)),
                               _dot_tn(v[:, sl], pre["ks"][:, sl]))
            results = [([], []), ([], [])]
            for d, h, sl in heads:
                _, _, pre, v, sts = work[d]
                scores, o_inter, st_add = first[d, h]
                a = jnp.where(_hg_mask(C, d == 1), scores, 0.0).astype(BF16)
                results[d][0].append(o_inter + _dot(a, v[:, sl]))
                results[d][1].append(sts[h] * pre["e_last"][:, sl] + st_add)
            results = [(jnp.concatenate(o_parts, axis=1), new_sts) for (o_parts, new_sts) in results]
            for d, (zq, zf, zi, lb_ref, o_ref, s_ref) in enumerate(dirs):
                cc, rows, pre, _, sts = work[d]
                o_ref[rows, :] = results[d][0]
                b_refs[d][rows, :] = pre["b"]
                for h in range(HEADS):
                    s_ref[cc, h] = sts[h]
                    st_ref[d, h] = results[d][1][h]
            return carry

        lax.fori_loop(0, cpb, chunk, 0)

    def zspec(cb, rev):
        return pl.BlockSpec((T, D), (lambda i: (nb - 1 - i, cb)) if rev else (lambda i: (i, cb)))

    def sspec(rev):
        shape = (cpb, HEADS, HEAD_DIM, HEAD_DIM)
        return pl.BlockSpec(shape, (lambda i: (nb - 1 - i, 0, 0, 0)) if rev else (lambda i: (i, 0, 0, 0)))

    lbspec = pl.BlockSpec((1, D), lambda i: (0, 0))
    states = jax.ShapeDtypeStruct((m // C, HEADS, HEAD_DIM, HEAD_DIM), F32)
    outs, extra = hosted_call(
        body, exch, name, (nb,),
        [zspec(ZQ, False), zspec(ZFF, False), zspec(ZI, False), zspec(ZQ, True), zspec(ZFB, True), zspec(ZI, True),
         lbspec, lbspec],
        [zspec(0, False), zspec(0, True), sspec(False), sspec(True), zspec(0, False), zspec(0, True)],
        [jax.ShapeDtypeStruct((m, D), F32), jax.ShapeDtypeStruct((m, D), F32), states, states,
         jax.ShapeDtypeStruct((m, D), F32), jax.ShapeDtypeStruct((m, D), F32)],
        [pltpu.VMEM((2, HEADS, HEAD_DIM, HEAD_DIM), F32)], [z, z, z, z, z, z, lb_f, lb_b], ("arbitrary",))
    return outs, extra


def hgrn_bwd(name, z, d_o, s_f, s_b, b_f, b_b, lb_f, lb_b, exch=None):
    m = z.shape[0]
    C, T = HG_CHUNK, min(HG_BLOCK_BWD, m)
    nb, cpb = m // T, T // C

    def body(zq_f, zf_f, zi_f, do_f, sf_ref, zq_b, zf_b, zi_b, do_b, sb_ref, lbf_ref, lbb_ref, bf_ref, bb_ref,
             dqf_ref, dvf_ref, dqb_ref, dvb_ref, dzf_f, dzf_b, dlbf_ref, dlbb_ref,
             dst_ref):
        b_refs = (bf_ref, bb_ref)
        @pl.when(pl.program_id(0) == 0)
        def _():
            dst_ref[...] = jnp.zeros(dst_ref.shape, F32)
            dlbf_ref[...] = jnp.zeros(dlbf_ref.shape, F32)
            dlbb_ref[...] = jnp.zeros(dlbb_ref.shape, F32)

        dirs = ((zq_f, zf_f, zi_f, do_f, sf_ref, lbf_ref, dqf_ref, dvf_ref, dzf_f, dlbf_ref),
                (zq_b, zf_b, zi_b, do_b, sb_ref, lbb_ref, dqb_ref, dvb_ref, dzf_b, dlbb_ref))

        def chunk(ci, carry):
            work = []
            for d, (zq, zf, zi, do_ref, s_ref, lb_ref, dq_ref, dv_ref, dzf_ref, dlb_ref) in enumerate(dirs):
                cc = cpb - 1 - ci if d == 0 else ci
                rows = pl.ds(pl.multiple_of(cc * C, C), C)
                lb = lb_ref[...]
                pre = _hg_prep(zq[rows, :], zf[rows, :], lb, d == 1, b=b_refs[d][rows, :])
                work.append((rows, lb, pre, zi[rows, :].astype(BF16), do_ref[rows, :],
                             [s_ref[cc, h] for h in range(HEADS)], [dst_ref[d, h] for h in range(HEADS)],
                             dlb_ref[...]))
            heads = [(d, h, slice(h * HEAD_DIM, (h + 1) * HEAD_DIM)) for d in range(2) for h in range(HEADS)]
            first = {}
            for d, h, sl in heads:
                _, _, pre, v, do, st_prevs, dsts, _ = work[d]
                dst16 = dsts[h].astype(BF16)
                first[d, h] = (_dot_nt(pre["qt"][:, sl], pre["kt"][:, sl]),
                               _dot_nt(do[:, sl], v[:, sl]),
                               _dot(do[:, sl], st_prevs[h].astype(BF16)),
                               _dot(v[:, sl], dst16),
                               _dot_nt(pre["ks"][:, sl], dst16),
                               _dot_tn(do[:, sl], pre["qe"][:, sl]))
            parts = [[[] for _ in range(6)] for _ in range(2)]
            for d, h, sl in heads:
                _, _, pre, v, do, st_prevs, dsts, _ = work[d]
                scores, dscores, dq_inter, dk_state, dv_state, dst_add = first[d, h]
                mask = _hg_mask(C, d == 1)
                a = jnp.where(mask, scores, 0.0).astype(BF16)
                da = jnp.where(mask, dscores, 0.0).astype(BF16)
                dq_p, dki_p, dks_p, dv_p, rr_p, new_dsts = parts[d]
                dq_p.append(_dot(da, pre["kt"][:, sl]) * pre["e2"][:, sl] + dq_inter * pre["e1"][:, sl])
                dki_p.append(_dot_tn(da, pre["qt"][:, sl]) * pre["e3"][:, sl])
                dks_p.append(dk_state * pre["e4"][:, sl])
                dv_p.append(_dot_tn(a, do[:, sl]) + dv_state)
                rr_p.append(pre["e_last"][:, sl] * _colsum(dsts[h] * st_prevs[h]))
                new_dsts.append(dsts[h] * pre["e_last"][:, sl] + dst_add)
            results = []
            for d, (rows, lb, pre, v, do, st_prevs, dsts, dlb_old) in enumerate(work):
                rev = d == 1
                dq_p, dki_p, dks_p, dv_p, rr_p, new_dsts = parts[d]
                dq, dki, dks, dv, rr = (jnp.concatenate(p_, axis=1) for p_ in (dq_p, dki_p, dks_p, dv_p, rr_p))
                x = pre["q"] * dq - pre["k"] * dki
                y = pre["k"] * dks
                if rev:
                    dg = _cumsum_rows(x - y) + _colsum(y) + rr
                else:
                    dg = _cumsum_rows(y - x) + (x - y) + _colsum(x) + rr
                inv_f = jnp.where(pre["f"] > TINY, 1.0 / pre["f"], 0.0)
                u = dg * inv_f - (dki + dks)
                results.append((dq, dv, (1.0 - lb) * pre["sig"] * pre["sn"] * u, dlb_old + _colsum(pre["sn"] * u),
                                new_dsts))
            for d, (zq, zf, zi, do_ref, s_ref, lb_ref, dq_ref, dv_ref, dzf_ref, dlb_ref) in enumerate(dirs):
                rows = work[d][0]
                dq, dv, dzf, dlb, new_dsts = results[d]
                dq_ref[rows, :] = dq.astype(dq_ref.dtype)
                dv_ref[rows, :] = dv.astype(dv_ref.dtype)
                dzf_ref[rows, :] = dzf.astype(dzf_ref.dtype)
                dlb_ref[...] = dlb
                for h in range(HEADS):
                    dst_ref[d, h] = new_dsts[h]
            return carry

        lax.fori_loop(0, cpb, chunk, 0)

    def rspec(cb, rev):
        return pl.BlockSpec((T, D), (lambda i: (i, cb)) if rev else (lambda i: (nb - 1 - i, cb)))

    def sspec(rev):
        shape = (cpb, HEADS, HEAD_DIM, HEAD_DIM)
        return pl.BlockSpec(shape, (lambda i: (i, 0, 0, 0)) if rev else (lambda i: (nb - 1 - i, 0, 0, 0)))

    lbspec = pl.BlockSpec((1, D), lambda i: (0, 0))
    half = jax.ShapeDtypeStruct((m, D), BF16)
    row = jax.ShapeDtypeStruct((1, D), F32)
    outs, extra = hosted_call(
        body, exch, name, (nb,),
        [rspec(ZQ, False), rspec(ZFF, False), rspec(ZI, False), rspec(0, False), sspec(False),
         rspec(ZQ, True), rspec(ZFB, True), rspec(ZI, True), rspec(0, True), sspec(True), lbspec, lbspec,
         rspec(0, False), rspec(0, True)],
        [rspec(0, False), rspec(0, False), rspec(0, True), rspec(0, True), rspec(0, False), rspec(0, True),
         lbspec, lbspec],
        [half, half, half, half, half, half, row, row],
        [pltpu.VMEM((2, HEADS, HEAD_DIM, HEAD_DIM), F32)],
        [z, z, z, d_o, s_f, z, z, z, d_o, s_b, lb_f, lb_b, b_f, b_b], ("arbitrary",))
    return outs, extra


def _heads(fn, *arrs):
    res = [fn(*[a[:, h * HEAD_DIM:(h + 1) * HEAD_DIM] for a in arrs]) for h in range(arrs[0].shape[1] // HEAD_DIM)]
    return [jnp.concatenate(parts, axis=1) for parts in zip(*res)]


def _hg_post(o_f, o_b, zg, g):
    def head(of, ob, zgh, gh):
        on, _ = _rms(of + ob)
        return (on * gh * _silu(zgh),)
    return _heads(head, o_f, o_b, zg, g)[0]


def _hg_post_bwd(da, o_f, o_b, zg, g):
    def head(dah, of, ob, zgh, gh):
        on, r = _rms(of + ob)
        sg = _silu(zgh)
        d_on = dah * sg
        return _rms_bwd(d_on, on, r, gh), dah * on * gh * _silu_grad(zgh), d_on * on
    d_o, dzg, dg = _heads(head, da, o_f, o_b, zg, g)
    return d_o, dzg, _colsum(dg)


def _sg_parts(zv, ln_g, ln_b):
    vg = _gelu(zv)
    xc = vg - _mean(vg)
    rstd = lax.rsqrt(_mean(xc * xc) + EPS)
    vh = xc * rstd
    return vh, rstd, vh * ln_g + ln_b


def _sg_lane_group(shape):
    return lax.broadcasted_iota(jnp.int32, shape, 1) < SG_GROUP_DIM


def _sg_mix(w, v16, transpose):
    rows = v16.shape[0]
    out = []
    for c in range(rows // SG_CHUNK):
        parts = []
        for j in range(SG_WIDTH // 128):
            vj = v16[c * SG_CHUNK:(c + 1) * SG_CHUNK, j * 128:(j + 1) * 128]
            w0 = w[(2 * j) * SG_CHUNK:(2 * j + 1) * SG_CHUNK]
            w1 = w[(2 * j + 1) * SG_CHUNK:(2 * j + 2) * SG_CHUNK]
            dot = _dot_tn if transpose else _dot
            parts.append(jnp.where(_sg_lane_group((SG_CHUNK, 128)), dot(w0, vj), dot(w1, vj)))
        out.append(jnp.concatenate(parts, axis=1))
    return jnp.concatenate(out, axis=0)


def _sg_fwd(zu, zv, w, bias, ln_g, ln_b):
    _, _, v = _sg_parts(zv, ln_g, ln_b)
    reps = zu.shape[0] // SG_CHUNK
    return _gelu(zu) * (_sg_mix(w, v.astype(BF16), False) + jnp.concatenate([bias] * reps, axis=0))


def _sg_bwd(db, zu, zv, w, bias, ln_g, ln_b):
    vh, rstd, v = _sg_parts(zv, ln_g, ln_b)
    v16 = v.astype(BF16)
    reps = zu.shape[0] // SG_CHUNK
    sg = _sg_mix(w, v16, False) + jnp.concatenate([bias] * reps, axis=0)
    dzu = db * sg * _gelu_grad(zu)
    dsg = db * _gelu(zu)
    dsg16 = dsg.astype(BF16)
    dv = _sg_mix(w, dsg16, True)
    low = _sg_lane_group((SG_CHUNK, 128))
    dw = []
    for g in range(SG_WIDTH // SG_GROUP_DIM):
        j, keep = g // 2, (low if g % 2 == 0 else jnp.logical_not(low))
        acc = jnp.zeros((SG_CHUNK, SG_CHUNK), F32)
        for c in range(reps):
            rows = slice(c * SG_CHUNK, (c + 1) * SG_CHUNK)
            dj = jnp.where(keep, dsg16[rows, j * 128:(j + 1) * 128], jnp.zeros((), BF16))
            acc = acc + _dot_nt(dj, v16[rows, j * 128:(j + 1) * 128])
        dw.append(acc)
    dbias = sum(dsg[c * SG_CHUNK:(c + 1) * SG_CHUNK] for c in range(reps))
    dvh = dv * ln_g
    dvg = rstd * (dvh - _mean(dvh) - vh * _mean(dvh * vh))
    dzuv = jnp.concatenate([dzu, dvg * _gelu_grad(zv)], axis=1)
    return (dzuv, jnp.concatenate(dw, axis=0), dbias, _colsum(dv * vh), _colsum(dv))


def lower_bounds(name, gamma_f, gamma_b):
    def body(gf_ref, gb_ref, lf_ref, lb_ref):
        for g_ref, o_ref in ((gf_ref, lf_ref), (gb_ref, lb_ref)):
            g0, g1 = g_ref[0:1, :], g_ref[1:2, :]
            mx = jnp.maximum(g0, g1)
            e0, e1 = jnp.exp(g0 - mx), jnp.exp(g1 - mx)
            sm0, sm1 = e0 / (e0 + e1), e1 / (e0 + e1)
            o_ref[0:1, :] = sm0 - sm0
            o_ref[1:2, :] = (sm0 + sm1) - sm0
    shp = jax.ShapeDtypeStruct(gamma_f.shape, F32)
    return pl.pallas_call(body, name=name, out_shape=[shp, shp])(gamma_f, gamma_b)


def lower_bounds_bwd(name, gamma_f, gamma_b, dlb_f, dlb_b):
    def body(gf_ref, gb_ref, df_ref, db_ref, of_ref, ob_ref):
        for g_ref, d_ref, o_ref in ((gf_ref, df_ref, of_ref), (gb_ref, db_ref, ob_ref)):
            g0, g1 = g_ref[0:1, :], g_ref[1:2, :]
            mx = jnp.maximum(g0, g1)
            e0, e1 = jnp.exp(g0 - mx), jnp.exp(g1 - mx)
            sm0, sm1 = e0 / (e0 + e1), e1 / (e0 + e1)
            d1 = d_ref[1:2, :] * sm0 * sm1
            o_ref[0:1, :] = -d1
            o_ref[1:2, :] = d1
    shp = jax.ShapeDtypeStruct(gamma_f.shape, F32)
    return pl.pallas_call(body, name=name, out_shape=[shp, shp])(gamma_f, gamma_b, dlb_f, dlb_b)


def _row(a, l):
    return a[l:l + 1]


class LocalPlan:
    def __init__(self, weights):
        self.W = weights
        self.grads = [dict() for _ in range(DEPTH)]

    def exch(self, host):
        return None

    def done(self, host, outs):
        pass

    def early_small(self, packed):
        pass


def local_step(x, p, target, S, plan):
    m = x.shape[0]

    def hmm(tag, *args, **kw):
        ex = plan.exch(tag)
        res = mm(tag, *args, exch=ex, **kw)
        if ex is None:
            return res
        plan.done(tag, res[1])
        return res[0]

    lb_f, lb_b = lower_bounds("lower_bounds", S["lb_gamma_fwd"], S["lb_gamma_bwd"])
    saved = []
    for l in range(DEPTH):
        t = f"l{l}_"
        W = plan.W[l]
        g_pre, g_post = _row(S["norm_mix_pre"], l), _row(S["norm_mix_post"], l)
        g_fpre, g_fpost = _row(S["norm_ffn_pre"], l), _row(S["norm_ffn_post"], l)
        hg_g = _row(S["hg_norm"], l)
        sg_w = S["sg_w"][l].reshape(SG_WIDTH // SG_GROUP_DIM * SG_CHUNK, SG_CHUNK).astype(BF16)
        sg_bias = jnp.repeat(S["sg_b"][l].T, SG_GROUP_DIM, axis=1)
        ln_g, ln_b = _row(S["sg_ln_g"], l), _row(S["sg_ln_b"], l)
        lbf, lbb = _row(lb_f, l), _row(lb_b, l)

        if l == 0:
            (h,) = rowwise(t + "pre_norm", lambda xv, g: (_rms(xv)[0] * g,), m, ins=[(x, D, 0)], consts=[g_pre],
                           outs=[(D, BF16)])
        z = hmm(t + "in_proj", h, W["w_in"], "nn", tm=BIG_TILE // 2, tn=BIG_TILE)
        (o_f, o_b, s_f, s_b, b_f, b_b), extra = hgrn_fwd(t + "hgrn_fwd", z, lbf, lbb, exch=plan.exch(t + "hgrn_fwd"))
        plan.done(t + "hgrn_fwd", extra)
        (b_out,) = rowwise(t + "sgu_fwd", _sg_fwd, m, ins=[(z, SG_WIDTH, ZU), (z, SG_WIDTH, ZV)],
                           consts=[sg_w, sg_bias, ln_g, ln_b], outs=[(SG_WIDTH, BF16)])

        def post_pro(of, ob, zg, g):
            ao = _hg_post(of, ob, zg, g).astype(BF16)
            return [ao], [ao]
        a_out, pa = mm_fused(t + "proj_a", m, [(o_f, 0), (o_b, 0), (z, ZG)], [(W["w_a"], 0)], "nn", D,
                             prologue=post_pro, a_outs=[BF16], a_consts=[hg_g], epilogue=lambda tot: (tot,),
                             outs=[(D, BF16)])
        pb = mm(t + "proj_b", b_out, W["w_b"], "nn", BF16)

        def merge_pro(a, b, ga, gb):
            mg = (_sigmoid(ga) * a + _sigmoid(gb) * b).astype(BF16)
            return [mg], [mg]

        def post_pre(mixv, xv, gp, gf):
            x1 = xv + _rms(mixv)[0] * gp
            return mixv, x1, _rms(x1)[0] * gf
        merged, mix, x1, h2 = mm_fused(
            t + "out_proj", m, [(pa, 0), (pb, 0), (z, GA), (z, GB)], [(W["w_out"], 0)], "nn", D, prologue=merge_pro,
            a_outs=[BF16], e_ins=[(x, D, 0)], consts=[g_post, g_fpre], epilogue=post_pre,
            outs=[(D, F32), (D, F32), (D, BF16)])
        gu = hmm(t + "ffn_in", h2, W["w_gu"], "nn", BF16, tm=BIG_TILE, tn=BIG_TILE)

        def act_pro(gt, up):
            hd = (_silu(gt.astype(F32)) * up).astype(BF16)
            return [hd], [hd]
        hid, ff, x2 = mm_fused(
            t + "ffn_out", m, [(gu, 0), (gu, FFN_PAD // 1024)], [(W["w_down"], 0)], "nn", FFN_PAD, prologue=act_pro,
            a_outs=[BF16], e_ins=[(x1, D, 0)], consts=[g_fpost],
            epilogue=lambda f, xv, g: (f, xv + _rms(f)[0] * g), outs=[(D, F32), (D, F32)], tm=1024, resident=True)
        e = mm(t + "ple_proj", (p, l), W["w_ple"], "nn")

        if l + 1 < DEPTH:
            def ple_add(tv, xv, ev, g):
                x3 = xv + ev * _sigmoid(tv)
                return tv, x3, _rms(x3)[0] * g
            tg, x3, h_next = mm_fused(
                t + "ple_gate", m, [(x2, 0)], [(W["w_ple_gate"], 0)], "nn", D, a_to_epilogue=(0,), e_ins=[(e, D, 0)],
                consts=[_row(S["norm_mix_pre"], l + 1)], epilogue=ple_add, outs=[(D, F32), (D, F32), (D, BF16)])
        else:
            def ple_loss(tv, xv, ev, tgt):
                err = xv + ev * _sigmoid(tv) - tgt
                return tv, err * (1.0 / D), _colsum(err * err)
            tg, x3, loss_cols = mm_fused(
                t + "ple_gate", m, [(x2, 0)], [(W["w_ple_gate"], 0)], "nn", D, a_to_epilogue=(0,),
                e_ins=[(e, D, 0), (target, D, 0)], epilogue=ple_loss, outs=[(D, F32), (D, F32)], accs=[(1, D)])
            h_next = None
        saved.append(dict(x=x, h=h, z=z, o_f=o_f, o_b=o_b, s_f=s_f, s_b=s_b, b_f=b_f, b_b=b_b, a_out=a_out,
                          b_out=b_out, pa=pa, pb=pb,
                          merged=merged, mix=mix, x1=x1, h2=h2, gu=gu, hid=hid, ff=ff, x2=x2, e=e, tg=tg,
                          sg_w=sg_w, sg_bias=sg_bias))
        x, h = x3, h_next

    dx = x

    gs = {n: [None] * DEPTH for n in SMALL}
    dlb_f, dlb_b = [None] * DEPTH, [None] * DEPTH

    for l in reversed(range(DEPTH)):
        t = f"l{l}_bwd_"
        sv, W = saved[l], plan.W[l]
        g_pre, g_post = _row(S["norm_mix_pre"], l), _row(S["norm_mix_post"], l)
        g_fpre, g_fpost = _row(S["norm_ffn_pre"], l), _row(S["norm_ffn_post"], l)
        hg_g = _row(S["hg_norm"], l)
        ln_g, ln_b = _row(S["sg_ln_g"], l), _row(S["sg_ln_b"], l)
        lbf, lbb = _row(lb_f, l), _row(lb_b, l)

        def wgrad(nm, tag, a, b):
            a_dtype = (a[0] if isinstance(a, tuple) else a).dtype
            plan.grads[l][nm] = mm(tag, a, b, "tn", BF16, tk=WGRAD_TOKENS if a_dtype == BF16 else WGRAD_TOKENS // 2)

        def ple_pro(d3, ev, tv):
            s = _sigmoid(tv)
            de_, dt_ = (d3 * s).astype(BF16), (d3 * ev * s * (1.0 - s)).astype(BF16)
            return [dt_], [dt_, de_]

        def ffn_post_bwd(d2p, d3, f, g):
            d2 = d3 + d2p
            fh, r = _rms(f)
            return d2, _rms_bwd(d2, fh, r, g), _colsum(d2 * fh)
        dt, de, dx2, dff, gs["norm_ffn_post"][l] = mm_fused(
            t + "ple_gate_dx", m, [(dx, 0), (sv["e"], 0), (sv["tg"], 0)], [(W["w_ple_gate"], 0)], "nt", D,
            prologue=ple_pro, a_outs=[BF16, BF16], a_to_epilogue=(0,), e_ins=[(sv["ff"], D, 0)], consts=[g_fpost],
            epilogue=ffn_post_bwd, outs=[(D, F32), (D, BF16)], accs=[(1, D)])
        wgrad("w_ple", t + "w_ple", (p, l), de)
        wgrad("w_ple_gate", t + "w_ple_gate", sv["x2"], dt)
        wgrad("w_down", t + "w_down", sv["hid"], dff)
        dhid = mm(t + "ffn_out_dx", dff, W["w_down"], "nt", BF16, tm=BIG_TILE)

        def act_bwd(dh, gt, up):
            dh, gt = dh.astype(F32), gt.astype(F32)
            s = _sigmoid(gt)
            dg_ = (dh * up * (s * (1.0 + gt * (1.0 - s)))).astype(BF16)
            du_ = (dh * (gt * s)).astype(BF16)
            return [dg_, du_], [dg_, du_]

        def pre_post_bwd(dh, d2, x1v, mixv, gf, gp):
            xh, r1 = _rms(x1v)
            d1 = d2 + _rms_bwd(dh, xh, r1, gf)
            mh, rm = _rms(mixv)
            return d1, _rms_bwd(d1, mh, rm, gp), _colsum(dh * xh), _colsum(d1 * mh)
        off = FFN_PAD // 1024
        w_gu_t = W["w_gu"].T
        dgate, dup, dx1, dmix, gs["norm_ffn_pre"][l], gs["norm_mix_post"][l] = mm_fused(
            t + "ffn_in_dx", m, [(dhid, 0), (sv["gu"], 0), (sv["gu"], off)], [(w_gu_t, 0), (w_gu_t, off)], "nn",
            FFN_PAD, prologue=act_bwd, a_outs=[BF16, BF16], e_ins=[(dx2, D, 0), (sv["x1"], D, 0), (sv["mix"], D, 0)],
            consts=[g_fpre, g_post], epilogue=pre_post_bwd, outs=[(D, F32), (D, BF16)], accs=[(1, D), (1, D)],
            resident=True)
        wgrad("w_gate", t + "w_gate", sv["h2"], dgate)
        wgrad("w_up", t + "w_up", sv["h2"], dup)
        wgrad("w_out", t + "w_out", sv["merged"], dmix)

        def merge_bwd(dm, a, b, gab):
            sa, sb = _sigmoid(gab[:, :D]), _sigmoid(gab[:, D:])
            dgab = jnp.concatenate([dm * a * sa * (1.0 - sa), dm * b * sb * (1.0 - sb)], axis=1)
            return dm * sa, dm * sb, dgab
        dpa, dpb, dz = mm_fused(
            t + "out_proj_dx", m, [(dmix, 0)], [(W["w_out"], 0)], "nt", D,
            e_ins=[(sv["pa"], D, 0), (sv["pb"], D, 0), (sv["z"], 2 * D, 3)], epilogue=merge_bwd,
            outs=[(D, BF16), (D, BF16)], alias_outs=[(jax.ShapeDtypeStruct((m, N_IN), BF16), 2 * D, 3)])
        wgrad("w_a", t + "w_a", sv["a_out"], dpa)
        wgrad("w_b", t + "w_b", sv["b_out"], dpb)
        db = mm(t + "proj_b_dx", dpb, W["w_b"], "nt")

        dz, dsw, dbias, gs["sg_ln_g"][l], gs["sg_ln_b"][l] = rowwise(
            t + "sgu", _sg_bwd, m, ins=[(db, SG_WIDTH, 0), (sv["z"], SG_WIDTH, ZU), (sv["z"], SG_WIDTH, ZV)],
            consts=[sv["sg_w"], sv["sg_bias"], ln_g, ln_b], alias_outs=[(dz, 2 * SG_WIDTH, 5)],
            accs=[(SG_WIDTH // SG_GROUP_DIM * SG_CHUNK, SG_CHUNK), (SG_CHUNK, SG_WIDTH), (1, SG_WIDTH), (1, SG_WIDTH)])
        gs["sg_w"][l] = dsw.reshape(1, SG_WIDTH // SG_GROUP_DIM, SG_CHUNK, SG_CHUNK)
        gs["sg_b"][l] = dbias.reshape(SG_CHUNK, SG_WIDTH // SG_GROUP_DIM, SG_GROUP_DIM).sum(-1).T[None]

        d_o, dz, gs["hg_norm"][l] = mm_fused(
            t + "proj_a_dx", m, [(dpa, 0)], [(W["w_a"], 0)], "nt", D,
            e_ins=[(sv["o_f"], D, 0), (sv["o_b"], D, 0), (sv["z"], D, ZG)], consts=[hg_g], epilogue=_hg_post_bwd,
            outs=[(D, BF16)], alias_outs=[(dz, D, ZG)], accs=[(1, D)], tm=256)
        if l == 0:
            part = {n: (g if not isinstance(g, list) else jnp.concatenate(
                [jnp.zeros((1,) + g[1].shape[1:], F32) if gl is None else gl for gl in g], axis=0))
                for n, g in gs.items()}
            plan.early_small(_pack([part[n].reshape(S[n].shape) for n in SMALL]))
        (dq_f, dv_f, dq_b, dv_b, dzf_f, dzf_b, dlb_f[l], dlb_b[l]), extra = hgrn_bwd(
            t + "hgrn", sv["z"], d_o, sv["s_f"], sv["s_b"], sv["b_f"], sv["b_b"], lbf, lbb,
            exch=plan.exch(t + "hgrn"))
        plan.done(t + "hgrn", extra)

        def combine(dqf, dqb, dvf, dvb, dff_, dfb_, zq):
            dq = dqf.astype(F32) + dqb.astype(F32)
            dv = dvf.astype(F32) + dvb.astype(F32)
            return (jnp.concatenate([(dq * _silu_grad(zq)).astype(BF16), dff_, dfb_, dv.astype(BF16)], axis=1),)
        (dz,) = rowwise(t + "hgrn_combine", combine, m,
                        ins=[(dq_f, D, 0), (dq_b, D, 0), (dv_f, D, 0), (dv_b, D, 0), (dzf_f, D, 0), (dzf_b, D, 0),
                             (sv["z"], D, ZQ)], alias_outs=[(dz, 4 * D, 0)])
        wgrad("w_in", t + "w_in", sv["h"], dz)

        def pre_bwd(dhv, d1, xv, g):
            xh, r = _rms(xv)
            return d1 + _rms_bwd(dhv, xh, r, g), _colsum(dhv * xh)
        ex = plan.exch(t + "in_proj_dx")
        res = mm_fused(t + "in_proj_dx", m, [(dz, 0)], [(W["w_in"], 0)], "nt", N_IN,
                       e_ins=[(dx1, D, 0), (sv["x"], D, 0)], consts=[g_pre], epilogue=pre_bwd, outs=[(D, F32)],
                       accs=[(1, D)], tm=1024, exch=ex)
        if ex is not None:
            res, extra = res
            plan.done(t + "in_proj_dx", extra)
        dx, gs["norm_mix_pre"][l] = res
        saved[l] = None
        if l == DEPTH - 1:
            none = jnp.zeros((1, D), F32)
            gs["lb_gamma_fwd"], gs["lb_gamma_bwd"] = lower_bounds_bwd(
                "lower_bounds_bwd", S["lb_gamma_fwd"], S["lb_gamma_bwd"], jnp.concatenate([none, dlb_f[l]], axis=0),
                jnp.concatenate([none, dlb_b[l]], axis=0))

    small ={n: (g if not isinstance(g, list) else jnp.concatenate(g, axis=0)).reshape(S[n].shape)
             for n, g in gs.items()}
    return loss_cols, dx, small


def cast_pad(name, w, rows_p, cols_p):
    _, r, c = w.shape

    def body(w_ref, o_ref):
        if (rows_p, cols_p) != (r, c):
            o_ref[...] = jnp.zeros(o_ref.shape, BF16)
        o_ref[0:r, 0:c] = w_ref[...].astype(BF16)

    return pl.pallas_call(
        body, name=name, grid=(DEPTH,), in_specs=[pl.BlockSpec((None, r, c), lambda l: (l, 0, 0))],
        out_specs=pl.BlockSpec((None, rows_p, cols_p), lambda l: (l, 0, 0)),
        out_shape=jax.ShapeDtypeStruct((DEPTH, rows_p, cols_p), BF16), compiler_params=_params(("parallel",)),
    )(w)


def _shard_shape(n, shape):
    axis, size, _, _ = LAYOUT[n]
    _, r, c = shape
    return (size, c) if axis == 0 else (r, size)


class DistPlan:
    def __init__(self, shards):
        self.shards = shards
        self.W = [dict() for _ in range(DEPTH)]
        self.grads = [dict() for _ in range(DEPTH)]
        self.slots = [dict() for _ in range(DEPTH)]
        rest = [n for n in BIG if n != "w_in"]
        ffn = ["w_gate", "w_up", "w_down"]
        self.schedule = {
            "l0_in_proj": ("gather", [(0, n) for n in rest]),
            "l0_hgrn_fwd": ("gather", [(1, n) for n in BIG if n not in ffn]),
            "l0_ffn_in": ("gather", [(1, n) for n in ffn]),
            "l1_bwd_hgrn": ("scatter", [(1, n) for n in rest]),
            "l1_bwd_in_proj_dx": ("scatter", [(1, "w_in")]),
            "l0_bwd_hgrn": ("scatter", [(0, n) for n in rest]),
            "l0_bwd_in_proj_dx": ("scatter", [(0, "w_in")]),
        }
        self.pending = {}
        self.small_part = self.small_slots = None
        axis, size, dst, _ = LAYOUT["w_in"]
        self.W[0][dst] = gather_two_level("gather_l0_w_in", shards["w_in"], 0, axis, size, GATHERED[dst])

    def _gather(self, host, parts):
        srcs, dsts, items, keys = [], [], [], []
        for layer, n in parts:
            axis, size, dst, base = LAYOUT[n]
            if (layer, dst) not in keys:
                keys.append((layer, dst))
                dsts.append((GATHERED[dst], BF16))
            srcs.append(self.shards[n])
            items.append(("gather", len(srcs) - 1, keys.index((layer, dst)), axis, size, base, layer))
        self.pending[host] = ("gather", keys)
        return Exchange(srcs, dsts, items)

    def _scatter(self, host, parts):
        srcs, dsts, items = [], [], []
        for layer, n in parts:
            axis, size, _, _ = LAYOUT[n]
            srcs.append(self.grads[layer][n])
            dsts.append(((NDEV,) + _shard_shape(n, self.shards[n].shape), BF16))
            items.append(("scatter", len(srcs) - 1, len(dsts) - 1, axis, size, 0, None))
        keys = list(parts)
        if host == "l0_bwd_hgrn" and self.small_part is not None:
            srcs.append(self.small_part)
            dsts.append(((NDEV,) + self.small_part.shape, F32))
            items.append(("copies", len(srcs) - 1, len(dsts) - 1, 0, 0, 0, None))
            keys.append(("small", None))
        self.pending[host] = ("scatter", keys)
        return Exchange(srcs, dsts, items)

    def early_small(self, packed):
        self.small_part = packed

    def exch(self, host):
        if host not in self.schedule:
            return None
        kind, parts = self.schedule[host]
        return self._gather(host, parts) if kind == "gather" else self._scatter(host, parts)

    def done(self, host, outs):
        if host not in self.pending:
            return
        kind, keys = self.pending.pop(host)
        for (layer, n), arr in zip(keys, outs):
            if layer == "small":
                self.small_slots = arr
            else:
                (self.W if kind == "gather" else self.slots)[layer][n] = arr


def adam(name, w, m_, v_, tr, g=None, slots=None):
    L, r, c = w.shape
    assert r % tr == 0
    nt = r // tr
    n_s = 0 if slots is None else L

    def body(*refs):
        s_refs = refs[:n_s]
        g_ref = refs[n_s] if g is not None else None
        w_ref, m_ref, v_ref, g_out, d_out, m_out, v_out = refs[n_s + (g is not None):]

        def update(gv):
            if g_ref is not None:
                gv = gv + g_ref[...] if gv is not None else g_ref[...]
            m2 = B1 * m_ref[...] + (1.0 - B1) * gv
            v2 = B2 * v_ref[...] + (1.0 - B2) * (gv * gv)
            m_hat = m2 / (1.0 - B1 ** STEP)
            v_hat = v2 / (1.0 - B2 ** STEP)
            g_out[...] = gv
            d_out[...] = -LR * (m_hat / (jnp.sqrt(v_hat) + AEPS) + WD * w_ref[...])
            m_out[...] = m2
            v_out[...] = v2

        if slots is None:
            update(None)
            return
        for layer, s_ref in enumerate(s_refs):
            @pl.when(pl.program_id(0) == layer)
            def _():
                gv = s_ref[0][:, :c].astype(F32)
                for j in range(1, NDEV):
                    gv = gv + s_ref[j][:, :c].astype(F32)
                update(gv)

    spec = pl.BlockSpec((None, tr, c), lambda l, i: (l, i, 0))
    arrs, specs = [], []
    if slots is not None:
        assert len(slots) == L and L <= 2
        arrs = list(slots)
        cp = slots[0].shape[2]
        specs = [pl.BlockSpec((NDEV, tr, cp), lambda l, i: (0, i * (1 - l) + (nt - 1) * l, 0)),
                 pl.BlockSpec((NDEV, tr, cp), lambda l, i: (0, i * l, 0))][:L]
    if g is not None:
        arrs.append(g)
        specs.append(spec)
    shp = jax.ShapeDtypeStruct(w.shape, F32)
    return pl.pallas_call(
        body, name=name, grid=(L, nt), in_specs=specs + [spec, spec, spec], out_specs=[spec] * 4,
        out_shape=[shp] * 4, compiler_params=_params(("arbitrary", "arbitrary")),
    )(*arrs, w, m_, v_)


def _pack(arrs):
    parts = []
    for a in arrs:
        a2 = a.reshape(-1, D)
        parts.append(jnp.pad(a2, ((0, -a2.shape[0] % 8), (0, 0))))
    return jnp.concatenate(parts, axis=0)


def _unpack(buf, shapes):
    out, off = [], 0
    for s in shapes:
        rows = 1
        for d_ in s:
            rows *= d_
        rows //= D
        out.append(buf[off:off + rows].reshape(s))
        off += rows + (-rows % 8)
    return out


def kernel(x, p, norm_mix_pre, w_in, lb_gamma_fwd, lb_gamma_bwd, hg_norm, sg_w, sg_b, sg_ln_g, sg_ln_b, w_a, w_b, w_out, norm_mix_post, norm_ffn_pre, w_gate, w_up, w_down, norm_ffn_post, w_ple, w_ple_gate, loss_target, m_norm_mix_pre, m_w_in, m_lb_gamma_fwd, m_lb_gamma_bwd, m_hg_norm, m_sg_w, m_sg_b, m_sg_ln_g, m_sg_ln_b, m_w_a, m_w_b, m_w_out, m_norm_mix_post, m_norm_ffn_pre, m_w_gate, m_w_up, m_w_down, m_norm_ffn_post, m_w_ple, m_w_ple_gate, v_norm_mix_pre, v_w_in, v_lb_gamma_fwd, v_lb_gamma_bwd, v_hg_norm, v_sg_w, v_sg_b, v_sg_ln_g, v_sg_ln_b, v_w_a, v_w_b, v_w_out, v_norm_mix_post, v_norm_ffn_pre, v_w_gate, v_w_up, v_w_down, v_norm_ffn_post, v_w_ple, v_w_ple_gate):
    a = dict(zip(INPUTS, (x, p, norm_mix_pre, w_in, lb_gamma_fwd, lb_gamma_bwd, hg_norm, sg_w, sg_b, sg_ln_g, sg_ln_b, w_a, w_b, w_out, norm_mix_post, norm_ffn_pre, w_gate, w_up, w_down, norm_ffn_post, w_ple, w_ple_gate, loss_target, m_norm_mix_pre, m_w_in, m_lb_gamma_fwd, m_lb_gamma_bwd, m_hg_norm, m_sg_w, m_sg_b, m_sg_ln_g, m_sg_ln_b, m_w_a, m_w_b, m_w_out, m_norm_mix_post, m_norm_ffn_pre, m_w_gate, m_w_up, m_w_down, m_norm_ffn_post, m_w_ple, m_w_ple_gate, v_norm_mix_pre, v_w_in, v_lb_gamma_fwd, v_lb_gamma_bwd, v_hg_norm, v_sg_w, v_sg_b, v_sg_ln_g, v_sg_ln_b, v_w_a, v_w_b, v_w_out, v_norm_mix_post, v_norm_ffn_pre, v_w_gate, v_w_up, v_w_down, v_norm_ffn_post, v_w_ple, v_w_ple_gate)))
    m = x.shape[1]

    shards = {n: cast_pad("cast_" + n, a[n], *_shard_shape(n, a[n].shape)) for n in BIG}
    plan = DistPlan(shards)
    loss_cols, dx, gs = local_step(x[0], p[:, 0], loss_target[0], {n: a[n] for n in SMALL}, plan)
    loss = lax.psum(jnp.sum(loss_cols) * (0.5 / D), ("x", "y", "c"))

    small_shapes = [a[n].shape for n in SMALL]
    rows = plan.small_slots.shape[1]
    late = allreduce_small("allreduce_small", jnp.pad(gs["norm_mix_pre"][0:1], ((0, 7), (0, 0))))
    g_late = jnp.pad(late, ((0, rows - 8), (0, 0)))[None]

    res = {}
    row_tiles = {"w_in": 128, "w_a": 128, "w_b": 512, "w_out": 128, "w_gate": 128, "w_up": 128, "w_down": 88,
                 "w_ple": 256, "w_ple_gate": 128}
    for n in BIG:
        res[n] = adam("adam_" + n, a[n], a["m_" + n], a["v_" + n], row_tiles[n],
                      slots=[plan.slots[l][n] for l in range(DEPTH)])
    packed = [_pack([a[pre + n] for n in SMALL])[None] for pre in ("", "m_", "v_")]
    small_res = adam("adam_small", packed[0], packed[1], packed[2], rows // 2, g=g_late, slots=[plan.small_slots])
    small_res = [_unpack(r_[0], small_shapes) for r_ in small_res]
    for i, n in enumerate(SMALL):
        res[n] = tuple(small_res[k][i] for k in range(4))

    outs = [loss, dx.reshape(1, m, D)]
    for k in range(4):
        outs += [res[n][k] for n in WEIGHTS]
    return tuple(outs)
```

```python
import jax
import jax.numpy as jnp
from jax import lax
from jax.experimental import pallas as pl
from jax.experimental.pallas import tpu as pltpu

F32 = jnp.float32
BF16 = jnp.bfloat16

D = 1024
N_IN = 8192
HEADS = 8
HEAD_DIM = 128
SG_CHUNK = 128
SG_WIDTH = 512
SG_GROUP_DIM = 64
FFN = 2816
PLE_DIM = 256
EPS = 1e-6
DEPTH = 2
ZQ, ZFF, ZFB, ZI, ZG, GA, GB = 0, 1, 2, 3, 4, 6, 7
ZU, ZV = 10, 11

NDEV = 8
FFN_SHARD_PAD = 384
FFN_PAD = NDEV * FFN_SHARD_PAD

LR, B1, B2, AEPS, WD, STEP = 0.001, 0.9, 0.999, 1e-08, 0.01, 10

ROW_TILE = 256
BIG_TILE = 2048
WGRAD_TOKENS = 4096
HG_CHUNK = 64
HG_BLOCK_FWD = 256
HG_BLOCK_BWD = 256
EXP_CLAMP = 80.0
PROLOGUE_CHUNK = 256
TINY = float(jnp.finfo(jnp.float32).tiny)
VMEM_LIMIT = 56 * 1024 * 1024

BIG = ["w_in", "w_a", "w_b", "w_out", "w_gate", "w_up", "w_down", "w_ple", "w_ple_gate"]
SMALL = ["norm_mix_pre", "lb_gamma_fwd", "lb_gamma_bwd", "hg_norm", "sg_w", "sg_b", "sg_ln_g", "sg_ln_b",
         "norm_mix_post", "norm_ffn_pre", "norm_ffn_post"]
WEIGHTS = ["norm_mix_pre", "w_in", "lb_gamma_fwd", "lb_gamma_bwd", "hg_norm", "sg_w", "sg_b", "sg_ln_g", "sg_ln_b",
           "w_a", "w_b", "w_out", "norm_mix_post", "norm_ffn_pre", "w_gate", "w_up", "w_down", "norm_ffn_post",
           "w_ple", "w_ple_gate"]
INPUTS = (["x", "p"] + WEIGHTS + ["loss_target"] + ["m_" + n for n in WEIGHTS] + ["v_" + n for n in WEIGHTS])
LAYOUT = {
    "w_in": (1, 1024, "w_in", 0), "w_a": (0, 128, "w_a", 0), "w_b": (1, 128, "w_b", 0),
    "w_out": (0, 128, "w_out", 0), "w_gate": (1, FFN_SHARD_PAD, "w_gu", 0),
    "w_up": (1, FFN_SHARD_PAD, "w_gu", FFN_PAD), "w_down": (0, FFN_SHARD_PAD, "w_down", 0),
    "w_ple": (1, 128, "w_ple", 0), "w_ple_gate": (0, 128, "w_ple_gate", 0),
}
GATHERED = {"w_in": (D, N_IN), "w_a": (D, D), "w_b": (SG_WIDTH, D), "w_out": (D, D), "w_gu": (D, 2 * FFN_PAD),
            "w_down": (FFN_PAD, D), "w_ple": (PLE_DIM, D), "w_ple_gate": (D, D)}


def _params(sem):
    return pltpu.CompilerParams(dimension_semantics=sem, vmem_limit_bytes=VMEM_LIMIT)


def _dot(a, b):
    return lax.dot_general(a, b, (((1,), (0,)), ((), ())), preferred_element_type=F32)


def _dot_nt(a, b):
    return lax.dot_general(a, b, (((1,), (1,)), ((), ())), preferred_element_type=F32)


def _dot_tn(a, b):
    return lax.dot_general(a, b, (((0,), (0,)), ((), ())), preferred_element_type=F32)


def _sigmoid(x):
    return jax.nn.sigmoid(x)


def _silu(x):
    return x * _sigmoid(x)


def _silu_grad(x):
    s = _sigmoid(x)
    return s * (1.0 + x * (1.0 - s))


def _gelu(x):
    return 0.5 * x * (1.0 + lax.erf(x * 0.7071067811865476))


def _gelu_grad(x):
    return 0.5 * (1.0 + lax.erf(x * 0.7071067811865476)) + x * jnp.exp(-0.5 * x * x) * 0.3989422804014327


def _mean(x):
    return jnp.mean(x, axis=-1, keepdims=True)


def _colsum(x):
    return jnp.sum(x, axis=0, keepdims=True)


def _rms(x):
    r = lax.rsqrt(_mean(x * x) + EPS)
    return x * r, r


def _rms_bwd(dy, xh, r, g):
    dyg = dy * g
    return r * (dyg - xh * _mean(dyg * xh))


MESH = pl.DeviceIdType.MESH
ANY = pl.BlockSpec(memory_space=pl.ANY)


def _slab(ref, axis, start, size):
    idx = [slice(None)] * 2
    idx[axis] = pl.ds(start, size)
    return ref.at[tuple(idx)]


class Exchange:
    def __init__(self, srcs, dsts, items):
        self.srcs, self.dsts, self.items = list(srcs), list(dsts), list(items)

    def specs(self):
        n = len(self.items)
        sems = [pltpu.SemaphoreType.DMA((n * (NDEV - 1),)), pltpu.SemaphoreType.DMA((n * (NDEV - 1),)),
                pltpu.SemaphoreType.DMA((n,))]
        return ([ANY] * len(self.srcs), [ANY] * len(self.dsts),
                [jax.ShapeDtypeStruct(s, dt) for (s, dt) in self.dsts], sems)

    def copies(self, src, dst, send_sem, recv_sem, loc_sem):
        x, y, c = lax.axis_index("x"), lax.axis_index("y"), lax.axis_index("c")
        me = 4 * x + 2 * y + c
        starts, waits = [], []
        for n, (kind, si, di, axis, size, base, layer) in enumerate(self.items):
            def views(to_dev, from_dev):
                if kind == "gather":
                    return (src[si].at[layer],
                            _slab(dst[di], axis, base + pl.multiple_of(from_dev * size, 128), size))
                if kind == "copies":
                    return src[si], dst[di].at[from_dev]
                return _slab(src[si], axis, base + pl.multiple_of(to_dev * size, 128), size), dst[di].at[from_dev]

            s_own, d_own = views(me, me)
            own = pltpu.make_async_copy(s_own, d_own, loc_sem.at[n])
            starts.append(own)
            waits.append(own)
            for k in range(1, NDEV):
                px = 1 - x if k & 4 else x
                py = 1 - y if k & 2 else y
                pc = 1 - c if k & 1 else c
                peer = 4 * px + 2 * py + pc
                s_out, _ = views(peer, me)
                _, d_in = views(me, peer)
                sem = n * (NDEV - 1) + k - 1
                starts.append(pltpu.make_async_remote_copy(s_out, d_own, send_sem.at[sem], recv_sem.at[sem],
                                                           device_id=(px, py, pc), device_id_type=MESH))
                waits.append(pltpu.make_async_remote_copy(s_out, d_in, send_sem.at[sem], recv_sem.at[sem],
                                                          device_id=(px, py, pc), device_id_type=MESH))
        return starts, waits


def gather_two_level(name, shards, layer, axis, size, full_shape):
    def body(src, dst, send_sem, recv_sem, loc_sem):
        x, y, c = lax.axis_index("x"), lax.axis_index("y"), lax.axis_index("c")
        mine = src.at[layer]
        chips = [(1 - x, y), (x, 1 - y), (1 - x, 1 - y)]

        def slab(px, py, pc):
            return _slab(dst, axis, pl.multiple_of((4 * px + 2 * py + pc) * size, 128), size)

        def copy(k, from_ref, block, to):
            return pltpu.make_async_remote_copy(from_ref, slab(*block), send_sem.at[k], recv_sem.at[k], device_id=to,
                                                device_id_type=MESH)

        own = pltpu.make_async_copy(mine, slab(x, y, c), loc_sem)
        own.start()
        first = [copy(0, mine, (x, y, c), (x, y, 1 - c))]
        first += [copy(1 + j, mine, (x, y, c), (*chip, c)) for j, chip in enumerate(chips)]
        for cp in first:
            cp.start()
        passed = []
        for j, chip in enumerate(chips):
            copy(1 + j, mine, (*chip, c), (x, y, c)).wait_recv()
            fwd = copy(4 + j, slab(*chip, c), (*chip, c), (x, y, 1 - c))
            fwd.start()
            passed.append(fwd)
        copy(0, mine, (x, y, 1 - c), (x, y, c)).wait_recv()
        for j, chip in enumerate(chips):
            copy(4 + j, mine, (*chip, 1 - c), (x, y, c)).wait_recv()
        for cp in first + passed:
            cp.wait_send()
        own.wait()

    return pl.pallas_call(
        body, name=name, in_specs=[ANY], out_specs=ANY, out_shape=jax.ShapeDtypeStruct(full_shape, shards.dtype),
        scratch_shapes=[pltpu.SemaphoreType.DMA((NDEV - 1,)), pltpu.SemaphoreType.DMA((NDEV - 1,)),
                        pltpu.SemaphoreType.DMA(())],
        compiler_params=pltpu.CompilerParams(has_side_effects=True))(shards)


def hosted_call(body, exch, name, grid, in_specs, out_specs, out_shape, scratch_shapes, operands, semantics,
                aliases=None):
    aliases = aliases or {}
    if exch is None:
        res = pl.pallas_call(body, name=name, grid=grid, in_specs=in_specs, out_specs=out_specs, out_shape=out_shape,
                             scratch_shapes=scratch_shapes, input_output_aliases=aliases,
                             compiler_params=_params(semantics))(*operands)
        return list(res), []
    n_in, n_out, n_scr = len(in_specs), len(out_specs), len(scratch_shapes)
    e_in, e_out, e_shape, e_scr = exch.specs()
    ns, nd = len(e_in), len(e_out)

    def at_step(last):
        cond = None
        for ax, n in enumerate(grid):
            c = pl.program_id(ax) == (n - 1 if last else 0)
            cond = c if cond is None else jnp.logical_and(cond, c)
        return cond

    def wrapped(*refs):
        ins, src = refs[:n_in], refs[n_in:n_in + ns]
        o0 = n_in + ns
        outs, dst = refs[o0:o0 + n_out], refs[o0 + n_out:o0 + n_out + nd]
        s0 = o0 + n_out + nd
        scr, sems = refs[s0:s0 + n_scr], refs[s0 + n_scr:]

        @pl.when(at_step(False))
        def _():
            for cp in exch.copies(src, dst, *sems)[0]:
                cp.start()

        body(*ins, *outs, *scr)

        @pl.when(at_step(True))
        def _():
            for cp in exch.copies(src, dst, *sems)[1]:
                cp.wait()

    res = pl.pallas_call(
        wrapped, name=name, grid=grid, in_specs=list(in_specs) + e_in, out_specs=list(out_specs) + e_out,
        out_shape=list(out_shape) + e_shape, scratch_shapes=list(scratch_shapes) + e_scr,
        input_output_aliases=aliases,
        compiler_params=pltpu.CompilerParams(dimension_semantics=("arbitrary",) * len(grid),
                                             vmem_limit_bytes=VMEM_LIMIT, has_side_effects=True),
    )(*operands, *exch.srcs)
    return list(res[:n_out]), list(res[n_out:])


def allreduce_small(name, part):
    rows, width = part.shape

    def body(p_ref, o_ref, buf, send_sem, recv_sem):
        x, y, c = lax.axis_index("x"), lax.axis_index("y"), lax.axis_index("c")
        me = 4 * x + 2 * y + c
        buf[me] = p_ref[...]
        waits = []
        for k in range(1, NDEV):
            px = 1 - x if k & 4 else x
            py = 1 - y if k & 2 else y
            pc = 1 - c if k & 1 else c
            peer = 4 * px + 2 * py + pc
            pltpu.make_async_remote_copy(p_ref, buf.at[me], send_sem.at[k - 1], recv_sem.at[k - 1],
                                         device_id=(px, py, pc), device_id_type=MESH).start()
            waits.append(pltpu.make_async_remote_copy(p_ref, buf.at[peer], send_sem.at[k - 1], recv_sem.at[k - 1],
                                                      device_id=(px, py, pc), device_id_type=MESH))
        for w in waits:
            w.wait()
        acc = buf[0]
        for j in range(1, NDEV):
            acc = acc + buf[j]
        o_ref[...] = acc

    vmem = pl.BlockSpec(memory_space=pltpu.VMEM)
    return pl.pallas_call(
        body, name=name, in_specs=[vmem], out_specs=vmem, out_shape=jax.ShapeDtypeStruct((rows, width), F32),
        scratch_shapes=[pltpu.VMEM((NDEV, rows, width), F32), pltpu.SemaphoreType.DMA((NDEV - 1,)),
                        pltpu.SemaphoreType.DMA((NDEV - 1,))],
        compiler_params=pltpu.CompilerParams(vmem_limit_bytes=VMEM_LIMIT, has_side_effects=True),
    )(part)


def rowwise(name, fn, m, ins=(), consts=(), outs=(), alias_outs=(), accs=(), tm=ROW_TILE):
    tm = min(tm, m)
    n_in, n_c, n_o, n_al, n_ac = len(ins), len(consts), len(outs), len(alias_outs), len(accs)
    held = [a for (a, _, _) in alias_outs if not isinstance(a, jax.ShapeDtypeStruct)]
    n_held = len(held)

    def body(*refs):
        in_refs = refs[:n_in + n_c]
        out_refs = refs[n_in + n_c + n_held:]
        vals = fn(*[r[...] for r in in_refs])
        if not isinstance(vals, (tuple, list)):
            vals = (vals,)
        for r, v in zip(out_refs[:n_o + n_al], vals[:n_o + n_al]):
            r[...] = v.astype(r.dtype)
        if n_ac:
            acc_refs = out_refs[n_o + n_al:]

            @pl.when(pl.program_id(0) == 0)
            def _():
                for r in acc_refs:
                    r[...] = jnp.zeros(r.shape, F32)

            for r, v in zip(acc_refs, vals[n_o + n_al:]):
                r[...] += v

    def col(cb):
        return lambda i: (i, cb)

    in_specs = [pl.BlockSpec((tm, w), col(cb)) for (_, w, cb) in ins]
    in_specs += [pl.BlockSpec(c.shape, lambda i, nd=c.ndim: (0,) * nd) for c in consts]
    in_specs += [ANY for _ in held]
    out_shape = [jax.ShapeDtypeStruct((m, w), dt) for (w, dt) in outs]
    out_specs = [pl.BlockSpec((tm, w), col(0)) for (w, _) in outs]
    out_shape += [jax.ShapeDtypeStruct(a.shape, a.dtype) for (a, _, _) in alias_outs]
    out_specs += [pl.BlockSpec((tm, w), col(cb)) for (_, w, cb) in alias_outs]
    out_shape += [jax.ShapeDtypeStruct(s, F32) for s in accs]
    out_specs += [pl.BlockSpec(s, lambda i: (0, 0)) for s in accs]
    aliases, k_in = {}, n_in + n_c
    for k, (a, _, _) in enumerate(alias_outs):
        if not isinstance(a, jax.ShapeDtypeStruct):
            aliases[k_in] = n_o + k
            k_in += 1
    return pl.pallas_call(
        body, name=name, grid=(m // tm,), in_specs=in_specs, out_specs=out_specs, out_shape=out_shape,
        input_output_aliases=aliases,
        compiler_params=_params(("arbitrary",) if n_ac else ("parallel",)),
    )(*[a for (a, _, _) in ins], *consts, *held)


def _operand(arr, bshape, imap):
    if isinstance(arr, tuple):
        arr, lead = arr
        return arr, pl.BlockSpec((None,) + bshape, lambda *g: (lead,) + imap(*g))
    return arr, pl.BlockSpec(bshape, imap)


def _shape2(arr):
    return arr[0].shape[1:] if isinstance(arr, tuple) else arr.shape


def mm(name, a, b, mode, out_dtype=F32, tm=1024, tn=1024, tk=1024, exch=None):
    sa, sb = _shape2(a), _shape2(b)
    if mode == "nn":
        (M, K), N = sa, sb[1]
    elif mode == "nt":
        (M, K), N = sa, sb[0]
    else:
        (K, M), N = sa, sb[1]
    tm, tn, tk = min(tm, M), min(tn, N), min(tk, K)
    assert M % tm == 0 and N % tn == 0 and K % tk == 0, (name, M, N, K)
    nk = K // tk
    if mode == "nn":
        a_arr, a_spec = _operand(a, (tm, tk), lambda i, j, k: (i, k))
        b_arr, b_spec = _operand(b, (tk, tn), lambda i, j, k: (k, j))
        dot = _dot
    elif mode == "nt":
        a_arr, a_spec = _operand(a, (tm, tk), lambda i, j, k: (i, k))
        b_arr, b_spec = _operand(b, (tn, tk), lambda i, j, k: (j, k))
        dot = _dot_nt
    else:
        a_arr, a_spec = _operand(a, (tk, tm), lambda i, j, k: (k, i))
        b_arr, b_spec = _operand(b, (tk, tn), lambda i, j, k: (k, j))
        dot = _dot_tn

    def body(a_ref, b_ref, o_ref, *acc):
        part = dot(a_ref[...].astype(BF16), b_ref[...].astype(BF16))
        if nk == 1:
            o_ref[...] = part.astype(o_ref.dtype)
            return
        acc_ref, k = acc[0], pl.program_id(2)

        @pl.when(k == 0)
        def _():
            acc_ref[...] = part

        @pl.when(k > 0)
        def _():
            acc_ref[...] += part

        @pl.when(k == nk - 1)
        def _():
            o_ref[...] = acc_ref[...].astype(o_ref.dtype)

    outs, extra = hosted_call(
        body, exch, name, (M // tm, N // tn, nk), [a_spec, b_spec], [pl.BlockSpec((tm, tn), lambda i, j, k: (i, j))],
        [jax.ShapeDtypeStruct((M, N), out_dtype)], [pltpu.VMEM((tm, tn), F32)] if nk > 1 else [], [a_arr, b_arr],
        ("parallel", "parallel", "arbitrary"))
    return outs[0] if exch is None else (outs[0], extra)


def mm_fused(name, m, a_ins, bs, mode, kdim, prologue=None, a_outs=(), e_ins=(), consts=(), epilogue=None, outs=(),
             alias_outs=(), accs=(), a_to_epilogue=(), a_consts=(), tm=512, tk=1024, resident=False, exch=None):
    tm = min(tm, m)
    nk = kdim // tk
    assert nk == 1 or not a_to_epilogue
    n = bs[0][0].shape[1 if mode == "nn" else 0]
    b_arrays = []
    for b_, _ in bs:
        if not (resident and any(b_ is u for u in b_arrays)):
            b_arrays.append(b_)
    b_of_pair = [next(j for j, u in enumerate(b_arrays) if u is b_) if resident else j for j, (b_, _) in enumerate(bs)]
    n_a, n_b, n_e, n_c = len(a_ins) + len(a_consts), len(b_arrays), len(e_ins), len(consts)
    n_ao, n_o, n_al, n_ac = len(a_outs), len(outs), len(alias_outs), len(accs)
    held = [a for (a, _, _) in alias_outs if not isinstance(a, jax.ShapeDtypeStruct)]
    dot = _dot if mode == "nn" else _dot_nt

    def body(*refs):
        a_refs, b_refs = refs[:n_a], refs[n_a:n_a + n_b]
        e_refs = refs[n_a + n_b:n_a + n_b + n_e + n_c]
        o0 = n_a + n_b + n_e + n_c + len(held)
        ao_refs = refs[o0:o0 + n_ao]
        out_refs = refs[o0 + n_ao:o0 + n_ao + n_o + n_al]
        acc_refs = refs[o0 + n_ao + n_o + n_al:o0 + n_ao + n_o + n_al + n_ac]
        scr = refs[o0 + n_ao + n_o + n_al + n_ac:]
        i, k = pl.program_id(0), pl.program_id(1)
        ck = tk if prologue is None else min(tk, PROLOGUE_CHUNK)
        part = None
        for c0 in range(0, tk, ck):
            cols = slice(c0, c0 + ck)
            tiles = [r[:, cols] for r in a_refs]
            a_list, extra = (tiles, []) if prologue is None else prologue(*tiles)
            for r, v in zip(ao_refs, extra):
                r[:, cols] = v.astype(r.dtype)
            for a, j_b, (_, off) in zip(a_list, b_of_pair, bs):
                b_ref = b_refs[j_b]
                if resident:
                    b = b_ref[pl.ds(pl.multiple_of((k + off) * tk + c0, ck), ck), :]
                else:
                    b = b_ref[cols, :] if mode == "nn" else b_ref[:, cols]
                prod = dot(a.astype(BF16), b.astype(BF16))
                part = prod if part is None else part + prod

        def finish(total):
            vals = epilogue(total, *[a_refs[j][...] for j in a_to_epilogue], *[r[...] for r in e_refs])
            if not isinstance(vals, (tuple, list)):
                vals = (vals,)
            for r, v in zip(out_refs, vals[:n_o + n_al]):
                r[...] = v.astype(r.dtype)
            for r, v in zip(acc_refs, vals[n_o + n_al:]):
                @pl.when(i == 0)
                def _():
                    r[...] = v

                @pl.when(i > 0)
                def _():
                    r[...] += v

        if nk == 1:
            finish(part)
            return
        acc_ref = scr[0]

        @pl.when(k == 0)
        def _():
            acc_ref[...] = part

        @pl.when(k > 0)
        def _():
            acc_ref[...] += part

        @pl.when(k == nk - 1)
        def _():
            finish(acc_ref[...])

    in_specs = [pl.BlockSpec((tm, tk), lambda i, k, off=off: (i, k + off)) for (_, off) in a_ins]
    in_specs += [pl.BlockSpec((1, tk), lambda i, k: (0, k)) for _ in a_consts]
    if resident:
        assert mode == "nn"
        in_specs += [pl.BlockSpec(b.shape, lambda i, k: (0, 0), pipeline_mode=pl.Buffered(1)) for b in b_arrays]
    elif mode == "nn":
        in_specs += [pl.BlockSpec((tk, n), lambda i, k, off=off: (k + off, 0)) for (_, off) in bs]
    else:
        in_specs += [pl.BlockSpec((n, tk), lambda i, k, off=off: (0, k + off)) for (_, off) in bs]
    in_specs += [pl.BlockSpec((tm, w), lambda i, k, cb=cb: (i, cb)) for (_, w, cb) in e_ins]
    in_specs += [pl.BlockSpec(c.shape, lambda i, k, nd=c.ndim: (0,) * nd) for c in consts]
    in_specs += [ANY for _ in held]
    out_shape = [jax.ShapeDtypeStruct((m, kdim), dt) for dt in a_outs]
    out_specs = [pl.BlockSpec((tm, tk), lambda i, k: (i, k)) for _ in a_outs]
    out_shape += [jax.ShapeDtypeStruct((m, w), dt) for (w, dt) in outs]
    out_specs += [pl.BlockSpec((tm, w), lambda i, k: (i, 0)) for (w, _) in outs]
    out_shape += [jax.ShapeDtypeStruct(a.shape, a.dtype) for (a, _, _) in alias_outs]
    out_specs += [pl.BlockSpec((tm, w), lambda i, k, cb=cb: (i, cb)) for (_, w, cb) in alias_outs]
    out_shape += [jax.ShapeDtypeStruct(s_, F32) for s_ in accs]
    out_specs += [pl.BlockSpec(s_, lambda i, k: (0, 0)) for s_ in accs]
    aliases, k_in = {}, n_a + n_b + n_e + n_c
    for j, (a, _, _) in enumerate(alias_outs):
        if not isinstance(a, jax.ShapeDtypeStruct):
            aliases[k_in] = n_ao + n_o + j
            k_in += 1
    operands = [a for (a, _) in a_ins] + list(a_consts) + b_arrays + [a for (a, _, _) in e_ins] + list(consts) + held
    res, extra = hosted_call(
        body, exch, name, (m // tm, nk), in_specs, out_specs, out_shape,
        [pltpu.VMEM((tm, n), F32)] if nk > 1 else [], operands,
        ("arbitrary" if n_ac else "parallel", "arbitrary"), aliases)
    return res if exch is None else (res, extra)


def _cumsum_rows(x):
    n = x.shape[0]
    row = lax.broadcasted_iota(jnp.int32, x.shape, 0)
    s = 1
    while s < n:
        x = x + jnp.where(row >= s, pltpu.roll(x, s, 0), 0.0)
        s *= 2
    return x


def _hg_prep(zq, zf, lb, reverse, b=None):
    n = zq.shape[0]
    q = _silu(zq)
    sig = _sigmoid(zf)
    sn = 1.0 - sig
    f = lb + (1.0 - lb) * sig
    k = (1.0 - lb) * sn
    if b is None:
        g = jnp.log(jnp.maximum(f, TINY))
        b = _cumsum_rows(g)
        if reverse:
            b = b[n - 1:n] - b + g
    b_last = b[0:1] if reverse else b[n - 1:n]
    b_ref = b[n // 2:n // 2 + 1]
    e1 = jnp.exp(b)
    e2 = jnp.exp(jnp.clip(b - b_ref, -EXP_CLAMP, EXP_CLAMP))
    e3 = jnp.exp(jnp.clip(b_ref - b, -EXP_CLAMP, EXP_CLAMP))
    e4 = jnp.exp(b_last - b)
    return dict(q=q, k=k, sig=sig, sn=sn, f=f, b=b, e1=e1, e2=e2, e3=e3, e4=e4, e_last=jnp.exp(b_last),
                qe=(q * e1).astype(BF16), qt=(q * e2).astype(BF16), kt=(k * e3).astype(BF16),
                ks=(k * e4).astype(BF16))


def _hg_mask(n, reverse):
    t = lax.broadcasted_iota(jnp.int32, (n, n), 0)
    s = lax.broadcasted_iota(jnp.int32, (n, n), 1)
    return (s >= t) if reverse else (s <= t)


def hgrn_fwd(name, z, lb_f, lb_b, exch=None):
    m = z.shape[0]
    C, T = HG_CHUNK, min(HG_BLOCK_FWD, m)
    nb, cpb = m // T, T // C

    def body(zq_f, zf_f, zi_f, zq_b, zf_b, zi_b, lbf_ref, lbb_ref, of_ref, ob_ref, sf_ref, sb_ref, bf_ref, bb_ref,
             st_ref):
        @pl.when(pl.program_id(0) == 0)
        def _():
            st_ref[...] = jnp.zeros(st_ref.shape, F32)

        dirs = ((zq_f, zf_f, zi_f, lbf_ref, of_ref, sf_ref), (zq_b, zf_b, zi_b, lbb_ref, ob_ref, sb_ref))
        b_refs = (bf_ref, bb_ref)

        def chunk(ci, carry):
            work = []
            for d, (zq, zf, zi, lb_ref, o_ref, s_ref) in enumerate(dirs):
                cc = ci if d == 0 else cpb - 1 - ci
                rows = pl.ds(pl.multiple_of(cc * C, C), C)
                pre = _hg_prep(zq[rows, :], zf[rows, :], lb_ref[...], d == 1)
                v = zi[rows, :].astype(BF16)
                work.append((cc, rows, pre, v, [st_ref[d, h] for h in range(HEADS)]))
            heads = [(d, h, slice(h * HEAD_DIM, (h + 1) * HEAD_DIM)) for d in range(2) for h in range(HEADS)]
            first = {}
            for d, h, sl in heads:
                _, _, pre, v, sts = work[d]
                first[d, h] = (_dot_nt(pre["qt"][:, sl], pre["kt"][:, sl]),
                               _dot_nt(pre["qe"][:, sl], sts[h].astype(BF16)),
                               _dot_tn(v[:, sl], pre["ks"][:, sl]))
            results = [([], []), ([], [])]
            for d, h, sl in heads:
                _, _, pre, v, sts = work[d]
                scores, o_inter, st_add = first[d, h]
                a = jnp.where(_hg_mask(C, d == 1), scores, 0.0).astype(BF16)
                results[d][0].append(o_inter + _dot(a, v[:, sl]))
                results[d][1].append(sts[h] * pre["e_last"][:, sl] + st_add)
            results = [(jnp.concatenate(o_parts, axis=1), new_sts) for (o_parts, new_sts) in results]
            for d, (zq, zf, zi, lb_ref, o_ref, s_ref) in enumerate(dirs):
                cc, rows, pre, _, sts = work[d]
                o_ref[rows, :] = results[d][0]
                b_refs[d][rows, :] = pre["b"]
                for h in range(HEADS):
                    s_ref[cc, h] = sts[h]
                    st_ref[d, h] = results[d][1][h]
            return carry

        lax.fori_loop(0, cpb, chunk, 0)

    def zspec(cb, rev):
        return pl.BlockSpec((T, D), (lambda i: (nb - 1 - i, cb)) if rev else (lambda i: (i, cb)))

    def sspec(rev):
        shape = (cpb, HEADS, HEAD_DIM, HEAD_DIM)
        return pl.BlockSpec(shape, (lambda i: (nb - 1 - i, 0, 0, 0)) if rev else (lambda i: (i, 0, 0, 0)))

    lbspec = pl.BlockSpec((1, D), lambda i: (0, 0))
    states = jax.ShapeDtypeStruct((m // C, HEADS, HEAD_DIM, HEAD_DIM), F32)
    outs, extra = hosted_call(
        body, exch, name, (nb,),
        [zspec(ZQ, False), zspec(ZFF, False), zspec(ZI, False), zspec(ZQ, True), zspec(ZFB, True), zspec(ZI, True),
         lbspec, lbspec],
        [zspec(0, False), zspec(0, True), sspec(False), sspec(True), zspec(0, False), zspec(0, True)],
        [jax.ShapeDtypeStruct((m, D), F32), jax.ShapeDtypeStruct((m, D), F32), states, states,
         jax.ShapeDtypeStruct((m, D), F32), jax.ShapeDtypeStruct((m, D), F32)],
        [pltpu.VMEM((2, HEADS, HEAD_DIM, HEAD_DIM), F32)], [z, z, z, z, z, z, lb_f, lb_b], ("arbitrary",))
    return outs, extra


def hgrn_bwd(name, z, d_o, s_f, s_b, b_f, b_b, lb_f, lb_b, exch=None):
    m = z.shape[0]
    C, T = HG_CHUNK, min(HG_BLOCK_BWD, m)
    nb, cpb = m // T, T // C

    def body(zq_f, zf_f, zi_f, do_f, sf_ref, zq_b, zf_b, zi_b, do_b, sb_ref, lbf_ref, lbb_ref, bf_ref, bb_ref,
             dqf_ref, dvf_ref, dqb_ref, dvb_ref, dzf_f, dzf_b, dlbf_ref, dlbb_ref,
             dst_ref):
        b_refs = (bf_ref, bb_ref)
        @pl.when(pl.program_id(0) == 0)
        def _():
            dst_ref[...] = jnp.zeros(dst_ref.shape, F32)
            dlbf_ref[...] = jnp.zeros(dlbf_ref.shape, F32)
            dlbb_ref[...] = jnp.zeros(dlbb_ref.shape, F32)

        dirs = ((zq_f, zf_f, zi_f, do_f, sf_ref, lbf_ref, dqf_ref, dvf_ref, dzf_f, dlbf_ref),
                (zq_b, zf_b, zi_b, do_b, sb_ref, lbb_ref, dqb_ref, dvb_ref, dzf_b, dlbb_ref))

        def chunk(ci, carry):
            work = []
            for d, (zq, zf, zi, do_ref, s_ref, lb_ref, dq_ref, dv_ref, dzf_ref, dlb_ref) in enumerate(dirs):
                cc = cpb - 1 - ci if d == 0 else ci
                rows = pl.ds(pl.multiple_of(cc * C, C), C)
                lb = lb_ref[...]
                pre = _hg_prep(zq[rows, :], zf[rows, :], lb, d == 1, b=b_refs[d][rows, :])
                work.append((rows, lb, pre, zi[rows, :].astype(BF16), do_ref[rows, :],
                             [s_ref[cc, h] for h in range(HEADS)], [dst_ref[d, h] for h in range(HEADS)],
                             dlb_ref[...]))
            heads = [(d, h, slice(h * HEAD_DIM, (h + 1) * HEAD_DIM)) for d in range(2) for h in range(HEADS)]
            first = {}
            for d, h, sl in heads:
                _, _, pre, v, do, st_prevs, dsts, _ = work[d]
                dst16 = dsts[h].astype(BF16)
                first[d, h] = (_dot_nt(pre["qt"][:, sl], pre["kt"][:, sl]),
                               _dot_nt(do[:, sl], v[:, sl]),
                               _dot(do[:, sl], st_prevs[h].astype(BF16)),
                               _dot(v[:, sl], dst16),
                               _dot_nt(pre["ks"][:, sl], dst16),
                               _dot_tn(do[:, sl], pre["qe"][:, sl]))
            parts = [[[] for _ in range(6)] for _ in range(2)]
            for d, h, sl in heads:
                _, _, pre, v, do, st_prevs, dsts, _ = work[d]
                scores, dscores, dq_inter, dk_state, dv_state, dst_add = first[d, h]
                mask = _hg_mask(C, d == 1)
                a = jnp.where(mask, scores, 0.0).astype(BF16)
                da = jnp.where(mask, dscores, 0.0).astype(BF16)
                dq_p, dki_p, dks_p, dv_p, rr_p, new_dsts = parts[d]
                dq_p.append(_dot(da, pre["kt"][:, sl]) * pre["e2"][:, sl] + dq_inter * pre["e1"][:, sl])
                dki_p.append(_dot_tn(da, pre["qt"][:, sl]) * pre["e3"][:, sl])
                dks_p.append(dk_state * pre["e4"][:, sl])
                dv_p.append(_dot_tn(a, do[:, sl]) + dv_state)
                rr_p.append(pre["e_last"][:, sl] * _colsum(dsts[h] * st_prevs[h]))
                new_dsts.append(dsts[h] * pre["e_last"][:, sl] + dst_add)
            results = []
            for d, (rows, lb, pre, v, do, st_prevs, dsts, dlb_old) in enumerate(work):
                rev = d == 1
                dq_p, dki_p, dks_p, dv_p, rr_p, new_dsts = parts[d]
                dq, dki, dks, dv, rr = (jnp.concatenate(p_, axis=1) for p_ in (dq_p, dki_p, dks_p, dv_p, rr_p))
                x = pre["q"] * dq - pre["k"] * dki
                y = pre["k"] * dks
                if rev:
                    dg = _cumsum_rows(x - y) + _colsum(y) + rr
                else:
                    dg = _cumsum_rows(y - x) + (x - y) + _colsum(x) + rr
                inv_f = jnp.where(pre["f"] > TINY, 1.0 / pre["f"], 0.0)
                u = dg * inv_f - (dki + dks)
                results.append((dq, dv, (1.0 - lb) * pre["sig"] * pre["sn"] * u, dlb_old + _colsum(pre["sn"] * u),
                                new_dsts))
            for d, (zq, zf, zi, do_ref, s_ref, lb_ref, dq_ref, dv_ref, dzf_ref, dlb_ref) in enumerate(dirs):
                rows = work[d][0]
                dq, dv, dzf, dlb, new_dsts = results[d]
                dq_ref[rows, :] = dq.astype(dq_ref.dtype)
                dv_ref[rows, :] = dv.astype(dv_ref.dtype)
                dzf_ref[rows, :] = dzf.astype(dzf_ref.dtype)
                dlb_ref[...] = dlb
                for h in range(HEADS):
                    dst_ref[d, h] = new_dsts[h]
            return carry

        lax.fori_loop(0, cpb, chunk, 0)

    def rspec(cb, rev):
        return pl.BlockSpec((T, D), (lambda i: (i, cb)) if rev else (lambda i: (nb - 1 - i, cb)))

    def sspec(rev):
        shape = (cpb, HEADS, HEAD_DIM, HEAD_DIM)
        return pl.BlockSpec(shape, (lambda i: (i, 0, 0, 0)) if rev else (lambda i: (nb - 1 - i, 0, 0, 0)))

    lbspec = pl.BlockSpec((1, D), lambda i: (0, 0))
    half = jax.ShapeDtypeStruct((m, D), BF16)
    row = jax.ShapeDtypeStruct((1, D), F32)
    outs, extra = hosted_call(
        body, exch, name, (nb,),
        [rspec(ZQ, False), rspec(ZFF, False), rspec(ZI, False), rspec(0, False), sspec(False),
         rspec(ZQ, True), rspec(ZFB, True), rspec(ZI, True), rspec(0, True), sspec(True), lbspec, lbspec,
         rspec(0, False), rspec(0, True)],
        [rspec(0, False), rspec(0, False), rspec(0, True), rspec(0, True), rspec(0, False), rspec(0, True),
         lbspec, lbspec],
        [half, half, half, half, half, half, row, row],
        [pltpu.VMEM((2, HEADS, HEAD_DIM, HEAD_DIM), F32)],
        [z, z, z, d_o, s_f, z, z, z, d_o, s_b, lb_f, lb_b, b_f, b_b], ("arbitrary",))
    return outs, extra


def _heads(fn, *arrs):
    res = [fn(*[a[:, h * HEAD_DIM:(h + 1) * HEAD_DIM] for a in arrs]) for h in range(arrs[0].shape[1] // HEAD_DIM)]
    return [jnp.concatenate(parts, axis=1) for parts in zip(*res)]


def _hg_post(o_f, o_b, zg, g):
    def head(of, ob, zgh, gh):
        on, _ = _rms(of + ob)
        return (on * gh * _silu(zgh),)
    return _heads(head, o_f, o_b, zg, g)[0]


def _hg_post_bwd(da, o_f, o_b, zg, g):
    def head(dah, of, ob, zgh, gh):
        on, r = _rms(of + ob)
        sg = _silu(zgh)
        d_on = dah * sg
        return _rms_bwd(d_on, on, r, gh), dah * on * gh * _silu_grad(zgh), d_on * on
    d_o, dzg, dg = _heads(head, da, o_f, o_b, zg, g)
    return d_o, dzg, _colsum(dg)


def _sg_parts(zv, ln_g, ln_b):
    vg = _gelu(zv)
    xc = vg - _mean(vg)
    rstd = lax.rsqrt(_mean(xc * xc) + EPS)
    vh = xc * rstd
    return vh, rstd, vh * ln_g + ln_b


def _sg_lane_group(shape):
    return lax.broadcasted_iota(jnp.int32, shape, 1) < SG_GROUP_DIM


def _sg_mix(w, v16, transpose):
    rows = v16.shape[0]
    out = []
    for c in range(rows // SG_CHUNK):
        parts = []
        for j in range(SG_WIDTH // 128):
            vj = v16[c * SG_CHUNK:(c + 1) * SG_CHUNK, j * 128:(j + 1) * 128]
            w0 = w[(2 * j) * SG_CHUNK:(2 * j + 1) * SG_CHUNK]
            w1 = w[(2 * j + 1) * SG_CHUNK:(2 * j + 2) * SG_CHUNK]
            dot = _dot_tn if transpose else _dot
            parts.append(jnp.where(_sg_lane_group((SG_CHUNK, 128)), dot(w0, vj), dot(w1, vj)))
        out.append(jnp.concatenate(parts, axis=1))
    return jnp.concatenate(out, axis=0)


def _sg_fwd(zu, zv, w, bias, ln_g, ln_b):
    _, _, v = _sg_parts(zv, ln_g, ln_b)
    reps = zu.shape[0] // SG_CHUNK
    return _gelu(zu) * (_sg_mix(w, v.astype(BF16), False) + jnp.concatenate([bias] * reps, axis=0))


def _sg_bwd(db, zu, zv, w, bias, ln_g, ln_b):
    vh, rstd, v = _sg_parts(zv, ln_g, ln_b)
    v16 = v.astype(BF16)
    reps = zu.shape[0] // SG_CHUNK
    sg = _sg_mix(w, v16, False) + jnp.concatenate([bias] * reps, axis=0)
    dzu = db * sg * _gelu_grad(zu)
    dsg = db * _gelu(zu)
    dsg16 = dsg.astype(BF16)
    dv = _sg_mix(w, dsg16, True)
    low = _sg_lane_group((SG_CHUNK, 128))
    dw = []
    for g in range(SG_WIDTH // SG_GROUP_DIM):
        j, keep = g // 2, (low if g % 2 == 0 else jnp.logical_not(low))
        acc = jnp.zeros((SG_CHUNK, SG_CHUNK), F32)
        for c in range(reps):
            rows = slice(c * SG_CHUNK, (c + 1) * SG_CHUNK)
            dj = jnp.where(keep, dsg16[rows, j * 128:(j + 1) * 128], jnp.zeros((), BF16))
            acc = acc + _dot_nt(dj, v16[rows, j * 128:(j + 1) * 128])
        dw.append(acc)
    dbias = sum(dsg[c * SG_CHUNK:(c + 1) * SG_CHUNK] for c in range(reps))
    dvh = dv * ln_g
    dvg = rstd * (dvh - _mean(dvh) - vh * _mean(dvh * vh))
    dzuv = jnp.concatenate([dzu, dvg * _gelu_grad(zv)], axis=1)
    return (dzuv, jnp.concatenate(dw, axis=0), dbias, _colsum(dv * vh), _colsum(dv))


def lower_bounds(name, gamma_f, gamma_b):
    def body(gf_ref, gb_ref, lf_ref, lb_ref):
        for g_ref, o_ref in ((gf_ref, lf_ref), (gb_ref, lb_ref)):
            g0, g1 = g_ref[0:1, :], g_ref[1:2, :]
            mx = jnp.maximum(g0, g1)
            e0, e1 = jnp.exp(g0 - mx), jnp.exp(g1 - mx)
            sm0, sm1 = e0 / (e0 + e1), e1 / (e0 + e1)
            o_ref[0:1, :] = sm0 - sm0
            o_ref[1:2, :] = (sm0 + sm1) - sm0
    shp = jax.ShapeDtypeStruct(gamma_f.shape, F32)
    return pl.pallas_call(body, name=name, out_shape=[shp, shp])(gamma_f, gamma_b)


def lower_bounds_bwd(name, gamma_f, gamma_b, dlb_f, dlb_b):
    def body(gf_ref, gb_ref, df_ref, db_ref, of_ref, ob_ref):
        for g_ref, d_ref, o_ref in ((gf_ref, df_ref, of_ref), (gb_ref, db_ref, ob_ref)):
            g0, g1 = g_ref[0:1, :], g_ref[1:2, :]
            mx = jnp.maximum(g0, g1)
            e0, e1 = jnp.exp(g0 - mx), jnp.exp(g1 - mx)
            sm0, sm1 = e0 / (e0 + e1), e1 / (e0 + e1)
            d1 = d_ref[1:2, :] * sm0 * sm1
            o_ref[0:1, :] = -d1
            o_ref[1:2, :] = d1
    shp = jax.ShapeDtypeStruct(gamma_f.shape, F32)
    return pl.pallas_call(body, name=name, out_shape=[shp, shp])(gamma_f, gamma_b, dlb_f, dlb_b)


def _row(a, l):
    return a[l:l + 1]


class LocalPlan:
    def __init__(self, weights):
        self.W = weights
        self.grads = [dict() for _ in range(DEPTH)]

    def exch(self, host):
        return None

    def done(self, host, outs):
        pass

    def early_small(self, packed):
        pass


def local_step(x, p, target, S, plan):
    m = x.shape[0]

    def hmm(tag, *args, **kw):
        ex = plan.exch(tag)
        res = mm(tag, *args, exch=ex, **kw)
        if ex is None:
            return res
        plan.done(tag, res[1])
        return res[0]

    lb_f, lb_b = lower_bounds("lower_bounds", S["lb_gamma_fwd"], S["lb_gamma_bwd"])
    saved = []
    for l in range(DEPTH):
        t = f"l{l}_"
        W = plan.W[l]
        g_pre, g_post = _row(S["norm_mix_pre"], l), _row(S["norm_mix_post"], l)
        g_fpre, g_fpost = _row(S["norm_ffn_pre"], l), _row(S["norm_ffn_post"], l)
        hg_g = _row(S["hg_norm"], l)
        sg_w = S["sg_w"][l].reshape(SG_WIDTH // SG_GROUP_DIM * SG_CHUNK, SG_CHUNK).astype(BF16)
        sg_bias = jnp.repeat(S["sg_b"][l].T, SG_GROUP_DIM, axis=1)
        ln_g, ln_b = _row(S["sg_ln_g"], l), _row(S["sg_ln_b"], l)
        lbf, lbb = _row(lb_f, l), _row(lb_b, l)

        if l == 0:
            (h,) = rowwise(t + "pre_norm", lambda xv, g: (_rms(xv)[0] * g,), m, ins=[(x, D, 0)], consts=[g_pre],
                           outs=[(D, BF16)])
        z = hmm(t + "in_proj", h, W["w_in"], "nn", tm=BIG_TILE // 2, tn=BIG_TILE)
        (o_f, o_b, s_f, s_b, b_f, b_b), extra = hgrn_fwd(t + "hgrn_fwd", z, lbf, lbb, exch=plan.exch(t + "hgrn_fwd"))
        plan.done(t + "hgrn_fwd", extra)
        (b_out,) = rowwise(t + "sgu_fwd", _sg_fwd, m, ins=[(z, SG_WIDTH, ZU), (z, SG_WIDTH, ZV)],
                           consts=[sg_w, sg_bias, ln_g, ln_b], outs=[(SG_WIDTH, BF16)])

        def post_pro(of, ob, zg, g):
            ao = _hg_post(of, ob, zg, g).astype(BF16)
            return [ao], [ao]
        a_out, pa = mm_fused(t + "proj_a", m, [(o_f, 0), (o_b, 0), (z, ZG)], [(W["w_a"], 0)], "nn", D,
                             prologue=post_pro, a_outs=[BF16], a_consts=[hg_g], epilogue=lambda tot: (tot,),
                             outs=[(D, BF16)], tm=1024)
        pb = mm(t + "proj_b", b_out, W["w_b"], "nn", BF16)

        def merge_pro(a, b, ga, gb):
            mg = (_sigmoid(ga) * a + _sigmoid(gb) * b).astype(BF16)
            return [mg], [mg]

        def post_pre(mixv, xv, gp, gf):
            x1 = xv + _rms(mixv)[0] * gp
            return mixv, x1, _rms(x1)[0] * gf
        merged, mix, x1, h2 = mm_fused(
            t + "out_proj", m, [(pa, 0), (pb, 0), (z, GA), (z, GB)], [(W["w_out"], 0)], "nn", D, prologue=merge_pro,
            a_outs=[BF16], e_ins=[(x, D, 0)], consts=[g_post, g_fpre], epilogue=post_pre,
            outs=[(D, F32), (D, F32), (D, BF16)])
        gu = hmm(t + "ffn_in", h2, W["w_gu"], "nn", BF16, tm=BIG_TILE, tn=BIG_TILE)

        def act_pro(gt, up):
            hd = (_silu(gt.astype(F32)) * up).astype(BF16)
            return [hd], [hd]
        hid, ff, x2 = mm_fused(
            t + "ffn_out", m, [(gu, 0), (gu, FFN_PAD // 1024)], [(W["w_down"], 0)], "nn", FFN_PAD, prologue=act_pro,
            a_outs=[BF16], e_ins=[(x1, D, 0)], consts=[g_fpost],
            epilogue=lambda f, xv, g: (f, xv + _rms(f)[0] * g), outs=[(D, F32), (D, F32)], tm=1024, resident=True)
        e = mm(t + "ple_proj", (p, l), W["w_ple"], "nn")

        if l + 1 < DEPTH:
            def ple_add(tv, xv, ev, g):
                x3 = xv + ev * _sigmoid(tv)
                return tv, x3, _rms(x3)[0] * g
            tg, x3, h_next = mm_fused(
                t + "ple_gate", m, [(x2, 0)], [(W["w_ple_gate"], 0)], "nn", D, a_to_epilogue=(0,), e_ins=[(e, D, 0)],
                consts=[_row(S["norm_mix_pre"], l + 1)], epilogue=ple_add, outs=[(D, F32), (D, F32), (D, BF16)],
                tm=1024)
        else:
            def ple_loss(tv, xv, ev, tgt):
                err = xv + ev * _sigmoid(tv) - tgt
                return tv, err * (1.0 / D), _colsum(err * err)
            tg, x3, loss_cols = mm_fused(
                t + "ple_gate", m, [(x2, 0)], [(W["w_ple_gate"], 0)], "nn", D, a_to_epilogue=(0,),
                e_ins=[(e, D, 0), (target, D, 0)], epilogue=ple_loss, outs=[(D, F32), (D, F32)], accs=[(1, D)],
                tm=1024)
            h_next = None
        saved.append(dict(x=x, h=h, z=z, o_f=o_f, o_b=o_b, s_f=s_f, s_b=s_b, b_f=b_f, b_b=b_b, a_out=a_out,
                          b_out=b_out, pa=pa, pb=pb,
                          merged=merged, mix=mix, x1=x1, h2=h2, gu=gu, hid=hid, ff=ff, x2=x2, e=e, tg=tg,
                          sg_w=sg_w, sg_bias=sg_bias))
        x, h = x3, h_next

    dx = x

    gs = {n: [None] * DEPTH for n in SMALL}
    dlb_f, dlb_b = [None] * DEPTH, [None] * DEPTH

    for l in reversed(range(DEPTH)):
        t = f"l{l}_bwd_"
        sv, W = saved[l], plan.W[l]
        g_pre, g_post = _row(S["norm_mix_pre"], l), _row(S["norm_mix_post"], l)
        g_fpre, g_fpost = _row(S["norm_ffn_pre"], l), _row(S["norm_ffn_post"], l)
        hg_g = _row(S["hg_norm"], l)
        ln_g, ln_b = _row(S["sg_ln_g"], l), _row(S["sg_ln_b"], l)
        lbf, lbb = _row(lb_f, l), _row(lb_b, l)

        def wgrad(nm, tag, a, b):
            a_dtype = (a[0] if isinstance(a, tuple) else a).dtype
            plan.grads[l][nm] = mm(tag, a, b, "tn", BF16, tk=WGRAD_TOKENS if a_dtype == BF16 else WGRAD_TOKENS // 2)

        def ple_pro(d3, ev, tv):
            s = _sigmoid(tv)
            de_, dt_ = (d3 * s).astype(BF16), (d3 * ev * s * (1.0 - s)).astype(BF16)
            return [dt_], [dt_, de_]

        def ffn_post_bwd(d2p, d3, f, g):
            d2 = d3 + d2p
            fh, r = _rms(f)
            return d2, _rms_bwd(d2, fh, r, g), _colsum(d2 * fh)
        dt, de, dx2, dff, gs["norm_ffn_post"][l] = mm_fused(
            t + "ple_gate_dx", m, [(dx, 0), (sv["e"], 0), (sv["tg"], 0)], [(W["w_ple_gate"], 0)], "nt", D,
            prologue=ple_pro, a_outs=[BF16, BF16], a_to_epilogue=(0,), e_ins=[(sv["ff"], D, 0)], consts=[g_fpost],
            epilogue=ffn_post_bwd, outs=[(D, F32), (D, BF16)], accs=[(1, D)])
        wgrad("w_ple", t + "w_ple", (p, l), de)
        wgrad("w_ple_gate", t + "w_ple_gate", sv["x2"], dt)
        wgrad("w_down", t + "w_down", sv["hid"], dff)
        dhid = mm(t + "ffn_out_dx", dff, W["w_down"], "nt", BF16, tm=BIG_TILE)

        def act_bwd(dh, gt, up):
            dh, gt = dh.astype(F32), gt.astype(F32)
            s = _sigmoid(gt)
            dg_ = (dh * up * (s * (1.0 + gt * (1.0 - s)))).astype(BF16)
            du_ = (dh * (gt * s)).astype(BF16)
            return [dg_, du_], [dg_, du_]

        def pre_post_bwd(dh, d2, x1v, mixv, gf, gp):
            xh, r1 = _rms(x1v)
            d1 = d2 + _rms_bwd(dh, xh, r1, gf)
            mh, rm = _rms(mixv)
            return d1, _rms_bwd(d1, mh, rm, gp), _colsum(dh * xh), _colsum(d1 * mh)
        off = FFN_PAD // 1024
        w_gu_t = W["w_gu"].T
        dgate, dup, dx1, dmix, gs["norm_ffn_pre"][l], gs["norm_mix_post"][l] = mm_fused(
            t + "ffn_in_dx", m, [(dhid, 0), (sv["gu"], 0), (sv["gu"], off)], [(w_gu_t, 0), (w_gu_t, off)], "nn",
            FFN_PAD, prologue=act_bwd, a_outs=[BF16, BF16], e_ins=[(dx2, D, 0), (sv["x1"], D, 0), (sv["mix"], D, 0)],
            consts=[g_fpre, g_post], epilogue=pre_post_bwd, outs=[(D, F32), (D, BF16)], accs=[(1, D), (1, D)],
            resident=True)
        wgrad("w_gate", t + "w_gate", sv["h2"], dgate)
        wgrad("w_up", t + "w_up", sv["h2"], dup)
        wgrad("w_out", t + "w_out", sv["merged"], dmix)

        def merge_bwd(dm, a, b, gab):
            sa, sb = _sigmoid(gab[:, :D]), _sigmoid(gab[:, D:])
            dgab = jnp.concatenate([dm * a * sa * (1.0 - sa), dm * b * sb * (1.0 - sb)], axis=1)
            return dm * sa, dm * sb, dgab
        dpa, dpb, dz = mm_fused(
            t + "out_proj_dx", m, [(dmix, 0)], [(W["w_out"], 0)], "nt", D,
            e_ins=[(sv["pa"], D, 0), (sv["pb"], D, 0), (sv["z"], 2 * D, 3)], epilogue=merge_bwd,
            outs=[(D, BF16), (D, BF16)], alias_outs=[(jax.ShapeDtypeStruct((m, N_IN), BF16), 2 * D, 3)])
        wgrad("w_a", t + "w_a", sv["a_out"], dpa)
        wgrad("w_b", t + "w_b", sv["b_out"], dpb)
        db = mm(t + "proj_b_dx", dpb, W["w_b"], "nt")

        dz, dsw, dbias, gs["sg_ln_g"][l], gs["sg_ln_b"][l] = rowwise(
            t + "sgu", _sg_bwd, m, ins=[(db, SG_WIDTH, 0), (sv["z"], SG_WIDTH, ZU), (sv["z"], SG_WIDTH, ZV)],
            consts=[sv["sg_w"], sv["sg_bias"], ln_g, ln_b], alias_outs=[(dz, 2 * SG_WIDTH, 5)],
            accs=[(SG_WIDTH // SG_GROUP_DIM * SG_CHUNK, SG_CHUNK), (SG_CHUNK, SG_WIDTH), (1, SG_WIDTH), (1, SG_WIDTH)])
        gs["sg_w"][l] = dsw.reshape(1, SG_WIDTH // SG_GROUP_DIM, SG_CHUNK, SG_CHUNK)
        gs["sg_b"][l] = dbias.reshape(SG_CHUNK, SG_WIDTH // SG_GROUP_DIM, SG_GROUP_DIM).sum(-1).T[None]

        d_o, dz, gs["hg_norm"][l] = mm_fused(
            t + "proj_a_dx", m, [(dpa, 0)], [(W["w_a"], 0)], "nt", D,
            e_ins=[(sv["o_f"], D, 0), (sv["o_b"], D, 0), (sv["z"], D, ZG)], consts=[hg_g], epilogue=_hg_post_bwd,
            outs=[(D, BF16)], alias_outs=[(dz, D, ZG)], accs=[(1, D)], tm=256)
        if l == 0:
            part = {n: (g if not isinstance(g, list) else jnp.concatenate(
                [jnp.zeros((1,) + g[1].shape[1:], F32) if gl is None else gl for gl in g], axis=0))
                for n, g in gs.items()}
            plan.early_small(_pack([part[n].reshape(S[n].shape) for n in SMALL]))
        (dq_f, dv_f, dq_b, dv_b, dzf_f, dzf_b, dlb_f[l], dlb_b[l]), extra = hgrn_bwd(
            t + "hgrn", sv["z"], d_o, sv["s_f"], sv["s_b"], sv["b_f"], sv["b_b"], lbf, lbb,
            exch=plan.exch(t + "hgrn"))
        plan.done(t + "hgrn", extra)

        def combine(dqf, dqb, dvf, dvb, dff_, dfb_, zq):
            dq = dqf.astype(F32) + dqb.astype(F32)
            dv = dvf.astype(F32) + dvb.astype(F32)
            return (jnp.concatenate([(dq * _silu_grad(zq)).astype(BF16), dff_, dfb_, dv.astype(BF16)], axis=1),)
        (dz,) = rowwise(t + "hgrn_combine", combine, m,
                        ins=[(dq_f, D, 0), (dq_b, D, 0), (dv_f, D, 0), (dv_b, D, 0), (dzf_f, D, 0), (dzf_b, D, 0),
                             (sv["z"], D, ZQ)], alias_outs=[(dz, 4 * D, 0)], tm=2 * ROW_TILE)
        wgrad("w_in", t + "w_in", sv["h"], dz)

        def pre_bwd(dhv, d1, xv, g):
            xh, r = _rms(xv)
            return d1 + _rms_bwd(dhv, xh, r, g), _colsum(dhv * xh)
        ex = plan.exch(t + "in_proj_dx")
        res = mm_fused(t + "in_proj_dx", m, [(dz, 0)], [(W["w_in"], 0)], "nt", N_IN,
                       e_ins=[(dx1, D, 0), (sv["x"], D, 0)], consts=[g_pre], epilogue=pre_bwd, outs=[(D, F32)],
                       accs=[(1, D)], tm=1024, exch=ex)
        if ex is not None:
            res, extra = res
            plan.done(t + "in_proj_dx", extra)
        dx, gs["norm_mix_pre"][l] = res
        saved[l] = None
        if l == DEPTH - 1:
            none = jnp.zeros((1, D), F32)
            gs["lb_gamma_fwd"], gs["lb_gamma_bwd"] = lower_bounds_bwd(
                "lower_bounds_bwd", S["lb_gamma_fwd"], S["lb_gamma_bwd"], jnp.concatenate([none, dlb_f[l]], axis=0),
                jnp.concatenate([none, dlb_b[l]], axis=0))

    small ={n: (g if not isinstance(g, list) else jnp.concatenate(g, axis=0)).reshape(S[n].shape)
             for n, g in gs.items()}
    return loss_cols, dx, small


def cast_pad(name, w, rows_p, cols_p):
    _, r, c = w.shape

    def body(w_ref, o_ref):
        if (rows_p, cols_p) != (r, c):
            o_ref[...] = jnp.zeros(o_ref.shape, BF16)
        o_ref[0:r, 0:c] = w_ref[...].astype(BF16)

    return pl.pallas_call(
        body, name=name, grid=(DEPTH,), in_specs=[pl.BlockSpec((None, r, c), lambda l: (l, 0, 0))],
        out_specs=pl.BlockSpec((None, rows_p, cols_p), lambda l: (l, 0, 0)),
        out_shape=jax.ShapeDtypeStruct((DEPTH, rows_p, cols_p), BF16), compiler_params=_params(("parallel",)),
    )(w)


def _shard_shape(n, shape):
    axis, size, _, _ = LAYOUT[n]
    _, r, c = shape
    return (size, c) if axis == 0 else (r, size)


class DistPlan:
    def __init__(self, shards):
        self.shards = shards
        self.W = [dict() for _ in range(DEPTH)]
        self.grads = [dict() for _ in range(DEPTH)]
        self.slots = [dict() for _ in range(DEPTH)]
        rest = [n for n in BIG if n != "w_in"]
        ffn = ["w_gate", "w_up", "w_down"]
        self.schedule = {
            "l0_in_proj": ("gather", [(0, n) for n in rest]),
            "l0_hgrn_fwd": ("gather", [(1, n) for n in BIG if n not in ffn]),
            "l0_ffn_in": ("gather", [(1, n) for n in ffn]),
            "l1_bwd_hgrn": ("scatter", [(1, n) for n in rest]),
            "l1_bwd_in_proj_dx": ("scatter", [(1, "w_in")]),
            "l0_bwd_hgrn": ("scatter", [(0, n) for n in rest]),
            "l0_bwd_in_proj_dx": ("scatter", [(0, "w_in")]),
        }
        self.pending = {}
        self.small_part = self.small_slots = None
        axis, size, dst, _ = LAYOUT["w_in"]
        self.W[0][dst] = gather_two_level("gather_l0_w_in", shards["w_in"], 0, axis, size, GATHERED[dst])

    def _gather(self, host, parts):
        srcs, dsts, items, keys = [], [], [], []
        for layer, n in parts:
            axis, size, dst, base = LAYOUT[n]
            if (layer, dst) not in keys:
                keys.append((layer, dst))
                dsts.append((GATHERED[dst], BF16))
            srcs.append(self.shards[n])
            items.append(("gather", len(srcs) - 1, keys.index((layer, dst)), axis, size, base, layer))
        self.pending[host] = ("gather", keys)
        return Exchange(srcs, dsts, items)

    def _scatter(self, host, parts):
        srcs, dsts, items = [], [], []
        for layer, n in parts:
            axis, size, _, _ = LAYOUT[n]
            srcs.append(self.grads[layer][n])
            dsts.append(((NDEV,) + _shard_shape(n, self.shards[n].shape), BF16))
            items.append(("scatter", len(srcs) - 1, len(dsts) - 1, axis, size, 0, None))
        keys = list(parts)
        if host == "l0_bwd_hgrn" and self.small_part is not None:
            srcs.append(self.small_part)
            dsts.append(((NDEV,) + self.small_part.shape, F32))
            items.append(("copies", len(srcs) - 1, len(dsts) - 1, 0, 0, 0, None))
            keys.append(("small", None))
        self.pending[host] = ("scatter", keys)
        return Exchange(srcs, dsts, items)

    def early_small(self, packed):
        self.small_part = packed

    def exch(self, host):
        if host not in self.schedule:
            return None
        kind, parts = self.schedule[host]
        return self._gather(host, parts) if kind == "gather" else self._scatter(host, parts)

    def done(self, host, outs):
        if host not in self.pending:
            return
        kind, keys = self.pending.pop(host)
        for (layer, n), arr in zip(keys, outs):
            if layer == "small":
                self.small_slots = arr
            else:
                (self.W if kind == "gather" else self.slots)[layer][n] = arr


def adam(name, w, m_, v_, tr, g=None, slots=None):
    L, r, c = w.shape
    assert r % tr == 0
    nt = r // tr
    n_s = 0 if slots is None else L

    def body(*refs):
        s_refs = refs[:n_s]
        g_ref = refs[n_s] if g is not None else None
        w_ref, m_ref, v_ref, g_out, d_out, m_out, v_out = refs[n_s + (g is not None):]

        def update(gv):
            if g_ref is not None:
                gv = gv + g_ref[...] if gv is not None else g_ref[...]
            m2 = B1 * m_ref[...] + (1.0 - B1) * gv
            v2 = B2 * v_ref[...] + (1.0 - B2) * (gv * gv)
            m_hat = m2 / (1.0 - B1 ** STEP)
            v_hat = v2 / (1.0 - B2 ** STEP)
            g_out[...] = gv
            d_out[...] = -LR * (m_hat / (jnp.sqrt(v_hat) + AEPS) + WD * w_ref[...])
            m_out[...] = m2
            v_out[...] = v2

        if slots is None:
            update(None)
            return
        for layer, s_ref in enumerate(s_refs):
            @pl.when(pl.program_id(0) == layer)
            def _():
                gv = s_ref[0][:, :c].astype(F32)
                for j in range(1, NDEV):
                    gv = gv + s_ref[j][:, :c].astype(F32)
                update(gv)

    spec = pl.BlockSpec((None, tr, c), lambda l, i: (l, i, 0))
    arrs, specs = [], []
    if slots is not None:
        assert len(slots) == L and L <= 2
        arrs = list(slots)
        cp = slots[0].shape[2]
        specs = [pl.BlockSpec((NDEV, tr, cp), lambda l, i: (0, i * (1 - l) + (nt - 1) * l, 0)),
                 pl.BlockSpec((NDEV, tr, cp), lambda l, i: (0, i * l, 0))][:L]
    if g is not None:
        arrs.append(g)
        specs.append(spec)
    shp = jax.ShapeDtypeStruct(w.shape, F32)
    return pl.pallas_call(
        body, name=name, grid=(L, nt), in_specs=specs + [spec, spec, spec], out_specs=[spec] * 4,
        out_shape=[shp] * 4, compiler_params=_params(("arbitrary", "arbitrary")),
    )(*arrs, w, m_, v_)


def _pack(arrs):
    parts = []
    for a in arrs:
        a2 = a.reshape(-1, D)
        parts.append(jnp.pad(a2, ((0, -a2.shape[0] % 8), (0, 0))))
    return jnp.concatenate(parts, axis=0)


def _unpack(buf, shapes):
    out, off = [], 0
    for s in shapes:
        rows = 1
        for d_ in s:
            rows *= d_
        rows //= D
        out.append(buf[off:off + rows].reshape(s))
        off += rows + (-rows % 8)
    return out


def kernel(x, p, norm_mix_pre, w_in, lb_gamma_fwd, lb_gamma_bwd, hg_norm, sg_w, sg_b, sg_ln_g, sg_ln_b, w_a, w_b, w_out, norm_mix_post, norm_ffn_pre, w_gate, w_up, w_down, norm_ffn_post, w_ple, w_ple_gate, loss_target, m_norm_mix_pre, m_w_in, m_lb_gamma_fwd, m_lb_gamma_bwd, m_hg_norm, m_sg_w, m_sg_b, m_sg_ln_g, m_sg_ln_b, m_w_a, m_w_b, m_w_out, m_norm_mix_post, m_norm_ffn_pre, m_w_gate, m_w_up, m_w_down, m_norm_ffn_post, m_w_ple, m_w_ple_gate, v_norm_mix_pre, v_w_in, v_lb_gamma_fwd, v_lb_gamma_bwd, v_hg_norm, v_sg_w, v_sg_b, v_sg_ln_g, v_sg_ln_b, v_w_a, v_w_b, v_w_out, v_norm_mix_post, v_norm_ffn_pre, v_w_gate, v_w_up, v_w_down, v_norm_ffn_post, v_w_ple, v_w_ple_gate):
    a = dict(zip(INPUTS, (x, p, norm_mix_pre, w_in, lb_gamma_fwd, lb_gamma_bwd, hg_norm, sg_w, sg_b, sg_ln_g, sg_ln_b, w_a, w_b, w_out, norm_mix_post, norm_ffn_pre, w_gate, w_up, w_down, norm_ffn_post, w_ple, w_ple_gate, loss_target, m_norm_mix_pre, m_w_in, m_lb_gamma_fwd, m_lb_gamma_bwd, m_hg_norm, m_sg_w, m_sg_b, m_sg_ln_g, m_sg_ln_b, m_w_a, m_w_b, m_w_out, m_norm_mix_post, m_norm_ffn_pre, m_w_gate, m_w_up, m_w_down, m_norm_ffn_post, m_w_ple, m_w_ple_gate, v_norm_mix_pre, v_w_in, v_lb_gamma_fwd, v_lb_gamma_bwd, v_hg_norm, v_sg_w, v_sg_b, v_sg_ln_g, v_sg_ln_b, v_w_a, v_w_b, v_w_out, v_norm_mix_post, v_norm_ffn_pre, v_w_gate, v_w_up, v_w_down, v_norm_ffn_post, v_w_ple, v_w_ple_gate)))
    m = x.shape[1]

    shards = {n: cast_pad("cast_" + n, a[n], *_shard_shape(n, a[n].shape)) for n in BIG}
    plan = DistPlan(shards)
    loss_cols, dx, gs = local_step(x[0], p[:, 0], loss_target[0], {n: a[n] for n in SMALL}, plan)
    loss = lax.psum(jnp.sum(loss_cols) * (0.5 / D), ("x", "y", "c"))

    small_shapes = [a[n].shape for n in SMALL]
    rows = plan.small_slots.shape[1]
    late = allreduce_small("allreduce_small", jnp.pad(gs["norm_mix_pre"][0:1], ((0, 7), (0, 0))))
    g_late = jnp.pad(late, ((0, rows - 8), (0, 0)))[None]

    res = {}
    row_tiles = {"w_in": 128, "w_a": 128, "w_b": 512, "w_out": 128, "w_gate": 128, "w_up": 128, "w_down": 88,
                 "w_ple": 256, "w_ple_gate": 128}
    for n in BIG:
        res[n] = adam("adam_" + n, a[n], a["m_" + n], a["v_" + n], row_tiles[n],
                      slots=[plan.slots[l][n] for l in range(DEPTH)])
    packed = [_pack([a[pre + n] for n in SMALL])[None] for pre in ("", "m_", "v_")]
    small_res = adam("adam_small", packed[0], packed[1], packed[2], rows // 2, g=g_late, slots=[plan.small_slots])
    small_res = [_unpack(r_[0], small_shapes) for r_ in small_res]
    for i, n in enumerate(SMALL):
        res[n] = tuple(small_res[k][i] for k in range(4))

    outs = [loss, dx.reshape(1, m, D)]
    for k in range(4):
        outs += [res[n][k] for n in WEIGHTS]
    return tuple(outs)
```

```python
import jax
import jax.numpy as jnp
from jax import lax
from jax.experimental import pallas as pl
from jax.experimental.pallas import tpu as pltpu

F32 = jnp.float32
BF16 = jnp.bfloat16

D = 1024
N_IN = 8192
HEADS = 8
HEAD_DIM = 128
SG_CHUNK = 128
SG_WIDTH = 512
SG_GROUP_DIM = 64
FFN = 2816
PLE_DIM = 256
EPS = 1e-6
DEPTH = 2
ZQ, ZFF, ZFB, ZI, ZG, GA, GB = 0, 1, 2, 3, 4, 6, 7
ZU, ZV = 10, 11

NDEV = 8
FFN_SHARD_PAD = 384
FFN_PAD = NDEV * FFN_SHARD_PAD

LR, B1, B2, AEPS, WD, STEP = 0.001, 0.9, 0.999, 1e-08, 0.01, 10

ROW_TILE = 512
BIG_TILE = 2048
WGRAD_TOKENS = 4096
HG_CHUNK = 64
HG_BLOCK_FWD = 256
HG_BLOCK_BWD = 256
EXP_CLAMP = 80.0
PROLOGUE_CHUNK = 256
TINY = float(jnp.finfo(jnp.float32).tiny)
VMEM_LIMIT = 56 * 1024 * 1024

BIG = ["w_in", "w_a", "w_b", "w_out", "w_gate", "w_up", "w_down", "w_ple", "w_ple_gate"]
SMALL = ["norm_mix_pre", "lb_gamma_fwd", "lb_gamma_bwd", "hg_norm", "sg_w", "sg_b", "sg_ln_g", "sg_ln_b",
         "norm_mix_post", "norm_ffn_pre", "norm_ffn_post"]
WEIGHTS = ["norm_mix_pre", "w_in", "lb_gamma_fwd", "lb_gamma_bwd", "hg_norm", "sg_w", "sg_b", "sg_ln_g", "sg_ln_b",
           "w_a", "w_b", "w_out", "norm_mix_post", "norm_ffn_pre", "w_gate", "w_up", "w_down", "norm_ffn_post",
           "w_ple", "w_ple_gate"]
INPUTS = (["x", "p"] + WEIGHTS + ["loss_target"] + ["m_" + n for n in WEIGHTS] + ["v_" + n for n in WEIGHTS])
LAYOUT = {
    "w_in": (1, 1024, "w_in", 0), "w_a": (0, 128, "w_a", 0), "w_b": (1, 128, "w_b", 0),
    "w_out": (0, 128, "w_out", 0), "w_gate": (1, FFN_SHARD_PAD, "w_gu", 0),
    "w_up": (1, FFN_SHARD_PAD, "w_gu", FFN_PAD), "w_down": (0, FFN_SHARD_PAD, "w_down", 0),
    "w_ple": (1, 128, "w_ple", 0), "w_ple_gate": (0, 128, "w_ple_gate", 0),
}
GATHERED = {"w_in": (D, N_IN), "w_a": (D, D), "w_b": (SG_WIDTH, D), "w_out": (D, D), "w_gu": (D, 2 * FFN_PAD),
            "w_down": (FFN_PAD, D), "w_ple": (PLE_DIM, D), "w_ple_gate": (D, D)}


def _params(sem):
    return pltpu.CompilerParams(dimension_semantics=sem, vmem_limit_bytes=VMEM_LIMIT)


def _dot(a, b):
    return lax.dot_general(a, b, (((1,), (0,)), ((), ())), preferred_element_type=F32)


def _dot_nt(a, b):
    return lax.dot_general(a, b, (((1,), (1,)), ((), ())), preferred_element_type=F32)


def _dot_tn(a, b):
    return lax.dot_general(a, b, (((0,), (0,)), ((), ())), preferred_element_type=F32)


def _sigmoid(x):
    return jax.nn.sigmoid(x)


def _silu(x):
    return x * _sigmoid(x)


def _silu_grad(x):
    s = _sigmoid(x)
    return s * (1.0 + x * (1.0 - s))


def _gelu(x):
    return 0.5 * x * (1.0 + lax.erf(x * 0.7071067811865476))


def _gelu_grad(x):
    return 0.5 * (1.0 + lax.erf(x * 0.7071067811865476)) + x * jnp.exp(-0.5 * x * x) * 0.3989422804014327


def _mean(x):
    return jnp.mean(x, axis=-1, keepdims=True)


def _colsum(x):
    return jnp.sum(x, axis=0, keepdims=True)


def _rms(x):
    r = lax.rsqrt(_mean(x * x) + EPS)
    return x * r, r


def _rms_bwd(dy, xh, r, g):
    dyg = dy * g
    return r * (dyg - xh * _mean(dyg * xh))


MESH = pl.DeviceIdType.MESH
ANY = pl.BlockSpec(memory_space=pl.ANY)


def _slab(ref, axis, start, size):
    idx = [slice(None)] * 2
    idx[axis] = pl.ds(start, size)
    return ref.at[tuple(idx)]


class Exchange:
    def __init__(self, srcs, dsts, items):
        self.srcs, self.dsts, self.items = list(srcs), list(dsts), list(items)

    def specs(self):
        n = len(self.items)
        sems = [pltpu.SemaphoreType.DMA((n * (NDEV - 1),)), pltpu.SemaphoreType.DMA((n * (NDEV - 1),)),
                pltpu.SemaphoreType.DMA((n,))]
        return ([ANY] * len(self.srcs), [ANY] * len(self.dsts),
                [jax.ShapeDtypeStruct(s, dt) for (s, dt) in self.dsts], sems)

    def copies(self, src, dst, send_sem, recv_sem, loc_sem):
        x, y, c = lax.axis_index("x"), lax.axis_index("y"), lax.axis_index("c")
        me = 4 * x + 2 * y + c
        starts, waits = [], []
        for n, (kind, si, di, axis, size, base, layer) in enumerate(self.items):
            def views(to_dev, from_dev):
                if kind == "gather":
                    return (src[si].at[layer],
                            _slab(dst[di], axis, base + pl.multiple_of(from_dev * size, 128), size))
                if kind == "copies":
                    return src[si], dst[di].at[from_dev]
                return _slab(src[si], axis, base + pl.multiple_of(to_dev * size, 128), size), dst[di].at[from_dev]

            s_own, d_own = views(me, me)
            own = pltpu.make_async_copy(s_own, d_own, loc_sem.at[n])
            starts.append(own)
            waits.append(own)
            for k in range(1, NDEV):
                px = 1 - x if k & 4 else x
                py = 1 - y if k & 2 else y
                pc = 1 - c if k & 1 else c
                peer = 4 * px + 2 * py + pc
                s_out, _ = views(peer, me)
                _, d_in = views(me, peer)
                sem = n * (NDEV - 1) + k - 1
                starts.append(pltpu.make_async_remote_copy(s_out, d_own, send_sem.at[sem], recv_sem.at[sem],
                                                           device_id=(px, py, pc), device_id_type=MESH))
                waits.append(pltpu.make_async_remote_copy(s_out, d_in, send_sem.at[sem], recv_sem.at[sem],
                                                          device_id=(px, py, pc), device_id_type=MESH))
        return starts, waits


def gather_two_level(name, shards, layer, axis, size, full_shape):
    def body(src, dst, send_sem, recv_sem, loc_sem):
        x, y, c = lax.axis_index("x"), lax.axis_index("y"), lax.axis_index("c")
        mine = src.at[layer]
        chips = [(1 - x, y), (x, 1 - y), (1 - x, 1 - y)]

        def slab(px, py, pc):
            return _slab(dst, axis, pl.multiple_of((4 * px + 2 * py + pc) * size, 128), size)

        def copy(k, from_ref, block, to):
            return pltpu.make_async_remote_copy(from_ref, slab(*block), send_sem.at[k], recv_sem.at[k], device_id=to,
                                                device_id_type=MESH)

        own = pltpu.make_async_copy(mine, slab(x, y, c), loc_sem)
        own.start()
        first = [copy(0, mine, (x, y, c), (x, y, 1 - c))]
        first += [copy(1 + j, mine, (x, y, c), (*chip, c)) for j, chip in enumerate(chips)]
        for cp in first:
            cp.start()
        passed = []
        for j, chip in enumerate(chips):
            copy(1 + j, mine, (*chip, c), (x, y, c)).wait_recv()
            fwd = copy(4 + j, slab(*chip, c), (*chip, c), (x, y, 1 - c))
            fwd.start()
            passed.append(fwd)
        copy(0, mine, (x, y, 1 - c), (x, y, c)).wait_recv()
        for j, chip in enumerate(chips):
            copy(4 + j, mine, (*chip, 1 - c), (x, y, c)).wait_recv()
        for cp in first + passed:
            cp.wait_send()
        own.wait()

    return pl.pallas_call(
        body, name=name, in_specs=[ANY], out_specs=ANY, out_shape=jax.ShapeDtypeStruct(full_shape, shards.dtype),
        scratch_shapes=[pltpu.SemaphoreType.DMA((NDEV - 1,)), pltpu.SemaphoreType.DMA((NDEV - 1,)),
                        pltpu.SemaphoreType.DMA(())],
        compiler_params=pltpu.CompilerParams(has_side_effects=True))(shards)


def hosted_call(body, exch, name, grid, in_specs, out_specs, out_shape, scratch_shapes, operands, semantics,
                aliases=None):
    aliases = aliases or {}
    if exch is None:
        res = pl.pallas_call(body, name=name, grid=grid, in_specs=in_specs, out_specs=out_specs, out_shape=out_shape,
                             scratch_shapes=scratch_shapes, input_output_aliases=aliases,
                             compiler_params=_params(semantics))(*operands)
        return list(res), []
    n_in, n_out, n_scr = len(in_specs), len(out_specs), len(scratch_shapes)
    e_in, e_out, e_shape, e_scr = exch.specs()
    ns, nd = len(e_in), len(e_out)

    def at_step(last):
        cond = None
        for ax, n in enumerate(grid):
            c = pl.program_id(ax) == (n - 1 if last else 0)
            cond = c if cond is None else jnp.logical_and(cond, c)
        return cond

    def wrapped(*refs):
        ins, src = refs[:n_in], refs[n_in:n_in + ns]
        o0 = n_in + ns
        outs, dst = refs[o0:o0 + n_out], refs[o0 + n_out:o0 + n_out + nd]
        s0 = o0 + n_out + nd
        scr, sems = refs[s0:s0 + n_scr], refs[s0 + n_scr:]

        @pl.when(at_step(False))
        def _():
            for cp in exch.copies(src, dst, *sems)[0]:
                cp.start()

        body(*ins, *outs, *scr)

        @pl.when(at_step(True))
        def _():
            for cp in exch.copies(src, dst, *sems)[1]:
                cp.wait()

    res = pl.pallas_call(
        wrapped, name=name, grid=grid, in_specs=list(in_specs) + e_in, out_specs=list(out_specs) + e_out,
        out_shape=list(out_shape) + e_shape, scratch_shapes=list(scratch_shapes) + e_scr,
        input_output_aliases=aliases,
        compiler_params=pltpu.CompilerParams(dimension_semantics=("arbitrary",) * len(grid),
                                             vmem_limit_bytes=VMEM_LIMIT, has_side_effects=True),
    )(*operands, *exch.srcs)
    return list(res[:n_out]), list(res[n_out:])


def allreduce_small(name, part):
    rows, width = part.shape

    def body(p_ref, o_ref, buf, send_sem, recv_sem):
        x, y, c = lax.axis_index("x"), lax.axis_index("y"), lax.axis_index("c")
        me = 4 * x + 2 * y + c
        buf[me] = p_ref[...]
        waits = []
        for k in range(1, NDEV):
            px = 1 - x if k & 4 else x
            py = 1 - y if k & 2 else y
            pc = 1 - c if k & 1 else c
            peer = 4 * px + 2 * py + pc
            pltpu.make_async_remote_copy(p_ref, buf.at[me], send_sem.at[k - 1], recv_sem.at[k - 1],
                                         device_id=(px, py, pc), device_id_type=MESH).start()
            waits.append(pltpu.make_async_remote_copy(p_ref, buf.at[peer], send_sem.at[k - 1], recv_sem.at[k - 1],
                                                      device_id=(px, py, pc), device_id_type=MESH))
        for w in waits:
            w.wait()
        acc = buf[0]
        for j in range(1, NDEV):
            acc = acc + buf[j]
        o_ref[...] = acc

    vmem = pl.BlockSpec(memory_space=pltpu.VMEM)
    return pl.pallas_call(
        body, name=name, in_specs=[vmem], out_specs=vmem, out_shape=jax.ShapeDtypeStruct((rows, width), F32),
        scratch_shapes=[pltpu.VMEM((NDEV, rows, width), F32), pltpu.SemaphoreType.DMA((NDEV - 1,)),
                        pltpu.SemaphoreType.DMA((NDEV - 1,))],
        compiler_params=pltpu.CompilerParams(vmem_limit_bytes=VMEM_LIMIT, has_side_effects=True),
    )(part)


def rowwise(name, fn, m, ins=(), consts=(), outs=(), alias_outs=(), accs=(), tm=ROW_TILE):
    tm = min(tm, m)
    n_in, n_c, n_o, n_al, n_ac = len(ins), len(consts), len(outs), len(alias_outs), len(accs)
    held = [a for (a, _, _) in alias_outs if not isinstance(a, jax.ShapeDtypeStruct)]
    n_held = len(held)

    def body(*refs):
        in_refs = refs[:n_in + n_c]
        out_refs = refs[n_in + n_c + n_held:]
        vals = fn(*[r[...] for r in in_refs])
        if not isinstance(vals, (tuple, list)):
            vals = (vals,)
        for r, v in zip(out_refs[:n_o + n_al], vals[:n_o + n_al]):
            r[...] = v.astype(r.dtype)
        if n_ac:
            acc_refs = out_refs[n_o + n_al:]

            @pl.when(pl.program_id(0) == 0)
            def _():
                for r in acc_refs:
                    r[...] = jnp.zeros(r.shape, F32)

            for r, v in zip(acc_refs, vals[n_o + n_al:]):
                r[...] += v

    def col(cb):
        return lambda i: (i, cb)

    in_specs = [pl.BlockSpec((tm, w), col(cb)) for (_, w, cb) in ins]
    in_specs += [pl.BlockSpec(c.shape, lambda i, nd=c.ndim: (0,) * nd) for c in consts]
    in_specs += [ANY for _ in held]
    out_shape = [jax.ShapeDtypeStruct((m, w), dt) for (w, dt) in outs]
    out_specs = [pl.BlockSpec((tm, w), col(0)) for (w, _) in outs]
    out_shape += [jax.ShapeDtypeStruct(a.shape, a.dtype) for (a, _, _) in alias_outs]
    out_specs += [pl.BlockSpec((tm, w), col(cb)) for (_, w, cb) in alias_outs]
    out_shape += [jax.ShapeDtypeStruct(s, F32) for s in accs]
    out_specs += [pl.BlockSpec(s, lambda i: (0, 0)) for s in accs]
    aliases, k_in = {}, n_in + n_c
    for k, (a, _, _) in enumerate(alias_outs):
        if not isinstance(a, jax.ShapeDtypeStruct):
            aliases[k_in] = n_o + k
            k_in += 1
    return pl.pallas_call(
        body, name=name, grid=(m // tm,), in_specs=in_specs, out_specs=out_specs, out_shape=out_shape,
        input_output_aliases=aliases,
        compiler_params=_params(("arbitrary",) if n_ac else ("parallel",)),
    )(*[a for (a, _, _) in ins], *consts, *held)


def _operand(arr, bshape, imap):
    if isinstance(arr, tuple):
        arr, lead = arr
        return arr, pl.BlockSpec((None,) + bshape, lambda *g: (lead,) + imap(*g))
    return arr, pl.BlockSpec(bshape, imap)


def _shape2(arr):
    return arr[0].shape[1:] if isinstance(arr, tuple) else arr.shape


def mm(name, a, b, mode, out_dtype=F32, tm=1024, tn=1024, tk=1024, exch=None):
    sa, sb = _shape2(a), _shape2(b)
    if mode == "nn":
        (M, K), N = sa, sb[1]
    elif mode == "nt":
        (M, K), N = sa, sb[0]
    else:
        (K, M), N = sa, sb[1]
    tm, tn, tk = min(tm, M), min(tn, N), min(tk, K)
    assert M % tm == 0 and N % tn == 0 and K % tk == 0, (name, M, N, K)
    nk = K // tk
    if mode == "nn":
        a_arr, a_spec = _operand(a, (tm, tk), lambda i, j, k: (i, k))
        b_arr, b_spec = _operand(b, (tk, tn), lambda i, j, k: (k, j))
        dot = _dot
    elif mode == "nt":
        a_arr, a_spec = _operand(a, (tm, tk), lambda i, j, k: (i, k))
        b_arr, b_spec = _operand(b, (tn, tk), lambda i, j, k: (j, k))
        dot = _dot_nt
    else:
        a_arr, a_spec = _operand(a, (tk, tm), lambda i, j, k: (k, i))
        b_arr, b_spec = _operand(b, (tk, tn), lambda i, j, k: (k, j))
        dot = _dot_tn

    def body(a_ref, b_ref, o_ref, *acc):
        part = dot(a_ref[...].astype(BF16), b_ref[...].astype(BF16))
        if nk == 1:
            o_ref[...] = part.astype(o_ref.dtype)
            return
        acc_ref, k = acc[0], pl.program_id(2)

        @pl.when(k == 0)
        def _():
            acc_ref[...] = part

        @pl.when(k > 0)
        def _():
            acc_ref[...] += part

        @pl.when(k == nk - 1)
        def _():
            o_ref[...] = acc_ref[...].astype(o_ref.dtype)

    outs, extra = hosted_call(
        body, exch, name, (M // tm, N // tn, nk), [a_spec, b_spec], [pl.BlockSpec((tm, tn), lambda i, j, k: (i, j))],
        [jax.ShapeDtypeStruct((M, N), out_dtype)], [pltpu.VMEM((tm, tn), F32)] if nk > 1 else [], [a_arr, b_arr],
        ("parallel", "parallel", "arbitrary"))
    return outs[0] if exch is None else (outs[0], extra)


def mm_fused(name, m, a_ins, bs, mode, kdim, prologue=None, a_outs=(), e_ins=(), consts=(), epilogue=None, outs=(),
             alias_outs=(), accs=(), a_to_epilogue=(), a_consts=(), tm=512, tk=1024, resident=False, exch=None):
    tm = min(tm, m)
    nk = kdim // tk
    assert nk == 1 or not a_to_epilogue
    n = bs[0][0].shape[1 if mode == "nn" else 0]
    b_arrays = []
    for b_, _ in bs:
        if not (resident and any(b_ is u for u in b_arrays)):
            b_arrays.append(b_)
    b_of_pair = [next(j for j, u in enumerate(b_arrays) if u is b_) if resident else j for j, (b_, _) in enumerate(bs)]
    n_a, n_b, n_e, n_c = len(a_ins) + len(a_consts), len(b_arrays), len(e_ins), len(consts)
    n_ao, n_o, n_al, n_ac = len(a_outs), len(outs), len(alias_outs), len(accs)
    held = [a for (a, _, _) in alias_outs if not isinstance(a, jax.ShapeDtypeStruct)]
    dot = _dot if mode == "nn" else _dot_nt

    def body(*refs):
        a_refs, b_refs = refs[:n_a], refs[n_a:n_a + n_b]
        e_refs = refs[n_a + n_b:n_a + n_b + n_e + n_c]
        o0 = n_a + n_b + n_e + n_c + len(held)
        ao_refs = refs[o0:o0 + n_ao]
        out_refs = refs[o0 + n_ao:o0 + n_ao + n_o + n_al]
        acc_refs = refs[o0 + n_ao + n_o + n_al:o0 + n_ao + n_o + n_al + n_ac]
        scr = refs[o0 + n_ao + n_o + n_al + n_ac:]
        i, k = pl.program_id(0), pl.program_id(1)
        ck = tk if prologue is None else min(tk, PROLOGUE_CHUNK)
        part = None
        for c0 in range(0, tk, ck):
            cols = slice(c0, c0 + ck)
            tiles = [r[:, cols] for r in a_refs]
            a_list, extra = (tiles, []) if prologue is None else prologue(*tiles)
            for r, v in zip(ao_refs, extra):
                r[:, cols] = v.astype(r.dtype)
            for a, j_b, (_, off) in zip(a_list, b_of_pair, bs):
                b_ref = b_refs[j_b]
                if resident:
                    b = b_ref[pl.ds(pl.multiple_of((k + off) * tk + c0, ck), ck), :]
                else:
                    b = b_ref[cols, :] if mode == "nn" else b_ref[:, cols]
                prod = dot(a.astype(BF16), b.astype(BF16))
                part = prod if part is None else part + prod

        def finish(total):
            vals = epilogue(total, *[a_refs[j][...] for j in a_to_epilogue], *[r[...] for r in e_refs])
            if not isinstance(vals, (tuple, list)):
                vals = (vals,)
            for r, v in zip(out_refs, vals[:n_o + n_al]):
                r[...] = v.astype(r.dtype)
            for r, v in zip(acc_refs, vals[n_o + n_al:]):
                @pl.when(i == 0)
                def _():
                    r[...] = v

                @pl.when(i > 0)
                def _():
                    r[...] += v

        if nk == 1:
            finish(part)
            return
        acc_ref = scr[0]

        @pl.when(k == 0)
        def _():
            acc_ref[...] = part

        @pl.when(k > 0)
        def _():
            acc_ref[...] += part

        @pl.when(k == nk - 1)
        def _():
            finish(acc_ref[...])

    in_specs = [pl.BlockSpec((tm, tk), lambda i, k, off=off: (i, k + off)) for (_, off) in a_ins]
    in_specs += [pl.BlockSpec((1, tk), lambda i, k: (0, k)) for _ in a_consts]
    if resident:
        assert mode == "nn"
        in_specs += [pl.BlockSpec(b.shape, lambda i, k: (0, 0), pipeline_mode=pl.Buffered(1)) for b in b_arrays]
    elif mode == "nn":
        in_specs += [pl.BlockSpec((tk, n), lambda i, k, off=off: (k + off, 0)) for (_, off) in bs]
    else:
        in_specs += [pl.BlockSpec((n, tk), lambda i, k, off=off: (0, k + off)) for (_, off) in bs]
    in_specs += [pl.BlockSpec((tm, w), lambda i, k, cb=cb: (i, cb)) for (_, w, cb) in e_ins]
    in_specs += [pl.BlockSpec(c.shape, lambda i, k, nd=c.ndim: (0,) * nd) for c in consts]
    in_specs += [ANY for _ in held]
    out_shape = [jax.ShapeDtypeStruct((m, kdim), dt) for dt in a_outs]
    out_specs = [pl.BlockSpec((tm, tk), lambda i, k: (i, k)) for _ in a_outs]
    out_shape += [jax.ShapeDtypeStruct((m, w), dt) for (w, dt) in outs]
    out_specs += [pl.BlockSpec((tm, w), lambda i, k: (i, 0)) for (w, _) in outs]
    out_shape += [jax.ShapeDtypeStruct(a.shape, a.dtype) for (a, _, _) in alias_outs]
    out_specs += [pl.BlockSpec((tm, w), lambda i, k, cb=cb: (i, cb)) for (_, w, cb) in alias_outs]
    out_shape += [jax.ShapeDtypeStruct(s_, F32) for s_ in accs]
    out_specs += [pl.BlockSpec(s_, lambda i, k: (0, 0)) for s_ in accs]
    aliases, k_in = {}, n_a + n_b + n_e + n_c
    for j, (a, _, _) in enumerate(alias_outs):
        if not isinstance(a, jax.ShapeDtypeStruct):
            aliases[k_in] = n_ao + n_o + j
            k_in += 1
    operands = [a for (a, _) in a_ins] + list(a_consts) + b_arrays + [a for (a, _, _) in e_ins] + list(consts) + held
    res, extra = hosted_call(
        body, exch, name, (m // tm, nk), in_specs, out_specs, out_shape,
        [pltpu.VMEM((tm, n), F32)] if nk > 1 else [], operands,
        ("arbitrary" if n_ac else "parallel", "arbitrary"), aliases)
    return res if exch is None else (res, extra)


def _cumsum_rows(x):
    n = x.shape[0]
    row = lax.broadcasted_iota(jnp.int32, x.shape, 0)
    s = 1
    while s < n:
        x = x + jnp.where(row >= s, pltpu.roll(x, s, 0), 0.0)
        s *= 2
    return x


def _hg_prep(zq, zf, lb, reverse, b=None):
    n = zq.shape[0]
    q = _silu(zq)
    sig = _sigmoid(zf)
    sn = 1.0 - sig
    f = lb + (1.0 - lb) * sig
    k = (1.0 - lb) * sn
    if b is None:
        g = jnp.log(jnp.maximum(f, TINY))
        b = _cumsum_rows(g)
        if reverse:
            b = b[n - 1:n] - b + g
    b_last = b[0:1] if reverse else b[n - 1:n]
    b_ref = b[n // 2:n // 2 + 1]
    e1 = jnp.exp(b)
    e2 = jnp.exp(jnp.clip(b - b_ref, -EXP_CLAMP, EXP_CLAMP))
    e3 = jnp.exp(jnp.clip(b_ref - b, -EXP_CLAMP, EXP_CLAMP))
    e4 = jnp.exp(b_last - b)
    return dict(q=q, k=k, sig=sig, sn=sn, f=f, b=b, e1=e1, e2=e2, e3=e3, e4=e4, e_last=jnp.exp(b_last),
                qe=(q * e1).astype(BF16), qt=(q * e2).astype(BF16), kt=(k * e3).astype(BF16),
                ks=(k * e4).astype(BF16))


def _hg_mask(n, reverse):
    t = lax.broadcasted_iota(jnp.int32, (n, n), 0)
    s = lax.broadcasted_iota(jnp.int32, (n, n), 1)
    return (s >= t) if reverse else (s <= t)


def hgrn_fwd(name, z, lb_f, lb_b, exch=None):
    m = z.shape[0]
    C, T = HG_CHUNK, min(HG_BLOCK_FWD, m)
    nb, cpb = m // T, T // C

    def body(zq_f, zf_f, zi_f, zq_b, zf_b, zi_b, lbf_ref, lbb_ref, of_ref, ob_ref, sf_ref, sb_ref, bf_ref, bb_ref,
             st_ref):
        @pl.when(pl.program_id(0) == 0)
        def _():
            st_ref[...] = jnp.zeros(st_ref.shape, F32)

        dirs = ((zq_f, zf_f, zi_f, lbf_ref, of_ref, sf_ref), (zq_b, zf_b, zi_b, lbb_ref, ob_ref, sb_ref))
        b_refs = (bf_ref, bb_ref)

        def chunk(ci, carry):
            work = []
            for d, (zq, zf, zi, lb_ref, o_ref, s_ref) in enumerate(dirs):
                cc = ci if d == 0 else cpb - 1 - ci
                rows = pl.ds(pl.multiple_of(cc * C, C), C)
                pre = _hg_prep(zq[rows, :], zf[rows, :], lb_ref[...], d == 1)
                v = zi[rows, :].astype(BF16)
                work.append((cc, rows, pre, v, [st_ref[d, h] for h in range(HEADS)]))
            heads = [(d, h, slice(h * HEAD_DIM, (h + 1) * HEAD_DIM)) for d in range(2) for h in range(HEADS)]
            first = {}
            for d, h, sl in heads:
                _, _, pre, v, sts = work[d]
                first[d, h] = (_dot_nt(pre["qt"][:, sl], pre["kt"][:, sl]),
                               _dot_nt(pre["qe"][:, sl], sts[h].astype(BF16)),
                               _dot_tn(v[:, sl], pre["ks"][:, sl]))
            results = [([], []), ([], [])]
            for d, h, sl in heads:
                _, _, pre, v, sts = work[d]
                scores, o_inter, st_add = first[d, h]
                a = jnp.where(_hg_mask(C, d == 1), scores, 0.0).astype(BF16)
                results[d][0].append(o_inter + _dot(a, v[:, sl]))
                results[d][1].append(sts[h] * pre["e_last"][:, sl] + st_add)
            results = [(jnp.concatenate(o_parts, axis=1), new_sts) for (o_parts, new_sts) in results]
            for d, (zq, zf, zi, lb_ref, o_ref, s_ref) in enumerate(dirs):
                cc, rows, pre, _, sts = work[d]
                o_ref[rows, :] = results[d][0]
                b_refs[d][rows, :] = pre["b"]
                for h in range(HEADS):
                    s_ref[cc, h] = sts[h]
                    st_ref[d, h] = results[d][1][h]
            return carry

        lax.fori_loop(0, cpb, chunk, 0)

    def zspec(cb, rev):
        return pl.BlockSpec((T, D), (lambda i: (nb - 1 - i, cb)) if rev else (lambda i: (i, cb)))

    def sspec(rev):
        shape = (cpb, HEADS, HEAD_DIM, HEAD_DIM)
        return pl.BlockSpec(shape, (lambda i: (nb - 1 - i, 0, 0, 0)) if rev else (lambda i: (i, 0, 0, 0)))

    lbspec = pl.BlockSpec((1, D), lambda i: (0, 0))
    states = jax.ShapeDtypeStruct((m // C, HEADS, HEAD_DIM, HEAD_DIM), F32)
    outs, extra = hosted_call(
        body, exch, name, (nb,),
        [zspec(ZQ, False), zspec(ZFF, False), zspec(ZI, False), zspec(ZQ, True), zspec(ZFB, True), zspec(ZI, True),
         lbspec, lbspec],
        [zspec(0, False), zspec(0, True), sspec(False), sspec(True), zspec(0, False), zspec(0, True)],
        [jax.ShapeDtypeStruct((m, D), F32), jax.ShapeDtypeStruct((m, D), F32), states, states,
         jax.ShapeDtypeStruct((m, D), F32), jax.ShapeDtypeStruct((m, D), F32)],
        [pltpu.VMEM((2, HEADS, HEAD_DIM, HEAD_DIM), F32)], [z, z, z, z, z, z, lb_f, lb_b], ("arbitrary",))
    return outs, extra


def hgrn_bwd(name, z, d_o, s_f, s_b, b_f, b_b, lb_f, lb_b, exch=None):
    m = z.shape[0]
    C, T = HG_CHUNK, min(HG_BLOCK_BWD, m)
    nb, cpb = m // T, T // C

    def body(zq_f, zf_f, zi_f, do_f, sf_ref, zq_b, zf_b, zi_b, do_b, sb_ref, lbf_ref, lbb_ref, bf_ref, bb_ref,
             dqf_ref, dvf_ref, dqb_ref, dvb_ref, dzf_f, dzf_b, dlbf_ref, dlbb_ref,
             dst_ref):
        b_refs = (bf_ref, bb_ref)
        @pl.when(pl.program_id(0) == 0)
        def _():
            dst_ref[...] = jnp.zeros(dst_ref.shape, F32)
            dlbf_ref[...] = jnp.zeros(dlbf_ref.shape, F32)
            dlbb_ref[...] = jnp.zeros(dlbb_ref.shape, F32)

        dirs = ((zq_f, zf_f, zi_f, do_f, sf_ref, lbf_ref, dqf_ref, dvf_ref, dzf_f, dlbf_ref),
                (zq_b, zf_b, zi_b, do_b, sb_ref, lbb_ref, dqb_ref, dvb_ref, dzf_b, dlbb_ref))

        def chunk(ci, carry):
            work = []
            for d, (zq, zf, zi, do_ref, s_ref, lb_ref, dq_ref, dv_ref, dzf_ref, dlb_ref) in enumerate(dirs):
                cc = cpb - 1 - ci if d == 0 else ci
                rows = pl.ds(pl.multiple_of(cc * C, C), C)
                lb = lb_ref[...]
                pre = _hg_prep(zq[rows, :], zf[rows, :], lb, d == 1, b=b_refs[d][rows, :])
                work.append((rows, lb, pre, zi[rows, :].astype(BF16), do_ref[rows, :],
                             [s_ref[cc, h] for h in range(HEADS)], [dst_ref[d, h] for h in range(HEADS)],
                             dlb_ref[...]))
            heads = [(d, h, slice(h * HEAD_DIM, (h + 1) * HEAD_DIM)) for d in range(2) for h in range(HEADS)]
            first = {}
            for d, h, sl in heads:
                _, _, pre, v, do, st_prevs, dsts, _ = work[d]
                dst16 = dsts[h].astype(BF16)
                first[d, h] = (_dot_nt(pre["qt"][:, sl], pre["kt"][:, sl]),
                               _dot_nt(do[:, sl], v[:, sl]),
                               _dot(do[:, sl], st_prevs[h].astype(BF16)),
                               _dot(v[:, sl], dst16),
                               _dot_nt(pre["ks"][:, sl], dst16),
                               _dot_tn(do[:, sl], pre["qe"][:, sl]))
            parts = [[[] for _ in range(6)] for _ in range(2)]
            for d, h, sl in heads:
                _, _, pre, v, do, st_prevs, dsts, _ = work[d]
                scores, dscores, dq_inter, dk_state, dv_state, dst_add = first[d, h]
                mask = _hg_mask(C, d == 1)
                a = jnp.where(mask, scores, 0.0).astype(BF16)
                da = jnp.where(mask, dscores, 0.0).astype(BF16)
                dq_p, dki_p, dks_p, dv_p, rr_p, new_dsts = parts[d]
                dq_p.append(_dot(da, pre["kt"][:, sl]) * pre["e2"][:, sl] + dq_inter * pre["e1"][:, sl])
                dki_p.append(_dot_tn(da, pre["qt"][:, sl]) * pre["e3"][:, sl])
                dks_p.append(dk_state * pre["e4"][:, sl])
                dv_p.append(_dot_tn(a, do[:, sl]) + dv_state)
                rr_p.append(pre["e_last"][:, sl] * _colsum(dsts[h] * st_prevs[h]))
                new_dsts.append(dsts[h] * pre["e_last"][:, sl] + dst_add)
            results = []
            for d, (rows, lb, pre, v, do, st_prevs, dsts, dlb_old) in enumerate(work):
                rev = d == 1
                dq_p, dki_p, dks_p, dv_p, rr_p, new_dsts = parts[d]
                dq, dki, dks, dv, rr = (jnp.concatenate(p_, axis=1) for p_ in (dq_p, dki_p, dks_p, dv_p, rr_p))
                x = pre["q"] * dq - pre["k"] * dki
                y = pre["k"] * dks
                if rev:
                    dg = _cumsum_rows(x - y) + _colsum(y) + rr
                else:
                    dg = _cumsum_rows(y - x) + (x - y) + _colsum(x) + rr
                inv_f = jnp.where(pre["f"] > TINY, 1.0 / pre["f"], 0.0)
                u = dg * inv_f - (dki + dks)
                results.append((dq, dv, (1.0 - lb) * pre["sig"] * pre["sn"] * u, dlb_old + _colsum(pre["sn"] * u),
                                new_dsts))
            for d, (zq, zf, zi, do_ref, s_ref, lb_ref, dq_ref, dv_ref, dzf_ref, dlb_ref) in enumerate(dirs):
                rows = work[d][0]
                dq, dv, dzf, dlb, new_dsts = results[d]
                dq_ref[rows, :] = dq.astype(dq_ref.dtype)
                dv_ref[rows, :] = dv.astype(dv_ref.dtype)
                dzf_ref[rows, :] = dzf.astype(dzf_ref.dtype)
                dlb_ref[...] = dlb
                for h in range(HEADS):
                    dst_ref[d, h] = new_dsts[h]
            return carry

        lax.fori_loop(0, cpb, chunk, 0)

    def rspec(cb, rev):
        return pl.BlockSpec((T, D), (lambda i: (i, cb)) if rev else (lambda i: (nb - 1 - i, cb)))

    def sspec(rev):
        shape = (cpb, HEADS, HEAD_DIM, HEAD_DIM)
        return pl.BlockSpec(shape, (lambda i: (i, 0, 0, 0)) if rev else (lambda i: (nb - 1 - i, 0, 0, 0)))

    lbspec = pl.BlockSpec((1, D), lambda i: (0, 0))
    half = jax.ShapeDtypeStruct((m, D), BF16)
    row = jax.ShapeDtypeStruct((1, D), F32)
    outs, extra = hosted_call(
        body, exch, name, (nb,),
        [rspec(ZQ, False), rspec(ZFF, False), rspec(ZI, False), rspec(0, False), sspec(False),
         rspec(ZQ, True), rspec(ZFB, True), rspec(ZI, True), rspec(0, True), sspec(True), lbspec, lbspec,
         rspec(0, False), rspec(0, True)],
        [rspec(0, False), rspec(0, False), rspec(0, True), rspec(0, True), rspec(0, False), rspec(0, True),
         lbspec, lbspec],
        [half, half, half, half, half, half, row, row],
        [pltpu.VMEM((2, HEADS, HEAD_DIM, HEAD_DIM), F32)],
        [z, z, z, d_o, s_f, z, z, z, d_o, s_b, lb_f, lb_b, b_f, b_b], ("arbitrary",))
    return outs, extra


def _heads(fn, *arrs):
    res = [fn(*[a[:, h * HEAD_DIM:(h + 1) * HEAD_DIM] for a in arrs]) for h in range(arrs[0].shape[1] // HEAD_DIM)]
    return [jnp.concatenate(parts, axis=1) for parts in zip(*res)]


def _hg_post(o_f, o_b, zg, g):
    def head(of, ob, zgh, gh):
        on, _ = _rms(of + ob)
        return (on * gh * _silu(zgh),)
    return _heads(head, o_f, o_b, zg, g)[0]


def _hg_post_bwd(da, o_f, o_b, zg, g):
    def head(dah, of, ob, zgh, gh):
        on, r = _rms(of + ob)
        sg = _silu(zgh)
        d_on = dah * sg
        return _rms_bwd(d_on, on, r, gh), dah * on * gh * _silu_grad(zgh), d_on * on
    d_o, dzg, dg = _heads(head, da, o_f, o_b, zg, g)
    return d_o, dzg, _colsum(dg)


def _sg_parts(zv, ln_g, ln_b):
    vg = _gelu(zv)
    xc = vg - _mean(vg)
    rstd = lax.rsqrt(_mean(xc * xc) + EPS)
    vh = xc * rstd
    return vh, rstd, vh * ln_g + ln_b


def _sg_lane_group(shape):
    return lax.broadcasted_iota(jnp.int32, shape, 1) < SG_GROUP_DIM


def _sg_mix(w, v16, transpose):
    rows = v16.shape[0]
    out = []
    for c in range(rows // SG_CHUNK):
        parts = []
        for j in range(SG_WIDTH // 128):
            vj = v16[c * SG_CHUNK:(c + 1) * SG_CHUNK, j * 128:(j + 1) * 128]
            w0 = w[(2 * j) * SG_CHUNK:(2 * j + 1) * SG_CHUNK]
            w1 = w[(2 * j + 1) * SG_CHUNK:(2 * j + 2) * SG_CHUNK]
            dot = _dot_tn if transpose else _dot
            parts.append(jnp.where(_sg_lane_group((SG_CHUNK, 128)), dot(w0, vj), dot(w1, vj)))
        out.append(jnp.concatenate(parts, axis=1))
    return jnp.concatenate(out, axis=0)


def _sg_fwd(zu, zv, w, bias, ln_g, ln_b):
    _, _, v = _sg_parts(zv, ln_g, ln_b)
    reps = zu.shape[0] // SG_CHUNK
    return _gelu(zu) * (_sg_mix(w, v.astype(BF16), False) + jnp.concatenate([bias] * reps, axis=0))


def _sg_bwd(db, zu, zv, w, bias, ln_g, ln_b):
    vh, rstd, v = _sg_parts(zv, ln_g, ln_b)
    v16 = v.astype(BF16)
    reps = zu.shape[0] // SG_CHUNK
    sg = _sg_mix(w, v16, False) + jnp.concatenate([bias] * reps, axis=0)
    dzu = db * sg * _gelu_grad(zu)
    dsg = db * _gelu(zu)
    dsg16 = dsg.astype(BF16)
    dv = _sg_mix(w, dsg16, True)
    low = _sg_lane_group((SG_CHUNK, 128))
    dw = []
    for g in range(SG_WIDTH // SG_GROUP_DIM):
        j, keep = g // 2, (low if g % 2 == 0 else jnp.logical_not(low))
        acc = jnp.zeros((SG_CHUNK, SG_CHUNK), F32)
        for c in range(reps):
            rows = slice(c * SG_CHUNK, (c + 1) * SG_CHUNK)
            dj = jnp.where(keep, dsg16[rows, j * 128:(j + 1) * 128], jnp.zeros((), BF16))
            acc = acc + _dot_nt(dj, v16[rows, j * 128:(j + 1) * 128])
        dw.append(acc)
    dbias = sum(dsg[c * SG_CHUNK:(c + 1) * SG_CHUNK] for c in range(reps))
    dvh = dv * ln_g
    dvg = rstd * (dvh - _mean(dvh) - vh * _mean(dvh * vh))
    dzuv = jnp.concatenate([dzu, dvg * _gelu_grad(zv)], axis=1)
    return (dzuv, jnp.concatenate(dw, axis=0), dbias, _colsum(dv * vh), _colsum(dv))


def lower_bounds(name, gamma_f, gamma_b):
    def body(gf_ref, gb_ref, lf_ref, lb_ref):
        for g_ref, o_ref in ((gf_ref, lf_ref), (gb_ref, lb_ref)):
            g0, g1 = g_ref[0:1, :], g_ref[1:2, :]
            mx = jnp.maximum(g0, g1)
            e0, e1 = jnp.exp(g0 - mx), jnp.exp(g1 - mx)
            sm0, sm1 = e0 / (e0 + e1), e1 / (e0 + e1)
            o_ref[0:1, :] = sm0 - sm0
            o_ref[1:2, :] = (sm0 + sm1) - sm0
    shp = jax.ShapeDtypeStruct(gamma_f.shape, F32)
    return pl.pallas_call(body, name=name, out_shape=[shp, shp])(gamma_f, gamma_b)


def lower_bounds_bwd(name, gamma_f, gamma_b, dlb_f, dlb_b):
    def body(gf_ref, gb_ref, df_ref, db_ref, of_ref, ob_ref):
        for g_ref, d_ref, o_ref in ((gf_ref, df_ref, of_ref), (gb_ref, db_ref, ob_ref)):
            g0, g1 = g_ref[0:1, :], g_ref[1:2, :]
            mx = jnp.maximum(g0, g1)
            e0, e1 = jnp.exp(g0 - mx), jnp.exp(g1 - mx)
            sm0, sm1 = e0 / (e0 + e1), e1 / (e0 + e1)
            d1 = d_ref[1:2, :] * sm0 * sm1
            o_ref[0:1, :] = -d1
            o_ref[1:2, :] = d1
    shp = jax.ShapeDtypeStruct(gamma_f.shape, F32)
    return pl.pallas_call(body, name=name, out_shape=[shp, shp])(gamma_f, gamma_b, dlb_f, dlb_b)


def _row(a, l):
    return a[l:l + 1]


class LocalPlan:
    def __init__(self, weights):
        self.W = weights
        self.grads = [dict() for _ in range(DEPTH)]

    def exch(self, host):
        return None

    def done(self, host, outs):
        pass

    def early_small(self, packed):
        pass


def local_step(x, p, target, S, plan):
    m = x.shape[0]

    def hmm(tag, *args, **kw):
        ex = plan.exch(tag)
        res = mm(tag, *args, exch=ex, **kw)
        if ex is None:
            return res
        plan.done(tag, res[1])
        return res[0]

    lb_f, lb_b = lower_bounds("lower_bounds", S["lb_gamma_fwd"], S["lb_gamma_bwd"])
    saved = []
    for l in range(DEPTH):
        t = f"l{l}_"
        W = plan.W[l]
        g_pre, g_post = _row(S["norm_mix_pre"], l), _row(S["norm_mix_post"], l)
        g_fpre, g_fpost = _row(S["norm_ffn_pre"], l), _row(S["norm_ffn_post"], l)
        hg_g = _row(S["hg_norm"], l)
        sg_w = S["sg_w"][l].reshape(SG_WIDTH // SG_GROUP_DIM * SG_CHUNK, SG_CHUNK).astype(BF16)
        sg_bias = jnp.repeat(S["sg_b"][l].T, SG_GROUP_DIM, axis=1)
        ln_g, ln_b = _row(S["sg_ln_g"], l), _row(S["sg_ln_b"], l)
        lbf, lbb = _row(lb_f, l), _row(lb_b, l)

        if l == 0:
            (h,) = rowwise(t + "pre_norm", lambda xv, g: (_rms(xv)[0] * g,), m, ins=[(x, D, 0)], consts=[g_pre],
                           outs=[(D, BF16)])
        z = hmm(t + "in_proj", h, W["w_in"], "nn", tm=BIG_TILE // 2, tn=BIG_TILE)
        (o_f, o_b, s_f, s_b, b_f, b_b), extra = hgrn_fwd(t + "hgrn_fwd", z, lbf, lbb, exch=plan.exch(t + "hgrn_fwd"))
        plan.done(t + "hgrn_fwd", extra)
        (b_out,) = rowwise(t + "sgu_fwd", _sg_fwd, m, ins=[(z, SG_WIDTH, ZU), (z, SG_WIDTH, ZV)],
                           consts=[sg_w, sg_bias, ln_g, ln_b], outs=[(SG_WIDTH, BF16)])

        def post_pro(of, ob, zg, g):
            ao = _hg_post(of, ob, zg, g).astype(BF16)
            return [ao], [ao]
        a_out, pa = mm_fused(t + "proj_a", m, [(o_f, 0), (o_b, 0), (z, ZG)], [(W["w_a"], 0)], "nn", D,
                             prologue=post_pro, a_outs=[BF16], a_consts=[hg_g], epilogue=lambda tot: (tot,),
                             outs=[(D, BF16)], tm=1024)
        pb = mm(t + "proj_b", b_out, W["w_b"], "nn", BF16)

        def merge_pro(a, b, ga, gb):
            mg = (_sigmoid(ga) * a + _sigmoid(gb) * b).astype(BF16)
            return [mg], [mg]

        def post_pre(mixv, xv, gp, gf):
            x1 = xv + _rms(mixv)[0] * gp
            return mixv, x1, _rms(x1)[0] * gf
        merged, mix, x1, h2 = mm_fused(
            t + "out_proj", m, [(pa, 0), (pb, 0), (z, GA), (z, GB)], [(W["w_out"], 0)], "nn", D, prologue=merge_pro,
            a_outs=[BF16], e_ins=[(x, D, 0)], consts=[g_post, g_fpre], epilogue=post_pre,
            outs=[(D, F32), (D, F32), (D, BF16)])
        gu = hmm(t + "ffn_in", h2, W["w_gu"], "nn", BF16, tm=BIG_TILE, tn=BIG_TILE)

        def act_pro(gt, up):
            hd = (_silu(gt.astype(F32)) * up).astype(BF16)
            return [hd], [hd]
        hid, ff, x2 = mm_fused(
            t + "ffn_out", m, [(gu, 0), (gu, FFN_PAD // 1024)], [(W["w_down"], 0)], "nn", FFN_PAD, prologue=act_pro,
            a_outs=[BF16], e_ins=[(x1, D, 0)], consts=[g_fpost],
            epilogue=lambda f, xv, g: (f, xv + _rms(f)[0] * g), outs=[(D, F32), (D, F32)], tm=1024, resident=True)
        e = mm(t + "ple_proj", (p, l), W["w_ple"], "nn")

        if l + 1 < DEPTH:
            def ple_add(tv, xv, ev, g):
                x3 = xv + ev * _sigmoid(tv)
                return tv, x3, _rms(x3)[0] * g
            tg, x3, h_next = mm_fused(
                t + "ple_gate", m, [(x2, 0)], [(W["w_ple_gate"], 0)], "nn", D, a_to_epilogue=(0,), e_ins=[(e, D, 0)],
                consts=[_row(S["norm_mix_pre"], l + 1)], epilogue=ple_add, outs=[(D, F32), (D, F32), (D, BF16)],
                tm=1024)
        else:
            def ple_loss(tv, xv, ev, tgt):
                err = xv + ev * _sigmoid(tv) - tgt
                return tv, err * (1.0 / D), _colsum(err * err)
            tg, x3, loss_cols = mm_fused(
                t + "ple_gate", m, [(x2, 0)], [(W["w_ple_gate"], 0)], "nn", D, a_to_epilogue=(0,),
                e_ins=[(e, D, 0), (target, D, 0)], epilogue=ple_loss, outs=[(D, F32), (D, F32)], accs=[(1, D)],
                tm=1024)
            h_next = None
        saved.append(dict(x=x, h=h, z=z, o_f=o_f, o_b=o_b, s_f=s_f, s_b=s_b, b_f=b_f, b_b=b_b, a_out=a_out,
                          b_out=b_out, pa=pa, pb=pb,
                          merged=merged, mix=mix, x1=x1, h2=h2, gu=gu, hid=hid, ff=ff, x2=x2, e=e, tg=tg,
                          sg_w=sg_w, sg_bias=sg_bias))
        x, h = x3, h_next

    dx = x

    gs = {n: [None] * DEPTH for n in SMALL}
    dlb_f, dlb_b = [None] * DEPTH, [None] * DEPTH

    for l in reversed(range(DEPTH)):
        t = f"l{l}_bwd_"
        sv, W = saved[l], plan.W[l]
        g_pre, g_post = _row(S["norm_mix_pre"], l), _row(S["norm_mix_post"], l)
        g_fpre, g_fpost = _row(S["norm_ffn_pre"], l), _row(S["norm_ffn_post"], l)
        hg_g = _row(S["hg_norm"], l)
        ln_g, ln_b = _row(S["sg_ln_g"], l), _row(S["sg_ln_b"], l)
        lbf, lbb = _row(lb_f, l), _row(lb_b, l)

        def wgrad(nm, tag, a, b):
            a_dtype = (a[0] if isinstance(a, tuple) else a).dtype
            plan.grads[l][nm] = mm(tag, a, b, "tn", BF16, tk=WGRAD_TOKENS if a_dtype == BF16 else WGRAD_TOKENS // 2)

        def ple_pro(d3, ev, tv):
            s = _sigmoid(tv)
            de_, dt_ = (d3 * s).astype(BF16), (d3 * ev * s * (1.0 - s)).astype(BF16)
            return [dt_], [dt_, de_]

        def ffn_post_bwd(d2p, d3, f, g):
            d2 = d3 + d2p
            fh, r = _rms(f)
            return d2, _rms_bwd(d2, fh, r, g), _colsum(d2 * fh)
        dt, de, dx2, dff, gs["norm_ffn_post"][l] = mm_fused(
            t + "ple_gate_dx", m, [(dx, 0), (sv["e"], 0), (sv["tg"], 0)], [(W["w_ple_gate"], 0)], "nt", D,
            prologue=ple_pro, a_outs=[BF16, BF16], a_to_epilogue=(0,), e_ins=[(sv["ff"], D, 0)], consts=[g_fpost],
            epilogue=ffn_post_bwd, outs=[(D, F32), (D, BF16)], accs=[(1, D)])
        wgrad("w_ple", t + "w_ple", (p, l), de)
        wgrad("w_ple_gate", t + "w_ple_gate", sv["x2"], dt)
        wgrad("w_down", t + "w_down", sv["hid"], dff)
        dhid = mm(t + "ffn_out_dx", dff, W["w_down"], "nt", BF16, tm=BIG_TILE)

        def act_bwd(dh, gt, up):
            dh, gt = dh.astype(F32), gt.astype(F32)
            s = _sigmoid(gt)
            dg_ = (dh * up * (s * (1.0 + gt * (1.0 - s)))).astype(BF16)
            du_ = (dh * (gt * s)).astype(BF16)
            return [dg_, du_], [dg_, du_]

        def pre_post_bwd(dh, d2, x1v, mixv, gf, gp):
            xh, r1 = _rms(x1v)
            d1 = d2 + _rms_bwd(dh, xh, r1, gf)
            mh, rm = _rms(mixv)
            return d1, _rms_bwd(d1, mh, rm, gp), _colsum(dh * xh), _colsum(d1 * mh)
        off = FFN_PAD // 1024
        w_gu_t = W["w_gu"].T
        dgate, dup, dx1, dmix, gs["norm_ffn_pre"][l], gs["norm_mix_post"][l] = mm_fused(
            t + "ffn_in_dx", m, [(dhid, 0), (sv["gu"], 0), (sv["gu"], off)], [(w_gu_t, 0), (w_gu_t, off)], "nn",
            FFN_PAD, prologue=act_bwd, a_outs=[BF16, BF16], e_ins=[(dx2, D, 0), (sv["x1"], D, 0), (sv["mix"], D, 0)],
            consts=[g_fpre, g_post], epilogue=pre_post_bwd, outs=[(D, F32), (D, BF16)], accs=[(1, D), (1, D)],
            resident=True)
        wgrad("w_gate", t + "w_gate", sv["h2"], dgate)
        wgrad("w_up", t + "w_up", sv["h2"], dup)
        wgrad("w_out", t + "w_out", sv["merged"], dmix)

        def merge_bwd(dm, a, b, gab):
            sa, sb = _sigmoid(gab[:, :D]), _sigmoid(gab[:, D:])
            dgab = jnp.concatenate([dm * a * sa * (1.0 - sa), dm * b * sb * (1.0 - sb)], axis=1)
            return dm * sa, dm * sb, dgab
        dpa, dpb, dz = mm_fused(
            t + "out_proj_dx", m, [(dmix, 0)], [(W["w_out"], 0)], "nt", D,
            e_ins=[(sv["pa"], D, 0), (sv["pb"], D, 0), (sv["z"], 2 * D, 3)], epilogue=merge_bwd,
            outs=[(D, BF16), (D, BF16)], alias_outs=[(jax.ShapeDtypeStruct((m, N_IN), BF16), 2 * D, 3)])
        wgrad("w_a", t + "w_a", sv["a_out"], dpa)
        wgrad("w_b", t + "w_b", sv["b_out"], dpb)
        db = mm(t + "proj_b_dx", dpb, W["w_b"], "nt")

        dz, dsw, dbias, gs["sg_ln_g"][l], gs["sg_ln_b"][l] = rowwise(
            t + "sgu", _sg_bwd, m, ins=[(db, SG_WIDTH, 0), (sv["z"], SG_WIDTH, ZU), (sv["z"], SG_WIDTH, ZV)],
            consts=[sv["sg_w"], sv["sg_bias"], ln_g, ln_b], alias_outs=[(dz, 2 * SG_WIDTH, 5)],
            accs=[(SG_WIDTH // SG_GROUP_DIM * SG_CHUNK, SG_CHUNK), (SG_CHUNK, SG_WIDTH), (1, SG_WIDTH), (1, SG_WIDTH)])
        gs["sg_w"][l] = dsw.reshape(1, SG_WIDTH // SG_GROUP_DIM, SG_CHUNK, SG_CHUNK)
        gs["sg_b"][l] = dbias.reshape(SG_CHUNK, SG_WIDTH // SG_GROUP_DIM, SG_GROUP_DIM).sum(-1).T[None]

        d_o, dz, gs["hg_norm"][l] = mm_fused(
            t + "proj_a_dx", m, [(dpa, 0)], [(W["w_a"], 0)], "nt", D,
            e_ins=[(sv["o_f"], D, 0), (sv["o_b"], D, 0), (sv["z"], D, ZG)], consts=[hg_g], epilogue=_hg_post_bwd,
            outs=[(D, BF16)], alias_outs=[(dz, D, ZG)], accs=[(1, D)], tm=256)
        if l == 0:
            part = {n: (g if not isinstance(g, list) else jnp.concatenate(
                [jnp.zeros((1,) + g[1].shape[1:], F32) if gl is None else gl for gl in g], axis=0))
                for n, g in gs.items()}
            plan.early_small(_pack([part[n].reshape(S[n].shape) for n in SMALL]))
        (dq_f, dv_f, dq_b, dv_b, dzf_f, dzf_b, dlb_f[l], dlb_b[l]), extra = hgrn_bwd(
            t + "hgrn", sv["z"], d_o, sv["s_f"], sv["s_b"], sv["b_f"], sv["b_b"], lbf, lbb,
            exch=plan.exch(t + "hgrn"))
        plan.done(t + "hgrn", extra)

        def combine(dqf, dqb, dvf, dvb, dff_, dfb_, zq):
            dq = dqf.astype(F32) + dqb.astype(F32)
            dv = dvf.astype(F32) + dvb.astype(F32)
            return (jnp.concatenate([(dq * _silu_grad(zq)).astype(BF16), dff_, dfb_, dv.astype(BF16)], axis=1),)
        (dz,) = rowwise(t + "hgrn_combine", combine, m,
                        ins=[(dq_f, D, 0), (dq_b, D, 0), (dv_f, D, 0), (dv_b, D, 0), (dzf_f, D, 0), (dzf_b, D, 0),
                             (sv["z"], D, ZQ)], alias_outs=[(dz, 4 * D, 0)])
        wgrad("w_in", t + "w_in", sv["h"], dz)

        def pre_bwd(dhv, d1, xv, g):
            xh, r = _rms(xv)
            return d1 + _rms_bwd(dhv, xh, r, g), _colsum(dhv * xh)
        ex = plan.exch(t + "in_proj_dx")
        res = mm_fused(t + "in_proj_dx", m, [(dz, 0)], [(W["w_in"], 0)], "nt", N_IN,
                       e_ins=[(dx1, D, 0), (sv["x"], D, 0)], consts=[g_pre], epilogue=pre_bwd, outs=[(D, F32)],
                       accs=[(1, D)], tm=1024, exch=ex)
        if ex is not None:
            res, extra = res
            plan.done(t + "in_proj_dx", extra)
        dx, gs["norm_mix_pre"][l] = res
        saved[l] = None
        if l == DEPTH - 1:
            none = jnp.zeros((1, D), F32)
            gs["lb_gamma_fwd"], gs["lb_gamma_bwd"] = lower_bounds_bwd(
                "lower_bounds_bwd", S["lb_gamma_fwd"], S["lb_gamma_bwd"], jnp.concatenate([none, dlb_f[l]], axis=0),
                jnp.concatenate([none, dlb_b[l]], axis=0))

    small ={n: (g if not isinstance(g, list) else jnp.concatenate(g, axis=0)).reshape(S[n].shape)
             for n, g in gs.items()}
    return loss_cols, dx, small


def cast_pad(name, w, rows_p, cols_p):
    _, r, c = w.shape

    def body(w_ref, o_ref):
        if (rows_p, cols_p) != (r, c):
            o_ref[...] = jnp.zeros(o_ref.shape, BF16)
        o_ref[0:r, 0:c] = w_ref[...].astype(BF16)

    return pl.pallas_call(
        body, name=name, grid=(DEPTH,), in_specs=[pl.BlockSpec((None, r, c), lambda l: (l, 0, 0))],
        out_specs=pl.BlockSpec((None, rows_p, cols_p), lambda l: (l, 0, 0)),
        out_shape=jax.ShapeDtypeStruct((DEPTH, rows_p, cols_p), BF16), compiler_params=_params(("parallel",)),
    )(w)


def _shard_shape(n, shape):
    axis, size, _, _ = LAYOUT[n]
    _, r, c = shape
    return (size, c) if axis == 0 else (r, size)


class DistPlan:
    def __init__(self, shards):
        self.shards = shards
        self.W = [dict() for _ in range(DEPTH)]
        self.grads = [dict() for _ in range(DEPTH)]
        self.slots = [dict() for _ in range(DEPTH)]
        rest = [n for n in BIG if n != "w_in"]
        ffn = ["w_gate", "w_up", "w_down"]
        self.schedule = {
            "l0_in_proj": ("gather", [(0, n) for n in rest]),
            "l0_hgrn_fwd": ("gather", [(1, n) for n in BIG if n not in ffn]),
            "l0_ffn_in": ("gather", [(1, n) for n in ffn]),
            "l1_bwd_hgrn": ("scatter", [(1, n) for n in rest]),
            "l1_bwd_in_proj_dx": ("scatter", [(1, "w_in")]),
            "l0_bwd_hgrn": ("scatter", [(0, n) for n in rest]),
            "l0_bwd_in_proj_dx": ("scatter", [(0, "w_in")]),
        }
        self.pending = {}
        self.small_part = self.small_slots = None
        axis, size, dst, _ = LAYOUT["w_in"]
        self.W[0][dst] = gather_two_level("gather_l0_w_in", shards["w_in"], 0, axis, size, GATHERED[dst])

    def _gather(self, host, parts):
        srcs, dsts, items, keys = [], [], [], []
        for layer, n in parts:
            axis, size, dst, base = LAYOUT[n]
            if (layer, dst) not in keys:
                keys.append((layer, dst))
                dsts.append((GATHERED[dst], BF16))
            srcs.append(self.shards[n])
            items.append(("gather", len(srcs) - 1, keys.index((layer, dst)), axis, size, base, layer))
        self.pending[host] = ("gather", keys)
        return Exchange(srcs, dsts, items)

    def _scatter(self, host, parts):
        srcs, dsts, items = [], [], []
        for layer, n in parts:
            axis, size, _, _ = LAYOUT[n]
            srcs.append(self.grads[layer][n])
            dsts.append(((NDEV,) + _shard_shape(n, self.shards[n].shape), BF16))
            items.append(("scatter", len(srcs) - 1, len(dsts) - 1, axis, size, 0, None))
        keys = list(parts)
        if host == "l0_bwd_hgrn" and self.small_part is not None:
            srcs.append(self.small_part)
            dsts.append(((NDEV,) + self.small_part.shape, F32))
            items.append(("copies", len(srcs) - 1, len(dsts) - 1, 0, 0, 0, None))
            keys.append(("small", None))
        self.pending[host] = ("scatter", keys)
        return Exchange(srcs, dsts, items)

    def early_small(self, packed):
        self.small_part = packed

    def exch(self, host):
        if host not in self.schedule:
            return None
        kind, parts = self.schedule[host]
        return self._gather(host, parts) if kind == "gather" else self._scatter(host, parts)

    def done(self, host, outs):
        if host not in self.pending:
            return
        kind, keys = self.pending.pop(host)
        for (layer, n), arr in zip(keys, outs):
            if layer == "small":
                self.small_slots = arr
            else:
                (self.W if kind == "gather" else self.slots)[layer][n] = arr


def adam(name, w, m_, v_, tr, g=None, slots=None):
    L, r, c = w.shape
    assert r % tr == 0
    nt = r // tr
    n_s = 0 if slots is None else L

    def body(*refs):
        s_refs = refs[:n_s]
        g_ref = refs[n_s] if g is not None else None
        w_ref, m_ref, v_ref, g_out, d_out, m_out, v_out = refs[n_s + (g is not None):]

        def update(gv):
            if g_ref is not None:
                gv = gv + g_ref[...] if gv is not None else g_ref[...]
            m2 = B1 * m_ref[...] + (1.0 - B1) * gv
            v2 = B2 * v_ref[...] + (1.0 - B2) * (gv * gv)
            m_hat = m2 / (1.0 - B1 ** STEP)
            v_hat = v2 / (1.0 - B2 ** STEP)
            g_out[...] = gv
            d_out[...] = -LR * (m_hat / (jnp.sqrt(v_hat) + AEPS) + WD * w_ref[...])
            m_out[...] = m2
            v_out[...] = v2

        if slots is None:
            update(None)
            return
        for layer, s_ref in enumerate(s_refs):
            @pl.when(pl.program_id(0) == layer)
            def _():
                gv = s_ref[0][:, :c].astype(F32)
                for j in range(1, NDEV):
                    gv = gv + s_ref[j][:, :c].astype(F32)
                update(gv)

    spec = pl.BlockSpec((None, tr, c), lambda l, i: (l, i, 0))
    arrs, specs = [], []
    if slots is not None:
        assert len(slots) == L and L <= 2
        arrs = list(slots)
        cp = slots[0].shape[2]
        specs = [pl.BlockSpec((NDEV, tr, cp), lambda l, i: (0, i * (1 - l) + (nt - 1) * l, 0)),
                 pl.BlockSpec((NDEV, tr, cp), lambda l, i: (0, i * l, 0))][:L]
    if g is not None:
        arrs.append(g)
        specs.append(spec)
    shp = jax.ShapeDtypeStruct(w.shape, F32)
    return pl.pallas_call(
        body, name=name, grid=(L, nt), in_specs=specs + [spec, spec, spec], out_specs=[spec] * 4,
        out_shape=[shp] * 4, compiler_params=_params(("arbitrary", "arbitrary")),
    )(*arrs, w, m_, v_)


def _pack(arrs):
    parts = []
    for a in arrs:
        a2 = a.reshape(-1, D)
        parts.append(jnp.pad(a2, ((0, -a2.shape[0] % 8), (0, 0))))
    return jnp.concatenate(parts, axis=0)


def _unpack(buf, shapes):
    out, off = [], 0
    for s in shapes:
        rows = 1
        for d_ in s:
            rows *= d_
        rows //= D
        out.append(buf[off:off + rows].reshape(s))
        off += rows + (-rows % 8)
    return out


def kernel(x, p, norm_mix_pre, w_in, lb_gamma_fwd, lb_gamma_bwd, hg_norm, sg_w, sg_b, sg_ln_g, sg_ln_b, w_a, w_b, w_out, norm_mix_post, norm_ffn_pre, w_gate, w_up, w_down, norm_ffn_post, w_ple, w_ple_gate, loss_target, m_norm_mix_pre, m_w_in, m_lb_gamma_fwd, m_lb_gamma_bwd, m_hg_norm, m_sg_w, m_sg_b, m_sg_ln_g, m_sg_ln_b, m_w_a, m_w_b, m_w_out, m_norm_mix_post, m_norm_ffn_pre, m_w_gate, m_w_up, m_w_down, m_norm_ffn_post, m_w_ple, m_w_ple_gate, v_norm_mix_pre, v_w_in, v_lb_gamma_fwd, v_lb_gamma_bwd, v_hg_norm, v_sg_w, v_sg_b, v_sg_ln_g, v_sg_ln_b, v_w_a, v_w_b, v_w_out, v_norm_mix_post, v_norm_ffn_pre, v_w_gate, v_w_up, v_w_down, v_norm_ffn_post, v_w_ple, v_w_ple_gate):
    a = dict(zip(INPUTS, (x, p, norm_mix_pre, w_in, lb_gamma_fwd, lb_gamma_bwd, hg_norm, sg_w, sg_b, sg_ln_g, sg_ln_b, w_a, w_b, w_out, norm_mix_post, norm_ffn_pre, w_gate, w_up, w_down, norm_ffn_post, w_ple, w_ple_gate, loss_target, m_norm_mix_pre, m_w_in, m_lb_gamma_fwd, m_lb_gamma_bwd, m_hg_norm, m_sg_w, m_sg_b, m_sg_ln_g, m_sg_ln_b, m_w_a, m_w_b, m_w_out, m_norm_mix_post, m_norm_ffn_pre, m_w_gate, m_w_up, m_w_down, m_norm_ffn_post, m_w_ple, m_w_ple_gate, v_norm_mix_pre, v_w_in, v_lb_gamma_fwd, v_lb_gamma_bwd, v_hg_norm, v_sg_w, v_sg_b, v_sg_ln_g, v_sg_ln_b, v_w_a, v_w_b, v_w_out, v_norm_mix_post, v_norm_ffn_pre, v_w_gate, v_w_up, v_w_down, v_norm_ffn_post, v_w_ple, v_w_ple_gate)))
    m = x.shape[1]

    shards = {n: cast_pad("cast_" + n, a[n], *_shard_shape(n, a[n].shape)) for n in BIG}
    plan = DistPlan(shards)
    loss_cols, dx, gs = local_step(x[0], p[:, 0], loss_target[0], {n: a[n] for n in SMALL}, plan)
    loss = lax.psum(jnp.sum(loss_cols) * (0.5 / D), ("x", "y", "c"))

    small_shapes = [a[n].shape for n in SMALL]
    rows = plan.small_slots.shape[1]
    late = allreduce_small("allreduce_small", jnp.pad(gs["norm_mix_pre"][0:1], ((0, 7), (0, 0))))
    g_late = jnp.pad(late, ((0, rows - 8), (0, 0)))[None]

    res = {}
    row_tiles = {"w_in": 128, "w_a": 128, "w_b": 512, "w_out": 128, "w_gate": 128, "w_up": 128, "w_down": 88,
                 "w_ple": 256, "w_ple_gate": 128}
    for n in BIG:
        res[n] = adam("adam_" + n, a[n], a["m_" + n], a["v_" + n], row_tiles[n],
                      slots=[plan.slots[l][n] for l in range(DEPTH)])
    packed = [_pack([a[pre + n] for n in SMALL])[None] for pre in ("", "m_", "v_")]
    small_res = adam("adam_small", packed[0], packed[1], packed[2], rows // 2, g=g_late, slots=[plan.small_slots])
    small_res = [_unpack(r_[0], small_shapes) for r_ in small_res]
    for i, n in enumerate(SMALL):
        res[n] = tuple(small_res[k][i] for k in range(4))

    outs = [loss, dx.reshape(1, m, D)]
    for k in range(4):
        outs += [res[n][k] for n in WEIGHTS]
    return tuple(outs)
```

```python
import jax
import jax.numpy as jnp
from jax import lax
from jax.experimental import pallas as pl
from jax.experimental.pallas import tpu as pltpu

F32 = jnp.float32
BF16 = jnp.bfloat16

D = 1024
N_IN = 8192
HEADS = 8
HEAD_DIM = 128
SG_CHUNK = 128
SG_WIDTH = 512
SG_GROUP_DIM = 64
FFN = 2816
PLE_DIM = 256
EPS = 1e-6
DEPTH = 2
ZQ, ZFF, ZFB, ZI, ZG, GA, GB = 0, 1, 2, 3, 4, 6, 7
ZU, ZV = 10, 11

NDEV = 8
FFN_SHARD_PAD = 384
FFN_PAD = NDEV * FFN_SHARD_PAD

LR, B1, B2, AEPS, WD, STEP = 0.001, 0.9, 0.999, 1e-08, 0.01, 10

ROW_TILE = 512
BIG_TILE = 2048
WGRAD_TOKENS = 4096
HG_CHUNK = 64
GROUP_HEADS = 4
HG_BLOCK_FWD = 256
HG_BLOCK_BWD = 256
EXP_CLAMP = 80.0
PROLOGUE_CHUNK = 256
TINY = float(jnp.finfo(jnp.float32).tiny)
VMEM_LIMIT = 56 * 1024 * 1024

BIG = ["w_in", "w_a", "w_b", "w_out", "w_gate", "w_up", "w_down", "w_ple", "w_ple_gate"]
SMALL = ["norm_mix_pre", "lb_gamma_fwd", "lb_gamma_bwd", "hg_norm", "sg_w", "sg_b", "sg_ln_g", "sg_ln_b",
         "norm_mix_post", "norm_ffn_pre", "norm_ffn_post"]
WEIGHTS = ["norm_mix_pre", "w_in", "lb_gamma_fwd", "lb_gamma_bwd", "hg_norm", "sg_w", "sg_b", "sg_ln_g", "sg_ln_b",
           "w_a", "w_b", "w_out", "norm_mix_post", "norm_ffn_pre", "w_gate", "w_up", "w_down", "norm_ffn_post",
           "w_ple", "w_ple_gate"]
INPUTS = (["x", "p"] + WEIGHTS + ["loss_target"] + ["m_" + n for n in WEIGHTS] + ["v_" + n for n in WEIGHTS])
LAYOUT = {
    "w_in": (1, 1024, "w_in", 0), "w_a": (0, 128, "w_a", 0), "w_b": (1, 128, "w_b", 0),
    "w_out": (0, 128, "w_out", 0), "w_gate": (1, FFN_SHARD_PAD, "w_gu", 0),
    "w_up": (1, FFN_SHARD_PAD, "w_gu", FFN_PAD), "w_down": (0, FFN_SHARD_PAD, "w_down", 0),
    "w_ple": (1, 128, "w_ple", 0), "w_ple_gate": (0, 128, "w_ple_gate", 0),
}
GATHERED = {"w_in": (D, N_IN), "w_a": (D, D), "w_b": (SG_WIDTH, D), "w_out": (D, D), "w_gu": (D, 2 * FFN_PAD),
            "w_down": (FFN_PAD, D), "w_ple": (PLE_DIM, D), "w_ple_gate": (D, D)}


def _params(sem):
    return pltpu.CompilerParams(dimension_semantics=sem, vmem_limit_bytes=VMEM_LIMIT)


def _dot(a, b):
    return lax.dot_general(a, b, (((1,), (0,)), ((), ())), preferred_element_type=F32)


def _dot_nt(a, b):
    return lax.dot_general(a, b, (((1,), (1,)), ((), ())), preferred_element_type=F32)


def _dot_tn(a, b):
    return lax.dot_general(a, b, (((0,), (0,)), ((), ())), preferred_element_type=F32)


def _sigmoid(x):
    return jax.nn.sigmoid(x)


def _silu(x):
    return x * _sigmoid(x)


def _silu_grad(x):
    s = _sigmoid(x)
    return s * (1.0 + x * (1.0 - s))


def _gelu(x):
    return 0.5 * x * (1.0 + lax.erf(x * 0.7071067811865476))


def _gelu_grad(x):
    return 0.5 * (1.0 + lax.erf(x * 0.7071067811865476)) + x * jnp.exp(-0.5 * x * x) * 0.3989422804014327


def _mean(x):
    return jnp.mean(x, axis=-1, keepdims=True)


def _colsum(x):
    return jnp.sum(x, axis=0, keepdims=True)


def _rms(x):
    r = lax.rsqrt(_mean(x * x) + EPS)
    return x * r, r


def _rms_bwd(dy, xh, r, g):
    dyg = dy * g
    return r * (dyg - xh * _mean(dyg * xh))


MESH = pl.DeviceIdType.MESH
ANY = pl.BlockSpec(memory_space=pl.ANY)


def _slab(ref, axis, start, size):
    idx = [slice(None)] * 2
    idx[axis] = pl.ds(start, size)
    return ref.at[tuple(idx)]


class Exchange:
    def __init__(self, srcs, dsts, items):
        self.srcs, self.dsts, self.items = list(srcs), list(dsts), list(items)

    def specs(self):
        n = len(self.items)
        sems = [pltpu.SemaphoreType.DMA((n * (NDEV - 1),)), pltpu.SemaphoreType.DMA((n * (NDEV - 1),)),
                pltpu.SemaphoreType.DMA((n,))]
        return ([ANY] * len(self.srcs), [ANY] * len(self.dsts),
                [jax.ShapeDtypeStruct(s, dt) for (s, dt) in self.dsts], sems)

    def copies(self, src, dst, send_sem, recv_sem, loc_sem):
        x, y, c = lax.axis_index("x"), lax.axis_index("y"), lax.axis_index("c")
        me = 4 * x + 2 * y + c
        starts, waits = [], []
        for n, (kind, si, di, axis, size, base, layer) in enumerate(self.items):
            def views(to_dev, from_dev):
                if kind == "gather":
                    return (src[si].at[layer],
                            _slab(dst[di], axis, base + pl.multiple_of(from_dev * size, 128), size))
                if kind == "copies":
                    return src[si], dst[di].at[from_dev]
                return _slab(src[si], axis, base + pl.multiple_of(to_dev * size, 128), size), dst[di].at[from_dev]

            s_own, d_own = views(me, me)
            own = pltpu.make_async_copy(s_own, d_own, loc_sem.at[n])
            starts.append(own)
            waits.append(own)
            for k in range(1, NDEV):
                px = 1 - x if k & 4 else x
                py = 1 - y if k & 2 else y
                pc = 1 - c if k & 1 else c
                peer = 4 * px + 2 * py + pc
                s_out, _ = views(peer, me)
                _, d_in = views(me, peer)
                sem = n * (NDEV - 1) + k - 1
                starts.append(pltpu.make_async_remote_copy(s_out, d_own, send_sem.at[sem], recv_sem.at[sem],
                                                           device_id=(px, py, pc), device_id_type=MESH))
                waits.append(pltpu.make_async_remote_copy(s_out, d_in, send_sem.at[sem], recv_sem.at[sem],
                                                          device_id=(px, py, pc), device_id_type=MESH))
        return starts, waits


def gather_two_level(name, shards, layer, axis, size, full_shape):
    def body(src, dst, send_sem, recv_sem, loc_sem):
        x, y, c = lax.axis_index("x"), lax.axis_index("y"), lax.axis_index("c")
        mine = src.at[layer]
        chips = [(1 - x, y), (x, 1 - y), (1 - x, 1 - y)]

        def slab(px, py, pc):
            return _slab(dst, axis, pl.multiple_of((4 * px + 2 * py + pc) * size, 128), size)

        def copy(k, from_ref, block, to):
            return pltpu.make_async_remote_copy(from_ref, slab(*block), send_sem.at[k], recv_sem.at[k], device_id=to,
                                                device_id_type=MESH)

        own = pltpu.make_async_copy(mine, slab(x, y, c), loc_sem)
        own.start()
        first = [copy(0, mine, (x, y, c), (x, y, 1 - c))]
        first += [copy(1 + j, mine, (x, y, c), (*chip, c)) for j, chip in enumerate(chips)]
        for cp in first:
            cp.start()
        passed = []
        for j, chip in enumerate(chips):
            copy(1 + j, mine, (*chip, c), (x, y, c)).wait_recv()
            fwd = copy(4 + j, slab(*chip, c), (*chip, c), (x, y, 1 - c))
            fwd.start()
            passed.append(fwd)
        copy(0, mine, (x, y, 1 - c), (x, y, c)).wait_recv()
        for j, chip in enumerate(chips):
            copy(4 + j, mine, (*chip, 1 - c), (x, y, c)).wait_recv()
        for cp in first + passed:
            cp.wait_send()
        own.wait()

    return pl.pallas_call(
        body, name=name, in_specs=[ANY], out_specs=ANY, out_shape=jax.ShapeDtypeStruct(full_shape, shards.dtype),
        scratch_shapes=[pltpu.SemaphoreType.DMA((NDEV - 1,)), pltpu.SemaphoreType.DMA((NDEV - 1,)),
                        pltpu.SemaphoreType.DMA(())],
        compiler_params=pltpu.CompilerParams(has_side_effects=True))(shards)


def hosted_call(body, exch, name, grid, in_specs, out_specs, out_shape, scratch_shapes, operands, semantics,
                aliases=None):
    aliases = aliases or {}
    if exch is None:
        res = pl.pallas_call(body, name=name, grid=grid, in_specs=in_specs, out_specs=out_specs, out_shape=out_shape,
                             scratch_shapes=scratch_shapes, input_output_aliases=aliases,
                             compiler_params=_params(semantics))(*operands)
        return list(res), []
    n_in, n_out, n_scr = len(in_specs), len(out_specs), len(scratch_shapes)
    e_in, e_out, e_shape, e_scr = exch.specs()
    ns, nd = len(e_in), len(e_out)

    def at_step(last):
        cond = None
        for ax, n in enumerate(grid):
            c = pl.program_id(ax) == (n - 1 if last else 0)
            cond = c if cond is None else jnp.logical_and(cond, c)
        return cond

    def wrapped(*refs):
        ins, src = refs[:n_in], refs[n_in:n_in + ns]
        o0 = n_in + ns
        outs, dst = refs[o0:o0 + n_out], refs[o0 + n_out:o0 + n_out + nd]
        s0 = o0 + n_out + nd
        scr, sems = refs[s0:s0 + n_scr], refs[s0 + n_scr:]

        @pl.when(at_step(False))
        def _():
            for cp in exch.copies(src, dst, *sems)[0]:
                cp.start()

        body(*ins, *outs, *scr)

        @pl.when(at_step(True))
        def _():
            for cp in exch.copies(src, dst, *sems)[1]:
                cp.wait()

    res = pl.pallas_call(
        wrapped, name=name, grid=grid, in_specs=list(in_specs) + e_in, out_specs=list(out_specs) + e_out,
        out_shape=list(out_shape) + e_shape, scratch_shapes=list(scratch_shapes) + e_scr,
        input_output_aliases=aliases,
        compiler_params=pltpu.CompilerParams(dimension_semantics=("arbitrary",) * len(grid),
                                             vmem_limit_bytes=VMEM_LIMIT, has_side_effects=True),
    )(*operands, *exch.srcs)
    return list(res[:n_out]), list(res[n_out:])


def allreduce_small(name, part):
    rows, width = part.shape

    def body(p_ref, o_ref, buf, send_sem, recv_sem):
        x, y, c = lax.axis_index("x"), lax.axis_index("y"), lax.axis_index("c")
        me = 4 * x + 2 * y + c
        buf[me] = p_ref[...]
        waits = []
        for k in range(1, NDEV):
            px = 1 - x if k & 4 else x
            py = 1 - y if k & 2 else y
            pc = 1 - c if k & 1 else c
            peer = 4 * px + 2 * py + pc
            pltpu.make_async_remote_copy(p_ref, buf.at[me], send_sem.at[k - 1], recv_sem.at[k - 1],
                                         device_id=(px, py, pc), device_id_type=MESH).start()
            waits.append(pltpu.make_async_remote_copy(p_ref, buf.at[peer], send_sem.at[k - 1], recv_sem.at[k - 1],
                                                      device_id=(px, py, pc), device_id_type=MESH))
        for w in waits:
            w.wait()
        acc = buf[0]
        for j in range(1, NDEV):
            acc = acc + buf[j]
        o_ref[...] = acc

    vmem = pl.BlockSpec(memory_space=pltpu.VMEM)
    return pl.pallas_call(
        body, name=name, in_specs=[vmem], out_specs=vmem, out_shape=jax.ShapeDtypeStruct((rows, width), F32),
        scratch_shapes=[pltpu.VMEM((NDEV, rows, width), F32), pltpu.SemaphoreType.DMA((NDEV - 1,)),
                        pltpu.SemaphoreType.DMA((NDEV - 1,))],
        compiler_params=pltpu.CompilerParams(vmem_limit_bytes=VMEM_LIMIT, has_side_effects=True),
    )(part)


def rowwise(name, fn, m, ins=(), consts=(), outs=(), alias_outs=(), accs=(), tm=ROW_TILE):
    tm = min(tm, m)
    n_in, n_c, n_o, n_al, n_ac = len(ins), len(consts), len(outs), len(alias_outs), len(accs)
    held = [a for (a, _, _) in alias_outs if not isinstance(a, jax.ShapeDtypeStruct)]
    n_held = len(held)

    def body(*refs):
        in_refs = refs[:n_in + n_c]
        out_refs = refs[n_in + n_c + n_held:]
        vals = fn(*[r[...] for r in in_refs])
        if not isinstance(vals, (tuple, list)):
            vals = (vals,)
        for r, v in zip(out_refs[:n_o + n_al], vals[:n_o + n_al]):
            r[...] = v.astype(r.dtype)
        if n_ac:
            acc_refs = out_refs[n_o + n_al:]

            @pl.when(pl.program_id(0) == 0)
            def _():
                for r in acc_refs:
                    r[...] = jnp.zeros(r.shape, F32)

            for r, v in zip(acc_refs, vals[n_o + n_al:]):
                r[...] += v

    def col(cb):
        return lambda i: (i, cb)

    in_specs = [pl.BlockSpec((tm, w), col(cb)) for (_, w, cb) in ins]
    in_specs += [pl.BlockSpec(c.shape, lambda i, nd=c.ndim: (0,) * nd) for c in consts]
    in_specs += [ANY for _ in held]
    out_shape = [jax.ShapeDtypeStruct((m, w), dt) for (w, dt) in outs]
    out_specs = [pl.BlockSpec((tm, w), col(0)) for (w, _) in outs]
    out_shape += [jax.ShapeDtypeStruct(a.shape, a.dtype) for (a, _, _) in alias_outs]
    out_specs += [pl.BlockSpec((tm, w), col(cb)) for (_, w, cb) in alias_outs]
    out_shape += [jax.ShapeDtypeStruct(s, F32) for s in accs]
    out_specs += [pl.BlockSpec(s, lambda i: (0, 0)) for s in accs]
    aliases, k_in = {}, n_in + n_c
    for k, (a, _, _) in enumerate(alias_outs):
        if not isinstance(a, jax.ShapeDtypeStruct):
            aliases[k_in] = n_o + k
            k_in += 1
    return pl.pallas_call(
        body, name=name, grid=(m // tm,), in_specs=in_specs, out_specs=out_specs, out_shape=out_shape,
        input_output_aliases=aliases,
        compiler_params=_params(("arbitrary",) if n_ac else ("parallel",)),
    )(*[a for (a, _, _) in ins], *consts, *held)


def _operand(arr, bshape, imap):
    if isinstance(arr, tuple):
        arr, lead = arr
        return arr, pl.BlockSpec((None,) + bshape, lambda *g: (lead,) + imap(*g))
    return arr, pl.BlockSpec(bshape, imap)


def _shape2(arr):
    return arr[0].shape[1:] if isinstance(arr, tuple) else arr.shape


def mm(name, a, b, mode, out_dtype=F32, tm=1024, tn=1024, tk=1024, exch=None):
    sa, sb = _shape2(a), _shape2(b)
    if mode == "nn":
        (M, K), N = sa, sb[1]
    elif mode == "nt":
        (M, K), N = sa, sb[0]
    else:
        (K, M), N = sa, sb[1]
    tm, tn, tk = min(tm, M), min(tn, N), min(tk, K)
    assert M % tm == 0 and N % tn == 0 and K % tk == 0, (name, M, N, K)
    nk = K // tk
    if mode == "nn":
        a_arr, a_spec = _operand(a, (tm, tk), lambda i, j, k: (i, k))
        b_arr, b_spec = _operand(b, (tk, tn), lambda i, j, k: (k, j))
        dot = _dot
    elif mode == "nt":
        a_arr, a_spec = _operand(a, (tm, tk), lambda i, j, k: (i, k))
        b_arr, b_spec = _operand(b, (tn, tk), lambda i, j, k: (j, k))
        dot = _dot_nt
    else:
        a_arr, a_spec = _operand(a, (tk, tm), lambda i, j, k: (k, i))
        b_arr, b_spec = _operand(b, (tk, tn), lambda i, j, k: (k, j))
        dot = _dot_tn

    def body(a_ref, b_ref, o_ref, *acc):
        part = dot(a_ref[...].astype(BF16), b_ref[...].astype(BF16))
        if nk == 1:
            o_ref[...] = part.astype(o_ref.dtype)
            return
        acc_ref, k = acc[0], pl.program_id(2)

        @pl.when(k == 0)
        def _():
            acc_ref[...] = part

        @pl.when(k > 0)
        def _():
            acc_ref[...] += part

        @pl.when(k == nk - 1)
        def _():
            o_ref[...] = acc_ref[...].astype(o_ref.dtype)

    outs, extra = hosted_call(
        body, exch, name, (M // tm, N // tn, nk), [a_spec, b_spec], [pl.BlockSpec((tm, tn), lambda i, j, k: (i, j))],
        [jax.ShapeDtypeStruct((M, N), out_dtype)], [pltpu.VMEM((tm, tn), F32)] if nk > 1 else [], [a_arr, b_arr],
        ("parallel", "parallel", "arbitrary"))
    return outs[0] if exch is None else (outs[0], extra)


def mm_fused(name, m, a_ins, bs, mode, kdim, prologue=None, a_outs=(), e_ins=(), consts=(), epilogue=None, outs=(),
             alias_outs=(), accs=(), a_to_epilogue=(), a_consts=(), tm=512, tk=1024, resident=False, exch=None):
    tm = min(tm, m)
    nk = kdim // tk
    assert nk == 1 or not a_to_epilogue
    n = bs[0][0].shape[1 if mode == "nn" else 0]
    b_arrays = []
    for b_, _ in bs:
        if not (resident and any(b_ is u for u in b_arrays)):
            b_arrays.append(b_)
    b_of_pair = [next(j for j, u in enumerate(b_arrays) if u is b_) if resident else j for j, (b_, _) in enumerate(bs)]
    n_a, n_b, n_e, n_c = len(a_ins) + len(a_consts), len(b_arrays), len(e_ins), len(consts)
    n_ao, n_o, n_al, n_ac = len(a_outs), len(outs), len(alias_outs), len(accs)
    held = [a for (a, _, _) in alias_outs if not isinstance(a, jax.ShapeDtypeStruct)]
    dot = _dot if mode == "nn" else _dot_nt

    def body(*refs):
        a_refs, b_refs = refs[:n_a], refs[n_a:n_a + n_b]
        e_refs = refs[n_a + n_b:n_a + n_b + n_e + n_c]
        o0 = n_a + n_b + n_e + n_c + len(held)
        ao_refs = refs[o0:o0 + n_ao]
        out_refs = refs[o0 + n_ao:o0 + n_ao + n_o + n_al]
        acc_refs = refs[o0 + n_ao + n_o + n_al:o0 + n_ao + n_o + n_al + n_ac]
        scr = refs[o0 + n_ao + n_o + n_al + n_ac:]
        i, k = pl.program_id(0), pl.program_id(1)
        ck = tk if prologue is None else min(tk, PROLOGUE_CHUNK)
        part = None
        for c0 in range(0, tk, ck):
            cols = slice(c0, c0 + ck)
            tiles = [r[:, cols] for r in a_refs]
            a_list, extra = (tiles, []) if prologue is None else prologue(*tiles)
            for r, v in zip(ao_refs, extra):
                r[:, cols] = v.astype(r.dtype)
            for a, j_b, (_, off) in zip(a_list, b_of_pair, bs):
                b_ref = b_refs[j_b]
                if resident:
                    b = b_ref[pl.ds(pl.multiple_of((k + off) * tk + c0, ck), ck), :]
                else:
                    b = b_ref[cols, :] if mode == "nn" else b_ref[:, cols]
                prod = dot(a.astype(BF16), b.astype(BF16))
                part = prod if part is None else part + prod

        def finish(total):
            vals = epilogue(total, *[a_refs[j][...] for j in a_to_epilogue], *[r[...] for r in e_refs])
            if not isinstance(vals, (tuple, list)):
                vals = (vals,)
            for r, v in zip(out_refs, vals[:n_o + n_al]):
                r[...] = v.astype(r.dtype)
            for r, v in zip(acc_refs, vals[n_o + n_al:]):
                @pl.when(i == 0)
                def _():
                    r[...] = v

                @pl.when(i > 0)
                def _():
                    r[...] += v

        if nk == 1:
            finish(part)
            return
        acc_ref = scr[0]

        @pl.when(k == 0)
        def _():
            acc_ref[...] = part

        @pl.when(k > 0)
        def _():
            acc_ref[...] += part

        @pl.when(k == nk - 1)
        def _():
            finish(acc_ref[...])

    in_specs = [pl.BlockSpec((tm, tk), lambda i, k, off=off: (i, k + off)) for (_, off) in a_ins]
    in_specs += [pl.BlockSpec((1, tk), lambda i, k: (0, k)) for _ in a_consts]
    if resident:
        assert mode == "nn"
        in_specs += [pl.BlockSpec(b.shape, lambda i, k: (0, 0), pipeline_mode=pl.Buffered(1)) for b in b_arrays]
    elif mode == "nn":
        in_specs += [pl.BlockSpec((tk, n), lambda i, k, off=off: (k + off, 0)) for (_, off) in bs]
    else:
        in_specs += [pl.BlockSpec((n, tk), lambda i, k, off=off: (0, k + off)) for (_, off) in bs]
    in_specs += [pl.BlockSpec((tm, w), lambda i, k, cb=cb: (i, cb)) for (_, w, cb) in e_ins]
    in_specs += [pl.BlockSpec(c.shape, lambda i, k, nd=c.ndim: (0,) * nd) for c in consts]
    in_specs += [ANY for _ in held]
    out_shape = [jax.ShapeDtypeStruct((m, kdim), dt) for dt in a_outs]
    out_specs = [pl.BlockSpec((tm, tk), lambda i, k: (i, k)) for _ in a_outs]
    out_shape += [jax.ShapeDtypeStruct((m, w), dt) for (w, dt) in outs]
    out_specs += [pl.BlockSpec((tm, w), lambda i, k: (i, 0)) for (w, _) in outs]
    out_shape += [jax.ShapeDtypeStruct(a.shape, a.dtype) for (a, _, _) in alias_outs]
    out_specs += [pl.BlockSpec((tm, w), lambda i, k, cb=cb: (i, cb)) for (_, w, cb) in alias_outs]
    out_shape += [jax.ShapeDtypeStruct(s_, F32) for s_ in accs]
    out_specs += [pl.BlockSpec(s_, lambda i, k: (0, 0)) for s_ in accs]
    aliases, k_in = {}, n_a + n_b + n_e + n_c
    for j, (a, _, _) in enumerate(alias_outs):
        if not isinstance(a, jax.ShapeDtypeStruct):
            aliases[k_in] = n_ao + n_o + j
            k_in += 1
    operands = [a for (a, _) in a_ins] + list(a_consts) + b_arrays + [a for (a, _, _) in e_ins] + list(consts) + held
    res, extra = hosted_call(
        body, exch, name, (m // tm, nk), in_specs, out_specs, out_shape,
        [pltpu.VMEM((tm, n), F32)] if nk > 1 else [], operands,
        ("arbitrary" if n_ac else "parallel", "arbitrary"), aliases)
    return res if exch is None else (res, extra)


def _cumsum_rows(x):
    n = x.shape[0]
    row = lax.broadcasted_iota(jnp.int32, x.shape, 0)
    s = 1
    while s < n:
        x = x + jnp.where(row >= s, pltpu.roll(x, s, 0), 0.0)
        s *= 2
    return x


def _hg_prep(zq, zf, lb, reverse, b=None):
    n = zq.shape[0]
    q = _silu(zq)
    sig = _sigmoid(zf)
    sn = 1.0 - sig
    f = lb + (1.0 - lb) * sig
    k = (1.0 - lb) * sn
    if b is None:
        g = jnp.log(jnp.maximum(f, TINY))
        b = _cumsum_rows(g)
        if reverse:
            b = b[n - 1:n] - b + g
    b_last = b[0:1] if reverse else b[n - 1:n]
    b_ref = b[n // 2:n // 2 + 1]
    e1 = jnp.exp(b)
    e2 = jnp.exp(jnp.clip(b - b_ref, -EXP_CLAMP, EXP_CLAMP))
    e3 = jnp.exp(jnp.clip(b_ref - b, -EXP_CLAMP, EXP_CLAMP))
    e4 = jnp.exp(b_last - b)
    return dict(q=q, k=k, sig=sig, sn=sn, f=f, b=b, e1=e1, e2=e2, e3=e3, e4=e4, e_last=jnp.exp(b_last),
                qe=(q * e1).astype(BF16), qt=(q * e2).astype(BF16), kt=(k * e3).astype(BF16),
                ks=(k * e4).astype(BF16))


def _hg_mask(n, reverse):
    t = lax.broadcasted_iota(jnp.int32, (n, n), 0)
    s = lax.broadcasted_iota(jnp.int32, (n, n), 1)
    return (s >= t) if reverse else (s <= t)


def hgrn_fwd(name, z, lb_f, lb_b, exch=None):
    m = z.shape[0]
    C, T = HG_CHUNK, min(HG_BLOCK_FWD, m)
    nb, cpb = m // T, T // C

    def body(zq_f, zf_f, zi_f, zq_b, zf_b, zi_b, lbf_ref, lbb_ref, of_ref, ob_ref, sf_ref, sb_ref, bf_ref, bb_ref,
             st_ref):
        @pl.when(pl.program_id(0) == 0)
        def _():
            st_ref[...] = jnp.zeros(st_ref.shape, F32)

        dirs = ((zq_f, zf_f, zi_f, lbf_ref, of_ref, sf_ref), (zq_b, zf_b, zi_b, lbb_ref, ob_ref, sb_ref))
        b_refs = (bf_ref, bb_ref)

        def group(ci, g):
            cols = slice(g * GROUP_HEADS * HEAD_DIM, (g + 1) * GROUP_HEADS * HEAD_DIM)
            h0 = g * GROUP_HEADS
            work = []
            for d, (zq, zf, zi, lb_ref, o_ref, s_ref) in enumerate(dirs):
                cc = ci if d == 0 else cpb - 1 - ci
                rows = pl.ds(pl.multiple_of(cc * C, C), C)
                pre = _hg_prep(zq[rows, cols], zf[rows, cols], lb_ref[:, cols], d == 1)
                v = zi[rows, cols].astype(BF16)
                work.append((cc, rows, pre, v, [st_ref[d, h0 + h] for h in range(GROUP_HEADS)]))
            heads = [(d, h, slice(h * HEAD_DIM, (h + 1) * HEAD_DIM)) for d in range(2) for h in range(GROUP_HEADS)]
            first = {}
            for d, h, sl in heads:
                _, _, pre, v, sts = work[d]
                first[d, h] = (_dot_nt(pre["qt"][:, sl], pre["kt"][:, sl]),
                               _dot_nt(pre["qe"][:, sl], sts[h].astype(BF16)),
                               _dot_tn(v[:, sl], pre["ks"][:, sl]))
            results = [([], []), ([], [])]
            for d, h, sl in heads:
                _, _, pre, v, sts = work[d]
                scores, o_inter, st_add = first[d, h]
                a = jnp.where(_hg_mask(C, d == 1), scores, 0.0).astype(BF16)
                results[d][0].append(o_inter + _dot(a, v[:, sl]))
                results[d][1].append(sts[h] * pre["e_last"][:, sl] + st_add)
            results = [(jnp.concatenate(o_parts, axis=1), new_sts) for (o_parts, new_sts) in results]
            for d, (zq, zf, zi, lb_ref, o_ref, s_ref) in enumerate(dirs):
                cc, rows, pre, _, sts = work[d]
                o_ref[rows, cols] = results[d][0]
                b_refs[d][rows, cols] = pre["b"]
                for h in range(GROUP_HEADS):
                    s_ref[cc, h0 + h] = sts[h]
                    st_ref[d, h0 + h] = results[d][1][h]

        def chunk(ci, carry):
            for g in range(HEADS // GROUP_HEADS):
                group(ci, g)
            return carry

        lax.fori_loop(0, cpb, chunk, 0)

    def zspec(cb, rev):
        return pl.BlockSpec((T, D), (lambda i: (nb - 1 - i, cb)) if rev else (lambda i: (i, cb)))

    def sspec(rev):
        shape = (cpb, HEADS, HEAD_DIM, HEAD_DIM)
        return pl.BlockSpec(shape, (lambda i: (nb - 1 - i, 0, 0, 0)) if rev else (lambda i: (i, 0, 0, 0)))

    lbspec = pl.BlockSpec((1, D), lambda i: (0, 0))
    states = jax.ShapeDtypeStruct((m // C, HEADS, HEAD_DIM, HEAD_DIM), F32)
    outs, extra = hosted_call(
        body, exch, name, (nb,),
        [zspec(ZQ, False), zspec(ZFF, False), zspec(ZI, False), zspec(ZQ, True), zspec(ZFB, True), zspec(ZI, True),
         lbspec, lbspec],
        [zspec(0, False), zspec(0, True), sspec(False), sspec(True), zspec(0, False), zspec(0, True)],
        [jax.ShapeDtypeStruct((m, D), F32), jax.ShapeDtypeStruct((m, D), F32), states, states,
         jax.ShapeDtypeStruct((m, D), F32), jax.ShapeDtypeStruct((m, D), F32)],
        [pltpu.VMEM((2, HEADS, HEAD_DIM, HEAD_DIM), F32)], [z, z, z, z, z, z, lb_f, lb_b], ("arbitrary",))
    return outs, extra


def hgrn_bwd(name, z, d_o, s_f, s_b, b_f, b_b, lb_f, lb_b, exch=None):
    m = z.shape[0]
    C, T = HG_CHUNK, min(HG_BLOCK_BWD, m)
    nb, cpb = m // T, T // C

    def body(zq_f, zf_f, zi_f, do_f, sf_ref, zq_b, zf_b, zi_b, do_b, sb_ref, lbf_ref, lbb_ref, bf_ref, bb_ref,
             dqf_ref, dvf_ref, dqb_ref, dvb_ref, dzf_f, dzf_b, dlbf_ref, dlbb_ref,
             dst_ref):
        b_refs = (bf_ref, bb_ref)
        @pl.when(pl.program_id(0) == 0)
        def _():
            dst_ref[...] = jnp.zeros(dst_ref.shape, F32)
            dlbf_ref[...] = jnp.zeros(dlbf_ref.shape, F32)
            dlbb_ref[...] = jnp.zeros(dlbb_ref.shape, F32)

        dirs = ((zq_f, zf_f, zi_f, do_f, sf_ref, lbf_ref, dqf_ref, dvf_ref, dzf_f, dlbf_ref),
                (zq_b, zf_b, zi_b, do_b, sb_ref, lbb_ref, dqb_ref, dvb_ref, dzf_b, dlbb_ref))

        def group(ci, g):
            cols = slice(g * GROUP_HEADS * HEAD_DIM, (g + 1) * GROUP_HEADS * HEAD_DIM)
            h0 = g * GROUP_HEADS
            work = []
            for d, (zq, zf, zi, do_ref, s_ref, lb_ref, dq_ref, dv_ref, dzf_ref, dlb_ref) in enumerate(dirs):
                cc = cpb - 1 - ci if d == 0 else ci
                rows = pl.ds(pl.multiple_of(cc * C, C), C)
                lb = lb_ref[:, cols]
                pre = _hg_prep(zq[rows, cols], zf[rows, cols], lb, d == 1, b=b_refs[d][rows, cols])
                work.append((rows, lb, pre, zi[rows, cols].astype(BF16), do_ref[rows, cols],
                             [s_ref[cc, h0 + h] for h in range(GROUP_HEADS)], [dst_ref[d, h0 + h] for h in range(GROUP_HEADS)],
                             dlb_ref[:, cols]))
            heads = [(d, h, slice(h * HEAD_DIM, (h + 1) * HEAD_DIM)) for d in range(2) for h in range(GROUP_HEADS)]
            first = {}
            for d, h, sl in heads:
                _, _, pre, v, do, st_prevs, dsts, _ = work[d]
                dst16 = dsts[h].astype(BF16)
                first[d, h] = (_dot_nt(pre["qt"][:, sl], pre["kt"][:, sl]),
                               _dot_nt(do[:, sl], v[:, sl]),
                               _dot(do[:, sl], st_prevs[h].astype(BF16)),
                               _dot(v[:, sl], dst16),
                               _dot_nt(pre["ks"][:, sl], dst16),
                               _dot_tn(do[:, sl], pre["qe"][:, sl]))
            parts = [[[] for _ in range(6)] for _ in range(2)]
            for d, h, sl in heads:
                _, _, pre, v, do, st_prevs, dsts, _ = work[d]
                scores, dscores, dq_inter, dk_state, dv_state, dst_add = first[d, h]
                mask = _hg_mask(C, d == 1)
                a = jnp.where(mask, scores, 0.0).astype(BF16)
                da = jnp.where(mask, dscores, 0.0).astype(BF16)
                dq_p, dki_p, dks_p, dv_p, rr_p, new_dsts = parts[d]
                dq_p.append(_dot(da, pre["kt"][:, sl]) * pre["e2"][:, sl] + dq_inter * pre["e1"][:, sl])
                dki_p.append(_dot_tn(da, pre["qt"][:, sl]) * pre["e3"][:, sl])
                dks_p.append(dk_state * pre["e4"][:, sl])
                dv_p.append(_dot_tn(a, do[:, sl]) + dv_state)
                rr_p.append(pre["e_last"][:, sl] * _colsum(dsts[h] * st_prevs[h]))
                new_dsts.append(dsts[h] * pre["e_last"][:, sl] + dst_add)
            results = []
            for d, (rows, lb, pre, v, do, st_prevs, dsts, dlb_old) in enumerate(work):
                rev = d == 1
                dq_p, dki_p, dks_p, dv_p, rr_p, new_dsts = parts[d]
                dq, dki, dks, dv, rr = (jnp.concatenate(p_, axis=1) for p_ in (dq_p, dki_p, dks_p, dv_p, rr_p))
                x = pre["q"] * dq - pre["k"] * dki
                y = pre["k"] * dks
                if rev:
                    dg = _cumsum_rows(x - y) + _colsum(y) + rr
                else:
                    dg = _cumsum_rows(y - x) + (x - y) + _colsum(x) + rr
                inv_f = jnp.where(pre["f"] > TINY, 1.0 / pre["f"], 0.0)
                u = dg * inv_f - (dki + dks)
                results.append((dq, dv, (1.0 - lb) * pre["sig"] * pre["sn"] * u, dlb_old + _colsum(pre["sn"] * u),
                                new_dsts))
            for d, (zq, zf, zi, do_ref, s_ref, lb_ref, dq_ref, dv_ref, dzf_ref, dlb_ref) in enumerate(dirs):
                rows = work[d][0]
                dq, dv, dzf, dlb, new_dsts = results[d]
                dq_ref[rows, cols] = dq.astype(dq_ref.dtype)
                dv_ref[rows, cols] = dv.astype(dv_ref.dtype)
                dzf_ref[rows, cols] = dzf.astype(dzf_ref.dtype)
                dlb_ref[:, cols] = dlb
                for h in range(GROUP_HEADS):
                    dst_ref[d, h0 + h] = new_dsts[h]

        def chunk(ci, carry):
            for g in range(HEADS // GROUP_HEADS):
                group(ci, g)
            return carry

        lax.fori_loop(0, cpb, chunk, 0)

    def rspec(cb, rev):
        return pl.BlockSpec((T, D), (lambda i: (i, cb)) if rev else (lambda i: (nb - 1 - i, cb)))

    def sspec(rev):
        shape = (cpb, HEADS, HEAD_DIM, HEAD_DIM)
        return pl.BlockSpec(shape, (lambda i: (i, 0, 0, 0)) if rev else (lambda i: (nb - 1 - i, 0, 0, 0)))

    lbspec = pl.BlockSpec((1, D), lambda i: (0, 0))
    half = jax.ShapeDtypeStruct((m, D), BF16)
    row = jax.ShapeDtypeStruct((1, D), F32)
    outs, extra = hosted_call(
        body, exch, name, (nb,),
        [rspec(ZQ, False), rspec(ZFF, False), rspec(ZI, False), rspec(0, False), sspec(False),
         rspec(ZQ, True), rspec(ZFB, True), rspec(ZI, True), rspec(0, True), sspec(True), lbspec, lbspec,
         rspec(0, False), rspec(0, True)],
        [rspec(0, False), rspec(0, False), rspec(0, True), rspec(0, True), rspec(0, False), rspec(0, True),
         lbspec, lbspec],
        [half, half, half, half, half, half, row, row],
        [pltpu.VMEM((2, HEADS, HEAD_DIM, HEAD_DIM), F32)],
        [z, z, z, d_o, s_f, z, z, z, d_o, s_b, lb_f, lb_b, b_f, b_b], ("arbitrary",))
    return outs, extra


def _heads(fn, *arrs):
    res = [fn(*[a[:, h * HEAD_DIM:(h + 1) * HEAD_DIM] for a in arrs]) for h in range(arrs[0].shape[1] // HEAD_DIM)]
    return [jnp.concatenate(parts, axis=1) for parts in zip(*res)]


def _hg_post(o_f, o_b, zg, g):
    def head(of, ob, zgh, gh):
        on, _ = _rms(of + ob)
        return (on * gh * _silu(zgh),)
    return _heads(head, o_f, o_b, zg, g)[0]


def _hg_post_bwd(da, o_f, o_b, zg, g):
    def head(dah, of, ob, zgh, gh):
        on, r = _rms(of + ob)
        sg = _silu(zgh)
        d_on = dah * sg
        return _rms_bwd(d_on, on, r, gh), dah * on * gh * _silu_grad(zgh), d_on * on
    d_o, dzg, dg = _heads(head, da, o_f, o_b, zg, g)
    return d_o, dzg, _colsum(dg)


def _sg_parts(zv, ln_g, ln_b):
    vg = _gelu(zv)
    xc = vg - _mean(vg)
    rstd = lax.rsqrt(_mean(xc * xc) + EPS)
    vh = xc * rstd
    return vh, rstd, vh * ln_g + ln_b


def _sg_lane_group(shape):
    return lax.broadcasted_iota(jnp.int32, shape, 1) < SG_GROUP_DIM


def _sg_mix(w, v16, transpose):
    rows = v16.shape[0]
    out = []
    for c in range(rows // SG_CHUNK):
        parts = []
        for j in range(SG_WIDTH // 128):
            vj = v16[c * SG_CHUNK:(c + 1) * SG_CHUNK, j * 128:(j + 1) * 128]
            w0 = w[(2 * j) * SG_CHUNK:(2 * j + 1) * SG_CHUNK]
            w1 = w[(2 * j + 1) * SG_CHUNK:(2 * j + 2) * SG_CHUNK]
            dot = _dot_tn if transpose else _dot
            parts.append(jnp.where(_sg_lane_group((SG_CHUNK, 128)), dot(w0, vj), dot(w1, vj)))
        out.append(jnp.concatenate(parts, axis=1))
    return jnp.concatenate(out, axis=0)


def _sg_fwd(zu, zv, w, bias, ln_g, ln_b):
    _, _, v = _sg_parts(zv, ln_g, ln_b)
    reps = zu.shape[0] // SG_CHUNK
    return _gelu(zu) * (_sg_mix(w, v.astype(BF16), False) + jnp.concatenate([bias] * reps, axis=0))


def _sg_bwd(db, zu, zv, w, bias, ln_g, ln_b):
    vh, rstd, v = _sg_parts(zv, ln_g, ln_b)
    v16 = v.astype(BF16)
    reps = zu.shape[0] // SG_CHUNK
    sg = _sg_mix(w, v16, False) + jnp.concatenate([bias] * reps, axis=0)
    dzu = db * sg * _gelu_grad(zu)
    dsg = db * _gelu(zu)
    dsg16 = dsg.astype(BF16)
    dv = _sg_mix(w, dsg16, True)
    low = _sg_lane_group((SG_CHUNK, 128))
    dw = []
    for g in range(SG_WIDTH // SG_GROUP_DIM):
        j, keep = g // 2, (low if g % 2 == 0 else jnp.logical_not(low))
        acc = jnp.zeros((SG_CHUNK, SG_CHUNK), F32)
        for c in range(reps):
            rows = slice(c * SG_CHUNK, (c + 1) * SG_CHUNK)
            dj = jnp.where(keep, dsg16[rows, j * 128:(j + 1) * 128], jnp.zeros((), BF16))
            acc = acc + _dot_nt(dj, v16[rows, j * 128:(j + 1) * 128])
        dw.append(acc)
    dbias = sum(dsg[c * SG_CHUNK:(c + 1) * SG_CHUNK] for c in range(reps))
    dvh = dv * ln_g
    dvg = rstd * (dvh - _mean(dvh) - vh * _mean(dvh * vh))
    dzuv = jnp.concatenate([dzu, dvg * _gelu_grad(zv)], axis=1)
    return (dzuv, jnp.concatenate(dw, axis=0), dbias, _colsum(dv * vh), _colsum(dv))


def lower_bounds(name, gamma_f, gamma_b):
    def body(gf_ref, gb_ref, lf_ref, lb_ref):
        for g_ref, o_ref in ((gf_ref, lf_ref), (gb_ref, lb_ref)):
            g0, g1 = g_ref[0:1, :], g_ref[1:2, :]
            mx = jnp.maximum(g0, g1)
            e0, e1 = jnp.exp(g0 - mx), jnp.exp(g1 - mx)
            sm0, sm1 = e0 / (e0 + e1), e1 / (e0 + e1)
            o_ref[0:1, :] = sm0 - sm0
            o_ref[1:2, :] = (sm0 + sm1) - sm0
    shp = jax.ShapeDtypeStruct(gamma_f.shape, F32)
    return pl.pallas_call(body, name=name, out_shape=[shp, shp])(gamma_f, gamma_b)


def lower_bounds_bwd(name, gamma_f, gamma_b, dlb_f, dlb_b):
    def body(gf_ref, gb_ref, df_ref, db_ref, of_ref, ob_ref):
        for g_ref, d_ref, o_ref in ((gf_ref, df_ref, of_ref), (gb_ref, db_ref, ob_ref)):
            g0, g1 = g_ref[0:1, :], g_ref[1:2, :]
            mx = jnp.maximum(g0, g1)
            e0, e1 = jnp.exp(g0 - mx), jnp.exp(g1 - mx)
            sm0, sm1 = e0 / (e0 + e1), e1 / (e0 + e1)
            d1 = d_ref[1:2, :] * sm0 * sm1
            o_ref[0:1, :] = -d1
            o_ref[1:2, :] = d1
    shp = jax.ShapeDtypeStruct(gamma_f.shape, F32)
    return pl.pallas_call(body, name=name, out_shape=[shp, shp])(gamma_f, gamma_b, dlb_f, dlb_b)


def _row(a, l):
    return a[l:l + 1]


class LocalPlan:
    def __init__(self, weights):
        self.W = weights
        self.grads = [dict() for _ in range(DEPTH)]

    def exch(self, host):
        return None

    def done(self, host, outs):
        pass

    def early_small(self, packed):
        pass


def local_step(x, p, target, S, plan):
    m = x.shape[0]

    def hmm(tag, *args, **kw):
        ex = plan.exch(tag)
        res = mm(tag, *args, exch=ex, **kw)
        if ex is None:
            return res
        plan.done(tag, res[1])
        return res[0]

    lb_f, lb_b = lower_bounds("lower_bounds", S["lb_gamma_fwd"], S["lb_gamma_bwd"])
    saved = []
    for l in range(DEPTH):
        t = f"l{l}_"
        W = plan.W[l]
        g_pre, g_post = _row(S["norm_mix_pre"], l), _row(S["norm_mix_post"], l)
        g_fpre, g_fpost = _row(S["norm_ffn_pre"], l), _row(S["norm_ffn_post"], l)
        hg_g = _row(S["hg_norm"], l)
        sg_w = S["sg_w"][l].reshape(SG_WIDTH // SG_GROUP_DIM * SG_CHUNK, SG_CHUNK).astype(BF16)
        sg_bias = jnp.repeat(S["sg_b"][l].T, SG_GROUP_DIM, axis=1)
        ln_g, ln_b = _row(S["sg_ln_g"], l), _row(S["sg_ln_b"], l)
        lbf, lbb = _row(lb_f, l), _row(lb_b, l)

        if l == 0:
            (h,) = rowwise(t + "pre_norm", lambda xv, g: (_rms(xv)[0] * g,), m, ins=[(x, D, 0)], consts=[g_pre],
                           outs=[(D, BF16)])
        z = hmm(t + "in_proj", h, W["w_in"], "nn", tm=BIG_TILE // 2, tn=BIG_TILE)
        (o_f, o_b, s_f, s_b, b_f, b_b), extra = hgrn_fwd(t + "hgrn_fwd", z, lbf, lbb, exch=plan.exch(t + "hgrn_fwd"))
        plan.done(t + "hgrn_fwd", extra)
        (b_out,) = rowwise(t + "sgu_fwd", _sg_fwd, m, ins=[(z, SG_WIDTH, ZU), (z, SG_WIDTH, ZV)],
                           consts=[sg_w, sg_bias, ln_g, ln_b], outs=[(SG_WIDTH, BF16)])

        def post_pro(of, ob, zg, g):
            ao = _hg_post(of, ob, zg, g).astype(BF16)
            return [ao], [ao]
        a_out, pa = mm_fused(t + "proj_a", m, [(o_f, 0), (o_b, 0), (z, ZG)], [(W["w_a"], 0)], "nn", D,
                             prologue=post_pro, a_outs=[BF16], a_consts=[hg_g], epilogue=lambda tot: (tot,),
                             outs=[(D, BF16)], tm=1024)
        pb = mm(t + "proj_b", b_out, W["w_b"], "nn", BF16)

        def merge_pro(a, b, ga, gb):
            mg = (_sigmoid(ga) * a + _sigmoid(gb) * b).astype(BF16)
            return [mg], [mg]

        def post_pre(mixv, xv, gp, gf):
            x1 = xv + _rms(mixv)[0] * gp
            return mixv, x1, _rms(x1)[0] * gf
        merged, mix, x1, h2 = mm_fused(
            t + "out_proj", m, [(pa, 0), (pb, 0), (z, GA), (z, GB)], [(W["w_out"], 0)], "nn", D, prologue=merge_pro,
            a_outs=[BF16], e_ins=[(x, D, 0)], consts=[g_post, g_fpre], epilogue=post_pre,
            outs=[(D, F32), (D, F32), (D, BF16)])
        gu = hmm(t + "ffn_in", h2, W["w_gu"], "nn", BF16, tm=BIG_TILE, tn=BIG_TILE)

        def act_pro(gt, up):
            hd = (_silu(gt.astype(F32)) * up).astype(BF16)
            return [hd], [hd]
        hid, ff, x2 = mm_fused(
            t + "ffn_out", m, [(gu, 0), (gu, FFN_PAD // 1024)], [(W["w_down"], 0)], "nn", FFN_PAD, prologue=act_pro,
            a_outs=[BF16], e_ins=[(x1, D, 0)], consts=[g_fpost],
            epilogue=lambda f, xv, g: (f, xv + _rms(f)[0] * g), outs=[(D, F32), (D, F32)], tm=1024, resident=True)
        e = mm(t + "ple_proj", (p, l), W["w_ple"], "nn")

        if l + 1 < DEPTH:
            def ple_add(tv, xv, ev, g):
                x3 = xv + ev * _sigmoid(tv)
                return tv, x3, _rms(x3)[0] * g
            tg, x3, h_next = mm_fused(
                t + "ple_gate", m, [(x2, 0)], [(W["w_ple_gate"], 0)], "nn", D, a_to_epilogue=(0,), e_ins=[(e, D, 0)],
                consts=[_row(S["norm_mix_pre"], l + 1)], epilogue=ple_add, outs=[(D, F32), (D, F32), (D, BF16)],
                tm=1024)
        else:
            def ple_loss(tv, xv, ev, tgt):
                err = xv + ev * _sigmoid(tv) - tgt
                return tv, err * (1.0 / D), _colsum(err * err)
            tg, x3, loss_cols = mm_fused(
                t + "ple_gate", m, [(x2, 0)], [(W["w_ple_gate"], 0)], "nn", D, a_to_epilogue=(0,),
                e_ins=[(e, D, 0), (target, D, 0)], epilogue=ple_loss, outs=[(D, F32), (D, F32)], accs=[(1, D)],
                tm=1024)
            h_next = None
        saved.append(dict(x=x, h=h, z=z, o_f=o_f, o_b=o_b, s_f=s_f, s_b=s_b, b_f=b_f, b_b=b_b, a_out=a_out,
                          b_out=b_out, pa=pa, pb=pb,
                          merged=merged, mix=mix, x1=x1, h2=h2, gu=gu, hid=hid, ff=ff, x2=x2, e=e, tg=tg,
                          sg_w=sg_w, sg_bias=sg_bias))
        x, h = x3, h_next

    dx = x

    gs = {n: [None] * DEPTH for n in SMALL}
    dlb_f, dlb_b = [None] * DEPTH, [None] * DEPTH

    for l in reversed(range(DEPTH)):
        t = f"l{l}_bwd_"
        sv, W = saved[l], plan.W[l]
        g_pre, g_post = _row(S["norm_mix_pre"], l), _row(S["norm_mix_post"], l)
        g_fpre, g_fpost = _row(S["norm_ffn_pre"], l), _row(S["norm_ffn_post"], l)
        hg_g = _row(S["hg_norm"], l)
        ln_g, ln_b = _row(S["sg_ln_g"], l), _row(S["sg_ln_b"], l)
        lbf, lbb = _row(lb_f, l), _row(lb_b, l)

        def wgrad(nm, tag, a, b):
            a_dtype = (a[0] if isinstance(a, tuple) else a).dtype
            plan.grads[l][nm] = mm(tag, a, b, "tn", BF16, tk=WGRAD_TOKENS if a_dtype == BF16 else WGRAD_TOKENS // 2)

        def ple_pro(d3, ev, tv):
            s = _sigmoid(tv)
            de_, dt_ = (d3 * s).astype(BF16), (d3 * ev * s * (1.0 - s)).astype(BF16)
            return [dt_], [dt_, de_]

        def ffn_post_bwd(d2p, d3, f, g):
            d2 = d3 + d2p
            fh, r = _rms(f)
            return d2, _rms_bwd(d2, fh, r, g), _colsum(d2 * fh)
        dt, de, dx2, dff, gs["norm_ffn_post"][l] = mm_fused(
            t + "ple_gate_dx", m, [(dx, 0), (sv["e"], 0), (sv["tg"], 0)], [(W["w_ple_gate"], 0)], "nt", D,
            prologue=ple_pro, a_outs=[BF16, BF16], a_to_epilogue=(0,), e_ins=[(sv["ff"], D, 0)], consts=[g_fpost],
            epilogue=ffn_post_bwd, outs=[(D, F32), (D, BF16)], accs=[(1, D)])
        wgrad("w_ple", t + "w_ple", (p, l), de)
        wgrad("w_ple_gate", t + "w_ple_gate", sv["x2"], dt)
        wgrad("w_down", t + "w_down", sv["hid"], dff)
        dhid = mm(t + "ffn_out_dx", dff, W["w_down"], "nt", BF16, tm=BIG_TILE)

        def act_bwd(dh, gt, up):
            dh, gt = dh.astype(F32), gt.astype(F32)
            s = _sigmoid(gt)
            dg_ = (dh * up * (s * (1.0 + gt * (1.0 - s)))).astype(BF16)
            du_ = (dh * (gt * s)).astype(BF16)
            return [dg_, du_], [dg_, du_]

        def pre_post_bwd(dh, d2, x1v, mixv, gf, gp):
            xh, r1 = _rms(x1v)
            d1 = d2 + _rms_bwd(dh, xh, r1, gf)
            mh, rm = _rms(mixv)
            return d1, _rms_bwd(d1, mh, rm, gp), _colsum(dh * xh), _colsum(d1 * mh)
        off = FFN_PAD // 1024
        w_gu_t = W["w_gu"].T
        dgate, dup, dx1, dmix, gs["norm_ffn_pre"][l], gs["norm_mix_post"][l] = mm_fused(
            t + "ffn_in_dx", m, [(dhid, 0), (sv["gu"], 0), (sv["gu"], off)], [(w_gu_t, 0), (w_gu_t, off)], "nn",
            FFN_PAD, prologue=act_bwd, a_outs=[BF16, BF16], e_ins=[(dx2, D, 0), (sv["x1"], D, 0), (sv["mix"], D, 0)],
            consts=[g_fpre, g_post], epilogue=pre_post_bwd, outs=[(D, F32), (D, BF16)], accs=[(1, D), (1, D)],
            resident=True)
        wgrad("w_gate", t + "w_gate", sv["h2"], dgate)
        wgrad("w_up", t + "w_up", sv["h2"], dup)
        wgrad("w_out", t + "w_out", sv["merged"], dmix)

        def merge_bwd(dm, a, b, gab):
            sa, sb = _sigmoid(gab[:, :D]), _sigmoid(gab[:, D:])
            dgab = jnp.concatenate([dm * a * sa * (1.0 - sa), dm * b * sb * (1.0 - sb)], axis=1)
            return dm * sa, dm * sb, dgab
        dpa, dpb, dz = mm_fused(
            t + "out_proj_dx", m, [(dmix, 0)], [(W["w_out"], 0)], "nt", D,
            e_ins=[(sv["pa"], D, 0), (sv["pb"], D, 0), (sv["z"], 2 * D, 3)], epilogue=merge_bwd,
            outs=[(D, BF16), (D, BF16)], alias_outs=[(jax.ShapeDtypeStruct((m, N_IN), BF16), 2 * D, 3)])
        wgrad("w_a", t + "w_a", sv["a_out"], dpa)
        wgrad("w_b", t + "w_b", sv["b_out"], dpb)
        db = mm(t + "proj_b_dx", dpb, W["w_b"], "nt")

        dz, dsw, dbias, gs["sg_ln_g"][l], gs["sg_ln_b"][l] = rowwise(
            t + "sgu", _sg_bwd, m, ins=[(db, SG_WIDTH, 0), (sv["z"], SG_WIDTH, ZU), (sv["z"], SG_WIDTH, ZV)],
            consts=[sv["sg_w"], sv["sg_bias"], ln_g, ln_b], alias_outs=[(dz, 2 * SG_WIDTH, 5)],
            accs=[(SG_WIDTH // SG_GROUP_DIM * SG_CHUNK, SG_CHUNK), (SG_CHUNK, SG_WIDTH), (1, SG_WIDTH), (1, SG_WIDTH)])
        gs["sg_w"][l] = dsw.reshape(1, SG_WIDTH // SG_GROUP_DIM, SG_CHUNK, SG_CHUNK)
        gs["sg_b"][l] = dbias.reshape(SG_CHUNK, SG_WIDTH // SG_GROUP_DIM, SG_GROUP_DIM).sum(-1).T[None]

        d_o, dz, gs["hg_norm"][l] = mm_fused(
            t + "proj_a_dx", m, [(dpa, 0)], [(W["w_a"], 0)], "nt", D,
            e_ins=[(sv["o_f"], D, 0), (sv["o_b"], D, 0), (sv["z"], D, ZG)], consts=[hg_g], epilogue=_hg_post_bwd,
            outs=[(D, BF16)], alias_outs=[(dz, D, ZG)], accs=[(1, D)], tm=256)
        if l == 0:
            part = {n: (g if not isinstance(g, list) else jnp.concatenate(
                [jnp.zeros((1,) + g[1].shape[1:], F32) if gl is None else gl for gl in g], axis=0))
                for n, g in gs.items()}
            plan.early_small(_pack([part[n].reshape(S[n].shape) for n in SMALL]))
        (dq_f, dv_f, dq_b, dv_b, dzf_f, dzf_b, dlb_f[l], dlb_b[l]), extra = hgrn_bwd(
            t + "hgrn", sv["z"], d_o, sv["s_f"], sv["s_b"], sv["b_f"], sv["b_b"], lbf, lbb,
            exch=plan.exch(t + "hgrn"))
        plan.done(t + "hgrn", extra)

        def combine(dqf, dqb, dvf, dvb, dff_, dfb_, zq):
            dq = dqf.astype(F32) + dqb.astype(F32)
            dv = dvf.astype(F32) + dvb.astype(F32)
            return (jnp.concatenate([(dq * _silu_grad(zq)).astype(BF16), dff_, dfb_, dv.astype(BF16)], axis=1),)
        (dz,) = rowwise(t + "hgrn_combine", combine, m,
                        ins=[(dq_f, D, 0), (dq_b, D, 0), (dv_f, D, 0), (dv_b, D, 0), (dzf_f, D, 0), (dzf_b, D, 0),
                             (sv["z"], D, ZQ)], alias_outs=[(dz, 4 * D, 0)])
        wgrad("w_in", t + "w_in", sv["h"], dz)

        def pre_bwd(dhv, d1, xv, g):
            xh, r = _rms(xv)
            return d1 + _rms_bwd(dhv, xh, r, g), _colsum(dhv * xh)
        ex = plan.exch(t + "in_proj_dx")
        res = mm_fused(t + "in_proj_dx", m, [(dz, 0)], [(W["w_in"], 0)], "nt", N_IN,
                       e_ins=[(dx1, D, 0), (sv["x"], D, 0)], consts=[g_pre], epilogue=pre_bwd, outs=[(D, F32)],
                       accs=[(1, D)], tm=1024, exch=ex)
        if ex is not None:
            res, extra = res
            plan.done(t + "in_proj_dx", extra)
        dx, gs["norm_mix_pre"][l] = res
        saved[l] = None
        if l == DEPTH - 1:
            none = jnp.zeros((1, D), F32)
            gs["lb_gamma_fwd"], gs["lb_gamma_bwd"] = lower_bounds_bwd(
                "lower_bounds_bwd", S["lb_gamma_fwd"], S["lb_gamma_bwd"], jnp.concatenate([none, dlb_f[l]], axis=0),
                jnp.concatenate([none, dlb_b[l]], axis=0))

    small ={n: (g if not isinstance(g, list) else jnp.concatenate(g, axis=0)).reshape(S[n].shape)
             for n, g in gs.items()}
    return loss_cols, dx, small


def cast_pad(name, w, rows_p, cols_p):
    _, r, c = w.shape

    def body(w_ref, o_ref):
        if (rows_p, cols_p) != (r, c):
            o_ref[...] = jnp.zeros(o_ref.shape, BF16)
        o_ref[0:r, 0:c] = w_ref[...].astype(BF16)

    return pl.pallas_call(
        body, name=name, grid=(DEPTH,), in_specs=[pl.BlockSpec((None, r, c), lambda l: (l, 0, 0))],
        out_specs=pl.BlockSpec((None, rows_p, cols_p), lambda l: (l, 0, 0)),
        out_shape=jax.ShapeDtypeStruct((DEPTH, rows_p, cols_p), BF16), compiler_params=_params(("parallel",)),
    )(w)


def _shard_shape(n, shape):
    axis, size, _, _ = LAYOUT[n]
    _, r, c = shape
    return (size, c) if axis == 0 else (r, size)


class DistPlan:
    def __init__(self, shards):
        self.shards = shards
        self.W = [dict() for _ in range(DEPTH)]
        self.grads = [dict() for _ in range(DEPTH)]
        self.slots = [dict() for _ in range(DEPTH)]
        rest = [n for n in BIG if n != "w_in"]
        ffn = ["w_gate", "w_up", "w_down"]
        self.schedule = {
            "l0_in_proj": ("gather", [(0, n) for n in rest]),
            "l0_hgrn_fwd": ("gather", [(1, n) for n in BIG if n not in ffn]),
            "l0_ffn_in": ("gather", [(1, n) for n in ffn]),
            "l1_bwd_hgrn": ("scatter", [(1, n) for n in rest]),
            "l1_bwd_in_proj_dx": ("scatter", [(1, "w_in")]),
            "l0_bwd_hgrn": ("scatter", [(0, n) for n in rest]),
            "l0_bwd_in_proj_dx": ("scatter", [(0, "w_in")]),
        }
        self.pending = {}
        self.small_part = self.small_slots = None
        axis, size, dst, _ = LAYOUT["w_in"]
        self.W[0][dst] = gather_two_level("gather_l0_w_in", shards["w_in"], 0, axis, size, GATHERED[dst])

    def _gather(self, host, parts):
        srcs, dsts, items, keys = [], [], [], []
        for layer, n in parts:
            axis, size, dst, base = LAYOUT[n]
            if (layer, dst) not in keys:
                keys.append((layer, dst))
                dsts.append((GATHERED[dst], BF16))
            srcs.append(self.shards[n])
            items.append(("gather", len(srcs) - 1, keys.index((layer, dst)), axis, size, base, layer))
        self.pending[host] = ("gather", keys)
        return Exchange(srcs, dsts, items)

    def _scatter(self, host, parts):
        srcs, dsts, items = [], [], []
        for layer, n in parts:
            axis, size, _, _ = LAYOUT[n]
            srcs.append(self.grads[layer][n])
            dsts.append(((NDEV,) + _shard_shape(n, self.shards[n].shape), BF16))
            items.append(("scatter", len(srcs) - 1, len(dsts) - 1, axis, size, 0, None))
        keys = list(parts)
        if host == "l0_bwd_hgrn" and self.small_part is not None:
            srcs.append(self.small_part)
            dsts.append(((NDEV,) + self.small_part.shape, F32))
            items.append(("copies", len(srcs) - 1, len(dsts) - 1, 0, 0, 0, None))
            keys.append(("small", None))
        self.pending[host] = ("scatter", keys)
        return Exchange(srcs, dsts, items)

    def early_small(self, packed):
        self.small_part = packed

    def exch(self, host):
        if host not in self.schedule:
            return None
        kind, parts = self.schedule[host]
        return self._gather(host, parts) if kind == "gather" else self._scatter(host, parts)

    def done(self, host, outs):
        if host not in self.pending:
            return
        kind, keys = self.pending.pop(host)
        for (layer, n), arr in zip(keys, outs):
            if layer == "small":
                self.small_slots = arr
            else:
                (self.W if kind == "gather" else self.slots)[layer][n] = arr


def adam(name, w, m_, v_, tr, g=None, slots=None):
    L, r, c = w.shape
    assert r % tr == 0
    nt = r // tr
    n_s = 0 if slots is None else L

    def body(*refs):
        s_refs = refs[:n_s]
        g_ref = refs[n_s] if g is not None else None
        w_ref, m_ref, v_ref, g_out, d_out, m_out, v_out = refs[n_s + (g is not None):]

        def update(gv):
            if g_ref is not None:
                gv = gv + g_ref[...] if gv is not None else g_ref[...]
            m2 = B1 * m_ref[...] + (1.0 - B1) * gv
            v2 = B2 * v_ref[...] + (1.0 - B2) * (gv * gv)
            m_hat = m2 / (1.0 - B1 ** STEP)
            v_hat = v2 / (1.0 - B2 ** STEP)
            g_out[...] = gv
            d_out[...] = -LR * (m_hat / (jnp.sqrt(v_hat) + AEPS) + WD * w_ref[...])
            m_out[...] = m2
            v_out[...] = v2

        if slots is None:
            update(None)
            return
        for layer, s_ref in enumerate(s_refs):
            @pl.when(pl.program_id(0) == layer)
            def _():
                gv = s_ref[0][:, :c].astype(F32)
                for j in range(1, NDEV):
                    gv = gv + s_ref[j][:, :c].astype(F32)
                update(gv)

    spec = pl.BlockSpec((None, tr, c), lambda l, i: (l, i, 0))
    arrs, specs = [], []
    if slots is not None:
        assert len(slots) == L and L <= 2
        arrs = list(slots)
        cp = slots[0].shape[2]
        specs = [pl.BlockSpec((NDEV, tr, cp), lambda l, i: (0, i * (1 - l) + (nt - 1) * l, 0)),
                 pl.BlockSpec((NDEV, tr, cp), lambda l, i: (0, i * l, 0))][:L]
    if g is not None:
        arrs.append(g)
        specs.append(spec)
    shp = jax.ShapeDtypeStruct(w.shape, F32)
    return pl.pallas_call(
        body, name=name, grid=(L, nt), in_specs=specs + [spec, spec, spec], out_specs=[spec] * 4,
        out_shape=[shp] * 4, compiler_params=_params(("arbitrary", "arbitrary")),
    )(*arrs, w, m_, v_)


def _pack(arrs):
    parts = []
    for a in arrs:
        a2 = a.reshape(-1, D)
        parts.append(jnp.pad(a2, ((0, -a2.shape[0] % 8), (0, 0))))
    return jnp.concatenate(parts, axis=0)


def _unpack(buf, shapes):
    out, off = [], 0
    for s in shapes:
        rows = 1
        for d_ in s:
            rows *= d_
        rows //= D
        out.append(buf[off:off + rows].reshape(s))
        off += rows + (-rows % 8)
    return out


def kernel(x, p, norm_mix_pre, w_in, lb_gamma_fwd, lb_gamma_bwd, hg_norm, sg_w, sg_b, sg_ln_g, sg_ln_b, w_a, w_b, w_out, norm_mix_post, norm_ffn_pre, w_gate, w_up, w_down, norm_ffn_post, w_ple, w_ple_gate, loss_target, m_norm_mix_pre, m_w_in, m_lb_gamma_fwd, m_lb_gamma_bwd, m_hg_norm, m_sg_w, m_sg_b, m_sg_ln_g, m_sg_ln_b, m_w_a, m_w_b, m_w_out, m_norm_mix_post, m_norm_ffn_pre, m_w_gate, m_w_up, m_w_down, m_norm_ffn_post, m_w_ple, m_w_ple_gate, v_norm_mix_pre, v_w_in, v_lb_gamma_fwd, v_lb_gamma_bwd, v_hg_norm, v_sg_w, v_sg_b, v_sg_ln_g, v_sg_ln_b, v_w_a, v_w_b, v_w_out, v_norm_mix_post, v_norm_ffn_pre, v_w_gate, v_w_up, v_w_down, v_norm_ffn_post, v_w_ple, v_w_ple_gate):
    a = dict(zip(INPUTS, (x, p, norm_mix_pre, w_in, lb_gamma_fwd, lb_gamma_bwd, hg_norm, sg_w, sg_b, sg_ln_g, sg_ln_b, w_a, w_b, w_out, norm_mix_post, norm_ffn_pre, w_gate, w_up, w_down, norm_ffn_post, w_ple, w_ple_gate, loss_target, m_norm_mix_pre, m_w_in, m_lb_gamma_fwd, m_lb_gamma_bwd, m_hg_norm, m_sg_w, m_sg_b, m_sg_ln_g, m_sg_ln_b, m_w_a, m_w_b, m_w_out, m_norm_mix_post, m_norm_ffn_pre, m_w_gate, m_w_up, m_w_down, m_norm_ffn_post, m_w_ple, m_w_ple_gate, v_norm_mix_pre, v_w_in, v_lb_gamma_fwd, v_lb_gamma_bwd, v_hg_norm, v_sg_w, v_sg_b, v_sg_ln_g, v_sg_ln_b, v_w_a, v_w_b, v_w_out, v_norm_mix_post, v_norm_ffn_pre, v_w_gate, v_w_up, v_w_down, v_norm_ffn_post, v_w_ple, v_w_ple_gate)))
    m = x.shape[1]

    shards = {n: cast_pad("cast_" + n, a[n], *_shard_shape(n, a[n].shape)) for n in BIG}
    plan = DistPlan(shards)
    loss_cols, dx, gs = local_step(x[0], p[:, 0], loss_target[0], {n: a[n] for n in SMALL}, plan)
    loss = lax.psum(jnp.sum(loss_cols) * (0.5 / D), ("x", "y", "c"))

    small_shapes = [a[n].shape for n in SMALL]
    rows = plan.small_slots.shape[1]
    late = allreduce_small("allreduce_small", jnp.pad(gs["norm_mix_pre"][0:1], ((0, 7), (0, 0))))
    g_late = jnp.pad(late, ((0, rows - 8), (0, 0)))[None]

    res = {}
    row_tiles = {"w_in": 128, "w_a": 128, "w_b": 512, "w_out": 128, "w_gate": 128, "w_up": 128, "w_down": 88,
                 "w_ple": 256, "w_ple_gate": 128}
    for n in BIG:
        res[n] = adam("adam_" + n, a[n], a["m_" + n], a["v_" + n], row_tiles[n],
                      slots=[plan.slots[l][n] for l in range(DEPTH)])
    packed = [_pack([a[pre + n] for n in SMALL])[None] for pre in ("", "m_", "v_")]
    small_res = adam("adam_small", packed[0], packed[1], packed[2], rows // 2, g=g_late, slots=[plan.small_slots])
    small_res = [_unpack(r_[0], small_shapes) for r_ in small_res]
    for i, n in enumerate(SMALL):
        res[n] = tuple(small_res[k][i] for k in range(4))

    outs = [loss, dx.reshape(1, m, D)]
    for k in range(4):
        outs += [res[n][k] for n in WEIGHTS]
    return tuple(outs)
```
